```python
import jax
import jax.numpy as jnp
from jax import lax
import numpy as np

D_MODEL = 1024
BATCH = 8
SEQ = 8192
DEPTH = 2
DEC_BATCH = 8
DEC_SEQ = 16
PAST_LEN = 2048

CHUNK = 64
N_EVEN = (DEPTH + 1) // 2
N_ODD = DEPTH // 2
MIX_W = D_MODEL
GROUP_W = MIX_W // 2
D_FF = 2816
NORM_EPS = 1e-6

LRU_W = GROUP_W
LRU_BLOCKS = 8
LRU_BS = LRU_W // LRU_BLOCKS
CONV_W = 4
LRU_C = 8.0

FOX_HEADS = 8
FOX_HD = GROUP_W // FOX_HEADS
FOX_BLOCK = 128

HG_HEADS = 4
HG_DK = GROUP_W // HG_HEADS
HG_DV = HG_DK

RW_HEADS = 8
RW_HD = GROUP_W // RW_HEADS
RW_DECAY_LORA = 64
RW_A_LORA = 64
RW_GATE_LORA = 128
RW_LN_EPS = 64e-5
RW_COLS = 3 * GROUP_W + RW_DECAY_LORA + RW_A_LORA + RW_GATE_LORA

E_COLS = 2 * LRU_W + 4 * GROUP_W + FOX_HEADS
O_COLS = 4 * GROUP_W + RW_COLS

kernel_name = 'hybrid_streaming_encoder_step'


def _rmsnorm(x, g, eps=NORM_EPS):
    xf = x.astype(jnp.float32)
    y = xf * lax.rsqrt(jnp.mean(xf * xf, axis=-1, keepdims=True) + eps)
    return (y * g.astype(jnp.float32)).astype(x.dtype)


def _swiglu(x, w_in, w_out):
    gate, up = jnp.split(x @ w_in, 2, axis=-1)
    return (jax.nn.silu(gate) * up) @ w_out


def _causal_conv(x, buf, w, b):
    T = x.shape[1]
    xp = jnp.concatenate([buf.astype(x.dtype), x], axis=1)
    y = b + sum(xp[:, j:j + T] * w[j] for j in range(CONV_W))
    return y, xp[:, xp.shape[1] - (CONV_W - 1):]


def _block_diag(x, w):
    B, T, C = x.shape
    return jnp.einsum('btnc,ncd->btnd', x.reshape(B, T, LRU_BLOCKS, LRU_BS), w).reshape(B, T, C)


def _rg_lru(x, h0, wa, ba, wx, bx, lam):
    r = jax.nn.sigmoid(_block_diag(x, wa) + ba).astype(jnp.float32)
    i = jax.nn.sigmoid(_block_diag(x, wx) + bx)
    log_a = -LRU_C * jax.nn.softplus(-lam.astype(jnp.float32)) * r
    a = jnp.exp(log_a)
    b = jnp.sqrt(-jnp.expm1(2.0 * log_a)) * (i * x).astype(jnp.float32)
    b = b.at[:, 0].add(a[:, 0] * h0.astype(jnp.float32))

    def combine(left, right):
        a1, b1 = left
        a2, b2 = right
        return a1 * a2, a2 * b1 + b2

    _, h = lax.associative_scan(combine, (a, b), axis=1)
    return h.astype(x.dtype), h[:, -1]


def _fox_block(q, pos_q, F_q, k, v, F_k, pos_k):
    s = jnp.einsum('bqhd,bkhd->bhqk', q, k).astype(jnp.float32) * (FOX_HD ** -0.5)
    s = s + jnp.swapaxes(F_q, 1, 2)[..., :, None] - jnp.swapaxes(F_k, 1, 2)[..., None, :]
    s = jnp.where(pos_k[None, :] <= pos_q[:, None], s, -jnp.inf)
    p = jax.nn.softmax(s, axis=-1).astype(v.dtype)
    return jnp.einsum('bhqk,bkhd->bqhd', p, v)


def _fox_attention(q, k, v, log_f, k_past, v_past, lf_past):
    B, T, H, D = q.shape
    P = k_past.shape[1]
    k_all = jnp.concatenate([k_past.astype(k.dtype), k], axis=1)
    v_all = jnp.concatenate([v_past.astype(v.dtype), v], axis=1)
    F = jnp.cumsum(jnp.concatenate([lf_past.astype(jnp.float32), log_f], axis=1), axis=1)
    pos_k = jnp.arange(P + T)
    pos_q = P + jnp.arange(T)
    F_q = F[:, P:]
    if T > FOX_BLOCK and T % FOX_BLOCK == 0:
        nb = T // FOX_BLOCK
        q_b = q.reshape(B, nb, FOX_BLOCK, H, D).swapaxes(0, 1)
        F_b = F_q.reshape(B, nb, FOX_BLOCK, H).swapaxes(0, 1)
        p_b = pos_q.reshape(nb, FOX_BLOCK)
        o = lax.map(lambda blk: _fox_block(blk[0], blk[1], blk[2], k_all, v_all, F, pos_k), (q_b, p_b, F_b))
        return o.swapaxes(0, 1).reshape(B, T, H, D)
    return _fox_block(q, pos_q, F_q, k_all, v_all, F, pos_k)


def _hgrn2(q, k, v, log_f, S0):
    B, T, H, DK = q.shape
    DV = v.shape[-1]
    c = CHUNK if T % CHUNK == 0 else T
    n = T // c
    causal = jnp.tril(jnp.ones((c, c), dtype=bool))

    def to_chunks(t):
        return t.reshape(B, n, c, H, t.shape[-1]).swapaxes(0, 1)

    def chunk_step(S, inp):
        qc, kc, vc, lfc = inp
        G = jnp.cumsum(lfc, axis=1)
        qg = qc * jnp.exp(G)
        kg = kc * jnp.exp(-G)
        A = jnp.where(causal, jnp.einsum('bthk,bshk->bhts', qg, kg), 0.0)
        o = jnp.einsum('bthk,bhkv->bthv', qg, S) + jnp.einsum('bhts,bshv->bthv', A, vc)
        G_last = G[:, -1]
        k_dec = kc * jnp.exp(G_last[:, None] - G)
        S = jnp.exp(G_last)[..., None] * S + jnp.einsum('bshk,bshv->bhkv', k_dec, vc)
        return S, o

    S_last, o = lax.scan(chunk_step, S0, (to_chunks(q), to_chunks(k), to_chunks(v), to_chunks(log_f)))
    return o.swapaxes(0, 1).reshape(B, T, H, DV), S_last


def _rwkv7(z, prev, S0, o_idx, W):
    B, T, _ = z.shape
    G = GROUP_W
    f32 = jnp.float32
    shifted = jnp.concatenate([prev[:, None].astype(z.dtype), z[:, :-1]], axis=1)
    zm = z + (shifted - z) * W['rw_mu'][o_idx]
    r, k, v, wd, ad, gd = jnp.split(zm, [G, 2 * G, 3 * G, 3 * G + RW_DECAY_LORA, 3 * G + RW_DECAY_LORA + RW_A_LORA], axis=-1)
    w = -jax.nn.softplus(-(W['rw_w0'][o_idx] + jnp.tanh(wd) @ W['rw_w2'][o_idx]).astype(f32)) - 0.5
    decay = jnp.exp(-jnp.exp(w))
    a = jax.nn.sigmoid((W['rw_a0'][o_idx] + ad @ W['rw_a2'][o_idx]).astype(f32))
    g = jax.nn.sigmoid(gd) @ W['rw_g2'][o_idx]

    def heads(t):
        return t.astype(f32).reshape(B, T, RW_HEADS, RW_HD)

    r, k, v, decay, a = heads(r), heads(k), heads(v), heads(decay), heads(a)
    kk = k * W['rw_kk'][o_idx].astype(f32).reshape(RW_HEADS, RW_HD)
    kk = kk / jnp.maximum(jnp.sqrt(jnp.sum(kk * kk, axis=-1, keepdims=True)), 1e-12)
    k = k * (1.0 + (a - 1.0) * W['rw_ka'][o_idx].astype(f32).reshape(RW_HEADS, RW_HD))

    def step(S, inp):
        r_t, w_t, k_t, v_t, kk_t, a_t = inp
        sa = jnp.einsum('bhvk,bhk->bhv', S, kk_t)
        S = (S * w_t[:, :, None, :] - sa[..., None] * (kk_t * a_t)[:, :, None, :]
             + v_t[..., None] * k_t[:, :, None, :])
        return S, jnp.einsum('bhvk,bhk->bhv', S, r_t)

    xs = (r.swapaxes(0, 1), decay.swapaxes(0, 1), k.swapaxes(0, 1), v.swapaxes(0, 1), kk.swapaxes(0, 1), a.swapaxes(0, 1))
    S_last, y = lax.scan(step, S0.astype(f32), xs)
    y = y.swapaxes(0, 1)
    mu = jnp.mean(y, axis=-1, keepdims=True)
    var = jnp.mean(jnp.square(y - mu), axis=-1, keepdims=True)
    y = ((y - mu) * lax.rsqrt(var + RW_LN_EPS) * W['rw_ln_g'][o_idx].astype(f32).reshape(RW_HEADS, RW_HD)
         + W['rw_ln_b'][o_idx].astype(f32).reshape(RW_HEADS, RW_HD))
    y = y + jnp.sum(r * k * W['rw_rk'][o_idx].astype(f32), axis=-1, keepdims=True) * v
    out = y.reshape(B, T, G).astype(z.dtype) * g
    return out, z[:, -1], S_last


def _even_mixer(h, st, e, W):
    conv_buf, lru_h, k_past, v_past, lf_past = st
    B, T, _ = h.shape
    G = GROUP_W
    z = h @ W['e_w_in'][e]
    x_rnn, gate, q, k, v, og, fl = jnp.split(
        z, [LRU_W, 2 * LRU_W, 2 * LRU_W + G, 2 * LRU_W + 2 * G, 2 * LRU_W + 3 * G, 2 * LRU_W + 4 * G], axis=-1)
    xc, conv_new = _causal_conv(x_rnn, conv_buf, W['lru_conv_w'][e], W['lru_conv_b'][e])
    hs, h_last = _rg_lru(xc, lru_h, W['lru_wa'][e], W['lru_ba'][e], W['lru_wx'][e], W['lru_bx'][e], W['lru_lambda'][e])
    rnn_out = jax.nn.gelu(gate) * hs
    q = _rmsnorm(q.reshape(B, T, FOX_HEADS, FOX_HD), W['fox_q_gain'][e])
    k = _rmsnorm(k.reshape(B, T, FOX_HEADS, FOX_HD), W['fox_k_gain'][e])
    v = v.reshape(B, T, FOX_HEADS, FOX_HD)
    log_f = jax.nn.log_sigmoid((fl + W['fox_f_bias'][e]).astype(jnp.float32))
    o = _fox_attention(q, k, v, log_f, k_past, v_past, lf_past)
    fox_out = o.reshape(B, T, G) * jax.nn.sigmoid(og)
    out = jnp.concatenate([rnn_out, fox_out], axis=-1) @ W['e_w_out'][e]
    return out.astype(h.dtype), (conv_new, h_last, k, v, log_f)


def _odd_mixer(h, st, o_idx, lb, W):
    S_hg, shift, S_rw = st
    B, T, _ = h.shape
    G = GROUP_W
    f32 = jnp.float32
    z = h @ W['o_w_in'][o_idx]
    hq, hf, hi, hg, zr = jnp.split(z, [G, 2 * G, 3 * G, 4 * G], axis=-1)
    f = lb + (1.0 - lb) * jax.nn.sigmoid(hf.astype(f32))

    def heads(t):
        return t.astype(f32).reshape(B, T, HG_HEADS, t.shape[-1] // HG_HEADS)

    o, S_hg_new = _hgrn2(heads(hq), heads(1.0 - f), heads(hi), heads(jnp.log(f)), S_hg.astype(f32))
    hg_out = _rmsnorm(o, W['hg_norm_g'][o_idx].reshape(HG_HEADS, HG_DV)).reshape(B, T, G).astype(h.dtype) * jax.nn.silu(hg)
    rw_out, shift_new, S_rw_new = _rwkv7(zr, shift, S_rw, o_idx, W)
    out = jnp.concatenate([hg_out, rw_out], axis=-1) @ W['o_w_out'][o_idx]
    return out.astype(h.dtype), (S_hg_new, shift_new, S_rw_new)


def _trunk(x, states, W):
    lru_conv, lru_h, fox_k, fox_v, fox_lf, hg_S, rw_shift, rw_S = states
    sm = jax.nn.softmax(W['hg_lb_logits'].astype(jnp.float32), axis=0)
    lower_bounds = jnp.cumsum(sm, axis=0) - sm[0]
    even_new, odd_new = [], []
    for layer in range(DEPTH):
        g = W['norm_g'][layer]
        x = x + 0.5 * _swiglu(_rmsnorm(x, g[0]), W['ffn_w_in'][layer, 0], W['ffn_w_out'][layer, 0])
        h = _rmsnorm(x, g[1])
        if layer % 2 == 0:
            e = layer // 2
            m, new = _even_mixer(h, (lru_conv[e], lru_h[e], fox_k[e], fox_v[e], fox_lf[e]), e, W)
            even_new.append(new)
        else:
            o = layer // 2
            m, new = _odd_mixer(h, (hg_S[o], rw_shift[o], rw_S[o]), o, lower_bounds[layer], W)
            odd_new.append(new)
        x = x + m
        x = x + 0.5 * _swiglu(_rmsnorm(x, g[2]), W['ffn_w_in'][layer, 1], W['ffn_w_out'][layer, 1])
    ev = [jnp.stack([n[j] for n in even_new]) for j in range(5)]
    od = [jnp.stack([n[j] for n in odd_new]) for j in range(3)]
    return x, (ev[0], ev[1], ev[2], ev[3], ev[4], od[0], od[1], od[2])


def setup_inputs(seed: int = 0) -> dict:
    key = jax.random.key(seed)
    keys = iter(list(jax.random.split(key, 64)))
    f32 = jnp.float32

    def nrm(shape, scale):
        return scale * jax.random.normal(next(keys), shape, f32)

    def gain(shape):
        return 1.0 + 0.02 * jax.random.normal(next(keys), shape, f32)

    def unif(shape, lo, hi):
        return jax.random.uniform(next(keys), shape, f32, lo, hi)

    lam_u = unif((N_EVEN, LRU_W), 0.9, 0.999)
    lam_s = lam_u ** (1.0 / LRU_C)
    return {
        'x_prompt': nrm((BATCH, SEQ, D_MODEL), 1.0),
        'x_sample': nrm((DEC_BATCH, DEC_SEQ, D_MODEL), 1.0),
        'state_lru_conv': nrm((N_EVEN, DEC_BATCH, CONV_W - 1, LRU_W), 1.0),
        'state_lru_h': nrm((N_EVEN, DEC_BATCH, LRU_W), 0.5),
        'cache_fox_k': nrm((N_EVEN, DEC_BATCH, PAST_LEN, FOX_HEADS, FOX_HD), 1.0),
        'cache_fox_v': nrm((N_EVEN, DEC_BATCH, PAST_LEN, FOX_HEADS, FOX_HD), 1.0),
        'cache_fox_logf': jax.nn.log_sigmoid(2.0 + nrm((N_EVEN, DEC_BATCH, PAST_LEN, FOX_HEADS), 1.0)),
        'state_hgrn_S': nrm((N_ODD, DEC_BATCH, HG_HEADS, HG_DK, HG_DV), 0.3),
        'state_rwkv_shift': nrm((N_ODD, DEC_BATCH, RW_COLS), 1.0),
        'state_rwkv_S': nrm((N_ODD, DEC_BATCH, RW_HEADS, RW_HD, RW_HD), 0.3),
        'norm_g': gain((DEPTH, 3, D_MODEL)),
        'ffn_w_in': nrm((DEPTH, 2, D_MODEL, 2 * D_FF), D_MODEL ** -0.5),
        'ffn_w_out': nrm((DEPTH, 2, D_FF, D_MODEL), D_FF ** -0.5),
        'e_w_in': nrm((N_EVEN, D_MODEL, E_COLS), D_MODEL ** -0.5),
        'e_w_out': nrm((N_EVEN, MIX_W, D_MODEL), MIX_W ** -0.5),
        'lru_conv_w': nrm((N_EVEN, CONV_W, LRU_W), CONV_W ** -0.5),
        'lru_conv_b': nrm((N_EVEN, LRU_W), 0.01),
        'lru_wa': nrm((N_EVEN, LRU_BLOCKS, LRU_BS, LRU_BS), LRU_BS ** -0.5),
        'lru_ba': nrm((N_EVEN, LRU_W), 0.01),
        'lru_wx': nrm((N_EVEN, LRU_BLOCKS, LRU_BS, LRU_BS), LRU_BS ** -0.5),
        'lru_bx': nrm((N_EVEN, LRU_W), 0.01),
        'lru_lambda': jnp.log(lam_s) - jnp.log1p(-lam_s),
        'fox_q_gain': gain((N_EVEN, FOX_HD)),
        'fox_k_gain': gain((N_EVEN, FOX_HD)),
        'fox_f_bias': 2.0 + nrm((N_EVEN, FOX_HEADS), 0.1),
        'o_w_in': nrm((N_ODD, D_MODEL, O_COLS), D_MODEL ** -0.5),
        'o_w_out': nrm((N_ODD, MIX_W, D_MODEL), MIX_W ** -0.5),
        'hg_lb_logits': 1.0 + nrm((DEPTH, GROUP_W), 0.1),
        'hg_norm_g': gain((N_ODD, GROUP_W)),
        'rw_mu': unif((N_ODD, RW_COLS), 0.1, 0.9),
        'rw_w0': unif((N_ODD, GROUP_W), -6.0, -1.0),
        'rw_w2': nrm((N_ODD, RW_DECAY_LORA, GROUP_W), 0.1),
        'rw_a0': nrm((N_ODD, GROUP_W), 0.1),
        'rw_a2': nrm((N_ODD, RW_A_LORA, GROUP_W), 0.1),
        'rw_g2': nrm((N_ODD, RW_GATE_LORA, GROUP_W), RW_GATE_LORA ** -0.5),
        'rw_kk': 0.85 + nrm((N_ODD, GROUP_W), 0.02),
        'rw_ka': 1.0 + nrm((N_ODD, GROUP_W), 0.02),
        'rw_rk': nrm((N_ODD, RW_HEADS, RW_HD), 0.1),
        'rw_ln_g': gain((N_ODD, GROUP_W)),
        'rw_ln_b': nrm((N_ODD, GROUP_W), 0.01),
    }


def reference(x_prompt, x_sample, state_lru_conv, state_lru_h, cache_fox_k, cache_fox_v, cache_fox_logf,
              state_hgrn_S, state_rwkv_shift, state_rwkv_S, norm_g, ffn_w_in, ffn_w_out, e_w_in, e_w_out,
              lru_conv_w, lru_conv_b, lru_wa, lru_ba, lru_wx, lru_bx, lru_lambda, fox_q_gain, fox_k_gain,
              fox_f_bias, o_w_in, o_w_out, hg_lb_logits, hg_norm_g, rw_mu, rw_w0, rw_w2, rw_a0, rw_a2, rw_g2,
              rw_kk, rw_ka, rw_rk, rw_ln_g, rw_ln_b):
    W = dict(norm_g=norm_g, ffn_w_in=ffn_w_in, ffn_w_out=ffn_w_out, e_w_in=e_w_in, e_w_out=e_w_out,
             lru_conv_w=lru_conv_w, lru_conv_b=lru_conv_b, lru_wa=lru_wa, lru_ba=lru_ba, lru_wx=lru_wx,
             lru_bx=lru_bx, lru_lambda=lru_lambda, fox_q_gain=fox_q_gain, fox_k_gain=fox_k_gain,
             fox_f_bias=fox_f_bias, o_w_in=o_w_in, o_w_out=o_w_out, hg_lb_logits=hg_lb_logits,
             hg_norm_g=hg_norm_g, rw_mu=rw_mu, rw_w0=rw_w0, rw_w2=rw_w2, rw_a0=rw_a0, rw_a2=rw_a2,
             rw_g2=rw_g2, rw_kk=rw_kk, rw_ka=rw_ka, rw_rk=rw_rk, rw_ln_g=rw_ln_g, rw_ln_b=rw_ln_b)
    nb = x_prompt.shape[0]
    dt = x_prompt.dtype
    prompt_states = (jnp.zeros((N_EVEN, nb, CONV_W - 1, LRU_W), dt),
                     jnp.zeros((N_EVEN, nb, LRU_W), dt),
                     jnp.zeros((N_EVEN, nb, 0, FOX_HEADS, FOX_HD), dt),
                     jnp.zeros((N_EVEN, nb, 0, FOX_HEADS, FOX_HD), dt),
                     jnp.zeros((N_EVEN, nb, 0, FOX_HEADS), dt),
                     jnp.zeros((N_ODD, nb, HG_HEADS, HG_DK, HG_DV), dt),
                     jnp.zeros((N_ODD, nb, RW_COLS), dt),
                     jnp.zeros((N_ODD, nb, RW_HEADS, RW_HD, RW_HD), dt))
    sample_states = (state_lru_conv, state_lru_h, cache_fox_k, cache_fox_v, cache_fox_logf,
                     state_hgrn_S, state_rwkv_shift, state_rwkv_S)
    y_prompt, p_new = _trunk(x_prompt, prompt_states, W)
    y_sample, s_new = _trunk(x_sample, sample_states, W)
    lru_conv_p, lru_h_p, fox_k_p, fox_v_p, fox_logf_p, hgrn_S_p, rwkv_shift_p, rwkv_S_p = p_new
    lru_conv_s, lru_h_s, fox_k_s, fox_v_s, fox_logf_s, hgrn_S_s, rwkv_shift_s, rwkv_S_s = s_new
    return (y_prompt, y_sample, lru_conv_p, lru_conv_s, lru_h_p, lru_h_s, fox_k_p, fox_k_s, fox_v_p, fox_v_s,
            fox_logf_p, fox_logf_s, hgrn_S_p, hgrn_S_s, rwkv_shift_p, rwkv_shift_s, rwkv_S_p, rwkv_S_s)
```

```python
import functools

import jax
import jax.numpy as jnp
from jax import lax
from jax.experimental import pallas as pl
from jax.experimental.pallas import tpu as pltpu

F32 = jnp.float32
BF16 = jnp.bfloat16

NORM_EPS = 1e-6
GROUP_W = 512
LRU_BLOCKS = 8
CONV_W = 4
LRU_C = 8.0
FOX_HEADS = 8
FOX_HD = 64
FOX_BLOCK = 128
HG_HEADS = 4
CHUNK = 64
RW_HEADS = 8
RW_HD = 64
RW_DECAY_LORA = 64
RW_A_LORA = 64
RW_GATE_LORA = 128
RW_LN_EPS = 64e-5

V7X_LANES = 128
FFN_COL_TILE = 256


def _row_tile(n, want):
    t = min(n, want)
    while n % t:
        t //= 2
    return t


def _params(sem, vmem_mib):
    return pltpu.CompilerParams(dimension_semantics=sem, vmem_limit_bytes=vmem_mib << 20)


def _pad_cols(w):
    pad = -w.shape[-1] % V7X_LANES
    return jnp.pad(w, [(0, 0)] * (w.ndim - 1) + [(0, pad)])


def _rms(x, g):
    return x * lax.rsqrt(jnp.mean(x * x, axis=-1, keepdims=True) + NORM_EPS) * g


def _ffn_body(x_ref, g_ref, wg_ref, wu_ref, wo_ref, o_ref, h_ref, acc_ref):
    j = pl.program_id(1)

    @pl.when(j == 0)
    def _():
        h_ref[...] = _rms(x_ref[...], g_ref[...]).astype(BF16)
        acc_ref[...] = jnp.zeros_like(acc_ref)

    h = h_ref[...]
    gate = jnp.dot(h, wg_ref[...], preferred_element_type=F32)
    up = jnp.dot(h, wu_ref[...], preferred_element_type=F32)
    act = (gate * jax.nn.sigmoid(gate) * up).astype(BF16)
    acc_ref[...] += jnp.dot(act, wo_ref[...], preferred_element_type=F32)

    @pl.when(j == pl.num_programs(1) - 1)
    def _():
        o_ref[...] = x_ref[...] + 0.5 * acc_ref[...]


def _ffn(x, g, w_in, w_out):
    n, d = x.shape
    f = w_out.shape[0]
    tm = _row_tile(n, 512)
    tf = FFN_COL_TILE
    nf = f // tf
    return pl.pallas_call(
        _ffn_body,
        grid=(n // tm, nf),
        in_specs=[
            pl.BlockSpec((tm, d), lambda i, j: (i, 0)),
            pl.BlockSpec((1, d), lambda i, j: (0, 0)),
            pl.BlockSpec((d, tf), lambda i, j: (0, j)),
            pl.BlockSpec((d, tf), lambda i, j: (0, nf + j)),
            pl.BlockSpec((tf, d), lambda i, j: (j, 0)),
        ],
        out_specs=pl.BlockSpec((tm, d), lambda i, j: (i, 0)),
        out_shape=jax.ShapeDtypeStruct((n, d), F32),
        scratch_shapes=[pltpu.VMEM((tm, d), BF16), pltpu.VMEM((tm, d), F32)],
        compiler_params=_params(("parallel", "arbitrary"), 40),
        name="ffn",
    )(x, g.reshape(1, d), w_in, w_in, w_out)


def _norm_matmul_body(x_ref, g_ref, w_ref, o_ref):
    h = _rms(x_ref[...], g_ref[...]).astype(BF16)
    o_ref[...] = jnp.dot(h, w_ref[...], preferred_element_type=F32)


def _norm_matmul(x, g, w):
    n, d = x.shape
    c = w.shape[1]
    tm = _row_tile(n, 256)
    return pl.pallas_call(
        _norm_matmul_body,
        grid=(n // tm,),
        in_specs=[
            pl.BlockSpec((tm, d), lambda i: (i, 0)),
            pl.BlockSpec((1, d), lambda i: (0, 0)),
            pl.BlockSpec((d, c), lambda i: (0, 0)),
        ],
        out_specs=pl.BlockSpec((tm, c), lambda i: (i, 0)),
        out_shape=jax.ShapeDtypeStruct((n, c), F32),
        compiler_params=_params(("parallel",), 48),
        name="norm_matmul",
    )(x, g.reshape(1, d), w)


def _out_proj_body(x_ref, a_ref, b_ref, wa_ref, wb_ref, o_ref):
    acc = jnp.dot(a_ref[...].astype(BF16), wa_ref[...], preferred_element_type=F32)
    acc += jnp.dot(b_ref[...].astype(BF16), wb_ref[...], preferred_element_type=F32)
    o_ref[...] = x_ref[...] + acc


def _out_proj(x, a, b, w):
    n, d = x.shape
    ga, gb = a.shape[1], b.shape[1]
    tm = _row_tile(n, 512)
    return pl.pallas_call(
        _out_proj_body,
        grid=(n // tm,),
        in_specs=[
            pl.BlockSpec((tm, d), lambda i: (i, 0)),
            pl.BlockSpec((tm, ga), lambda i: (i, 0)),
            pl.BlockSpec((tm, gb), lambda i: (i, 0)),
            pl.BlockSpec((ga, d), lambda i: (0, 0)),
            pl.BlockSpec((gb, d), lambda i: (0, 0)),
        ],
        out_specs=pl.BlockSpec((tm, d), lambda i: (i, 0)),
        out_shape=jax.ShapeDtypeStruct((n, d), F32),
        compiler_params=_params(("parallel",), 32),
        name="out_proj",
    )(x, a, b, w[:ga], w[ga:])


def _rmsnorm_j(x, g, eps=NORM_EPS):
    y = x * lax.rsqrt(jnp.mean(x * x, axis=-1, keepdims=True) + eps)
    return y * g


def _causal_conv_j(x, buf, w, b):
    T = x.shape[1]
    xp = jnp.concatenate([buf, x], axis=1)
    y = b + sum(xp[:, j:j + T] * w[j] for j in range(CONV_W))
    return y, xp[:, xp.shape[1] - (CONV_W - 1):]


def _block_diag_j(x, w):
    B, T, C = x.shape
    return jnp.einsum('btnc,ncd->btnd', x.reshape(B, T, LRU_BLOCKS, C // LRU_BLOCKS), w).reshape(B, T, C)


def _rg_lru_j(x, h0, wa, ba, wx, bx, lam):
    r = jax.nn.sigmoid(_block_diag_j(x, wa) + ba)
    i = jax.nn.sigmoid(_block_diag_j(x, wx) + bx)
    log_a = -LRU_C * jax.nn.softplus(-lam) * r
    a = jnp.exp(log_a)
    b = jnp.sqrt(-jnp.expm1(2.0 * log_a)) * (i * x)
    b = b.at[:, 0].add(a[:, 0] * h0)

    def combine(left, right):
        a1, b1 = left
        a2, b2 = right
        return a1 * a2, a2 * b1 + b2

    _, h = lax.associative_scan(combine, (a, b), axis=1)
    return h, h[:, -1]


def _fox_block_j(q, pos_q, F_q, k, v, F_k, pos_k):
    s = jnp.einsum('bqhd,bkhd->bhqk', q, k) * (FOX_HD ** -0.5)
    s = s + jnp.swapaxes(F_q, 1, 2)[..., :, None] - jnp.swapaxes(F_k, 1, 2)[..., None, :]
    s = jnp.where(pos_k[None, :] <= pos_q[:, None], s, -jnp.inf)
    p = jax.nn.softmax(s, axis=-1)
    return jnp.einsum('bhqk,bkhd->bqhd', p, v)


def _fox_attention_j(q, k, v, log_f, k_past, v_past, lf_past):
    B, T, H, D = q.shape
    P = k_past.shape[1]
    k_all = jnp.concatenate([k_past, k], axis=1)
    v_all = jnp.concatenate([v_past, v], axis=1)
    F = jnp.cumsum(jnp.concatenate([lf_past, log_f], axis=1), axis=1)
    pos_k = jnp.arange(P + T)
    pos_q = P + jnp.arange(T)
    F_q = F[:, P:]
    if T > FOX_BLOCK and T % FOX_BLOCK == 0:
        nb = T // FOX_BLOCK
        q_b = q.reshape(B, nb, FOX_BLOCK, H, D).swapaxes(0, 1)
        F_b = F_q.reshape(B, nb, FOX_BLOCK, H).swapaxes(0, 1)
        p_b = pos_q.reshape(nb, FOX_BLOCK)
        o = lax.map(lambda blk: _fox_block_j(blk[0], blk[1], blk[2], k_all, v_all, F, pos_k), (q_b, p_b, F_b))
        return o.swapaxes(0, 1).reshape(B, T, H, D)
    return _fox_block_j(q, pos_q, F_q, k_all, v_all, F, pos_k)


def _hgrn2_j(q, k, v, log_f, S0):
    B, T, H, DK = q.shape
    DV = v.shape[-1]
    c = CHUNK if T % CHUNK == 0 else T
    n = T // c
    causal = jnp.tril(jnp.ones((c, c), dtype=bool))

    def to_chunks(t):
        return t.reshape(B, n, c, H, t.shape[-1]).swapaxes(0, 1)

    def chunk_step(S, inp):
        qc, kc, vc, lfc = inp
        G = jnp.cumsum(lfc, axis=1)
        qg = qc * jnp.exp(G)
        kg = kc * jnp.exp(-G)
        A = jnp.where(causal, jnp.einsum('bthk,bshk->bhts', qg, kg), 0.0)
        o = jnp.einsum('bthk,bhkv->bthv', qg, S) + jnp.einsum('bhts,bshv->bthv', A, vc)
        G_last = G[:, -1]
        k_dec = kc * jnp.exp(G_last[:, None] - G)
        S = jnp.exp(G_last)[..., None] * S + jnp.einsum('bshk,bshv->bhkv', k_dec, vc)
        return S, o

    S_last, o = lax.scan(chunk_step, S0, (to_chunks(q), to_chunks(k), to_chunks(v), to_chunks(log_f)))
    return o.swapaxes(0, 1).reshape(B, T, H, DV), S_last


def _rwkv7_j(z, prev, S0, P):
    B, T, _ = z.shape
    G = GROUP_W
    shifted = jnp.concatenate([prev[:, None], z[:, :-1]], axis=1)
    zm = z + (shifted - z) * P['rw_mu']
    r, k, v, wd, ad, gd = jnp.split(zm, [G, 2 * G, 3 * G, 3 * G + RW_DECAY_LORA, 3 * G + RW_DECAY_LORA + RW_A_LORA], axis=-1)
    w = -jax.nn.softplus(-(P['rw_w0'] + jnp.tanh(wd) @ P['rw_w2'])) - 0.5
    decay = jnp.exp(-jnp.exp(w))
    a = jax.nn.sigmoid(P['rw_a0'] + ad @ P['rw_a2'])
    g = jax.nn.sigmoid(gd) @ P['rw_g2']

    def heads(t):
        return t.reshape(B, T, RW_HEADS, RW_HD)

    r, k, v, decay, a = heads(r), heads(k), heads(v), heads(decay), heads(a)
    kk = k * P['rw_kk'].reshape(RW_HEADS, RW_HD)
    kk = kk / jnp.maximum(jnp.sqrt(jnp.sum(kk * kk, axis=-1, keepdims=True)), 1e-12)
    k = k * (1.0 + (a - 1.0) * P['rw_ka'].reshape(RW_HEADS, RW_HD))

    def step(S, inp):
        r_t, w_t, k_t, v_t, kk_t, a_t = inp
        sa = jnp.einsum('bhvk,bhk->bhv', S, kk_t)
        S = (S * w_t[:, :, None, :] - sa[..., None] * (kk_t * a_t)[:, :, None, :]
             + v_t[..., None] * k_t[:, :, None, :])
        return S, jnp.einsum('bhvk,bhk->bhv', S, r_t)

    xs = tuple(t.swapaxes(0, 1) for t in (r, decay, k, v, kk, a))
    S_last, y = lax.scan(step, S0, xs)
    y = y.swapaxes(0, 1)
    mu = jnp.mean(y, axis=-1, keepdims=True)
    var = jnp.mean(jnp.square(y - mu), axis=-1, keepdims=True)
    y = ((y - mu) * lax.rsqrt(var + RW_LN_EPS) * P['rw_ln_g'].reshape(RW_HEADS, RW_HD)
         + P['rw_ln_b'].reshape(RW_HEADS, RW_HD))
    y = y + jnp.sum(r * k * P['rw_rk'], axis=-1, keepdims=True) * v
    out = y.reshape(B, T, G) * g
    return out, z[:, -1], S_last


def _even_mixer(x2, B, T, g, st, P):
    conv_buf, lru_h, k_past, v_past, lf_past = st
    G = GROUP_W
    z = _norm_matmul(x2, g, P['e_w_in'])
    z = z.reshape(B, T, -1)
    x_rnn, gate, q, k, v, og = (z[..., i * G:(i + 1) * G] for i in range(6))
    fl = z[..., 6 * G:6 * G + FOX_HEADS]
    xc, conv_new = _causal_conv_j(x_rnn, conv_buf, P['lru_conv_w'], P['lru_conv_b'])
    hs, h_last = _rg_lru_j(xc, lru_h, P['lru_wa'], P['lru_ba'], P['lru_wx'], P['lru_bx'], P['lru_lambda'])
    rnn_out = jax.nn.gelu(gate) * hs
    q = _rmsnorm_j(q.reshape(B, T, FOX_HEADS, FOX_HD), P['fox_q_gain'])
    k = _rmsnorm_j(k.reshape(B, T, FOX_HEADS, FOX_HD), P['fox_k_gain'])
    v = v.reshape(B, T, FOX_HEADS, FOX_HD)
    log_f = jax.nn.log_sigmoid(fl + P['fox_f_bias'])
    o = _fox_attention_j(q, k, v, log_f, k_past, v_past, lf_past)
    fox_out = o.reshape(B, T, G) * jax.nn.sigmoid(og)
    x2 = _out_proj(x2, rnn_out.reshape(B * T, G), fox_out.reshape(B * T, G), P['e_w_out'])
    return x2, (conv_new, h_last, k, v, log_f)


def _odd_mixer(x2, B, T, g, st, lb, P):
    S_hg, shift, S_rw = st
    G = GROUP_W
    z = _norm_matmul(x2, g, P['o_w_in']).reshape(B, T, -1)
    hq, hf, hi, hg = (z[..., i * G:(i + 1) * G] for i in range(4))
    zr = z[..., 4 * G:]
    f = lb + (1.0 - lb) * jax.nn.sigmoid(hf)

    def heads(t):
        return t.reshape(B, T, HG_HEADS, G // HG_HEADS)

    o, S_hg_new = _hgrn2_j(heads(hq), heads(1.0 - f), heads(hi), heads(jnp.log(f)), S_hg)
    hg_out = _rmsnorm_j(o, P['hg_norm_g'].reshape(HG_HEADS, G // HG_HEADS)).reshape(B, T, G) * jax.nn.silu(hg)
    rw_out, shift_new, S_rw_new = _rwkv7_j(zr, shift, S_rw, P)
    x2 = _out_proj(x2, hg_out.reshape(B * T, G), rw_out.reshape(B * T, G), P['o_w_out'])
    return x2, (S_hg_new, shift_new, S_rw_new)


def _trunk(x, states, W):
    lru_conv, lru_h, fox_k, fox_v, fox_lf, hg_S, rw_shift, rw_S = states
    B, T, D = x.shape
    depth = W['norm_g'].shape[0]
    sm = jax.nn.softmax(W['hg_lb_logits'], axis=0)
    lower_bounds = jnp.cumsum(sm, axis=0) - sm[0]
    x2 = x.reshape(B * T, D)
    even_new, odd_new = [], []
    for layer in range(depth):
        g = W['norm_g'][layer]
        x2 = _ffn(x2, g[0], W['ffn_w_in'][layer][0], W['ffn_w_out'][layer][0])
        if layer % 2 == 0:
            e = layer // 2
            P = {n: W[n][e] for n in ('e_w_in', 'e_w_out', 'lru_conv_w', 'lru_conv_b', 'lru_wa', 'lru_ba', 'lru_wx',
                                      'lru_bx', 'lru_lambda', 'fox_q_gain', 'fox_k_gain', 'fox_f_bias')}
            x2, new = _even_mixer(x2, B, T, g[1], (lru_conv[e], lru_h[e], fox_k[e], fox_v[e], fox_lf[e]), P)
            even_new.append(new)
        else:
            o = layer // 2
            P = {n: W[n][o] for n in ('o_w_in', 'o_w_out', 'hg_norm_g', 'rw_mu', 'rw_w0', 'rw_w2', 'rw_a0', 'rw_a2',
                                      'rw_g2', 'rw_kk', 'rw_ka', 'rw_rk', 'rw_ln_g', 'rw_ln_b')}
            x2, new = _odd_mixer(x2, B, T, g[1], (hg_S[o], rw_shift[o], rw_S[o]), lower_bounds[layer], P)
            odd_new.append(new)
        x2 = _ffn(x2, g[2], W['ffn_w_in'][layer][1], W['ffn_w_out'][layer][1])
    ev = [jnp.stack([n[j] for n in even_new]) for j in range(5)]
    od = [jnp.stack([n[j] for n in odd_new]) for j in range(3)]
    return x2.reshape(B, T, D), (ev[0], ev[1], ev[2], ev[3], ev[4], od[0], od[1], od[2])


def kernel(x_prompt, x_sample, state_lru_conv, state_lru_h, cache_fox_k, cache_fox_v, cache_fox_logf,
           state_hgrn_S, state_rwkv_shift, state_rwkv_S, norm_g, ffn_w_in, ffn_w_out, e_w_in, e_w_out,
           lru_conv_w, lru_conv_b, lru_wa, lru_ba, lru_wx, lru_bx, lru_lambda, fox_q_gain, fox_k_gain,
           fox_f_bias, o_w_in, o_w_out, hg_lb_logits, hg_norm_g, rw_mu, rw_w0, rw_w2, rw_a0, rw_a2, rw_g2,
           rw_kk, rw_ka, rw_rk, rw_ln_g, rw_ln_b):
    n_even, n_odd = e_w_in.shape[0], o_w_in.shape[0]
    W = dict(norm_g=norm_g, ffn_w_in=ffn_w_in.astype(BF16), ffn_w_out=ffn_w_out.astype(BF16),
             e_w_in=_pad_cols(e_w_in.astype(BF16)), e_w_out=e_w_out.astype(BF16),
             lru_conv_w=lru_conv_w, lru_conv_b=lru_conv_b, lru_wa=lru_wa, lru_ba=lru_ba, lru_wx=lru_wx,
             lru_bx=lru_bx, lru_lambda=lru_lambda, fox_q_gain=fox_q_gain, fox_k_gain=fox_k_gain,
             fox_f_bias=fox_f_bias, o_w_in=o_w_in.astype(BF16), o_w_out=o_w_out.astype(BF16),
             hg_lb_logits=hg_lb_logits, hg_norm_g=hg_norm_g, rw_mu=rw_mu, rw_w0=rw_w0, rw_w2=rw_w2, rw_a0=rw_a0,
             rw_a2=rw_a2, rw_g2=rw_g2, rw_kk=rw_kk, rw_ka=rw_ka, rw_rk=rw_rk, rw_ln_g=rw_ln_g, rw_ln_b=rw_ln_b)
    nb = x_prompt.shape[0]
    dt = x_prompt.dtype
    prompt_states = (jnp.zeros((n_even, nb, CONV_W - 1, GROUP_W), dt),
                     jnp.zeros((n_even, nb, GROUP_W), dt),
                     jnp.zeros((n_even, nb, 0, FOX_HEADS, FOX_HD), dt),
                     jnp.zeros((n_even, nb, 0, FOX_HEADS, FOX_HD), dt),
                     jnp.zeros((n_even, nb, 0, FOX_HEADS), dt),
                     jnp.zeros((n_odd, nb, HG_HEADS, GROUP_W // HG_HEADS, GROUP_W // HG_HEADS), dt),
                     jnp.zeros((n_odd, nb, rw_mu.shape[1]), dt),
                     jnp.zeros((n_odd, nb, RW_HEADS, RW_HD, RW_HD), dt))
    sample_states = (state_lru_conv, state_lru_h, cache_fox_k, cache_fox_v, cache_fox_logf,
                     state_hgrn_S, state_rwkv_shift, state_rwkv_S)
    y_prompt, p_new = _trunk(x_prompt, prompt_states, W)
    y_sample, s_new = _trunk(x_sample, sample_states, W)
    lru_conv_p, lru_h_p, fox_k_p, fox_v_p, fox_logf_p, hgrn_S_p, rwkv_shift_p, rwkv_S_p = p_new
    lru_conv_s, lru_h_s, fox_k_s, fox_v_s, fox_logf_s, hgrn_S_s, rwkv_shift_s, rwkv_S_s = s_new
    return (y_prompt, y_sample, lru_conv_p, lru_conv_s, lru_h_p, lru_h_s, fox_k_p, fox_k_s, fox_v_p, fox_v_s,
            fox_logf_p, fox_logf_s, hgrn_S_p, hgrn_S_s, rwkv_shift_p, rwkv_shift_s, rwkv_S_p, rwkv_S_s)
```

```python
import functools

import jax
import jax.numpy as jnp
from jax import lax
from jax.experimental import pallas as pl
from jax.experimental.pallas import tpu as pltpu

F32 = jnp.float32
BF16 = jnp.bfloat16

NORM_EPS = 1e-6
GROUP_W = 512
LRU_BLOCKS = 8
CONV_W = 4
LRU_C = 8.0
FOX_HEADS = 8
FOX_HD = 64
FOX_BLOCK = 128
HG_HEADS = 4
CHUNK = 64
RW_HEADS = 8
RW_HD = 64
RW_DECAY_LORA = 64
RW_A_LORA = 64
RW_GATE_LORA = 128
RW_LN_EPS = 64e-5

V7X_LANES = 128
FFN_COL_TILE = 256


def _row_tile(n, want):
    t = min(n, want)
    while n % t:
        t //= 2
    return t


def _params(sem, vmem_mib):
    return pltpu.CompilerParams(dimension_semantics=sem, vmem_limit_bytes=vmem_mib << 20)


def _pad_cols(w):
    pad = -w.shape[-1] % V7X_LANES
    return jnp.pad(w, [(0, 0)] * (w.ndim - 1) + [(0, pad)])


def _rms(x, g):
    return x * lax.rsqrt(jnp.mean(x * x, axis=-1, keepdims=True) + NORM_EPS) * g


def _ffn_body(x_ref, g_ref, wg_ref, wu_ref, wo_ref, o_ref, h_ref, acc_ref):
    j = pl.program_id(1)

    @pl.when(j == 0)
    def _():
        h_ref[...] = _rms(x_ref[...], g_ref[...]).astype(BF16)
        acc_ref[...] = jnp.zeros_like(acc_ref)

    h = h_ref[...]
    gate = jnp.dot(h, wg_ref[...], preferred_element_type=F32)
    up = jnp.dot(h, wu_ref[...], preferred_element_type=F32)
    act = (gate * jax.nn.sigmoid(gate) * up).astype(BF16)
    acc_ref[...] += jnp.dot(act, wo_ref[...], preferred_element_type=F32)

    @pl.when(j == pl.num_programs(1) - 1)
    def _():
        o_ref[...] = x_ref[...] + 0.5 * acc_ref[...]


def _ffn(x, g, w_in, w_out):
    n, d = x.shape
    f = w_out.shape[0]
    tm = _row_tile(n, 512)
    tf = FFN_COL_TILE
    nf = f // tf
    return pl.pallas_call(
        _ffn_body,
        grid=(n // tm, nf),
        in_specs=[
            pl.BlockSpec((tm, d), lambda i, j: (i, 0)),
            pl.BlockSpec((1, d), lambda i, j: (0, 0)),
            pl.BlockSpec((d, tf), lambda i, j: (0, j)),
            pl.BlockSpec((d, tf), lambda i, j: (0, nf + j)),
            pl.BlockSpec((tf, d), lambda i, j: (j, 0)),
        ],
        out_specs=pl.BlockSpec((tm, d), lambda i, j: (i, 0)),
        out_shape=jax.ShapeDtypeStruct((n, d), F32),
        scratch_shapes=[pltpu.VMEM((tm, d), BF16), pltpu.VMEM((tm, d), F32)],
        compiler_params=_params(("parallel", "arbitrary"), 40),
        name="ffn",
    )(x, g.reshape(1, d), w_in, w_in, w_out)


def _norm_matmul_body(x_ref, g_ref, w_ref, *o_refs):
    h = _rms(x_ref[...], g_ref[...]).astype(BF16)
    z = jnp.dot(h, w_ref[...], preferred_element_type=F32)
    start = 0
    for o_ref in o_refs:
        width = o_ref.shape[1]
        o_ref[...] = z[:, start:start + width]
        start += width


def _norm_matmul(x, g, w, widths):
    n, d = x.shape
    c = w.shape[1]
    assert sum(widths) == c and all(wd % V7X_LANES == 0 for wd in widths)
    tm = _row_tile(n, 256)
    return pl.pallas_call(
        _norm_matmul_body,
        grid=(n // tm,),
        in_specs=[
            pl.BlockSpec((tm, d), lambda i: (i, 0)),
            pl.BlockSpec((1, d), lambda i: (0, 0)),
            pl.BlockSpec((d, c), lambda i: (0, 0)),
        ],
        out_specs=[pl.BlockSpec((tm, wd), lambda i: (i, 0)) for wd in widths],
        out_shape=[jax.ShapeDtypeStruct((n, wd), F32) for wd in widths],
        compiler_params=_params(("parallel",), 48),
        name="norm_matmul",
    )(x, g.reshape(1, d), w)


def _out_proj_body(x_ref, a_ref, b_ref, wa_ref, wb_ref, o_ref):
    acc = jnp.dot(a_ref[...].astype(BF16), wa_ref[...], preferred_element_type=F32)
    acc += jnp.dot(b_ref[...].astype(BF16), wb_ref[...], preferred_element_type=F32)
    o_ref[...] = x_ref[...] + acc


def _out_proj(x, a, b, w):
    n, d = x.shape
    ga, gb = a.shape[1], b.shape[1]
    tm = _row_tile(n, 512)
    return pl.pallas_call(
        _out_proj_body,
        grid=(n // tm,),
        in_specs=[
            pl.BlockSpec((tm, d), lambda i: (i, 0)),
            pl.BlockSpec((tm, ga), lambda i: (i, 0)),
            pl.BlockSpec((tm, gb), lambda i: (i, 0)),
            pl.BlockSpec((ga, d), lambda i: (0, 0)),
            pl.BlockSpec((gb, d), lambda i: (0, 0)),
        ],
        out_specs=pl.BlockSpec((tm, d), lambda i: (i, 0)),
        out_shape=jax.ShapeDtypeStruct((n, d), F32),
        compiler_params=_params(("parallel",), 32),
        name="out_proj",
    )(x, a, b, w[:ga], w[ga:])


LRU_ROWS = 256
CONV_PAD = 8


def _expm1(x):
    series = x * (1.0 + x * (1 / 2 + x * (1 / 6 + x * (1 / 24 + x * (1 / 120 + x * (1 / 720 + x * (1 / 5040 + x * (1 / 40320))))))))
    return jnp.where(jnp.abs(x) < 0.25, series, jnp.exp(x) - 1.0)


def _shift_rows(x, s, fill):
    row = lax.broadcasted_iota(jnp.int32, x.shape, 0)
    return jnp.where(row >= s, pltpu.roll(x, s, axis=0), fill)


def _lru_body(z_ref, buf_ref, h0_ref, cw_ref, cb_ref, wa_ref, ba_ref, wx_ref, bx_ref, lam_ref,
              o_ref, bufo_ref, ho_ref, x_ref, hc_ref):
    G = GROUP_W
    tt = z_ref.shape[0]

    @pl.when(pl.program_id(1) == 0)
    def _():
        x_ref[0:CONV_PAD, :] = buf_ref[0]
        hc_ref[...] = jnp.broadcast_to(h0_ref[0], hc_ref.shape)

    x_ref[CONV_PAD:CONV_PAD + tt, :] = z_ref[:, 0:G]
    xc = cb_ref[...]
    for j in range(CONV_W):
        lo = CONV_PAD - (CONV_W - 1) + j
        xc = xc + x_ref[lo:lo + tt, :] * cw_ref[j:j + 1, :]
    hist = x_ref[tt:tt + CONV_PAD, :]
    x_ref[0:CONV_PAD, :] = hist
    bufo_ref[0] = hist

    xb = xc.astype(BF16)
    r = jax.nn.sigmoid(jnp.dot(xb, wa_ref[...], preferred_element_type=F32) + ba_ref[...])
    ig = jax.nn.sigmoid(jnp.dot(xb, wx_ref[...], preferred_element_type=F32) + bx_ref[...])
    log_a = (-LRU_C * _softplus(-lam_ref[...])) * r
    a = jnp.exp(log_a)
    b = jnp.sqrt(-_expm1(2.0 * log_a)) * (ig * xc)
    s = 1
    while s < tt:
        b = a * _shift_rows(b, s, 0.0) + b
        a = a * _shift_rows(a, s, 1.0)
        s *= 2
    h = a * hc_ref[0:1, :] + b
    hc_ref[...] = jnp.broadcast_to(h[tt - 1:tt, :], hc_ref.shape)
    ho_ref[0] = h[tt - 1:tt, :]
    o_ref[...] = jax.nn.gelu(z_ref[:, G:2 * G]) * h


def _block_diag_dense(w):
    nb, bs, _ = w.shape
    eye = jnp.eye(nb, dtype=w.dtype)
    return (eye[:, None, :, None] * w[:, :, None, :]).reshape(nb * bs, nb * bs)


def _lru(z_rg, conv_buf, h0, B, T, P):
    G = GROUP_W
    n = B * T
    tt = _row_tile(T, LRU_ROWS)
    nt = T // tt
    buf = jnp.pad(conv_buf, ((0, 0), (CONV_PAD - (CONV_W - 1), 0), (0, 0)))
    cw = jnp.pad(P['lru_conv_w'], ((0, CONV_PAD - CONV_W), (0, 0)))
    row = lambda x: x.reshape(1, G)
    full = lambda shape: pl.BlockSpec(shape, lambda b, i: (0,) * len(shape))
    out, bufo, ho = pl.pallas_call(
        _lru_body,
        grid=(B, nt),
        in_specs=[
            pl.BlockSpec((tt, 2 * G), lambda b, i: (b * nt + i, 0)),
            pl.BlockSpec((1, CONV_PAD, G), lambda b, i: (b, 0, 0)),
            pl.BlockSpec((1, 1, G), lambda b, i: (b, 0, 0)),
            full((CONV_PAD, G)), full((1, G)), full((G, G)), full((1, G)), full((G, G)), full((1, G)), full((1, G)),
        ],
        out_specs=[
            pl.BlockSpec((tt, G), lambda b, i: (b * nt + i, 0)),
            pl.BlockSpec((1, CONV_PAD, G), lambda b, i: (b, 0, 0)),
            pl.BlockSpec((1, 1, G), lambda b, i: (b, 0, 0)),
        ],
        out_shape=[jax.ShapeDtypeStruct((n, G), F32), jax.ShapeDtypeStruct((B, CONV_PAD, G), F32),
                   jax.ShapeDtypeStruct((B, 1, G), F32)],
        scratch_shapes=[pltpu.VMEM((tt + CONV_PAD, G), F32), pltpu.VMEM((8, G), F32)],
        compiler_params=_params(("parallel", "arbitrary"), 32),
        name="lru",
    )(z_rg, buf, h0.reshape(B, 1, G), cw, row(P['lru_conv_b']), _block_diag_dense(P['lru_wa']).astype(BF16),
      row(P['lru_ba']), _block_diag_dense(P['lru_wx']).astype(BF16), row(P['lru_bx']), row(P['lru_lambda']))
    return out, bufo[:, CONV_PAD - (CONV_W - 1):], ho.reshape(B, G)


FOX_Q_ROWS = 512
FOX_K_ROWS = 512
FOX_NEG = -1e30
HEAD_PAIRS = FOX_HEADS // 2


def _fox_prep_body(z_ref, fl_ref, qg_ref, kg_ref, fb_ref, ones_ref, q_ref, k_ref, kb_ref, v_ref, vb_ref, lf_ref):
    G = GROUP_W
    q, k, v = z_ref[:, 0:G], z_ref[:, G:2 * G], z_ref[:, 2 * G:3 * G]
    inv = 1.0 / FOX_HD
    qn = q * lax.rsqrt(_dot_exact_rhs(q * q, ones_ref[...]) * inv + NORM_EPS) * qg_ref[...]
    kn = k * lax.rsqrt(_dot_exact_rhs(k * k, ones_ref[...]) * inv + NORM_EPS) * kg_ref[...]
    q_ref[...] = (qn * (FOX_HD ** -0.5)).astype(BF16)
    k_ref[...] = kn
    kb_ref[...] = kn.astype(BF16)
    v_ref[...] = v
    vb_ref[...] = v.astype(BF16)
    x = fl_ref[...] + fb_ref[...]
    lf_ref[...] = -_softplus(-x)


def _fox_prep(z_qkv, z_fl, P):
    n = z_qkv.shape[0]
    G = GROUP_W
    tt = _row_tile(n, 256)
    ones_bd = jnp.kron(jnp.eye(FOX_HEADS, dtype=F32), jnp.ones((FOX_HD, FOX_HD), F32)).astype(BF16)
    fb = jnp.pad(P['fox_f_bias'], (0, V7X_LANES - FOX_HEADS)).reshape(1, V7X_LANES)
    tile = lambda w: pl.BlockSpec((tt, w), lambda i: (i, 0))
    full = lambda shape: pl.BlockSpec(shape, lambda i: (0,) * len(shape))
    return pl.pallas_call(
        _fox_prep_body,
        grid=(n // tt,),
        in_specs=[tile(3 * G), tile(V7X_LANES), full((1, G)), full((1, G)), full((1, V7X_LANES)), full((G, G))],
        out_specs=[tile(G)] * 5 + [tile(V7X_LANES)],
        out_shape=[jax.ShapeDtypeStruct((n, G), BF16), jax.ShapeDtypeStruct((n, G), F32),
                   jax.ShapeDtypeStruct((n, G), BF16), jax.ShapeDtypeStruct((n, G), F32),
                   jax.ShapeDtypeStruct((n, G), BF16), jax.ShapeDtypeStruct((n, V7X_LANES), F32)],
        compiler_params=_params(("parallel",), 32),
        name="fox_prep",
    )(z_qkv, z_fl, jnp.tile(P['fox_q_gain'], FOX_HEADS).reshape(1, G),
      jnp.tile(P['fox_k_gain'], FOX_HEADS).reshape(1, G), fb, ones_bd)


def _cumsum_body(x_ref, o_ref, c_ref):
    tt = x_ref.shape[1]

    @pl.when(pl.program_id(1) == 0)
    def _():
        c_ref[...] = jnp.zeros_like(c_ref)

    row, col = _tri_masks(tt)
    tri = jnp.where(col <= row, 1.0, 0.0).astype(BF16)
    f = _dot_exact_lhs(tri, x_ref[0]) + c_ref[0:1, :]
    o_ref[0] = f
    c_ref[...] = jnp.broadcast_to(f[tt - 1:tt, :], c_ref.shape)


def _cumsum_time(x):
    B, T, L = x.shape
    tt = _row_tile(T, 256)
    return pl.pallas_call(
        _cumsum_body,
        grid=(B, T // tt),
        in_specs=[pl.BlockSpec((1, tt, L), lambda b, i: (b, i, 0))],
        out_specs=pl.BlockSpec((1, tt, L), lambda b, i: (b, i, 0)),
        out_shape=jax.ShapeDtypeStruct(x.shape, F32),
        scratch_shapes=[pltpu.VMEM((8, L), F32)],
        compiler_params=_params(("parallel", "arbitrary"), 32),
        name="cumsum_time",
    )(x)


def _fox_attn_body(q_ref, k_ref, v_ref, fq_ref, fk_ref, og_ref, o_ref, m_ref, l_ref, acc_ref, *, past, tk):
    qi = pl.program_id(2)
    tq = q_ref.shape[1]
    lane = lax.broadcasted_iota(jnp.int32, (1, V7X_LANES), 1)
    first_q = past + qi * tq
    n_full = (first_q + 1) // tk
    n_all = (first_q + tq - 1) // tk + 1
    q = q_ref[0]
    qh = [jnp.where(lane // FOX_HD == h, q, jnp.zeros_like(q)) for h in range(2)]
    fq = [fq_ref[0, 0, :, h:h + 1] for h in range(2)]
    m_ref[...] = jnp.full(m_ref.shape, FOX_NEG, F32)
    l_ref[...] = jnp.zeros_like(l_ref)
    acc_ref[...] = jnp.zeros_like(acc_ref)

    def block(ki, masked):
        ks = pl.multiple_of(ki * tk, tk)
        k = k_ref[0, pl.ds(ks, tk), :]
        v = v_ref[0, pl.ds(ks, tk), :]
        if masked:
            qpos = first_q + lax.broadcasted_iota(jnp.int32, (tq, tk), 0)
            kpos = ks + lax.broadcasted_iota(jnp.int32, (tq, tk), 1)
            vis = kpos <= qpos
        for h in range(2):
            s = lax.dot_general(qh[h], k, _NT, preferred_element_type=F32)
            s = s + fq[h] - fk_ref[0, 0, h:h + 1, pl.ds(ks, tk)]
            if masked:
                s = jnp.where(vis, s, FOX_NEG)
            m_prev = m_ref[h][:, 0:1]
            m_new = jnp.maximum(m_prev, jnp.max(s, axis=-1, keepdims=True))
            alpha = jnp.exp(m_prev - m_new)
            p = jnp.exp(s - m_new)
            l_ref[h] = alpha * l_ref[h] + jnp.sum(p, axis=-1, keepdims=True)
            acc_ref[h] = alpha * acc_ref[h] + jnp.dot(p.astype(BF16), v, preferred_element_type=F32)
            m_ref[h] = jnp.broadcast_to(m_new, m_ref.shape[1:])

    def full_body(ki, c):
        block(ki, False)
        return c

    def diag_body(ki, c):
        block(ki, True)
        return c

    lax.fori_loop(0, n_full, full_body, 0)
    lax.fori_loop(n_full, n_all, diag_body, 0)
    o = jnp.where(lane < FOX_HD, acc_ref[0] / l_ref[0][:, 0:1], acc_ref[1] / l_ref[1][:, 0:1])
    o_ref[...] = o * jax.nn.sigmoid(og_ref[...])


def _fox_attention(qb, kb_all, vb_all, f_all, z_og, B, T, past):
    G = GROUP_W
    tq = _row_tile(T, FOX_Q_ROWS)
    nq = T // tq
    tk_real = kb_all.shape[1]
    tk = min(FOX_K_ROWS, -(-tk_real // V7X_LANES) * V7X_LANES)
    tkp = -(-tk_real // tk) * tk
    pad = ((0, 0), (0, tkp - tk_real), (0, 0))
    kb_all, vb_all = jnp.pad(kb_all, pad), jnp.pad(vb_all, pad)
    fh = f_all[:, :, :FOX_HEADS].reshape(B, tk_real, HEAD_PAIRS, 2)
    fq = jnp.pad(fh[:, past:].transpose(0, 2, 1, 3), ((0, 0), (0, 0), (0, 0), (0, V7X_LANES - 2)))
    fk = jnp.pad(fh.transpose(0, 2, 3, 1), ((0, 0), (0, 0), (0, 6), (0, tkp - tk_real)))
    return pl.pallas_call(
        functools.partial(_fox_attn_body, past=past, tk=tk),
        grid=(B, HEAD_PAIRS, nq),
        in_specs=[
            pl.BlockSpec((1, tq, V7X_LANES), lambda b, p, i: (b, i, p)),
            pl.BlockSpec((1, tkp, V7X_LANES), lambda b, p, i: (b, 0, p)),
            pl.BlockSpec((1, tkp, V7X_LANES), lambda b, p, i: (b, 0, p)),
            pl.BlockSpec((1, 1, tq, V7X_LANES), lambda b, p, i: (b, p, i, 0)),
            pl.BlockSpec((1, 1, 8, tkp), lambda b, p, i: (b, p, 0, 0)),
            pl.BlockSpec((tq, V7X_LANES), lambda b, p, i: (b * nq + i, p)),
        ],
        out_specs=pl.BlockSpec((tq, V7X_LANES), lambda b, p, i: (b * nq + i, p)),
        out_shape=jax.ShapeDtypeStruct((B * T, G), F32),
        scratch_shapes=[pltpu.VMEM((2, tq, V7X_LANES), F32), pltpu.VMEM((2, tq, V7X_LANES), F32),
                        pltpu.VMEM((2, tq, V7X_LANES), F32)],
        compiler_params=_params(("parallel", "parallel", "arbitrary"), 48),
        name="fox_attn",
    )(qb.reshape(B, T, G), kb_all, vb_all, fq, fk, z_og)


HG_CHUNK = 64


def _hgrn_body(z_ref, lb_ref, s0_ref, ng_ref, o_ref, so_ref, st_ref):
    G = GROUP_W
    c = z_ref.shape[0]
    dk = G // HG_HEADS

    @pl.when(pl.program_id(1) == 0)
    def _():
        st_ref[...] = s0_ref[0]

    lb = lb_ref[...]
    f = lb + (1.0 - lb) * jax.nn.sigmoid(z_ref[:, G:2 * G])
    kx = 1.0 - f
    row, col = _tri_masks(c)
    tri = jnp.where(col <= row, 1.0, 0.0).astype(BF16)
    incl = col <= row
    gs = _dot_exact_lhs(tri, jnp.log(f))
    eg = jnp.exp(gs)
    qg_all = z_ref[:, 0:G] * eg
    kg_all = kx * jnp.exp(-gs)
    g_last = gs[c - 1:c, :]
    kd_all = kx * jnp.exp(g_last - gs)
    eg_last = jnp.exp(g_last)
    HS = range(HG_HEADS)
    sls = [slice(h * dk, (h + 1) * dk) for h in HS]
    vv = [z_ref[:, 2 * G + h * dk:2 * G + (h + 1) * dk] for h in HS]
    A = [jnp.where(incl, _dot_lo(qg_all[:, sl], kg_all[:, sl], _NT), 0.0) for sl in sls]
    st = [st_ref[h] for h in HS]
    o = [_dot_lo(qg_all[:, sls[h]], st[h], _NT) + _dot_lo(A[h], vv[h]) for h in HS]
    for h in HS:
        st_ref[h] = st[h] * eg_last[:, sls[h]] + _dot_lo(vv[h], kd_all[:, sls[h]], _TN)
    for h in HS:
        sl = sls[h]
        hg = z_ref[:, 3 * G + h * dk:3 * G + (h + 1) * dk]
        o_ref[:, sl] = _rms(o[h], ng_ref[:, sl]) * (hg * jax.nn.sigmoid(hg))

    @pl.when(pl.program_id(1) == pl.num_programs(1) - 1)
    def _():
        so_ref[0] = st_ref[...]


def _hgrn2(z_hg, lb, S0, B, T, P):
    G = GROUP_W
    c = min(HG_CHUNK, T)
    nc = T // c
    dk = G // HG_HEADS
    st_spec = pl.BlockSpec((1, HG_HEADS, dk, dk), lambda b, i: (b, 0, 0, 0))
    out, so = pl.pallas_call(
        _hgrn_body,
        grid=(B, nc),
        in_specs=[pl.BlockSpec((c, 4 * G), lambda b, i: (b * nc + i, 0)),
                  pl.BlockSpec((1, G), lambda b, i: (0, 0)), st_spec, pl.BlockSpec((1, G), lambda b, i: (0, 0))],
        out_specs=[pl.BlockSpec((c, G), lambda b, i: (b * nc + i, 0)), st_spec],
        out_shape=[jax.ShapeDtypeStruct((B * T, G), F32), jax.ShapeDtypeStruct(S0.shape, F32)],
        scratch_shapes=[pltpu.VMEM((HG_HEADS, dk, dk), F32)],
        compiler_params=_params(("parallel", "arbitrary"), 32),
        name="hgrn2",
    )(z_hg, lb.reshape(1, G), jnp.swapaxes(S0, -1, -2), P['hg_norm_g'].reshape(1, G))
    return out, jnp.swapaxes(so, -1, -2)


RW_CHUNK = 64
RW_SUB = 16
RW_INV_LANES = 1024

_NT = (((1,), (1,)), ((), ()))
_TN = (((0,), (0,)), ((), ()))
_NN = (((1,), (0,)), ((), ()))


def _split3(x):
    h1 = x.astype(BF16)
    r1 = x - h1.astype(F32)
    h2 = r1.astype(BF16)
    h3 = (r1 - h2.astype(F32)).astype(BF16)
    return h1, h2, h3


def _dot_lo(a, b, dims=_NN):
    return lax.dot_general(a.astype(BF16), b.astype(BF16), dims, preferred_element_type=F32)


def _dot_hi(a, b, dims=_NN):
    ah = a.astype(BF16)
    al = (a - ah.astype(F32)).astype(BF16)
    bh = b.astype(BF16)
    bl = (b - bh.astype(F32)).astype(BF16)
    d = functools.partial(lax.dot_general, dimension_numbers=dims, preferred_element_type=F32)
    return d(ah, bh) + (d(al, bh) + d(ah, bl))


def _dot_exact_rhs(a, b):
    h1, h2, h3 = _split3(a)
    d = functools.partial(jnp.dot, preferred_element_type=F32)
    return d(h1, b) + (d(h2, b) + d(h3, b))


def _dot_exact_lhs(a, b):
    h1, h2, h3 = _split3(b)
    d = functools.partial(jnp.dot, preferred_element_type=F32)
    return d(a, h1) + (d(a, h2) + d(a, h3))


def _softplus(x):
    return jnp.maximum(x, 0.0) + jnp.log1p(jnp.exp(-jnp.abs(x)))


def _rw_prep_body(z_ref, shift_ref, mu_ref, w0_ref, w2_ref, a0_ref, a2_ref, g2_ref, kk_ref, ka_ref, ones_ref,
                  r_ref, lw_ref, k_ref, v_ref, kap_ref, bet_ref, g_ref, prev_ref):
    G = GROUP_W

    @pl.when(pl.program_id(1) == 0)
    def _():
        prev_ref[0:1, :] = shift_ref[0]

    z = z_ref[...]
    tt = z.shape[0]
    row = lax.broadcasted_iota(jnp.int32, z.shape, 0)
    shifted = jnp.where(row == 0, prev_ref[0:1, :], pltpu.roll(z, 1, axis=0))
    prev_ref[0:1, :] = z[tt - 1:tt, :]
    zm = z + (shifted - z) * mu_ref[...]
    r, k, v = zm[:, 0:G], zm[:, G:2 * G], zm[:, 2 * G:3 * G]
    o = 3 * G
    wd = zm[:, o:o + RW_DECAY_LORA]
    ad = zm[:, o + RW_DECAY_LORA:o + RW_DECAY_LORA + RW_A_LORA]
    gd = zm[:, o + RW_DECAY_LORA + RW_A_LORA:]
    w = -_softplus(-(w0_ref[...] + _dot_lo(jnp.tanh(wd), w2_ref[...]))) - 0.5
    a = jax.nn.sigmoid(a0_ref[...] + _dot_lo(ad, a2_ref[...]))
    kk = k * kk_ref[...]
    ss = _dot_exact_rhs(kk * kk, ones_ref[...])
    kap = kk / jnp.maximum(jnp.sqrt(ss), 1e-12)
    r_ref[...] = r
    lw_ref[...] = -jnp.exp(w)
    k_ref[...] = k * (1.0 + (a - 1.0) * ka_ref[...])
    v_ref[...] = v
    kap_ref[...] = kap
    bet_ref[...] = kap * a
    g_ref[...] = _dot_lo(jax.nn.sigmoid(gd), g2_ref[...])


def _rw_prep(zr, shift, B, T, P):
    n, cols = zr.shape
    G = GROUP_W
    tt = _row_tile(T, 256)
    nt = T // tt
    ones_bd = jnp.kron(jnp.eye(RW_HEADS, dtype=F32), jnp.ones((RW_HD, RW_HD), F32)).astype(BF16)
    row = lambda x: x.reshape(1, -1)
    full = lambda shape: pl.BlockSpec(shape, lambda b, i: (0,) * len(shape))
    tile = pl.BlockSpec((tt, G), lambda b, i: (b * nt + i, 0))
    return pl.pallas_call(
        _rw_prep_body,
        grid=(B, nt),
        in_specs=[
            pl.BlockSpec((tt, cols), lambda b, i: (b * nt + i, 0)),
            pl.BlockSpec((1, 1, cols), lambda b, i: (b, 0, 0)),
            full((1, cols)), full((1, G)), full((RW_DECAY_LORA, G)), full((1, G)), full((RW_A_LORA, G)),
            full((RW_GATE_LORA, G)), full((1, G)), full((1, G)), full((G, G)),
        ],
        out_specs=[tile] * 7,
        out_shape=[jax.ShapeDtypeStruct((n, G), F32)] * 7,
        scratch_shapes=[pltpu.VMEM((8, cols), F32)],
        compiler_params=_params(("parallel", "arbitrary"), 40),
        name="rwkv_prep",
    )(zr, shift.reshape(B, 1, cols), row(P['rw_mu']), row(P['rw_w0']), P['rw_w2'].astype(BF16), row(P['rw_a0']),
      P['rw_a2'].astype(BF16), P['rw_g2'].astype(BF16), row(P['rw_kk']), row(P['rw_ka']), ones_bd)


def _rw_scaled(lw, kap, bet, tri):
    cs = _dot_exact_lhs(tri, lw)
    return cs, kap * jnp.exp(cs - lw), bet * jnp.exp(-cs)


def _tri_masks(c):
    row = lax.broadcasted_iota(jnp.int32, (c, c), 0)
    col = lax.broadcasted_iota(jnp.int32, (c, c), 1)
    return row, col


def _rw_ldiag_body(lw_ref, kap_ref, bet_ref, o_ref):
    c = lw_ref.shape[0]
    row, col = _tri_masks(c)
    tri = jnp.where(col <= row, 1.0, 0.0).astype(BF16)
    _, kk_all, bt_all = _rw_scaled(lw_ref[...], kap_ref[...], bet_ref[...], tri)
    srow, scol = _tri_masks(RW_SUB)
    for h in range(RW_HEADS):
        sl = slice(h * RW_HD, (h + 1) * RW_HD)
        L = _dot_hi(kk_all[:, sl], bt_all[:, sl], _NT)
        for b in range(c // RW_SUB):
            rs = slice(b * RW_SUB, (b + 1) * RW_SUB)
            o_ref[rs, h * RW_SUB:(h + 1) * RW_SUB] = jnp.where(scol < srow, L[rs, rs], 0.0)


def _rw_inv_body(l_ref, t_ref):
    n = RW_SUB
    one = jnp.ones(l_ref.shape[2:], F32)
    zero = jnp.zeros(l_ref.shape[2:], F32)
    for t in range(n):
        for s in range(n):
            if s > t:
                t_ref[t, s] = zero
            elif s == t:
                t_ref[t, s] = one
            else:
                acc = l_ref[t, s]
                for j in range(s + 1, t):
                    acc = acc + l_ref[t, j] * t_ref[j, s]
                t_ref[t, s] = -acc


def _rw_main_body(r_ref, lw_ref, k_ref, v_ref, kap_ref, bet_ref, g_ref, td_ref, h0_ref, rk_ref, lng_ref, lnb_ref,
                  o_ref, hout_ref, h_ref):
    ci = pl.program_id(1)
    c = r_ref.shape[0]
    nb = c // RW_SUB

    @pl.when(ci == 0)
    def _():
        h_ref[...] = h0_ref[0]

    row, col = _tri_masks(c)
    tri = jnp.where(col <= row, 1.0, 0.0).astype(BF16)
    strict = col < row
    incl = col <= row
    lw = lw_ref[...]
    cs, kk_all, bt_all = _rw_scaled(lw, kap_ref[...], bet_ref[...], tri)
    gi = jnp.exp(-cs)
    gg = jnp.exp(cs)
    kt_all = k_ref[...] * gi
    rt_all = r_ref[...] * gg
    bonus_all = r_ref[...] * k_ref[...] * rk_ref[...]
    hrow = lax.broadcasted_iota(jnp.int32, (RW_HD, RW_HD), 0)
    hcol = lax.broadcasted_iota(jnp.int32, (RW_HD, RW_HD), 1)
    HS = range(RW_HEADS)
    sls = [slice(h * RW_HD, (h + 1) * RW_HD) for h in HS]
    Kk = [kk_all[:, sl] for sl in sls]
    Bt = [bt_all[:, sl] for sl in sls]
    Kt = [kt_all[:, sl] for sl in sls]
    Rt = [rt_all[:, sl] for sl in sls]
    vv = [v_ref[:, sl] for sl in sls]
    Lm = [jnp.where(strict, _dot_hi(Kk[h], Bt[h], _NT), 0.0) for h in HS]
    A1 = [jnp.where(strict, _dot_hi(Kk[h], Kt[h], _NT), 0.0) for h in HS]
    A4 = [jnp.where(incl, _dot_lo(Rt[h], Bt[h], _NT), 0.0) for h in HS]
    A3 = [jnp.where(incl, _dot_lo(Rt[h], Kt[h], _NT), 0.0) for h in HS]
    X = [jnp.concatenate([Kk[h], _dot_hi(A1[h], vv[h])], axis=1) for h in HS]
    zs = [[] for _ in HS]
    for b in range(nb):
        rs = slice(b * RW_SUB, (b + 1) * RW_SUB)
        rhs = [X[h][rs] for h in HS]
        if b:
            rhs = [rhs[h] - _dot_hi(Lm[h][rs, 0:b * RW_SUB], jnp.concatenate(zs[h], axis=0)) for h in HS]
        for h in HS:
            zs[h].append(_dot_hi(td_ref[rs, h * RW_SUB:(h + 1) * RW_SUB], rhs[h]))
    Z = [jnp.concatenate(zs[h], axis=0) if nb > 1 else zs[h][0] for h in HS]
    A4Z = [_dot_lo(A4[h], Z[h]) for h in HS]
    Rhat = [Rt[h] - A4Z[h][:, :RW_HD] for h in HS]
    Yhat = [_dot_lo(A3[h], vv[h]) - A4Z[h][:, RW_HD:] for h in HS]
    gC = [gg[c - 1:c, sl] for sl in sls]
    MN = [_dot_hi(Bt[h] * gC[h], Z[h], _TN) for h in HS]
    Mp = [jnp.where(hrow == hcol, gC[h], 0.0) - MN[h][:, :RW_HD] for h in HS]
    Np = [_dot_hi(Kt[h] * gC[h], vv[h], _TN) - MN[h][:, RW_HD:] for h in HS]
    H0 = [h_ref[h] for h in HS]
    ys = [_dot_lo(Rhat[h], H0[h]) + Yhat[h] for h in HS]
    for h in HS:
        h_ref[h] = _dot_hi(Mp[h], H0[h]) + Np[h]
    for h in HS:
        sl, y = sls[h], ys[h]
        mu = jnp.mean(y, axis=-1, keepdims=True)
        var = jnp.mean(jnp.square(y - mu), axis=-1, keepdims=True)
        yn = (y - mu) * lax.rsqrt(var + RW_LN_EPS) * lng_ref[:, sl] + lnb_ref[:, sl]
        yn = yn + jnp.sum(bonus_all[:, sl], axis=-1, keepdims=True) * vv[h]
        o_ref[:, sl] = yn * g_ref[:, sl]

    @pl.when(ci == pl.num_programs(1) - 1)
    def _():
        hout_ref[0] = h_ref[...]


def _rwkv7(zr, shift, S0, B, T, P):
    G = GROUP_W
    n = B * T
    r, lw, k, v, kap, bet, g = _rw_prep(zr, shift, B, T, P)
    c = min(RW_CHUNK, T)
    nc = T // c
    nb = c // RW_SUB
    tile = pl.BlockSpec((c, G), lambda b, i: (b * nc + i, 0))
    ld = pl.pallas_call(
        _rw_ldiag_body,
        grid=(B, nc),
        in_specs=[tile] * 3,
        out_specs=pl.BlockSpec((c, RW_HEADS * RW_SUB), lambda b, i: (b * nc + i, 0)),
        out_shape=jax.ShapeDtypeStruct((n, RW_HEADS * RW_SUB), F32),
        compiler_params=_params(("parallel", "parallel"), 32),
        name="rwkv_ldiag",
    )(lw, kap, bet)
    ni = n // RW_SUB * RW_HEADS
    lt = ld.reshape(n // RW_SUB, RW_SUB, RW_HEADS, RW_SUB).transpose(1, 3, 0, 2).reshape(RW_SUB, RW_SUB, ni)
    nip = -(-ni // RW_INV_LANES) * RW_INV_LANES
    lt = jnp.pad(lt, ((0, 0), (0, 0), (0, nip - ni))).reshape(RW_SUB, RW_SUB, nip // V7X_LANES, V7X_LANES)
    inv_spec = pl.BlockSpec((RW_SUB, RW_SUB, RW_INV_LANES // V7X_LANES, V7X_LANES), lambda i: (0, 0, i, 0))
    tt = pl.pallas_call(
        _rw_inv_body,
        grid=(nip // RW_INV_LANES,),
        in_specs=[inv_spec],
        out_specs=inv_spec,
        out_shape=jax.ShapeDtypeStruct(lt.shape, F32),
        compiler_params=_params(("parallel",), 32),
        name="rwkv_inv",
    )(lt)
    td = tt.reshape(RW_SUB, RW_SUB, nip)[:, :, :ni].reshape(RW_SUB, RW_SUB, n // RW_SUB, RW_HEADS)
    td = td.transpose(2, 0, 3, 1).reshape(n, RW_HEADS * RW_SUB)
    h0 = jnp.swapaxes(S0, -1, -2)
    prow = lambda x: pl.BlockSpec((1, G), lambda b, i: (0, 0))
    st_spec = pl.BlockSpec((1, RW_HEADS, RW_HD, RW_HD), lambda b, i: (b, 0, 0, 0))
    out, hl = pl.pallas_call(
        _rw_main_body,
        grid=(B, nc),
        in_specs=[tile] * 7 + [pl.BlockSpec((c, RW_HEADS * RW_SUB), lambda b, i: (b * nc + i, 0)), st_spec,
                               prow(0), prow(0), prow(0)],
        out_specs=[tile, st_spec],
        out_shape=[jax.ShapeDtypeStruct((n, G), F32), jax.ShapeDtypeStruct(S0.shape, F32)],
        scratch_shapes=[pltpu.VMEM((RW_HEADS, RW_HD, RW_HD), F32)],
        compiler_params=_params(("parallel", "arbitrary"), 32),
        name="rwkv_main",
    )(r, lw, k, v, kap, bet, g, td, h0, P['rw_rk'].reshape(1, G), P['rw_ln_g'].reshape(1, G),
      P['rw_ln_b'].reshape(1, G))
    return out, zr.reshape(B, T, -1)[:, -1], jnp.swapaxes(hl, -1, -2)


def _even_mixer(x2, B, T, g, st, P):
    conv_buf, lru_h, k_past, v_past, lf_past = st
    G = GROUP_W
    z_rg, z_qkv, z_og, z_fl = _norm_matmul(x2, g, P['e_w_in'], (2 * G, 3 * G, G, V7X_LANES))
    rnn_out, conv_new, h_last = _lru(z_rg, conv_buf, lru_h, B, T, P)
    qb, kn, kb, v, vb, lf = _fox_prep(z_qkv, z_fl, P)
    past = k_past.shape[1]
    lf_all = lf.reshape(B, T, V7X_LANES)
    kb_all, vb_all = kb.reshape(B, T, G), vb.reshape(B, T, G)
    if past:
        lf_all = jnp.concatenate([jnp.pad(lf_past, ((0, 0), (0, 0), (0, V7X_LANES - FOX_HEADS))), lf_all], axis=1)
        kb_all = jnp.concatenate([k_past.reshape(B, past, G).astype(BF16), kb_all], axis=1)
        vb_all = jnp.concatenate([v_past.reshape(B, past, G).astype(BF16), vb_all], axis=1)
    f_all = _cumsum_time(lf_all)
    fox_out = _fox_attention(qb, kb_all, vb_all, f_all, z_og, B, T, past)
    x2 = _out_proj(x2, rnn_out, fox_out, P['e_w_out'])
    heads = lambda t: t.reshape(B, T, FOX_HEADS, FOX_HD)
    return x2, (conv_new, h_last, heads(kn), heads(v), lf.reshape(B, T, V7X_LANES)[..., :FOX_HEADS])


def _odd_mixer(x2, B, T, g, st, lb, P):
    S_hg, shift, S_rw = st
    G = GROUP_W
    z_hg, z_rw = _norm_matmul(x2, g, P['o_w_in'], (4 * G, P['o_w_in'].shape[1] - 4 * G))
    hg_out, S_hg_new = _hgrn2(z_hg, lb, S_hg, B, T, P)
    rw_out, shift_new, S_rw_new = _rwkv7(z_rw, shift, S_rw, B, T, P)
    x2 = _out_proj(x2, hg_out, rw_out, P['o_w_out'])
    return x2, (S_hg_new, shift_new, S_rw_new)


def _trunk(x, states, W):
    lru_conv, lru_h, fox_k, fox_v, fox_lf, hg_S, rw_shift, rw_S = states
    B, T, D = x.shape
    depth = W['norm_g'].shape[0]
    sm = jax.nn.softmax(W['hg_lb_logits'], axis=0)
    lower_bounds = jnp.cumsum(sm, axis=0) - sm[0]
    x2 = x.reshape(B * T, D)
    even_new, odd_new = [], []
    for layer in range(depth):
        g = W['norm_g'][layer]
        x2 = _ffn(x2, g[0], W['ffn_w_in'][layer][0], W['ffn_w_out'][layer][0])
        if layer % 2 == 0:
            e = layer // 2
            P = {n: W[n][e] for n in ('e_w_in', 'e_w_out', 'lru_conv_w', 'lru_conv_b', 'lru_wa', 'lru_ba', 'lru_wx',
                                      'lru_bx', 'lru_lambda', 'fox_q_gain', 'fox_k_gain', 'fox_f_bias')}
            x2, new = _even_mixer(x2, B, T, g[1], (lru_conv[e], lru_h[e], fox_k[e], fox_v[e], fox_lf[e]), P)
            even_new.append(new)
        else:
            o = layer // 2
            P = {n: W[n][o] for n in ('o_w_in', 'o_w_out', 'hg_norm_g', 'rw_mu', 'rw_w0', 'rw_w2', 'rw_a0', 'rw_a2',
                                      'rw_g2', 'rw_kk', 'rw_ka', 'rw_rk', 'rw_ln_g', 'rw_ln_b')}
            x2, new = _odd_mixer(x2, B, T, g[1], (hg_S[o], rw_shift[o], rw_S[o]), lower_bounds[layer], P)
            odd_new.append(new)
        x2 = _ffn(x2, g[2], W['ffn_w_in'][layer][1], W['ffn_w_out'][layer][1])
    ev = [jnp.stack([n[j] for n in even_new]) for j in range(5)]
    od = [jnp.stack([n[j] for n in odd_new]) for j in range(3)]
    return x2.reshape(B, T, D), (ev[0], ev[1], ev[2], ev[3], ev[4], od[0], od[1], od[2])


def kernel(x_prompt, x_sample, state_lru_conv, state_lru_h, cache_fox_k, cache_fox_v, cache_fox_logf,
           state_hgrn_S, state_rwkv_shift, state_rwkv_S, norm_g, ffn_w_in, ffn_w_out, e_w_in, e_w_out,
           lru_conv_w, lru_conv_b, lru_wa, lru_ba, lru_wx, lru_bx, lru_lambda, fox_q_gain, fox_k_gain,
           fox_f_bias, o_w_in, o_w_out, hg_lb_logits, hg_norm_g, rw_mu, rw_w0, rw_w2, rw_a0, rw_a2, rw_g2,
           rw_kk, rw_ka, rw_rk, rw_ln_g, rw_ln_b):
    n_even, n_odd = e_w_in.shape[0], o_w_in.shape[0]
    W = dict(norm_g=norm_g, ffn_w_in=ffn_w_in.astype(BF16), ffn_w_out=ffn_w_out.astype(BF16),
             e_w_in=_pad_cols(e_w_in.astype(BF16)), e_w_out=e_w_out.astype(BF16),
             lru_conv_w=lru_conv_w, lru_conv_b=lru_conv_b, lru_wa=lru_wa, lru_ba=lru_ba, lru_wx=lru_wx,
             lru_bx=lru_bx, lru_lambda=lru_lambda, fox_q_gain=fox_q_gain, fox_k_gain=fox_k_gain,
             fox_f_bias=fox_f_bias, o_w_in=o_w_in.astype(BF16), o_w_out=o_w_out.astype(BF16),
             hg_lb_logits=hg_lb_logits, hg_norm_g=hg_norm_g, rw_mu=rw_mu, rw_w0=rw_w0, rw_w2=rw_w2, rw_a0=rw_a0,
             rw_a2=rw_a2, rw_g2=rw_g2, rw_kk=rw_kk, rw_ka=rw_ka, rw_rk=rw_rk, rw_ln_g=rw_ln_g, rw_ln_b=rw_ln_b)
    nb = x_prompt.shape[0]
    dt = x_prompt.dtype
    prompt_states = (jnp.zeros((n_even, nb, CONV_W - 1, GROUP_W), dt),
                     jnp.zeros((n_even, nb, GROUP_W), dt),
                     jnp.zeros((n_even, nb, 0, FOX_HEADS, FOX_HD), dt),
                     jnp.zeros((n_even, nb, 0, FOX_HEADS, FOX_HD), dt),
                     jnp.zeros((n_even, nb, 0, FOX_HEADS), dt),
                     jnp.zeros((n_odd, nb, HG_HEADS, GROUP_W // HG_HEADS, GROUP_W // HG_HEADS), dt),
                     jnp.zeros((n_odd, nb, rw_mu.shape[1]), dt),
                     jnp.zeros((n_odd, nb, RW_HEADS, RW_HD, RW_HD), dt))
    sample_states = (state_lru_conv, state_lru_h, cache_fox_k, cache_fox_v, cache_fox_logf,
                     state_hgrn_S, state_rwkv_shift, state_rwkv_S)
    y_prompt, p_new = _trunk(x_prompt, prompt_states, W)
    y_sample, s_new = _trunk(x_sample, sample_states, W)
    lru_conv_p, lru_h_p, fox_k_p, fox_v_p, fox_logf_p, hgrn_S_p, rwkv_shift_p, rwkv_S_p = p_new
    lru_conv_s, lru_h_s, fox_k_s, fox_v_s, fox_logf_s, hgrn_S_s, rwkv_shift_s, rwkv_S_s = s_new
    return (y_prompt, y_sample, lru_conv_p, lru_conv_s, lru_h_p, lru_h_s, fox_k_p, fox_k_s, fox_v_p, fox_v_s,
            fox_logf_p, fox_logf_s, hgrn_S_p, hgrn_S_s, rwkv_shift_p, rwkv_shift_s, rwkv_S_p, rwkv_S_s)
```

```python
import functools

import jax
import jax.numpy as jnp
from jax import lax
from jax.experimental import pallas as pl
from jax.experimental.pallas import tpu as pltpu

F32 = jnp.float32
BF16 = jnp.bfloat16

NORM_EPS = 1e-6
GROUP_W = 512
LRU_BLOCKS = 8
CONV_W = 4
LRU_C = 8.0
FOX_HEADS = 8
FOX_HD = 64
FOX_BLOCK = 128
HG_HEADS = 4
CHUNK = 64
RW_HEADS = 8
RW_HD = 64
RW_DECAY_LORA = 64
RW_A_LORA = 64
RW_GATE_LORA = 128
RW_LN_EPS = 64e-5

V7X_LANES = 128
FFN_COL_TILE = 256


def _row_tile(n, want):
    t = min(n, want)
    while n % t:
        t //= 2
    return t


def _params(sem, vmem_mib):
    return pltpu.CompilerParams(dimension_semantics=sem, vmem_limit_bytes=vmem_mib << 20)


def _pad_cols(w):
    pad = -w.shape[-1] % V7X_LANES
    return jnp.pad(w, [(0, 0)] * (w.ndim - 1) + [(0, pad)])


def _rms(x, g):
    return x * lax.rsqrt(jnp.mean(x * x, axis=-1, keepdims=True) + NORM_EPS) * g


def _ffn_body(x_ref, g_ref, wg_ref, wu_ref, wo_ref, o_ref, h_ref, acc_ref):
    j = pl.program_id(1)

    @pl.when(j == 0)
    def _():
        h_ref[...] = _rms(x_ref[...], g_ref[...]).astype(BF16)
        acc_ref[...] = jnp.zeros_like(acc_ref)

    h = h_ref[...]
    gate = jnp.dot(h, wg_ref[...], preferred_element_type=F32)
    up = jnp.dot(h, wu_ref[...], preferred_element_type=F32)
    act = (gate * jax.nn.sigmoid(gate) * up).astype(BF16)
    acc_ref[...] += jnp.dot(act, wo_ref[...], preferred_element_type=F32)

    @pl.when(j == pl.num_programs(1) - 1)
    def _():
        o_ref[...] = x_ref[...] + 0.5 * acc_ref[...]


def _ffn(x, g, w_in, w_out):
    n, d = x.shape
    f = w_out.shape[0]
    tm = _row_tile(n, 512)
    tf = FFN_COL_TILE
    nf = f // tf
    return pl.pallas_call(
        _ffn_body,
        grid=(n // tm, nf),
        in_specs=[
            pl.BlockSpec((tm, d), lambda i, j: (i, 0)),
            pl.BlockSpec((1, d), lambda i, j: (0, 0)),
            pl.BlockSpec((d, tf), lambda i, j: (0, j)),
            pl.BlockSpec((d, tf), lambda i, j: (0, nf + j)),
            pl.BlockSpec((tf, d), lambda i, j: (j, 0)),
        ],
        out_specs=pl.BlockSpec((tm, d), lambda i, j: (i, 0)),
        out_shape=jax.ShapeDtypeStruct((n, d), F32),
        scratch_shapes=[pltpu.VMEM((tm, d), BF16), pltpu.VMEM((tm, d), F32)],
        compiler_params=_params(("parallel", "arbitrary"), 40),
        name="ffn",
    )(x, g.reshape(1, d), w_in, w_in, w_out)


def _norm_matmul_body(x_ref, g_ref, w_ref, *o_refs):
    h = _rms(x_ref[...], g_ref[...]).astype(BF16)
    z = jnp.dot(h, w_ref[...], preferred_element_type=F32)
    start = 0
    for o_ref in o_refs:
        width = o_ref.shape[1]
        o_ref[...] = z[:, start:start + width]
        start += width


def _norm_matmul(x, g, w, widths):
    n, d = x.shape
    c = w.shape[1]
    assert sum(widths) == c and all(wd % V7X_LANES == 0 for wd in widths)
    tm = _row_tile(n, 256)
    return pl.pallas_call(
        _norm_matmul_body,
        grid=(n // tm,),
        in_specs=[
            pl.BlockSpec((tm, d), lambda i: (i, 0)),
            pl.BlockSpec((1, d), lambda i: (0, 0)),
            pl.BlockSpec((d, c), lambda i: (0, 0)),
        ],
        out_specs=[pl.BlockSpec((tm, wd), lambda i: (i, 0)) for wd in widths],
        out_shape=[jax.ShapeDtypeStruct((n, wd), F32) for wd in widths],
        compiler_params=_params(("parallel",), 48),
        name="norm_matmul",
    )(x, g.reshape(1, d), w)


def _out_proj_body(x_ref, a_ref, b_ref, wa_ref, wb_ref, o_ref):
    acc = jnp.dot(a_ref[...].astype(BF16), wa_ref[...], preferred_element_type=F32)
    acc += jnp.dot(b_ref[...].astype(BF16), wb_ref[...], preferred_element_type=F32)
    o_ref[...] = x_ref[...] + acc


def _out_proj(x, a, b, w):
    n, d = x.shape
    ga, gb = a.shape[1], b.shape[1]
    tm = _row_tile(n, 512)
    return pl.pallas_call(
        _out_proj_body,
        grid=(n // tm,),
        in_specs=[
            pl.BlockSpec((tm, d), lambda i: (i, 0)),
            pl.BlockSpec((tm, ga), lambda i: (i, 0)),
            pl.BlockSpec((tm, gb), lambda i: (i, 0)),
            pl.BlockSpec((ga, d), lambda i: (0, 0)),
            pl.BlockSpec((gb, d), lambda i: (0, 0)),
        ],
        out_specs=pl.BlockSpec((tm, d), lambda i: (i, 0)),
        out_shape=jax.ShapeDtypeStruct((n, d), F32),
        compiler_params=_params(("parallel",), 32),
        name="out_proj",
    )(x, a, b, w[:ga], w[ga:])


LRU_ROWS = 256
CONV_PAD = 8


def _expm1(x):
    series = x * (1.0 + x * (1 / 2 + x * (1 / 6 + x * (1 / 24 + x * (1 / 120 + x * (1 / 720 + x * (1 / 5040 + x * (1 / 40320))))))))
    return jnp.where(jnp.abs(x) < 0.25, series, jnp.exp(x) - 1.0)


def _shift_rows(x, s, fill):
    row = lax.broadcasted_iota(jnp.int32, x.shape, 0)
    return jnp.where(row >= s, pltpu.roll(x, s, axis=0), fill)


def _lru_body(z_ref, buf_ref, h0_ref, cw_ref, cb_ref, wa_ref, ba_ref, wx_ref, bx_ref, lam_ref,
              o_ref, bufo_ref, ho_ref, x_ref, hc_ref):
    G = GROUP_W
    tt = z_ref.shape[0]

    @pl.when(pl.program_id(1) == 0)
    def _():
        x_ref[0:CONV_PAD, :] = buf_ref[0]
        hc_ref[...] = jnp.broadcast_to(h0_ref[0], hc_ref.shape)

    x_ref[CONV_PAD:CONV_PAD + tt, :] = z_ref[:, 0:G]
    xc = cb_ref[...]
    for j in range(CONV_W):
        lo = CONV_PAD - (CONV_W - 1) + j
        xc = xc + x_ref[lo:lo + tt, :] * cw_ref[j:j + 1, :]
    hist = x_ref[tt:tt + CONV_PAD, :]
    x_ref[0:CONV_PAD, :] = hist
    bufo_ref[0] = hist

    xb = xc.astype(BF16)
    r = jax.nn.sigmoid(jnp.dot(xb, wa_ref[...], preferred_element_type=F32) + ba_ref[...])
    ig = jax.nn.sigmoid(jnp.dot(xb, wx_ref[...], preferred_element_type=F32) + bx_ref[...])
    log_a = (-LRU_C * _softplus(-lam_ref[...])) * r
    a = jnp.exp(log_a)
    b = jnp.sqrt(-_expm1(2.0 * log_a)) * (ig * xc)
    s = 1
    while s < tt:
        b = a * _shift_rows(b, s, 0.0) + b
        a = a * _shift_rows(a, s, 1.0)
        s *= 2
    h = a * hc_ref[0:1, :] + b
    hc_ref[...] = jnp.broadcast_to(h[tt - 1:tt, :], hc_ref.shape)
    ho_ref[0] = h[tt - 1:tt, :]
    o_ref[...] = jax.nn.gelu(z_ref[:, G:2 * G]) * h


def _block_diag_dense(w):
    nb, bs, _ = w.shape
    eye = jnp.eye(nb, dtype=w.dtype)
    return (eye[:, None, :, None] * w[:, :, None, :]).reshape(nb * bs, nb * bs)


def _lru(z_rg, conv_buf, h0, B, T, P):
    G = GROUP_W
    n = B * T
    tt = _row_tile(T, LRU_ROWS)
    nt = T // tt
    buf = jnp.pad(conv_buf, ((0, 0), (CONV_PAD - (CONV_W - 1), 0), (0, 0)))
    cw = jnp.pad(P['lru_conv_w'], ((0, CONV_PAD - CONV_W), (0, 0)))
    row = lambda x: x.reshape(1, G)
    full = lambda shape: pl.BlockSpec(shape, lambda b, i: (0,) * len(shape))
    out, bufo, ho = pl.pallas_call(
        _lru_body,
        grid=(B, nt),
        in_specs=[
            pl.BlockSpec((tt, 2 * G), lambda b, i: (b * nt + i, 0)),
            pl.BlockSpec((1, CONV_PAD, G), lambda b, i: (b, 0, 0)),
            pl.BlockSpec((1, 1, G), lambda b, i: (b, 0, 0)),
            full((CONV_PAD, G)), full((1, G)), full((G, G)), full((1, G)), full((G, G)), full((1, G)), full((1, G)),
        ],
        out_specs=[
            pl.BlockSpec((tt, G), lambda b, i: (b * nt + i, 0)),
            pl.BlockSpec((1, CONV_PAD, G), lambda b, i: (b, 0, 0)),
            pl.BlockSpec((1, 1, G), lambda b, i: (b, 0, 0)),
        ],
        out_shape=[jax.ShapeDtypeStruct((n, G), F32), jax.ShapeDtypeStruct((B, CONV_PAD, G), F32),
                   jax.ShapeDtypeStruct((B, 1, G), F32)],
        scratch_shapes=[pltpu.VMEM((tt + CONV_PAD, G), F32), pltpu.VMEM((8, G), F32)],
        compiler_params=_params(("parallel", "arbitrary"), 32),
        name="lru",
    )(z_rg, buf, h0.reshape(B, 1, G), cw, row(P['lru_conv_b']), _block_diag_dense(P['lru_wa']).astype(BF16),
      row(P['lru_ba']), _block_diag_dense(P['lru_wx']).astype(BF16), row(P['lru_bx']), row(P['lru_lambda']))
    return out, bufo[:, CONV_PAD - (CONV_W - 1):], ho.reshape(B, G)


FOX_Q_COLS = 512
FOX_K_ROWS = 512
FOX_F_SPLIT = 3
FOX_NEG = -1e30
LOG2E = 1.4426950408889634
HEAD_PAIRS = FOX_HEADS // 2


def _fox_prep_body(z_ref, fl_ref, qg_ref, kg_ref, fb_ref, ones_ref, q_ref, k_ref, kb_ref, v_ref, vb_ref, lf_ref):
    G = GROUP_W
    q, k, v = z_ref[:, 0:G], z_ref[:, G:2 * G], z_ref[:, 2 * G:3 * G]
    inv = 1.0 / FOX_HD
    qn = q * lax.rsqrt(_dot_exact_rhs(q * q, ones_ref[...]) * inv + NORM_EPS) * qg_ref[...]
    kn = k * lax.rsqrt(_dot_exact_rhs(k * k, ones_ref[...]) * inv + NORM_EPS) * kg_ref[...]
    q_ref[...] = (qn * (LOG2E * FOX_HD ** -0.5)).astype(BF16)
    k_ref[...] = kn
    kb_ref[...] = kn.astype(BF16)
    v_ref[...] = v
    vb_ref[...] = v.astype(BF16)
    x = fl_ref[...] + fb_ref[...]
    lf_ref[...] = -_softplus(-x)


def _fox_prep(z_qkv, z_fl, P):
    n = z_qkv.shape[0]
    G = GROUP_W
    tt = _row_tile(n, 256)
    ones_bd = jnp.kron(jnp.eye(FOX_HEADS, dtype=F32), jnp.ones((FOX_HD, FOX_HD), F32)).astype(BF16)
    fb = jnp.pad(P['fox_f_bias'], (0, V7X_LANES - FOX_HEADS)).reshape(1, V7X_LANES)
    tile = lambda w: pl.BlockSpec((tt, w), lambda i: (i, 0))
    full = lambda shape: pl.BlockSpec(shape, lambda i: (0,) * len(shape))
    return pl.pallas_call(
        _fox_prep_body,
        grid=(n // tt,),
        in_specs=[tile(3 * G), tile(V7X_LANES), full((1, G)), full((1, G)), full((1, V7X_LANES)), full((G, G))],
        out_specs=[tile(G)] * 5 + [tile(V7X_LANES)],
        out_shape=[jax.ShapeDtypeStruct((n, G), BF16), jax.ShapeDtypeStruct((n, G), F32),
                   jax.ShapeDtypeStruct((n, G), BF16), jax.ShapeDtypeStruct((n, G), F32),
                   jax.ShapeDtypeStruct((n, G), BF16), jax.ShapeDtypeStruct((n, V7X_LANES), F32)],
        compiler_params=_params(("parallel",), 32),
        name="fox_prep",
    )(z_qkv, z_fl, jnp.tile(P['fox_q_gain'], FOX_HEADS).reshape(1, G),
      jnp.tile(P['fox_k_gain'], FOX_HEADS).reshape(1, G), fb, ones_bd)


def _cumsum_body(x_ref, hi_ref, mid_ref, lo_ref, c_ref):
    tt = x_ref.shape[1]

    @pl.when(pl.program_id(1) == 0)
    def _():
        c_ref[...] = jnp.zeros_like(c_ref)

    row, col = _tri_masks(tt)
    tri = jnp.where(col <= row, 1.0, 0.0).astype(BF16)
    f = _dot_exact_lhs(tri, x_ref[0]) + c_ref[0:1, :]
    c_ref[...] = jnp.broadcast_to(f[tt - 1:tt, :], c_ref.shape)
    hi_ref[0], mid_ref[0], lo_ref[0] = _split3(f * LOG2E)


def _cumsum_time_split(x):
    B, T, L = x.shape
    tt = _row_tile(T, 256)
    spec = pl.BlockSpec((1, tt, L), lambda b, i: (b, i, 0))
    return pl.pallas_call(
        _cumsum_body,
        grid=(B, T // tt),
        in_specs=[spec],
        out_specs=[spec] * FOX_F_SPLIT,
        out_shape=[jax.ShapeDtypeStruct(x.shape, BF16)] * FOX_F_SPLIT,
        scratch_shapes=[pltpu.VMEM((8, L), F32)],
        compiler_params=_params(("parallel", "arbitrary"), 32),
        name="cumsum_time",
    )(x)


def _fox_attn_body(qt_ref, ka_ref, vt_ref, og_ref, o_ref, acc_ref, *, past, tk, t_real):
    qi = pl.program_id(2)
    tq = qt_ref.shape[3]
    t_out = o_ref.shape[0]
    first_q = past + qi * tq
    last_q = past + jnp.minimum(qi * tq + tq, t_real) - 1
    n_full = (first_q + 1) // tk
    n_all = last_q // tk + 1
    qt = qt_ref[0, 0]
    drow = lax.broadcasted_iota(jnp.int32, qt.shape, 0)
    rhs = []
    for h in range(2):
        top = jnp.where(drow // FOX_HD == h, qt, jnp.zeros_like(qt))
        aug = jnp.where((drow >= FOX_F_SPLIT * h) & (drow < FOX_F_SPLIT * (h + 1)), -1.0, 0.0).astype(BF16)
        rhs.append(jnp.concatenate([top, aug], axis=0))
    acc_ref[...] = jnp.zeros_like(acc_ref)
    krow = lax.broadcasted_iota(jnp.int32, (tk, tq), 0)
    qcol = lax.broadcasted_iota(jnp.int32, (tk, tq), 1)

    def block(ki, c, masked):
        ks = pl.multiple_of(ki * tk, tk)
        ka = ka_ref[0, 0, pl.ds(ks, tk), :]
        s = [jnp.dot(ka, rhs[h], preferred_element_type=F32) for h in range(2)]
        if masked:
            vis = ks + krow <= first_q + qcol
            s = [jnp.where(vis, s[h], FOX_NEG) for h in range(2)]
        m_new = [jnp.maximum(c[h][0], jnp.max(s[h], axis=0, keepdims=True)) for h in range(2)]
        alpha = [jnp.exp2(c[h][0] - m_new[h]) for h in range(2)]
        p = [jnp.exp2(s[h] - m_new[h]) for h in range(2)]
        l_new = [alpha[h] * c[h][1] + jnp.sum(p[h], axis=0, keepdims=True) for h in range(2)]
        vt = [vt_ref[0, 0, h * FOX_HD:(h + 1) * FOX_HD, pl.ds(ks, tk)] for h in range(2)]
        pv = [jnp.dot(vt[h], p[h].astype(BF16), preferred_element_type=F32) for h in range(2)]
        for h in range(2):
            acc_ref[h] = alpha[h] * acc_ref[h] + pv[h]
        return tuple((m_new[h], l_new[h]) for h in range(2))

    init = tuple((jnp.full((1, tq), FOX_NEG, F32), jnp.zeros((1, tq), F32)) for _ in range(2))
    c = lax.fori_loop(0, n_full, lambda ki, c: block(ki, c, False), init)
    c = lax.fori_loop(n_full, n_all, lambda ki, c: block(ki, c, True), c)
    o_t = jnp.concatenate([acc_ref[0] / c[0][1], acc_ref[1] / c[1][1]], axis=0)
    o_ref[...] = o_t.T[:t_out] * jax.nn.sigmoid(og_ref[...])


def _fox_attention(qb, kb_all, vb_all, f_split, z_og, B, T, past):
    G = GROUP_W
    pw = 2 * FOX_HD
    tq = max(_row_tile(T, FOX_Q_COLS), V7X_LANES)
    tqp = -(-T // tq) * tq
    nq = tqp // tq
    t_out = min(tq, T)
    tk_real = kb_all.shape[1]
    tk = FOX_K_ROWS
    tkp = -(-tk_real // tk) * tk
    qt = jnp.pad(qb.reshape(B, T, HEAD_PAIRS, pw).transpose(0, 2, 3, 1), ((0, 0), (0, 0), (0, 0), (0, tqp - T)))
    vt = jnp.pad(vb_all.reshape(B, tk_real, HEAD_PAIRS, pw).transpose(0, 2, 3, 1),
                 ((0, 0), (0, 0), (0, 0), (0, tkp - tk_real)))
    fsplit = jnp.stack([t[:, :, :FOX_HEADS] for t in f_split], axis=-1)
    fsplit = jnp.pad(fsplit.reshape(B, tk_real, HEAD_PAIRS, 2 * FOX_F_SPLIT),
                     ((0, 0), (0, 0), (0, 0), (0, pw - 2 * FOX_F_SPLIT)))
    ka = jnp.concatenate([kb_all.reshape(B, tk_real, HEAD_PAIRS, pw), fsplit], axis=-1).transpose(0, 2, 1, 3)
    ka = jnp.pad(ka, ((0, 0), (0, 0), (0, tkp - tk_real), (0, 0)))
    return pl.pallas_call(
        functools.partial(_fox_attn_body, past=past, tk=tk, t_real=T),
        grid=(B, HEAD_PAIRS, nq),
        in_specs=[
            pl.BlockSpec((1, 1, pw, tq), lambda b, p, i: (b, p, 0, i)),
            pl.BlockSpec((1, 1, tkp, 2 * pw), lambda b, p, i: (b, p, 0, 0)),
            pl.BlockSpec((1, 1, pw, tkp), lambda b, p, i: (b, p, 0, 0)),
            pl.BlockSpec((t_out, pw), lambda b, p, i: (b * nq + i, p)),
        ],
        out_specs=pl.BlockSpec((t_out, pw), lambda b, p, i: (b * nq + i, p)),
        out_shape=jax.ShapeDtypeStruct((B * T, G), F32),
        scratch_shapes=[pltpu.VMEM((2, FOX_HD, tq), F32)],
        compiler_params=_params(("parallel", "parallel", "arbitrary"), 40),
        name="fox_attn",
    )(qt, ka, vt, z_og)


HG_CHUNK = 64


def _hgrn_body(z_ref, lb_ref, s0_ref, ng_ref, o_ref, so_ref, st_ref):
    G = GROUP_W
    c = z_ref.shape[0]
    dk = G // HG_HEADS

    @pl.when(pl.program_id(1) == 0)
    def _():
        st_ref[...] = s0_ref[0]

    lb = lb_ref[...]
    f = lb + (1.0 - lb) * jax.nn.sigmoid(z_ref[:, G:2 * G])
    kx = 1.0 - f
    row, col = _tri_masks(c)
    tri = jnp.where(col <= row, 1.0, 0.0).astype(BF16)
    incl = col <= row
    gs = _dot_exact_lhs(tri, jnp.log(f))
    eg = jnp.exp(gs)
    qg_all = z_ref[:, 0:G] * eg
    kg_all = kx * jnp.exp(-gs)
    g_last = gs[c - 1:c, :]
    kd_all = kx * jnp.exp(g_last - gs)
    eg_last = jnp.exp(g_last)
    HS = range(HG_HEADS)
    sls = [slice(h * dk, (h + 1) * dk) for h in HS]
    vv = [z_ref[:, 2 * G + h * dk:2 * G + (h + 1) * dk] for h in HS]
    A = [jnp.where(incl, _dot_lo(qg_all[:, sl], kg_all[:, sl], _NT), 0.0) for sl in sls]
    st = [st_ref[h] for h in HS]
    o = [_dot_lo(qg_all[:, sls[h]], st[h], _NT) + _dot_lo(A[h], vv[h]) for h in HS]
    for h in HS:
        st_ref[h] = st[h] * eg_last[:, sls[h]] + _dot_lo(vv[h], kd_all[:, sls[h]], _TN)
    for h in HS:
        sl = sls[h]
        hg = z_ref[:, 3 * G + h * dk:3 * G + (h + 1) * dk]
        o_ref[:, sl] = _rms(o[h], ng_ref[:, sl]) * (hg * jax.nn.sigmoid(hg))

    @pl.when(pl.program_id(1) == pl.num_programs(1) - 1)
    def _():
        so_ref[0] = st_ref[...]


def _hgrn2(z_hg, lb, S0, B, T, P):
    G = GROUP_W
    c = min(HG_CHUNK, T)
    nc = T // c
    dk = G // HG_HEADS
    st_spec = pl.BlockSpec((1, HG_HEADS, dk, dk), lambda b, i: (b, 0, 0, 0))
    out, so = pl.pallas_call(
        _hgrn_body,
        grid=(B, nc),
        in_specs=[pl.BlockSpec((c, 4 * G), lambda b, i: (b * nc + i, 0)),
                  pl.BlockSpec((1, G), lambda b, i: (0, 0)), st_spec, pl.BlockSpec((1, G), lambda b, i: (0, 0))],
        out_specs=[pl.BlockSpec((c, G), lambda b, i: (b * nc + i, 0)), st_spec],
        out_shape=[jax.ShapeDtypeStruct((B * T, G), F32), jax.ShapeDtypeStruct(S0.shape, F32)],
        scratch_shapes=[pltpu.VMEM((HG_HEADS, dk, dk), F32)],
        compiler_params=_params(("parallel", "arbitrary"), 32),
        name="hgrn2",
    )(z_hg, lb.reshape(1, G), jnp.swapaxes(S0, -1, -2), P['hg_norm_g'].reshape(1, G))
    return out, jnp.swapaxes(so, -1, -2)


RW_CHUNK = 64
RW_SUB = 16
RW_INV_LANES = 1024

_NT = (((1,), (1,)), ((), ()))
_TN = (((0,), (0,)), ((), ()))
_NN = (((1,), (0,)), ((), ()))


def _split3(x):
    h1 = x.astype(BF16)
    r1 = x - h1.astype(F32)
    h2 = r1.astype(BF16)
    h3 = (r1 - h2.astype(F32)).astype(BF16)
    return h1, h2, h3


def _dot_lo(a, b, dims=_NN):
    return lax.dot_general(a.astype(BF16), b.astype(BF16), dims, preferred_element_type=F32)


def _dot_hi(a, b, dims=_NN):
    ah = a.astype(BF16)
    al = (a - ah.astype(F32)).astype(BF16)
    bh = b.astype(BF16)
    bl = (b - bh.astype(F32)).astype(BF16)
    d = functools.partial(lax.dot_general, dimension_numbers=dims, preferred_element_type=F32)
    return d(ah, bh) + (d(al, bh) + d(ah, bl))


def _dot_exact_rhs(a, b):
    h1, h2, h3 = _split3(a)
    d = functools.partial(jnp.dot, preferred_element_type=F32)
    return d(h1, b) + (d(h2, b) + d(h3, b))


def _dot_exact_lhs(a, b):
    h1, h2, h3 = _split3(b)
    d = functools.partial(jnp.dot, preferred_element_type=F32)
    return d(a, h1) + (d(a, h2) + d(a, h3))


def _softplus(x):
    return jnp.maximum(x, 0.0) + jnp.log1p(jnp.exp(-jnp.abs(x)))


def _rw_prep_body(z_ref, shift_ref, mu_ref, w0_ref, w2_ref, a0_ref, a2_ref, g2_ref, kk_ref, ka_ref, ones_ref,
                  r_ref, lw_ref, k_ref, v_ref, kap_ref, bet_ref, g_ref, prev_ref):
    G = GROUP_W

    @pl.when(pl.program_id(1) == 0)
    def _():
        prev_ref[0:1, :] = shift_ref[0]

    z = z_ref[...]
    tt = z.shape[0]
    row = lax.broadcasted_iota(jnp.int32, z.shape, 0)
    shifted = jnp.where(row == 0, prev_ref[0:1, :], pltpu.roll(z, 1, axis=0))
    prev_ref[0:1, :] = z[tt - 1:tt, :]
    zm = z + (shifted - z) * mu_ref[...]
    r, k, v = zm[:, 0:G], zm[:, G:2 * G], zm[:, 2 * G:3 * G]
    o = 3 * G
    wd = zm[:, o:o + RW_DECAY_LORA]
    ad = zm[:, o + RW_DECAY_LORA:o + RW_DECAY_LORA + RW_A_LORA]
    gd = zm[:, o + RW_DECAY_LORA + RW_A_LORA:]
    w = -_softplus(-(w0_ref[...] + _dot_lo(jnp.tanh(wd), w2_ref[...]))) - 0.5
    a = jax.nn.sigmoid(a0_ref[...] + _dot_lo(ad, a2_ref[...]))
    kk = k * kk_ref[...]
    ss = _dot_exact_rhs(kk * kk, ones_ref[...])
    kap = kk / jnp.maximum(jnp.sqrt(ss), 1e-12)
    r_ref[...] = r
    lw_ref[...] = -jnp.exp(w)
    k_ref[...] = k * (1.0 + (a - 1.0) * ka_ref[...])
    v_ref[...] = v
    kap_ref[...] = kap
    bet_ref[...] = kap * a
    g_ref[...] = _dot_lo(jax.nn.sigmoid(gd), g2_ref[...])


def _rw_prep(zr, shift, B, T, P):
    n, cols = zr.shape
    G = GROUP_W
    tt = _row_tile(T, 256)
    nt = T // tt
    ones_bd = jnp.kron(jnp.eye(RW_HEADS, dtype=F32), jnp.ones((RW_HD, RW_HD), F32)).astype(BF16)
    row = lambda x: x.reshape(1, -1)
    full = lambda shape: pl.BlockSpec(shape, lambda b, i: (0,) * len(shape))
    tile = pl.BlockSpec((tt, G), lambda b, i: (b * nt + i, 0))
    return pl.pallas_call(
        _rw_prep_body,
        grid=(B, nt),
        in_specs=[
            pl.BlockSpec((tt, cols), lambda b, i: (b * nt + i, 0)),
            pl.BlockSpec((1, 1, cols), lambda b, i: (b, 0, 0)),
            full((1, cols)), full((1, G)), full((RW_DECAY_LORA, G)), full((1, G)), full((RW_A_LORA, G)),
            full((RW_GATE_LORA, G)), full((1, G)), full((1, G)), full((G, G)),
        ],
        out_specs=[tile] * 7,
        out_shape=[jax.ShapeDtypeStruct((n, G), F32)] * 7,
        scratch_shapes=[pltpu.VMEM((8, cols), F32)],
        compiler_params=_params(("parallel", "arbitrary"), 40),
        name="rwkv_prep",
    )(zr, shift.reshape(B, 1, cols), row(P['rw_mu']), row(P['rw_w0']), P['rw_w2'].astype(BF16), row(P['rw_a0']),
      P['rw_a2'].astype(BF16), P['rw_g2'].astype(BF16), row(P['rw_kk']), row(P['rw_ka']), ones_bd)


def _rw_scaled(lw, kap, bet, tri):
    cs = _dot_exact_lhs(tri, lw)
    return cs, kap * jnp.exp(cs - lw), bet * jnp.exp(-cs)


def _tri_masks(c):
    row = lax.broadcasted_iota(jnp.int32, (c, c), 0)
    col = lax.broadcasted_iota(jnp.int32, (c, c), 1)
    return row, col


def _rw_ldiag_body(lw_ref, kap_ref, bet_ref, o_ref):
    c = lw_ref.shape[0]
    row, col = _tri_masks(c)
    tri = jnp.where(col <= row, 1.0, 0.0).astype(BF16)
    _, kk_all, bt_all = _rw_scaled(lw_ref[...], kap_ref[...], bet_ref[...], tri)
    srow, scol = _tri_masks(RW_SUB)
    for h in range(RW_HEADS):
        sl = slice(h * RW_HD, (h + 1) * RW_HD)
        L = _dot_hi(kk_all[:, sl], bt_all[:, sl], _NT)
        for b in range(c // RW_SUB):
            rs = slice(b * RW_SUB, (b + 1) * RW_SUB)
            o_ref[rs, h * RW_SUB:(h + 1) * RW_SUB] = jnp.where(scol < srow, L[rs, rs], 0.0)


def _rw_inv_body(l_ref, t_ref):
    n = RW_SUB
    one = jnp.ones(l_ref.shape[2:], F32)
    zero = jnp.zeros(l_ref.shape[2:], F32)
    for t in range(n):
        for s in range(n):
            if s > t:
                t_ref[t, s] = zero
            elif s == t:
                t_ref[t, s] = one
            else:
                acc = l_ref[t, s]
                for j in range(s + 1, t):
                    acc = acc + l_ref[t, j] * t_ref[j, s]
                t_ref[t, s] = -acc


def _rw_main_body(r_ref, lw_ref, k_ref, v_ref, kap_ref, bet_ref, g_ref, td_ref, h0_ref, rk_ref, lng_ref, lnb_ref,
                  o_ref, hout_ref, h_ref):
    ci = pl.program_id(1)
    c = r_ref.shape[0]
    nb = c // RW_SUB

    @pl.when(ci == 0)
    def _():
        h_ref[...] = h0_ref[0]

    row, col = _tri_masks(c)
    tri = jnp.where(col <= row, 1.0, 0.0).astype(BF16)
    strict = col < row
    incl = col <= row
    lw = lw_ref[...]
    cs, kk_all, bt_all = _rw_scaled(lw, kap_ref[...], bet_ref[...], tri)
    gi = jnp.exp(-cs)
    gg = jnp.exp(cs)
    kt_all = k_ref[...] * gi
    rt_all = r_ref[...] * gg
    bonus_all = r_ref[...] * k_ref[...] * rk_ref[...]
    hrow = lax.broadcasted_iota(jnp.int32, (RW_HD, RW_HD), 0)
    hcol = lax.broadcasted_iota(jnp.int32, (RW_HD, RW_HD), 1)
    HS = range(RW_HEADS)
    sls = [slice(h * RW_HD, (h + 1) * RW_HD) for h in HS]
    Kk = [kk_all[:, sl] for sl in sls]
    Bt = [bt_all[:, sl] for sl in sls]
    Kt = [kt_all[:, sl] for sl in sls]
    Rt = [rt_all[:, sl] for sl in sls]
    vv = [v_ref[:, sl] for sl in sls]
    Lm = [jnp.where(strict, _dot_hi(Kk[h], Bt[h], _NT), 0.0) for h in HS]
    A1 = [jnp.where(strict, _dot_hi(Kk[h], Kt[h], _NT), 0.0) for h in HS]
    A4 = [jnp.where(incl, _dot_lo(Rt[h], Bt[h], _NT), 0.0) for h in HS]
    A3 = [jnp.where(incl, _dot_lo(Rt[h], Kt[h], _NT), 0.0) for h in HS]
    X = [jnp.concatenate([Kk[h], _dot_hi(A1[h], vv[h])], axis=1) for h in HS]
    zs = [[] for _ in HS]
    for b in range(nb):
        rs = slice(b * RW_SUB, (b + 1) * RW_SUB)
        rhs = [X[h][rs] for h in HS]
        if b:
            rhs = [rhs[h] - _dot_hi(Lm[h][rs, 0:b * RW_SUB], jnp.concatenate(zs[h], axis=0)) for h in HS]
        for h in HS:
            zs[h].append(_dot_hi(td_ref[rs, h * RW_SUB:(h + 1) * RW_SUB], rhs[h]))
    Z = [jnp.concatenate(zs[h], axis=0) if nb > 1 else zs[h][0] for h in HS]
    A4Z = [_dot_lo(A4[h], Z[h]) for h in HS]
    Rhat = [Rt[h] - A4Z[h][:, :RW_HD] for h in HS]
    Yhat = [_dot_lo(A3[h], vv[h]) - A4Z[h][:, RW_HD:] for h in HS]
    gC = [gg[c - 1:c, sl] for sl in sls]
    MN = [_dot_hi(Bt[h] * gC[h], Z[h], _TN) for h in HS]
    Mp = [jnp.where(hrow == hcol, gC[h], 0.0) - MN[h][:, :RW_HD] for h in HS]
    Np = [_dot_hi(Kt[h] * gC[h], vv[h], _TN) - MN[h][:, RW_HD:] for h in HS]
    H0 = [h_ref[h] for h in HS]
    ys = [_dot_lo(Rhat[h], H0[h]) + Yhat[h] for h in HS]
    for h in HS:
        h_ref[h] = _dot_hi(Mp[h], H0[h]) + Np[h]
    for h in HS:
        sl, y = sls[h], ys[h]
        mu = jnp.mean(y, axis=-1, keepdims=True)
        var = jnp.mean(jnp.square(y - mu), axis=-1, keepdims=True)
        yn = (y - mu) * lax.rsqrt(var + RW_LN_EPS) * lng_ref[:, sl] + lnb_ref[:, sl]
        yn = yn + jnp.sum(bonus_all[:, sl], axis=-1, keepdims=True) * vv[h]
        o_ref[:, sl] = yn * g_ref[:, sl]

    @pl.when(ci == pl.num_programs(1) - 1)
    def _():
        hout_ref[0] = h_ref[...]


def _rwkv7(zr, shift, S0, B, T, P):
    G = GROUP_W
    n = B * T
    r, lw, k, v, kap, bet, g = _rw_prep(zr, shift, B, T, P)
    c = min(RW_CHUNK, T)
    nc = T // c
    nb = c // RW_SUB
    tile = pl.BlockSpec((c, G), lambda b, i: (b * nc + i, 0))
    ld = pl.pallas_call(
        _rw_ldiag_body,
        grid=(B, nc),
        in_specs=[tile] * 3,
        out_specs=pl.BlockSpec((c, RW_HEADS * RW_SUB), lambda b, i: (b * nc + i, 0)),
        out_shape=jax.ShapeDtypeStruct((n, RW_HEADS * RW_SUB), F32),
        compiler_params=_params(("parallel", "parallel"), 32),
        name="rwkv_ldiag",
    )(lw, kap, bet)
    ni = n // RW_SUB * RW_HEADS
    lt = ld.reshape(n // RW_SUB, RW_SUB, RW_HEADS, RW_SUB).transpose(1, 3, 0, 2).reshape(RW_SUB, RW_SUB, ni)
    nip = -(-ni // RW_INV_LANES) * RW_INV_LANES
    lt = jnp.pad(lt, ((0, 0), (0, 0), (0, nip - ni))).reshape(RW_SUB, RW_SUB, nip // V7X_LANES, V7X_LANES)
    inv_spec = pl.BlockSpec((RW_SUB, RW_SUB, RW_INV_LANES // V7X_LANES, V7X_LANES), lambda i: (0, 0, i, 0))
    tt = pl.pallas_call(
        _rw_inv_body,
        grid=(nip // RW_INV_LANES,),
        in_specs=[inv_spec],
        out_specs=inv_spec,
        out_shape=jax.ShapeDtypeStruct(lt.shape, F32),
        compiler_params=_params(("parallel",), 32),
        name="rwkv_inv",
    )(lt)
    td = tt.reshape(RW_SUB, RW_SUB, nip)[:, :, :ni].reshape(RW_SUB, RW_SUB, n // RW_SUB, RW_HEADS)
    td = td.transpose(2, 0, 3, 1).reshape(n, RW_HEADS * RW_SUB)
    h0 = jnp.swapaxes(S0, -1, -2)
    prow = lambda x: pl.BlockSpec((1, G), lambda b, i: (0, 0))
    st_spec = pl.BlockSpec((1, RW_HEADS, RW_HD, RW_HD), lambda b, i: (b, 0, 0, 0))
    out, hl = pl.pallas_call(
        _rw_main_body,
        grid=(B, nc),
        in_specs=[tile] * 7 + [pl.BlockSpec((c, RW_HEADS * RW_SUB), lambda b, i: (b * nc + i, 0)), st_spec,
                               prow(0), prow(0), prow(0)],
        out_specs=[tile, st_spec],
        out_shape=[jax.ShapeDtypeStruct((n, G), F32), jax.ShapeDtypeStruct(S0.shape, F32)],
        scratch_shapes=[pltpu.VMEM((RW_HEADS, RW_HD, RW_HD), F32)],
        compiler_params=_params(("parallel", "arbitrary"), 32),
        name="rwkv_main",
    )(r, lw, k, v, kap, bet, g, td, h0, P['rw_rk'].reshape(1, G), P['rw_ln_g'].reshape(1, G),
      P['rw_ln_b'].reshape(1, G))
    return out, zr.reshape(B, T, -1)[:, -1], jnp.swapaxes(hl, -1, -2)


def _even_mixer(x2, B, T, g, st, P):
    conv_buf, lru_h, k_past, v_past, lf_past = st
    G = GROUP_W
    z_rg, z_qkv, z_og, z_fl = _norm_matmul(x2, g, P['e_w_in'], (2 * G, 3 * G, G, V7X_LANES))
    rnn_out, conv_new, h_last = _lru(z_rg, conv_buf, lru_h, B, T, P)
    qb, kn, kb, v, vb, lf = _fox_prep(z_qkv, z_fl, P)
    past = k_past.shape[1]
    lf_all = lf.reshape(B, T, V7X_LANES)
    kb_all, vb_all = kb.reshape(B, T, G), vb.reshape(B, T, G)
    if past:
        lf_all = jnp.concatenate([jnp.pad(lf_past, ((0, 0), (0, 0), (0, V7X_LANES - FOX_HEADS))), lf_all], axis=1)
        kb_all = jnp.concatenate([k_past.reshape(B, past, G).astype(BF16), kb_all], axis=1)
        vb_all = jnp.concatenate([v_past.reshape(B, past, G).astype(BF16), vb_all], axis=1)
    f_split = _cumsum_time_split(lf_all)
    fox_out = _fox_attention(qb, kb_all, vb_all, f_split, z_og, B, T, past)
    x2 = _out_proj(x2, rnn_out, fox_out, P['e_w_out'])
    heads = lambda t: t.reshape(B, T, FOX_HEADS, FOX_HD)
    return x2, (conv_new, h_last, heads(kn), heads(v), lf.reshape(B, T, V7X_LANES)[..., :FOX_HEADS])


def _odd_mixer(x2, B, T, g, st, lb, P):
    S_hg, shift, S_rw = st
    G = GROUP_W
    z_hg, z_rw = _norm_matmul(x2, g, P['o_w_in'], (4 * G, P['o_w_in'].shape[1] - 4 * G))
    hg_out, S_hg_new = _hgrn2(z_hg, lb, S_hg, B, T, P)
    rw_out, shift_new, S_rw_new = _rwkv7(z_rw, shift, S_rw, B, T, P)
    x2 = _out_proj(x2, hg_out, rw_out, P['o_w_out'])
    return x2, (S_hg_new, shift_new, S_rw_new)


def _trunk(x, states, W):
    lru_conv, lru_h, fox_k, fox_v, fox_lf, hg_S, rw_shift, rw_S = states
    B, T, D = x.shape
    depth = W['norm_g'].shape[0]
    sm = jax.nn.softmax(W['hg_lb_logits'], axis=0)
    lower_bounds = jnp.cumsum(sm, axis=0) - sm[0]
    x2 = x.reshape(B * T, D)
    even_new, odd_new = [], []
    for layer in range(depth):
        g = W['norm_g'][layer]
        x2 = _ffn(x2, g[0], W['ffn_w_in'][layer][0], W['ffn_w_out'][layer][0])
        if layer % 2 == 0:
            e = layer // 2
            P = {n: W[n][e] for n in ('e_w_in', 'e_w_out', 'lru_conv_w', 'lru_conv_b', 'lru_wa', 'lru_ba', 'lru_wx',
                                      'lru_bx', 'lru_lambda', 'fox_q_gain', 'fox_k_gain', 'fox_f_bias')}
            x2, new = _even_mixer(x2, B, T, g[1], (lru_conv[e], lru_h[e], fox_k[e], fox_v[e], fox_lf[e]), P)
            even_new.append(new)
        else:
            o = layer // 2
            P = {n: W[n][o] for n in ('o_w_in', 'o_w_out', 'hg_norm_g', 'rw_mu', 'rw_w0', 'rw_w2', 'rw_a0', 'rw_a2',
                                      'rw_g2', 'rw_kk', 'rw_ka', 'rw_rk', 'rw_ln_g', 'rw_ln_b')}
            x2, new = _odd_mixer(x2, B, T, g[1], (hg_S[o], rw_shift[o], rw_S[o]), lower_bounds[layer], P)
            odd_new.append(new)
        x2 = _ffn(x2, g[2], W['ffn_w_in'][layer][1], W['ffn_w_out'][layer][1])
    ev = [jnp.stack([n[j] for n in even_new]) for j in range(5)]
    od = [jnp.stack([n[j] for n in odd_new]) for j in range(3)]
    return x2.reshape(B, T, D), (ev[0], ev[1], ev[2], ev[3], ev[4], od[0], od[1], od[2])


def kernel(x_prompt, x_sample, state_lru_conv, state_lru_h, cache_fox_k, cache_fox_v, cache_fox_logf,
           state_hgrn_S, state_rwkv_shift, state_rwkv_S, norm_g, ffn_w_in, ffn_w_out, e_w_in, e_w_out,
           lru_conv_w, lru_conv_b, lru_wa, lru_ba, lru_wx, lru_bx, lru_lambda, fox_q_gain, fox_k_gain,
           fox_f_bias, o_w_in, o_w_out, hg_lb_logits, hg_norm_g, rw_mu, rw_w0, rw_w2, rw_a0, rw_a2, rw_g2,
           rw_kk, rw_ka, rw_rk, rw_ln_g, rw_ln_b):
    n_even, n_odd = e_w_in.shape[0], o_w_in.shape[0]
    W = dict(norm_g=norm_g, ffn_w_in=ffn_w_in.astype(BF16), ffn_w_out=ffn_w_out.astype(BF16),
             e_w_in=_pad_cols(e_w_in.astype(BF16)), e_w_out=e_w_out.astype(BF16),
             lru_conv_w=lru_conv_w, lru_conv_b=lru_conv_b, lru_wa=lru_wa, lru_ba=lru_ba, lru_wx=lru_wx,
             lru_bx=lru_bx, lru_lambda=lru_lambda, fox_q_gain=fox_q_gain, fox_k_gain=fox_k_gain,
             fox_f_bias=fox_f_bias, o_w_in=o_w_in.astype(BF16), o_w_out=o_w_out.astype(BF16),
             hg_lb_logits=hg_lb_logits, hg_norm_g=hg_norm_g, rw_mu=rw_mu, rw_w0=rw_w0, rw_w2=rw_w2, rw_a0=rw_a0,
             rw_a2=rw_a2, rw_g2=rw_g2, rw_kk=rw_kk, rw_ka=rw_ka, rw_rk=rw_rk, rw_ln_g=rw_ln_g, rw_ln_b=rw_ln_b)
    nb = x_prompt.shape[0]
    dt = x_prompt.dtype
    prompt_states = (jnp.zeros((n_even, nb, CONV_W - 1, GROUP_W), dt),
                     jnp.zeros((n_even, nb, GROUP_W), dt),
                     jnp.zeros((n_even, nb, 0, FOX_HEADS, FOX_HD), dt),
                     jnp.zeros((n_even, nb, 0, FOX_HEADS, FOX_HD), dt),
                     jnp.zeros((n_even, nb, 0, FOX_HEADS), dt),
                     jnp.zeros((n_odd, nb, HG_HEADS, GROUP_W // HG_HEADS, GROUP_W // HG_HEADS), dt),
                     jnp.zeros((n_odd, nb, rw_mu.shape[1]), dt),
                     jnp.zeros((n_odd, nb, RW_HEADS, RW_HD, RW_HD), dt))
    sample_states = (state_lru_conv, state_lru_h, cache_fox_k, cache_fox_v, cache_fox_logf,
                     state_hgrn_S, state_rwkv_shift, state_rwkv_S)
    y_prompt, p_new = _trunk(x_prompt, prompt_states, W)
    y_sample, s_new = _trunk(x_sample, sample_states, W)
    lru_conv_p, lru_h_p, fox_k_p, fox_v_p, fox_logf_p, hgrn_S_p, rwkv_shift_p, rwkv_S_p = p_new
    lru_conv_s, lru_h_s, fox_k_s, fox_v_s, fox_logf_s, hgrn_S_s, rwkv_shift_s, rwkv_S_s = s_new
    return (y_prompt, y_sample, lru_conv_p, lru_conv_s, lru_h_p, lru_h_s, fox_k_p, fox_k_s, fox_v_p, fox_v_s,
            fox_logf_p, fox_logf_s, hgrn_S_p, hgrn_S_s, rwkv_shift_p, rwkv_shift_s, rwkv_S_p, rwkv_S_s)
```

```python
import functools

import jax
import jax.numpy as jnp
from jax import lax
from jax.experimental import pallas as pl
from jax.experimental.pallas import tpu as pltpu

F32 = jnp.float32
BF16 = jnp.bfloat16

NORM_EPS = 1e-6
GROUP_W = 512
LRU_BLOCKS = 8
CONV_W = 4
LRU_C = 8.0
FOX_HEADS = 8
FOX_HD = 64
FOX_BLOCK = 128
HG_HEADS = 4
CHUNK = 64
RW_HEADS = 8
RW_HD = 64
RW_DECAY_LORA = 64
RW_A_LORA = 64
RW_GATE_LORA = 128
RW_LN_EPS = 64e-5

V7X_LANES = 128
FFN_COL_TILE = 1408


def _row_tile(n, want):
    t = min(n, want)
    while n % t:
        t //= 2
    return t


def _params(sem, vmem_mib):
    return pltpu.CompilerParams(dimension_semantics=sem, vmem_limit_bytes=vmem_mib << 20)


def _pad_cols(w):
    pad = -w.shape[-1] % V7X_LANES
    return jnp.pad(w, [(0, 0)] * (w.ndim - 1) + [(0, pad)])


def _rms(x, g):
    return x * lax.rsqrt(jnp.mean(x * x, axis=-1, keepdims=True) + NORM_EPS) * g


def _ffn_body(x_ref, g_ref, wg_ref, wu_ref, wo_ref, o_ref, h_ref, acc_ref):
    j = pl.program_id(1)

    @pl.when(j == 0)
    def _():
        h_ref[...] = _rms(x_ref[...], g_ref[...]).astype(BF16)
        acc_ref[...] = jnp.zeros_like(acc_ref)

    h = h_ref[...]
    gate = jnp.dot(h, wg_ref[...], preferred_element_type=F32)
    up = jnp.dot(h, wu_ref[...], preferred_element_type=F32)
    act = (gate * jax.nn.sigmoid(gate) * up).astype(BF16)
    acc_ref[...] += jnp.dot(act, wo_ref[...], preferred_element_type=F32)

    @pl.when(j == pl.num_programs(1) - 1)
    def _():
        o_ref[...] = x_ref[...] + 0.5 * acc_ref[...]


def _ffn(x, g, w_in, w_out):
    n, d = x.shape
    f = w_out.shape[0]
    tm = _row_tile(n, 512)
    tf = FFN_COL_TILE
    nf = f // tf
    return pl.pallas_call(
        _ffn_body,
        grid=(n // tm, nf),
        in_specs=[
            pl.BlockSpec((tm, d), lambda i, j: (i, 0)),
            pl.BlockSpec((1, d), lambda i, j: (0, 0)),
            pl.BlockSpec((d, tf), lambda i, j: (0, j)),
            pl.BlockSpec((d, tf), lambda i, j: (0, nf + j)),
            pl.BlockSpec((tf, d), lambda i, j: (j, 0)),
        ],
        out_specs=pl.BlockSpec((tm, d), lambda i, j: (i, 0)),
        out_shape=jax.ShapeDtypeStruct((n, d), F32),
        scratch_shapes=[pltpu.VMEM((tm, d), BF16), pltpu.VMEM((tm, d), F32)],
        compiler_params=_params(("parallel", "arbitrary"), 40),
        name="ffn",
    )(x, g.reshape(1, d), w_in, w_in, w_out)


def _norm_matmul_body(x_ref, g_ref, w_ref, *o_refs):
    h = _rms(x_ref[...], g_ref[...]).astype(BF16)
    z = jnp.dot(h, w_ref[...], preferred_element_type=F32)
    start = 0
    for o_ref in o_refs:
        width = o_ref.shape[1]
        o_ref[...] = z[:, start:start + width]
        start += width


def _norm_matmul(x, g, w, widths):
    n, d = x.shape
    c = w.shape[1]
    assert sum(widths) == c and all(wd % V7X_LANES == 0 for wd in widths)
    tm = _row_tile(n, 256)
    return pl.pallas_call(
        _norm_matmul_body,
        grid=(n // tm,),
        in_specs=[
            pl.BlockSpec((tm, d), lambda i: (i, 0)),
            pl.BlockSpec((1, d), lambda i: (0, 0)),
            pl.BlockSpec((d, c), lambda i: (0, 0)),
        ],
        out_specs=[pl.BlockSpec((tm, wd), lambda i: (i, 0)) for wd in widths],
        out_shape=[jax.ShapeDtypeStruct((n, wd), F32) for wd in widths],
        compiler_params=_params(("parallel",), 48),
        name="norm_matmul",
    )(x, g.reshape(1, d), w)


def _out_proj_body(x_ref, a_ref, b_ref, wa_ref, wb_ref, o_ref):
    acc = jnp.dot(a_ref[...].astype(BF16), wa_ref[...], preferred_element_type=F32)
    acc += jnp.dot(b_ref[...].astype(BF16), wb_ref[...], preferred_element_type=F32)
    o_ref[...] = x_ref[...] + acc


def _out_proj(x, a, b, w):
    n, d = x.shape
    ga, gb = a.shape[1], b.shape[1]
    tm = _row_tile(n, 512)
    return pl.pallas_call(
        _out_proj_body,
        grid=(n // tm,),
        in_specs=[
            pl.BlockSpec((tm, d), lambda i: (i, 0)),
            pl.BlockSpec((tm, ga), lambda i: (i, 0)),
            pl.BlockSpec((tm, gb), lambda i: (i, 0)),
            pl.BlockSpec((ga, d), lambda i: (0, 0)),
            pl.BlockSpec((gb, d), lambda i: (0, 0)),
        ],
        out_specs=pl.BlockSpec((tm, d), lambda i: (i, 0)),
        out_shape=jax.ShapeDtypeStruct((n, d), F32),
        compiler_params=_params(("parallel",), 32),
        name="out_proj",
    )(x, a, b, w[:ga], w[ga:])


LRU_ROWS = 256
CONV_PAD = 8


def _expm1(x):
    series = x * (1.0 + x * (1 / 2 + x * (1 / 6 + x * (1 / 24 + x * (1 / 120 + x * (1 / 720 + x * (1 / 5040 + x * (1 / 40320))))))))
    return jnp.where(jnp.abs(x) < 0.25, series, jnp.exp(x) - 1.0)


def _shift_rows(x, s, fill):
    row = lax.broadcasted_iota(jnp.int32, x.shape, 0)
    return jnp.where(row >= s, pltpu.roll(x, s, axis=0), fill)


def _lru_body(z_ref, buf_ref, h0_ref, cw_ref, cb_ref, wa_ref, ba_ref, wx_ref, bx_ref, lam_ref,
              o_ref, bufo_ref, ho_ref, x_ref, hc_ref):
    G = GROUP_W
    tt = z_ref.shape[0]

    @pl.when(pl.program_id(1) == 0)
    def _():
        x_ref[0:CONV_PAD, :] = buf_ref[0]
        hc_ref[...] = jnp.broadcast_to(h0_ref[0], hc_ref.shape)

    x_ref[CONV_PAD:CONV_PAD + tt, :] = z_ref[:, 0:G]
    xc = cb_ref[...]
    for j in range(CONV_W):
        lo = CONV_PAD - (CONV_W - 1) + j
        xc = xc + x_ref[lo:lo + tt, :] * cw_ref[j:j + 1, :]
    hist = x_ref[tt:tt + CONV_PAD, :]
    x_ref[0:CONV_PAD, :] = hist
    bufo_ref[0] = hist

    xb = xc.astype(BF16)
    r = jax.nn.sigmoid(jnp.dot(xb, wa_ref[...], preferred_element_type=F32) + ba_ref[...])
    ig = jax.nn.sigmoid(jnp.dot(xb, wx_ref[...], preferred_element_type=F32) + bx_ref[...])
    log_a = (-LRU_C * _softplus(-lam_ref[...])) * r
    a = jnp.exp(log_a)
    b = jnp.sqrt(-_expm1(2.0 * log_a)) * (ig * xc)
    s = 1
    while s < tt:
        b = a * _shift_rows(b, s, 0.0) + b
        a = a * _shift_rows(a, s, 1.0)
        s *= 2
    h = a * hc_ref[0:1, :] + b
    hc_ref[...] = jnp.broadcast_to(h[tt - 1:tt, :], hc_ref.shape)
    ho_ref[0] = h[tt - 1:tt, :]
    o_ref[...] = jax.nn.gelu(z_ref[:, G:2 * G]) * h


def _block_diag_dense(w):
    nb, bs, _ = w.shape
    eye = jnp.eye(nb, dtype=w.dtype)
    return (eye[:, None, :, None] * w[:, :, None, :]).reshape(nb * bs, nb * bs)


def _lru(z_rg, conv_buf, h0, B, T, P):
    G = GROUP_W
    n = B * T
    tt = _row_tile(T, LRU_ROWS)
    nt = T // tt
    buf = jnp.pad(conv_buf, ((0, 0), (CONV_PAD - (CONV_W - 1), 0), (0, 0)))
    cw = jnp.pad(P['lru_conv_w'], ((0, CONV_PAD - CONV_W), (0, 0)))
    row = lambda x: x.reshape(1, G)
    full = lambda shape: pl.BlockSpec(shape, lambda b, i: (0,) * len(shape))
    out, bufo, ho = pl.pallas_call(
        _lru_body,
        grid=(B, nt),
        in_specs=[
            pl.BlockSpec((tt, 2 * G), lambda b, i: (b * nt + i, 0)),
            pl.BlockSpec((1, CONV_PAD, G), lambda b, i: (b, 0, 0)),
            pl.BlockSpec((1, 1, G), lambda b, i: (b, 0, 0)),
            full((CONV_PAD, G)), full((1, G)), full((G, G)), full((1, G)), full((G, G)), full((1, G)), full((1, G)),
        ],
        out_specs=[
            pl.BlockSpec((tt, G), lambda b, i: (b * nt + i, 0)),
            pl.BlockSpec((1, CONV_PAD, G), lambda b, i: (b, 0, 0)),
            pl.BlockSpec((1, 1, G), lambda b, i: (b, 0, 0)),
        ],
        out_shape=[jax.ShapeDtypeStruct((n, G), F32), jax.ShapeDtypeStruct((B, CONV_PAD, G), F32),
                   jax.ShapeDtypeStruct((B, 1, G), F32)],
        scratch_shapes=[pltpu.VMEM((tt + CONV_PAD, G), F32), pltpu.VMEM((8, G), F32)],
        compiler_params=_params(("parallel", "arbitrary"), 32),
        name="lru",
    )(z_rg, buf, h0.reshape(B, 1, G), cw, row(P['lru_conv_b']), _block_diag_dense(P['lru_wa']).astype(BF16),
      row(P['lru_ba']), _block_diag_dense(P['lru_wx']).astype(BF16), row(P['lru_bx']), row(P['lru_lambda']))
    return out, bufo[:, CONV_PAD - (CONV_W - 1):], ho.reshape(B, G)


FOX_Q_COLS = 512
FOX_K_ROWS = 512
FOX_F_SPLIT = 3
FOX_NEG = -1e30
LOG2E = 1.4426950408889634
HEAD_PAIRS = FOX_HEADS // 2


def _fox_prep_body(z_ref, fl_ref, qg_ref, kg_ref, fb_ref, ones_ref, q_ref, k_ref, kb_ref, v_ref, vb_ref, lf_ref):
    G = GROUP_W
    q, k, v = z_ref[:, 0:G], z_ref[:, G:2 * G], z_ref[:, 2 * G:3 * G]
    inv = 1.0 / FOX_HD
    qn = q * lax.rsqrt(_dot_exact_rhs(q * q, ones_ref[...]) * inv + NORM_EPS) * qg_ref[...]
    kn = k * lax.rsqrt(_dot_exact_rhs(k * k, ones_ref[...]) * inv + NORM_EPS) * kg_ref[...]
    q_ref[...] = (qn * (LOG2E * FOX_HD ** -0.5)).astype(BF16)
    k_ref[...] = kn
    kb_ref[...] = kn.astype(BF16)
    v_ref[...] = v
    vb_ref[...] = v.astype(BF16)
    x = fl_ref[...] + fb_ref[...]
    lf_ref[...] = -_softplus(-x)


def _fox_prep(z_qkv, z_fl, P):
    n = z_qkv.shape[0]
    G = GROUP_W
    tt = _row_tile(n, 256)
    ones_bd = jnp.kron(jnp.eye(FOX_HEADS, dtype=F32), jnp.ones((FOX_HD, FOX_HD), F32)).astype(BF16)
    fb = jnp.pad(P['fox_f_bias'], (0, V7X_LANES - FOX_HEADS)).reshape(1, V7X_LANES)
    tile = lambda w: pl.BlockSpec((tt, w), lambda i: (i, 0))
    full = lambda shape: pl.BlockSpec(shape, lambda i: (0,) * len(shape))
    return pl.pallas_call(
        _fox_prep_body,
        grid=(n // tt,),
        in_specs=[tile(3 * G), tile(V7X_LANES), full((1, G)), full((1, G)), full((1, V7X_LANES)), full((G, G))],
        out_specs=[tile(G)] * 5 + [tile(V7X_LANES)],
        out_shape=[jax.ShapeDtypeStruct((n, G), BF16), jax.ShapeDtypeStruct((n, G), F32),
                   jax.ShapeDtypeStruct((n, G), BF16), jax.ShapeDtypeStruct((n, G), F32),
                   jax.ShapeDtypeStruct((n, G), BF16), jax.ShapeDtypeStruct((n, V7X_LANES), F32)],
        compiler_params=_params(("parallel",), 32),
        name="fox_prep",
    )(z_qkv, z_fl, jnp.tile(P['fox_q_gain'], FOX_HEADS).reshape(1, G),
      jnp.tile(P['fox_k_gain'], FOX_HEADS).reshape(1, G), fb, ones_bd)


def _cumsum_body(x_ref, hi_ref, mid_ref, lo_ref, c_ref):
    tt = x_ref.shape[1]

    @pl.when(pl.program_id(1) == 0)
    def _():
        c_ref[...] = jnp.zeros_like(c_ref)

    row, col = _tri_masks(tt)
    tri = jnp.where(col <= row, 1.0, 0.0).astype(BF16)
    f = _dot_exact_lhs(tri, x_ref[0]) + c_ref[0:1, :]
    c_ref[...] = jnp.broadcast_to(f[tt - 1:tt, :], c_ref.shape)
    hi_ref[0], mid_ref[0], lo_ref[0] = _split3(f * LOG2E)


def _cumsum_time_split(x):
    B, T, L = x.shape
    tt = _row_tile(T, 256)
    spec = pl.BlockSpec((1, tt, L), lambda b, i: (b, i, 0))
    return pl.pallas_call(
        _cumsum_body,
        grid=(B, T // tt),
        in_specs=[spec],
        out_specs=[spec] * FOX_F_SPLIT,
        out_shape=[jax.ShapeDtypeStruct(x.shape, BF16)] * FOX_F_SPLIT,
        scratch_shapes=[pltpu.VMEM((8, L), F32)],
        compiler_params=_params(("parallel", "arbitrary"), 32),
        name="cumsum_time",
    )(x)


def _fox_attn_body(qt_ref, ka_ref, vt_ref, og_ref, o_ref, acc_ref, *, past, tk, t_real):
    qi = pl.program_id(2)
    tq = qt_ref.shape[3]
    t_out = o_ref.shape[0]
    first_q = past + qi * tq
    last_q = past + jnp.minimum(qi * tq + tq, t_real) - 1
    n_full = (first_q + 1) // tk
    n_all = last_q // tk + 1
    qt = qt_ref[0, 0]
    drow = lax.broadcasted_iota(jnp.int32, qt.shape, 0)
    rhs = []
    for h in range(2):
        top = jnp.where(drow // FOX_HD == h, qt, jnp.zeros_like(qt))
        aug = jnp.where((drow >= FOX_F_SPLIT * h) & (drow < FOX_F_SPLIT * (h + 1)), -1.0, 0.0).astype(BF16)
        rhs.append(jnp.concatenate([top, aug], axis=0))
    acc_ref[...] = jnp.zeros_like(acc_ref)
    krow = lax.broadcasted_iota(jnp.int32, (tk, tq), 0)
    qcol = lax.broadcasted_iota(jnp.int32, (tk, tq), 1)

    def block(ki, c, masked):
        ks = pl.multiple_of(ki * tk, tk)
        ka = ka_ref[0, 0, pl.ds(ks, tk), :]
        s = [jnp.dot(ka, rhs[h], preferred_element_type=F32) for h in range(2)]
        if masked:
            vis = ks + krow <= first_q + qcol
            s = [jnp.where(vis, s[h], FOX_NEG) for h in range(2)]
        m_new = [jnp.maximum(c[h][0], jnp.max(s[h], axis=0, keepdims=True)) for h in range(2)]
        alpha = [jnp.exp2(c[h][0] - m_new[h]) for h in range(2)]
        p = [jnp.exp2(s[h] - m_new[h]) for h in range(2)]
        l_new = [alpha[h] * c[h][1] + jnp.sum(p[h], axis=0, keepdims=True) for h in range(2)]
        vt = [vt_ref[0, 0, h * FOX_HD:(h + 1) * FOX_HD, pl.ds(ks, tk)] for h in range(2)]
        pv = [jnp.dot(vt[h], p[h].astype(BF16), preferred_element_type=F32) for h in range(2)]
        for h in range(2):
            acc_ref[h] = alpha[h] * acc_ref[h] + pv[h]
        return tuple((m_new[h], l_new[h]) for h in range(2))

    init = tuple((jnp.full((1, tq), FOX_NEG, F32), jnp.zeros((1, tq), F32)) for _ in range(2))
    c = lax.fori_loop(0, n_full, lambda ki, c: block(ki, c, False), init)
    c = lax.fori_loop(n_full, n_all, lambda ki, c: block(ki, c, True), c)
    o_t = jnp.concatenate([acc_ref[0] / c[0][1], acc_ref[1] / c[1][1]], axis=0)
    o_ref[...] = o_t.T[:t_out] * jax.nn.sigmoid(og_ref[...])


def _fox_attention(qb, kb_all, vb_all, f_split, z_og, B, T, past):
    G = GROUP_W
    pw = 2 * FOX_HD
    tq = max(_row_tile(T, FOX_Q_COLS), V7X_LANES)
    tqp = -(-T // tq) * tq
    nq = tqp // tq
    t_out = min(tq, T)
    tk_real = kb_all.shape[1]
    tk = FOX_K_ROWS
    tkp = -(-tk_real // tk) * tk
    qt = jnp.pad(qb.reshape(B, T, HEAD_PAIRS, pw).transpose(0, 2, 3, 1), ((0, 0), (0, 0), (0, 0), (0, tqp - T)))
    vt = jnp.pad(vb_all.reshape(B, tk_real, HEAD_PAIRS, pw).transpose(0, 2, 3, 1),
                 ((0, 0), (0, 0), (0, 0), (0, tkp - tk_real)))
    fsplit = jnp.stack([t[:, :, :FOX_HEADS] for t in f_split], axis=-1)
    fsplit = jnp.pad(fsplit.reshape(B, tk_real, HEAD_PAIRS, 2 * FOX_F_SPLIT),
                     ((0, 0), (0, 0), (0, 0), (0, pw - 2 * FOX_F_SPLIT)))
    ka = jnp.concatenate([kb_all.reshape(B, tk_real, HEAD_PAIRS, pw), fsplit], axis=-1).transpose(0, 2, 1, 3)
    ka = jnp.pad(ka, ((0, 0), (0, 0), (0, tkp - tk_real), (0, 0)))
    return pl.pallas_call(
        functools.partial(_fox_attn_body, past=past, tk=tk, t_real=T),
        grid=(B, HEAD_PAIRS, nq),
        in_specs=[
            pl.BlockSpec((1, 1, pw, tq), lambda b, p, i: (b, p, 0, i)),
            pl.BlockSpec((1, 1, tkp, 2 * pw), lambda b, p, i: (b, p, 0, 0)),
            pl.BlockSpec((1, 1, pw, tkp), lambda b, p, i: (b, p, 0, 0)),
            pl.BlockSpec((t_out, pw), lambda b, p, i: (b * nq + i, p)),
        ],
        out_specs=pl.BlockSpec((t_out, pw), lambda b, p, i: (b * nq + i, p)),
        out_shape=jax.ShapeDtypeStruct((B * T, G), F32),
        scratch_shapes=[pltpu.VMEM((2, FOX_HD, tq), F32)],
        compiler_params=_params(("parallel", "parallel", "arbitrary"), 40),
        name="fox_attn",
    )(qt, ka, vt, z_og)


HG_CHUNK = 64


def _hgrn_body(z_ref, lb_ref, s0_ref, ng_ref, o_ref, so_ref, st_ref):
    G = GROUP_W
    c = z_ref.shape[0]
    dk = G // HG_HEADS

    @pl.when(pl.program_id(1) == 0)
    def _():
        st_ref[...] = s0_ref[0]

    lb = lb_ref[...]
    f = lb + (1.0 - lb) * jax.nn.sigmoid(z_ref[:, G:2 * G])
    kx = 1.0 - f
    row, col = _tri_masks(c)
    tri = jnp.where(col <= row, 1.0, 0.0).astype(BF16)
    incl = col <= row
    gs = _dot_exact_lhs(tri, jnp.log(f))
    eg = jnp.exp(gs)
    qg_all = z_ref[:, 0:G] * eg
    kg_all = kx * jnp.exp(-gs)
    g_last = gs[c - 1:c, :]
    kd_all = kx * jnp.exp(g_last - gs)
    eg_last = jnp.exp(g_last)
    HS = range(HG_HEADS)
    sls = [slice(h * dk, (h + 1) * dk) for h in HS]
    vv = [z_ref[:, 2 * G + h * dk:2 * G + (h + 1) * dk] for h in HS]
    A = [jnp.where(incl, _dot_lo(qg_all[:, sl], kg_all[:, sl], _NT), 0.0) for sl in sls]
    st = [st_ref[h] for h in HS]
    o = [_dot_lo(qg_all[:, sls[h]], st[h], _NT) + _dot_lo(A[h], vv[h]) for h in HS]
    for h in HS:
        st_ref[h] = st[h] * eg_last[:, sls[h]] + _dot_lo(vv[h], kd_all[:, sls[h]], _TN)
    for h in HS:
        sl = sls[h]
        hg = z_ref[:, 3 * G + h * dk:3 * G + (h + 1) * dk]
        o_ref[:, sl] = _rms(o[h], ng_ref[:, sl]) * (hg * jax.nn.sigmoid(hg))

    @pl.when(pl.program_id(1) == pl.num_programs(1) - 1)
    def _():
        so_ref[0] = st_ref[...]


def _hgrn2(z_hg, lb, S0, B, T, P):
    G = GROUP_W
    c = min(HG_CHUNK, T)
    nc = T // c
    dk = G // HG_HEADS
    st_spec = pl.BlockSpec((1, HG_HEADS, dk, dk), lambda b, i: (b, 0, 0, 0))
    out, so = pl.pallas_call(
        _hgrn_body,
        grid=(B, nc),
        in_specs=[pl.BlockSpec((c, 4 * G), lambda b, i: (b * nc + i, 0)),
                  pl.BlockSpec((1, G), lambda b, i: (0, 0)), st_spec, pl.BlockSpec((1, G), lambda b, i: (0, 0))],
        out_specs=[pl.BlockSpec((c, G), lambda b, i: (b * nc + i, 0)), st_spec],
        out_shape=[jax.ShapeDtypeStruct((B * T, G), F32), jax.ShapeDtypeStruct(S0.shape, F32)],
        scratch_shapes=[pltpu.VMEM((HG_HEADS, dk, dk), F32)],
        compiler_params=_params(("parallel", "arbitrary"), 32),
        name="hgrn2",
    )(z_hg, lb.reshape(1, G), jnp.swapaxes(S0, -1, -2), P['hg_norm_g'].reshape(1, G))
    return out, jnp.swapaxes(so, -1, -2)


RW_CHUNK = 64
RW_SUB = 16
RW_INV_LANES = 1024

_NT = (((1,), (1,)), ((), ()))
_TN = (((0,), (0,)), ((), ()))
_NN = (((1,), (0,)), ((), ()))


def _split3(x):
    h1 = x.astype(BF16)
    r1 = x - h1.astype(F32)
    h2 = r1.astype(BF16)
    h3 = (r1 - h2.astype(F32)).astype(BF16)
    return h1, h2, h3


def _dot_lo(a, b, dims=_NN):
    return lax.dot_general(a.astype(BF16), b.astype(BF16), dims, preferred_element_type=F32)


def _dot_hi(a, b, dims=_NN):
    ah = a.astype(BF16)
    al = (a - ah.astype(F32)).astype(BF16)
    bh = b.astype(BF16)
    bl = (b - bh.astype(F32)).astype(BF16)
    d = functools.partial(lax.dot_general, dimension_numbers=dims, preferred_element_type=F32)
    return d(ah, bh) + (d(al, bh) + d(ah, bl))


def _dot_exact_rhs(a, b):
    h1, h2, h3 = _split3(a)
    d = functools.partial(jnp.dot, preferred_element_type=F32)
    return d(h1, b) + (d(h2, b) + d(h3, b))


def _dot_exact_lhs(a, b):
    h1, h2, h3 = _split3(b)
    d = functools.partial(jnp.dot, preferred_element_type=F32)
    return d(a, h1) + (d(a, h2) + d(a, h3))


def _softplus(x):
    return jnp.maximum(x, 0.0) + jnp.log1p(jnp.exp(-jnp.abs(x)))


def _rw_prep_body(z_ref, shift_ref, mu_ref, w0_ref, w2_ref, a0_ref, a2_ref, g2_ref, kk_ref, ka_ref, ones_ref,
                  r_ref, lw_ref, k_ref, v_ref, kap_ref, bet_ref, g_ref, prev_ref):
    G = GROUP_W

    @pl.when(pl.program_id(1) == 0)
    def _():
        prev_ref[0:1, :] = shift_ref[0]

    z = z_ref[...]
    tt = z.shape[0]
    row = lax.broadcasted_iota(jnp.int32, z.shape, 0)
    shifted = jnp.where(row == 0, prev_ref[0:1, :], pltpu.roll(z, 1, axis=0))
    prev_ref[0:1, :] = z[tt - 1:tt, :]
    zm = z + (shifted - z) * mu_ref[...]
    r, k, v = zm[:, 0:G], zm[:, G:2 * G], zm[:, 2 * G:3 * G]
    o = 3 * G
    wd = zm[:, o:o + RW_DECAY_LORA]
    ad = zm[:, o + RW_DECAY_LORA:o + RW_DECAY_LORA + RW_A_LORA]
    gd = zm[:, o + RW_DECAY_LORA + RW_A_LORA:]
    w = -_softplus(-(w0_ref[...] + _dot_lo(jnp.tanh(wd), w2_ref[...]))) - 0.5
    a = jax.nn.sigmoid(a0_ref[...] + _dot_lo(ad, a2_ref[...]))
    kk = k * kk_ref[...]
    ss = _dot_exact_rhs(kk * kk, ones_ref[...])
    kap = kk / jnp.maximum(jnp.sqrt(ss), 1e-12)
    r_ref[...] = r
    lw_ref[...] = -jnp.exp(w)
    k_ref[...] = k * (1.0 + (a - 1.0) * ka_ref[...])
    v_ref[...] = v
    kap_ref[...] = kap
    bet_ref[...] = kap * a
    g_ref[...] = _dot_lo(jax.nn.sigmoid(gd), g2_ref[...])


def _rw_prep(zr, shift, B, T, P):
    n, cols = zr.shape
    G = GROUP_W
    tt = _row_tile(T, 256)
    nt = T // tt
    ones_bd = jnp.kron(jnp.eye(RW_HEADS, dtype=F32), jnp.ones((RW_HD, RW_HD), F32)).astype(BF16)
    row = lambda x: x.reshape(1, -1)
    full = lambda shape: pl.BlockSpec(shape, lambda b, i: (0,) * len(shape))
    tile = pl.BlockSpec((tt, G), lambda b, i: (b * nt + i, 0))
    return pl.pallas_call(
        _rw_prep_body,
        grid=(B, nt),
        in_specs=[
            pl.BlockSpec((tt, cols), lambda b, i: (b * nt + i, 0)),
            pl.BlockSpec((1, 1, cols), lambda b, i: (b, 0, 0)),
            full((1, cols)), full((1, G)), full((RW_DECAY_LORA, G)), full((1, G)), full((RW_A_LORA, G)),
            full((RW_GATE_LORA, G)), full((1, G)), full((1, G)), full((G, G)),
        ],
        out_specs=[tile] * 7,
        out_shape=[jax.ShapeDtypeStruct((n, G), F32)] * 7,
        scratch_shapes=[pltpu.VMEM((8, cols), F32)],
        compiler_params=_params(("parallel", "arbitrary"), 40),
        name="rwkv_prep",
    )(zr, shift.reshape(B, 1, cols), row(P['rw_mu']), row(P['rw_w0']), P['rw_w2'].astype(BF16), row(P['rw_a0']),
      P['rw_a2'].astype(BF16), P['rw_g2'].astype(BF16), row(P['rw_kk']), row(P['rw_ka']), ones_bd)


def _rw_scaled(lw, kap, bet, tri):
    cs = _dot_exact_lhs(tri, lw)
    return cs, kap * jnp.exp(cs - lw), bet * jnp.exp(-cs)


def _tri_masks(c):
    row = lax.broadcasted_iota(jnp.int32, (c, c), 0)
    col = lax.broadcasted_iota(jnp.int32, (c, c), 1)
    return row, col


def _rw_ldiag_body(lw_ref, kap_ref, bet_ref, o_ref):
    c = lw_ref.shape[0]
    row, col = _tri_masks(c)
    tri = jnp.where(col <= row, 1.0, 0.0).astype(BF16)
    _, kk_all, bt_all = _rw_scaled(lw_ref[...], kap_ref[...], bet_ref[...], tri)
    srow, scol = _tri_masks(RW_SUB)
    for h in range(RW_HEADS):
        sl = slice(h * RW_HD, (h + 1) * RW_HD)
        L = _dot_lo(kk_all[:, sl], bt_all[:, sl], _NT)
        for b in range(c // RW_SUB):
            rs = slice(b * RW_SUB, (b + 1) * RW_SUB)
            o_ref[rs, h * RW_SUB:(h + 1) * RW_SUB] = jnp.where(scol < srow, L[rs, rs], 0.0)


def _rw_inv_body(l_ref, t_ref):
    n = RW_SUB
    one = jnp.ones(l_ref.shape[2:], F32)
    zero = jnp.zeros(l_ref.shape[2:], F32)
    for t in range(n):
        for s in range(n):
            if s > t:
                t_ref[t, s] = zero
            elif s == t:
                t_ref[t, s] = one
            else:
                acc = l_ref[t, s]
                for j in range(s + 1, t):
                    acc = acc + l_ref[t, j] * t_ref[j, s]
                t_ref[t, s] = -acc


def _rw_main_body(r_ref, lw_ref, k_ref, v_ref, kap_ref, bet_ref, g_ref, td_ref, h0_ref, rk_ref, lng_ref, lnb_ref,
                  o_ref, hout_ref, h_ref):
    ci = pl.program_id(1)
    c = r_ref.shape[0]
    nb = c // RW_SUB

    @pl.when(ci == 0)
    def _():
        h_ref[...] = h0_ref[0]

    row, col = _tri_masks(c)
    tri = jnp.where(col <= row, 1.0, 0.0).astype(BF16)
    strict = col < row
    incl = col <= row
    lw = lw_ref[...]
    cs, kk_all, bt_all = _rw_scaled(lw, kap_ref[...], bet_ref[...], tri)
    gi = jnp.exp(-cs)
    gg = jnp.exp(cs)
    kt_all = k_ref[...] * gi
    rt_all = r_ref[...] * gg
    bonus_all = r_ref[...] * k_ref[...] * rk_ref[...]
    hrow = lax.broadcasted_iota(jnp.int32, (RW_HD, RW_HD), 0)
    hcol = lax.broadcasted_iota(jnp.int32, (RW_HD, RW_HD), 1)
    HS = range(RW_HEADS)
    sls = [slice(h * RW_HD, (h + 1) * RW_HD) for h in HS]
    Kk = [kk_all[:, sl] for sl in sls]
    Bt = [bt_all[:, sl] for sl in sls]
    Kt = [kt_all[:, sl] for sl in sls]
    Rt = [rt_all[:, sl] for sl in sls]
    vv = [v_ref[:, sl] for sl in sls]
    Lm = [jnp.where(strict, _dot_lo(Kk[h], Bt[h], _NT), 0.0) for h in HS]
    A1 = [jnp.where(strict, _dot_lo(Kk[h], Kt[h], _NT), 0.0) for h in HS]
    A4 = [jnp.where(incl, _dot_lo(Rt[h], Bt[h], _NT), 0.0) for h in HS]
    A3 = [jnp.where(incl, _dot_lo(Rt[h], Kt[h], _NT), 0.0) for h in HS]
    X = [jnp.concatenate([Kk[h], _dot_lo(A1[h], vv[h])], axis=1) for h in HS]
    zs = [[] for _ in HS]
    for b in range(nb):
        rs = slice(b * RW_SUB, (b + 1) * RW_SUB)
        rhs = [X[h][rs] for h in HS]
        if b:
            rhs = [rhs[h] - _dot_lo(Lm[h][rs, 0:b * RW_SUB], jnp.concatenate(zs[h], axis=0)) for h in HS]
        for h in HS:
            zs[h].append(_dot_lo(td_ref[rs, h * RW_SUB:(h + 1) * RW_SUB], rhs[h]))
    Z = [jnp.concatenate(zs[h], axis=0) if nb > 1 else zs[h][0] for h in HS]
    A4Z = [_dot_lo(A4[h], Z[h]) for h in HS]
    Rhat = [Rt[h] - A4Z[h][:, :RW_HD] for h in HS]
    Yhat = [_dot_lo(A3[h], vv[h]) - A4Z[h][:, RW_HD:] for h in HS]
    gC = [gg[c - 1:c, sl] for sl in sls]
    MN = [_dot_lo(Bt[h] * gC[h], Z[h], _TN) for h in HS]
    Mp = [jnp.where(hrow == hcol, gC[h], 0.0) - MN[h][:, :RW_HD] for h in HS]
    Np = [_dot_lo(Kt[h] * gC[h], vv[h], _TN) - MN[h][:, RW_HD:] for h in HS]
    H0 = [h_ref[h] for h in HS]
    ys = [_dot_lo(Rhat[h], H0[h]) + Yhat[h] for h in HS]
    for h in HS:
        h_ref[h] = _dot_hi(Mp[h], H0[h]) + Np[h]
    for h in HS:
        sl, y = sls[h], ys[h]
        mu = jnp.mean(y, axis=-1, keepdims=True)
        var = jnp.mean(jnp.square(y - mu), axis=-1, keepdims=True)
        yn = (y - mu) * lax.rsqrt(var + RW_LN_EPS) * lng_ref[:, sl] + lnb_ref[:, sl]
        yn = yn + jnp.sum(bonus_all[:, sl], axis=-1, keepdims=True) * vv[h]
        o_ref[:, sl] = yn * g_ref[:, sl]

    @pl.when(ci == pl.num_programs(1) - 1)
    def _():
        hout_ref[0] = h_ref[...]


def _rwkv7(zr, shift, S0, B, T, P):
    G = GROUP_W
    n = B * T
    r, lw, k, v, kap, bet, g = _rw_prep(zr, shift, B, T, P)
    c = min(RW_CHUNK, T)
    nc = T // c
    nb = c // RW_SUB
    tile = pl.BlockSpec((c, G), lambda b, i: (b * nc + i, 0))
    ld = pl.pallas_call(
        _rw_ldiag_body,
        grid=(B, nc),
        in_specs=[tile] * 3,
        out_specs=pl.BlockSpec((c, RW_HEADS * RW_SUB), lambda b, i: (b * nc + i, 0)),
        out_shape=jax.ShapeDtypeStruct((n, RW_HEADS * RW_SUB), F32),
        compiler_params=_params(("parallel", "parallel"), 32),
        name="rwkv_ldiag",
    )(lw, kap, bet)
    ni = n // RW_SUB * RW_HEADS
    lt = ld.reshape(n // RW_SUB, RW_SUB, RW_HEADS, RW_SUB).transpose(1, 3, 0, 2).reshape(RW_SUB, RW_SUB, ni)
    nip = -(-ni // RW_INV_LANES) * RW_INV_LANES
    lt = jnp.pad(lt, ((0, 0), (0, 0), (0, nip - ni))).reshape(RW_SUB, RW_SUB, nip // V7X_LANES, V7X_LANES)
    inv_spec = pl.BlockSpec((RW_SUB, RW_SUB, RW_INV_LANES // V7X_LANES, V7X_LANES), lambda i: (0, 0, i, 0))
    tt = pl.pallas_call(
        _rw_inv_body,
        grid=(nip // RW_INV_LANES,),
        in_specs=[inv_spec],
        out_specs=inv_spec,
        out_shape=jax.ShapeDtypeStruct(lt.shape, F32),
        compiler_params=_params(("parallel",), 32),
        name="rwkv_inv",
    )(lt)
    td = tt.reshape(RW_SUB, RW_SUB, nip)[:, :, :ni].reshape(RW_SUB, RW_SUB, n // RW_SUB, RW_HEADS)
    td = td.transpose(2, 0, 3, 1).reshape(n, RW_HEADS * RW_SUB)
    h0 = jnp.swapaxes(S0, -1, -2)
    prow = lambda x: pl.BlockSpec((1, G), lambda b, i: (0, 0))
    st_spec = pl.BlockSpec((1, RW_HEADS, RW_HD, RW_HD), lambda b, i: (b, 0, 0, 0))
    out, hl = pl.pallas_call(
        _rw_main_body,
        grid=(B, nc),
        in_specs=[tile] * 7 + [pl.BlockSpec((c, RW_HEADS * RW_SUB), lambda b, i: (b * nc + i, 0)), st_spec,
                               prow(0), prow(0), prow(0)],
        out_specs=[tile, st_spec],
        out_shape=[jax.ShapeDtypeStruct((n, G), F32), jax.ShapeDtypeStruct(S0.shape, F32)],
        scratch_shapes=[pltpu.VMEM((RW_HEADS, RW_HD, RW_HD), F32)],
        compiler_params=_params(("parallel", "arbitrary"), 32),
        name="rwkv_main",
    )(r, lw, k, v, kap, bet, g, td, h0, P['rw_rk'].reshape(1, G), P['rw_ln_g'].reshape(1, G),
      P['rw_ln_b'].reshape(1, G))
    return out, zr.reshape(B, T, -1)[:, -1], jnp.swapaxes(hl, -1, -2)


def _even_mixer(x2, B, T, g, st, P):
    conv_buf, lru_h, k_past, v_past, lf_past = st
    G = GROUP_W
    z_rg, z_qkv, z_og, z_fl = _norm_matmul(x2, g, P['e_w_in'], (2 * G, 3 * G, G, V7X_LANES))
    rnn_out, conv_new, h_last = _lru(z_rg, conv_buf, lru_h, B, T, P)
    qb, kn, kb, v, vb, lf = _fox_prep(z_qkv, z_fl, P)
    past = k_past.shape[1]
    lf_all = lf.reshape(B, T, V7X_LANES)
    kb_all, vb_all = kb.reshape(B, T, G), vb.reshape(B, T, G)
    if past:
        lf_all = jnp.concatenate([jnp.pad(lf_past, ((0, 0), (0, 0), (0, V7X_LANES - FOX_HEADS))), lf_all], axis=1)
        kb_all = jnp.concatenate([k_past.reshape(B, past, G).astype(BF16), kb_all], axis=1)
        vb_all = jnp.concatenate([v_past.reshape(B, past, G).astype(BF16), vb_all], axis=1)
    f_split = _cumsum_time_split(lf_all)
    fox_out = _fox_attention(qb, kb_all, vb_all, f_split, z_og, B, T, past)
    x2 = _out_proj(x2, rnn_out, fox_out, P['e_w_out'])
    heads = lambda t: t.reshape(B, T, FOX_HEADS, FOX_HD)
    return x2, (conv_new, h_last, heads(kn), heads(v), lf.reshape(B, T, V7X_LANES)[..., :FOX_HEADS])


def _odd_mixer(x2, B, T, g, st, lb, P):
    S_hg, shift, S_rw = st
    G = GROUP_W
    z_hg, z_rw = _norm_matmul(x2, g, P['o_w_in'], (4 * G, P['o_w_in'].shape[1] - 4 * G))
    hg_out, S_hg_new = _hgrn2(z_hg, lb, S_hg, B, T, P)
    rw_out, shift_new, S_rw_new = _rwkv7(z_rw, shift, S_rw, B, T, P)
    x2 = _out_proj(x2, hg_out, rw_out, P['o_w_out'])
    return x2, (S_hg_new, shift_new, S_rw_new)


def _trunk(x, states, W):
    lru_conv, lru_h, fox_k, fox_v, fox_lf, hg_S, rw_shift, rw_S = states
    B, T, D = x.shape
    depth = W['norm_g'].shape[0]
    sm = jax.nn.softmax(W['hg_lb_logits'], axis=0)
    lower_bounds = jnp.cumsum(sm, axis=0) - sm[0]
    x2 = x.reshape(B * T, D)
    even_new, odd_new = [], []
    for layer in range(depth):
        g = W['norm_g'][layer]
        x2 = _ffn(x2, g[0], W['ffn_w_in'][layer][0], W['ffn_w_out'][layer][0])
        if layer % 2 == 0:
            e = layer // 2
            P = {n: W[n][e] for n in ('e_w_in', 'e_w_out', 'lru_conv_w', 'lru_conv_b', 'lru_wa', 'lru_ba', 'lru_wx',
                                      'lru_bx', 'lru_lambda', 'fox_q_gain', 'fox_k_gain', 'fox_f_bias')}
            x2, new = _even_mixer(x2, B, T, g[1], (lru_conv[e], lru_h[e], fox_k[e], fox_v[e], fox_lf[e]), P)
            even_new.append(new)
        else:
            o = layer // 2
            P = {n: W[n][o] for n in ('o_w_in', 'o_w_out', 'hg_norm_g', 'rw_mu', 'rw_w0', 'rw_w2', 'rw_a0', 'rw_a2',
                                      'rw_g2', 'rw_kk', 'rw_ka', 'rw_rk', 'rw_ln_g', 'rw_ln_b')}
            x2, new = _odd_mixer(x2, B, T, g[1], (hg_S[o], rw_shift[o], rw_S[o]), lower_bounds[layer], P)
            odd_new.append(new)
        x2 = _ffn(x2, g[2], W['ffn_w_in'][layer][1], W['ffn_w_out'][layer][1])
    ev = [jnp.stack([n[j] for n in even_new]) for j in range(5)]
    od = [jnp.stack([n[j] for n in odd_new]) for j in range(3)]
    return x2.reshape(B, T, D), (ev[0], ev[1], ev[2], ev[3], ev[4], od[0], od[1], od[2])


def kernel(x_prompt, x_sample, state_lru_conv, state_lru_h, cache_fox_k, cache_fox_v, cache_fox_logf,
           state_hgrn_S, state_rwkv_shift, state_rwkv_S, norm_g, ffn_w_in, ffn_w_out, e_w_in, e_w_out,
           lru_conv_w, lru_conv_b, lru_wa, lru_ba, lru_wx, lru_bx, lru_lambda, fox_q_gain, fox_k_gain,
           fox_f_bias, o_w_in, o_w_out, hg_lb_logits, hg_norm_g, rw_mu, rw_w0, rw_w2, rw_a0, rw_a2, rw_g2,
           rw_kk, rw_ka, rw_rk, rw_ln_g, rw_ln_b):
    n_even, n_odd = e_w_in.shape[0], o_w_in.shape[0]
    W = dict(norm_g=norm_g, ffn_w_in=ffn_w_in.astype(BF16), ffn_w_out=ffn_w_out.astype(BF16),
             e_w_in=_pad_cols(e_w_in.astype(BF16)), e_w_out=e_w_out.astype(BF16),
             lru_conv_w=lru_conv_w, lru_conv_b=lru_conv_b, lru_wa=lru_wa, lru_ba=lru_ba, lru_wx=lru_wx,
             lru_bx=lru_bx, lru_lambda=lru_lambda, fox_q_gain=fox_q_gain, fox_k_gain=fox_k_gain,
             fox_f_bias=fox_f_bias, o_w_in=o_w_in.astype(BF16), o_w_out=o_w_out.astype(BF16),
             hg_lb_logits=hg_lb_logits, hg_norm_g=hg_norm_g, rw_mu=rw_mu, rw_w0=rw_w0, rw_w2=rw_w2, rw_a0=rw_a0,
             rw_a2=rw_a2, rw_g2=rw_g2, rw_kk=rw_kk, rw_ka=rw_ka, rw_rk=rw_rk, rw_ln_g=rw_ln_g, rw_ln_b=rw_ln_b)
    nb = x_prompt.shape[0]
    dt = x_prompt.dtype
    prompt_states = (jnp.zeros((n_even, nb, CONV_W - 1, GROUP_W), dt),
                     jnp.zeros((n_even, nb, GROUP_W), dt),
                     jnp.zeros((n_even, nb, 0, FOX_HEADS, FOX_HD), dt),
                     jnp.zeros((n_even, nb, 0, FOX_HEADS, FOX_HD), dt),
                     jnp.zeros((n_even, nb, 0, FOX_HEADS), dt),
                     jnp.zeros((n_odd, nb, HG_HEADS, GROUP_W // HG_HEADS, GROUP_W // HG_HEADS), dt),
                     jnp.zeros((n_odd, nb, rw_mu.shape[1]), dt),
                     jnp.zeros((n_odd, nb, RW_HEADS, RW_HD, RW_HD), dt))
    sample_states = (state_lru_conv, state_lru_h, cache_fox_k, cache_fox_v, cache_fox_logf,
                     state_hgrn_S, state_rwkv_shift, state_rwkv_S)
    y_prompt, p_new = _trunk(x_prompt, prompt_states, W)
    y_sample, s_new = _trunk(x_sample, sample_states, W)
    lru_conv_p, lru_h_p, fox_k_p, fox_v_p, fox_logf_p, hgrn_S_p, rwkv_shift_p, rwkv_S_p = p_new
    lru_conv_s, lru_h_s, fox_k_s, fox_v_s, fox_logf_s, hgrn_S_s, rwkv_shift_s, rwkv_S_s = s_new
    return (y_prompt, y_sample, lru_conv_p, lru_conv_s, lru_h_p, lru_h_s, fox_k_p, fox_k_s, fox_v_p, fox_v_s,
            fox_logf_p, fox_logf_s, hgrn_S_p, hgrn_S_s, rwkv_shift_p, rwkv_shift_s, rwkv_S_p, rwkv_S_s)
```

```python
import functools

import jax
import jax.numpy as jnp
from jax import lax
from jax.experimental import pallas as pl
from jax.experimental.pallas import tpu as pltpu

F32 = jnp.float32
BF16 = jnp.bfloat16

NORM_EPS = 1e-6
GROUP_W = 512
LRU_BLOCKS = 8
CONV_W = 4
LRU_C = 8.0
FOX_HEADS = 8
FOX_HD = 64
FOX_BLOCK = 128
HG_HEADS = 4
CHUNK = 64
RW_HEADS = 8
RW_HD = 64
RW_DECAY_LORA = 64
RW_A_LORA = 64
RW_GATE_LORA = 128
RW_LN_EPS = 64e-5

V7X_LANES = 128
FFN_COL_TILE = 1408


def _row_tile(n, want):
    t = min(n, want)
    while n % t:
        t //= 2
    return t


def _params(sem, vmem_mib):
    return pltpu.CompilerParams(dimension_semantics=sem, vmem_limit_bytes=vmem_mib << 20)


def _pad_cols(w):
    pad = -w.shape[-1] % V7X_LANES
    return jnp.pad(w, [(0, 0)] * (w.ndim - 1) + [(0, pad)])


def _rms(x, g):
    return x * lax.rsqrt(jnp.mean(x * x, axis=-1, keepdims=True) + NORM_EPS) * g


def _ffn_body(x_ref, g_ref, wg_ref, wu_ref, wo_ref, o_ref, h_ref, acc_ref):
    j = pl.program_id(1)

    @pl.when(j == 0)
    def _():
        h_ref[...] = _rms(x_ref[...], g_ref[...]).astype(BF16)
        acc_ref[...] = jnp.zeros_like(acc_ref)

    h = h_ref[...]
    gate = jnp.dot(h, wg_ref[...], preferred_element_type=F32)
    up = jnp.dot(h, wu_ref[...], preferred_element_type=F32)
    act = (gate * jax.nn.sigmoid(gate) * up).astype(BF16)
    acc_ref[...] += jnp.dot(act, wo_ref[...], preferred_element_type=F32)

    @pl.when(j == pl.num_programs(1) - 1)
    def _():
        o_ref[...] = x_ref[...] + 0.5 * acc_ref[...]


def _ffn(x, g, w_in, w_out):
    n, d = x.shape
    f = w_out.shape[0]
    tm = _row_tile(n, 512)
    tf = FFN_COL_TILE
    nf = f // tf
    return pl.pallas_call(
        _ffn_body,
        grid=(n // tm, nf),
        in_specs=[
            pl.BlockSpec((tm, d), lambda i, j: (i, 0)),
            pl.BlockSpec((1, d), lambda i, j: (0, 0)),
            pl.BlockSpec((d, tf), lambda i, j: (0, j)),
            pl.BlockSpec((d, tf), lambda i, j: (0, nf + j)),
            pl.BlockSpec((tf, d), lambda i, j: (j, 0)),
        ],
        out_specs=pl.BlockSpec((tm, d), lambda i, j: (i, 0)),
        out_shape=jax.ShapeDtypeStruct((n, d), F32),
        scratch_shapes=[pltpu.VMEM((tm, d), BF16), pltpu.VMEM((tm, d), F32)],
        compiler_params=_params(("parallel", "arbitrary"), 40),
        name="ffn",
    )(x, g.reshape(1, d), w_in, w_in, w_out)


def _norm_matmul_body(x_ref, g_ref, w_ref, *o_refs):
    h = _rms(x_ref[...], g_ref[...]).astype(BF16)
    z = jnp.dot(h, w_ref[...], preferred_element_type=F32)
    start = 0
    for o_ref in o_refs:
        width = o_ref.shape[1]
        o_ref[...] = z[:, start:start + width]
        start += width


def _norm_matmul(x, g, w, widths):
    n, d = x.shape
    c = w.shape[1]
    assert sum(widths) == c and all(wd % V7X_LANES == 0 for wd in widths)
    tm = _row_tile(n, 256)
    return pl.pallas_call(
        _norm_matmul_body,
        grid=(n // tm,),
        in_specs=[
            pl.BlockSpec((tm, d), lambda i: (i, 0)),
            pl.BlockSpec((1, d), lambda i: (0, 0)),
            pl.BlockSpec((d, c), lambda i: (0, 0)),
        ],
        out_specs=[pl.BlockSpec((tm, wd), lambda i: (i, 0)) for wd in widths],
        out_shape=[jax.ShapeDtypeStruct((n, wd), F32) for wd in widths],
        compiler_params=_params(("parallel",), 48),
        name="norm_matmul",
    )(x, g.reshape(1, d), w)


def _out_proj_body(x_ref, a_ref, b_ref, wa_ref, wb_ref, o_ref):
    acc = jnp.dot(a_ref[...].astype(BF16), wa_ref[...], preferred_element_type=F32)
    acc += jnp.dot(b_ref[...].astype(BF16), wb_ref[...], preferred_element_type=F32)
    o_ref[...] = x_ref[...] + acc


def _out_proj(x, a, b, w):
    n, d = x.shape
    ga, gb = a.shape[1], b.shape[1]
    tm = _row_tile(n, 512)
    return pl.pallas_call(
        _out_proj_body,
        grid=(n // tm,),
        in_specs=[
            pl.BlockSpec((tm, d), lambda i: (i, 0)),
            pl.BlockSpec((tm, ga), lambda i: (i, 0)),
            pl.BlockSpec((tm, gb), lambda i: (i, 0)),
            pl.BlockSpec((ga, d), lambda i: (0, 0)),
            pl.BlockSpec((gb, d), lambda i: (0, 0)),
        ],
        out_specs=pl.BlockSpec((tm, d), lambda i: (i, 0)),
        out_shape=jax.ShapeDtypeStruct((n, d), F32),
        compiler_params=_params(("parallel",), 32),
        name="out_proj",
    )(x, a, b, w[:ga], w[ga:])


LRU_ROWS = 256
CONV_PAD = 8


def _expm1(x):
    series = x * (1.0 + x * (1 / 2 + x * (1 / 6 + x * (1 / 24 + x * (1 / 120 + x * (1 / 720 + x * (1 / 5040 + x * (1 / 40320))))))))
    return jnp.where(jnp.abs(x) < 0.25, series, jnp.exp(x) - 1.0)


def _shift_rows(x, s, fill):
    row = lax.broadcasted_iota(jnp.int32, x.shape, 0)
    return jnp.where(row >= s, pltpu.roll(x, s, axis=0), fill)


def _lru_body(z_ref, buf_ref, h0_ref, cw_ref, cb_ref, wa_ref, ba_ref, wx_ref, bx_ref, lam_ref,
              o_ref, bufo_ref, ho_ref, x_ref, hc_ref):
    G = GROUP_W
    tt = z_ref.shape[0]

    @pl.when(pl.program_id(1) == 0)
    def _():
        x_ref[0:CONV_PAD, :] = buf_ref[0]
        hc_ref[...] = jnp.broadcast_to(h0_ref[0], hc_ref.shape)

    x_ref[CONV_PAD:CONV_PAD + tt, :] = z_ref[:, 0:G]
    xc = cb_ref[...]
    for j in range(CONV_W):
        lo = CONV_PAD - (CONV_W - 1) + j
        xc = xc + x_ref[lo:lo + tt, :] * cw_ref[j:j + 1, :]
    hist = x_ref[tt:tt + CONV_PAD, :]
    x_ref[0:CONV_PAD, :] = hist
    bufo_ref[0] = hist

    xb = xc.astype(BF16)
    r = jax.nn.sigmoid(jnp.dot(xb, wa_ref[...], preferred_element_type=F32) + ba_ref[...])
    ig = jax.nn.sigmoid(jnp.dot(xb, wx_ref[...], preferred_element_type=F32) + bx_ref[...])
    log_a = (-LRU_C * _softplus(-lam_ref[...])) * r
    a = jnp.exp(log_a)
    b = jnp.sqrt(-_expm1(2.0 * log_a)) * (ig * xc)
    s = 1
    while s < tt:
        b = a * _shift_rows(b, s, 0.0) + b
        a = a * _shift_rows(a, s, 1.0)
        s *= 2
    h = a * hc_ref[0:1, :] + b
    hc_ref[...] = jnp.broadcast_to(h[tt - 1:tt, :], hc_ref.shape)
    ho_ref[0] = h[tt - 1:tt, :]
    o_ref[...] = jax.nn.gelu(z_ref[:, G:2 * G]) * h


def _block_diag_dense(w):
    nb, bs, _ = w.shape
    eye = jnp.eye(nb, dtype=w.dtype)
    return (eye[:, None, :, None] * w[:, :, None, :]).reshape(nb * bs, nb * bs)


def _lru(z_rg, conv_buf, h0, B, T, P):
    G = GROUP_W
    n = B * T
    tt = _row_tile(T, LRU_ROWS)
    nt = T // tt
    buf = jnp.pad(conv_buf, ((0, 0), (CONV_PAD - (CONV_W - 1), 0), (0, 0)))
    cw = jnp.pad(P['lru_conv_w'], ((0, CONV_PAD - CONV_W), (0, 0)))
    row = lambda x: x.reshape(1, G)
    full = lambda shape: pl.BlockSpec(shape, lambda b, i: (0,) * len(shape))
    out, bufo, ho = pl.pallas_call(
        _lru_body,
        grid=(B, nt),
        in_specs=[
            pl.BlockSpec((tt, 2 * G), lambda b, i: (b * nt + i, 0)),
            pl.BlockSpec((1, CONV_PAD, G), lambda b, i: (b, 0, 0)),
            pl.BlockSpec((1, 1, G), lambda b, i: (b, 0, 0)),
            full((CONV_PAD, G)), full((1, G)), full((G, G)), full((1, G)), full((G, G)), full((1, G)), full((1, G)),
        ],
        out_specs=[
            pl.BlockSpec((tt, G), lambda b, i: (b * nt + i, 0)),
            pl.BlockSpec((1, CONV_PAD, G), lambda b, i: (b, 0, 0)),
            pl.BlockSpec((1, 1, G), lambda b, i: (b, 0, 0)),
        ],
        out_shape=[jax.ShapeDtypeStruct((n, G), F32), jax.ShapeDtypeStruct((B, CONV_PAD, G), F32),
                   jax.ShapeDtypeStruct((B, 1, G), F32)],
        scratch_shapes=[pltpu.VMEM((tt + CONV_PAD, G), F32), pltpu.VMEM((8, G), F32)],
        compiler_params=_params(("parallel", "arbitrary"), 32),
        name="lru",
    )(z_rg, buf, h0.reshape(B, 1, G), cw, row(P['lru_conv_b']), _block_diag_dense(P['lru_wa']).astype(BF16),
      row(P['lru_ba']), _block_diag_dense(P['lru_wx']).astype(BF16), row(P['lru_bx']), row(P['lru_lambda']))
    return out, bufo[:, CONV_PAD - (CONV_W - 1):], ho.reshape(B, G)


FOX_Q_COLS = 512
FOX_K_ROWS = 512
FOX_F_SPLIT = 3
FOX_NEG = -1e30
LOG2E = 1.4426950408889634
HEAD_PAIRS = FOX_HEADS // 2


def _fox_prep_body(z_ref, fl_ref, qg_ref, kg_ref, fb_ref, ones_ref, q_ref, k_ref, kb_ref, v_ref, vb_ref, lf_ref):
    G = GROUP_W
    q, k, v = z_ref[:, 0:G], z_ref[:, G:2 * G], z_ref[:, 2 * G:3 * G]
    inv = 1.0 / FOX_HD
    qn = q * lax.rsqrt(_dot_exact_rhs(q * q, ones_ref[...]) * inv + NORM_EPS) * qg_ref[...]
    kn = k * lax.rsqrt(_dot_exact_rhs(k * k, ones_ref[...]) * inv + NORM_EPS) * kg_ref[...]
    q_ref[...] = (qn * (LOG2E * FOX_HD ** -0.5)).astype(BF16)
    k_ref[...] = kn
    kb_ref[...] = kn.astype(BF16)
    v_ref[...] = v
    vb_ref[...] = v.astype(BF16)
    x = fl_ref[...] + fb_ref[...]
    lf_ref[...] = -_softplus(-x)


def _fox_prep(z_qkv, z_fl, P):
    n = z_qkv.shape[0]
    G = GROUP_W
    tt = _row_tile(n, 256)
    ones_bd = jnp.kron(jnp.eye(FOX_HEADS, dtype=F32), jnp.ones((FOX_HD, FOX_HD), F32)).astype(BF16)
    fb = jnp.pad(P['fox_f_bias'], (0, V7X_LANES - FOX_HEADS)).reshape(1, V7X_LANES)
    tile = lambda w: pl.BlockSpec((tt, w), lambda i: (i, 0))
    full = lambda shape: pl.BlockSpec(shape, lambda i: (0,) * len(shape))
    return pl.pallas_call(
        _fox_prep_body,
        grid=(n // tt,),
        in_specs=[tile(3 * G), tile(V7X_LANES), full((1, G)), full((1, G)), full((1, V7X_LANES)), full((G, G))],
        out_specs=[tile(G)] * 5 + [tile(V7X_LANES)],
        out_shape=[jax.ShapeDtypeStruct((n, G), BF16), jax.ShapeDtypeStruct((n, G), F32),
                   jax.ShapeDtypeStruct((n, G), BF16), jax.ShapeDtypeStruct((n, G), F32),
                   jax.ShapeDtypeStruct((n, G), BF16), jax.ShapeDtypeStruct((n, V7X_LANES), F32)],
        compiler_params=_params(("parallel",), 32),
        name="fox_prep",
    )(z_qkv, z_fl, jnp.tile(P['fox_q_gain'], FOX_HEADS).reshape(1, G),
      jnp.tile(P['fox_k_gain'], FOX_HEADS).reshape(1, G), fb, ones_bd)


def _cumsum_body(x_ref, hi_ref, mid_ref, lo_ref, c_ref):
    tt = x_ref.shape[1]

    @pl.when(pl.program_id(1) == 0)
    def _():
        c_ref[...] = jnp.zeros_like(c_ref)

    row, col = _tri_masks(tt)
    tri = jnp.where(col <= row, 1.0, 0.0).astype(BF16)
    f = _dot_exact_lhs(tri, x_ref[0]) + c_ref[0:1, :]
    c_ref[...] = jnp.broadcast_to(f[tt - 1:tt, :], c_ref.shape)
    hi_ref[0], mid_ref[0], lo_ref[0] = _split3(f * LOG2E)


def _cumsum_time_split(x):
    B, T, L = x.shape
    tt = _row_tile(T, 256)
    spec = pl.BlockSpec((1, tt, L), lambda b, i: (b, i, 0))
    return pl.pallas_call(
        _cumsum_body,
        grid=(B, T // tt),
        in_specs=[spec],
        out_specs=[spec] * FOX_F_SPLIT,
        out_shape=[jax.ShapeDtypeStruct(x.shape, BF16)] * FOX_F_SPLIT,
        scratch_shapes=[pltpu.VMEM((8, L), F32)],
        compiler_params=_params(("parallel", "arbitrary"), 32),
        name="cumsum_time",
    )(x)


def _fox_attn_body(qt_ref, ka_ref, vt_ref, og_ref, o_ref, acc_ref, *, past, tk, t_real):
    qi = pl.program_id(2)
    tq = qt_ref.shape[3]
    t_out = o_ref.shape[0]
    first_q = past + qi * tq
    last_q = past + jnp.minimum(qi * tq + tq, t_real) - 1
    n_full = (first_q + 1) // tk
    n_all = last_q // tk + 1
    qt = qt_ref[0, 0]
    drow = lax.broadcasted_iota(jnp.int32, qt.shape, 0)
    rhs = []
    for h in range(2):
        top = jnp.where(drow // FOX_HD == h, qt, jnp.zeros_like(qt))
        aug = jnp.where((drow >= FOX_F_SPLIT * h) & (drow < FOX_F_SPLIT * (h + 1)), -1.0, 0.0).astype(BF16)
        rhs.append(jnp.concatenate([top, aug], axis=0))
    acc_ref[...] = jnp.zeros_like(acc_ref)
    krow = lax.broadcasted_iota(jnp.int32, (tk, tq), 0)
    qcol = lax.broadcasted_iota(jnp.int32, (tk, tq), 1)

    def block(ki, c, masked):
        ks = pl.multiple_of(ki * tk, tk)
        ka = ka_ref[0, 0, pl.ds(ks, tk), :]
        s = [jnp.dot(ka, rhs[h], preferred_element_type=F32) for h in range(2)]
        if masked:
            vis = ks + krow <= first_q + qcol
            s = [jnp.where(vis, s[h], FOX_NEG) for h in range(2)]
        m_new = [jnp.maximum(c[h][0], jnp.max(s[h], axis=0, keepdims=True)) for h in range(2)]
        alpha = [jnp.exp2(c[h][0] - m_new[h]) for h in range(2)]
        p = [jnp.exp2(s[h] - m_new[h]) for h in range(2)]
        l_new = [alpha[h] * c[h][1] + jnp.sum(p[h], axis=0, keepdims=True) for h in range(2)]
        vt = [vt_ref[0, 0, h * FOX_HD:(h + 1) * FOX_HD, pl.ds(ks, tk)] for h in range(2)]
        pv = [jnp.dot(vt[h], p[h].astype(BF16), preferred_element_type=F32) for h in range(2)]
        for h in range(2):
            acc_ref[h] = alpha[h] * acc_ref[h] + pv[h]
        return tuple((m_new[h], l_new[h]) for h in range(2))

    init = tuple((jnp.full((1, tq), FOX_NEG, F32), jnp.zeros((1, tq), F32)) for _ in range(2))
    c = lax.fori_loop(0, n_full, lambda ki, c: block(ki, c, False), init)
    c = lax.fori_loop(n_full, n_all, lambda ki, c: block(ki, c, True), c)
    o_t = jnp.concatenate([acc_ref[0] / c[0][1], acc_ref[1] / c[1][1]], axis=0)
    o_ref[...] = o_t.T[:t_out] * jax.nn.sigmoid(og_ref[...])


def _fox_attention(qb, kb_all, vb_all, f_split, z_og, B, T, past):
    G = GROUP_W
    pw = 2 * FOX_HD
    tq = max(_row_tile(T, FOX_Q_COLS), V7X_LANES)
    tqp = -(-T // tq) * tq
    nq = tqp // tq
    t_out = min(tq, T)
    tk_real = kb_all.shape[1]
    tk = FOX_K_ROWS
    tkp = -(-tk_real // tk) * tk
    qt = jnp.pad(qb.reshape(B, T, HEAD_PAIRS, pw).transpose(0, 2, 3, 1), ((0, 0), (0, 0), (0, 0), (0, tqp - T)))
    vt = jnp.pad(vb_all.reshape(B, tk_real, HEAD_PAIRS, pw).transpose(0, 2, 3, 1),
                 ((0, 0), (0, 0), (0, 0), (0, tkp - tk_real)))
    fsplit = jnp.stack([t[:, :, :FOX_HEADS] for t in f_split], axis=-1)
    fsplit = jnp.pad(fsplit.reshape(B, tk_real, HEAD_PAIRS, 2 * FOX_F_SPLIT),
                     ((0, 0), (0, 0), (0, 0), (0, pw - 2 * FOX_F_SPLIT)))
    ka = jnp.concatenate([kb_all.reshape(B, tk_real, HEAD_PAIRS, pw), fsplit], axis=-1).transpose(0, 2, 1, 3)
    ka = jnp.pad(ka, ((0, 0), (0, 0), (0, tkp - tk_real), (0, 0)))
    return pl.pallas_call(
        functools.partial(_fox_attn_body, past=past, tk=tk, t_real=T),
        grid=(B, HEAD_PAIRS, nq),
        in_specs=[
            pl.BlockSpec((1, 1, pw, tq), lambda b, p, i: (b, p, 0, i)),
            pl.BlockSpec((1, 1, tkp, 2 * pw), lambda b, p, i: (b, p, 0, 0)),
            pl.BlockSpec((1, 1, pw, tkp), lambda b, p, i: (b, p, 0, 0)),
            pl.BlockSpec((t_out, pw), lambda b, p, i: (b * nq + i, p)),
        ],
        out_specs=pl.BlockSpec((t_out, pw), lambda b, p, i: (b * nq + i, p)),
        out_shape=jax.ShapeDtypeStruct((B * T, G), F32),
        scratch_shapes=[pltpu.VMEM((2, FOX_HD, tq), F32)],
        compiler_params=_params(("parallel", "parallel", "arbitrary"), 40),
        name="fox_attn",
    )(qt, ka, vt, z_og)


HG_CHUNK = 64
HG_STEP_CHUNKS = 4


def _hgrn_body(z_ref, lb_ref, s0_ref, ng_ref, o_ref, so_ref, st_ref, *, c):
    G = GROUP_W
    rows = z_ref.shape[0]
    nch = rows // c
    dk = G // HG_HEADS

    @pl.when(pl.program_id(1) == 0)
    def _():
        st_ref[...] = s0_ref[0]

    lb = lb_ref[...]
    f = lb + (1.0 - lb) * jax.nn.sigmoid(z_ref[:, G:2 * G])
    kx = 1.0 - f
    crow, ccol = _tri_masks(c)
    incl = ccol <= crow
    gs = _dot_exact_lhs(_chunk_tri(rows, c), jnp.log(f))
    qg_all = z_ref[:, 0:G] * jnp.exp(gs)
    kg_all = kx * jnp.exp(-gs)
    HS = range(HG_HEADS)
    units = [(cc, h) for cc in range(nch) for h in HS]
    US = range(len(units))
    rsl = [slice(cc * c, (cc + 1) * c) for cc, _ in units]
    lsl = [slice(h * dk, (h + 1) * dk) for _, h in units]
    g_last = [gs[(cc + 1) * c - 1:(cc + 1) * c, lsl[u]] for u, (cc, _) in enumerate(units)]
    vv = [z_ref[rsl[u], 2 * G + h * dk:2 * G + (h + 1) * dk] for u, (_, h) in enumerate(units)]
    A = [jnp.where(incl, _dot_lo(qg_all[rsl[u], lsl[u]], kg_all[rsl[u], lsl[u]], _NT), 0.0) for u in US]
    av = [_dot_lo(A[u], vv[u]) for u in US]
    kd = [kx[rsl[u], lsl[u]] * jnp.exp(g_last[u] - gs[rsl[u], lsl[u]]) for u in US]
    upd = [_dot_lo(vv[u], kd[u], _TN) for u in US]
    st = [st_ref[h] for h in HS]
    o = [None for _ in US]
    for cc in range(nch):
        for h in HS:
            u = cc * HG_HEADS + h
            o[u] = _dot_lo(qg_all[rsl[u], lsl[u]], st[h], _NT) + av[u]
        st = [st[h] * jnp.exp(g_last[cc * HG_HEADS + h]) + upd[cc * HG_HEADS + h] for h in HS]
    for h in HS:
        st_ref[h] = st[h]
    for u, (_, h) in enumerate(units):
        hg = z_ref[rsl[u], 3 * G + h * dk:3 * G + (h + 1) * dk]
        o_ref[rsl[u], lsl[u]] = _rms(o[u], ng_ref[:, lsl[u]]) * (hg * jax.nn.sigmoid(hg))

    @pl.when(pl.program_id(1) == pl.num_programs(1) - 1)
    def _():
        so_ref[0] = st_ref[...]


def _hgrn2(z_hg, lb, S0, B, T, P):
    G = GROUP_W
    c = min(HG_CHUNK, T)
    rows = _row_tile(T, c * HG_STEP_CHUNKS)
    nc = T // rows
    dk = G // HG_HEADS
    st_spec = pl.BlockSpec((1, HG_HEADS, dk, dk), lambda b, i: (b, 0, 0, 0))
    out, so = pl.pallas_call(
        functools.partial(_hgrn_body, c=c),
        grid=(B, nc),
        in_specs=[pl.BlockSpec((rows, 4 * G), lambda b, i: (b * nc + i, 0)),
                  pl.BlockSpec((1, G), lambda b, i: (0, 0)), st_spec, pl.BlockSpec((1, G), lambda b, i: (0, 0))],
        out_specs=[pl.BlockSpec((rows, G), lambda b, i: (b * nc + i, 0)), st_spec],
        out_shape=[jax.ShapeDtypeStruct((B * T, G), F32), jax.ShapeDtypeStruct(S0.shape, F32)],
        scratch_shapes=[pltpu.VMEM((HG_HEADS, dk, dk), F32)],
        compiler_params=_params(("parallel", "arbitrary"), 32),
        name="hgrn2",
    )(z_hg, lb.reshape(1, G), jnp.swapaxes(S0, -1, -2), P['hg_norm_g'].reshape(1, G))
    return out, jnp.swapaxes(so, -1, -2)


RW_CHUNK = 64
RW_SUB = 16
RW_LDIAG_CHUNKS = 4
RW_MAIN_CHUNKS = 2
RW_INV_LANES = 1024

_NT = (((1,), (1,)), ((), ()))
_TN = (((0,), (0,)), ((), ()))
_NN = (((1,), (0,)), ((), ()))


def _split3(x):
    h1 = x.astype(BF16)
    r1 = x - h1.astype(F32)
    h2 = r1.astype(BF16)
    h3 = (r1 - h2.astype(F32)).astype(BF16)
    return h1, h2, h3


def _dot_lo(a, b, dims=_NN):
    return lax.dot_general(a.astype(BF16), b.astype(BF16), dims, preferred_element_type=F32)


def _dot_hi(a, b, dims=_NN):
    ah = a.astype(BF16)
    al = (a - ah.astype(F32)).astype(BF16)
    bh = b.astype(BF16)
    bl = (b - bh.astype(F32)).astype(BF16)
    d = functools.partial(lax.dot_general, dimension_numbers=dims, preferred_element_type=F32)
    return d(ah, bh) + (d(al, bh) + d(ah, bl))


def _dot_exact_rhs(a, b):
    h1, h2, h3 = _split3(a)
    d = functools.partial(jnp.dot, preferred_element_type=F32)
    return d(h1, b) + (d(h2, b) + d(h3, b))


def _dot_exact_lhs(a, b):
    h1, h2, h3 = _split3(b)
    d = functools.partial(jnp.dot, preferred_element_type=F32)
    return d(a, h1) + (d(a, h2) + d(a, h3))


def _softplus(x):
    return jnp.maximum(x, 0.0) + jnp.log1p(jnp.exp(-jnp.abs(x)))


def _rw_prep_body(z_ref, shift_ref, mu_ref, w0_ref, w2_ref, a0_ref, a2_ref, g2_ref, kk_ref, ka_ref, ones_ref,
                  r_ref, lw_ref, k_ref, v_ref, kap_ref, bet_ref, g_ref, prev_ref):
    G = GROUP_W

    @pl.when(pl.program_id(1) == 0)
    def _():
        prev_ref[0:1, :] = shift_ref[0]

    z = z_ref[...]
    tt = z.shape[0]
    row = lax.broadcasted_iota(jnp.int32, z.shape, 0)
    shifted = jnp.where(row == 0, prev_ref[0:1, :], pltpu.roll(z, 1, axis=0))
    prev_ref[0:1, :] = z[tt - 1:tt, :]
    zm = z + (shifted - z) * mu_ref[...]
    r, k, v = zm[:, 0:G], zm[:, G:2 * G], zm[:, 2 * G:3 * G]
    o = 3 * G
    wd = zm[:, o:o + RW_DECAY_LORA]
    ad = zm[:, o + RW_DECAY_LORA:o + RW_DECAY_LORA + RW_A_LORA]
    gd = zm[:, o + RW_DECAY_LORA + RW_A_LORA:]
    w = -_softplus(-(w0_ref[...] + _dot_lo(jnp.tanh(wd), w2_ref[...]))) - 0.5
    a = jax.nn.sigmoid(a0_ref[...] + _dot_lo(ad, a2_ref[...]))
    kk = k * kk_ref[...]
    ss = _dot_exact_rhs(kk * kk, ones_ref[...])
    kap = kk / jnp.maximum(jnp.sqrt(ss), 1e-12)
    r_ref[...] = r
    lw_ref[...] = -jnp.exp(w)
    k_ref[...] = k * (1.0 + (a - 1.0) * ka_ref[...])
    v_ref[...] = v
    kap_ref[...] = kap
    bet_ref[...] = kap * a
    g_ref[...] = _dot_lo(jax.nn.sigmoid(gd), g2_ref[...])


def _rw_prep(zr, shift, B, T, P):
    n, cols = zr.shape
    G = GROUP_W
    tt = _row_tile(T, 256)
    nt = T // tt
    ones_bd = jnp.kron(jnp.eye(RW_HEADS, dtype=F32), jnp.ones((RW_HD, RW_HD), F32)).astype(BF16)
    row = lambda x: x.reshape(1, -1)
    full = lambda shape: pl.BlockSpec(shape, lambda b, i: (0,) * len(shape))
    tile = pl.BlockSpec((tt, G), lambda b, i: (b * nt + i, 0))
    return pl.pallas_call(
        _rw_prep_body,
        grid=(B, nt),
        in_specs=[
            pl.BlockSpec((tt, cols), lambda b, i: (b * nt + i, 0)),
            pl.BlockSpec((1, 1, cols), lambda b, i: (b, 0, 0)),
            full((1, cols)), full((1, G)), full((RW_DECAY_LORA, G)), full((1, G)), full((RW_A_LORA, G)),
            full((RW_GATE_LORA, G)), full((1, G)), full((1, G)), full((G, G)),
        ],
        out_specs=[tile] * 7,
        out_shape=[jax.ShapeDtypeStruct((n, G), F32)] * 7,
        scratch_shapes=[pltpu.VMEM((8, cols), F32)],
        compiler_params=_params(("parallel", "arbitrary"), 40),
        name="rwkv_prep",
    )(zr, shift.reshape(B, 1, cols), row(P['rw_mu']), row(P['rw_w0']), P['rw_w2'].astype(BF16), row(P['rw_a0']),
      P['rw_a2'].astype(BF16), P['rw_g2'].astype(BF16), row(P['rw_kk']), row(P['rw_ka']), ones_bd)


def _rw_scaled(lw, kap, bet, tri):
    cs = _dot_exact_lhs(tri, lw)
    return cs, kap * jnp.exp(cs - lw), bet * jnp.exp(-cs)


def _tri_masks(c):
    row = lax.broadcasted_iota(jnp.int32, (c, c), 0)
    col = lax.broadcasted_iota(jnp.int32, (c, c), 1)
    return row, col


def _chunk_tri(rows, c):
    row, col = _tri_masks(rows)
    return jnp.where((col <= row) & (row // c == col // c), 1.0, 0.0).astype(BF16)


def _rw_ldiag_body(lw_ref, kap_ref, bet_ref, o_ref, *, c):
    rows = lw_ref.shape[0]
    _, kk_all, bt_all = _rw_scaled(lw_ref[...], kap_ref[...], bet_ref[...], _chunk_tri(rows, c))
    srow, scol = _tri_masks(RW_SUB)
    units = [(cc, h) for cc in range(rows // c) for h in range(RW_HEADS)]
    Ls = [_dot_lo(kk_all[cc * c:(cc + 1) * c, h * RW_HD:(h + 1) * RW_HD],
                  bt_all[cc * c:(cc + 1) * c, h * RW_HD:(h + 1) * RW_HD], _NT) for cc, h in units]
    for (cc, h), L in zip(units, Ls):
        for b in range(c // RW_SUB):
            rs = slice(b * RW_SUB, (b + 1) * RW_SUB)
            o_ref[cc * c + b * RW_SUB:cc * c + (b + 1) * RW_SUB, h * RW_SUB:(h + 1) * RW_SUB] = (
                jnp.where(scol < srow, L[rs, rs], 0.0))


def _rw_inv_body(l_ref, t_ref):
    n = RW_SUB
    one = jnp.ones(l_ref.shape[2:], F32)
    zero = jnp.zeros(l_ref.shape[2:], F32)
    for t in range(n):
        for s in range(n):
            if s > t:
                t_ref[t, s] = zero
            elif s == t:
                t_ref[t, s] = one
            else:
                acc = l_ref[t, s]
                for j in range(s + 1, t):
                    acc = acc + l_ref[t, j] * t_ref[j, s]
                t_ref[t, s] = -acc


def _rw_main_body(r_ref, lw_ref, k_ref, v_ref, kap_ref, bet_ref, g_ref, td_ref, h0_ref, rk_ref, lng_ref, lnb_ref,
                  o_ref, hout_ref, h_ref, *, c):
    ci = pl.program_id(1)
    rows = r_ref.shape[0]
    nb = c // RW_SUB

    @pl.when(ci == 0)
    def _():
        h_ref[...] = h0_ref[0]

    crow, ccol = _tri_masks(c)
    strict = ccol < crow
    incl = ccol <= crow
    lw = lw_ref[...]
    cs, kk_all, bt_all = _rw_scaled(lw, kap_ref[...], bet_ref[...], _chunk_tri(rows, c))
    gi = jnp.exp(-cs)
    gg = jnp.exp(cs)
    kt_all = k_ref[...] * gi
    rt_all = r_ref[...] * gg
    bonus_all = r_ref[...] * k_ref[...] * rk_ref[...]
    hrow = lax.broadcasted_iota(jnp.int32, (RW_HD, RW_HD), 0)
    hcol = lax.broadcasted_iota(jnp.int32, (RW_HD, RW_HD), 1)
    HS = range(RW_HEADS)
    units = [(cc, h) for cc in range(rows // c) for h in HS]
    US = range(len(units))
    rsl = [slice(cc * c, (cc + 1) * c) for cc, _ in units]
    lsl = [slice(h * RW_HD, (h + 1) * RW_HD) for _, h in units]
    Kk = [kk_all[rsl[u], lsl[u]] for u in US]
    Bt = [bt_all[rsl[u], lsl[u]] for u in US]
    Kt = [kt_all[rsl[u], lsl[u]] for u in US]
    Rt = [rt_all[rsl[u], lsl[u]] for u in US]
    vv = [v_ref[rsl[u], lsl[u]] for u in US]
    Lm = [jnp.where(strict, _dot_lo(Kk[u], Bt[u], _NT), 0.0) for u in US]
    A1 = [jnp.where(strict, _dot_lo(Kk[u], Kt[u], _NT), 0.0) for u in US]
    A4 = [jnp.where(incl, _dot_lo(Rt[u], Bt[u], _NT), 0.0) for u in US]
    A3 = [jnp.where(incl, _dot_lo(Rt[u], Kt[u], _NT), 0.0) for u in US]
    X = [jnp.concatenate([Kk[u], _dot_lo(A1[u], vv[u])], axis=1) for u in US]
    zs = [[] for _ in US]
    for b in range(nb):
        rs = slice(b * RW_SUB, (b + 1) * RW_SUB)
        rhs = [X[u][rs] for u in US]
        if b:
            rhs = [rhs[u] - _dot_lo(Lm[u][rs, 0:b * RW_SUB], jnp.concatenate(zs[u], axis=0)) for u in US]
        for u, (cc, h) in enumerate(units):
            tbb = td_ref[cc * c + b * RW_SUB:cc * c + (b + 1) * RW_SUB, h * RW_SUB:(h + 1) * RW_SUB]
            zs[u].append(_dot_lo(tbb, rhs[u]))
    Z = [jnp.concatenate(zs[u], axis=0) if nb > 1 else zs[u][0] for u in US]
    A4Z = [_dot_lo(A4[u], Z[u]) for u in US]
    Rhat = [Rt[u] - A4Z[u][:, :RW_HD] for u in US]
    Yhat = [_dot_lo(A3[u], vv[u]) - A4Z[u][:, RW_HD:] for u in US]
    gC = [gg[(cc + 1) * c - 1:(cc + 1) * c, lsl[u]] for u, (cc, _) in enumerate(units)]
    MN = [_dot_lo(Bt[u] * gC[u], Z[u], _TN) for u in US]
    Mp = [jnp.where(hrow == hcol, gC[u], 0.0) - MN[u][:, :RW_HD] for u in US]
    Np = [_dot_lo(Kt[u] * gC[u], vv[u], _TN) - MN[u][:, RW_HD:] for u in US]
    H = [h_ref[h] for h in HS]
    ys = [None for _ in US]
    for cc in range(rows // c):
        for h in HS:
            u = cc * RW_HEADS + h
            ys[u] = _dot_lo(Rhat[u], H[h]) + Yhat[u]
        H = [_dot_hi(Mp[cc * RW_HEADS + h], H[h]) + Np[cc * RW_HEADS + h] for h in HS]
    for h in HS:
        h_ref[h] = H[h]
    for u in US:
        y = ys[u]
        mu = jnp.mean(y, axis=-1, keepdims=True)
        var = jnp.mean(jnp.square(y - mu), axis=-1, keepdims=True)
        yn = (y - mu) * lax.rsqrt(var + RW_LN_EPS) * lng_ref[:, lsl[u]] + lnb_ref[:, lsl[u]]
        yn = yn + jnp.sum(bonus_all[rsl[u], lsl[u]], axis=-1, keepdims=True) * vv[u]
        o_ref[rsl[u], lsl[u]] = yn * g_ref[rsl[u], lsl[u]]

    @pl.when(ci == pl.num_programs(1) - 1)
    def _():
        hout_ref[0] = h_ref[...]


def _rwkv7(zr, shift, S0, B, T, P):
    G = GROUP_W
    n = B * T
    r, lw, k, v, kap, bet, g = _rw_prep(zr, shift, B, T, P)
    c = min(RW_CHUNK, T)
    rows_l = _row_tile(T, c * RW_LDIAG_CHUNKS)
    rows_m = _row_tile(T, c * RW_MAIN_CHUNKS)
    nl, nc = T // rows_l, T // rows_m
    tile_l = pl.BlockSpec((rows_l, G), lambda b, i: (b * nl + i, 0))
    tile = pl.BlockSpec((rows_m, G), lambda b, i: (b * nc + i, 0))
    ld = pl.pallas_call(
        functools.partial(_rw_ldiag_body, c=c),
        grid=(B, nl),
        in_specs=[tile_l] * 3,
        out_specs=pl.BlockSpec((rows_l, RW_HEADS * RW_SUB), lambda b, i: (b * nl + i, 0)),
        out_shape=jax.ShapeDtypeStruct((n, RW_HEADS * RW_SUB), F32),
        compiler_params=_params(("parallel", "parallel"), 32),
        name="rwkv_ldiag",
    )(lw, kap, bet)
    ni = n // RW_SUB * RW_HEADS
    lt = ld.reshape(n // RW_SUB, RW_SUB, RW_HEADS, RW_SUB).transpose(1, 3, 0, 2).reshape(RW_SUB, RW_SUB, ni)
    nip = -(-ni // RW_INV_LANES) * RW_INV_LANES
    lt = jnp.pad(lt, ((0, 0), (0, 0), (0, nip - ni))).reshape(RW_SUB, RW_SUB, nip // V7X_LANES, V7X_LANES)
    inv_spec = pl.BlockSpec((RW_SUB, RW_SUB, RW_INV_LANES // V7X_LANES, V7X_LANES), lambda i: (0, 0, i, 0))
    tt = pl.pallas_call(
        _rw_inv_body,
        grid=(nip // RW_INV_LANES,),
        in_specs=[inv_spec],
        out_specs=inv_spec,
        out_shape=jax.ShapeDtypeStruct(lt.shape, F32),
        compiler_params=_params(("parallel",), 32),
        name="rwkv_inv",
    )(lt)
    td = tt.reshape(RW_SUB, RW_SUB, nip)[:, :, :ni].reshape(RW_SUB, RW_SUB, n // RW_SUB, RW_HEADS)
    td = td.transpose(2, 0, 3, 1).reshape(n, RW_HEADS * RW_SUB)
    h0 = jnp.swapaxes(S0, -1, -2)
    prow = lambda x: pl.BlockSpec((1, G), lambda b, i: (0, 0))
    st_spec = pl.BlockSpec((1, RW_HEADS, RW_HD, RW_HD), lambda b, i: (b, 0, 0, 0))
    out, hl = pl.pallas_call(
        functools.partial(_rw_main_body, c=c),
        grid=(B, nc),
        in_specs=[tile] * 7 + [pl.BlockSpec((rows_m, RW_HEADS * RW_SUB), lambda b, i: (b * nc + i, 0)), st_spec,
                               prow(0), prow(0), prow(0)],
        out_specs=[tile, st_spec],
        out_shape=[jax.ShapeDtypeStruct((n, G), F32), jax.ShapeDtypeStruct(S0.shape, F32)],
        scratch_shapes=[pltpu.VMEM((RW_HEADS, RW_HD, RW_HD), F32)],
        compiler_params=_params(("parallel", "arbitrary"), 32),
        name="rwkv_main",
    )(r, lw, k, v, kap, bet, g, td, h0, P['rw_rk'].reshape(1, G), P['rw_ln_g'].reshape(1, G),
      P['rw_ln_b'].reshape(1, G))
    return out, zr.reshape(B, T, -1)[:, -1], jnp.swapaxes(hl, -1, -2)


def _even_mixer(x2, B, T, g, st, P):
    conv_buf, lru_h, k_past, v_past, lf_past = st
    G = GROUP_W
    z_rg, z_qkv, z_og, z_fl = _norm_matmul(x2, g, P['e_w_in'], (2 * G, 3 * G, G, V7X_LANES))
    rnn_out, conv_new, h_last = _lru(z_rg, conv_buf, lru_h, B, T, P)
    qb, kn, kb, v, vb, lf = _fox_prep(z_qkv, z_fl, P)
    past = k_past.shape[1]
    lf_all = lf.reshape(B, T, V7X_LANES)
    kb_all, vb_all = kb.reshape(B, T, G), vb.reshape(B, T, G)
    if past:
        lf_all = jnp.concatenate([jnp.pad(lf_past, ((0, 0), (0, 0), (0, V7X_LANES - FOX_HEADS))), lf_all], axis=1)
        kb_all = jnp.concatenate([k_past.reshape(B, past, G).astype(BF16), kb_all], axis=1)
        vb_all = jnp.concatenate([v_past.reshape(B, past, G).astype(BF16), vb_all], axis=1)
    f_split = _cumsum_time_split(lf_all)
    fox_out = _fox_attention(qb, kb_all, vb_all, f_split, z_og, B, T, past)
    x2 = _out_proj(x2, rnn_out, fox_out, P['e_w_out'])
    heads = lambda t: t.reshape(B, T, FOX_HEADS, FOX_HD)
    return x2, (conv_new, h_last, heads(kn), heads(v), lf.reshape(B, T, V7X_LANES)[..., :FOX_HEADS])


def _odd_mixer(x2, B, T, g, st, lb, P):
    S_hg, shift, S_rw = st
    G = GROUP_W
    z_hg, z_rw = _norm_matmul(x2, g, P['o_w_in'], (4 * G, P['o_w_in'].shape[1] - 4 * G))
    hg_out, S_hg_new = _hgrn2(z_hg, lb, S_hg, B, T, P)
    rw_out, shift_new, S_rw_new = _rwkv7(z_rw, shift, S_rw, B, T, P)
    x2 = _out_proj(x2, hg_out, rw_out, P['o_w_out'])
    return x2, (S_hg_new, shift_new, S_rw_new)


def _trunk(x, states, W):
    lru_conv, lru_h, fox_k, fox_v, fox_lf, hg_S, rw_shift, rw_S = states
    B, T, D = x.shape
    depth = W['norm_g'].shape[0]
    sm = jax.nn.softmax(W['hg_lb_logits'], axis=0)
    lower_bounds = jnp.cumsum(sm, axis=0) - sm[0]
    x2 = x.reshape(B * T, D)
    even_new, odd_new = [], []
    for layer in range(depth):
        g = W['norm_g'][layer]
        x2 = _ffn(x2, g[0], W['ffn_w_in'][layer][0], W['ffn_w_out'][layer][0])
        if layer % 2 == 0:
            e = layer // 2
            P = {n: W[n][e] for n in ('e_w_in', 'e_w_out', 'lru_conv_w', 'lru_conv_b', 'lru_wa', 'lru_ba', 'lru_wx',
                                      'lru_bx', 'lru_lambda', 'fox_q_gain', 'fox_k_gain', 'fox_f_bias')}
            x2, new = _even_mixer(x2, B, T, g[1], (lru_conv[e], lru_h[e], fox_k[e], fox_v[e], fox_lf[e]), P)
            even_new.append(new)
        else:
            o = layer // 2
            P = {n: W[n][o] for n in ('o_w_in', 'o_w_out', 'hg_norm_g', 'rw_mu', 'rw_w0', 'rw_w2', 'rw_a0', 'rw_a2',
                                      'rw_g2', 'rw_kk', 'rw_ka', 'rw_rk', 'rw_ln_g', 'rw_ln_b')}
            x2, new = _odd_mixer(x2, B, T, g[1], (hg_S[o], rw_shift[o], rw_S[o]), lower_bounds[layer], P)
            odd_new.append(new)
        x2 = _ffn(x2, g[2], W['ffn_w_in'][layer][1], W['ffn_w_out'][layer][1])
    ev = [jnp.stack([n[j] for n in even_new]) for j in range(5)]
    od = [jnp.stack([n[j] for n in odd_new]) for j in range(3)]
    return x2.reshape(B, T, D), (ev[0], ev[1], ev[2], ev[3], ev[4], od[0], od[1], od[2])


def kernel(x_prompt, x_sample, state_lru_conv, state_lru_h, cache_fox_k, cache_fox_v, cache_fox_logf,
           state_hgrn_S, state_rwkv_shift, state_rwkv_S, norm_g, ffn_w_in, ffn_w_out, e_w_in, e_w_out,
           lru_conv_w, lru_conv_b, lru_wa, lru_ba, lru_wx, lru_bx, lru_lambda, fox_q_gain, fox_k_gain,
           fox_f_bias, o_w_in, o_w_out, hg_lb_logits, hg_norm_g, rw_mu, rw_w0, rw_w2, rw_a0, rw_a2, rw_g2,
           rw_kk, rw_ka, rw_rk, rw_ln_g, rw_ln_b):
    n_even, n_odd = e_w_in.shape[0], o_w_in.shape[0]
    W = dict(norm_g=norm_g, ffn_w_in=ffn_w_in.astype(BF16), ffn_w_out=ffn_w_out.astype(BF16),
             e_w_in=_pad_cols(e_w_in.astype(BF16)), e_w_out=e_w_out.astype(BF16),
             lru_conv_w=lru_conv_w, lru_conv_b=lru_conv_b, lru_wa=lru_wa, lru_ba=lru_ba, lru_wx=lru_wx,
             lru_bx=lru_bx, lru_lambda=lru_lambda, fox_q_gain=fox_q_gain, fox_k_gain=fox_k_gain,
             fox_f_bias=fox_f_bias, o_w_in=o_w_in.astype(BF16), o_w_out=o_w_out.astype(BF16),
             hg_lb_logits=hg_lb_logits, hg_norm_g=hg_norm_g, rw_mu=rw_mu, rw_w0=rw_w0, rw_w2=rw_w2, rw_a0=rw_a0,
             rw_a2=rw_a2, rw_g2=rw_g2, rw_kk=rw_kk, rw_ka=rw_ka, rw_rk=rw_rk, rw_ln_g=rw_ln_g, rw_ln_b=rw_ln_b)
    nb = x_prompt.shape[0]
    dt = x_prompt.dtype
    prompt_states = (jnp.zeros((n_even, nb, CONV_W - 1, GROUP_W), dt),
                     jnp.zeros((n_even, nb, GROUP_W), dt),
                     jnp.zeros((n_even, nb, 0, FOX_HEADS, FOX_HD), dt),
                     jnp.zeros((n_even, nb, 0, FOX_HEADS, FOX_HD), dt),
                     jnp.zeros((n_even, nb, 0, FOX_HEADS), dt),
                     jnp.zeros((n_odd, nb, HG_HEADS, GROUP_W // HG_HEADS, GROUP_W // HG_HEADS), dt),
                     jnp.zeros((n_odd, nb, rw_mu.shape[1]), dt),
                     jnp.zeros((n_odd, nb, RW_HEADS, RW_HD, RW_HD), dt))
    sample_states = (state_lru_conv, state_lru_h, cache_fox_k, cache_fox_v, cache_fox_logf,
                     state_hgrn_S, state_rwkv_shift, state_rwkv_S)
    y_prompt, p_new = _trunk(x_prompt, prompt_states, W)
    y_sample, s_new = _trunk(x_sample, sample_states, W)
    lru_conv_p, lru_h_p, fox_k_p, fox_v_p, fox_logf_p, hgrn_S_p, rwkv_shift_p, rwkv_S_p = p_new
    lru_conv_s, lru_h_s, fox_k_s, fox_v_s, fox_logf_s, hgrn_S_s, rwkv_shift_s, rwkv_S_s = s_new
    return (y_prompt, y_sample, lru_conv_p, lru_conv_s, lru_h_p, lru_h_s, fox_k_p, fox_k_s, fox_v_p, fox_v_s,
            fox_logf_p, fox_logf_s, hgrn_S_p, hgrn_S_s, rwkv_shift_p, rwkv_shift_s, rwkv_S_p, rwkv_S_s)
```

```python
import functools

import jax
import jax.numpy as jnp
from jax import lax
from jax.experimental import pallas as pl
from jax.experimental.pallas import tpu as pltpu

F32 = jnp.float32
BF16 = jnp.bfloat16

NORM_EPS = 1e-6
GROUP_W = 512
LRU_BLOCKS = 8
CONV_W = 4
LRU_C = 8.0
FOX_HEADS = 8
FOX_HD = 64
FOX_BLOCK = 128
HG_HEADS = 4
CHUNK = 64
RW_HEADS = 8
RW_HD = 64
RW_DECAY_LORA = 64
RW_A_LORA = 64
RW_GATE_LORA = 128
RW_LN_EPS = 64e-5

V7X_LANES = 128
FFN_COL_TILE = 1408


def _row_tile(n, want):
    t = min(n, want)
    while n % t:
        t //= 2
    return t


def _params(sem, vmem_mib):
    return pltpu.CompilerParams(dimension_semantics=sem, vmem_limit_bytes=vmem_mib << 20)


def _pad_cols(w):
    pad = -w.shape[-1] % V7X_LANES
    return jnp.pad(w, [(0, 0)] * (w.ndim - 1) + [(0, pad)])


def _rms(x, g):
    return x * lax.rsqrt(jnp.mean(x * x, axis=-1, keepdims=True) + NORM_EPS) * g


def _ffn_body(x_ref, g_ref, wg_ref, wu_ref, wo_ref, o_ref, h_ref, acc_ref):
    j = pl.program_id(1)

    @pl.when(j == 0)
    def _():
        h_ref[...] = _rms(x_ref[...], g_ref[...]).astype(BF16)
        acc_ref[...] = jnp.zeros_like(acc_ref)

    h = h_ref[...]
    gate = jnp.dot(h, wg_ref[...], preferred_element_type=F32)
    up = jnp.dot(h, wu_ref[...], preferred_element_type=F32)
    act = (gate * jax.nn.sigmoid(gate) * up).astype(BF16)
    acc_ref[...] += jnp.dot(act, wo_ref[...], preferred_element_type=F32)

    @pl.when(j == pl.num_programs(1) - 1)
    def _():
        o_ref[...] = x_ref[...] + 0.5 * acc_ref[...]


def _ffn(x, g, w_in, w_out):
    n, d = x.shape
    f = w_out.shape[0]
    tm = _row_tile(n, 512)
    tf = FFN_COL_TILE
    nf = f // tf
    return pl.pallas_call(
        _ffn_body,
        grid=(n // tm, nf),
        in_specs=[
            pl.BlockSpec((tm, d), lambda i, j: (i, 0)),
            pl.BlockSpec((1, d), lambda i, j: (0, 0)),
            pl.BlockSpec((d, tf), lambda i, j: (0, j)),
            pl.BlockSpec((d, tf), lambda i, j: (0, nf + j)),
            pl.BlockSpec((tf, d), lambda i, j: (j, 0)),
        ],
        out_specs=pl.BlockSpec((tm, d), lambda i, j: (i, 0)),
        out_shape=jax.ShapeDtypeStruct((n, d), F32),
        scratch_shapes=[pltpu.VMEM((tm, d), BF16), pltpu.VMEM((tm, d), F32)],
        compiler_params=_params(("parallel", "arbitrary"), 40),
        name="ffn",
    )(x, g.reshape(1, d), w_in, w_in, w_out)


def _norm_matmul_body(x_ref, g_ref, w_ref, *o_refs):
    h = _rms(x_ref[...], g_ref[...]).astype(BF16)
    z = jnp.dot(h, w_ref[...], preferred_element_type=F32)
    start = 0
    for o_ref in o_refs:
        width = o_ref.shape[1]
        o_ref[...] = z[:, start:start + width]
        start += width


def _norm_matmul(x, g, w, widths):
    n, d = x.shape
    c = w.shape[1]
    assert sum(widths) == c and all(wd % V7X_LANES == 0 for wd in widths)
    tm = _row_tile(n, 256)
    return pl.pallas_call(
        _norm_matmul_body,
        grid=(n // tm,),
        in_specs=[
            pl.BlockSpec((tm, d), lambda i: (i, 0)),
            pl.BlockSpec((1, d), lambda i: (0, 0)),
            pl.BlockSpec((d, c), lambda i: (0, 0)),
        ],
        out_specs=[pl.BlockSpec((tm, wd), lambda i: (i, 0)) for wd in widths],
        out_shape=[jax.ShapeDtypeStruct((n, wd), F32) for wd in widths],
        compiler_params=_params(("parallel",), 48),
        name="norm_matmul",
    )(x, g.reshape(1, d), w)


def _out_proj_body(x_ref, a_ref, b_ref, wa_ref, wb_ref, o_ref):
    acc = jnp.dot(a_ref[...].astype(BF16), wa_ref[...], preferred_element_type=F32)
    acc += jnp.dot(b_ref[...].astype(BF16), wb_ref[...], preferred_element_type=F32)
    o_ref[...] = x_ref[...] + acc


def _out_proj(x, a, b, w):
    n, d = x.shape
    ga, gb = a.shape[1], b.shape[1]
    tm = _row_tile(n, 512)
    return pl.pallas_call(
        _out_proj_body,
        grid=(n // tm,),
        in_specs=[
            pl.BlockSpec((tm, d), lambda i: (i, 0)),
            pl.BlockSpec((tm, ga), lambda i: (i, 0)),
            pl.BlockSpec((tm, gb), lambda i: (i, 0)),
            pl.BlockSpec((ga, d), lambda i: (0, 0)),
            pl.BlockSpec((gb, d), lambda i: (0, 0)),
        ],
        out_specs=pl.BlockSpec((tm, d), lambda i: (i, 0)),
        out_shape=jax.ShapeDtypeStruct((n, d), F32),
        compiler_params=_params(("parallel",), 32),
        name="out_proj",
    )(x, a, b, w[:ga], w[ga:])


LRU_ROWS = 256
CONV_PAD = 8


def _expm1(x):
    series = x * (1.0 + x * (1 / 2 + x * (1 / 6 + x * (1 / 24 + x * (1 / 120 + x * (1 / 720 + x * (1 / 5040 + x * (1 / 40320))))))))
    return jnp.where(jnp.abs(x) < 0.25, series, jnp.exp(x) - 1.0)


def _shift_rows(x, s, fill):
    row = lax.broadcasted_iota(jnp.int32, x.shape, 0)
    return jnp.where(row >= s, pltpu.roll(x, s, axis=0), fill)


def _lru_body(z_ref, buf_ref, h0_ref, cw_ref, cb_ref, wa_ref, ba_ref, wx_ref, bx_ref, lam_ref,
              o_ref, bufo_ref, ho_ref, x_ref, hc_ref):
    G = GROUP_W
    tt = z_ref.shape[0]

    @pl.when(pl.program_id(1) == 0)
    def _():
        x_ref[0:CONV_PAD, :] = buf_ref[0]
        hc_ref[...] = jnp.broadcast_to(h0_ref[0], hc_ref.shape)

    x_ref[CONV_PAD:CONV_PAD + tt, :] = z_ref[:, 0:G]
    xc = cb_ref[...]
    for j in range(CONV_W):
        lo = CONV_PAD - (CONV_W - 1) + j
        xc = xc + x_ref[lo:lo + tt, :] * cw_ref[j:j + 1, :]
    hist = x_ref[tt:tt + CONV_PAD, :]
    x_ref[0:CONV_PAD, :] = hist
    bufo_ref[0] = hist

    xb = xc.astype(BF16)
    r = jax.nn.sigmoid(jnp.dot(xb, wa_ref[...], preferred_element_type=F32) + ba_ref[...])
    ig = jax.nn.sigmoid(jnp.dot(xb, wx_ref[...], preferred_element_type=F32) + bx_ref[...])
    log_a = (-LRU_C * _softplus(-lam_ref[...])) * r
    a = jnp.exp(log_a)
    b = jnp.sqrt(-_expm1(2.0 * log_a)) * (ig * xc)
    s = 1
    while s < tt:
        b = a * _shift_rows(b, s, 0.0) + b
        a = a * _shift_rows(a, s, 1.0)
        s *= 2
    h = a * hc_ref[0:1, :] + b
    hc_ref[...] = jnp.broadcast_to(h[tt - 1:tt, :], hc_ref.shape)
    ho_ref[0] = h[tt - 1:tt, :]
    o_ref[...] = jax.nn.gelu(z_ref[:, G:2 * G]) * h


def _block_diag_dense(w):
    nb, bs, _ = w.shape
    eye = jnp.eye(nb, dtype=w.dtype)
    return (eye[:, None, :, None] * w[:, :, None, :]).reshape(nb * bs, nb * bs)


def _lru(z_rg, conv_buf, h0, B, T, P):
    G = GROUP_W
    n = B * T
    tt = _row_tile(T, LRU_ROWS)
    nt = T // tt
    buf = jnp.pad(conv_buf, ((0, 0), (CONV_PAD - (CONV_W - 1), 0), (0, 0)))
    cw = jnp.pad(P['lru_conv_w'], ((0, CONV_PAD - CONV_W), (0, 0)))
    row = lambda x: x.reshape(1, G)
    full = lambda shape: pl.BlockSpec(shape, lambda b, i: (0,) * len(shape))
    out, bufo, ho = pl.pallas_call(
        _lru_body,
        grid=(B, nt),
        in_specs=[
            pl.BlockSpec((tt, 2 * G), lambda b, i: (b * nt + i, 0)),
            pl.BlockSpec((1, CONV_PAD, G), lambda b, i: (b, 0, 0)),
            pl.BlockSpec((1, 1, G), lambda b, i: (b, 0, 0)),
            full((CONV_PAD, G)), full((1, G)), full((G, G)), full((1, G)), full((G, G)), full((1, G)), full((1, G)),
        ],
        out_specs=[
            pl.BlockSpec((tt, G), lambda b, i: (b * nt + i, 0)),
            pl.BlockSpec((1, CONV_PAD, G), lambda b, i: (b, 0, 0)),
            pl.BlockSpec((1, 1, G), lambda b, i: (b, 0, 0)),
        ],
        out_shape=[jax.ShapeDtypeStruct((n, G), F32), jax.ShapeDtypeStruct((B, CONV_PAD, G), F32),
                   jax.ShapeDtypeStruct((B, 1, G), F32)],
        scratch_shapes=[pltpu.VMEM((tt + CONV_PAD, G), F32), pltpu.VMEM((8, G), F32)],
        compiler_params=_params(("parallel", "arbitrary"), 32),
        name="lru",
    )(z_rg, buf, h0.reshape(B, 1, G), cw, row(P['lru_conv_b']), _block_diag_dense(P['lru_wa']).astype(BF16),
      row(P['lru_ba']), _block_diag_dense(P['lru_wx']).astype(BF16), row(P['lru_bx']), row(P['lru_lambda']))
    return out, bufo[:, CONV_PAD - (CONV_W - 1):], ho.reshape(B, G)


FOX_Q_COLS = 512
FOX_K_ROWS = 512
FOX_F_SPLIT = 3
FOX_NEG = -1e30
LOG2E = 1.4426950408889634
HEAD_PAIRS = FOX_HEADS // 2
PAIR_W = 2 * FOX_HD
FOX_KEY_TILE = 256


def _fox_prep_body(z_ref, fl_ref, qg_ref, kg_ref, fb_ref, ones_ref, q_ref, k_ref, kb_ref, v_ref, vb_ref, lf_ref,
                   *, q_transposed):
    G = GROUP_W
    q, k, v = z_ref[:, 0:G], z_ref[:, G:2 * G], z_ref[:, 2 * G:3 * G]
    inv = 1.0 / FOX_HD
    qn = q * lax.rsqrt(_dot_exact_rhs(q * q, ones_ref[...]) * inv + NORM_EPS) * qg_ref[...]
    kn = k * lax.rsqrt(_dot_exact_rhs(k * k, ones_ref[...]) * inv + NORM_EPS) * kg_ref[...]
    qs = qn * (LOG2E * FOX_HD ** -0.5)
    if q_transposed:
        for p in range(HEAD_PAIRS):
            q_ref[0, p] = qs[:, p * PAIR_W:(p + 1) * PAIR_W].T.astype(BF16)
    else:
        q_ref[...] = qs.astype(BF16)
    k_ref[...] = kn
    kb_ref[...] = kn.astype(BF16)
    v_ref[...] = v
    vb_ref[...] = v.astype(BF16)
    x = fl_ref[...] + fb_ref[...]
    lf_ref[...] = -_softplus(-x)


def _fox_prep(z_qkv, z_fl, B, T, P):
    n = z_qkv.shape[0]
    G = GROUP_W
    tt = _row_tile(T, 256)
    nt = T // tt
    q_transposed = tt % V7X_LANES == 0
    ones_bd = jnp.kron(jnp.eye(FOX_HEADS, dtype=F32), jnp.ones((FOX_HD, FOX_HD), F32)).astype(BF16)
    fb = jnp.pad(P['fox_f_bias'], (0, V7X_LANES - FOX_HEADS)).reshape(1, V7X_LANES)
    tile = lambda w: pl.BlockSpec((tt, w), lambda b, i: (b * nt + i, 0))
    full = lambda shape: pl.BlockSpec(shape, lambda b, i: (0,) * len(shape))
    if q_transposed:
        q_spec = pl.BlockSpec((1, HEAD_PAIRS, PAIR_W, tt), lambda b, i: (b, 0, 0, i))
        q_shape = jax.ShapeDtypeStruct((B, HEAD_PAIRS, PAIR_W, T), BF16)
    else:
        q_spec, q_shape = tile(G), jax.ShapeDtypeStruct((n, G), BF16)
    return pl.pallas_call(
        functools.partial(_fox_prep_body, q_transposed=q_transposed),
        grid=(B, nt),
        in_specs=[tile(3 * G), tile(V7X_LANES), full((1, G)), full((1, G)), full((1, V7X_LANES)), full((G, G))],
        out_specs=[q_spec] + [tile(G)] * 4 + [tile(V7X_LANES)],
        out_shape=[q_shape, jax.ShapeDtypeStruct((n, G), F32),
                   jax.ShapeDtypeStruct((n, G), BF16), jax.ShapeDtypeStruct((n, G), F32),
                   jax.ShapeDtypeStruct((n, G), BF16), jax.ShapeDtypeStruct((n, V7X_LANES), F32)],
        compiler_params=_params(("parallel", "parallel"), 32),
        name="fox_prep",
    )(z_qkv, z_fl, jnp.tile(P['fox_q_gain'], FOX_HEADS).reshape(1, G),
      jnp.tile(P['fox_k_gain'], FOX_HEADS).reshape(1, G), fb, ones_bd)


def _fox_keys_body(lf_ref, kb_ref, vb_ref, ka_ref, vt_ref, c_ref):
    tt = lf_ref.shape[1]

    @pl.when(pl.program_id(1) == 0)
    def _():
        c_ref[...] = jnp.zeros_like(c_ref)

    row, col = _tri_masks(tt)
    tri = jnp.where(col <= row, 1.0, 0.0).astype(BF16)
    f = _dot_exact_lhs(tri, lf_ref[0]) + c_ref[0:1, :]
    c_ref[...] = jnp.broadcast_to(f[tt - 1:tt, :], c_ref.shape)
    parts = _split3(f * LOG2E)
    srow, scol = _tri_masks(V7X_LANES)
    for p in range(HEAD_PAIRS):
        aug = jnp.zeros((tt, PAIR_W), F32)
        for t, part in enumerate(parts):
            sel = ((scol == t) & (srow == 2 * p)) | ((scol == FOX_F_SPLIT + t) & (srow == 2 * p + 1))
            aug = aug + jnp.dot(part, jnp.where(sel, 1.0, 0.0).astype(BF16), preferred_element_type=F32)
        ka_ref[0, p] = jnp.concatenate([kb_ref[0, :, p * PAIR_W:(p + 1) * PAIR_W], aug.astype(BF16)], axis=1)
        vt_ref[0, p] = vb_ref[0, :, p * PAIR_W:(p + 1) * PAIR_W].astype(F32).T.astype(BF16)


def _fox_keys(lf_all, kb_all, vb_all):
    B, tk_all, L = lf_all.shape
    G = GROUP_W
    tt = FOX_KEY_TILE
    return pl.pallas_call(
        _fox_keys_body,
        grid=(B, tk_all // tt),
        in_specs=[pl.BlockSpec((1, tt, L), lambda b, i: (b, i, 0)),
                  pl.BlockSpec((1, tt, G), lambda b, i: (b, i, 0)),
                  pl.BlockSpec((1, tt, G), lambda b, i: (b, i, 0))],
        out_specs=[pl.BlockSpec((1, HEAD_PAIRS, tt, 2 * PAIR_W), lambda b, i: (b, 0, i, 0)),
                   pl.BlockSpec((1, HEAD_PAIRS, PAIR_W, tt), lambda b, i: (b, 0, 0, i))],
        out_shape=[jax.ShapeDtypeStruct((B, HEAD_PAIRS, tk_all, 2 * PAIR_W), BF16),
                   jax.ShapeDtypeStruct((B, HEAD_PAIRS, PAIR_W, tk_all), BF16)],
        scratch_shapes=[pltpu.VMEM((8, L), F32)],
        compiler_params=_params(("parallel", "arbitrary"), 32),
        name="fox_keys",
    )(lf_all, kb_all, vb_all)


def _fox_attn_body(qt_ref, ka_ref, vt_ref, og_ref, o_ref, acc_ref, *, past, tk, t_real):
    qi = pl.program_id(2)
    tq = qt_ref.shape[3]
    t_out = o_ref.shape[0]
    first_q = past + qi * tq
    last_q = past + jnp.minimum(qi * tq + tq, t_real) - 1
    n_full = (first_q + 1) // tk
    n_all = last_q // tk + 1
    qt = qt_ref[0, 0]
    drow = lax.broadcasted_iota(jnp.int32, qt.shape, 0)
    rhs = []
    for h in range(2):
        top = jnp.where(drow // FOX_HD == h, qt, jnp.zeros_like(qt))
        aug = jnp.where((drow >= FOX_F_SPLIT * h) & (drow < FOX_F_SPLIT * (h + 1)), -1.0, 0.0).astype(BF16)
        rhs.append(jnp.concatenate([top, aug], axis=0))
    acc_ref[...] = jnp.zeros_like(acc_ref)
    krow = lax.broadcasted_iota(jnp.int32, (tk, tq), 0)
    qcol = lax.broadcasted_iota(jnp.int32, (tk, tq), 1)

    def block(ki, c, masked):
        ks = pl.multiple_of(ki * tk, tk)
        ka = ka_ref[0, 0, pl.ds(ks, tk), :]
        s = [jnp.dot(ka, rhs[h], preferred_element_type=F32) for h in range(2)]
        if masked:
            vis = ks + krow <= first_q + qcol
            s = [jnp.where(vis, s[h], FOX_NEG) for h in range(2)]
        m_new = [jnp.maximum(c[h][0], jnp.max(s[h], axis=0, keepdims=True)) for h in range(2)]
        alpha = [jnp.exp2(c[h][0] - m_new[h]) for h in range(2)]
        p = [jnp.exp2(s[h] - m_new[h]) for h in range(2)]
        l_new = [alpha[h] * c[h][1] + jnp.sum(p[h], axis=0, keepdims=True) for h in range(2)]
        vt = [vt_ref[0, 0, h * FOX_HD:(h + 1) * FOX_HD, pl.ds(ks, tk)] for h in range(2)]
        pv = [jnp.dot(vt[h], p[h].astype(BF16), preferred_element_type=F32) for h in range(2)]
        for h in range(2):
            acc_ref[h] = alpha[h] * acc_ref[h] + pv[h]
        return tuple((m_new[h], l_new[h]) for h in range(2))

    init = tuple((jnp.full((1, tq), FOX_NEG, F32), jnp.zeros((1, tq), F32)) for _ in range(2))
    c = lax.fori_loop(0, n_full, lambda ki, c: block(ki, c, False), init)
    c = lax.fori_loop(n_full, n_all, lambda ki, c: block(ki, c, True), c)
    o_t = jnp.concatenate([acc_ref[0] / c[0][1], acc_ref[1] / c[1][1]], axis=0)
    o_ref[...] = o_t.T[:t_out] * jax.nn.sigmoid(og_ref[...])


def _fox_attention(q, ka, vt, z_og, B, T, past):
    G = GROUP_W
    pw = PAIR_W
    tq = max(_row_tile(T, FOX_Q_COLS), V7X_LANES)
    tqp = -(-T // tq) * tq
    nq = tqp // tq
    t_out = min(tq, T)
    tk = FOX_K_ROWS
    tkp = ka.shape[2]
    if q.ndim == 2:
        q = q.reshape(B, T, HEAD_PAIRS, pw).transpose(0, 2, 3, 1)
    qt = jnp.pad(q, ((0, 0), (0, 0), (0, 0), (0, tqp - T)))
    return pl.pallas_call(
        functools.partial(_fox_attn_body, past=past, tk=tk, t_real=T),
        grid=(B, HEAD_PAIRS, nq),
        in_specs=[
            pl.BlockSpec((1, 1, pw, tq), lambda b, p, i: (b, p, 0, i)),
            pl.BlockSpec((1, 1, tkp, 2 * pw), lambda b, p, i: (b, p, 0, 0)),
            pl.BlockSpec((1, 1, pw, tkp), lambda b, p, i: (b, p, 0, 0)),
            pl.BlockSpec((t_out, pw), lambda b, p, i: (b * nq + i, p)),
        ],
        out_specs=pl.BlockSpec((t_out, pw), lambda b, p, i: (b * nq + i, p)),
        out_shape=jax.ShapeDtypeStruct((B * T, G), F32),
        scratch_shapes=[pltpu.VMEM((2, FOX_HD, tq), F32)],
        compiler_params=_params(("parallel", "parallel", "arbitrary"), 40),
        name="fox_attn",
    )(qt, ka, vt, z_og)


HG_CHUNK = 64
HG_STEP_CHUNKS = 4


def _hgrn_body(z_ref, lb_ref, s0_ref, ng_ref, o_ref, so_ref, st_ref, *, c):
    G = GROUP_W
    rows = z_ref.shape[0]
    nch = rows // c
    dk = G // HG_HEADS

    @pl.when(pl.program_id(1) == 0)
    def _():
        st_ref[...] = s0_ref[0]

    lb = lb_ref[...]
    f = lb + (1.0 - lb) * jax.nn.sigmoid(z_ref[:, G:2 * G])
    kx = 1.0 - f
    crow, ccol = _tri_masks(c)
    incl = ccol <= crow
    gs = _dot_exact_lhs(_chunk_tri(rows, c), jnp.log(f))
    qg_all = z_ref[:, 0:G] * jnp.exp(gs)
    kg_all = kx * jnp.exp(-gs)
    HS = range(HG_HEADS)
    units = [(cc, h) for cc in range(nch) for h in HS]
    US = range(len(units))
    rsl = [slice(cc * c, (cc + 1) * c) for cc, _ in units]
    lsl = [slice(h * dk, (h + 1) * dk) for _, h in units]
    g_last = [gs[(cc + 1) * c - 1:(cc + 1) * c, lsl[u]] for u, (cc, _) in enumerate(units)]
    vv = [z_ref[rsl[u], 2 * G + h * dk:2 * G + (h + 1) * dk] for u, (_, h) in enumerate(units)]
    A = [jnp.where(incl, _dot_lo(qg_all[rsl[u], lsl[u]], kg_all[rsl[u], lsl[u]], _NT), 0.0) for u in US]
    av = [_dot_lo(A[u], vv[u]) for u in US]
    kd = [kx[rsl[u], lsl[u]] * jnp.exp(g_last[u] - gs[rsl[u], lsl[u]]) for u in US]
    upd = [_dot_lo(vv[u], kd[u], _TN) for u in US]
    st = [st_ref[h] for h in HS]
    o = [None for _ in US]
    for cc in range(nch):
        for h in HS:
            u = cc * HG_HEADS + h
            o[u] = _dot_lo(qg_all[rsl[u], lsl[u]], st[h], _NT) + av[u]
        st = [st[h] * jnp.exp(g_last[cc * HG_HEADS + h]) + upd[cc * HG_HEADS + h] for h in HS]
    for h in HS:
        st_ref[h] = st[h]
    for u, (_, h) in enumerate(units):
        hg = z_ref[rsl[u], 3 * G + h * dk:3 * G + (h + 1) * dk]
        o_ref[rsl[u], lsl[u]] = _rms(o[u], ng_ref[:, lsl[u]]) * (hg * jax.nn.sigmoid(hg))

    @pl.when(pl.program_id(1) == pl.num_programs(1) - 1)
    def _():
        so_ref[0] = st_ref[...]


def _hgrn2(z_hg, lb, S0, B, T, P):
    G = GROUP_W
    c = min(HG_CHUNK, T)
    rows = _row_tile(T, c * HG_STEP_CHUNKS)
    nc = T // rows
    dk = G // HG_HEADS
    st_spec = pl.BlockSpec((1, HG_HEADS, dk, dk), lambda b, i: (b, 0, 0, 0))
    out, so = pl.pallas_call(
        functools.partial(_hgrn_body, c=c),
        grid=(B, nc),
        in_specs=[pl.BlockSpec((rows, 4 * G), lambda b, i: (b * nc + i, 0)),
                  pl.BlockSpec((1, G), lambda b, i: (0, 0)), st_spec, pl.BlockSpec((1, G), lambda b, i: (0, 0))],
        out_specs=[pl.BlockSpec((rows, G), lambda b, i: (b * nc + i, 0)), st_spec],
        out_shape=[jax.ShapeDtypeStruct((B * T, G), F32), jax.ShapeDtypeStruct(S0.shape, F32)],
        scratch_shapes=[pltpu.VMEM((HG_HEADS, dk, dk), F32)],
        compiler_params=_params(("parallel", "arbitrary"), 32),
        name="hgrn2",
    )(z_hg, lb.reshape(1, G), jnp.swapaxes(S0, -1, -2), P['hg_norm_g'].reshape(1, G))
    return out, jnp.swapaxes(so, -1, -2)


RW_CHUNK = 64
RW_SUB = 16
RW_LDIAG_CHUNKS = 4
RW_MAIN_CHUNKS = 2

_NT = (((1,), (1,)), ((), ()))
_TN = (((0,), (0,)), ((), ()))
_NN = (((1,), (0,)), ((), ()))


def _split3(x):
    h1 = x.astype(BF16)
    r1 = x - h1.astype(F32)
    h2 = r1.astype(BF16)
    h3 = (r1 - h2.astype(F32)).astype(BF16)
    return h1, h2, h3


def _dot_lo(a, b, dims=_NN):
    return lax.dot_general(a.astype(BF16), b.astype(BF16), dims, preferred_element_type=F32)


def _dot_hi(a, b, dims=_NN):
    ah = a.astype(BF16)
    al = (a - ah.astype(F32)).astype(BF16)
    bh = b.astype(BF16)
    bl = (b - bh.astype(F32)).astype(BF16)
    d = functools.partial(lax.dot_general, dimension_numbers=dims, preferred_element_type=F32)
    return d(ah, bh) + (d(al, bh) + d(ah, bl))


def _dot_exact_rhs(a, b):
    h1, h2, h3 = _split3(a)
    d = functools.partial(jnp.dot, preferred_element_type=F32)
    return d(h1, b) + (d(h2, b) + d(h3, b))


def _dot_exact_lhs(a, b):
    h1, h2, h3 = _split3(b)
    d = functools.partial(jnp.dot, preferred_element_type=F32)
    return d(a, h1) + (d(a, h2) + d(a, h3))


def _softplus(x):
    return jnp.maximum(x, 0.0) + jnp.log1p(jnp.exp(-jnp.abs(x)))


def _rw_prep_body(z_ref, shift_ref, mu_ref, w0_ref, w2_ref, a0_ref, a2_ref, g2_ref, kk_ref, ka_ref, ones_ref,
                  r_ref, lw_ref, k_ref, v_ref, kap_ref, bet_ref, g_ref, prev_ref):
    G = GROUP_W

    @pl.when(pl.program_id(1) == 0)
    def _():
        prev_ref[0:1, :] = shift_ref[0]

    z = z_ref[...]
    tt = z.shape[0]
    row = lax.broadcasted_iota(jnp.int32, z.shape, 0)
    shifted = jnp.where(row == 0, prev_ref[0:1, :], pltpu.roll(z, 1, axis=0))
    prev_ref[0:1, :] = z[tt - 1:tt, :]
    zm = z + (shifted - z) * mu_ref[...]
    r, k, v = zm[:, 0:G], zm[:, G:2 * G], zm[:, 2 * G:3 * G]
    o = 3 * G
    wd = zm[:, o:o + RW_DECAY_LORA]
    ad = zm[:, o + RW_DECAY_LORA:o + RW_DECAY_LORA + RW_A_LORA]
    gd = zm[:, o + RW_DECAY_LORA + RW_A_LORA:]
    w = -_softplus(-(w0_ref[...] + _dot_lo(jnp.tanh(wd), w2_ref[...]))) - 0.5
    a = jax.nn.sigmoid(a0_ref[...] + _dot_lo(ad, a2_ref[...]))
    kk = k * kk_ref[...]
    ss = _dot_exact_rhs(kk * kk, ones_ref[...])
    kap = kk / jnp.maximum(jnp.sqrt(ss), 1e-12)
    r_ref[...] = r
    lw_ref[...] = -jnp.exp(w)
    k_ref[...] = k * (1.0 + (a - 1.0) * ka_ref[...])
    v_ref[...] = v
    kap_ref[...] = kap
    bet_ref[...] = kap * a
    g_ref[...] = _dot_lo(jax.nn.sigmoid(gd), g2_ref[...])


def _rw_prep(zr, shift, B, T, P):
    n, cols = zr.shape
    G = GROUP_W
    tt = _row_tile(T, 256)
    nt = T // tt
    ones_bd = jnp.kron(jnp.eye(RW_HEADS, dtype=F32), jnp.ones((RW_HD, RW_HD), F32)).astype(BF16)
    row = lambda x: x.reshape(1, -1)
    full = lambda shape: pl.BlockSpec(shape, lambda b, i: (0,) * len(shape))
    tile = pl.BlockSpec((tt, G), lambda b, i: (b * nt + i, 0))
    return pl.pallas_call(
        _rw_prep_body,
        grid=(B, nt),
        in_specs=[
            pl.BlockSpec((tt, cols), lambda b, i: (b * nt + i, 0)),
            pl.BlockSpec((1, 1, cols), lambda b, i: (b, 0, 0)),
            full((1, cols)), full((1, G)), full((RW_DECAY_LORA, G)), full((1, G)), full((RW_A_LORA, G)),
            full((RW_GATE_LORA, G)), full((1, G)), full((1, G)), full((G, G)),
        ],
        out_specs=[tile] * 7,
        out_shape=[jax.ShapeDtypeStruct((n, G), F32)] * 7,
        scratch_shapes=[pltpu.VMEM((8, cols), F32)],
        compiler_params=_params(("parallel", "arbitrary"), 40),
        name="rwkv_prep",
    )(zr, shift.reshape(B, 1, cols), row(P['rw_mu']), row(P['rw_w0']), P['rw_w2'].astype(BF16), row(P['rw_a0']),
      P['rw_a2'].astype(BF16), P['rw_g2'].astype(BF16), row(P['rw_kk']), row(P['rw_ka']), ones_bd)


def _rw_scaled(lw, kap, bet, tri):
    cs = _dot_exact_lhs(tri, lw)
    return cs, kap * jnp.exp(cs - lw), bet * jnp.exp(-cs)


def _tri_masks(c):
    row = lax.broadcasted_iota(jnp.int32, (c, c), 0)
    col = lax.broadcasted_iota(jnp.int32, (c, c), 1)
    return row, col


def _chunk_tri(rows, c):
    row, col = _tri_masks(rows)
    return jnp.where((col <= row) & (row // c == col // c), 1.0, 0.0).astype(BF16)


def _rw_ldiag_body(lw_ref, kap_ref, bet_ref, o_ref, *, c):
    rows = lw_ref.shape[0]
    _, kk_all, bt_all = _rw_scaled(lw_ref[...], kap_ref[...], bet_ref[...], _chunk_tri(rows, c))
    srow, scol = _tri_masks(RW_SUB)
    units = [(cc, h) for cc in range(rows // c) for h in range(RW_HEADS)]
    Ls = [_dot_lo(kk_all[cc * c:(cc + 1) * c, h * RW_HD:(h + 1) * RW_HD],
                  bt_all[cc * c:(cc + 1) * c, h * RW_HD:(h + 1) * RW_HD], _NT) for cc, h in units]
    for (cc, h), L in zip(units, Ls):
        for b in range(c // RW_SUB):
            rs = slice(b * RW_SUB, (b + 1) * RW_SUB)
            o_ref[cc * c + b * RW_SUB:cc * c + (b + 1) * RW_SUB, h * RW_SUB:(h + 1) * RW_SUB] = (
                jnp.where(scol < srow, L[rs, rs], 0.0))


def _rw_inv_body(l_ref, t_ref, a_ref, b_ref):
    n = RW_SUB
    nblk = l_ref.shape[0] // n
    for t in range(n):
        a_ref[t] = l_ref[pl.ds(t, nblk, stride=n), :].T
    entry = lambda ref, t, s: ref.at[t, pl.ds(s, RW_HEADS, stride=n), :]
    one = jnp.ones((RW_HEADS, nblk), F32)
    zero = jnp.zeros((RW_HEADS, nblk), F32)
    for t in range(n):
        for s in range(n):
            if s > t:
                entry(b_ref, t, s)[...] = zero
            elif s == t:
                entry(b_ref, t, s)[...] = one
            else:
                acc = entry(a_ref, t, s)[...]
                for j in range(s + 1, t):
                    acc = acc + entry(a_ref, t, j)[...] * entry(b_ref, j, s)[...]
                entry(b_ref, t, s)[...] = -acc
    for t in range(n):
        t_ref[pl.ds(t, nblk, stride=n), :] = b_ref[t].T


def _rw_main_body(r_ref, lw_ref, k_ref, v_ref, kap_ref, bet_ref, g_ref, td_ref, h0_ref, rk_ref, lng_ref, lnb_ref,
                  o_ref, hout_ref, h_ref, *, c):
    ci = pl.program_id(1)
    rows = r_ref.shape[0]
    nb = c // RW_SUB

    @pl.when(ci == 0)
    def _():
        h_ref[...] = h0_ref[0]

    crow, ccol = _tri_masks(c)
    strict = ccol < crow
    incl = ccol <= crow
    lw = lw_ref[...]
    cs, kk_all, bt_all = _rw_scaled(lw, kap_ref[...], bet_ref[...], _chunk_tri(rows, c))
    gi = jnp.exp(-cs)
    gg = jnp.exp(cs)
    kt_all = k_ref[...] * gi
    rt_all = r_ref[...] * gg
    bonus_all = r_ref[...] * k_ref[...] * rk_ref[...]
    hrow = lax.broadcasted_iota(jnp.int32, (RW_HD, RW_HD), 0)
    hcol = lax.broadcasted_iota(jnp.int32, (RW_HD, RW_HD), 1)
    HS = range(RW_HEADS)
    units = [(cc, h) for cc in range(rows // c) for h in HS]
    US = range(len(units))
    rsl = [slice(cc * c, (cc + 1) * c) for cc, _ in units]
    lsl = [slice(h * RW_HD, (h + 1) * RW_HD) for _, h in units]
    Kk = [kk_all[rsl[u], lsl[u]] for u in US]
    Bt = [bt_all[rsl[u], lsl[u]] for u in US]
    Kt = [kt_all[rsl[u], lsl[u]] for u in US]
    Rt = [rt_all[rsl[u], lsl[u]] for u in US]
    vv = [v_ref[rsl[u], lsl[u]] for u in US]
    Lm = [jnp.where(strict, _dot_lo(Kk[u], Bt[u], _NT), 0.0) for u in US]
    A1 = [jnp.where(strict, _dot_lo(Kk[u], Kt[u], _NT), 0.0) for u in US]
    A4 = [jnp.where(incl, _dot_lo(Rt[u], Bt[u], _NT), 0.0) for u in US]
    A3 = [jnp.where(incl, _dot_lo(Rt[u], Kt[u], _NT), 0.0) for u in US]
    X = [jnp.concatenate([Kk[u], _dot_lo(A1[u], vv[u])], axis=1) for u in US]
    zs = [[] for _ in US]
    for b in range(nb):
        rs = slice(b * RW_SUB, (b + 1) * RW_SUB)
        rhs = [X[u][rs] for u in US]
        if b:
            rhs = [rhs[u] - _dot_lo(Lm[u][rs, 0:b * RW_SUB], jnp.concatenate(zs[u], axis=0)) for u in US]
        for u, (cc, h) in enumerate(units):
            tbb = td_ref[cc * c + b * RW_SUB:cc * c + (b + 1) * RW_SUB, h * RW_SUB:(h + 1) * RW_SUB]
            zs[u].append(_dot_lo(tbb, rhs[u]))
    Z = [jnp.concatenate(zs[u], axis=0) if nb > 1 else zs[u][0] for u in US]
    A4Z = [_dot_lo(A4[u], Z[u]) for u in US]
    Rhat = [Rt[u] - A4Z[u][:, :RW_HD] for u in US]
    Yhat = [_dot_lo(A3[u], vv[u]) - A4Z[u][:, RW_HD:] for u in US]
    gC = [gg[(cc + 1) * c - 1:(cc + 1) * c, lsl[u]] for u, (cc, _) in enumerate(units)]
    MN = [_dot_lo(Bt[u] * gC[u], Z[u], _TN) for u in US]
    Mp = [jnp.where(hrow == hcol, gC[u], 0.0) - MN[u][:, :RW_HD] for u in US]
    Np = [_dot_lo(Kt[u] * gC[u], vv[u], _TN) - MN[u][:, RW_HD:] for u in US]
    H = [h_ref[h] for h in HS]
    ys = [None for _ in US]
    for cc in range(rows // c):
        for h in HS:
            u = cc * RW_HEADS + h
            ys[u] = _dot_lo(Rhat[u], H[h]) + Yhat[u]
        H = [_dot_hi(Mp[cc * RW_HEADS + h], H[h]) + Np[cc * RW_HEADS + h] for h in HS]
    for h in HS:
        h_ref[h] = H[h]
    for u in US:
        y = ys[u]
        mu = jnp.mean(y, axis=-1, keepdims=True)
        var = jnp.mean(jnp.square(y - mu), axis=-1, keepdims=True)
        yn = (y - mu) * lax.rsqrt(var + RW_LN_EPS) * lng_ref[:, lsl[u]] + lnb_ref[:, lsl[u]]
        yn = yn + jnp.sum(bonus_all[rsl[u], lsl[u]], axis=-1, keepdims=True) * vv[u]
        o_ref[rsl[u], lsl[u]] = yn * g_ref[rsl[u], lsl[u]]

    @pl.when(ci == pl.num_programs(1) - 1)
    def _():
        hout_ref[0] = h_ref[...]


def _rwkv7(zr, shift, S0, B, T, P):
    G = GROUP_W
    n = B * T
    r, lw, k, v, kap, bet, g = _rw_prep(zr, shift, B, T, P)
    c = min(RW_CHUNK, T)
    rows_l = _row_tile(T, c * RW_LDIAG_CHUNKS)
    rows_m = _row_tile(T, c * RW_MAIN_CHUNKS)
    nl, nc = T // rows_l, T // rows_m
    tile_l = pl.BlockSpec((rows_l, G), lambda b, i: (b * nl + i, 0))
    tile = pl.BlockSpec((rows_m, G), lambda b, i: (b * nc + i, 0))
    ld = pl.pallas_call(
        functools.partial(_rw_ldiag_body, c=c),
        grid=(B, nl),
        in_specs=[tile_l] * 3,
        out_specs=pl.BlockSpec((rows_l, RW_HEADS * RW_SUB), lambda b, i: (b * nl + i, 0)),
        out_shape=jax.ShapeDtypeStruct((n, RW_HEADS * RW_SUB), F32),
        compiler_params=_params(("parallel", "parallel"), 32),
        name="rwkv_ldiag",
    )(lw, kap, bet)
    rows_i = V7X_LANES * RW_SUB
    npad = -(-n // rows_i) * rows_i
    inv_spec = pl.BlockSpec((rows_i, RW_HEADS * RW_SUB), lambda i: (i, 0))
    inv_scratch = pltpu.VMEM((RW_SUB, RW_HEADS * RW_SUB, V7X_LANES), F32)
    td = pl.pallas_call(
        _rw_inv_body,
        grid=(npad // rows_i,),
        in_specs=[inv_spec],
        out_specs=inv_spec,
        out_shape=jax.ShapeDtypeStruct((npad, RW_HEADS * RW_SUB), F32),
        scratch_shapes=[inv_scratch, inv_scratch],
        compiler_params=_params(("parallel",), 32),
        name="rwkv_inv",
    )(jnp.pad(ld, ((0, npad - n), (0, 0))))[:n]
    h0 = jnp.swapaxes(S0, -1, -2)
    prow = lambda x: pl.BlockSpec((1, G), lambda b, i: (0, 0))
    st_spec = pl.BlockSpec((1, RW_HEADS, RW_HD, RW_HD), lambda b, i: (b, 0, 0, 0))
    out, hl = pl.pallas_call(
        functools.partial(_rw_main_body, c=c),
        grid=(B, nc),
        in_specs=[tile] * 7 + [pl.BlockSpec((rows_m, RW_HEADS * RW_SUB), lambda b, i: (b * nc + i, 0)), st_spec,
                               prow(0), prow(0), prow(0)],
        out_specs=[tile, st_spec],
        out_shape=[jax.ShapeDtypeStruct((n, G), F32), jax.ShapeDtypeStruct(S0.shape, F32)],
        scratch_shapes=[pltpu.VMEM((RW_HEADS, RW_HD, RW_HD), F32)],
        compiler_params=_params(("parallel", "arbitrary"), 32),
        name="rwkv_main",
    )(r, lw, k, v, kap, bet, g, td, h0, P['rw_rk'].reshape(1, G), P['rw_ln_g'].reshape(1, G),
      P['rw_ln_b'].reshape(1, G))
    return out, zr.reshape(B, T, -1)[:, -1], jnp.swapaxes(hl, -1, -2)


def _even_mixer(x2, B, T, g, st, P):
    conv_buf, lru_h, k_past, v_past, lf_past = st
    G = GROUP_W
    z_rg, z_qkv, z_og, z_fl = _norm_matmul(x2, g, P['e_w_in'], (2 * G, 3 * G, G, V7X_LANES))
    rnn_out, conv_new, h_last = _lru(z_rg, conv_buf, lru_h, B, T, P)
    qb, kn, kb, v, vb, lf = _fox_prep(z_qkv, z_fl, B, T, P)
    past = k_past.shape[1]
    lf_all = lf.reshape(B, T, V7X_LANES)
    kb_all, vb_all = kb.reshape(B, T, G), vb.reshape(B, T, G)
    if past:
        lf_all = jnp.concatenate([jnp.pad(lf_past, ((0, 0), (0, 0), (0, V7X_LANES - FOX_HEADS))), lf_all], axis=1)
        kb_all = jnp.concatenate([k_past.reshape(B, past, G).astype(BF16), kb_all], axis=1)
        vb_all = jnp.concatenate([v_past.reshape(B, past, G).astype(BF16), vb_all], axis=1)
    tail = ((0, 0), (0, -(past + T) % FOX_K_ROWS), (0, 0))
    ka, vt = _fox_keys(jnp.pad(lf_all, tail), jnp.pad(kb_all, tail), jnp.pad(vb_all, tail))
    fox_out = _fox_attention(qb, ka, vt, z_og, B, T, past)
    x2 = _out_proj(x2, rnn_out, fox_out, P['e_w_out'])
    heads = lambda t: t.reshape(B, T, FOX_HEADS, FOX_HD)
    return x2, (conv_new, h_last, heads(kn), heads(v), lf.reshape(B, T, V7X_LANES)[..., :FOX_HEADS])


def _odd_mixer(x2, B, T, g, st, lb, P):
    S_hg, shift, S_rw = st
    G = GROUP_W
    z_hg, z_rw = _norm_matmul(x2, g, P['o_w_in'], (4 * G, P['o_w_in'].shape[1] - 4 * G))
    hg_out, S_hg_new = _hgrn2(z_hg, lb, S_hg, B, T, P)
    rw_out, shift_new, S_rw_new = _rwkv7(z_rw, shift, S_rw, B, T, P)
    x2 = _out_proj(x2, hg_out, rw_out, P['o_w_out'])
    return x2, (S_hg_new, shift_new, S_rw_new)


def _trunk(x, states, W):
    lru_conv, lru_h, fox_k, fox_v, fox_lf, hg_S, rw_shift, rw_S = states
    B, T, D = x.shape
    depth = W['norm_g'].shape[0]
    sm = jax.nn.softmax(W['hg_lb_logits'], axis=0)
    lower_bounds = jnp.cumsum(sm, axis=0) - sm[0]
    x2 = x.reshape(B * T, D)
    even_new, odd_new = [], []
    for layer in range(depth):
        g = W['norm_g'][layer]
        x2 = _ffn(x2, g[0], W['ffn_w_in'][layer][0], W['ffn_w_out'][layer][0])
        if layer % 2 == 0:
            e = layer // 2
            P = {n: W[n][e] for n in ('e_w_in', 'e_w_out', 'lru_conv_w', 'lru_conv_b', 'lru_wa', 'lru_ba', 'lru_wx',
                                      'lru_bx', 'lru_lambda', 'fox_q_gain', 'fox_k_gain', 'fox_f_bias')}
            x2, new = _even_mixer(x2, B, T, g[1], (lru_conv[e], lru_h[e], fox_k[e], fox_v[e], fox_lf[e]), P)
            even_new.append(new)
        else:
            o = layer // 2
            P = {n: W[n][o] for n in ('o_w_in', 'o_w_out', 'hg_norm_g', 'rw_mu', 'rw_w0', 'rw_w2', 'rw_a0', 'rw_a2',
                                      'rw_g2', 'rw_kk', 'rw_ka', 'rw_rk', 'rw_ln_g', 'rw_ln_b')}
            x2, new = _odd_mixer(x2, B, T, g[1], (hg_S[o], rw_shift[o], rw_S[o]), lower_bounds[layer], P)
            odd_new.append(new)
        x2 = _ffn(x2, g[2], W['ffn_w_in'][layer][1], W['ffn_w_out'][layer][1])
    ev = [jnp.stack([n[j] for n in even_new]) for j in range(5)]
    od = [jnp.stack([n[j] for n in odd_new]) for j in range(3)]
    return x2.reshape(B, T, D), (ev[0], ev[1], ev[2], ev[3], ev[4], od[0], od[1], od[2])


def kernel(x_prompt, x_sample, state_lru_conv, state_lru_h, cache_fox_k, cache_fox_v, cache_fox_logf,
           state_hgrn_S, state_rwkv_shift, state_rwkv_S, norm_g, ffn_w_in, ffn_w_out, e_w_in, e_w_out,
           lru_conv_w, lru_conv_b, lru_wa, lru_ba, lru_wx, lru_bx, lru_lambda, fox_q_gain, fox_k_gain,
           fox_f_bias, o_w_in, o_w_out, hg_lb_logits, hg_norm_g, rw_mu, rw_w0, rw_w2, rw_a0, rw_a2, rw_g2,
           rw_kk, rw_ka, rw_rk, rw_ln_g, rw_ln_b):
    n_even, n_odd = e_w_in.shape[0], o_w_in.shape[0]
    W = dict(norm_g=norm_g, ffn_w_in=ffn_w_in.astype(BF16), ffn_w_out=ffn_w_out.astype(BF16),
             e_w_in=_pad_cols(e_w_in.astype(BF16)), e_w_out=e_w_out.astype(BF16),
             lru_conv_w=lru_conv_w, lru_conv_b=lru_conv_b, lru_wa=lru_wa, lru_ba=lru_ba, lru_wx=lru_wx,
             lru_bx=lru_bx, lru_lambda=lru_lambda, fox_q_gain=fox_q_gain, fox_k_gain=fox_k_gain,
             fox_f_bias=fox_f_bias, o_w_in=o_w_in.astype(BF16), o_w_out=o_w_out.astype(BF16),
             hg_lb_logits=hg_lb_logits, hg_norm_g=hg_norm_g, rw_mu=rw_mu, rw_w0=rw_w0, rw_w2=rw_w2, rw_a0=rw_a0,
             rw_a2=rw_a2, rw_g2=rw_g2, rw_kk=rw_kk, rw_ka=rw_ka, rw_rk=rw_rk, rw_ln_g=rw_ln_g, rw_ln_b=rw_ln_b)
    nb = x_prompt.shape[0]
    dt = x_prompt.dtype
    prompt_states = (jnp.zeros((n_even, nb, CONV_W - 1, GROUP_W), dt),
                     jnp.zeros((n_even, nb, GROUP_W), dt),
                     jnp.zeros((n_even, nb, 0, FOX_HEADS, FOX_HD), dt),
                     jnp.zeros((n_even, nb, 0, FOX_HEADS, FOX_HD), dt),
                     jnp.zeros((n_even, nb, 0, FOX_HEADS), dt),
                     jnp.zeros((n_odd, nb, HG_HEADS, GROUP_W // HG_HEADS, GROUP_W // HG_HEADS), dt),
                     jnp.zeros((n_odd, nb, rw_mu.shape[1]), dt),
                     jnp.zeros((n_odd, nb, RW_HEADS, RW_HD, RW_HD), dt))
    sample_states = (state_lru_conv, state_lru_h, cache_fox_k, cache_fox_v, cache_fox_logf,
                     state_hgrn_S, state_rwkv_shift, state_rwkv_S)
    y_prompt, p_new = _trunk(x_prompt, prompt_states, W)
    y_sample, s_new = _trunk(x_sample, sample_states, W)
    lru_conv_p, lru_h_p, fox_k_p, fox_v_p, fox_logf_p, hgrn_S_p, rwkv_shift_p, rwkv_S_p = p_new
    lru_conv_s, lru_h_s, fox_k_s, fox_v_s, fox_logf_s, hgrn_S_s, rwkv_shift_s, rwkv_S_s = s_new
    return (y_prompt, y_sample, lru_conv_p, lru_conv_s, lru_h_p, lru_h_s, fox_k_p, fox_k_s, fox_v_p, fox_v_s,
            fox_logf_p, fox_logf_s, hgrn_S_p, hgrn_S_s, rwkv_shift_p, rwkv_shift_s, rwkv_S_p, rwkv_S_s)
```

```python
import functools

import jax
import jax.numpy as jnp
from jax import lax
from jax.experimental import pallas as pl
from jax.experimental.pallas import tpu as pltpu

F32 = jnp.float32
BF16 = jnp.bfloat16

NORM_EPS = 1e-6
GROUP_W = 512
LRU_BLOCKS = 8
CONV_W = 4
LRU_C = 8.0
FOX_HEADS = 8
FOX_HD = 64
FOX_BLOCK = 128
HG_HEADS = 4
CHUNK = 64
RW_HEADS = 8
RW_HD = 64
RW_DECAY_LORA = 64
RW_A_LORA = 64
RW_GATE_LORA = 128
RW_LN_EPS = 64e-5

V7X_LANES = 128
FFN_COL_TILE = 1408
FFN_ROW_TILE = 1024
FFN_VMEM_MIB = 60


def _row_tile(n, want):
    t = min(n, want)
    while n % t:
        t //= 2
    return t


def _params(sem, vmem_mib):
    return pltpu.CompilerParams(dimension_semantics=sem, vmem_limit_bytes=vmem_mib << 20)


def _pad_cols(w):
    pad = -w.shape[-1] % V7X_LANES
    return jnp.pad(w, [(0, 0)] * (w.ndim - 1) + [(0, pad)])


def _rms(x, g):
    return x * lax.rsqrt(jnp.mean(x * x, axis=-1, keepdims=True) + NORM_EPS) * g


def _ffn_body(x_ref, g_ref, wi_ref, wo_ref, o_ref, h_ref, acc_ref):
    j = pl.program_id(1)

    @pl.when(j == 0)
    def _():
        h_ref[...] = _rms(x_ref[...], g_ref[...]).astype(BF16)
        acc_ref[...] = jnp.zeros_like(acc_ref)

    tf = wo_ref.shape[0]
    gu = jnp.dot(h_ref[...], wi_ref[...], preferred_element_type=F32)
    gate, up = gu[:, :tf], gu[:, tf:]
    act = (gate * jax.nn.sigmoid(gate) * up).astype(BF16)
    acc_ref[...] += jnp.dot(act, wo_ref[...], preferred_element_type=F32)

    @pl.when(j == pl.num_programs(1) - 1)
    def _():
        o_ref[...] = x_ref[...] + 0.5 * acc_ref[...]


def _ffn_w_in_tiles(w_in):
    *lead, d, f2 = w_in.shape
    nf = f2 // 2 // FFN_COL_TILE
    w = w_in.reshape(*lead, d, 2, nf, FFN_COL_TILE)
    return jnp.swapaxes(w, -3, -2).reshape(*lead, d, f2)


def _ffn(x, g, w_in, w_out):
    n, d = x.shape
    f = w_out.shape[0]
    tm = _row_tile(n, FFN_ROW_TILE)
    tf = FFN_COL_TILE
    nf = f // tf
    return pl.pallas_call(
        _ffn_body,
        grid=(n // tm, nf),
        in_specs=[
            pl.BlockSpec((tm, d), lambda i, j: (i, 0)),
            pl.BlockSpec((1, d), lambda i, j: (0, 0)),
            pl.BlockSpec((d, 2 * tf), lambda i, j: (0, j)),
            pl.BlockSpec((tf, d), lambda i, j: (j, 0)),
        ],
        out_specs=pl.BlockSpec((tm, d), lambda i, j: (i, 0)),
        out_shape=jax.ShapeDtypeStruct((n, d), F32),
        scratch_shapes=[pltpu.VMEM((tm, d), BF16), pltpu.VMEM((tm, d), F32)],
        compiler_params=_params(("parallel", "arbitrary"), FFN_VMEM_MIB),
        name="ffn",
    )(x, g.reshape(1, d), w_in, w_out)


def _norm_matmul_body(x_ref, g_ref, w_ref, *o_refs):
    h = _rms(x_ref[...], g_ref[...]).astype(BF16)
    z = jnp.dot(h, w_ref[...], preferred_element_type=F32)
    start = 0
    for o_ref in o_refs:
        width = o_ref.shape[1]
        o_ref[...] = z[:, start:start + width]
        start += width


def _norm_matmul(x, g, w, widths):
    n, d = x.shape
    c = w.shape[1]
    assert sum(widths) == c and all(wd % V7X_LANES == 0 for wd in widths)
    tm = _row_tile(n, 256)
    return pl.pallas_call(
        _norm_matmul_body,
        grid=(n // tm,),
        in_specs=[
            pl.BlockSpec((tm, d), lambda i: (i, 0)),
            pl.BlockSpec((1, d), lambda i: (0, 0)),
            pl.BlockSpec((d, c), lambda i: (0, 0)),
        ],
        out_specs=[pl.BlockSpec((tm, wd), lambda i: (i, 0)) for wd in widths],
        out_shape=[jax.ShapeDtypeStruct((n, wd), F32) for wd in widths],
        compiler_params=_params(("parallel",), 48),
        name="norm_matmul",
    )(x, g.reshape(1, d), w)


def _out_proj_body(x_ref, a_ref, b_ref, wa_ref, wb_ref, o_ref):
    acc = jnp.dot(a_ref[...].astype(BF16), wa_ref[...], preferred_element_type=F32)
    acc += jnp.dot(b_ref[...].astype(BF16), wb_ref[...], preferred_element_type=F32)
    o_ref[...] = x_ref[...] + acc


def _out_proj(x, a, b, w):
    n, d = x.shape
    ga, gb = a.shape[1], b.shape[1]
    tm = _row_tile(n, 512)
    return pl.pallas_call(
        _out_proj_body,
        grid=(n // tm,),
        in_specs=[
            pl.BlockSpec((tm, d), lambda i: (i, 0)),
            pl.BlockSpec((tm, ga), lambda i: (i, 0)),
            pl.BlockSpec((tm, gb), lambda i: (i, 0)),
            pl.BlockSpec((ga, d), lambda i: (0, 0)),
            pl.BlockSpec((gb, d), lambda i: (0, 0)),
        ],
        out_specs=pl.BlockSpec((tm, d), lambda i: (i, 0)),
        out_shape=jax.ShapeDtypeStruct((n, d), F32),
        compiler_params=_params(("parallel",), 32),
        name="out_proj",
    )(x, a, b, w[:ga], w[ga:])


LRU_ROWS = 256
CONV_PAD = 8


def _expm1(x):
    series = x * (1.0 + x * (1 / 2 + x * (1 / 6 + x * (1 / 24 + x * (1 / 120 + x * (1 / 720 + x * (1 / 5040 + x * (1 / 40320))))))))
    return jnp.where(jnp.abs(x) < 0.25, series, jnp.exp(x) - 1.0)


def _shift_rows(x, s, fill):
    row = lax.broadcasted_iota(jnp.int32, x.shape, 0)
    return jnp.where(row >= s, pltpu.roll(x, s, axis=0), fill)


def _lru_body(z_ref, buf_ref, h0_ref, cw_ref, cb_ref, wa_ref, ba_ref, wx_ref, bx_ref, lam_ref,
              o_ref, bufo_ref, ho_ref, x_ref, hc_ref):
    G = GROUP_W
    tt = z_ref.shape[0]

    @pl.when(pl.program_id(1) == 0)
    def _():
        x_ref[0:CONV_PAD, :] = buf_ref[0]
        hc_ref[...] = jnp.broadcast_to(h0_ref[0], hc_ref.shape)

    x_ref[CONV_PAD:CONV_PAD + tt, :] = z_ref[:, 0:G]
    xc = cb_ref[...]
    for j in range(CONV_W):
        lo = CONV_PAD - (CONV_W - 1) + j
        xc = xc + x_ref[lo:lo + tt, :] * cw_ref[j:j + 1, :]
    hist = x_ref[tt:tt + CONV_PAD, :]
    x_ref[0:CONV_PAD, :] = hist
    bufo_ref[0] = hist

    xb = xc.astype(BF16)
    r = jax.nn.sigmoid(jnp.dot(xb, wa_ref[...], preferred_element_type=F32) + ba_ref[...])
    ig = jax.nn.sigmoid(jnp.dot(xb, wx_ref[...], preferred_element_type=F32) + bx_ref[...])
    log_a = (-LRU_C * _softplus(-lam_ref[...])) * r
    a = jnp.exp(log_a)
    b = jnp.sqrt(-_expm1(2.0 * log_a)) * (ig * xc)
    s = 1
    while s < tt:
        if s % 8:
            b = a * _shift_rows(b, s, 0.0) + b
            a = a * _shift_rows(a, s, 1.0)
        else:
            b = jnp.concatenate([b[:s], a[s:] * b[:tt - s] + b[s:]], axis=0)
            a = jnp.concatenate([a[:s], a[s:] * a[:tt - s]], axis=0)
        s *= 2
    h = a * hc_ref[0:1, :] + b
    hc_ref[...] = jnp.broadcast_to(h[tt - 1:tt, :], hc_ref.shape)
    ho_ref[0] = h[tt - 1:tt, :]
    o_ref[...] = jax.nn.gelu(z_ref[:, G:2 * G]) * h


def _block_diag_dense(w):
    nb, bs, _ = w.shape
    eye = jnp.eye(nb, dtype=w.dtype)
    return (eye[:, None, :, None] * w[:, :, None, :]).reshape(nb * bs, nb * bs)


def _lru(z_rg, conv_buf, h0, B, T, P):
    G = GROUP_W
    n = B * T
    tt = _row_tile(T, LRU_ROWS)
    nt = T // tt
    buf = jnp.pad(conv_buf, ((0, 0), (CONV_PAD - (CONV_W - 1), 0), (0, 0)))
    cw = jnp.pad(P['lru_conv_w'], ((0, CONV_PAD - CONV_W), (0, 0)))
    row = lambda x: x.reshape(1, G)
    full = lambda shape: pl.BlockSpec(shape, lambda b, i: (0,) * len(shape))
    out, bufo, ho = pl.pallas_call(
        _lru_body,
        grid=(B, nt),
        in_specs=[
            pl.BlockSpec((tt, 2 * G), lambda b, i: (b * nt + i, 0)),
            pl.BlockSpec((1, CONV_PAD, G), lambda b, i: (b, 0, 0)),
            pl.BlockSpec((1, 1, G), lambda b, i: (b, 0, 0)),
            full((CONV_PAD, G)), full((1, G)), full((G, G)), full((1, G)), full((G, G)), full((1, G)), full((1, G)),
        ],
        out_specs=[
            pl.BlockSpec((tt, G), lambda b, i: (b * nt + i, 0)),
            pl.BlockSpec((1, CONV_PAD, G), lambda b, i: (b, 0, 0)),
            pl.BlockSpec((1, 1, G), lambda b, i: (b, 0, 0)),
        ],
        out_shape=[jax.ShapeDtypeStruct((n, G), F32), jax.ShapeDtypeStruct((B, CONV_PAD, G), F32),
                   jax.ShapeDtypeStruct((B, 1, G), F32)],
        scratch_shapes=[pltpu.VMEM((tt + CONV_PAD, G), F32), pltpu.VMEM((8, G), F32)],
        compiler_params=_params(("parallel", "arbitrary"), 32),
        name="lru",
    )(z_rg, buf, h0.reshape(B, 1, G), cw, row(P['lru_conv_b']), _block_diag_dense(P['lru_wa']).astype(BF16),
      row(P['lru_ba']), _block_diag_dense(P['lru_wx']).astype(BF16), row(P['lru_bx']), row(P['lru_lambda']))
    return out, bufo[:, CONV_PAD - (CONV_W - 1):], ho.reshape(B, G)


FOX_Q_COLS = 512
FOX_K_ROWS = 512
FOX_F_SPLIT = 3
FOX_NEG = -1e30
LOG2E = 1.4426950408889634
HEAD_PAIRS = FOX_HEADS // 2
PAIR_W = 2 * FOX_HD
FOX_KEY_TILE = 256


def _fox_prep_body(z_ref, fl_ref, qg_ref, kg_ref, fb_ref, ones_ref, q_ref, k_ref, kb_ref, v_ref, vb_ref, lf_ref,
                   *, q_transposed):
    G = GROUP_W
    q, k, v = z_ref[:, 0:G], z_ref[:, G:2 * G], z_ref[:, 2 * G:3 * G]
    inv = 1.0 / FOX_HD
    qn = q * lax.rsqrt(_dot_exact_rhs(q * q, ones_ref[...]) * inv + NORM_EPS) * qg_ref[...]
    kn = k * lax.rsqrt(_dot_exact_rhs(k * k, ones_ref[...]) * inv + NORM_EPS) * kg_ref[...]
    qs = qn * (LOG2E * FOX_HD ** -0.5)
    if q_transposed:
        for p in range(HEAD_PAIRS):
            q_ref[0, p] = qs[:, p * PAIR_W:(p + 1) * PAIR_W].T.astype(BF16)
    else:
        q_ref[...] = qs.astype(BF16)
    k_ref[...] = kn
    kb_ref[...] = kn.astype(BF16)
    v_ref[...] = v
    vb_ref[...] = v.astype(BF16)
    x = fl_ref[...] + fb_ref[...]
    lf_ref[...] = -_softplus(-x)


def _fox_prep(z_qkv, z_fl, B, T, P):
    n = z_qkv.shape[0]
    G = GROUP_W
    tt = _row_tile(T, 256)
    nt = T // tt
    q_transposed = tt % V7X_LANES == 0
    ones_bd = jnp.kron(jnp.eye(FOX_HEADS, dtype=F32), jnp.ones((FOX_HD, FOX_HD), F32)).astype(BF16)
    fb = jnp.pad(P['fox_f_bias'], (0, V7X_LANES - FOX_HEADS)).reshape(1, V7X_LANES)
    tile = lambda w: pl.BlockSpec((tt, w), lambda b, i: (b * nt + i, 0))
    full = lambda shape: pl.BlockSpec(shape, lambda b, i: (0,) * len(shape))
    if q_transposed:
        q_spec = pl.BlockSpec((1, HEAD_PAIRS, PAIR_W, tt), lambda b, i: (b, 0, 0, i))
        q_shape = jax.ShapeDtypeStruct((B, HEAD_PAIRS, PAIR_W, T), BF16)
    else:
        q_spec, q_shape = tile(G), jax.ShapeDtypeStruct((n, G), BF16)
    return pl.pallas_call(
        functools.partial(_fox_prep_body, q_transposed=q_transposed),
        grid=(B, nt),
        in_specs=[tile(3 * G), tile(V7X_LANES), full((1, G)), full((1, G)), full((1, V7X_LANES)), full((G, G))],
        out_specs=[q_spec] + [tile(G)] * 4 + [tile(V7X_LANES)],
        out_shape=[q_shape, jax.ShapeDtypeStruct((n, G), F32),
                   jax.ShapeDtypeStruct((n, G), BF16), jax.ShapeDtypeStruct((n, G), F32),
                   jax.ShapeDtypeStruct((n, G), BF16), jax.ShapeDtypeStruct((n, V7X_LANES), F32)],
        compiler_params=_params(("parallel", "parallel"), 32),
        name="fox_prep",
    )(z_qkv, z_fl, jnp.tile(P['fox_q_gain'], FOX_HEADS).reshape(1, G),
      jnp.tile(P['fox_k_gain'], FOX_HEADS).reshape(1, G), fb, ones_bd)


def _fox_keys_body(lf_ref, kb_ref, vb_ref, ka_ref, vt_ref, c_ref):
    tt = lf_ref.shape[1]

    @pl.when(pl.program_id(1) == 0)
    def _():
        c_ref[...] = jnp.zeros_like(c_ref)

    row, col = _tri_masks(tt)
    tri = jnp.where(col <= row, 1.0, 0.0).astype(BF16)
    f = _dot_exact_lhs(tri, lf_ref[0]) + c_ref[0:1, :]
    c_ref[...] = jnp.broadcast_to(f[tt - 1:tt, :], c_ref.shape)
    parts = _split3(f * LOG2E)
    srow = lax.broadcasted_iota(jnp.int32, (V7X_LANES, FOX_HD), 0)
    scol = lax.broadcasted_iota(jnp.int32, (V7X_LANES, FOX_HD), 1)
    for h in range(FOX_HEADS):
        aug = jnp.zeros((tt, FOX_HD), F32)
        for t, part in enumerate(parts):
            sel = jnp.where((srow == h) & (scol == t), 1.0, 0.0).astype(BF16)
            aug = aug + jnp.dot(part, sel, preferred_element_type=F32)
        ka_ref[0, h] = jnp.concatenate([kb_ref[0, :, h * FOX_HD:(h + 1) * FOX_HD], aug.astype(BF16)], axis=1)
    for p in range(HEAD_PAIRS):
        vt_ref[0, p] = vb_ref[0, :, p * PAIR_W:(p + 1) * PAIR_W].astype(F32).T.astype(BF16)


def _fox_keys(lf_all, kb_all, vb_all):
    B, tk_all, L = lf_all.shape
    G = GROUP_W
    tt = FOX_KEY_TILE
    return pl.pallas_call(
        _fox_keys_body,
        grid=(B, tk_all // tt),
        in_specs=[pl.BlockSpec((1, tt, L), lambda b, i: (b, i, 0)),
                  pl.BlockSpec((1, tt, G), lambda b, i: (b, i, 0)),
                  pl.BlockSpec((1, tt, G), lambda b, i: (b, i, 0))],
        out_specs=[pl.BlockSpec((1, FOX_HEADS, tt, 2 * FOX_HD), lambda b, i: (b, 0, i, 0)),
                   pl.BlockSpec((1, HEAD_PAIRS, PAIR_W, tt), lambda b, i: (b, 0, 0, i))],
        out_shape=[jax.ShapeDtypeStruct((B, FOX_HEADS, tk_all, 2 * FOX_HD), BF16),
                   jax.ShapeDtypeStruct((B, HEAD_PAIRS, PAIR_W, tk_all), BF16)],
        scratch_shapes=[pltpu.VMEM((8, L), F32)],
        compiler_params=_params(("parallel", "arbitrary"), 32),
        name="fox_keys",
    )(lf_all, kb_all, vb_all)


def _fox_attn_body(qt_ref, ka_ref, vt_ref, og_ref, o_ref, acc_ref, *, past, tk, t_real):
    qi = pl.program_id(2)
    tq = qt_ref.shape[3]
    t_out = o_ref.shape[0]
    first_q = past + qi * tq
    last_q = past + jnp.minimum(qi * tq + tq, t_real) - 1
    n_full = (first_q + 1) // tk
    n_all = last_q // tk + 1
    drow = lax.broadcasted_iota(jnp.int32, (FOX_HD, tq), 0)
    minus = jnp.where(drow < FOX_F_SPLIT, -1.0, 0.0).astype(BF16)
    rhs = [jnp.concatenate([qt_ref[0, 0, h * FOX_HD:(h + 1) * FOX_HD, :], minus], axis=0) for h in range(2)]
    acc_ref[...] = jnp.zeros_like(acc_ref)
    krow = lax.broadcasted_iota(jnp.int32, (tk, tq), 0)
    qcol = lax.broadcasted_iota(jnp.int32, (tk, tq), 1)

    def block(ki, c, masked):
        ks = pl.multiple_of(ki * tk, tk)
        s = [jnp.dot(ka_ref[0, h, pl.ds(ks, tk), :], rhs[h], preferred_element_type=F32) for h in range(2)]
        if masked:
            vis = ks + krow <= first_q + qcol
            s = [jnp.where(vis, s[h], FOX_NEG) for h in range(2)]
        m_new = [jnp.maximum(c[h][0], jnp.max(s[h], axis=0, keepdims=True)) for h in range(2)]
        alpha = [jnp.exp2(c[h][0] - m_new[h]) for h in range(2)]
        p = [jnp.exp2(s[h] - m_new[h]) for h in range(2)]
        l_new = [alpha[h] * c[h][1] + jnp.sum(p[h], axis=0, keepdims=True) for h in range(2)]
        vt = [vt_ref[0, 0, h * FOX_HD:(h + 1) * FOX_HD, pl.ds(ks, tk)] for h in range(2)]
        pv = [jnp.dot(vt[h], p[h].astype(BF16), preferred_element_type=F32) for h in range(2)]
        for h in range(2):
            acc_ref[h] = alpha[h] * acc_ref[h] + pv[h]
        return tuple((m_new[h], l_new[h]) for h in range(2))

    init = tuple((jnp.full((1, tq), FOX_NEG, F32), jnp.zeros((1, tq), F32)) for _ in range(2))
    c = lax.fori_loop(0, n_full, lambda ki, c: block(ki, c, False), init)
    c = lax.fori_loop(n_full, n_all, lambda ki, c: block(ki, c, True), c)
    o_t = jnp.concatenate([acc_ref[0] / c[0][1], acc_ref[1] / c[1][1]], axis=0)
    o_ref[...] = o_t.T[:t_out] * jax.nn.sigmoid(og_ref[...])


def _fox_attention(q, ka, vt, z_og, B, T, past):
    G = GROUP_W
    pw = PAIR_W
    tq = max(_row_tile(T, FOX_Q_COLS), V7X_LANES)
    tqp = -(-T // tq) * tq
    nq = tqp // tq
    t_out = min(tq, T)
    tk = FOX_K_ROWS
    tkp = ka.shape[2]
    if q.ndim == 2:
        q = q.reshape(B, T, HEAD_PAIRS, pw).transpose(0, 2, 3, 1)
    qt = jnp.pad(q, ((0, 0), (0, 0), (0, 0), (0, tqp - T)))
    return pl.pallas_call(
        functools.partial(_fox_attn_body, past=past, tk=tk, t_real=T),
        grid=(B, HEAD_PAIRS, nq),
        in_specs=[
            pl.BlockSpec((1, 1, pw, tq), lambda b, p, i: (b, p, 0, i)),
            pl.BlockSpec((1, 2, tkp, 2 * FOX_HD), lambda b, p, i: (b, p, 0, 0)),
            pl.BlockSpec((1, 1, pw, tkp), lambda b, p, i: (b, p, 0, 0)),
            pl.BlockSpec((t_out, pw), lambda b, p, i: (b * nq + i, p)),
        ],
        out_specs=pl.BlockSpec((t_out, pw), lambda b, p, i: (b * nq + i, p)),
        out_shape=jax.ShapeDtypeStruct((B * T, G), F32),
        scratch_shapes=[pltpu.VMEM((2, FOX_HD, tq), F32)],
        compiler_params=_params(("parallel", "parallel", "arbitrary"), 40),
        name="fox_attn",
    )(qt, ka, vt, z_og)


HG_CHUNK = 64
HG_STEP_CHUNKS = 4


def _hgrn_body(z_ref, lb_ref, s0_ref, ng_ref, o_ref, so_ref, st_ref, *, c):
    G = GROUP_W
    rows = z_ref.shape[0]
    nch = rows // c
    dk = G // HG_HEADS

    @pl.when(pl.program_id(1) == 0)
    def _():
        st_ref[...] = s0_ref[0]

    lb = lb_ref[...]
    f = lb + (1.0 - lb) * jax.nn.sigmoid(z_ref[:, G:2 * G])
    kx = 1.0 - f
    crow, ccol = _tri_masks(c)
    incl = ccol <= crow
    gs = _dot_exact_lhs(_chunk_tri(rows, c), jnp.log(f))
    qg_all = z_ref[:, 0:G] * jnp.exp(gs)
    kg_all = kx * jnp.exp(-gs)
    HS = range(HG_HEADS)
    units = [(cc, h) for cc in range(nch) for h in HS]
    US = range(len(units))
    rsl = [slice(cc * c, (cc + 1) * c) for cc, _ in units]
    lsl = [slice(h * dk, (h + 1) * dk) for _, h in units]
    g_last = [gs[(cc + 1) * c - 1:(cc + 1) * c, lsl[u]] for u, (cc, _) in enumerate(units)]
    vv = [z_ref[rsl[u], 2 * G + h * dk:2 * G + (h + 1) * dk] for u, (_, h) in enumerate(units)]
    A = [jnp.where(incl, _dot_lo(qg_all[rsl[u], lsl[u]], kg_all[rsl[u], lsl[u]], _NT), 0.0) for u in US]
    av = [_dot_lo(A[u], vv[u]) for u in US]
    kd = [kx[rsl[u], lsl[u]] * jnp.exp(g_last[u] - gs[rsl[u], lsl[u]]) for u in US]
    upd = [_dot_lo(vv[u], kd[u], _TN) for u in US]
    st = [st_ref[h] for h in HS]
    o = [None for _ in US]
    for cc in range(nch):
        for h in HS:
            u = cc * HG_HEADS + h
            o[u] = _dot_lo(qg_all[rsl[u], lsl[u]], st[h], _NT) + av[u]
        st = [st[h] * jnp.exp(g_last[cc * HG_HEADS + h]) + upd[cc * HG_HEADS + h] for h in HS]
    for h in HS:
        st_ref[h] = st[h]
    for u, (_, h) in enumerate(units):
        hg = z_ref[rsl[u], 3 * G + h * dk:3 * G + (h + 1) * dk]
        o_ref[rsl[u], lsl[u]] = _rms(o[u], ng_ref[:, lsl[u]]) * (hg * jax.nn.sigmoid(hg))

    @pl.when(pl.program_id(1) == pl.num_programs(1) - 1)
    def _():
        so_ref[0] = st_ref[...]


def _hgrn2(z_hg, lb, S0, B, T, P):
    G = GROUP_W
    c = min(HG_CHUNK, T)
    rows = _row_tile(T, c * HG_STEP_CHUNKS)
    nc = T // rows
    dk = G // HG_HEADS
    st_spec = pl.BlockSpec((1, HG_HEADS, dk, dk), lambda b, i: (b, 0, 0, 0))
    out, so = pl.pallas_call(
        functools.partial(_hgrn_body, c=c),
        grid=(B, nc),
        in_specs=[pl.BlockSpec((rows, 4 * G), lambda b, i: (b * nc + i, 0)),
                  pl.BlockSpec((1, G), lambda b, i: (0, 0)), st_spec, pl.BlockSpec((1, G), lambda b, i: (0, 0))],
        out_specs=[pl.BlockSpec((rows, G), lambda b, i: (b * nc + i, 0)), st_spec],
        out_shape=[jax.ShapeDtypeStruct((B * T, G), F32), jax.ShapeDtypeStruct(S0.shape, F32)],
        scratch_shapes=[pltpu.VMEM((HG_HEADS, dk, dk), F32)],
        compiler_params=_params(("parallel", "arbitrary"), 32),
        name="hgrn2",
    )(z_hg, lb.reshape(1, G), jnp.swapaxes(S0, -1, -2), P['hg_norm_g'].reshape(1, G))
    return out, jnp.swapaxes(so, -1, -2)


RW_CHUNK = 64
RW_SUB = 16
RW_LDIAG_CHUNKS = 4
RW_MAIN_CHUNKS = 2

_NT = (((1,), (1,)), ((), ()))
_TN = (((0,), (0,)), ((), ()))
_NN = (((1,), (0,)), ((), ()))


def _split3(x):
    h1 = x.astype(BF16)
    r1 = x - h1.astype(F32)
    h2 = r1.astype(BF16)
    h3 = (r1 - h2.astype(F32)).astype(BF16)
    return h1, h2, h3


def _dot_lo(a, b, dims=_NN):
    return lax.dot_general(a.astype(BF16), b.astype(BF16), dims, preferred_element_type=F32)


def _dot_hi(a, b, dims=_NN):
    ah = a.astype(BF16)
    al = (a - ah.astype(F32)).astype(BF16)
    bh = b.astype(BF16)
    bl = (b - bh.astype(F32)).astype(BF16)
    d = functools.partial(lax.dot_general, dimension_numbers=dims, preferred_element_type=F32)
    return d(ah, bh) + (d(al, bh) + d(ah, bl))


def _dot_exact_rhs(a, b):
    h1, h2, h3 = _split3(a)
    d = functools.partial(jnp.dot, preferred_element_type=F32)
    return d(h1, b) + (d(h2, b) + d(h3, b))


def _dot_exact_lhs(a, b):
    h1, h2, h3 = _split3(b)
    d = functools.partial(jnp.dot, preferred_element_type=F32)
    return d(a, h1) + (d(a, h2) + d(a, h3))


def _softplus(x):
    return jnp.maximum(x, 0.0) + jnp.log1p(jnp.exp(-jnp.abs(x)))


def _rw_prep_body(z_ref, shift_ref, mu_ref, w0_ref, w2_ref, a0_ref, a2_ref, g2_ref, kk_ref, ka_ref, ones_ref,
                  r_ref, lw_ref, k_ref, v_ref, kap_ref, bet_ref, g_ref, prev_ref):
    G = GROUP_W

    @pl.when(pl.program_id(1) == 0)
    def _():
        prev_ref[0:1, :] = shift_ref[0]

    z = z_ref[...]
    tt = z.shape[0]
    row = lax.broadcasted_iota(jnp.int32, z.shape, 0)
    shifted = jnp.where(row == 0, prev_ref[0:1, :], pltpu.roll(z, 1, axis=0))
    prev_ref[0:1, :] = z[tt - 1:tt, :]
    zm = z + (shifted - z) * mu_ref[...]
    r, k, v = zm[:, 0:G], zm[:, G:2 * G], zm[:, 2 * G:3 * G]
    o = 3 * G
    wd = zm[:, o:o + RW_DECAY_LORA]
    ad = zm[:, o + RW_DECAY_LORA:o + RW_DECAY_LORA + RW_A_LORA]
    gd = zm[:, o + RW_DECAY_LORA + RW_A_LORA:]
    w = -_softplus(-(w0_ref[...] + _dot_lo(jnp.tanh(wd), w2_ref[...]))) - 0.5
    a = jax.nn.sigmoid(a0_ref[...] + _dot_lo(ad, a2_ref[...]))
    kk = k * kk_ref[...]
    ss = _dot_exact_rhs(kk * kk, ones_ref[...])
    kap = kk / jnp.maximum(jnp.sqrt(ss), 1e-12)
    r_ref[...] = r
    lw_ref[...] = -jnp.exp(w)
    k_ref[...] = k * (1.0 + (a - 1.0) * ka_ref[...])
    v_ref[...] = v
    kap_ref[...] = kap
    bet_ref[...] = kap * a
    g_ref[...] = _dot_lo(jax.nn.sigmoid(gd), g2_ref[...])


def _rw_prep(zr, shift, B, T, P):
    n, cols = zr.shape
    G = GROUP_W
    tt = _row_tile(T, 256)
    nt = T // tt
    ones_bd = jnp.kron(jnp.eye(RW_HEADS, dtype=F32), jnp.ones((RW_HD, RW_HD), F32)).astype(BF16)
    row = lambda x: x.reshape(1, -1)
    full = lambda shape: pl.BlockSpec(shape, lambda b, i: (0,) * len(shape))
    tile = pl.BlockSpec((tt, G), lambda b, i: (b * nt + i, 0))
    return pl.pallas_call(
        _rw_prep_body,
        grid=(B, nt),
        in_specs=[
            pl.BlockSpec((tt, cols), lambda b, i: (b * nt + i, 0)),
            pl.BlockSpec((1, 1, cols), lambda b, i: (b, 0, 0)),
            full((1, cols)), full((1, G)), full((RW_DECAY_LORA, G)), full((1, G)), full((RW_A_LORA, G)),
            full((RW_GATE_LORA, G)), full((1, G)), full((1, G)), full((G, G)),
        ],
        out_specs=[tile] * 7,
        out_shape=[jax.ShapeDtypeStruct((n, G), F32)] * 7,
        scratch_shapes=[pltpu.VMEM((8, cols), F32)],
        compiler_params=_params(("parallel", "arbitrary"), 40),
        name="rwkv_prep",
    )(zr, shift.reshape(B, 1, cols), row(P['rw_mu']), row(P['rw_w0']), P['rw_w2'].astype(BF16), row(P['rw_a0']),
      P['rw_a2'].astype(BF16), P['rw_g2'].astype(BF16), row(P['rw_kk']), row(P['rw_ka']), ones_bd)


def _rw_scaled(lw, kap, bet, tri):
    cs = _dot_exact_lhs(tri, lw)
    return cs, kap * jnp.exp(cs - lw), bet * jnp.exp(-cs)


def _tri_masks(c):
    row = lax.broadcasted_iota(jnp.int32, (c, c), 0)
    col = lax.broadcasted_iota(jnp.int32, (c, c), 1)
    return row, col


def _chunk_tri(rows, c):
    row, col = _tri_masks(rows)
    return jnp.where((col <= row) & (row // c == col // c), 1.0, 0.0).astype(BF16)


def _rw_ldiag_body(lw_ref, kap_ref, bet_ref, o_ref, *, c):
    rows = lw_ref.shape[0]
    _, kk_all, bt_all = _rw_scaled(lw_ref[...], kap_ref[...], bet_ref[...], _chunk_tri(rows, c))
    srow, scol = _tri_masks(RW_SUB)
    units = [(cc, h) for cc in range(rows // c) for h in range(RW_HEADS)]
    Ls = [_dot_lo(kk_all[cc * c:(cc + 1) * c, h * RW_HD:(h + 1) * RW_HD],
                  bt_all[cc * c:(cc + 1) * c, h * RW_HD:(h + 1) * RW_HD], _NT) for cc, h in units]
    for (cc, h), L in zip(units, Ls):
        for b in range(c // RW_SUB):
            rs = slice(b * RW_SUB, (b + 1) * RW_SUB)
            o_ref[cc * c + b * RW_SUB:cc * c + (b + 1) * RW_SUB, h * RW_SUB:(h + 1) * RW_SUB] = (
                jnp.where(scol < srow, L[rs, rs], 0.0))


def _rw_inv_body(l_ref, t_ref, a_ref, b_ref):
    n = RW_SUB
    nblk = l_ref.shape[0] // n
    for t in range(n):
        a_ref[t] = l_ref[pl.ds(t, nblk, stride=n), :].T
    entry = lambda ref, t, s: ref.at[t, pl.ds(s, RW_HEADS, stride=n), :]
    one = jnp.ones((RW_HEADS, nblk), F32)
    zero = jnp.zeros((RW_HEADS, nblk), F32)
    for t in range(n):
        for s in range(n):
            if s > t:
                entry(b_ref, t, s)[...] = zero
            elif s == t:
                entry(b_ref, t, s)[...] = one
            else:
                acc = entry(a_ref, t, s)[...]
                for j in range(s + 1, t):
                    acc = acc + entry(a_ref, t, j)[...] * entry(b_ref, j, s)[...]
                entry(b_ref, t, s)[...] = -acc
    for t in range(n):
        t_ref[pl.ds(t, nblk, stride=n), :] = b_ref[t].T


def _rw_main_body(r_ref, lw_ref, k_ref, v_ref, kap_ref, bet_ref, g_ref, td_ref, h0_ref, rk_ref, lng_ref, lnb_ref,
                  o_ref, hout_ref, h_ref, *, c):
    ci = pl.program_id(1)
    rows = r_ref.shape[0]
    nb = c // RW_SUB

    @pl.when(ci == 0)
    def _():
        h_ref[...] = h0_ref[0]

    crow, ccol = _tri_masks(c)
    strict = ccol < crow
    incl = ccol <= crow
    lw = lw_ref[...]
    cs, kk_all, bt_all = _rw_scaled(lw, kap_ref[...], bet_ref[...], _chunk_tri(rows, c))
    gi = jnp.exp(-cs)
    gg = jnp.exp(cs)
    kt_all = k_ref[...] * gi
    rt_all = r_ref[...] * gg
    bonus_all = r_ref[...] * k_ref[...] * rk_ref[...]
    hrow = lax.broadcasted_iota(jnp.int32, (RW_HD, RW_HD), 0)
    hcol = lax.broadcasted_iota(jnp.int32, (RW_HD, RW_HD), 1)
    HS = range(RW_HEADS)
    units = [(cc, h) for cc in range(rows // c) for h in HS]
    US = range(len(units))
    rsl = [slice(cc * c, (cc + 1) * c) for cc, _ in units]
    lsl = [slice(h * RW_HD, (h + 1) * RW_HD) for _, h in units]
    Kk = [kk_all[rsl[u], lsl[u]] for u in US]
    Bt = [bt_all[rsl[u], lsl[u]] for u in US]
    Kt = [kt_all[rsl[u], lsl[u]] for u in US]
    Rt = [rt_all[rsl[u], lsl[u]] for u in US]
    vv = [v_ref[rsl[u], lsl[u]] for u in US]
    Lm = [jnp.where(strict, _dot_lo(Kk[u], Bt[u], _NT), 0.0) for u in US]
    A1 = [jnp.where(strict, _dot_lo(Kk[u], Kt[u], _NT), 0.0) for u in US]
    A4 = [jnp.where(incl, _dot_lo(Rt[u], Bt[u], _NT), 0.0) for u in US]
    A3 = [jnp.where(incl, _dot_lo(Rt[u], Kt[u], _NT), 0.0) for u in US]
    X = [jnp.concatenate([Kk[u], _dot_lo(A1[u], vv[u])], axis=1) for u in US]
    zs = [[] for _ in US]
    for b in range(nb):
        rs = slice(b * RW_SUB, (b + 1) * RW_SUB)
        rhs = [X[u][rs] for u in US]
        if b:
            rhs = [rhs[u] - _dot_lo(Lm[u][rs, 0:b * RW_SUB], jnp.concatenate(zs[u], axis=0)) for u in US]
        for u, (cc, h) in enumerate(units):
            tbb = td_ref[cc * c + b * RW_SUB:cc * c + (b + 1) * RW_SUB, h * RW_SUB:(h + 1) * RW_SUB]
            zs[u].append(_dot_lo(tbb, rhs[u]))
    Z = [jnp.concatenate(zs[u], axis=0) if nb > 1 else zs[u][0] for u in US]
    A4Z = [_dot_lo(A4[u], Z[u]) for u in US]
    Rhat = [Rt[u] - A4Z[u][:, :RW_HD] for u in US]
    Yhat = [_dot_lo(A3[u], vv[u]) - A4Z[u][:, RW_HD:] for u in US]
    gC = [gg[(cc + 1) * c - 1:(cc + 1) * c, lsl[u]] for u, (cc, _) in enumerate(units)]
    MN = [_dot_lo(Bt[u] * gC[u], Z[u], _TN) for u in US]
    Mp = [jnp.where(hrow == hcol, gC[u], 0.0) - MN[u][:, :RW_HD] for u in US]
    Np = [_dot_lo(Kt[u] * gC[u], vv[u], _TN) - MN[u][:, RW_HD:] for u in US]
    H = [h_ref[h] for h in HS]
    ys = [None for _ in US]
    for cc in range(rows // c):
        for h in HS:
            u = cc * RW_HEADS + h
            ys[u] = _dot_lo(Rhat[u], H[h]) + Yhat[u]
        H = [_dot_hi(Mp[cc * RW_HEADS + h], H[h]) + Np[cc * RW_HEADS + h] for h in HS]
    for h in HS:
        h_ref[h] = H[h]
    for u in US:
        y = ys[u]
        mu = jnp.mean(y, axis=-1, keepdims=True)
        var = jnp.mean(jnp.square(y - mu), axis=-1, keepdims=True)
        yn = (y - mu) * lax.rsqrt(var + RW_LN_EPS) * lng_ref[:, lsl[u]] + lnb_ref[:, lsl[u]]
        yn = yn + jnp.sum(bonus_all[rsl[u], lsl[u]], axis=-1, keepdims=True) * vv[u]
        o_ref[rsl[u], lsl[u]] = yn * g_ref[rsl[u], lsl[u]]

    @pl.when(ci == pl.num_programs(1) - 1)
    def _():
        hout_ref[0] = h_ref[...]


def _rwkv7(zr, shift, S0, B, T, P):
    G = GROUP_W
    n = B * T
    r, lw, k, v, kap, bet, g = _rw_prep(zr, shift, B, T, P)
    c = min(RW_CHUNK, T)
    rows_l = _row_tile(T, c * RW_LDIAG_CHUNKS)
    rows_m = _row_tile(T, c * RW_MAIN_CHUNKS)
    nl, nc = T // rows_l, T // rows_m
    tile_l = pl.BlockSpec((rows_l, G), lambda b, i: (b * nl + i, 0))
    tile = pl.BlockSpec((rows_m, G), lambda b, i: (b * nc + i, 0))
    ld = pl.pallas_call(
        functools.partial(_rw_ldiag_body, c=c),
        grid=(B, nl),
        in_specs=[tile_l] * 3,
        out_specs=pl.BlockSpec((rows_l, RW_HEADS * RW_SUB), lambda b, i: (b * nl + i, 0)),
        out_shape=jax.ShapeDtypeStruct((n, RW_HEADS * RW_SUB), F32),
        compiler_params=_params(("parallel", "parallel"), 32),
        name="rwkv_ldiag",
    )(lw, kap, bet)
    rows_i = V7X_LANES * RW_SUB
    npad = -(-n // rows_i) * rows_i
    inv_spec = pl.BlockSpec((rows_i, RW_HEADS * RW_SUB), lambda i: (i, 0))
    inv_scratch = pltpu.VMEM((RW_SUB, RW_HEADS * RW_SUB, V7X_LANES), F32)
    td = pl.pallas_call(
        _rw_inv_body,
        grid=(npad // rows_i,),
        in_specs=[inv_spec],
        out_specs=inv_spec,
        out_shape=jax.ShapeDtypeStruct((npad, RW_HEADS * RW_SUB), F32),
        scratch_shapes=[inv_scratch, inv_scratch],
        compiler_params=_params(("parallel",), 32),
        name="rwkv_inv",
    )(jnp.pad(ld, ((0, npad - n), (0, 0))))[:n]
    h0 = jnp.swapaxes(S0, -1, -2)
    prow = lambda x: pl.BlockSpec((1, G), lambda b, i: (0, 0))
    st_spec = pl.BlockSpec((1, RW_HEADS, RW_HD, RW_HD), lambda b, i: (b, 0, 0, 0))
    out, hl = pl.pallas_call(
        functools.partial(_rw_main_body, c=c),
        grid=(B, nc),
        in_specs=[tile] * 7 + [pl.BlockSpec((rows_m, RW_HEADS * RW_SUB), lambda b, i: (b * nc + i, 0)), st_spec,
                               prow(0), prow(0), prow(0)],
        out_specs=[tile, st_spec],
        out_shape=[jax.ShapeDtypeStruct((n, G), F32), jax.ShapeDtypeStruct(S0.shape, F32)],
        scratch_shapes=[pltpu.VMEM((RW_HEADS, RW_HD, RW_HD), F32)],
        compiler_params=_params(("parallel", "arbitrary"), 32),
        name="rwkv_main",
    )(r, lw, k, v, kap, bet, g, td, h0, P['rw_rk'].reshape(1, G), P['rw_ln_g'].reshape(1, G),
      P['rw_ln_b'].reshape(1, G))
    return out, zr.reshape(B, T, -1)[:, -1], jnp.swapaxes(hl, -1, -2)


def _even_mixer(x2, B, T, g, st, P):
    conv_buf, lru_h, k_past, v_past, lf_past = st
    G = GROUP_W
    z_rg, z_qkv, z_og, z_fl = _norm_matmul(x2, g, P['e_w_in'], (2 * G, 3 * G, G, V7X_LANES))
    rnn_out, conv_new, h_last = _lru(z_rg, conv_buf, lru_h, B, T, P)
    qb, kn, kb, v, vb, lf = _fox_prep(z_qkv, z_fl, B, T, P)
    past = k_past.shape[1]
    lf_all = lf.reshape(B, T, V7X_LANES)
    kb_all, vb_all = kb.reshape(B, T, G), vb.reshape(B, T, G)
    if past:
        lf_all = jnp.concatenate([jnp.pad(lf_past, ((0, 0), (0, 0), (0, V7X_LANES - FOX_HEADS))), lf_all], axis=1)
        kb_all = jnp.concatenate([k_past.reshape(B, past, G).astype(BF16), kb_all], axis=1)
        vb_all = jnp.concatenate([v_past.reshape(B, past, G).astype(BF16), vb_all], axis=1)
    tail = ((0, 0), (0, -(past + T) % FOX_K_ROWS), (0, 0))
    ka, vt = _fox_keys(jnp.pad(lf_all, tail), jnp.pad(kb_all, tail), jnp.pad(vb_all, tail))
    fox_out = _fox_attention(qb, ka, vt, z_og, B, T, past)
    x2 = _out_proj(x2, rnn_out, fox_out, P['e_w_out'])
    heads = lambda t: t.reshape(B, T, FOX_HEADS, FOX_HD)
    return x2, (conv_new, h_last, heads(kn), heads(v), lf.reshape(B, T, V7X_LANES)[..., :FOX_HEADS])


def _odd_mixer(x2, B, T, g, st, lb, P):
    S_hg, shift, S_rw = st
    G = GROUP_W
    z_hg, z_rw = _norm_matmul(x2, g, P['o_w_in'], (4 * G, P['o_w_in'].shape[1] - 4 * G))
    hg_out, S_hg_new = _hgrn2(z_hg, lb, S_hg, B, T, P)
    rw_out, shift_new, S_rw_new = _rwkv7(z_rw, shift, S_rw, B, T, P)
    x2 = _out_proj(x2, hg_out, rw_out, P['o_w_out'])
    return x2, (S_hg_new, shift_new, S_rw_new)


def _trunk(x, states, W):
    lru_conv, lru_h, fox_k, fox_v, fox_lf, hg_S, rw_shift, rw_S = states
    B, T, D = x.shape
    depth = W['norm_g'].shape[0]
    sm = jax.nn.softmax(W['hg_lb_logits'], axis=0)
    lower_bounds = jnp.cumsum(sm, axis=0) - sm[0]
    x2 = x.reshape(B * T, D)
    even_new, odd_new = [], []
    for layer in range(depth):
        g = W['norm_g'][layer]
        x2 = _ffn(x2, g[0], W['ffn_w_in'][layer][0], W['ffn_w_out'][layer][0])
        if layer % 2 == 0:
            e = layer // 2
            P = {n: W[n][e] for n in ('e_w_in', 'e_w_out', 'lru_conv_w', 'lru_conv_b', 'lru_wa', 'lru_ba', 'lru_wx',
                                      'lru_bx', 'lru_lambda', 'fox_q_gain', 'fox_k_gain', 'fox_f_bias')}
            x2, new = _even_mixer(x2, B, T, g[1], (lru_conv[e], lru_h[e], fox_k[e], fox_v[e], fox_lf[e]), P)
            even_new.append(new)
        else:
            o = layer // 2
            P = {n: W[n][o] for n in ('o_w_in', 'o_w_out', 'hg_norm_g', 'rw_mu', 'rw_w0', 'rw_w2', 'rw_a0', 'rw_a2',
                                      'rw_g2', 'rw_kk', 'rw_ka', 'rw_rk', 'rw_ln_g', 'rw_ln_b')}
            x2, new = _odd_mixer(x2, B, T, g[1], (hg_S[o], rw_shift[o], rw_S[o]), lower_bounds[layer], P)
            odd_new.append(new)
        x2 = _ffn(x2, g[2], W['ffn_w_in'][layer][1], W['ffn_w_out'][layer][1])
    ev = [jnp.stack([n[j] for n in even_new]) for j in range(5)]
    od = [jnp.stack([n[j] for n in odd_new]) for j in range(3)]
    return x2.reshape(B, T, D), (ev[0], ev[1], ev[2], ev[3], ev[4], od[0], od[1], od[2])


def kernel(x_prompt, x_sample, state_lru_conv, state_lru_h, cache_fox_k, cache_fox_v, cache_fox_logf,
           state_hgrn_S, state_rwkv_shift, state_rwkv_S, norm_g, ffn_w_in, ffn_w_out, e_w_in, e_w_out,
           lru_conv_w, lru_conv_b, lru_wa, lru_ba, lru_wx, lru_bx, lru_lambda, fox_q_gain, fox_k_gain,
           fox_f_bias, o_w_in, o_w_out, hg_lb_logits, hg_norm_g, rw_mu, rw_w0, rw_w2, rw_a0, rw_a2, rw_g2,
           rw_kk, rw_ka, rw_rk, rw_ln_g, rw_ln_b):
    n_even, n_odd = e_w_in.shape[0], o_w_in.shape[0]
    W = dict(norm_g=norm_g, ffn_w_in=_ffn_w_in_tiles(ffn_w_in.astype(BF16)), ffn_w_out=ffn_w_out.astype(BF16),
             e_w_in=_pad_cols(e_w_in.astype(BF16)), e_w_out=e_w_out.astype(BF16),
             lru_conv_w=lru_conv_w, lru_conv_b=lru_conv_b, lru_wa=lru_wa, lru_ba=lru_ba, lru_wx=lru_wx,
             lru_bx=lru_bx, lru_lambda=lru_lambda, fox_q_gain=fox_q_gain, fox_k_gain=fox_k_gain,
             fox_f_bias=fox_f_bias, o_w_in=o_w_in.astype(BF16), o_w_out=o_w_out.astype(BF16),
             hg_lb_logits=hg_lb_logits, hg_norm_g=hg_norm_g, rw_mu=rw_mu, rw_w0=rw_w0, rw_w2=rw_w2, rw_a0=rw_a0,
             rw_a2=rw_a2, rw_g2=rw_g2, rw_kk=rw_kk, rw_ka=rw_ka, rw_rk=rw_rk, rw_ln_g=rw_ln_g, rw_ln_b=rw_ln_b)
    nb = x_prompt.shape[0]
    dt = x_prompt.dtype
    prompt_states = (jnp.zeros((n_even, nb, CONV_W - 1, GROUP_W), dt),
                     jnp.zeros((n_even, nb, GROUP_W), dt),
                     jnp.zeros((n_even, nb, 0, FOX_HEADS, FOX_HD), dt),
                     jnp.zeros((n_even, nb, 0, FOX_HEADS, FOX_HD), dt),
                     jnp.zeros((n_even, nb, 0, FOX_HEADS), dt),
                     jnp.zeros((n_odd, nb, HG_HEADS, GROUP_W // HG_HEADS, GROUP_W // HG_HEADS), dt),
                     jnp.zeros((n_odd, nb, rw_mu.shape[1]), dt),
                     jnp.zeros((n_odd, nb, RW_HEADS, RW_HD, RW_HD), dt))
    sample_states = (state_lru_conv, state_lru_h, cache_fox_k, cache_fox_v, cache_fox_logf,
                     state_hgrn_S, state_rwkv_shift, state_rwkv_S)
    y_prompt, p_new = _trunk(x_prompt, prompt_states, W)
    y_sample, s_new = _trunk(x_sample, sample_states, W)
    lru_conv_p, lru_h_p, fox_k_p, fox_v_p, fox_logf_p, hgrn_S_p, rwkv_shift_p, rwkv_S_p = p_new
    lru_conv_s, lru_h_s, fox_k_s, fox_v_s, fox_logf_s, hgrn_S_s, rwkv_shift_s, rwkv_S_s = s_new
    return (y_prompt, y_sample, lru_conv_p, lru_conv_s, lru_h_p, lru_h_s, fox_k_p, fox_k_s, fox_v_p, fox_v_s,
            fox_logf_p, fox_logf_s, hgrn_S_p, hgrn_S_s, rwkv_shift_p, rwkv_shift_s, rwkv_S_p, rwkv_S_s)
```

```python
import functools

import jax
import jax.numpy as jnp
from jax import lax
from jax.experimental import pallas as pl
from jax.experimental.pallas import tpu as pltpu

F32 = jnp.float32
BF16 = jnp.bfloat16

NORM_EPS = 1e-6
GROUP_W = 512
CONV_W = 4
LRU_C = 8.0
FOX_HEADS = 8
FOX_HD = 64
HG_HEADS = 4
RW_HEADS = 8
RW_HD = 64
RW_DECAY_LORA = 64
RW_A_LORA = 64
RW_GATE_LORA = 128
RW_LN_EPS = 64e-5

V7X_LANES = 128
FFN_COL_TILE = 1408
FFN_ROW_TILE = 1024
FFN_VMEM_MIB = 60


def _row_tile(n, want):
    t = min(n, want)
    while n % t:
        t //= 2
    return t


def _params(sem, vmem_mib):
    return pltpu.CompilerParams(dimension_semantics=sem, vmem_limit_bytes=vmem_mib << 20)


def _pad_cols(w):
    pad = -w.shape[-1] % V7X_LANES
    return jnp.pad(w, [(0, 0)] * (w.ndim - 1) + [(0, pad)])


def _rms(x, g):
    return x * lax.rsqrt(jnp.mean(x * x, axis=-1, keepdims=True) + NORM_EPS) * g


def _ffn_body(x_ref, g_ref, wi_ref, wo_ref, o_ref, h_ref, acc_ref):
    j = pl.program_id(1)

    @pl.when(j == 0)
    def _():
        h_ref[...] = _rms(x_ref[...], g_ref[...]).astype(BF16)
        acc_ref[...] = jnp.zeros_like(acc_ref)

    tf = wo_ref.shape[0]
    gu = jnp.dot(h_ref[...], wi_ref[...], preferred_element_type=F32)
    gate, up = gu[:, :tf], gu[:, tf:]
    act = (gate * jax.nn.sigmoid(gate) * up).astype(BF16)
    acc_ref[...] += jnp.dot(act, wo_ref[...], preferred_element_type=F32)

    @pl.when(j == pl.num_programs(1) - 1)
    def _():
        o_ref[...] = x_ref[...] + 0.5 * acc_ref[...]


def _cast_body(x_ref, o_ref):
    o_ref[...] = x_ref[...].astype(o_ref.dtype)


def _ffn_w_in_tiles(w_in):
    *lead, d, f2 = w_in.shape
    tf = FFN_COL_TILE
    nf = f2 // 2 // tf
    w = w_in.reshape(-1, d, f2)
    out = pl.pallas_call(
        _cast_body,
        grid=(w.shape[0], nf, 2),
        in_specs=[pl.BlockSpec((1, d, tf), lambda i, j, gu: (i, 0, gu * nf + j))],
        out_specs=pl.BlockSpec((1, d, tf), lambda i, j, gu: (i, 0, 2 * j + gu)),
        out_shape=jax.ShapeDtypeStruct(w.shape, BF16),
        compiler_params=_params(("parallel", "parallel", "parallel"), 32),
        name="ffn_weight_tiles",
    )(w)
    return out.reshape(*lead, d, f2)


def _ffn(x, g, w_in, w_out):
    n, d = x.shape
    f = w_out.shape[0]
    tm = _row_tile(n, FFN_ROW_TILE)
    tf = FFN_COL_TILE
    nf = f // tf
    return pl.pallas_call(
        _ffn_body,
        grid=(n // tm, nf),
        in_specs=[
            pl.BlockSpec((tm, d), lambda i, j: (i, 0)),
            pl.BlockSpec((1, d), lambda i, j: (0, 0)),
            pl.BlockSpec((d, 2 * tf), lambda i, j: (0, j)),
            pl.BlockSpec((tf, d), lambda i, j: (j, 0)),
        ],
        out_specs=pl.BlockSpec((tm, d), lambda i, j: (i, 0)),
        out_shape=jax.ShapeDtypeStruct((n, d), F32),
        scratch_shapes=[pltpu.VMEM((tm, d), BF16), pltpu.VMEM((tm, d), F32)],
        compiler_params=_params(("parallel", "arbitrary"), FFN_VMEM_MIB),
        name="ffn",
    )(x, g.reshape(1, d), w_in, w_out)


def _norm_matmul_body(x_ref, g_ref, w_ref, *o_refs):
    h = _rms(x_ref[...], g_ref[...]).astype(BF16)
    z = jnp.dot(h, w_ref[...], preferred_element_type=F32)
    start = 0
    for o_ref in o_refs:
        width = o_ref.shape[1]
        o_ref[...] = z[:, start:start + width]
        start += width


def _norm_matmul(x, g, w, widths):
    n, d = x.shape
    c = w.shape[1]
    assert sum(widths) == c and all(wd % V7X_LANES == 0 for wd in widths)
    tm = _row_tile(n, 256)
    return pl.pallas_call(
        _norm_matmul_body,
        grid=(n // tm,),
        in_specs=[
            pl.BlockSpec((tm, d), lambda i: (i, 0)),
            pl.BlockSpec((1, d), lambda i: (0, 0)),
            pl.BlockSpec((d, c), lambda i: (0, 0)),
        ],
        out_specs=[pl.BlockSpec((tm, wd), lambda i: (i, 0)) for wd in widths],
        out_shape=[jax.ShapeDtypeStruct((n, wd), F32) for wd in widths],
        compiler_params=_params(("parallel",), 48),
        name="norm_matmul",
    )(x, g.reshape(1, d), w)


def _out_proj_body(x_ref, a_ref, b_ref, wa_ref, wb_ref, o_ref):
    acc = jnp.dot(a_ref[...].astype(BF16), wa_ref[...], preferred_element_type=F32)
    acc += jnp.dot(b_ref[...].astype(BF16), wb_ref[...], preferred_element_type=F32)
    o_ref[...] = x_ref[...] + acc


def _out_proj(x, a, b, w):
    n, d = x.shape
    ga, gb = a.shape[1], b.shape[1]
    tm = _row_tile(n, 512)
    return pl.pallas_call(
        _out_proj_body,
        grid=(n // tm,),
        in_specs=[
            pl.BlockSpec((tm, d), lambda i: (i, 0)),
            pl.BlockSpec((tm, ga), lambda i: (i, 0)),
            pl.BlockSpec((tm, gb), lambda i: (i, 0)),
            pl.BlockSpec((ga, d), lambda i: (0, 0)),
            pl.BlockSpec((gb, d), lambda i: (0, 0)),
        ],
        out_specs=pl.BlockSpec((tm, d), lambda i: (i, 0)),
        out_shape=jax.ShapeDtypeStruct((n, d), F32),
        compiler_params=_params(("parallel",), 32),
        name="out_proj",
    )(x, a, b, w[:ga], w[ga:])


LRU_ROWS = 256
CONV_PAD = 8


def _expm1(x):
    series = x * (1.0 + x * (1 / 2 + x * (1 / 6 + x * (1 / 24 + x * (1 / 120 + x * (1 / 720 + x * (1 / 5040 + x * (1 / 40320))))))))
    return jnp.where(jnp.abs(x) < 0.25, series, jnp.exp(x) - 1.0)


def _shift_rows(x, s, fill):
    row = lax.broadcasted_iota(jnp.int32, x.shape, 0)
    return jnp.where(row >= s, pltpu.roll(x, s, axis=0), fill)


def _lru_body(z_ref, buf_ref, h0_ref, cw_ref, cb_ref, wa_ref, ba_ref, wx_ref, bx_ref, lam_ref,
              o_ref, bufo_ref, ho_ref, x_ref, hc_ref):
    G = GROUP_W
    tt = z_ref.shape[0]

    @pl.when(pl.program_id(1) == 0)
    def _():
        x_ref[0:CONV_PAD, :] = buf_ref[0]
        hc_ref[...] = jnp.broadcast_to(h0_ref[0], hc_ref.shape)

    x_ref[CONV_PAD:CONV_PAD + tt, :] = z_ref[:, 0:G]
    xc = cb_ref[...]
    for j in range(CONV_W):
        lo = CONV_PAD - (CONV_W - 1) + j
        xc = xc + x_ref[lo:lo + tt, :] * cw_ref[j:j + 1, :]
    hist = x_ref[tt:tt + CONV_PAD, :]
    x_ref[0:CONV_PAD, :] = hist
    bufo_ref[0] = hist

    xb = xc.astype(BF16)
    r = jax.nn.sigmoid(jnp.dot(xb, wa_ref[...], preferred_element_type=F32) + ba_ref[...])
    ig = jax.nn.sigmoid(jnp.dot(xb, wx_ref[...], preferred_element_type=F32) + bx_ref[...])
    log_a = (-LRU_C * _softplus(-lam_ref[...])) * r
    a = jnp.exp(log_a)
    b = jnp.sqrt(-_expm1(2.0 * log_a)) * (ig * xc)
    s = 1
    while s < tt:
        if s % 8:
            b = a * _shift_rows(b, s, 0.0) + b
            a = a * _shift_rows(a, s, 1.0)
        else:
            b = jnp.concatenate([b[:s], a[s:] * b[:tt - s] + b[s:]], axis=0)
            a = jnp.concatenate([a[:s], a[s:] * a[:tt - s]], axis=0)
        s *= 2
    h = a * hc_ref[0:1, :] + b
    hc_ref[...] = jnp.broadcast_to(h[tt - 1:tt, :], hc_ref.shape)
    ho_ref[0] = h[tt - 1:tt, :]
    o_ref[...] = jax.nn.gelu(z_ref[:, G:2 * G]) * h


def _block_diag_dense(w):
    nb, bs, _ = w.shape
    eye = jnp.eye(nb, dtype=w.dtype)
    return (eye[:, None, :, None] * w[:, :, None, :]).reshape(nb * bs, nb * bs)


def _lru(z_rg, conv_buf, h0, B, T, P):
    G = GROUP_W
    n = B * T
    tt = _row_tile(T, LRU_ROWS)
    nt = T // tt
    buf = jnp.pad(conv_buf, ((0, 0), (CONV_PAD - (CONV_W - 1), 0), (0, 0)))
    cw = jnp.pad(P['lru_conv_w'], ((0, CONV_PAD - CONV_W), (0, 0)))
    row = lambda x: x.reshape(1, G)
    full = lambda shape: pl.BlockSpec(shape, lambda b, i: (0,) * len(shape))
    out, bufo, ho = pl.pallas_call(
        _lru_body,
        grid=(B, nt),
        in_specs=[
            pl.BlockSpec((tt, 2 * G), lambda b, i: (b * nt + i, 0)),
            pl.BlockSpec((1, CONV_PAD, G), lambda b, i: (b, 0, 0)),
            pl.BlockSpec((1, 1, G), lambda b, i: (b, 0, 0)),
            full((CONV_PAD, G)), full((1, G)), full((G, G)), full((1, G)), full((G, G)), full((1, G)), full((1, G)),
        ],
        out_specs=[
            pl.BlockSpec((tt, G), lambda b, i: (b * nt + i, 0)),
            pl.BlockSpec((1, CONV_PAD, G), lambda b, i: (b, 0, 0)),
            pl.BlockSpec((1, 1, G), lambda b, i: (b, 0, 0)),
        ],
        out_shape=[jax.ShapeDtypeStruct((n, G), F32), jax.ShapeDtypeStruct((B, CONV_PAD, G), F32),
                   jax.ShapeDtypeStruct((B, 1, G), F32)],
        scratch_shapes=[pltpu.VMEM((tt + CONV_PAD, G), F32), pltpu.VMEM((8, G), F32)],
        compiler_params=_params(("parallel", "arbitrary"), 32),
        name="lru",
    )(z_rg, buf, h0.reshape(B, 1, G), cw, row(P['lru_conv_b']), _block_diag_dense(P['lru_wa']).astype(BF16),
      row(P['lru_ba']), _block_diag_dense(P['lru_wx']).astype(BF16), row(P['lru_bx']), row(P['lru_lambda']))
    return out, bufo[:, CONV_PAD - (CONV_W - 1):], ho.reshape(B, G)


FOX_Q_COLS = 512
FOX_K_ROWS = 512
FOX_F_SPLIT = 3
FOX_NEG = -1e30
LOG2E = 1.4426950408889634
HEAD_PAIRS = FOX_HEADS // 2
PAIR_W = 2 * FOX_HD
FOX_KEY_TILE = 256


def _fox_prep_body(z_ref, fl_ref, qg_ref, kg_ref, fb_ref, ones_ref, q_ref, k_ref, kb_ref, v_ref, vb_ref, lf_ref,
                   *, q_transposed):
    G = GROUP_W
    q, k, v = z_ref[:, 0:G], z_ref[:, G:2 * G], z_ref[:, 2 * G:3 * G]
    inv = 1.0 / FOX_HD
    qn = q * lax.rsqrt(_dot_exact_rhs(q * q, ones_ref[...]) * inv + NORM_EPS) * qg_ref[...]
    kn = k * lax.rsqrt(_dot_exact_rhs(k * k, ones_ref[...]) * inv + NORM_EPS) * kg_ref[...]
    qs = qn * (LOG2E * FOX_HD ** -0.5)
    if q_transposed:
        for p in range(HEAD_PAIRS):
            q_ref[0, p] = qs[:, p * PAIR_W:(p + 1) * PAIR_W].T.astype(BF16)
    else:
        q_ref[...] = qs.astype(BF16)
    k_ref[...] = kn
    kb_ref[...] = kn.astype(BF16)
    v_ref[...] = v
    vb_ref[...] = v.astype(BF16)
    x = fl_ref[...] + fb_ref[...]
    lf_ref[...] = -_softplus(-x)


def _fox_prep(z_qkv, z_fl, B, T, P):
    n = z_qkv.shape[0]
    G = GROUP_W
    tt = _row_tile(T, 256)
    nt = T // tt
    q_transposed = tt % V7X_LANES == 0
    ones_bd = jnp.kron(jnp.eye(FOX_HEADS, dtype=F32), jnp.ones((FOX_HD, FOX_HD), F32)).astype(BF16)
    fb = jnp.pad(P['fox_f_bias'], (0, V7X_LANES - FOX_HEADS)).reshape(1, V7X_LANES)
    tile = lambda w: pl.BlockSpec((tt, w), lambda b, i: (b * nt + i, 0))
    full = lambda shape: pl.BlockSpec(shape, lambda b, i: (0,) * len(shape))
    if q_transposed:
        q_spec = pl.BlockSpec((1, HEAD_PAIRS, PAIR_W, tt), lambda b, i: (b, 0, 0, i))
        q_shape = jax.ShapeDtypeStruct((B, HEAD_PAIRS, PAIR_W, T), BF16)
    else:
        q_spec, q_shape = tile(G), jax.ShapeDtypeStruct((n, G), BF16)
    return pl.pallas_call(
        functools.partial(_fox_prep_body, q_transposed=q_transposed),
        grid=(B, nt),
        in_specs=[tile(3 * G), tile(V7X_LANES), full((1, G)), full((1, G)), full((1, V7X_LANES)), full((G, G))],
        out_specs=[q_spec] + [tile(G)] * 4 + [tile(V7X_LANES)],
        out_shape=[q_shape, jax.ShapeDtypeStruct((n, G), F32),
                   jax.ShapeDtypeStruct((n, G), BF16), jax.ShapeDtypeStruct((n, G), F32),
                   jax.ShapeDtypeStruct((n, G), BF16), jax.ShapeDtypeStruct((n, V7X_LANES), F32)],
        compiler_params=_params(("parallel", "parallel"), 32),
        name="fox_prep",
    )(z_qkv, z_fl, jnp.tile(P['fox_q_gain'], FOX_HEADS).reshape(1, G),
      jnp.tile(P['fox_k_gain'], FOX_HEADS).reshape(1, G), fb, ones_bd)


def _fox_keys_body(lf_ref, kb_ref, vb_ref, ka_ref, vt_ref, c_ref):
    tt = lf_ref.shape[1]

    @pl.when(pl.program_id(1) == 0)
    def _():
        c_ref[...] = jnp.zeros_like(c_ref)

    row, col = _tri_masks(tt)
    tri = jnp.where(col <= row, 1.0, 0.0).astype(BF16)
    f = _dot_exact_lhs(tri, lf_ref[0]) + c_ref[0:1, :]
    c_ref[...] = jnp.broadcast_to(f[tt - 1:tt, :], c_ref.shape)
    parts = _split3(f * LOG2E)
    srow = lax.broadcasted_iota(jnp.int32, (V7X_LANES, FOX_HD), 0)
    scol = lax.broadcasted_iota(jnp.int32, (V7X_LANES, FOX_HD), 1)
    for h in range(FOX_HEADS):
        aug = jnp.zeros((tt, FOX_HD), F32)
        for t, part in enumerate(parts):
            sel = jnp.where((srow == h) & (scol == t), 1.0, 0.0).astype(BF16)
            aug = aug + jnp.dot(part, sel, preferred_element_type=F32)
        ka_ref[0, h] = jnp.concatenate([kb_ref[0, :, h * FOX_HD:(h + 1) * FOX_HD], aug.astype(BF16)], axis=1)
    for p in range(HEAD_PAIRS):
        vt_ref[0, p] = vb_ref[0, :, p * PAIR_W:(p + 1) * PAIR_W].astype(F32).T.astype(BF16)


def _fox_keys(lf_all, kb_all, vb_all):
    B, tk_all, L = lf_all.shape
    G = GROUP_W
    tt = FOX_KEY_TILE
    return pl.pallas_call(
        _fox_keys_body,
        grid=(B, tk_all // tt),
        in_specs=[pl.BlockSpec((1, tt, L), lambda b, i: (b, i, 0)),
                  pl.BlockSpec((1, tt, G), lambda b, i: (b, i, 0)),
                  pl.BlockSpec((1, tt, G), lambda b, i: (b, i, 0))],
        out_specs=[pl.BlockSpec((1, FOX_HEADS, tt, 2 * FOX_HD), lambda b, i: (b, 0, i, 0)),
                   pl.BlockSpec((1, HEAD_PAIRS, PAIR_W, tt), lambda b, i: (b, 0, 0, i))],
        out_shape=[jax.ShapeDtypeStruct((B, FOX_HEADS, tk_all, 2 * FOX_HD), BF16),
                   jax.ShapeDtypeStruct((B, HEAD_PAIRS, PAIR_W, tk_all), BF16)],
        scratch_shapes=[pltpu.VMEM((8, L), F32)],
        compiler_params=_params(("parallel", "arbitrary"), 32),
        name="fox_keys",
    )(lf_all, kb_all, vb_all)


def _fox_attn_body(qt_ref, ka_ref, vt_ref, og_ref, o_ref, acc_ref, *, past, tk, t_real):
    qi = pl.program_id(2)
    tq = qt_ref.shape[3]
    t_out = o_ref.shape[0]
    first_q = past + qi * tq
    last_q = past + jnp.minimum(qi * tq + tq, t_real) - 1
    n_full = (first_q + 1) // tk
    n_all = last_q // tk + 1
    drow = lax.broadcasted_iota(jnp.int32, (FOX_HD, tq), 0)
    minus = jnp.where(drow < FOX_F_SPLIT, -1.0, 0.0).astype(BF16)
    rhs = [jnp.concatenate([qt_ref[0, 0, h * FOX_HD:(h + 1) * FOX_HD, :], minus], axis=0) for h in range(2)]
    acc_ref[...] = jnp.zeros_like(acc_ref)
    krow = lax.broadcasted_iota(jnp.int32, (tk, tq), 0)
    qcol = lax.broadcasted_iota(jnp.int32, (tk, tq), 1)

    def block(ki, c, masked):
        ks = pl.multiple_of(ki * tk, tk)
        s = [jnp.dot(ka_ref[0, h, pl.ds(ks, tk), :], rhs[h], preferred_element_type=F32) for h in range(2)]
        if masked:
            vis = ks + krow <= first_q + qcol
            s = [jnp.where(vis, s[h], FOX_NEG) for h in range(2)]
        m_new = [jnp.maximum(c[h][0], jnp.max(s[h], axis=0, keepdims=True)) for h in range(2)]
        alpha = [jnp.exp2(c[h][0] - m_new[h]) for h in range(2)]
        p = [jnp.exp2(s[h] - m_new[h]) for h in range(2)]
        l_new = [alpha[h] * c[h][1] + jnp.sum(p[h], axis=0, keepdims=True) for h in range(2)]
        vt = [vt_ref[0, 0, h * FOX_HD:(h + 1) * FOX_HD, pl.ds(ks, tk)] for h in range(2)]
        pv = [jnp.dot(vt[h], p[h].astype(BF16), preferred_element_type=F32) for h in range(2)]
        for h in range(2):
            acc_ref[h] = alpha[h] * acc_ref[h] + pv[h]
        return tuple((m_new[h], l_new[h]) for h in range(2))

    init = tuple((jnp.full((1, tq), FOX_NEG, F32), jnp.zeros((1, tq), F32)) for _ in range(2))
    c = lax.fori_loop(0, n_full, lambda ki, c: block(ki, c, False), init)
    c = lax.fori_loop(n_full, n_all, lambda ki, c: block(ki, c, True), c)
    o_t = jnp.concatenate([acc_ref[0] / c[0][1], acc_ref[1] / c[1][1]], axis=0)
    o_ref[...] = o_t.T[:t_out] * jax.nn.sigmoid(og_ref[...])


def _fox_attention(q, ka, vt, z_og, B, T, past):
    G = GROUP_W
    pw = PAIR_W
    tq = max(_row_tile(T, FOX_Q_COLS), V7X_LANES)
    tqp = -(-T // tq) * tq
    nq = tqp // tq
    t_out = min(tq, T)
    tk = FOX_K_ROWS
    tkp = ka.shape[2]
    if q.ndim == 2:
        q = q.reshape(B, T, HEAD_PAIRS, pw).transpose(0, 2, 3, 1)
    qt = jnp.pad(q, ((0, 0), (0, 0), (0, 0), (0, tqp - T)))
    return pl.pallas_call(
        functools.partial(_fox_attn_body, past=past, tk=tk, t_real=T),
        grid=(B, HEAD_PAIRS, nq),
        in_specs=[
            pl.BlockSpec((1, 1, pw, tq), lambda b, p, i: (b, p, 0, i)),
            pl.BlockSpec((1, 2, tkp, 2 * FOX_HD), lambda b, p, i: (b, p, 0, 0)),
            pl.BlockSpec((1, 1, pw, tkp), lambda b, p, i: (b, p, 0, 0)),
            pl.BlockSpec((t_out, pw), lambda b, p, i: (b * nq + i, p)),
        ],
        out_specs=pl.BlockSpec((t_out, pw), lambda b, p, i: (b * nq + i, p)),
        out_shape=jax.ShapeDtypeStruct((B * T, G), F32),
        scratch_shapes=[pltpu.VMEM((2, FOX_HD, tq), F32)],
        compiler_params=_params(("parallel", "parallel", "arbitrary"), 40),
        name="fox_attn",
    )(qt, ka, vt, z_og)


HG_CHUNK = 64
HG_STEP_CHUNKS = 4


def _hgrn_body(z_ref, lb_ref, s0_ref, ng_ref, o_ref, so_ref, st_ref, *, c):
    G = GROUP_W
    rows = z_ref.shape[0]
    nch = rows // c
    dk = G // HG_HEADS

    @pl.when(pl.program_id(1) == 0)
    def _():
        st_ref[...] = s0_ref[0]

    lb = lb_ref[...]
    f = lb + (1.0 - lb) * jax.nn.sigmoid(z_ref[:, G:2 * G])
    kx = 1.0 - f
    crow, ccol = _tri_masks(c)
    incl = ccol <= crow
    gs = _dot_exact_lhs(_chunk_tri(rows, c), jnp.log(f))
    qg_all = z_ref[:, 0:G] * jnp.exp(gs)
    kg_all = kx * jnp.exp(-gs)
    HS = range(HG_HEADS)
    units = [(cc, h) for cc in range(nch) for h in HS]
    US = range(len(units))
    rsl = [slice(cc * c, (cc + 1) * c) for cc, _ in units]
    lsl = [slice(h * dk, (h + 1) * dk) for _, h in units]
    g_last = [gs[(cc + 1) * c - 1:(cc + 1) * c, lsl[u]] for u, (cc, _) in enumerate(units)]
    vv = [z_ref[rsl[u], 2 * G + h * dk:2 * G + (h + 1) * dk] for u, (_, h) in enumerate(units)]
    A = [jnp.where(incl, _dot_lo(qg_all[rsl[u], lsl[u]], kg_all[rsl[u], lsl[u]], _NT), 0.0) for u in US]
    av = [_dot_lo(A[u], vv[u]) for u in US]
    kd = [kx[rsl[u], lsl[u]] * jnp.exp(g_last[u] - gs[rsl[u], lsl[u]]) for u in US]
    upd = [_dot_lo(vv[u], kd[u], _TN) for u in US]
    st = [st_ref[h] for h in HS]
    o = [None for _ in US]
    for cc in range(nch):
        for h in HS:
            u = cc * HG_HEADS + h
            o[u] = _dot_lo(qg_all[rsl[u], lsl[u]], st[h], _NT) + av[u]
        st = [st[h] * jnp.exp(g_last[cc * HG_HEADS + h]) + upd[cc * HG_HEADS + h] for h in HS]
    for h in HS:
        st_ref[h] = st[h]
    for u, (_, h) in enumerate(units):
        hg = z_ref[rsl[u], 3 * G + h * dk:3 * G + (h + 1) * dk]
        o_ref[rsl[u], lsl[u]] = _rms(o[u], ng_ref[:, lsl[u]]) * (hg * jax.nn.sigmoid(hg))

    @pl.when(pl.program_id(1) == pl.num_programs(1) - 1)
    def _():
        so_ref[0] = st_ref[...]


def _hgrn2(z_hg, lb, S0, B, T, P):
    G = GROUP_W
    c = min(HG_CHUNK, T)
    rows = _row_tile(T, c * HG_STEP_CHUNKS)
    nc = T // rows
    dk = G // HG_HEADS
    st_spec = pl.BlockSpec((1, HG_HEADS, dk, dk), lambda b, i: (b, 0, 0, 0))
    out, so = pl.pallas_call(
        functools.partial(_hgrn_body, c=c),
        grid=(B, nc),
        in_specs=[pl.BlockSpec((rows, 4 * G), lambda b, i: (b * nc + i, 0)),
                  pl.BlockSpec((1, G), lambda b, i: (0, 0)), st_spec, pl.BlockSpec((1, G), lambda b, i: (0, 0))],
        out_specs=[pl.BlockSpec((rows, G), lambda b, i: (b * nc + i, 0)), st_spec],
        out_shape=[jax.ShapeDtypeStruct((B * T, G), F32), jax.ShapeDtypeStruct(S0.shape, F32)],
        scratch_shapes=[pltpu.VMEM((HG_HEADS, dk, dk), F32)],
        compiler_params=_params(("parallel", "arbitrary"), 32),
        name="hgrn2",
    )(z_hg, lb.reshape(1, G), jnp.swapaxes(S0, -1, -2), P['hg_norm_g'].reshape(1, G))
    return out, jnp.swapaxes(so, -1, -2)


RW_CHUNK = 64
RW_SUB = 16
RW_LDIAG_CHUNKS = 4
RW_MAIN_CHUNKS = 2

_NT = (((1,), (1,)), ((), ()))
_TN = (((0,), (0,)), ((), ()))
_NN = (((1,), (0,)), ((), ()))


def _split3(x):
    h1 = x.astype(BF16)
    r1 = x - h1.astype(F32)
    h2 = r1.astype(BF16)
    h3 = (r1 - h2.astype(F32)).astype(BF16)
    return h1, h2, h3


def _dot_lo(a, b, dims=_NN):
    return lax.dot_general(a.astype(BF16), b.astype(BF16), dims, preferred_element_type=F32)


def _dot_hi(a, b, dims=_NN):
    ah = a.astype(BF16)
    al = (a - ah.astype(F32)).astype(BF16)
    bh = b.astype(BF16)
    bl = (b - bh.astype(F32)).astype(BF16)
    d = functools.partial(lax.dot_general, dimension_numbers=dims, preferred_element_type=F32)
    return d(ah, bh) + (d(al, bh) + d(ah, bl))


def _dot_exact_rhs(a, b):
    h1, h2, h3 = _split3(a)
    d = functools.partial(jnp.dot, preferred_element_type=F32)
    return d(h1, b) + (d(h2, b) + d(h3, b))


def _dot_exact_lhs(a, b):
    h1, h2, h3 = _split3(b)
    d = functools.partial(jnp.dot, preferred_element_type=F32)
    return d(a, h1) + (d(a, h2) + d(a, h3))


def _softplus(x):
    return jnp.maximum(x, 0.0) + jnp.log1p(jnp.exp(-jnp.abs(x)))


def _rw_prep_body(z_ref, shift_ref, mu_ref, w0_ref, w2_ref, a0_ref, a2_ref, g2_ref, kk_ref, ka_ref, ones_ref,
                  r_ref, lw_ref, k_ref, v_ref, kap_ref, bet_ref, g_ref, prev_ref):
    G = GROUP_W

    @pl.when(pl.program_id(1) == 0)
    def _():
        prev_ref[0:1, :] = shift_ref[0]

    z = z_ref[...]
    tt = z.shape[0]
    row = lax.broadcasted_iota(jnp.int32, z.shape, 0)
    shifted = jnp.where(row == 0, prev_ref[0:1, :], pltpu.roll(z, 1, axis=0))
    prev_ref[0:1, :] = z[tt - 1:tt, :]
    zm = z + (shifted - z) * mu_ref[...]
    r, k, v = zm[:, 0:G], zm[:, G:2 * G], zm[:, 2 * G:3 * G]
    o = 3 * G
    wd = zm[:, o:o + RW_DECAY_LORA]
    ad = zm[:, o + RW_DECAY_LORA:o + RW_DECAY_LORA + RW_A_LORA]
    gd = zm[:, o + RW_DECAY_LORA + RW_A_LORA:]
    w = -_softplus(-(w0_ref[...] + _dot_lo(jnp.tanh(wd), w2_ref[...]))) - 0.5
    a = jax.nn.sigmoid(a0_ref[...] + _dot_lo(ad, a2_ref[...]))
    kk = k * kk_ref[...]
    ss = _dot_exact_rhs(kk * kk, ones_ref[...])
    kap = kk / jnp.maximum(jnp.sqrt(ss), 1e-12)
    r_ref[...] = r
    lw_ref[...] = -jnp.exp(w)
    k_ref[...] = k * (1.0 + (a - 1.0) * ka_ref[...])
    v_ref[...] = v
    kap_ref[...] = kap
    bet_ref[...] = kap * a
    g_ref[...] = _dot_lo(jax.nn.sigmoid(gd), g2_ref[...])


def _rw_prep(zr, shift, B, T, P):
    n, cols = zr.shape
    G = GROUP_W
    tt = _row_tile(T, 256)
    nt = T // tt
    ones_bd = jnp.kron(jnp.eye(RW_HEADS, dtype=F32), jnp.ones((RW_HD, RW_HD), F32)).astype(BF16)
    row = lambda x: x.reshape(1, -1)
    full = lambda shape: pl.BlockSpec(shape, lambda b, i: (0,) * len(shape))
    tile = pl.BlockSpec((tt, G), lambda b, i: (b * nt + i, 0))
    return pl.pallas_call(
        _rw_prep_body,
        grid=(B, nt),
        in_specs=[
            pl.BlockSpec((tt, cols), lambda b, i: (b * nt + i, 0)),
            pl.BlockSpec((1, 1, cols), lambda b, i: (b, 0, 0)),
            full((1, cols)), full((1, G)), full((RW_DECAY_LORA, G)), full((1, G)), full((RW_A_LORA, G)),
            full((RW_GATE_LORA, G)), full((1, G)), full((1, G)), full((G, G)),
        ],
        out_specs=[tile] * 7,
        out_shape=[jax.ShapeDtypeStruct((n, G), F32)] * 7,
        scratch_shapes=[pltpu.VMEM((8, cols), F32)],
        compiler_params=_params(("parallel", "arbitrary"), 40),
        name="rwkv_prep",
    )(zr, shift.reshape(B, 1, cols), row(P['rw_mu']), row(P['rw_w0']), P['rw_w2'].astype(BF16), row(P['rw_a0']),
      P['rw_a2'].astype(BF16), P['rw_g2'].astype(BF16), row(P['rw_kk']), row(P['rw_ka']), ones_bd)


def _rw_scaled(lw, kap, bet, tri):
    cs = _dot_exact_lhs(tri, lw)
    return cs, kap * jnp.exp(cs - lw), bet * jnp.exp(-cs)


def _tri_masks(c):
    row = lax.broadcasted_iota(jnp.int32, (c, c), 0)
    col = lax.broadcasted_iota(jnp.int32, (c, c), 1)
    return row, col


def _chunk_tri(rows, c):
    row, col = _tri_masks(rows)
    return jnp.where((col <= row) & (row // c == col // c), 1.0, 0.0).astype(BF16)


def _rw_ldiag_body(lw_ref, kap_ref, bet_ref, o_ref, *, c):
    rows = lw_ref.shape[0]
    _, kk_all, bt_all = _rw_scaled(lw_ref[...], kap_ref[...], bet_ref[...], _chunk_tri(rows, c))
    srow, scol = _tri_masks(RW_SUB)
    units = [(cc, h) for cc in range(rows // c) for h in range(RW_HEADS)]
    Ls = [_dot_lo(kk_all[cc * c:(cc + 1) * c, h * RW_HD:(h + 1) * RW_HD],
                  bt_all[cc * c:(cc + 1) * c, h * RW_HD:(h + 1) * RW_HD], _NT) for cc, h in units]
    for (cc, h), L in zip(units, Ls):
        for b in range(c // RW_SUB):
            rs = slice(b * RW_SUB, (b + 1) * RW_SUB)
            o_ref[cc * c + b * RW_SUB:cc * c + (b + 1) * RW_SUB, h * RW_SUB:(h + 1) * RW_SUB] = (
                jnp.where(scol < srow, L[rs, rs], 0.0))


def _rw_inv_body(l_ref, t_ref, a_ref, b_ref):
    n = RW_SUB
    nblk = l_ref.shape[0] // n
    for t in range(n):
        a_ref[t] = l_ref[pl.ds(t, nblk, stride=n), :].T
    entry = lambda ref, t, s: ref.at[t, pl.ds(s, RW_HEADS, stride=n), :]
    one = jnp.ones((RW_HEADS, nblk), F32)
    zero = jnp.zeros((RW_HEADS, nblk), F32)
    for t in range(n):
        for s in range(n):
            if s > t:
                entry(b_ref, t, s)[...] = zero
            elif s == t:
                entry(b_ref, t, s)[...] = one
            else:
                acc = entry(a_ref, t, s)[...]
                for j in range(s + 1, t):
                    acc = acc + entry(a_ref, t, j)[...] * entry(b_ref, j, s)[...]
                entry(b_ref, t, s)[...] = -acc
    for t in range(n):
        t_ref[pl.ds(t, nblk, stride=n), :] = b_ref[t].T


def _rw_main_body(r_ref, lw_ref, k_ref, v_ref, kap_ref, bet_ref, g_ref, td_ref, h0_ref, rk_ref, lng_ref, lnb_ref,
                  o_ref, hout_ref, h_ref, *, c):
    ci = pl.program_id(1)
    rows = r_ref.shape[0]
    nb = c // RW_SUB

    @pl.when(ci == 0)
    def _():
        h_ref[...] = h0_ref[0]

    crow, ccol = _tri_masks(c)
    strict = ccol < crow
    incl = ccol <= crow
    lw = lw_ref[...]
    cs, kk_all, bt_all = _rw_scaled(lw, kap_ref[...], bet_ref[...], _chunk_tri(rows, c))
    gi = jnp.exp(-cs)
    gg = jnp.exp(cs)
    kt_all = k_ref[...] * gi
    rt_all = r_ref[...] * gg
    bonus_all = r_ref[...] * k_ref[...] * rk_ref[...]
    hrow = lax.broadcasted_iota(jnp.int32, (RW_HD, RW_HD), 0)
    hcol = lax.broadcasted_iota(jnp.int32, (RW_HD, RW_HD), 1)
    HS = range(RW_HEADS)
    units = [(cc, h) for cc in range(rows // c) for h in HS]
    US = range(len(units))
    rsl = [slice(cc * c, (cc + 1) * c) for cc, _ in units]
    lsl = [slice(h * RW_HD, (h + 1) * RW_HD) for _, h in units]
    Kk = [kk_all[rsl[u], lsl[u]] for u in US]
    Bt = [bt_all[rsl[u], lsl[u]] for u in US]
    Kt = [kt_all[rsl[u], lsl[u]] for u in US]
    Rt = [rt_all[rsl[u], lsl[u]] for u in US]
    vv = [v_ref[rsl[u], lsl[u]] for u in US]
    Lm = [jnp.where(strict, _dot_lo(Kk[u], Bt[u], _NT), 0.0) for u in US]
    A1 = [jnp.where(strict, _dot_lo(Kk[u], Kt[u], _NT), 0.0) for u in US]
    A4 = [jnp.where(incl, _dot_lo(Rt[u], Bt[u], _NT), 0.0) for u in US]
    A3 = [jnp.where(incl, _dot_lo(Rt[u], Kt[u], _NT), 0.0) for u in US]
    X = [jnp.concatenate([Kk[u], _dot_lo(A1[u], vv[u])], axis=1) for u in US]
    zs = [[] for _ in US]
    for b in range(nb):
        rs = slice(b * RW_SUB, (b + 1) * RW_SUB)
        rhs = [X[u][rs] for u in US]
        if b:
            rhs = [rhs[u] - _dot_lo(Lm[u][rs, 0:b * RW_SUB], jnp.concatenate(zs[u], axis=0)) for u in US]
        for u, (cc, h) in enumerate(units):
            tbb = td_ref[cc * c + b * RW_SUB:cc * c + (b + 1) * RW_SUB, h * RW_SUB:(h + 1) * RW_SUB]
            zs[u].append(_dot_lo(tbb, rhs[u]))
    Z = [jnp.concatenate(zs[u], axis=0) if nb > 1 else zs[u][0] for u in US]
    A4Z = [_dot_lo(A4[u], Z[u]) for u in US]
    Rhat = [Rt[u] - A4Z[u][:, :RW_HD] for u in US]
    Yhat = [_dot_lo(A3[u], vv[u]) - A4Z[u][:, RW_HD:] for u in US]
    gC = [gg[(cc + 1) * c - 1:(cc + 1) * c, lsl[u]] for u, (cc, _) in enumerate(units)]
    MN = [_dot_lo(Bt[u] * gC[u], Z[u], _TN) for u in US]
    Mp = [jnp.where(hrow == hcol, gC[u], 0.0) - MN[u][:, :RW_HD] for u in US]
    Np = [_dot_lo(Kt[u] * gC[u], vv[u], _TN) - MN[u][:, RW_HD:] for u in US]
    H = [h_ref[h] for h in HS]
    ys = [None for _ in US]
    for cc in range(rows // c):
        for h in HS:
            u = cc * RW_HEADS + h
            ys[u] = _dot_lo(Rhat[u], H[h]) + Yhat[u]
        H = [_dot_hi(Mp[cc * RW_HEADS + h], H[h]) + Np[cc * RW_HEADS + h] for h in HS]
    for h in HS:
        h_ref[h] = H[h]
    for u in US:
        y = ys[u]
        mu = jnp.mean(y, axis=-1, keepdims=True)
        var = jnp.mean(jnp.square(y - mu), axis=-1, keepdims=True)
        yn = (y - mu) * lax.rsqrt(var + RW_LN_EPS) * lng_ref[:, lsl[u]] + lnb_ref[:, lsl[u]]
        yn = yn + jnp.sum(bonus_all[rsl[u], lsl[u]], axis=-1, keepdims=True) * vv[u]
        o_ref[rsl[u], lsl[u]] = yn * g_ref[rsl[u], lsl[u]]

    @pl.when(ci == pl.num_programs(1) - 1)
    def _():
        hout_ref[0] = h_ref[...]


def _rwkv7(zr, shift, S0, B, T, P):
    G = GROUP_W
    n = B * T
    r, lw, k, v, kap, bet, g = _rw_prep(zr, shift, B, T, P)
    c = min(RW_CHUNK, T)
    rows_l = _row_tile(T, c * RW_LDIAG_CHUNKS)
    rows_m = _row_tile(T, c * RW_MAIN_CHUNKS)
    nl, nc = T // rows_l, T // rows_m
    tile_l = pl.BlockSpec((rows_l, G), lambda b, i: (b * nl + i, 0))
    tile = pl.BlockSpec((rows_m, G), lambda b, i: (b * nc + i, 0))
    ld = pl.pallas_call(
        functools.partial(_rw_ldiag_body, c=c),
        grid=(B, nl),
        in_specs=[tile_l] * 3,
        out_specs=pl.BlockSpec((rows_l, RW_HEADS * RW_SUB), lambda b, i: (b * nl + i, 0)),
        out_shape=jax.ShapeDtypeStruct((n, RW_HEADS * RW_SUB), F32),
        compiler_params=_params(("parallel", "parallel"), 32),
        name="rwkv_ldiag",
    )(lw, kap, bet)
    rows_i = V7X_LANES * RW_SUB
    npad = -(-n // rows_i) * rows_i
    inv_spec = pl.BlockSpec((rows_i, RW_HEADS * RW_SUB), lambda i: (i, 0))
    inv_scratch = pltpu.VMEM((RW_SUB, RW_HEADS * RW_SUB, V7X_LANES), F32)
    td = pl.pallas_call(
        _rw_inv_body,
        grid=(npad // rows_i,),
        in_specs=[inv_spec],
        out_specs=inv_spec,
        out_shape=jax.ShapeDtypeStruct((npad, RW_HEADS * RW_SUB), F32),
        scratch_shapes=[inv_scratch, inv_scratch],
        compiler_params=_params(("parallel",), 32),
        name="rwkv_inv",
    )(jnp.pad(ld, ((0, npad - n), (0, 0))))[:n]
    h0 = jnp.swapaxes(S0, -1, -2)
    prow = lambda x: pl.BlockSpec((1, G), lambda b, i: (0, 0))
    st_spec = pl.BlockSpec((1, RW_HEADS, RW_HD, RW_HD), lambda b, i: (b, 0, 0, 0))
    out, hl = pl.pallas_call(
        functools.partial(_rw_main_body, c=c),
        grid=(B, nc),
        in_specs=[tile] * 7 + [pl.BlockSpec((rows_m, RW_HEADS * RW_SUB), lambda b, i: (b * nc + i, 0)), st_spec,
                               prow(0), prow(0), prow(0)],
        out_specs=[tile, st_spec],
        out_shape=[jax.ShapeDtypeStruct((n, G), F32), jax.ShapeDtypeStruct(S0.shape, F32)],
        scratch_shapes=[pltpu.VMEM((RW_HEADS, RW_HD, RW_HD), F32)],
        compiler_params=_params(("parallel", "arbitrary"), 32),
        name="rwkv_main",
    )(r, lw, k, v, kap, bet, g, td, h0, P['rw_rk'].reshape(1, G), P['rw_ln_g'].reshape(1, G),
      P['rw_ln_b'].reshape(1, G))
    return out, zr.reshape(B, T, -1)[:, -1], jnp.swapaxes(hl, -1, -2)


def _even_mixer(x2, B, T, g, st, P):
    conv_buf, lru_h, k_past, v_past, lf_past = st
    G = GROUP_W
    z_rg, z_qkv, z_og, z_fl = _norm_matmul(x2, g, P['e_w_in'], (2 * G, 3 * G, G, V7X_LANES))
    rnn_out, conv_new, h_last = _lru(z_rg, conv_buf, lru_h, B, T, P)
    qb, kn, kb, v, vb, lf = _fox_prep(z_qkv, z_fl, B, T, P)
    past = k_past.shape[1]
    lf_all = lf.reshape(B, T, V7X_LANES)
    kb_all, vb_all = kb.reshape(B, T, G), vb.reshape(B, T, G)
    if past:
        lf_all = jnp.concatenate([jnp.pad(lf_past, ((0, 0), (0, 0), (0, V7X_LANES - FOX_HEADS))), lf_all], axis=1)
        kb_all = jnp.concatenate([k_past.reshape(B, past, G).astype(BF16), kb_all], axis=1)
        vb_all = jnp.concatenate([v_past.reshape(B, past, G).astype(BF16), vb_all], axis=1)
    tail = ((0, 0), (0, -(past + T) % FOX_K_ROWS), (0, 0))
    ka, vt = _fox_keys(jnp.pad(lf_all, tail), jnp.pad(kb_all, tail), jnp.pad(vb_all, tail))
    fox_out = _fox_attention(qb, ka, vt, z_og, B, T, past)
    x2 = _out_proj(x2, rnn_out, fox_out, P['e_w_out'])
    heads = lambda t: t.reshape(B, T, FOX_HEADS, FOX_HD)
    return x2, (conv_new, h_last, heads(kn), heads(v), lf.reshape(B, T, V7X_LANES)[..., :FOX_HEADS])


def _odd_mixer(x2, B, T, g, st, lb, P):
    S_hg, shift, S_rw = st
    G = GROUP_W
    z_hg, z_rw = _norm_matmul(x2, g, P['o_w_in'], (4 * G, P['o_w_in'].shape[1] - 4 * G))
    hg_out, S_hg_new = _hgrn2(z_hg, lb, S_hg, B, T, P)
    rw_out, shift_new, S_rw_new = _rwkv7(z_rw, shift, S_rw, B, T, P)
    x2 = _out_proj(x2, hg_out, rw_out, P['o_w_out'])
    return x2, (S_hg_new, shift_new, S_rw_new)


def _trunk(x, states, W):
    lru_conv, lru_h, fox_k, fox_v, fox_lf, hg_S, rw_shift, rw_S = states
    B, T, D = x.shape
    depth = W['norm_g'].shape[0]
    sm = jax.nn.softmax(W['hg_lb_logits'], axis=0)
    lower_bounds = jnp.cumsum(sm, axis=0) - sm[0]
    x2 = x.reshape(B * T, D)
    even_new, odd_new = [], []
    for layer in range(depth):
        g = W['norm_g'][layer]
        x2 = _ffn(x2, g[0], W['ffn_w_in'][layer][0], W['ffn_w_out'][layer][0])
        if layer % 2 == 0:
            e = layer // 2
            P = {n: W[n][e] for n in ('e_w_in', 'e_w_out', 'lru_conv_w', 'lru_conv_b', 'lru_wa', 'lru_ba', 'lru_wx',
                                      'lru_bx', 'lru_lambda', 'fox_q_gain', 'fox_k_gain', 'fox_f_bias')}
            x2, new = _even_mixer(x2, B, T, g[1], (lru_conv[e], lru_h[e], fox_k[e], fox_v[e], fox_lf[e]), P)
            even_new.append(new)
        else:
            o = layer // 2
            P = {n: W[n][o] for n in ('o_w_in', 'o_w_out', 'hg_norm_g', 'rw_mu', 'rw_w0', 'rw_w2', 'rw_a0', 'rw_a2',
                                      'rw_g2', 'rw_kk', 'rw_ka', 'rw_rk', 'rw_ln_g', 'rw_ln_b')}
            x2, new = _odd_mixer(x2, B, T, g[1], (hg_S[o], rw_shift[o], rw_S[o]), lower_bounds[layer], P)
            odd_new.append(new)
        x2 = _ffn(x2, g[2], W['ffn_w_in'][layer][1], W['ffn_w_out'][layer][1])
    ev = [jnp.stack([n[j] for n in even_new]) for j in range(5)]
    od = [jnp.stack([n[j] for n in odd_new]) for j in range(3)]
    return x2.reshape(B, T, D), (ev[0], ev[1], ev[2], ev[3], ev[4], od[0], od[1], od[2])


def kernel(x_prompt, x_sample, state_lru_conv, state_lru_h, cache_fox_k, cache_fox_v, cache_fox_logf,
           state_hgrn_S, state_rwkv_shift, state_rwkv_S, norm_g, ffn_w_in, ffn_w_out, e_w_in, e_w_out,
           lru_conv_w, lru_conv_b, lru_wa, lru_ba, lru_wx, lru_bx, lru_lambda, fox_q_gain, fox_k_gain,
           fox_f_bias, o_w_in, o_w_out, hg_lb_logits, hg_norm_g, rw_mu, rw_w0, rw_w2, rw_a0, rw_a2, rw_g2,
           rw_kk, rw_ka, rw_rk, rw_ln_g, rw_ln_b):
    n_even, n_odd = e_w_in.shape[0], o_w_in.shape[0]
    W = dict(norm_g=norm_g, ffn_w_in=_ffn_w_in_tiles(ffn_w_in), ffn_w_out=ffn_w_out.astype(BF16),
             e_w_in=_pad_cols(e_w_in.astype(BF16)), e_w_out=e_w_out.astype(BF16),
             lru_conv_w=lru_conv_w, lru_conv_b=lru_conv_b, lru_wa=lru_wa, lru_ba=lru_ba, lru_wx=lru_wx,
             lru_bx=lru_bx, lru_lambda=lru_lambda, fox_q_gain=fox_q_gain, fox_k_gain=fox_k_gain,
             fox_f_bias=fox_f_bias, o_w_in=o_w_in.astype(BF16), o_w_out=o_w_out.astype(BF16),
             hg_lb_logits=hg_lb_logits, hg_norm_g=hg_norm_g, rw_mu=rw_mu, rw_w0=rw_w0, rw_w2=rw_w2, rw_a0=rw_a0,
             rw_a2=rw_a2, rw_g2=rw_g2, rw_kk=rw_kk, rw_ka=rw_ka, rw_rk=rw_rk, rw_ln_g=rw_ln_g, rw_ln_b=rw_ln_b)
    nb = x_prompt.shape[0]
    dt = x_prompt.dtype
    prompt_states = (jnp.zeros((n_even, nb, CONV_W - 1, GROUP_W), dt),
                     jnp.zeros((n_even, nb, GROUP_W), dt),
                     jnp.zeros((n_even, nb, 0, FOX_HEADS, FOX_HD), dt),
                     jnp.zeros((n_even, nb, 0, FOX_HEADS, FOX_HD), dt),
                     jnp.zeros((n_even, nb, 0, FOX_HEADS), dt),
                     jnp.zeros((n_odd, nb, HG_HEADS, GROUP_W // HG_HEADS, GROUP_W // HG_HEADS), dt),
                     jnp.zeros((n_odd, nb, rw_mu.shape[1]), dt),
                     jnp.zeros((n_odd, nb, RW_HEADS, RW_HD, RW_HD), dt))
    sample_states = (state_lru_conv, state_lru_h, cache_fox_k, cache_fox_v, cache_fox_logf,
                     state_hgrn_S, state_rwkv_shift, state_rwkv_S)
    y_prompt, p_new = _trunk(x_prompt, prompt_states, W)
    y_sample, s_new = _trunk(x_sample, sample_states, W)
    lru_conv_p, lru_h_p, fox_k_p, fox_v_p, fox_logf_p, hgrn_S_p, rwkv_shift_p, rwkv_S_p = p_new
    lru_conv_s, lru_h_s, fox_k_s, fox_v_s, fox_logf_s, hgrn_S_s, rwkv_shift_s, rwkv_S_s = s_new
    return (y_prompt, y_sample, lru_conv_p, lru_conv_s, lru_h_p, lru_h_s, fox_k_p, fox_k_s, fox_v_p, fox_v_s,
            fox_logf_p, fox_logf_s, hgrn_S_p, hgrn_S_s, rwkv_shift_p, rwkv_shift_s, rwkv_S_p, rwkv_S_s)
```

```python
import functools

import jax
import jax.numpy as jnp
from jax import lax
from jax.experimental import pallas as pl
from jax.experimental.pallas import tpu as pltpu

F32 = jnp.float32
BF16 = jnp.bfloat16

NORM_EPS = 1e-6
GROUP_W = 512
CONV_W = 4
LRU_C = 8.0
FOX_HEADS = 8
FOX_HD = 64
HG_HEADS = 4
RW_HEADS = 8
RW_HD = 64
RW_DECAY_LORA = 64
RW_A_LORA = 64
RW_GATE_LORA = 128
RW_LN_EPS = 64e-5

V7X_LANES = 128
FFN_COL_TILE = 1408
PREP_ROWS = 512
FFN_ROW_TILE = 1024
FFN_VMEM_MIB = 60


def _row_tile(n, want):
    t = min(n, want)
    while n % t:
        t //= 2
    return t


def _params(sem, vmem_mib):
    return pltpu.CompilerParams(dimension_semantics=sem, vmem_limit_bytes=vmem_mib << 20)


def _pad_cols(w):
    pad = -w.shape[-1] % V7X_LANES
    return jnp.pad(w, [(0, 0)] * (w.ndim - 1) + [(0, pad)])


def _rms(x, g):
    return x * lax.rsqrt(jnp.mean(x * x, axis=-1, keepdims=True) + NORM_EPS) * g


def _ffn_body(x_ref, g_ref, wi_ref, wo_ref, o_ref, h_ref, acc_ref):
    j = pl.program_id(1)

    @pl.when(j == 0)
    def _():
        h_ref[...] = _rms(x_ref[...], g_ref[...]).astype(BF16)
        acc_ref[...] = jnp.zeros_like(acc_ref)

    tf = wo_ref.shape[0]
    gu = jnp.dot(h_ref[...], wi_ref[...], preferred_element_type=F32)
    gate, up = gu[:, :tf], gu[:, tf:]
    act = (gate * jax.nn.sigmoid(gate) * up).astype(BF16)
    acc_ref[...] += jnp.dot(act, wo_ref[...], preferred_element_type=F32)

    @pl.when(j == pl.num_programs(1) - 1)
    def _():
        o_ref[...] = x_ref[...] + 0.5 * acc_ref[...]


def _cast_body(x_ref, o_ref):
    o_ref[...] = x_ref[...].astype(o_ref.dtype)


def _ffn_w_in_tiles(w_in):
    *lead, d, f2 = w_in.shape
    tf = FFN_COL_TILE
    nf = f2 // 2 // tf
    w = w_in.reshape(-1, d, f2)
    out = pl.pallas_call(
        _cast_body,
        grid=(w.shape[0], nf, 2),
        in_specs=[pl.BlockSpec((1, d, tf), lambda i, j, gu: (i, 0, gu * nf + j))],
        out_specs=pl.BlockSpec((1, d, tf), lambda i, j, gu: (i, 0, 2 * j + gu)),
        out_shape=jax.ShapeDtypeStruct(w.shape, BF16),
        compiler_params=_params(("parallel", "parallel", "parallel"), 32),
        name="ffn_weight_tiles",
    )(w)
    return out.reshape(*lead, d, f2)


def _ffn(x, g, w_in, w_out):
    n, d = x.shape
    f = w_out.shape[0]
    tm = _row_tile(n, FFN_ROW_TILE)
    tf = FFN_COL_TILE
    nf = f // tf
    return pl.pallas_call(
        _ffn_body,
        grid=(n // tm, nf),
        in_specs=[
            pl.BlockSpec((tm, d), lambda i, j: (i, 0)),
            pl.BlockSpec((1, d), lambda i, j: (0, 0)),
            pl.BlockSpec((d, 2 * tf), lambda i, j: (0, j)),
            pl.BlockSpec((tf, d), lambda i, j: (j, 0)),
        ],
        out_specs=pl.BlockSpec((tm, d), lambda i, j: (i, 0)),
        out_shape=jax.ShapeDtypeStruct((n, d), F32),
        scratch_shapes=[pltpu.VMEM((tm, d), BF16), pltpu.VMEM((tm, d), F32)],
        compiler_params=_params(("parallel", "arbitrary"), FFN_VMEM_MIB),
        name="ffn",
    )(x, g.reshape(1, d), w_in, w_out)


def _norm_matmul_body(x_ref, g_ref, w_ref, *o_refs):
    h = _rms(x_ref[...], g_ref[...]).astype(BF16)
    z = jnp.dot(h, w_ref[...], preferred_element_type=F32)
    start = 0
    for o_ref in o_refs:
        width = o_ref.shape[1]
        o_ref[...] = z[:, start:start + width]
        start += width


def _norm_matmul(x, g, w, widths):
    n, d = x.shape
    c = w.shape[1]
    assert sum(widths) == c and all(wd % V7X_LANES == 0 for wd in widths)
    tm = _row_tile(n, 512)
    return pl.pallas_call(
        _norm_matmul_body,
        grid=(n // tm,),
        in_specs=[
            pl.BlockSpec((tm, d), lambda i: (i, 0)),
            pl.BlockSpec((1, d), lambda i: (0, 0)),
            pl.BlockSpec((d, c), lambda i: (0, 0)),
        ],
        out_specs=[pl.BlockSpec((tm, wd), lambda i: (i, 0)) for wd in widths],
        out_shape=[jax.ShapeDtypeStruct((n, wd), F32) for wd in widths],
        compiler_params=_params(("parallel",), 48),
        name="norm_matmul",
    )(x, g.reshape(1, d), w)


def _out_proj_body(x_ref, a_ref, b_ref, wa_ref, wb_ref, o_ref):
    acc = jnp.dot(a_ref[...].astype(BF16), wa_ref[...], preferred_element_type=F32)
    acc += jnp.dot(b_ref[...].astype(BF16), wb_ref[...], preferred_element_type=F32)
    o_ref[...] = x_ref[...] + acc


def _out_proj(x, a, b, w):
    n, d = x.shape
    ga, gb = a.shape[1], b.shape[1]
    tm = _row_tile(n, 512)
    return pl.pallas_call(
        _out_proj_body,
        grid=(n // tm,),
        in_specs=[
            pl.BlockSpec((tm, d), lambda i: (i, 0)),
            pl.BlockSpec((tm, ga), lambda i: (i, 0)),
            pl.BlockSpec((tm, gb), lambda i: (i, 0)),
            pl.BlockSpec((ga, d), lambda i: (0, 0)),
            pl.BlockSpec((gb, d), lambda i: (0, 0)),
        ],
        out_specs=pl.BlockSpec((tm, d), lambda i: (i, 0)),
        out_shape=jax.ShapeDtypeStruct((n, d), F32),
        compiler_params=_params(("parallel",), 32),
        name="out_proj",
    )(x, a, b, w[:ga], w[ga:])


LRU_ROWS = 256
CONV_PAD = 8


def _expm1(x):
    series = x * (1.0 + x * (1 / 2 + x * (1 / 6 + x * (1 / 24 + x * (1 / 120 + x * (1 / 720 + x * (1 / 5040 + x * (1 / 40320))))))))
    return jnp.where(jnp.abs(x) < 0.25, series, jnp.exp(x) - 1.0)


def _shift_rows(x, s, fill):
    row = lax.broadcasted_iota(jnp.int32, x.shape, 0)
    return jnp.where(row >= s, pltpu.roll(x, s, axis=0), fill)


def _lru_body(z_ref, buf_ref, h0_ref, cw_ref, cb_ref, wa_ref, ba_ref, wx_ref, bx_ref, lam_ref,
              o_ref, bufo_ref, ho_ref, x_ref, hc_ref):
    G = GROUP_W
    tt = z_ref.shape[0]

    @pl.when(pl.program_id(1) == 0)
    def _():
        x_ref[0:CONV_PAD, :] = buf_ref[0]
        hc_ref[...] = jnp.broadcast_to(h0_ref[0], hc_ref.shape)

    x_ref[CONV_PAD:CONV_PAD + tt, :] = z_ref[:, 0:G]
    xc = cb_ref[...]
    for j in range(CONV_W):
        lo = CONV_PAD - (CONV_W - 1) + j
        xc = xc + x_ref[lo:lo + tt, :] * cw_ref[j:j + 1, :]
    hist = x_ref[tt:tt + CONV_PAD, :]
    x_ref[0:CONV_PAD, :] = hist
    bufo_ref[0] = hist

    xb = xc.astype(BF16)
    r = jax.nn.sigmoid(jnp.dot(xb, wa_ref[...], preferred_element_type=F32) + ba_ref[...])
    ig = jax.nn.sigmoid(jnp.dot(xb, wx_ref[...], preferred_element_type=F32) + bx_ref[...])
    log_a = (-LRU_C * _softplus(-lam_ref[...])) * r
    a = jnp.exp(log_a)
    b = jnp.sqrt(-_expm1(2.0 * log_a)) * (ig * xc)
    s = 1
    while s < tt:
        if s % 8:
            b = a * _shift_rows(b, s, 0.0) + b
            a = a * _shift_rows(a, s, 1.0)
        else:
            b = jnp.concatenate([b[:s], a[s:] * b[:tt - s] + b[s:]], axis=0)
            a = jnp.concatenate([a[:s], a[s:] * a[:tt - s]], axis=0)
        s *= 2
    h = a * hc_ref[0:1, :] + b
    hc_ref[...] = jnp.broadcast_to(h[tt - 1:tt, :], hc_ref.shape)
    ho_ref[0] = h[tt - 1:tt, :]
    o_ref[...] = jax.nn.gelu(z_ref[:, G:2 * G]) * h


def _block_diag_dense(w):
    nb, bs, _ = w.shape
    eye = jnp.eye(nb, dtype=w.dtype)
    return (eye[:, None, :, None] * w[:, :, None, :]).reshape(nb * bs, nb * bs)


def _lru(z_rg, conv_buf, h0, B, T, P):
    G = GROUP_W
    n = B * T
    tt = _row_tile(T, LRU_ROWS)
    nt = T // tt
    buf = jnp.pad(conv_buf, ((0, 0), (CONV_PAD - (CONV_W - 1), 0), (0, 0)))
    cw = jnp.pad(P['lru_conv_w'], ((0, CONV_PAD - CONV_W), (0, 0)))
    row = lambda x: x.reshape(1, G)
    full = lambda shape: pl.BlockSpec(shape, lambda b, i: (0,) * len(shape))
    out, bufo, ho = pl.pallas_call(
        _lru_body,
        grid=(B, nt),
        in_specs=[
            pl.BlockSpec((tt, 2 * G), lambda b, i: (b * nt + i, 0)),
            pl.BlockSpec((1, CONV_PAD, G), lambda b, i: (b, 0, 0)),
            pl.BlockSpec((1, 1, G), lambda b, i: (b, 0, 0)),
            full((CONV_PAD, G)), full((1, G)), full((G, G)), full((1, G)), full((G, G)), full((1, G)), full((1, G)),
        ],
        out_specs=[
            pl.BlockSpec((tt, G), lambda b, i: (b * nt + i, 0)),
            pl.BlockSpec((1, CONV_PAD, G), lambda b, i: (b, 0, 0)),
            pl.BlockSpec((1, 1, G), lambda b, i: (b, 0, 0)),
        ],
        out_shape=[jax.ShapeDtypeStruct((n, G), F32), jax.ShapeDtypeStruct((B, CONV_PAD, G), F32),
                   jax.ShapeDtypeStruct((B, 1, G), F32)],
        scratch_shapes=[pltpu.VMEM((tt + CONV_PAD, G), F32), pltpu.VMEM((8, G), F32)],
        compiler_params=_params(("parallel", "arbitrary"), 32),
        name="lru",
    )(z_rg, buf, h0.reshape(B, 1, G), cw, row(P['lru_conv_b']), _block_diag_dense(P['lru_wa']).astype(BF16),
      row(P['lru_ba']), _block_diag_dense(P['lru_wx']).astype(BF16), row(P['lru_bx']), row(P['lru_lambda']))
    return out, bufo[:, CONV_PAD - (CONV_W - 1):], ho.reshape(B, G)


FOX_Q_COLS = 1024
FOX_K_ROWS = 512
FOX_F_SPLIT = 3
FOX_NEG = -1e30
LOG2E = 1.4426950408889634
HEAD_PAIRS = FOX_HEADS // 2
PAIR_W = 2 * FOX_HD
FOX_KEY_TILE = 512


def _fox_prep_body(z_ref, fl_ref, qg_ref, kg_ref, fb_ref, ones_ref, q_ref, k_ref, kb_ref, v_ref, vb_ref, lf_ref,
                   *, q_transposed):
    G = GROUP_W
    q, k, v = z_ref[:, 0:G], z_ref[:, G:2 * G], z_ref[:, 2 * G:3 * G]
    inv = 1.0 / FOX_HD
    qn = q * lax.rsqrt(_dot_exact_rhs(q * q, ones_ref[...]) * inv + NORM_EPS) * qg_ref[...]
    kn = k * lax.rsqrt(_dot_exact_rhs(k * k, ones_ref[...]) * inv + NORM_EPS) * kg_ref[...]
    qs = qn * (LOG2E * FOX_HD ** -0.5)
    if q_transposed:
        for p in range(HEAD_PAIRS):
            q_ref[0, p] = qs[:, p * PAIR_W:(p + 1) * PAIR_W].T.astype(BF16)
    else:
        q_ref[...] = qs.astype(BF16)
    k_ref[...] = kn
    kb_ref[...] = kn.astype(BF16)
    v_ref[...] = v
    vb_ref[...] = v.astype(BF16)
    x = fl_ref[...] + fb_ref[...]
    lf_ref[...] = -_softplus(-x)


def _fox_prep(z_qkv, z_fl, B, T, P):
    n = z_qkv.shape[0]
    G = GROUP_W
    tt = _row_tile(T, PREP_ROWS)
    nt = T // tt
    q_transposed = tt % V7X_LANES == 0
    ones_bd = jnp.kron(jnp.eye(FOX_HEADS, dtype=F32), jnp.ones((FOX_HD, FOX_HD), F32)).astype(BF16)
    fb = jnp.pad(P['fox_f_bias'], (0, V7X_LANES - FOX_HEADS)).reshape(1, V7X_LANES)
    tile = lambda w: pl.BlockSpec((tt, w), lambda b, i: (b * nt + i, 0))
    full = lambda shape: pl.BlockSpec(shape, lambda b, i: (0,) * len(shape))
    if q_transposed:
        q_spec = pl.BlockSpec((1, HEAD_PAIRS, PAIR_W, tt), lambda b, i: (b, 0, 0, i))
        q_shape = jax.ShapeDtypeStruct((B, HEAD_PAIRS, PAIR_W, T), BF16)
    else:
        q_spec, q_shape = tile(G), jax.ShapeDtypeStruct((n, G), BF16)
    return pl.pallas_call(
        functools.partial(_fox_prep_body, q_transposed=q_transposed),
        grid=(B, nt),
        in_specs=[tile(3 * G), tile(V7X_LANES), full((1, G)), full((1, G)), full((1, V7X_LANES)), full((G, G))],
        out_specs=[q_spec] + [tile(G)] * 4 + [tile(V7X_LANES)],
        out_shape=[q_shape, jax.ShapeDtypeStruct((n, G), F32),
                   jax.ShapeDtypeStruct((n, G), BF16), jax.ShapeDtypeStruct((n, G), F32),
                   jax.ShapeDtypeStruct((n, G), BF16), jax.ShapeDtypeStruct((n, V7X_LANES), F32)],
        compiler_params=_params(("parallel", "parallel"), 32),
        name="fox_prep",
    )(z_qkv, z_fl, jnp.tile(P['fox_q_gain'], FOX_HEADS).reshape(1, G),
      jnp.tile(P['fox_k_gain'], FOX_HEADS).reshape(1, G), fb, ones_bd)


def _fox_keys_body(lf_ref, kb_ref, vb_ref, ka_ref, vt_ref, c_ref):
    tt = lf_ref.shape[1]

    @pl.when(pl.program_id(1) == 0)
    def _():
        c_ref[...] = jnp.zeros_like(c_ref)

    row, col = _tri_masks(tt)
    tri = jnp.where(col <= row, 1.0, 0.0).astype(BF16)
    f = _dot_exact_lhs(tri, lf_ref[0]) + c_ref[0:1, :]
    c_ref[...] = jnp.broadcast_to(f[tt - 1:tt, :], c_ref.shape)
    parts = _split3(f * LOG2E)
    srow = lax.broadcasted_iota(jnp.int32, (V7X_LANES, FOX_HD), 0)
    scol = lax.broadcasted_iota(jnp.int32, (V7X_LANES, FOX_HD), 1)
    for h in range(FOX_HEADS):
        aug = jnp.zeros((tt, FOX_HD), F32)
        for t, part in enumerate(parts):
            sel = jnp.where((srow == h) & (scol == t), 1.0, 0.0).astype(BF16)
            aug = aug + jnp.dot(part, sel, preferred_element_type=F32)
        ka_ref[0, h] = jnp.concatenate([kb_ref[0, :, h * FOX_HD:(h + 1) * FOX_HD], aug.astype(BF16)], axis=1)
    for p in range(HEAD_PAIRS):
        vt_ref[0, p] = vb_ref[0, :, p * PAIR_W:(p + 1) * PAIR_W].astype(F32).T.astype(BF16)


def _fox_keys(lf_all, kb_all, vb_all):
    B, tk_all, L = lf_all.shape
    G = GROUP_W
    tt = FOX_KEY_TILE
    return pl.pallas_call(
        _fox_keys_body,
        grid=(B, tk_all // tt),
        in_specs=[pl.BlockSpec((1, tt, L), lambda b, i: (b, i, 0)),
                  pl.BlockSpec((1, tt, G), lambda b, i: (b, i, 0)),
                  pl.BlockSpec((1, tt, G), lambda b, i: (b, i, 0))],
        out_specs=[pl.BlockSpec((1, FOX_HEADS, tt, 2 * FOX_HD), lambda b, i: (b, 0, i, 0)),
                   pl.BlockSpec((1, HEAD_PAIRS, PAIR_W, tt), lambda b, i: (b, 0, 0, i))],
        out_shape=[jax.ShapeDtypeStruct((B, FOX_HEADS, tk_all, 2 * FOX_HD), BF16),
                   jax.ShapeDtypeStruct((B, HEAD_PAIRS, PAIR_W, tk_all), BF16)],
        scratch_shapes=[pltpu.VMEM((8, L), F32)],
        compiler_params=_params(("parallel", "arbitrary"), 32),
        name="fox_keys",
    )(lf_all, kb_all, vb_all)


def _fox_attn_body(qt_ref, ka_ref, vt_ref, og_ref, o_ref, acc_ref, *, past, tk, t_real):
    qi = pl.program_id(2)
    tq = qt_ref.shape[3]
    t_out = o_ref.shape[0]
    first_q = past + qi * tq
    last_q = past + jnp.minimum(qi * tq + tq, t_real) - 1
    n_full = (first_q + 1) // tk
    n_all = last_q // tk + 1
    drow = lax.broadcasted_iota(jnp.int32, (FOX_HD, tq), 0)
    minus = jnp.where(drow < FOX_F_SPLIT, -1.0, 0.0).astype(BF16)
    rhs = [jnp.concatenate([qt_ref[0, 0, h * FOX_HD:(h + 1) * FOX_HD, :], minus], axis=0) for h in range(2)]
    acc_ref[...] = jnp.zeros_like(acc_ref)
    krow = lax.broadcasted_iota(jnp.int32, (tk, tq), 0)
    qcol = lax.broadcasted_iota(jnp.int32, (tk, tq), 1)

    def block(ki, c, masked):
        ks = pl.multiple_of(ki * tk, tk)
        s = [jnp.dot(ka_ref[0, h, pl.ds(ks, tk), :], rhs[h], preferred_element_type=F32) for h in range(2)]
        if masked:
            vis = ks + krow <= first_q + qcol
            s = [jnp.where(vis, s[h], FOX_NEG) for h in range(2)]
        m_new = [jnp.maximum(c[h][0], jnp.max(s[h], axis=0, keepdims=True)) for h in range(2)]
        alpha = [jnp.exp2(c[h][0] - m_new[h]) for h in range(2)]
        p = [jnp.exp2(s[h] - m_new[h]) for h in range(2)]
        l_new = [alpha[h] * c[h][1] + jnp.sum(p[h], axis=0, keepdims=True) for h in range(2)]
        vt = [vt_ref[0, 0, h * FOX_HD:(h + 1) * FOX_HD, pl.ds(ks, tk)] for h in range(2)]
        pv = [jnp.dot(vt[h], p[h].astype(BF16), preferred_element_type=F32) for h in range(2)]
        for h in range(2):
            acc_ref[h] = alpha[h] * acc_ref[h] + pv[h]
        return tuple((m_new[h], l_new[h]) for h in range(2))

    init = tuple((jnp.full((1, tq), FOX_NEG, F32), jnp.zeros((1, tq), F32)) for _ in range(2))
    c = lax.fori_loop(0, n_full, lambda ki, c: block(ki, c, False), init)
    c = lax.fori_loop(n_full, n_all, lambda ki, c: block(ki, c, True), c)
    o_t = jnp.concatenate([acc_ref[0] / c[0][1], acc_ref[1] / c[1][1]], axis=0)
    o_ref[...] = o_t.T[:t_out] * jax.nn.sigmoid(og_ref[...])


def _fox_attention(q, ka, vt, z_og, B, T, past):
    G = GROUP_W
    pw = PAIR_W
    tq = max(_row_tile(T, FOX_Q_COLS), V7X_LANES)
    tqp = -(-T // tq) * tq
    nq = tqp // tq
    t_out = min(tq, T)
    tk = FOX_K_ROWS
    tkp = ka.shape[2]
    if q.ndim == 2:
        q = q.reshape(B, T, HEAD_PAIRS, pw).transpose(0, 2, 3, 1)
    qt = jnp.pad(q, ((0, 0), (0, 0), (0, 0), (0, tqp - T)))
    return pl.pallas_call(
        functools.partial(_fox_attn_body, past=past, tk=tk, t_real=T),
        grid=(B, HEAD_PAIRS, nq),
        in_specs=[
            pl.BlockSpec((1, 1, pw, tq), lambda b, p, i: (b, p, 0, i)),
            pl.BlockSpec((1, 2, tkp, 2 * FOX_HD), lambda b, p, i: (b, p, 0, 0)),
            pl.BlockSpec((1, 1, pw, tkp), lambda b, p, i: (b, p, 0, 0)),
            pl.BlockSpec((t_out, pw), lambda b, p, i: (b * nq + i, p)),
        ],
        out_specs=pl.BlockSpec((t_out, pw), lambda b, p, i: (b * nq + i, p)),
        out_shape=jax.ShapeDtypeStruct((B * T, G), F32),
        scratch_shapes=[pltpu.VMEM((2, FOX_HD, tq), F32)],
        compiler_params=_params(("parallel", "parallel", "arbitrary"), 40),
        name="fox_attn",
    )(qt, ka, vt, z_og)


HG_CHUNK = 64
HG_STEP_CHUNKS = 8


def _hgrn_body(z_ref, lb_ref, s0_ref, ng_ref, o_ref, so_ref, st_ref, *, c):
    G = GROUP_W
    rows = z_ref.shape[0]
    nch = rows // c
    dk = G // HG_HEADS

    @pl.when(pl.program_id(1) == 0)
    def _():
        st_ref[...] = s0_ref[0]

    lb = lb_ref[...]
    f = lb + (1.0 - lb) * jax.nn.sigmoid(z_ref[:, G:2 * G])
    kx = 1.0 - f
    crow, ccol = _tri_masks(c)
    incl = ccol <= crow
    gs = _dot_exact_lhs(_chunk_tri(rows, c), jnp.log(f))
    qg_all = z_ref[:, 0:G] * jnp.exp(gs)
    kg_all = kx * jnp.exp(-gs)
    HS = range(HG_HEADS)
    units = [(cc, h) for cc in range(nch) for h in HS]
    US = range(len(units))
    rsl = [slice(cc * c, (cc + 1) * c) for cc, _ in units]
    lsl = [slice(h * dk, (h + 1) * dk) for _, h in units]
    g_last = [gs[(cc + 1) * c - 1:(cc + 1) * c, lsl[u]] for u, (cc, _) in enumerate(units)]
    vv = [z_ref[rsl[u], 2 * G + h * dk:2 * G + (h + 1) * dk] for u, (_, h) in enumerate(units)]
    A = [jnp.where(incl, _dot_lo(qg_all[rsl[u], lsl[u]], kg_all[rsl[u], lsl[u]], _NT), 0.0) for u in US]
    av = [_dot_lo(A[u], vv[u]) for u in US]
    kd = [kx[rsl[u], lsl[u]] * jnp.exp(g_last[u] - gs[rsl[u], lsl[u]]) for u in US]
    upd = [_dot_lo(vv[u], kd[u], _TN) for u in US]
    st = [st_ref[h] for h in HS]
    o = [None for _ in US]
    for cc in range(nch):
        for h in HS:
            u = cc * HG_HEADS + h
            o[u] = _dot_lo(qg_all[rsl[u], lsl[u]], st[h], _NT) + av[u]
        st = [st[h] * jnp.exp(g_last[cc * HG_HEADS + h]) + upd[cc * HG_HEADS + h] for h in HS]
    for h in HS:
        st_ref[h] = st[h]
    for u, (_, h) in enumerate(units):
        hg = z_ref[rsl[u], 3 * G + h * dk:3 * G + (h + 1) * dk]
        o_ref[rsl[u], lsl[u]] = _rms(o[u], ng_ref[:, lsl[u]]) * (hg * jax.nn.sigmoid(hg))

    @pl.when(pl.program_id(1) == pl.num_programs(1) - 1)
    def _():
        so_ref[0] = st_ref[...]


def _hgrn2(z_hg, lb, S0, B, T, P):
    G = GROUP_W
    c = min(HG_CHUNK, T)
    rows = _row_tile(T, c * HG_STEP_CHUNKS)
    nc = T // rows
    dk = G // HG_HEADS
    st_spec = pl.BlockSpec((1, HG_HEADS, dk, dk), lambda b, i: (b, 0, 0, 0))
    out, so = pl.pallas_call(
        functools.partial(_hgrn_body, c=c),
        grid=(B, nc),
        in_specs=[pl.BlockSpec((rows, 4 * G), lambda b, i: (b * nc + i, 0)),
                  pl.BlockSpec((1, G), lambda b, i: (0, 0)), st_spec, pl.BlockSpec((1, G), lambda b, i: (0, 0))],
        out_specs=[pl.BlockSpec((rows, G), lambda b, i: (b * nc + i, 0)), st_spec],
        out_shape=[jax.ShapeDtypeStruct((B * T, G), F32), jax.ShapeDtypeStruct(S0.shape, F32)],
        scratch_shapes=[pltpu.VMEM((HG_HEADS, dk, dk), F32)],
        compiler_params=_params(("parallel", "arbitrary"), 32),
        name="hgrn2",
    )(z_hg, lb.reshape(1, G), jnp.swapaxes(S0, -1, -2), P['hg_norm_g'].reshape(1, G))
    return out, jnp.swapaxes(so, -1, -2)


RW_CHUNK = 64
RW_SUB = 16
RW_LDIAG_CHUNKS = 4
RW_MAIN_CHUNKS = 2

_NT = (((1,), (1,)), ((), ()))
_TN = (((0,), (0,)), ((), ()))
_NN = (((1,), (0,)), ((), ()))


def _split3(x):
    h1 = x.astype(BF16)
    r1 = x - h1.astype(F32)
    h2 = r1.astype(BF16)
    h3 = (r1 - h2.astype(F32)).astype(BF16)
    return h1, h2, h3


def _dot_lo(a, b, dims=_NN):
    return lax.dot_general(a.astype(BF16), b.astype(BF16), dims, preferred_element_type=F32)


def _dot_hi(a, b, dims=_NN):
    ah = a.astype(BF16)
    al = (a - ah.astype(F32)).astype(BF16)
    bh = b.astype(BF16)
    bl = (b - bh.astype(F32)).astype(BF16)
    d = functools.partial(lax.dot_general, dimension_numbers=dims, preferred_element_type=F32)
    return d(ah, bh) + (d(al, bh) + d(ah, bl))


def _dot_exact_rhs(a, b):
    h1, h2, h3 = _split3(a)
    d = functools.partial(jnp.dot, preferred_element_type=F32)
    return d(h1, b) + (d(h2, b) + d(h3, b))


def _dot_exact_lhs(a, b):
    h1, h2, h3 = _split3(b)
    d = functools.partial(jnp.dot, preferred_element_type=F32)
    return d(a, h1) + (d(a, h2) + d(a, h3))


def _softplus(x):
    return jnp.maximum(x, 0.0) + jnp.log1p(jnp.exp(-jnp.abs(x)))


def _rw_prep_body(z_ref, shift_ref, mu_ref, w0_ref, w2_ref, a0_ref, a2_ref, g2_ref, kk_ref, ka_ref, ones_ref,
                  r_ref, lw_ref, k_ref, v_ref, kap_ref, bet_ref, g_ref, prev_ref):
    G = GROUP_W

    @pl.when(pl.program_id(1) == 0)
    def _():
        prev_ref[0:1, :] = shift_ref[0]

    z = z_ref[...]
    tt = z.shape[0]
    row = lax.broadcasted_iota(jnp.int32, z.shape, 0)
    shifted = jnp.where(row == 0, prev_ref[0:1, :], pltpu.roll(z, 1, axis=0))
    prev_ref[0:1, :] = z[tt - 1:tt, :]
    zm = z + (shifted - z) * mu_ref[...]
    r, k, v = zm[:, 0:G], zm[:, G:2 * G], zm[:, 2 * G:3 * G]
    o = 3 * G
    wd = zm[:, o:o + RW_DECAY_LORA]
    ad = zm[:, o + RW_DECAY_LORA:o + RW_DECAY_LORA + RW_A_LORA]
    gd = zm[:, o + RW_DECAY_LORA + RW_A_LORA:]
    w = -_softplus(-(w0_ref[...] + _dot_lo(jnp.tanh(wd), w2_ref[...]))) - 0.5
    a = jax.nn.sigmoid(a0_ref[...] + _dot_lo(ad, a2_ref[...]))
    kk = k * kk_ref[...]
    ss = _dot_exact_rhs(kk * kk, ones_ref[...])
    kap = kk / jnp.maximum(jnp.sqrt(ss), 1e-12)
    r_ref[...] = r
    lw_ref[...] = -jnp.exp(w)
    k_ref[...] = k * (1.0 + (a - 1.0) * ka_ref[...])
    v_ref[...] = v
    kap_ref[...] = kap
    bet_ref[...] = kap * a
    g_ref[...] = _dot_lo(jax.nn.sigmoid(gd), g2_ref[...])


def _rw_prep(zr, shift, B, T, P):
    n, cols = zr.shape
    G = GROUP_W
    tt = _row_tile(T, PREP_ROWS)
    nt = T // tt
    ones_bd = jnp.kron(jnp.eye(RW_HEADS, dtype=F32), jnp.ones((RW_HD, RW_HD), F32)).astype(BF16)
    row = lambda x: x.reshape(1, -1)
    full = lambda shape: pl.BlockSpec(shape, lambda b, i: (0,) * len(shape))
    tile = pl.BlockSpec((tt, G), lambda b, i: (b * nt + i, 0))
    return pl.pallas_call(
        _rw_prep_body,
        grid=(B, nt),
        in_specs=[
            pl.BlockSpec((tt, cols), lambda b, i: (b * nt + i, 0)),
            pl.BlockSpec((1, 1, cols), lambda b, i: (b, 0, 0)),
            full((1, cols)), full((1, G)), full((RW_DECAY_LORA, G)), full((1, G)), full((RW_A_LORA, G)),
            full((RW_GATE_LORA, G)), full((1, G)), full((1, G)), full((G, G)),
        ],
        out_specs=[tile] * 7,
        out_shape=[jax.ShapeDtypeStruct((n, G), F32)] * 7,
        scratch_shapes=[pltpu.VMEM((8, cols), F32)],
        compiler_params=_params(("parallel", "arbitrary"), 40),
        name="rwkv_prep",
    )(zr, shift.reshape(B, 1, cols), row(P['rw_mu']), row(P['rw_w0']), P['rw_w2'].astype(BF16), row(P['rw_a0']),
      P['rw_a2'].astype(BF16), P['rw_g2'].astype(BF16), row(P['rw_kk']), row(P['rw_ka']), ones_bd)


def _rw_scaled(lw, kap, bet, tri):
    cs = _dot_exact_lhs(tri, lw)
    return cs, kap * jnp.exp(cs - lw), bet * jnp.exp(-cs)


def _tri_masks(c):
    row = lax.broadcasted_iota(jnp.int32, (c, c), 0)
    col = lax.broadcasted_iota(jnp.int32, (c, c), 1)
    return row, col


def _chunk_tri(rows, c):
    row, col = _tri_masks(rows)
    return jnp.where((col <= row) & (row // c == col // c), 1.0, 0.0).astype(BF16)


def _rw_ldiag_body(lw_ref, kap_ref, bet_ref, o_ref, *, c):
    rows = lw_ref.shape[0]
    _, kk_all, bt_all = _rw_scaled(lw_ref[...], kap_ref[...], bet_ref[...], _chunk_tri(rows, c))
    srow, scol = _tri_masks(RW_SUB)
    units = [(cc, h) for cc in range(rows // c) for h in range(RW_HEADS)]
    Ls = [_dot_lo(kk_all[cc * c:(cc + 1) * c, h * RW_HD:(h + 1) * RW_HD],
                  bt_all[cc * c:(cc + 1) * c, h * RW_HD:(h + 1) * RW_HD], _NT) for cc, h in units]
    for (cc, h), L in zip(units, Ls):
        for b in range(c // RW_SUB):
            rs = slice(b * RW_SUB, (b + 1) * RW_SUB)
            o_ref[cc * c + b * RW_SUB:cc * c + (b + 1) * RW_SUB, h * RW_SUB:(h + 1) * RW_SUB] = (
                jnp.where(scol < srow, L[rs, rs], 0.0))


def _rw_inv_body(l_ref, t_ref, a_ref, b_ref):
    n = RW_SUB
    nblk = l_ref.shape[0] // n
    for t in range(n):
        a_ref[t] = l_ref[pl.ds(t, nblk, stride=n), :].T
    entry = lambda ref, t, s: ref.at[t, pl.ds(s, RW_HEADS, stride=n), :]
    one = jnp.ones((RW_HEADS, nblk), F32)
    zero = jnp.zeros((RW_HEADS, nblk), F32)
    for t in range(n):
        for s in range(n):
            if s > t:
                entry(b_ref, t, s)[...] = zero
            elif s == t:
                entry(b_ref, t, s)[...] = one
            else:
                acc = entry(a_ref, t, s)[...]
                for j in range(s + 1, t):
                    acc = acc + entry(a_ref, t, j)[...] * entry(b_ref, j, s)[...]
                entry(b_ref, t, s)[...] = -acc
    for t in range(n):
        t_ref[pl.ds(t, nblk, stride=n), :] = b_ref[t].T


def _rw_main_body(r_ref, lw_ref, k_ref, v_ref, kap_ref, bet_ref, g_ref, td_ref, h0_ref, rk_ref, lng_ref, lnb_ref,
                  o_ref, hout_ref, h_ref, *, c):
    ci = pl.program_id(1)
    rows = r_ref.shape[0]
    nb = c // RW_SUB

    @pl.when(ci == 0)
    def _():
        h_ref[...] = h0_ref[0]

    crow, ccol = _tri_masks(c)
    strict = ccol < crow
    incl = ccol <= crow
    lw = lw_ref[...]
    cs, kk_all, bt_all = _rw_scaled(lw, kap_ref[...], bet_ref[...], _chunk_tri(rows, c))
    gi = jnp.exp(-cs)
    gg = jnp.exp(cs)
    kt_all = k_ref[...] * gi
    rt_all = r_ref[...] * gg
    bonus_all = r_ref[...] * k_ref[...] * rk_ref[...]
    hrow = lax.broadcasted_iota(jnp.int32, (RW_HD, RW_HD), 0)
    hcol = lax.broadcasted_iota(jnp.int32, (RW_HD, RW_HD), 1)
    HS = range(RW_HEADS)
    units = [(cc, h) for cc in range(rows // c) for h in HS]
    US = range(len(units))
    rsl = [slice(cc * c, (cc + 1) * c) for cc, _ in units]
    lsl = [slice(h * RW_HD, (h + 1) * RW_HD) for _, h in units]
    Kk = [kk_all[rsl[u], lsl[u]] for u in US]
    Bt = [bt_all[rsl[u], lsl[u]] for u in US]
    Kt = [kt_all[rsl[u], lsl[u]] for u in US]
    Rt = [rt_all[rsl[u], lsl[u]] for u in US]
    vv = [v_ref[rsl[u], lsl[u]] for u in US]
    Lm = [jnp.where(strict, _dot_lo(Kk[u], Bt[u], _NT), 0.0) for u in US]
    A1 = [jnp.where(strict, _dot_lo(Kk[u], Kt[u], _NT), 0.0) for u in US]
    A4 = [jnp.where(incl, _dot_lo(Rt[u], Bt[u], _NT), 0.0) for u in US]
    A3 = [jnp.where(incl, _dot_lo(Rt[u], Kt[u], _NT), 0.0) for u in US]
    X = [jnp.concatenate([Kk[u], _dot_lo(A1[u], vv[u])], axis=1) for u in US]
    zs = [[] for _ in US]
    for b in range(nb):
        rs = slice(b * RW_SUB, (b + 1) * RW_SUB)
        rhs = [X[u][rs] for u in US]
        if b:
            rhs = [rhs[u] - _dot_lo(Lm[u][rs, 0:b * RW_SUB], jnp.concatenate(zs[u], axis=0)) for u in US]
        for u, (cc, h) in enumerate(units):
            tbb = td_ref[cc * c + b * RW_SUB:cc * c + (b + 1) * RW_SUB, h * RW_SUB:(h + 1) * RW_SUB]
            zs[u].append(_dot_lo(tbb, rhs[u]))
    Z = [jnp.concatenate(zs[u], axis=0) if nb > 1 else zs[u][0] for u in US]
    A4Z = [_dot_lo(A4[u], Z[u]) for u in US]
    Rhat = [Rt[u] - A4Z[u][:, :RW_HD] for u in US]
    Yhat = [_dot_lo(A3[u], vv[u]) - A4Z[u][:, RW_HD:] for u in US]
    gC = [gg[(cc + 1) * c - 1:(cc + 1) * c, lsl[u]] for u, (cc, _) in enumerate(units)]
    MN = [_dot_lo(Bt[u] * gC[u], Z[u], _TN) for u in US]
    Mp = [jnp.where(hrow == hcol, gC[u], 0.0) - MN[u][:, :RW_HD] for u in US]
    Np = [_dot_lo(Kt[u] * gC[u], vv[u], _TN) - MN[u][:, RW_HD:] for u in US]
    H = [h_ref[h] for h in HS]
    ys = [None for _ in US]
    for cc in range(rows // c):
        for h in HS:
            u = cc * RW_HEADS + h
            ys[u] = _dot_lo(Rhat[u], H[h]) + Yhat[u]
        H = [_dot_hi(Mp[cc * RW_HEADS + h], H[h]) + Np[cc * RW_HEADS + h] for h in HS]
    for h in HS:
        h_ref[h] = H[h]
    for u in US:
        y = ys[u]
        mu = jnp.mean(y, axis=-1, keepdims=True)
        var = jnp.mean(jnp.square(y - mu), axis=-1, keepdims=True)
        yn = (y - mu) * lax.rsqrt(var + RW_LN_EPS) * lng_ref[:, lsl[u]] + lnb_ref[:, lsl[u]]
        yn = yn + jnp.sum(bonus_all[rsl[u], lsl[u]], axis=-1, keepdims=True) * vv[u]
        o_ref[rsl[u], lsl[u]] = yn * g_ref[rsl[u], lsl[u]]

    @pl.when(ci == pl.num_programs(1) - 1)
    def _():
        hout_ref[0] = h_ref[...]


def _rwkv7(zr, shift, S0, B, T, P):
    G = GROUP_W
    n = B * T
    r, lw, k, v, kap, bet, g = _rw_prep(zr, shift, B, T, P)
    c = min(RW_CHUNK, T)
    rows_l = _row_tile(T, c * RW_LDIAG_CHUNKS)
    rows_m = _row_tile(T, c * RW_MAIN_CHUNKS)
    nl, nc = T // rows_l, T // rows_m
    tile_l = pl.BlockSpec((rows_l, G), lambda b, i: (b * nl + i, 0))
    tile = pl.BlockSpec((rows_m, G), lambda b, i: (b * nc + i, 0))
    ld = pl.pallas_call(
        functools.partial(_rw_ldiag_body, c=c),
        grid=(B, nl),
        in_specs=[tile_l] * 3,
        out_specs=pl.BlockSpec((rows_l, RW_HEADS * RW_SUB), lambda b, i: (b * nl + i, 0)),
        out_shape=jax.ShapeDtypeStruct((n, RW_HEADS * RW_SUB), F32),
        compiler_params=_params(("parallel", "parallel"), 32),
        name="rwkv_ldiag",
    )(lw, kap, bet)
    rows_i = V7X_LANES * RW_SUB
    npad = -(-n // rows_i) * rows_i
    inv_spec = pl.BlockSpec((rows_i, RW_HEADS * RW_SUB), lambda i: (i, 0))
    inv_scratch = pltpu.VMEM((RW_SUB, RW_HEADS * RW_SUB, V7X_LANES), F32)
    td = pl.pallas_call(
        _rw_inv_body,
        grid=(npad // rows_i,),
        in_specs=[inv_spec],
        out_specs=inv_spec,
        out_shape=jax.ShapeDtypeStruct((npad, RW_HEADS * RW_SUB), F32),
        scratch_shapes=[inv_scratch, inv_scratch],
        compiler_params=_params(("parallel",), 32),
        name="rwkv_inv",
    )(jnp.pad(ld, ((0, npad - n), (0, 0))))[:n]
    h0 = jnp.swapaxes(S0, -1, -2)
    prow = lambda x: pl.BlockSpec((1, G), lambda b, i: (0, 0))
    st_spec = pl.BlockSpec((1, RW_HEADS, RW_HD, RW_HD), lambda b, i: (b, 0, 0, 0))
    out, hl = pl.pallas_call(
        functools.partial(_rw_main_body, c=c),
        grid=(B, nc),
        in_specs=[tile] * 7 + [pl.BlockSpec((rows_m, RW_HEADS * RW_SUB), lambda b, i: (b * nc + i, 0)), st_spec,
                               prow(0), prow(0), prow(0)],
        out_specs=[tile, st_spec],
        out_shape=[jax.ShapeDtypeStruct((n, G), F32), jax.ShapeDtypeStruct(S0.shape, F32)],
        scratch_shapes=[pltpu.VMEM((RW_HEADS, RW_HD, RW_HD), F32)],
        compiler_params=_params(("parallel", "arbitrary"), 32),
        name="rwkv_main",
    )(r, lw, k, v, kap, bet, g, td, h0, P['rw_rk'].reshape(1, G), P['rw_ln_g'].reshape(1, G),
      P['rw_ln_b'].reshape(1, G))
    return out, zr.reshape(B, T, -1)[:, -1], jnp.swapaxes(hl, -1, -2)


def _even_mixer(x2, B, T, g, st, P):
    conv_buf, lru_h, k_past, v_past, lf_past = st
    G = GROUP_W
    z_rg, z_qkv, z_og, z_fl = _norm_matmul(x2, g, P['e_w_in'], (2 * G, 3 * G, G, V7X_LANES))
    rnn_out, conv_new, h_last = _lru(z_rg, conv_buf, lru_h, B, T, P)
    qb, kn, kb, v, vb, lf = _fox_prep(z_qkv, z_fl, B, T, P)
    past = k_past.shape[1]
    lf_all = lf.reshape(B, T, V7X_LANES)
    kb_all, vb_all = kb.reshape(B, T, G), vb.reshape(B, T, G)
    if past:
        lf_all = jnp.concatenate([jnp.pad(lf_past, ((0, 0), (0, 0), (0, V7X_LANES - FOX_HEADS))), lf_all], axis=1)
        kb_all = jnp.concatenate([k_past.reshape(B, past, G).astype(BF16), kb_all], axis=1)
        vb_all = jnp.concatenate([v_past.reshape(B, past, G).astype(BF16), vb_all], axis=1)
    tail = ((0, 0), (0, -(past + T) % FOX_K_ROWS), (0, 0))
    ka, vt = _fox_keys(jnp.pad(lf_all, tail), jnp.pad(kb_all, tail), jnp.pad(vb_all, tail))
    fox_out = _fox_attention(qb, ka, vt, z_og, B, T, past)
    x2 = _out_proj(x2, rnn_out, fox_out, P['e_w_out'])
    heads = lambda t: t.reshape(B, T, FOX_HEADS, FOX_HD)
    return x2, (conv_new, h_last, heads(kn), heads(v), lf.reshape(B, T, V7X_LANES)[..., :FOX_HEADS])


def _odd_mixer(x2, B, T, g, st, lb, P):
    S_hg, shift, S_rw = st
    G = GROUP_W
    z_hg, z_rw = _norm_matmul(x2, g, P['o_w_in'], (4 * G, P['o_w_in'].shape[1] - 4 * G))
    hg_out, S_hg_new = _hgrn2(z_hg, lb, S_hg, B, T, P)
    rw_out, shift_new, S_rw_new = _rwkv7(z_rw, shift, S_rw, B, T, P)
    x2 = _out_proj(x2, hg_out, rw_out, P['o_w_out'])
    return x2, (S_hg_new, shift_new, S_rw_new)


def _trunk(x, states, W):
    lru_conv, lru_h, fox_k, fox_v, fox_lf, hg_S, rw_shift, rw_S = states
    B, T, D = x.shape
    depth = W['norm_g'].shape[0]
    sm = jax.nn.softmax(W['hg_lb_logits'], axis=0)
    lower_bounds = jnp.cumsum(sm, axis=0) - sm[0]
    x2 = x.reshape(B * T, D)
    even_new, odd_new = [], []
    for layer in range(depth):
        g = W['norm_g'][layer]
        x2 = _ffn(x2, g[0], W['ffn_w_in'][layer][0], W['ffn_w_out'][layer][0])
        if layer % 2 == 0:
            e = layer // 2
            P = {n: W[n][e] for n in ('e_w_in', 'e_w_out', 'lru_conv_w', 'lru_conv_b', 'lru_wa', 'lru_ba', 'lru_wx',
                                      'lru_bx', 'lru_lambda', 'fox_q_gain', 'fox_k_gain', 'fox_f_bias')}
            x2, new = _even_mixer(x2, B, T, g[1], (lru_conv[e], lru_h[e], fox_k[e], fox_v[e], fox_lf[e]), P)
            even_new.append(new)
        else:
            o = layer // 2
            P = {n: W[n][o] for n in ('o_w_in', 'o_w_out', 'hg_norm_g', 'rw_mu', 'rw_w0', 'rw_w2', 'rw_a0', 'rw_a2',
                                      'rw_g2', 'rw_kk', 'rw_ka', 'rw_rk', 'rw_ln_g', 'rw_ln_b')}
            x2, new = _odd_mixer(x2, B, T, g[1], (hg_S[o], rw_shift[o], rw_S[o]), lower_bounds[layer], P)
            odd_new.append(new)
        x2 = _ffn(x2, g[2], W['ffn_w_in'][layer][1], W['ffn_w_out'][layer][1])
    ev = [jnp.stack([n[j] for n in even_new]) for j in range(5)]
    od = [jnp.stack([n[j] for n in odd_new]) for j in range(3)]
    return x2.reshape(B, T, D), (ev[0], ev[1], ev[2], ev[3], ev[4], od[0], od[1], od[2])


def kernel(x_prompt, x_sample, state_lru_conv, state_lru_h, cache_fox_k, cache_fox_v, cache_fox_logf,
           state_hgrn_S, state_rwkv_shift, state_rwkv_S, norm_g, ffn_w_in, ffn_w_out, e_w_in, e_w_out,
           lru_conv_w, lru_conv_b, lru_wa, lru_ba, lru_wx, lru_bx, lru_lambda, fox_q_gain, fox_k_gain,
           fox_f_bias, o_w_in, o_w_out, hg_lb_logits, hg_norm_g, rw_mu, rw_w0, rw_w2, rw_a0, rw_a2, rw_g2,
           rw_kk, rw_ka, rw_rk, rw_ln_g, rw_ln_b):
    n_even, n_odd = e_w_in.shape[0], o_w_in.shape[0]
    W = dict(norm_g=norm_g, ffn_w_in=_ffn_w_in_tiles(ffn_w_in), ffn_w_out=ffn_w_out.astype(BF16),
             e_w_in=_pad_cols(e_w_in.astype(BF16)), e_w_out=e_w_out.astype(BF16),
             lru_conv_w=lru_conv_w, lru_conv_b=lru_conv_b, lru_wa=lru_wa, lru_ba=lru_ba, lru_wx=lru_wx,
             lru_bx=lru_bx, lru_lambda=lru_lambda, fox_q_gain=fox_q_gain, fox_k_gain=fox_k_gain,
             fox_f_bias=fox_f_bias, o_w_in=o_w_in.astype(BF16), o_w_out=o_w_out.astype(BF16),
             hg_lb_logits=hg_lb_logits, hg_norm_g=hg_norm_g, rw_mu=rw_mu, rw_w0=rw_w0, rw_w2=rw_w2, rw_a0=rw_a0,
             rw_a2=rw_a2, rw_g2=rw_g2, rw_kk=rw_kk, rw_ka=rw_ka, rw_rk=rw_rk, rw_ln_g=rw_ln_g, rw_ln_b=rw_ln_b)
    nb = x_prompt.shape[0]
    dt = x_prompt.dtype
    prompt_states = (jnp.zeros((n_even, nb, CONV_W - 1, GROUP_W), dt),
                     jnp.zeros((n_even, nb, GROUP_W), dt),
                     jnp.zeros((n_even, nb, 0, FOX_HEADS, FOX_HD), dt),
                     jnp.zeros((n_even, nb, 0, FOX_HEADS, FOX_HD), dt),
                     jnp.zeros((n_even, nb, 0, FOX_HEADS), dt),
                     jnp.zeros((n_odd, nb, HG_HEADS, GROUP_W // HG_HEADS, GROUP_W // HG_HEADS), dt),
                     jnp.zeros((n_odd, nb, rw_mu.shape[1]), dt),
                     jnp.zeros((n_odd, nb, RW_HEADS, RW_HD, RW_HD), dt))
    sample_states = (state_lru_conv, state_lru_h, cache_fox_k, cache_fox_v, cache_fox_logf,
                     state_hgrn_S, state_rwkv_shift, state_rwkv_S)
    y_prompt, p_new = _trunk(x_prompt, prompt_states, W)
    y_sample, s_new = _trunk(x_sample, sample_states, W)
    lru_conv_p, lru_h_p, fox_k_p, fox_v_p, fox_logf_p, hgrn_S_p, rwkv_shift_p, rwkv_S_p = p_new
    lru_conv_s, lru_h_s, fox_k_s, fox_v_s, fox_logf_s, hgrn_S_s, rwkv_shift_s, rwkv_S_s = s_new
    return (y_prompt, y_sample, lru_conv_p, lru_conv_s, lru_h_p, lru_h_s, fox_k_p, fox_k_s, fox_v_p, fox_v_s,
            fox_logf_p, fox_logf_s, hgrn_S_p, hgrn_S_s, rwkv_shift_p, rwkv_shift_s, rwkv_S_p, rwkv_S_s)
```

```python
import functools

import jax
import jax.numpy as jnp
from jax import lax
from jax.experimental import pallas as pl
from jax.experimental.pallas import tpu as pltpu

F32 = jnp.float32
BF16 = jnp.bfloat16

NORM_EPS = 1e-6
GROUP_W = 512
CONV_W = 4
LRU_C = 8.0
FOX_HEADS = 8
FOX_HD = 64
HG_HEADS = 4
RW_HEADS = 8
RW_HD = 64
RW_DECAY_LORA = 64
RW_A_LORA = 64
RW_GATE_LORA = 128
RW_LN_EPS = 64e-5

V7X_LANES = 128
FFN_COL_TILE = 1408
PREP_ROWS = 512
FFN_ROW_TILE = 1024
FFN_VMEM_MIB = 60


def _row_tile(n, want):
    t = min(n, want)
    while n % t:
        t //= 2
    return t


def _params(sem, vmem_mib):
    return pltpu.CompilerParams(dimension_semantics=sem, vmem_limit_bytes=vmem_mib << 20)


def _pad_cols(w):
    pad = -w.shape[-1] % V7X_LANES
    return jnp.pad(w, [(0, 0)] * (w.ndim - 1) + [(0, pad)])


def _rms(x, g):
    return x * lax.rsqrt(jnp.mean(x * x, axis=-1, keepdims=True) + NORM_EPS) * g


def _ffn_body(x_ref, g_ref, wi_ref, wo_ref, o_ref, h_ref, acc_ref):
    j = pl.program_id(1)

    @pl.when(j == 0)
    def _():
        h_ref[...] = _rms(x_ref[...], g_ref[...]).astype(BF16)
        acc_ref[...] = jnp.zeros_like(acc_ref)

    tf = wo_ref.shape[0]
    gu = jnp.dot(h_ref[...], wi_ref[...], preferred_element_type=F32)
    gate, up = gu[:, :tf], gu[:, tf:]
    act = (gate * jax.nn.sigmoid(gate) * up).astype(BF16)
    acc_ref[...] += jnp.dot(act, wo_ref[...], preferred_element_type=F32)

    @pl.when(j == pl.num_programs(1) - 1)
    def _():
        o_ref[...] = x_ref[...] + 0.5 * acc_ref[...]


def _cast_body(x_ref, o_ref):
    o_ref[...] = x_ref[...].astype(o_ref.dtype)


def _ffn_w_in_tiles(w_in):
    *lead, d, f2 = w_in.shape
    tf = FFN_COL_TILE
    nf = f2 // 2 // tf
    w = w_in.reshape(-1, d, f2)
    out = pl.pallas_call(
        _cast_body,
        grid=(w.shape[0], nf, 2),
        in_specs=[pl.BlockSpec((1, d, tf), lambda i, j, gu: (i, 0, gu * nf + j))],
        out_specs=pl.BlockSpec((1, d, tf), lambda i, j, gu: (i, 0, 2 * j + gu)),
        out_shape=jax.ShapeDtypeStruct(w.shape, BF16),
        compiler_params=_params(("parallel", "parallel", "parallel"), 32),
        name="ffn_weight_tiles",
    )(w)
    return out.reshape(*lead, d, f2)


def _ffn(x, g, w_in, w_out):
    n, d = x.shape
    f = w_out.shape[0]
    tm = _row_tile(n, FFN_ROW_TILE)
    tf = FFN_COL_TILE
    nf = f // tf
    return pl.pallas_call(
        _ffn_body,
        grid=(n // tm, nf),
        in_specs=[
            pl.BlockSpec((tm, d), lambda i, j: (i, 0)),
            pl.BlockSpec((1, d), lambda i, j: (0, 0)),
            pl.BlockSpec((d, 2 * tf), lambda i, j: (0, j)),
            pl.BlockSpec((tf, d), lambda i, j: (j, 0)),
        ],
        out_specs=pl.BlockSpec((tm, d), lambda i, j: (i, 0)),
        out_shape=jax.ShapeDtypeStruct((n, d), F32),
        scratch_shapes=[pltpu.VMEM((tm, d), BF16), pltpu.VMEM((tm, d), F32)],
        compiler_params=_params(("parallel", "arbitrary"), FFN_VMEM_MIB),
        name="ffn",
    )(x, g.reshape(1, d), w_in, w_out)


def _norm_matmul_body(x_ref, g_ref, w_ref, *o_refs):
    h = _rms(x_ref[...], g_ref[...]).astype(BF16)
    z = jnp.dot(h, w_ref[...], preferred_element_type=F32)
    start = 0
    for o_ref in o_refs:
        width = o_ref.shape[1]
        o_ref[...] = z[:, start:start + width]
        start += width


def _norm_matmul(x, g, w, widths):
    n, d = x.shape
    c = w.shape[1]
    assert sum(widths) == c and all(wd % V7X_LANES == 0 for wd in widths)
    tm = _row_tile(n, 512)
    return pl.pallas_call(
        _norm_matmul_body,
        grid=(n // tm,),
        in_specs=[
            pl.BlockSpec((tm, d), lambda i: (i, 0)),
            pl.BlockSpec((1, d), lambda i: (0, 0)),
            pl.BlockSpec((d, c), lambda i: (0, 0)),
        ],
        out_specs=[pl.BlockSpec((tm, wd), lambda i: (i, 0)) for wd in widths],
        out_shape=[jax.ShapeDtypeStruct((n, wd), F32) for wd in widths],
        compiler_params=_params(("parallel",), 48),
        name="norm_matmul",
    )(x, g.reshape(1, d), w)


def _out_proj_body(x_ref, a_ref, b_ref, wa_ref, wb_ref, o_ref):
    acc = jnp.dot(a_ref[...].astype(BF16), wa_ref[...], preferred_element_type=F32)
    acc += jnp.dot(b_ref[...].astype(BF16), wb_ref[...], preferred_element_type=F32)
    o_ref[...] = x_ref[...] + acc


def _out_proj(x, a, b, w):
    n, d = x.shape
    ga, gb = a.shape[1], b.shape[1]
    tm = _row_tile(n, 512)
    return pl.pallas_call(
        _out_proj_body,
        grid=(n // tm,),
        in_specs=[
            pl.BlockSpec((tm, d), lambda i: (i, 0)),
            pl.BlockSpec((tm, ga), lambda i: (i, 0)),
            pl.BlockSpec((tm, gb), lambda i: (i, 0)),
            pl.BlockSpec((ga, d), lambda i: (0, 0)),
            pl.BlockSpec((gb, d), lambda i: (0, 0)),
        ],
        out_specs=pl.BlockSpec((tm, d), lambda i: (i, 0)),
        out_shape=jax.ShapeDtypeStruct((n, d), F32),
        compiler_params=_params(("parallel",), 32),
        name="out_proj",
    )(x, a, b, w[:ga], w[ga:])


LRU_ROWS = 256
CONV_PAD = 8


def _expm1(x):
    series = x * (1.0 + x * (1 / 2 + x * (1 / 6 + x * (1 / 24 + x * (1 / 120 + x * (1 / 720 + x * (1 / 5040 + x * (1 / 40320))))))))
    return jnp.where(jnp.abs(x) < 0.25, series, jnp.exp(x) - 1.0)


def _shift_rows(x, s, fill):
    row = lax.broadcasted_iota(jnp.int32, x.shape, 0)
    return jnp.where(row >= s, pltpu.roll(x, s, axis=0), fill)


def _lru_body(z_ref, buf_ref, h0_ref, cw_ref, cb_ref, wa_ref, ba_ref, wx_ref, bx_ref, lam_ref,
              o_ref, bufo_ref, ho_ref, x_ref, hc_ref):
    G = GROUP_W
    tt = z_ref.shape[0]

    @pl.when(pl.program_id(1) == 0)
    def _():
        x_ref[0:CONV_PAD, :] = buf_ref[0]
        hc_ref[...] = jnp.broadcast_to(h0_ref[0], hc_ref.shape)

    x_ref[CONV_PAD:CONV_PAD + tt, :] = z_ref[:, 0:G]
    xc = cb_ref[...]
    for j in range(CONV_W):
        lo = CONV_PAD - (CONV_W - 1) + j
        xc = xc + x_ref[lo:lo + tt, :] * cw_ref[j:j + 1, :]
    hist = x_ref[tt:tt + CONV_PAD, :]
    x_ref[0:CONV_PAD, :] = hist
    bufo_ref[0] = hist

    xb = xc.astype(BF16)
    r = jax.nn.sigmoid(jnp.dot(xb, wa_ref[...], preferred_element_type=F32) + ba_ref[...])
    ig = jax.nn.sigmoid(jnp.dot(xb, wx_ref[...], preferred_element_type=F32) + bx_ref[...])
    log_a = (-LRU_C * _softplus(-lam_ref[...])) * r
    a = jnp.exp(log_a)
    b = jnp.sqrt(-_expm1(2.0 * log_a)) * (ig * xc)
    s = 1
    while s < tt:
        if s % 8:
            b = a * _shift_rows(b, s, 0.0) + b
            a = a * _shift_rows(a, s, 1.0)
        else:
            b = jnp.concatenate([b[:s], a[s:] * b[:tt - s] + b[s:]], axis=0)
            a = jnp.concatenate([a[:s], a[s:] * a[:tt - s]], axis=0)
        s *= 2
    h = a * hc_ref[0:1, :] + b
    hc_ref[...] = jnp.broadcast_to(h[tt - 1:tt, :], hc_ref.shape)
    ho_ref[0] = h[tt - 1:tt, :]
    o_ref[...] = jax.nn.gelu(z_ref[:, G:2 * G]) * h


def _block_diag_dense(w):
    nb, bs, _ = w.shape
    eye = jnp.eye(nb, dtype=w.dtype)
    return (eye[:, None, :, None] * w[:, :, None, :]).reshape(nb * bs, nb * bs)


def _lru(z_rg, conv_buf, h0, B, T, P):
    G = GROUP_W
    n = B * T
    tt = _row_tile(T, LRU_ROWS)
    nt = T // tt
    buf = jnp.pad(conv_buf, ((0, 0), (CONV_PAD - (CONV_W - 1), 0), (0, 0)))
    cw = jnp.pad(P['lru_conv_w'], ((0, CONV_PAD - CONV_W), (0, 0)))
    row = lambda x: x.reshape(1, G)
    full = lambda shape: pl.BlockSpec(shape, lambda b, i: (0,) * len(shape))
    out, bufo, ho = pl.pallas_call(
        _lru_body,
        grid=(B, nt),
        in_specs=[
            pl.BlockSpec((tt, 2 * G), lambda b, i: (b * nt + i, 0)),
            pl.BlockSpec((1, CONV_PAD, G), lambda b, i: (b, 0, 0)),
            pl.BlockSpec((1, 1, G), lambda b, i: (b, 0, 0)),
            full((CONV_PAD, G)), full((1, G)), full((G, G)), full((1, G)), full((G, G)), full((1, G)), full((1, G)),
        ],
        out_specs=[
            pl.BlockSpec((tt, G), lambda b, i: (b * nt + i, 0)),
            pl.BlockSpec((1, CONV_PAD, G), lambda b, i: (b, 0, 0)),
            pl.BlockSpec((1, 1, G), lambda b, i: (b, 0, 0)),
        ],
        out_shape=[jax.ShapeDtypeStruct((n, G), F32), jax.ShapeDtypeStruct((B, CONV_PAD, G), F32),
                   jax.ShapeDtypeStruct((B, 1, G), F32)],
        scratch_shapes=[pltpu.VMEM((tt + CONV_PAD, G), F32), pltpu.VMEM((8, G), F32)],
        compiler_params=_params(("parallel", "arbitrary"), 32),
        name="lru",
    )(z_rg, buf, h0.reshape(B, 1, G), cw, row(P['lru_conv_b']), _block_diag_dense(P['lru_wa']).astype(BF16),
      row(P['lru_ba']), _block_diag_dense(P['lru_wx']).astype(BF16), row(P['lru_bx']), row(P['lru_lambda']))
    return out, bufo[:, CONV_PAD - (CONV_W - 1):], ho.reshape(B, G)


FOX_Q_COLS = 1024
FOX_K_ROWS = 512
FOX_F_SPLIT = 3
FOX_NEG = -1e30
LOG2E = 1.4426950408889634
HEAD_PAIRS = FOX_HEADS // 2
PAIR_W = 2 * FOX_HD
FOX_KEY_TILE = 512


def _fox_prep_body(z_ref, fl_ref, qg_ref, kg_ref, fb_ref, ones_ref, q_ref, k_ref, kb_ref, v_ref, vb_ref, lf_ref,
                   *, q_transposed):
    G = GROUP_W
    q, k, v = z_ref[:, 0:G], z_ref[:, G:2 * G], z_ref[:, 2 * G:3 * G]
    inv = 1.0 / FOX_HD
    qn = q * lax.rsqrt(_dot_exact_rhs(q * q, ones_ref[...]) * inv + NORM_EPS) * qg_ref[...]
    kn = k * lax.rsqrt(_dot_exact_rhs(k * k, ones_ref[...]) * inv + NORM_EPS) * kg_ref[...]
    qs = qn * (LOG2E * FOX_HD ** -0.5)
    if q_transposed:
        for p in range(HEAD_PAIRS):
            q_ref[0, p] = qs[:, p * PAIR_W:(p + 1) * PAIR_W].T.astype(BF16)
    else:
        q_ref[...] = qs.astype(BF16)
    k_ref[...] = kn
    kb_ref[...] = kn.astype(BF16)
    v_ref[...] = v
    vb_ref[...] = v.astype(BF16)
    x = fl_ref[...] + fb_ref[...]
    lf_ref[...] = -_softplus(-x)


def _fox_prep(z_qkv, z_fl, B, T, P):
    n = z_qkv.shape[0]
    G = GROUP_W
    tt = _row_tile(T, PREP_ROWS)
    nt = T // tt
    q_transposed = tt % V7X_LANES == 0
    ones_bd = jnp.kron(jnp.eye(FOX_HEADS, dtype=F32), jnp.ones((FOX_HD, FOX_HD), F32)).astype(BF16)
    fb = jnp.pad(P['fox_f_bias'], (0, V7X_LANES - FOX_HEADS)).reshape(1, V7X_LANES)
    tile = lambda w: pl.BlockSpec((tt, w), lambda b, i: (b * nt + i, 0))
    full = lambda shape: pl.BlockSpec(shape, lambda b, i: (0,) * len(shape))
    if q_transposed:
        q_spec = pl.BlockSpec((1, HEAD_PAIRS, PAIR_W, tt), lambda b, i: (b, 0, 0, i))
        q_shape = jax.ShapeDtypeStruct((B, HEAD_PAIRS, PAIR_W, T), BF16)
    else:
        q_spec, q_shape = tile(G), jax.ShapeDtypeStruct((n, G), BF16)
    return pl.pallas_call(
        functools.partial(_fox_prep_body, q_transposed=q_transposed),
        grid=(B, nt),
        in_specs=[tile(3 * G), tile(V7X_LANES), full((1, G)), full((1, G)), full((1, V7X_LANES)), full((G, G))],
        out_specs=[q_spec] + [tile(G)] * 4 + [tile(V7X_LANES)],
        out_shape=[q_shape, jax.ShapeDtypeStruct((n, G), F32),
                   jax.ShapeDtypeStruct((n, G), BF16), jax.ShapeDtypeStruct((n, G), F32),
                   jax.ShapeDtypeStruct((n, G), BF16), jax.ShapeDtypeStruct((n, V7X_LANES), F32)],
        compiler_params=_params(("parallel", "parallel"), 32),
        name="fox_prep",
    )(z_qkv, z_fl, jnp.tile(P['fox_q_gain'], FOX_HEADS).reshape(1, G),
      jnp.tile(P['fox_k_gain'], FOX_HEADS).reshape(1, G), fb, ones_bd)


def _fox_keys_body(lf_ref, kb_ref, vb_ref, ka_ref, vt_ref, c_ref):
    tt = lf_ref.shape[1]

    @pl.when(pl.program_id(1) == 0)
    def _():
        c_ref[...] = jnp.zeros_like(c_ref)

    row, col = _tri_masks(tt)
    tri = jnp.where(col <= row, 1.0, 0.0).astype(BF16)
    f = _dot_exact_lhs(tri, lf_ref[0]) + c_ref[0:1, :]
    c_ref[...] = jnp.broadcast_to(f[tt - 1:tt, :], c_ref.shape)
    parts = _split3(f * LOG2E)
    srow = lax.broadcasted_iota(jnp.int32, (V7X_LANES, FOX_HD), 0)
    scol = lax.broadcasted_iota(jnp.int32, (V7X_LANES, FOX_HD), 1)
    for h in range(FOX_HEADS):
        aug = jnp.zeros((tt, FOX_HD), F32)
        for t, part in enumerate(parts):
            sel = jnp.where((srow == h) & (scol == t), 1.0, 0.0).astype(BF16)
            aug = aug + jnp.dot(part, sel, preferred_element_type=F32)
        ka_ref[0, h] = jnp.concatenate([kb_ref[0, :, h * FOX_HD:(h + 1) * FOX_HD], aug.astype(BF16)], axis=1)
    for p in range(HEAD_PAIRS):
        vt_ref[0, p] = vb_ref[0, :, p * PAIR_W:(p + 1) * PAIR_W].astype(F32).T.astype(BF16)


def _fox_keys(lf_all, kb_all, vb_all):
    B, tk_all, L = lf_all.shape
    G = GROUP_W
    tt = FOX_KEY_TILE
    return pl.pallas_call(
        _fox_keys_body,
        grid=(B, tk_all // tt),
        in_specs=[pl.BlockSpec((1, tt, L), lambda b, i: (b, i, 0)),
                  pl.BlockSpec((1, tt, G), lambda b, i: (b, i, 0)),
                  pl.BlockSpec((1, tt, G), lambda b, i: (b, i, 0))],
        out_specs=[pl.BlockSpec((1, FOX_HEADS, tt, 2 * FOX_HD), lambda b, i: (b, 0, i, 0)),
                   pl.BlockSpec((1, HEAD_PAIRS, PAIR_W, tt), lambda b, i: (b, 0, 0, i))],
        out_shape=[jax.ShapeDtypeStruct((B, FOX_HEADS, tk_all, 2 * FOX_HD), BF16),
                   jax.ShapeDtypeStruct((B, HEAD_PAIRS, PAIR_W, tk_all), BF16)],
        scratch_shapes=[pltpu.VMEM((8, L), F32)],
        compiler_params=_params(("parallel", "arbitrary"), 32),
        name="fox_keys",
    )(lf_all, kb_all, vb_all)


def _fox_attn_body(qt_ref, ka_ref, vt_ref, og_ref, o_ref, acc_ref, m_ref, l_ref, *, past, tk, t_real):
    qi = pl.program_id(2)
    tq = qt_ref.shape[3]
    t_out = o_ref.shape[0]
    first_q = past + qi * tq
    last_q = past + jnp.minimum(qi * tq + tq, t_real) - 1
    n_full = (first_q + 1) // tk
    n_all = last_q // tk + 1
    drow = lax.broadcasted_iota(jnp.int32, (FOX_HD, tq), 0)
    minus = jnp.where(drow < FOX_F_SPLIT, -1.0, 0.0).astype(BF16)
    rhs = [jnp.concatenate([qt_ref[0, 0, h * FOX_HD:(h + 1) * FOX_HD, :], minus], axis=0) for h in range(2)]
    acc_ref[...] = jnp.zeros_like(acc_ref)

    def update(ki, m_prev, l_prev, masked, q0=0):
        ks = pl.multiple_of(ki * tk, tk)
        s = [jnp.dot(ka_ref[0, h, pl.ds(ks, tk), :], rhs[h][:, q0:], preferred_element_type=F32) for h in range(2)]
        if masked:
            krow = lax.broadcasted_iota(jnp.int32, (tk, tq - q0), 0)
            qcol = lax.broadcasted_iota(jnp.int32, (tk, tq - q0), 1)
            vis = ks + krow <= first_q + q0 + qcol
            s = [jnp.where(vis, s[h], FOX_NEG) for h in range(2)]
        m_new = [jnp.maximum(m_prev[h], jnp.max(s[h], axis=0, keepdims=True)) for h in range(2)]
        alpha = [jnp.exp2(m_prev[h] - m_new[h]) for h in range(2)]
        p = [jnp.exp2(s[h] - m_new[h]) for h in range(2)]
        l_new = [alpha[h] * l_prev[h] + jnp.sum(p[h], axis=0, keepdims=True) for h in range(2)]
        vt = [vt_ref[0, 0, h * FOX_HD:(h + 1) * FOX_HD, pl.ds(ks, tk)] for h in range(2)]
        pv = [jnp.dot(vt[h], p[h].astype(BF16), preferred_element_type=F32) for h in range(2)]
        for h in range(2):
            acc_ref[h, :, q0:] = alpha[h] * acc_ref[h, :, q0:] + pv[h]
        return m_new, l_new

    def carried(ki, c, masked):
        return update(ki, c[0], c[1], masked)

    c = ([jnp.full((1, tq), FOX_NEG, F32)] * 2, [jnp.zeros((1, tq), F32)] * 2)
    c = lax.fori_loop(0, n_full, lambda ki, c: carried(ki, c, False), c)
    if tq == 2 * tk and past % tq == 0 and t_real % tq == 0:
        m, l = carried(n_full, c, True)
        for h in range(2):
            m_ref[h], l_ref[h] = m[h], l[h]
        _, l = update(n_full + 1, [m_ref[h, :, tk:] for h in range(2)], [l_ref[h, :, tk:] for h in range(2)],
                      True, q0=tk)
        for h in range(2):
            l_ref[h, :, tk:] = l[h]
    else:
        _, l = lax.fori_loop(n_full, n_all, lambda ki, c: carried(ki, c, True), c)
        for h in range(2):
            l_ref[h] = l[h]
    o_t = jnp.concatenate([acc_ref[h] / l_ref[h] for h in range(2)], axis=0)
    o_ref[...] = o_t.T[:t_out] * jax.nn.sigmoid(og_ref[...])


def _fox_attention(q, ka, vt, z_og, B, T, past):
    G = GROUP_W
    pw = PAIR_W
    tq = max(_row_tile(T, FOX_Q_COLS), V7X_LANES)
    tqp = -(-T // tq) * tq
    nq = tqp // tq
    t_out = min(tq, T)
    tk = FOX_K_ROWS
    tkp = ka.shape[2]
    if q.ndim == 2:
        q = q.reshape(B, T, HEAD_PAIRS, pw).transpose(0, 2, 3, 1)
    qt = jnp.pad(q, ((0, 0), (0, 0), (0, 0), (0, tqp - T)))
    return pl.pallas_call(
        functools.partial(_fox_attn_body, past=past, tk=tk, t_real=T),
        grid=(B, HEAD_PAIRS, nq),
        in_specs=[
            pl.BlockSpec((1, 1, pw, tq), lambda b, p, i: (b, p, 0, i)),
            pl.BlockSpec((1, 2, tkp, 2 * FOX_HD), lambda b, p, i: (b, p, 0, 0)),
            pl.BlockSpec((1, 1, pw, tkp), lambda b, p, i: (b, p, 0, 0)),
            pl.BlockSpec((t_out, pw), lambda b, p, i: (b * nq + i, p)),
        ],
        out_specs=pl.BlockSpec((t_out, pw), lambda b, p, i: (b * nq + i, p)),
        out_shape=jax.ShapeDtypeStruct((B * T, G), F32),
        scratch_shapes=[pltpu.VMEM((2, FOX_HD, tq), F32), pltpu.VMEM((2, 1, tq), F32), pltpu.VMEM((2, 1, tq), F32)],
        compiler_params=_params(("parallel", "parallel", "arbitrary"), 40),
        name="fox_attn",
    )(qt, ka, vt, z_og)


HG_CHUNK = 64
HG_STEP_CHUNKS = 8


def _hgrn_body(z_ref, lb_ref, s0_ref, ng_ref, o_ref, so_ref, st_ref, *, c):
    G = GROUP_W
    rows = z_ref.shape[0]
    nch = rows // c
    dk = G // HG_HEADS

    @pl.when(pl.program_id(1) == 0)
    def _():
        st_ref[...] = s0_ref[0]

    lb = lb_ref[...]
    f = lb + (1.0 - lb) * jax.nn.sigmoid(z_ref[:, G:2 * G])
    kx = 1.0 - f
    crow, ccol = _tri_masks(c)
    incl = ccol <= crow
    gs = _dot_exact_lhs(_chunk_tri(rows, c), jnp.log(f))
    qg_all = z_ref[:, 0:G] * jnp.exp(gs)
    kg_all = kx * jnp.exp(-gs)
    HS = range(HG_HEADS)
    units = [(cc, h) for cc in range(nch) for h in HS]
    US = range(len(units))
    rsl = [slice(cc * c, (cc + 1) * c) for cc, _ in units]
    lsl = [slice(h * dk, (h + 1) * dk) for _, h in units]
    g_last = [gs[(cc + 1) * c - 1:(cc + 1) * c, lsl[u]] for u, (cc, _) in enumerate(units)]
    vv = [z_ref[rsl[u], 2 * G + h * dk:2 * G + (h + 1) * dk] for u, (_, h) in enumerate(units)]
    A = [jnp.where(incl, _dot_lo(qg_all[rsl[u], lsl[u]], kg_all[rsl[u], lsl[u]], _NT), 0.0) for u in US]
    av = [_dot_lo(A[u], vv[u]) for u in US]
    kd = [kx[rsl[u], lsl[u]] * jnp.exp(g_last[u] - gs[rsl[u], lsl[u]]) for u in US]
    upd = [_dot_lo(vv[u], kd[u], _TN) for u in US]
    st = [st_ref[h] for h in HS]
    o = [None for _ in US]
    for cc in range(nch):
        for h in HS:
            u = cc * HG_HEADS + h
            o[u] = _dot_lo(qg_all[rsl[u], lsl[u]], st[h], _NT) + av[u]
        st = [st[h] * jnp.exp(g_last[cc * HG_HEADS + h]) + upd[cc * HG_HEADS + h] for h in HS]
    for h in HS:
        st_ref[h] = st[h]
    for u, (_, h) in enumerate(units):
        hg = z_ref[rsl[u], 3 * G + h * dk:3 * G + (h + 1) * dk]
        o_ref[rsl[u], lsl[u]] = _rms(o[u], ng_ref[:, lsl[u]]) * (hg * jax.nn.sigmoid(hg))

    @pl.when(pl.program_id(1) == pl.num_programs(1) - 1)
    def _():
        so_ref[0] = st_ref[...]


def _hgrn2(z_hg, lb, S0, B, T, P):
    G = GROUP_W
    c = min(HG_CHUNK, T)
    rows = _row_tile(T, c * HG_STEP_CHUNKS)
    nc = T // rows
    dk = G // HG_HEADS
    st_spec = pl.BlockSpec((1, HG_HEADS, dk, dk), lambda b, i: (b, 0, 0, 0))
    out, so = pl.pallas_call(
        functools.partial(_hgrn_body, c=c),
        grid=(B, nc),
        in_specs=[pl.BlockSpec((rows, 4 * G), lambda b, i: (b * nc + i, 0)),
                  pl.BlockSpec((1, G), lambda b, i: (0, 0)), st_spec, pl.BlockSpec((1, G), lambda b, i: (0, 0))],
        out_specs=[pl.BlockSpec((rows, G), lambda b, i: (b * nc + i, 0)), st_spec],
        out_shape=[jax.ShapeDtypeStruct((B * T, G), F32), jax.ShapeDtypeStruct(S0.shape, F32)],
        scratch_shapes=[pltpu.VMEM((HG_HEADS, dk, dk), F32)],
        compiler_params=_params(("parallel", "arbitrary"), 32),
        name="hgrn2",
    )(z_hg, lb.reshape(1, G), jnp.swapaxes(S0, -1, -2), P['hg_norm_g'].reshape(1, G))
    return out, jnp.swapaxes(so, -1, -2)


RW_CHUNK = 64
RW_SUB = 16
RW_LDIAG_CHUNKS = 4
RW_MAIN_CHUNKS = 2

_NT = (((1,), (1,)), ((), ()))
_TN = (((0,), (0,)), ((), ()))
_NN = (((1,), (0,)), ((), ()))


def _split3(x):
    h1 = x.astype(BF16)
    r1 = x - h1.astype(F32)
    h2 = r1.astype(BF16)
    h3 = (r1 - h2.astype(F32)).astype(BF16)
    return h1, h2, h3


def _dot_lo(a, b, dims=_NN):
    return lax.dot_general(a.astype(BF16), b.astype(BF16), dims, preferred_element_type=F32)


def _dot_hi(a, b, dims=_NN):
    ah = a.astype(BF16)
    al = (a - ah.astype(F32)).astype(BF16)
    bh = b.astype(BF16)
    bl = (b - bh.astype(F32)).astype(BF16)
    d = functools.partial(lax.dot_general, dimension_numbers=dims, preferred_element_type=F32)
    return d(ah, bh) + (d(al, bh) + d(ah, bl))


def _dot_exact_rhs(a, b):
    h1, h2, h3 = _split3(a)
    d = functools.partial(jnp.dot, preferred_element_type=F32)
    return d(h1, b) + (d(h2, b) + d(h3, b))


def _dot_exact_lhs(a, b):
    h1, h2, h3 = _split3(b)
    d = functools.partial(jnp.dot, preferred_element_type=F32)
    return d(a, h1) + (d(a, h2) + d(a, h3))


def _softplus(x):
    return jnp.maximum(x, 0.0) + jnp.log1p(jnp.exp(-jnp.abs(x)))


def _rw_prep_body(z_ref, shift_ref, mu_ref, w0_ref, w2_ref, a0_ref, a2_ref, g2_ref, kk_ref, ka_ref, ones_ref,
                  r_ref, lw_ref, k_ref, v_ref, kap_ref, bet_ref, g_ref, prev_ref):
    G = GROUP_W

    @pl.when(pl.program_id(1) == 0)
    def _():
        prev_ref[0:1, :] = shift_ref[0]

    z = z_ref[...]
    tt = z.shape[0]
    row = lax.broadcasted_iota(jnp.int32, z.shape, 0)
    shifted = jnp.where(row == 0, prev_ref[0:1, :], pltpu.roll(z, 1, axis=0))
    prev_ref[0:1, :] = z[tt - 1:tt, :]
    zm = z + (shifted - z) * mu_ref[...]
    r, k, v = zm[:, 0:G], zm[:, G:2 * G], zm[:, 2 * G:3 * G]
    o = 3 * G
    wd = zm[:, o:o + RW_DECAY_LORA]
    ad = zm[:, o + RW_DECAY_LORA:o + RW_DECAY_LORA + RW_A_LORA]
    gd = zm[:, o + RW_DECAY_LORA + RW_A_LORA:]
    w = -_softplus(-(w0_ref[...] + _dot_lo(jnp.tanh(wd), w2_ref[...]))) - 0.5
    a = jax.nn.sigmoid(a0_ref[...] + _dot_lo(ad, a2_ref[...]))
    kk = k * kk_ref[...]
    ss = _dot_exact_rhs(kk * kk, ones_ref[...])
    kap = kk / jnp.maximum(jnp.sqrt(ss), 1e-12)
    r_ref[...] = r
    lw_ref[...] = -jnp.exp(w)
    k_ref[...] = k * (1.0 + (a - 1.0) * ka_ref[...])
    v_ref[...] = v
    kap_ref[...] = kap
    bet_ref[...] = kap * a
    g_ref[...] = _dot_lo(jax.nn.sigmoid(gd), g2_ref[...])


def _rw_prep(zr, shift, B, T, P):
    n, cols = zr.shape
    G = GROUP_W
    tt = _row_tile(T, PREP_ROWS)
    nt = T // tt
    ones_bd = jnp.kron(jnp.eye(RW_HEADS, dtype=F32), jnp.ones((RW_HD, RW_HD), F32)).astype(BF16)
    row = lambda x: x.reshape(1, -1)
    full = lambda shape: pl.BlockSpec(shape, lambda b, i: (0,) * len(shape))
    tile = pl.BlockSpec((tt, G), lambda b, i: (b * nt + i, 0))
    return pl.pallas_call(
        _rw_prep_body,
        grid=(B, nt),
        in_specs=[
            pl.BlockSpec((tt, cols), lambda b, i: (b * nt + i, 0)),
            pl.BlockSpec((1, 1, cols), lambda b, i: (b, 0, 0)),
            full((1, cols)), full((1, G)), full((RW_DECAY_LORA, G)), full((1, G)), full((RW_A_LORA, G)),
            full((RW_GATE_LORA, G)), full((1, G)), full((1, G)), full((G, G)),
        ],
        out_specs=[tile] * 7,
        out_shape=[jax.ShapeDtypeStruct((n, G), F32)] * 7,
        scratch_shapes=[pltpu.VMEM((8, cols), F32)],
        compiler_params=_params(("parallel", "arbitrary"), 40),
        name="rwkv_prep",
    )(zr, shift.reshape(B, 1, cols), row(P['rw_mu']), row(P['rw_w0']), P['rw_w2'].astype(BF16), row(P['rw_a0']),
      P['rw_a2'].astype(BF16), P['rw_g2'].astype(BF16), row(P['rw_kk']), row(P['rw_ka']), ones_bd)


def _rw_scaled(lw, kap, bet, tri):
    cs = _dot_exact_lhs(tri, lw)
    return cs, kap * jnp.exp(cs - lw), bet * jnp.exp(-cs)


def _tri_masks(c):
    row = lax.broadcasted_iota(jnp.int32, (c, c), 0)
    col = lax.broadcasted_iota(jnp.int32, (c, c), 1)
    return row, col


def _chunk_tri(rows, c):
    row, col = _tri_masks(rows)
    return jnp.where((col <= row) & (row // c == col // c), 1.0, 0.0).astype(BF16)


def _rw_ldiag_body(lw_ref, kap_ref, bet_ref, o_ref, *, c):
    rows = lw_ref.shape[0]
    _, kk_all, bt_all = _rw_scaled(lw_ref[...], kap_ref[...], bet_ref[...], _chunk_tri(rows, c))
    srow, scol = _tri_masks(RW_SUB)
    units = [(cc, h) for cc in range(rows // c) for h in range(RW_HEADS)]
    Ls = [_dot_lo(kk_all[cc * c:(cc + 1) * c, h * RW_HD:(h + 1) * RW_HD],
                  bt_all[cc * c:(cc + 1) * c, h * RW_HD:(h + 1) * RW_HD], _NT) for cc, h in units]
    for (cc, h), L in zip(units, Ls):
        for b in range(c // RW_SUB):
            rs = slice(b * RW_SUB, (b + 1) * RW_SUB)
            o_ref[cc * c + b * RW_SUB:cc * c + (b + 1) * RW_SUB, h * RW_SUB:(h + 1) * RW_SUB] = (
                jnp.where(scol < srow, L[rs, rs], 0.0))


def _rw_inv_body(l_ref, t_ref, a_ref, b_ref):
    n = RW_SUB
    nblk = l_ref.shape[0] // n
    for t in range(n):
        a_ref[t] = l_ref[pl.ds(t, nblk, stride=n), :].T
    entry = lambda ref, t, s: ref.at[t, pl.ds(s, RW_HEADS, stride=n), :]
    one = jnp.ones((RW_HEADS, nblk), F32)
    zero = jnp.zeros((RW_HEADS, nblk), F32)
    for t in range(n):
        for s in range(n):
            if s > t:
                entry(b_ref, t, s)[...] = zero
            elif s == t:
                entry(b_ref, t, s)[...] = one
            else:
                acc = entry(a_ref, t, s)[...]
                for j in range(s + 1, t):
                    acc = acc + entry(a_ref, t, j)[...] * entry(b_ref, j, s)[...]
                entry(b_ref, t, s)[...] = -acc
    for t in range(n):
        t_ref[pl.ds(t, nblk, stride=n), :] = b_ref[t].T


def _rw_main_body(r_ref, lw_ref, k_ref, v_ref, kap_ref, bet_ref, g_ref, td_ref, h0_ref, rk_ref, lng_ref, lnb_ref,
                  o_ref, hout_ref, h_ref, *, c):
    ci = pl.program_id(1)
    rows = r_ref.shape[0]
    nb = c // RW_SUB

    @pl.when(ci == 0)
    def _():
        h_ref[...] = h0_ref[0]

    crow, ccol = _tri_masks(c)
    strict = ccol < crow
    incl = ccol <= crow
    lw = lw_ref[...]
    cs, kk_all, bt_all = _rw_scaled(lw, kap_ref[...], bet_ref[...], _chunk_tri(rows, c))
    gi = jnp.exp(-cs)
    gg = jnp.exp(cs)
    kt_all = k_ref[...] * gi
    rt_all = r_ref[...] * gg
    bonus_all = r_ref[...] * k_ref[...] * rk_ref[...]
    hrow = lax.broadcasted_iota(jnp.int32, (RW_HD, RW_HD), 0)
    hcol = lax.broadcasted_iota(jnp.int32, (RW_HD, RW_HD), 1)
    HS = range(RW_HEADS)
    units = [(cc, h) for cc in range(rows // c) for h in HS]
    US = range(len(units))
    rsl = [slice(cc * c, (cc + 1) * c) for cc, _ in units]
    lsl = [slice(h * RW_HD, (h + 1) * RW_HD) for _, h in units]
    Kk = [kk_all[rsl[u], lsl[u]] for u in US]
    Bt = [bt_all[rsl[u], lsl[u]] for u in US]
    Kt = [kt_all[rsl[u], lsl[u]] for u in US]
    Rt = [rt_all[rsl[u], lsl[u]] for u in US]
    vv = [v_ref[rsl[u], lsl[u]] for u in US]
    Lm = [jnp.where(strict, _dot_lo(Kk[u], Bt[u], _NT), 0.0) for u in US]
    A1 = [jnp.where(strict, _dot_lo(Kk[u], Kt[u], _NT), 0.0) for u in US]
    A4 = [jnp.where(incl, _dot_lo(Rt[u], Bt[u], _NT), 0.0) for u in US]
    A3 = [jnp.where(incl, _dot_lo(Rt[u], Kt[u], _NT), 0.0) for u in US]
    X = [jnp.concatenate([Kk[u], _dot_lo(A1[u], vv[u])], axis=1) for u in US]
    zs = [[] for _ in US]
    for b in range(nb):
        rs = slice(b * RW_SUB, (b + 1) * RW_SUB)
        rhs = [X[u][rs] for u in US]
        if b:
            rhs = [rhs[u] - _dot_lo(Lm[u][rs, 0:b * RW_SUB], jnp.concatenate(zs[u], axis=0)) for u in US]
        for u, (cc, h) in enumerate(units):
            tbb = td_ref[cc * c + b * RW_SUB:cc * c + (b + 1) * RW_SUB, h * RW_SUB:(h + 1) * RW_SUB]
            zs[u].append(_dot_lo(tbb, rhs[u]))
    Z = [jnp.concatenate(zs[u], axis=0) if nb > 1 else zs[u][0] for u in US]
    A4Z = [_dot_lo(A4[u], Z[u]) for u in US]
    Rhat = [Rt[u] - A4Z[u][:, :RW_HD] for u in US]
    Yhat = [_dot_lo(A3[u], vv[u]) - A4Z[u][:, RW_HD:] for u in US]
    gC = [gg[(cc + 1) * c - 1:(cc + 1) * c, lsl[u]] for u, (cc, _) in enumerate(units)]
    MN = [_dot_lo(Bt[u] * gC[u], Z[u], _TN) for u in US]
    Mp = [jnp.where(hrow == hcol, gC[u], 0.0) - MN[u][:, :RW_HD] for u in US]
    Np = [_dot_lo(Kt[u] * gC[u], vv[u], _TN) - MN[u][:, RW_HD:] for u in US]
    H = [h_ref[h] for h in HS]
    ys = [None for _ in US]
    for cc in range(rows // c):
        for h in HS:
            u = cc * RW_HEADS + h
            ys[u] = _dot_lo(Rhat[u], H[h]) + Yhat[u]
        H = [_dot_hi(Mp[cc * RW_HEADS + h], H[h]) + Np[cc * RW_HEADS + h] for h in HS]
    for h in HS:
        h_ref[h] = H[h]
    for u in US:
        y = ys[u]
        mu = jnp.mean(y, axis=-1, keepdims=True)
        var = jnp.mean(jnp.square(y - mu), axis=-1, keepdims=True)
        yn = (y - mu) * lax.rsqrt(var + RW_LN_EPS) * lng_ref[:, lsl[u]] + lnb_ref[:, lsl[u]]
        yn = yn + jnp.sum(bonus_all[rsl[u], lsl[u]], axis=-1, keepdims=True) * vv[u]
        o_ref[rsl[u], lsl[u]] = yn * g_ref[rsl[u], lsl[u]]

    @pl.when(ci == pl.num_programs(1) - 1)
    def _():
        hout_ref[0] = h_ref[...]


def _rwkv7(zr, shift, S0, B, T, P):
    G = GROUP_W
    n = B * T
    r, lw, k, v, kap, bet, g = _rw_prep(zr, shift, B, T, P)
    c = min(RW_CHUNK, T)
    rows_l = _row_tile(T, c * RW_LDIAG_CHUNKS)
    rows_m = _row_tile(T, c * RW_MAIN_CHUNKS)
    nl, nc = T // rows_l, T // rows_m
    tile_l = pl.BlockSpec((rows_l, G), lambda b, i: (b * nl + i, 0))
    tile = pl.BlockSpec((rows_m, G), lambda b, i: (b * nc + i, 0))
    ld = pl.pallas_call(
        functools.partial(_rw_ldiag_body, c=c),
        grid=(B, nl),
        in_specs=[tile_l] * 3,
        out_specs=pl.BlockSpec((rows_l, RW_HEADS * RW_SUB), lambda b, i: (b * nl + i, 0)),
        out_shape=jax.ShapeDtypeStruct((n, RW_HEADS * RW_SUB), F32),
        compiler_params=_params(("parallel", "parallel"), 32),
        name="rwkv_ldiag",
    )(lw, kap, bet)
    rows_i = V7X_LANES * RW_SUB
    npad = -(-n // rows_i) * rows_i
    inv_spec = pl.BlockSpec((rows_i, RW_HEADS * RW_SUB), lambda i: (i, 0))
    inv_scratch = pltpu.VMEM((RW_SUB, RW_HEADS * RW_SUB, V7X_LANES), F32)
    td = pl.pallas_call(
        _rw_inv_body,
        grid=(npad // rows_i,),
        in_specs=[inv_spec],
        out_specs=inv_spec,
        out_shape=jax.ShapeDtypeStruct((npad, RW_HEADS * RW_SUB), F32),
        scratch_shapes=[inv_scratch, inv_scratch],
        compiler_params=_params(("parallel",), 32),
        name="rwkv_inv",
    )(jnp.pad(ld, ((0, npad - n), (0, 0))))[:n]
    h0 = jnp.swapaxes(S0, -1, -2)
    prow = lambda x: pl.BlockSpec((1, G), lambda b, i: (0, 0))
    st_spec = pl.BlockSpec((1, RW_HEADS, RW_HD, RW_HD), lambda b, i: (b, 0, 0, 0))
    out, hl = pl.pallas_call(
        functools.partial(_rw_main_body, c=c),
        grid=(B, nc),
        in_specs=[tile] * 7 + [pl.BlockSpec((rows_m, RW_HEADS * RW_SUB), lambda b, i: (b * nc + i, 0)), st_spec,
                               prow(0), prow(0), prow(0)],
        out_specs=[tile, st_spec],
        out_shape=[jax.ShapeDtypeStruct((n, G), F32), jax.ShapeDtypeStruct(S0.shape, F32)],
        scratch_shapes=[pltpu.VMEM((RW_HEADS, RW_HD, RW_HD), F32)],
        compiler_params=_params(("parallel", "arbitrary"), 32),
        name="rwkv_main",
    )(r, lw, k, v, kap, bet, g, td, h0, P['rw_rk'].reshape(1, G), P['rw_ln_g'].reshape(1, G),
      P['rw_ln_b'].reshape(1, G))
    return out, zr.reshape(B, T, -1)[:, -1], jnp.swapaxes(hl, -1, -2)


def _even_mixer(x2, B, T, g, st, P):
    conv_buf, lru_h, k_past, v_past, lf_past = st
    G = GROUP_W
    z_rg, z_qkv, z_og, z_fl = _norm_matmul(x2, g, P['e_w_in'], (2 * G, 3 * G, G, V7X_LANES))
    rnn_out, conv_new, h_last = _lru(z_rg, conv_buf, lru_h, B, T, P)
    qb, kn, kb, v, vb, lf = _fox_prep(z_qkv, z_fl, B, T, P)
    past = k_past.shape[1]
    lf_all = lf.reshape(B, T, V7X_LANES)
    kb_all, vb_all = kb.reshape(B, T, G), vb.reshape(B, T, G)
    if past:
        lf_all = jnp.concatenate([jnp.pad(lf_past, ((0, 0), (0, 0), (0, V7X_LANES - FOX_HEADS))), lf_all], axis=1)
        kb_all = jnp.concatenate([k_past.reshape(B, past, G).astype(BF16), kb_all], axis=1)
        vb_all = jnp.concatenate([v_past.reshape(B, past, G).astype(BF16), vb_all], axis=1)
    tail = ((0, 0), (0, -(past + T) % FOX_K_ROWS), (0, 0))
    ka, vt = _fox_keys(jnp.pad(lf_all, tail), jnp.pad(kb_all, tail), jnp.pad(vb_all, tail))
    fox_out = _fox_attention(qb, ka, vt, z_og, B, T, past)
    x2 = _out_proj(x2, rnn_out, fox_out, P['e_w_out'])
    heads = lambda t: t.reshape(B, T, FOX_HEADS, FOX_HD)
    return x2, (conv_new, h_last, heads(kn), heads(v), lf.reshape(B, T, V7X_LANES)[..., :FOX_HEADS])


def _odd_mixer(x2, B, T, g, st, lb, P):
    S_hg, shift, S_rw = st
    G = GROUP_W
    z_hg, z_rw = _norm_matmul(x2, g, P['o_w_in'], (4 * G, P['o_w_in'].shape[1] - 4 * G))
    hg_out, S_hg_new = _hgrn2(z_hg, lb, S_hg, B, T, P)
    rw_out, shift_new, S_rw_new = _rwkv7(z_rw, shift, S_rw, B, T, P)
    x2 = _out_proj(x2, hg_out, rw_out, P['o_w_out'])
    return x2, (S_hg_new, shift_new, S_rw_new)


def _trunk(x, states, W):
    lru_conv, lru_h, fox_k, fox_v, fox_lf, hg_S, rw_shift, rw_S = states
    B, T, D = x.shape
    depth = W['norm_g'].shape[0]
    sm = jax.nn.softmax(W['hg_lb_logits'], axis=0)
    lower_bounds = jnp.cumsum(sm, axis=0) - sm[0]
    x2 = x.reshape(B * T, D)
    even_new, odd_new = [], []
    for layer in range(depth):
        g = W['norm_g'][layer]
        x2 = _ffn(x2, g[0], W['ffn_w_in'][layer][0], W['ffn_w_out'][layer][0])
        if layer % 2 == 0:
            e = layer // 2
            P = {n: W[n][e] for n in ('e_w_in', 'e_w_out', 'lru_conv_w', 'lru_conv_b', 'lru_wa', 'lru_ba', 'lru_wx',
                                      'lru_bx', 'lru_lambda', 'fox_q_gain', 'fox_k_gain', 'fox_f_bias')}
            x2, new = _even_mixer(x2, B, T, g[1], (lru_conv[e], lru_h[e], fox_k[e], fox_v[e], fox_lf[e]), P)
            even_new.append(new)
        else:
            o = layer // 2
            P = {n: W[n][o] for n in ('o_w_in', 'o_w_out', 'hg_norm_g', 'rw_mu', 'rw_w0', 'rw_w2', 'rw_a0', 'rw_a2',
                                      'rw_g2', 'rw_kk', 'rw_ka', 'rw_rk', 'rw_ln_g', 'rw_ln_b')}
            x2, new = _odd_mixer(x2, B, T, g[1], (hg_S[o], rw_shift[o], rw_S[o]), lower_bounds[layer], P)
            odd_new.append(new)
        x2 = _ffn(x2, g[2], W['ffn_w_in'][layer][1], W['ffn_w_out'][layer][1])
    ev = [jnp.stack([n[j] for n in even_new]) for j in range(5)]
    od = [jnp.stack([n[j] for n in odd_new]) for j in range(3)]
    return x2.reshape(B, T, D), (ev[0], ev[1], ev[2], ev[3], ev[4], od[0], od[1], od[2])


def kernel(x_prompt, x_sample, state_lru_conv, state_lru_h, cache_fox_k, cache_fox_v, cache_fox_logf,
           state_hgrn_S, state_rwkv_shift, state_rwkv_S, norm_g, ffn_w_in, ffn_w_out, e_w_in, e_w_out,
           lru_conv_w, lru_conv_b, lru_wa, lru_ba, lru_wx, lru_bx, lru_lambda, fox_q_gain, fox_k_gain,
           fox_f_bias, o_w_in, o_w_out, hg_lb_logits, hg_norm_g, rw_mu, rw_w0, rw_w2, rw_a0, rw_a2, rw_g2,
           rw_kk, rw_ka, rw_rk, rw_ln_g, rw_ln_b):
    n_even, n_odd = e_w_in.shape[0], o_w_in.shape[0]
    W = dict(norm_g=norm_g, ffn_w_in=_ffn_w_in_tiles(ffn_w_in), ffn_w_out=ffn_w_out.astype(BF16),
             e_w_in=_pad_cols(e_w_in.astype(BF16)), e_w_out=e_w_out.astype(BF16),
             lru_conv_w=lru_conv_w, lru_conv_b=lru_conv_b, lru_wa=lru_wa, lru_ba=lru_ba, lru_wx=lru_wx,
             lru_bx=lru_bx, lru_lambda=lru_lambda, fox_q_gain=fox_q_gain, fox_k_gain=fox_k_gain,
             fox_f_bias=fox_f_bias, o_w_in=o_w_in.astype(BF16), o_w_out=o_w_out.astype(BF16),
             hg_lb_logits=hg_lb_logits, hg_norm_g=hg_norm_g, rw_mu=rw_mu, rw_w0=rw_w0, rw_w2=rw_w2, rw_a0=rw_a0,
             rw_a2=rw_a2, rw_g2=rw_g2, rw_kk=rw_kk, rw_ka=rw_ka, rw_rk=rw_rk, rw_ln_g=rw_ln_g, rw_ln_b=rw_ln_b)
    nb = x_prompt.shape[0]
    dt = x_prompt.dtype
    prompt_states = (jnp.zeros((n_even, nb, CONV_W - 1, GROUP_W), dt),
                     jnp.zeros((n_even, nb, GROUP_W), dt),
                     jnp.zeros((n_even, nb, 0, FOX_HEADS, FOX_HD), dt),
                     jnp.zeros((n_even, nb, 0, FOX_HEADS, FOX_HD), dt),
                     jnp.zeros((n_even, nb, 0, FOX_HEADS), dt),
                     jnp.zeros((n_odd, nb, HG_HEADS, GROUP_W // HG_HEADS, GROUP_W // HG_HEADS), dt),
                     jnp.zeros((n_odd, nb, rw_mu.shape[1]), dt),
                     jnp.zeros((n_odd, nb, RW_HEADS, RW_HD, RW_HD), dt))
    sample_states = (state_lru_conv, state_lru_h, cache_fox_k, cache_fox_v, cache_fox_logf,
                     state_hgrn_S, state_rwkv_shift, state_rwkv_S)
    y_prompt, p_new = _trunk(x_prompt, prompt_states, W)
    y_sample, s_new = _trunk(x_sample, sample_states, W)
    lru_conv_p, lru_h_p, fox_k_p, fox_v_p, fox_logf_p, hgrn_S_p, rwkv_shift_p, rwkv_S_p = p_new
    lru_conv_s, lru_h_s, fox_k_s, fox_v_s, fox_logf_s, hgrn_S_s, rwkv_shift_s, rwkv_S_s = s_new
    return (y_prompt, y_sample, lru_conv_p, lru_conv_s, lru_h_p, lru_h_s, fox_k_p, fox_k_s, fox_v_p, fox_v_s,
            fox_logf_p, fox_logf_s, hgrn_S_p, hgrn_S_s, rwkv_shift_p, rwkv_shift_s, rwkv_S_p, rwkv_S_s)
```

```python
import functools

import jax
import jax.numpy as jnp
from jax import lax
from jax.experimental import pallas as pl
from jax.experimental.pallas import tpu as pltpu

F32 = jnp.float32
BF16 = jnp.bfloat16

NORM_EPS = 1e-6
GROUP_W = 512
CONV_W = 4
LRU_C = 8.0
FOX_HEADS = 8
FOX_HD = 64
HG_HEADS = 4
RW_HEADS = 8
RW_HD = 64
RW_DECAY_LORA = 64
RW_A_LORA = 64
RW_GATE_LORA = 128
RW_LN_EPS = 64e-5

V7X_LANES = 128
FFN_COL_TILE = 1408
PREP_ROWS = 512
FFN_ROW_TILE = 1024
FFN_VMEM_MIB = 60


def _row_tile(n, want):
    t = min(n, want)
    while n % t:
        t //= 2
    return t


def _params(sem, vmem_mib):
    return pltpu.CompilerParams(dimension_semantics=sem, vmem_limit_bytes=vmem_mib << 20)


def _pad_cols(w):
    pad = -w.shape[-1] % V7X_LANES
    return jnp.pad(w, [(0, 0)] * (w.ndim - 1) + [(0, pad)])


def _rms(x, g):
    return x * lax.rsqrt(jnp.mean(x * x, axis=-1, keepdims=True) + NORM_EPS) * g


def _ffn_body(x_ref, g_ref, wi_ref, wo_ref, o_ref, h_ref, acc_ref):
    j = pl.program_id(1)

    @pl.when(j == 0)
    def _():
        h_ref[...] = _rms(x_ref[...], g_ref[...]).astype(BF16)
        acc_ref[...] = jnp.zeros_like(acc_ref)

    tf = wo_ref.shape[0]
    gu = jnp.dot(h_ref[...], wi_ref[...], preferred_element_type=F32)
    gate, up = gu[:, :tf], gu[:, tf:]
    act = (gate * jax.nn.sigmoid(gate) * up).astype(BF16)
    acc_ref[...] += jnp.dot(act, wo_ref[...], preferred_element_type=F32)

    @pl.when(j == pl.num_programs(1) - 1)
    def _():
        o_ref[...] = x_ref[...] + 0.5 * acc_ref[...]


def _cast_body(x_ref, o_ref):
    o_ref[...] = x_ref[...].astype(o_ref.dtype)


def _ffn_w_in_tiles(w_in):
    *lead, d, f2 = w_in.shape
    tf = FFN_COL_TILE
    nf = f2 // 2 // tf
    w = w_in.reshape(-1, d, f2)
    out = pl.pallas_call(
        _cast_body,
        grid=(w.shape[0], nf, 2),
        in_specs=[pl.BlockSpec((1, d, tf), lambda i, j, gu: (i, 0, gu * nf + j))],
        out_specs=pl.BlockSpec((1, d, tf), lambda i, j, gu: (i, 0, 2 * j + gu)),
        out_shape=jax.ShapeDtypeStruct(w.shape, BF16),
        compiler_params=_params(("parallel", "parallel", "parallel"), 32),
        name="ffn_weight_tiles",
    )(w)
    return out.reshape(*lead, d, f2)


def _ffn(x, g, w_in, w_out):
    n, d = x.shape
    f = w_out.shape[0]
    tm = _row_tile(n, FFN_ROW_TILE)
    tf = FFN_COL_TILE
    nf = f // tf
    return pl.pallas_call(
        _ffn_body,
        grid=(n // tm, nf),
        in_specs=[
            pl.BlockSpec((tm, d), lambda i, j: (i, 0)),
            pl.BlockSpec((1, d), lambda i, j: (0, 0)),
            pl.BlockSpec((d, 2 * tf), lambda i, j: (0, j)),
            pl.BlockSpec((tf, d), lambda i, j: (j, 0)),
        ],
        out_specs=pl.BlockSpec((tm, d), lambda i, j: (i, 0)),
        out_shape=jax.ShapeDtypeStruct((n, d), F32),
        scratch_shapes=[pltpu.VMEM((tm, d), BF16), pltpu.VMEM((tm, d), F32)],
        compiler_params=_params(("parallel", "arbitrary"), FFN_VMEM_MIB),
        name="ffn",
    )(x, g.reshape(1, d), w_in, w_out)


def _norm_matmul_body(x_ref, g_ref, w_ref, *o_refs):
    h = _rms(x_ref[...], g_ref[...]).astype(BF16)
    z = jnp.dot(h, w_ref[...], preferred_element_type=F32)
    start = 0
    for o_ref in o_refs:
        width = o_ref.shape[1]
        o_ref[...] = z[:, start:start + width]
        start += width


def _norm_matmul(x, g, w, widths):
    n, d = x.shape
    c = w.shape[1]
    assert sum(widths) == c and all(wd % V7X_LANES == 0 for wd in widths)
    tm = _row_tile(n, 512)
    return pl.pallas_call(
        _norm_matmul_body,
        grid=(n // tm,),
        in_specs=[
            pl.BlockSpec((tm, d), lambda i: (i, 0)),
            pl.BlockSpec((1, d), lambda i: (0, 0)),
            pl.BlockSpec((d, c), lambda i: (0, 0)),
        ],
        out_specs=[pl.BlockSpec((tm, wd), lambda i: (i, 0)) for wd in widths],
        out_shape=[jax.ShapeDtypeStruct((n, wd), F32) for wd in widths],
        compiler_params=_params(("parallel",), 48),
        name="norm_matmul",
    )(x, g.reshape(1, d), w)


def _out_proj_body(x_ref, a_ref, b_ref, wa_ref, wb_ref, o_ref):
    acc = jnp.dot(a_ref[...].astype(BF16), wa_ref[...], preferred_element_type=F32)
    acc += jnp.dot(b_ref[...].astype(BF16), wb_ref[...], preferred_element_type=F32)
    o_ref[...] = x_ref[...] + acc


def _out_proj(x, a, b, w):
    n, d = x.shape
    ga, gb = a.shape[1], b.shape[1]
    tm = _row_tile(n, 512)
    return pl.pallas_call(
        _out_proj_body,
        grid=(n // tm,),
        in_specs=[
            pl.BlockSpec((tm, d), lambda i: (i, 0)),
            pl.BlockSpec((tm, ga), lambda i: (i, 0)),
            pl.BlockSpec((tm, gb), lambda i: (i, 0)),
            pl.BlockSpec((ga, d), lambda i: (0, 0)),
            pl.BlockSpec((gb, d), lambda i: (0, 0)),
        ],
        out_specs=pl.BlockSpec((tm, d), lambda i: (i, 0)),
        out_shape=jax.ShapeDtypeStruct((n, d), F32),
        compiler_params=_params(("parallel",), 32),
        name="out_proj",
    )(x, a, b, w[:ga], w[ga:])


LRU_ROWS = 256
CONV_PAD = 8


def _expm1(x):
    series = x * (1.0 + x * (1 / 2 + x * (1 / 6 + x * (1 / 24 + x * (1 / 120 + x * (1 / 720 + x * (1 / 5040 + x * (1 / 40320))))))))
    return jnp.where(jnp.abs(x) < 0.25, series, jnp.exp(x) - 1.0)


def _shift_rows(x, s, fill):
    row = lax.broadcasted_iota(jnp.int32, x.shape, 0)
    return jnp.where(row >= s, pltpu.roll(x, s, axis=0), fill)


def _lru_body(z_ref, buf_ref, h0_ref, cw_ref, cb_ref, wa_ref, ba_ref, wx_ref, bx_ref, lam_ref,
              o_ref, bufo_ref, ho_ref, x_ref, hc_ref):
    G = GROUP_W
    tt = z_ref.shape[0]

    @pl.when(pl.program_id(1) == 0)
    def _():
        x_ref[0:CONV_PAD, :] = buf_ref[0]
        hc_ref[...] = jnp.broadcast_to(h0_ref[0], hc_ref.shape)

    x_ref[CONV_PAD:CONV_PAD + tt, :] = z_ref[:, 0:G]
    xc = cb_ref[...]
    for j in range(CONV_W):
        lo = CONV_PAD - (CONV_W - 1) + j
        xc = xc + x_ref[lo:lo + tt, :] * cw_ref[j:j + 1, :]
    hist = x_ref[tt:tt + CONV_PAD, :]
    x_ref[0:CONV_PAD, :] = hist
    bufo_ref[0] = hist

    xb = xc.astype(BF16)
    r = jax.nn.sigmoid(jnp.dot(xb, wa_ref[...], preferred_element_type=F32) + ba_ref[...])
    ig = jax.nn.sigmoid(jnp.dot(xb, wx_ref[...], preferred_element_type=F32) + bx_ref[...])
    log_a = (-LRU_C * _softplus(-lam_ref[...])) * r
    a = jnp.exp(log_a)
    b = jnp.sqrt(-_expm1(2.0 * log_a)) * (ig * xc)
    s = 1
    while s < tt:
        if s % 8:
            b = a * _shift_rows(b, s, 0.0) + b
            a = a * _shift_rows(a, s, 1.0)
        else:
            b = jnp.concatenate([b[:s], a[s:] * b[:tt - s] + b[s:]], axis=0)
            a = jnp.concatenate([a[:s], a[s:] * a[:tt - s]], axis=0)
        s *= 2
    h = a * hc_ref[0:1, :] + b
    hc_ref[...] = jnp.broadcast_to(h[tt - 1:tt, :], hc_ref.shape)
    ho_ref[0] = h[tt - 1:tt, :]
    o_ref[...] = jax.nn.gelu(z_ref[:, G:2 * G]) * h


def _block_diag_dense(w):
    nb, bs, _ = w.shape
    eye = jnp.eye(nb, dtype=w.dtype)
    return (eye[:, None, :, None] * w[:, :, None, :]).reshape(nb * bs, nb * bs)


def _lru(z_rg, conv_buf, h0, B, T, P):
    G = GROUP_W
    n = B * T
    tt = _row_tile(T, LRU_ROWS)
    nt = T // tt
    buf = jnp.pad(conv_buf, ((0, 0), (CONV_PAD - (CONV_W - 1), 0), (0, 0)))
    cw = jnp.pad(P['lru_conv_w'], ((0, CONV_PAD - CONV_W), (0, 0)))
    row = lambda x: x.reshape(1, G)
    full = lambda shape: pl.BlockSpec(shape, lambda b, i: (0,) * len(shape))
    out, bufo, ho = pl.pallas_call(
        _lru_body,
        grid=(B, nt),
        in_specs=[
            pl.BlockSpec((tt, 2 * G), lambda b, i: (b * nt + i, 0)),
            pl.BlockSpec((1, CONV_PAD, G), lambda b, i: (b, 0, 0)),
            pl.BlockSpec((1, 1, G), lambda b, i: (b, 0, 0)),
            full((CONV_PAD, G)), full((1, G)), full((G, G)), full((1, G)), full((G, G)), full((1, G)), full((1, G)),
        ],
        out_specs=[
            pl.BlockSpec((tt, G), lambda b, i: (b * nt + i, 0)),
            pl.BlockSpec((1, CONV_PAD, G), lambda b, i: (b, 0, 0)),
            pl.BlockSpec((1, 1, G), lambda b, i: (b, 0, 0)),
        ],
        out_shape=[jax.ShapeDtypeStruct((n, G), F32), jax.ShapeDtypeStruct((B, CONV_PAD, G), F32),
                   jax.ShapeDtypeStruct((B, 1, G), F32)],
        scratch_shapes=[pltpu.VMEM((tt + CONV_PAD, G), F32), pltpu.VMEM((8, G), F32)],
        compiler_params=_params(("parallel", "arbitrary"), 32),
        name="lru",
    )(z_rg, buf, h0.reshape(B, 1, G), cw, row(P['lru_conv_b']), _block_diag_dense(P['lru_wa']).astype(BF16),
      row(P['lru_ba']), _block_diag_dense(P['lru_wx']).astype(BF16), row(P['lru_bx']), row(P['lru_lambda']))
    return out, bufo[:, CONV_PAD - (CONV_W - 1):], ho.reshape(B, G)


FOX_Q_COLS = 2048
FOX_K_ROWS = 512
FOX_F_SPLIT = 3
FOX_NEG = -1e30
LOG2E = 1.4426950408889634
HEAD_PAIRS = FOX_HEADS // 2
PAIR_W = 2 * FOX_HD
FOX_KEY_TILE = 512


def _fox_prep_body(z_ref, fl_ref, qg_ref, kg_ref, fb_ref, ones_ref, q_ref, k_ref, kb_ref, v_ref, vb_ref, lf_ref,
                   *, q_transposed):
    G = GROUP_W
    q, k, v = z_ref[:, 0:G], z_ref[:, G:2 * G], z_ref[:, 2 * G:3 * G]
    inv = 1.0 / FOX_HD
    qn = q * lax.rsqrt(_dot_exact_rhs(q * q, ones_ref[...]) * inv + NORM_EPS) * qg_ref[...]
    kn = k * lax.rsqrt(_dot_exact_rhs(k * k, ones_ref[...]) * inv + NORM_EPS) * kg_ref[...]
    qs = qn * (LOG2E * FOX_HD ** -0.5)
    if q_transposed:
        for p in range(HEAD_PAIRS):
            q_ref[0, p] = qs[:, p * PAIR_W:(p + 1) * PAIR_W].T.astype(BF16)
    else:
        q_ref[...] = qs.astype(BF16)
    k_ref[...] = kn
    kb_ref[...] = kn.astype(BF16)
    v_ref[...] = v
    vb_ref[...] = v.astype(BF16)
    x = fl_ref[...] + fb_ref[...]
    lf_ref[...] = -_softplus(-x)


def _fox_prep(z_qkv, z_fl, B, T, P):
    n = z_qkv.shape[0]
    G = GROUP_W
    tt = _row_tile(T, PREP_ROWS)
    nt = T // tt
    q_transposed = tt % V7X_LANES == 0
    ones_bd = jnp.kron(jnp.eye(FOX_HEADS, dtype=F32), jnp.ones((FOX_HD, FOX_HD), F32)).astype(BF16)
    fb = jnp.pad(P['fox_f_bias'], (0, V7X_LANES - FOX_HEADS)).reshape(1, V7X_LANES)
    tile = lambda w: pl.BlockSpec((tt, w), lambda b, i: (b * nt + i, 0))
    full = lambda shape: pl.BlockSpec(shape, lambda b, i: (0,) * len(shape))
    if q_transposed:
        q_spec = pl.BlockSpec((1, HEAD_PAIRS, PAIR_W, tt), lambda b, i: (b, 0, 0, i))
        q_shape = jax.ShapeDtypeStruct((B, HEAD_PAIRS, PAIR_W, T), BF16)
    else:
        q_spec, q_shape = tile(G), jax.ShapeDtypeStruct((n, G), BF16)
    return pl.pallas_call(
        functools.partial(_fox_prep_body, q_transposed=q_transposed),
        grid=(B, nt),
        in_specs=[tile(3 * G), tile(V7X_LANES), full((1, G)), full((1, G)), full((1, V7X_LANES)), full((G, G))],
        out_specs=[q_spec] + [tile(G)] * 4 + [tile(V7X_LANES)],
        out_shape=[q_shape, jax.ShapeDtypeStruct((n, G), F32),
                   jax.ShapeDtypeStruct((n, G), BF16), jax.ShapeDtypeStruct((n, G), F32),
                   jax.ShapeDtypeStruct((n, G), BF16), jax.ShapeDtypeStruct((n, V7X_LANES), F32)],
        compiler_params=_params(("parallel", "parallel"), 32),
        name="fox_prep",
    )(z_qkv, z_fl, jnp.tile(P['fox_q_gain'], FOX_HEADS).reshape(1, G),
      jnp.tile(P['fox_k_gain'], FOX_HEADS).reshape(1, G), fb, ones_bd)


def _fox_keys_body(lf_ref, kb_ref, vb_ref, ka_ref, vt_ref, c_ref):
    tt = lf_ref.shape[1]

    @pl.when(pl.program_id(1) == 0)
    def _():
        c_ref[...] = jnp.zeros_like(c_ref)

    row, col = _tri_masks(tt)
    tri = jnp.where(col <= row, 1.0, 0.0).astype(BF16)
    f = _dot_exact_lhs(tri, lf_ref[0]) + c_ref[0:1, :]
    c_ref[...] = jnp.broadcast_to(f[tt - 1:tt, :], c_ref.shape)
    parts = _split3(f * LOG2E)
    srow = lax.broadcasted_iota(jnp.int32, (V7X_LANES, FOX_HD), 0)
    scol = lax.broadcasted_iota(jnp.int32, (V7X_LANES, FOX_HD), 1)
    for h in range(FOX_HEADS):
        aug = jnp.zeros((tt, FOX_HD), F32)
        for t, part in enumerate(parts):
            sel = jnp.where((srow == h) & (scol == t), 1.0, 0.0).astype(BF16)
            aug = aug + jnp.dot(part, sel, preferred_element_type=F32)
        ka_ref[0, h] = jnp.concatenate([kb_ref[0, :, h * FOX_HD:(h + 1) * FOX_HD], aug.astype(BF16)], axis=1)
    for p in range(HEAD_PAIRS):
        vt_ref[0, p] = vb_ref[0, :, p * PAIR_W:(p + 1) * PAIR_W].astype(F32).T.astype(BF16)


def _fox_keys(lf_all, kb_all, vb_all):
    B, tk_all, L = lf_all.shape
    G = GROUP_W
    tt = FOX_KEY_TILE
    return pl.pallas_call(
        _fox_keys_body,
        grid=(B, tk_all // tt),
        in_specs=[pl.BlockSpec((1, tt, L), lambda b, i: (b, i, 0)),
                  pl.BlockSpec((1, tt, G), lambda b, i: (b, i, 0)),
                  pl.BlockSpec((1, tt, G), lambda b, i: (b, i, 0))],
        out_specs=[pl.BlockSpec((1, FOX_HEADS, tt, 2 * FOX_HD), lambda b, i: (b, 0, i, 0)),
                   pl.BlockSpec((1, HEAD_PAIRS, PAIR_W, tt), lambda b, i: (b, 0, 0, i))],
        out_shape=[jax.ShapeDtypeStruct((B, FOX_HEADS, tk_all, 2 * FOX_HD), BF16),
                   jax.ShapeDtypeStruct((B, HEAD_PAIRS, PAIR_W, tk_all), BF16)],
        scratch_shapes=[pltpu.VMEM((8, L), F32)],
        compiler_params=_params(("parallel", "arbitrary"), 32),
        name="fox_keys",
    )(lf_all, kb_all, vb_all)


def _fox_attn_body(qt_ref, ka_ref, vt_ref, og_ref, o_ref, acc_ref, m_ref, l_ref, *, past, tk, t_real):
    qi = pl.program_id(2)
    tq = qt_ref.shape[3]
    t_out = o_ref.shape[0]
    first_q = past + qi * tq
    last_q = past + jnp.minimum(qi * tq + tq, t_real) - 1
    n_full = (first_q + 1) // tk
    n_all = last_q // tk + 1
    drow = lax.broadcasted_iota(jnp.int32, (FOX_HD, tq), 0)
    minus = jnp.where(drow < FOX_F_SPLIT, -1.0, 0.0).astype(BF16)
    rhs = [jnp.concatenate([qt_ref[0, 0, h * FOX_HD:(h + 1) * FOX_HD, :], minus], axis=0) for h in range(2)]
    acc_ref[...] = jnp.zeros_like(acc_ref)

    def update(ki, m_prev, l_prev, masked, q0=0):
        ks = pl.multiple_of(ki * tk, tk)
        s = [jnp.dot(ka_ref[0, h, pl.ds(ks, tk), :], rhs[h][:, q0:], preferred_element_type=F32) for h in range(2)]
        if masked:
            krow = lax.broadcasted_iota(jnp.int32, (tk, tq - q0), 0)
            qcol = lax.broadcasted_iota(jnp.int32, (tk, tq - q0), 1)
            vis = ks + krow <= first_q + q0 + qcol
            s = [jnp.where(vis, s[h], FOX_NEG) for h in range(2)]
        m_new = [jnp.maximum(m_prev[h], jnp.max(s[h], axis=0, keepdims=True)) for h in range(2)]
        alpha = [jnp.exp2(m_prev[h] - m_new[h]) for h in range(2)]
        p = [jnp.exp2(s[h] - m_new[h]) for h in range(2)]
        l_new = [alpha[h] * l_prev[h] + jnp.sum(p[h], axis=0, keepdims=True) for h in range(2)]
        vt = [vt_ref[0, 0, h * FOX_HD:(h + 1) * FOX_HD, pl.ds(ks, tk)] for h in range(2)]
        pv = [jnp.dot(vt[h], p[h].astype(BF16), preferred_element_type=F32) for h in range(2)]
        for h in range(2):
            acc_ref[h, :, q0:] = alpha[h] * acc_ref[h, :, q0:] + pv[h]
        return m_new, l_new

    def carried(ki, c, masked):
        return update(ki, c[0], c[1], masked)

    c = ([jnp.full((1, tq), FOX_NEG, F32)] * 2, [jnp.zeros((1, tq), F32)] * 2)
    c = lax.fori_loop(0, n_full, lambda ki, c: carried(ki, c, False), c)
    if tq > tk and tq % tk == 0 and past % tq == 0 and t_real % tq == 0:
        m, l = carried(n_full, c, True)
        for h in range(2):
            m_ref[h], l_ref[h] = m[h], l[h]
        for j in range(1, tq // tk):
            q0 = j * tk
            m, l = update(n_full + j, [m_ref[h, :, q0:] for h in range(2)], [l_ref[h, :, q0:] for h in range(2)],
                          True, q0=q0)
            for h in range(2):
                m_ref[h, :, q0:], l_ref[h, :, q0:] = m[h], l[h]
    else:
        _, l = lax.fori_loop(n_full, n_all, lambda ki, c: carried(ki, c, True), c)
        for h in range(2):
            l_ref[h] = l[h]
    o_t = jnp.concatenate([acc_ref[h] / l_ref[h] for h in range(2)], axis=0)
    o_ref[...] = o_t.T[:t_out] * jax.nn.sigmoid(og_ref[...])


def _fox_attention(q, ka, vt, z_og, B, T, past):
    G = GROUP_W
    pw = PAIR_W
    tq = max(_row_tile(T, FOX_Q_COLS), V7X_LANES)
    tqp = -(-T // tq) * tq
    nq = tqp // tq
    t_out = min(tq, T)
    tk = FOX_K_ROWS
    tkp = ka.shape[2]
    if q.ndim == 2:
        q = q.reshape(B, T, HEAD_PAIRS, pw).transpose(0, 2, 3, 1)
    qt = jnp.pad(q, ((0, 0), (0, 0), (0, 0), (0, tqp - T)))
    return pl.pallas_call(
        functools.partial(_fox_attn_body, past=past, tk=tk, t_real=T),
        grid=(B, HEAD_PAIRS, nq),
        in_specs=[
            pl.BlockSpec((1, 1, pw, tq), lambda b, p, i: (b, p, 0, i)),
            pl.BlockSpec((1, 2, tkp, 2 * FOX_HD), lambda b, p, i: (b, p, 0, 0)),
            pl.BlockSpec((1, 1, pw, tkp), lambda b, p, i: (b, p, 0, 0)),
            pl.BlockSpec((t_out, pw), lambda b, p, i: (b * nq + i, p)),
        ],
        out_specs=pl.BlockSpec((t_out, pw), lambda b, p, i: (b * nq + i, p)),
        out_shape=jax.ShapeDtypeStruct((B * T, G), F32),
        scratch_shapes=[pltpu.VMEM((2, FOX_HD, tq), F32), pltpu.VMEM((2, 1, tq), F32), pltpu.VMEM((2, 1, tq), F32)],
        compiler_params=_params(("parallel", "parallel", "arbitrary"), 40),
        name="fox_attn",
    )(qt, ka, vt, z_og)


HG_CHUNK = 64
HG_STEP_CHUNKS = 8


def _hgrn_body(z_ref, lb_ref, s0_ref, ng_ref, o_ref, so_ref, st_ref, *, c):
    G = GROUP_W
    rows = z_ref.shape[0]
    nch = rows // c
    dk = G // HG_HEADS

    @pl.when(pl.program_id(1) == 0)
    def _():
        st_ref[...] = s0_ref[0]

    lb = lb_ref[...]
    f = lb + (1.0 - lb) * jax.nn.sigmoid(z_ref[:, G:2 * G])
    kx = 1.0 - f
    crow, ccol = _tri_masks(c)
    incl = ccol <= crow
    gs = _dot_exact_lhs(_chunk_tri(rows, c), jnp.log(f))
    qg_all = z_ref[:, 0:G] * jnp.exp(gs)
    kg_all = kx * jnp.exp(-gs)
    HS = range(HG_HEADS)
    units = [(cc, h) for cc in range(nch) for h in HS]
    US = range(len(units))
    rsl = [slice(cc * c, (cc + 1) * c) for cc, _ in units]
    lsl = [slice(h * dk, (h + 1) * dk) for _, h in units]
    g_last = [gs[(cc + 1) * c - 1:(cc + 1) * c, lsl[u]] for u, (cc, _) in enumerate(units)]
    vv = [z_ref[rsl[u], 2 * G + h * dk:2 * G + (h + 1) * dk] for u, (_, h) in enumerate(units)]
    A = [jnp.where(incl, _dot_lo(qg_all[rsl[u], lsl[u]], kg_all[rsl[u], lsl[u]], _NT), 0.0) for u in US]
    av = [_dot_lo(A[u], vv[u]) for u in US]
    kd = [kx[rsl[u], lsl[u]] * jnp.exp(g_last[u] - gs[rsl[u], lsl[u]]) for u in US]
    upd = [_dot_lo(vv[u], kd[u], _TN) for u in US]
    st = [st_ref[h] for h in HS]
    o = [None for _ in US]
    for cc in range(nch):
        for h in HS:
            u = cc * HG_HEADS + h
            o[u] = _dot_lo(qg_all[rsl[u], lsl[u]], st[h], _NT) + av[u]
        st = [st[h] * jnp.exp(g_last[cc * HG_HEADS + h]) + upd[cc * HG_HEADS + h] for h in HS]
    for h in HS:
        st_ref[h] = st[h]
    for u, (_, h) in enumerate(units):
        hg = z_ref[rsl[u], 3 * G + h * dk:3 * G + (h + 1) * dk]
        o_ref[rsl[u], lsl[u]] = _rms(o[u], ng_ref[:, lsl[u]]) * (hg * jax.nn.sigmoid(hg))

    @pl.when(pl.program_id(1) == pl.num_programs(1) - 1)
    def _():
        so_ref[0] = st_ref[...]


def _hgrn2(z_hg, lb, S0, B, T, P):
    G = GROUP_W
    c = min(HG_CHUNK, T)
    rows = _row_tile(T, c * HG_STEP_CHUNKS)
    nc = T // rows
    dk = G // HG_HEADS
    st_spec = pl.BlockSpec((1, HG_HEADS, dk, dk), lambda b, i: (b, 0, 0, 0))
    out, so = pl.pallas_call(
        functools.partial(_hgrn_body, c=c),
        grid=(B, nc),
        in_specs=[pl.BlockSpec((rows, 4 * G), lambda b, i: (b * nc + i, 0)),
                  pl.BlockSpec((1, G), lambda b, i: (0, 0)), st_spec, pl.BlockSpec((1, G), lambda b, i: (0, 0))],
        out_specs=[pl.BlockSpec((rows, G), lambda b, i: (b * nc + i, 0)), st_spec],
        out_shape=[jax.ShapeDtypeStruct((B * T, G), F32), jax.ShapeDtypeStruct(S0.shape, F32)],
        scratch_shapes=[pltpu.VMEM((HG_HEADS, dk, dk), F32)],
        compiler_params=_params(("parallel", "arbitrary"), 32),
        name="hgrn2",
    )(z_hg, lb.reshape(1, G), jnp.swapaxes(S0, -1, -2), P['hg_norm_g'].reshape(1, G))
    return out, jnp.swapaxes(so, -1, -2)


RW_CHUNK = 64
RW_SUB = 16
RW_LDIAG_CHUNKS = 4
RW_MAIN_CHUNKS = 2

_NT = (((1,), (1,)), ((), ()))
_TN = (((0,), (0,)), ((), ()))
_NN = (((1,), (0,)), ((), ()))


def _split3(x):
    h1 = x.astype(BF16)
    r1 = x - h1.astype(F32)
    h2 = r1.astype(BF16)
    h3 = (r1 - h2.astype(F32)).astype(BF16)
    return h1, h2, h3


def _dot_lo(a, b, dims=_NN):
    return lax.dot_general(a.astype(BF16), b.astype(BF16), dims, preferred_element_type=F32)


def _dot_hi(a, b, dims=_NN):
    ah = a.astype(BF16)
    al = (a - ah.astype(F32)).astype(BF16)
    bh = b.astype(BF16)
    bl = (b - bh.astype(F32)).astype(BF16)
    d = functools.partial(lax.dot_general, dimension_numbers=dims, preferred_element_type=F32)
    return d(ah, bh) + (d(al, bh) + d(ah, bl))


def _dot_exact_rhs(a, b):
    h1, h2, h3 = _split3(a)
    d = functools.partial(jnp.dot, preferred_element_type=F32)
    return d(h1, b) + (d(h2, b) + d(h3, b))


def _dot_exact_lhs(a, b):
    h1, h2, h3 = _split3(b)
    d = functools.partial(jnp.dot, preferred_element_type=F32)
    return d(a, h1) + (d(a, h2) + d(a, h3))


def _softplus(x):
    return jnp.maximum(x, 0.0) + jnp.log1p(jnp.exp(-jnp.abs(x)))


def _rw_prep_body(z_ref, shift_ref, mu_ref, w0_ref, w2_ref, a0_ref, a2_ref, g2_ref, kk_ref, ka_ref, ones_ref,
                  r_ref, lw_ref, k_ref, v_ref, kap_ref, bet_ref, g_ref, prev_ref):
    G = GROUP_W

    @pl.when(pl.program_id(1) == 0)
    def _():
        prev_ref[0:1, :] = shift_ref[0]

    z = z_ref[...]
    tt = z.shape[0]
    row = lax.broadcasted_iota(jnp.int32, z.shape, 0)
    shifted = jnp.where(row == 0, prev_ref[0:1, :], pltpu.roll(z, 1, axis=0))
    prev_ref[0:1, :] = z[tt - 1:tt, :]
    zm = z + (shifted - z) * mu_ref[...]
    r, k, v = zm[:, 0:G], zm[:, G:2 * G], zm[:, 2 * G:3 * G]
    o = 3 * G
    wd = zm[:, o:o + RW_DECAY_LORA]
    ad = zm[:, o + RW_DECAY_LORA:o + RW_DECAY_LORA + RW_A_LORA]
    gd = zm[:, o + RW_DECAY_LORA + RW_A_LORA:]
    w = -_softplus(-(w0_ref[...] + _dot_lo(jnp.tanh(wd), w2_ref[...]))) - 0.5
    a = jax.nn.sigmoid(a0_ref[...] + _dot_lo(ad, a2_ref[...]))
    kk = k * kk_ref[...]
    ss = _dot_exact_rhs(kk * kk, ones_ref[...])
    kap = kk / jnp.maximum(jnp.sqrt(ss), 1e-12)
    r_ref[...] = r
    lw_ref[...] = -jnp.exp(w)
    k_ref[...] = k * (1.0 + (a - 1.0) * ka_ref[...])
    v_ref[...] = v
    kap_ref[...] = kap
    bet_ref[...] = kap * a
    g_ref[...] = _dot_lo(jax.nn.sigmoid(gd), g2_ref[...])


def _rw_prep(zr, shift, B, T, P):
    n, cols = zr.shape
    G = GROUP_W
    tt = _row_tile(T, PREP_ROWS)
    nt = T // tt
    ones_bd = jnp.kron(jnp.eye(RW_HEADS, dtype=F32), jnp.ones((RW_HD, RW_HD), F32)).astype(BF16)
    row = lambda x: x.reshape(1, -1)
    full = lambda shape: pl.BlockSpec(shape, lambda b, i: (0,) * len(shape))
    tile = pl.BlockSpec((tt, G), lambda b, i: (b * nt + i, 0))
    return pl.pallas_call(
        _rw_prep_body,
        grid=(B, nt),
        in_specs=[
            pl.BlockSpec((tt, cols), lambda b, i: (b * nt + i, 0)),
            pl.BlockSpec((1, 1, cols), lambda b, i: (b, 0, 0)),
            full((1, cols)), full((1, G)), full((RW_DECAY_LORA, G)), full((1, G)), full((RW_A_LORA, G)),
            full((RW_GATE_LORA, G)), full((1, G)), full((1, G)), full((G, G)),
        ],
        out_specs=[tile] * 7,
        out_shape=[jax.ShapeDtypeStruct((n, G), F32)] * 7,
        scratch_shapes=[pltpu.VMEM((8, cols), F32)],
        compiler_params=_params(("parallel", "arbitrary"), 40),
        name="rwkv_prep",
    )(zr, shift.reshape(B, 1, cols), row(P['rw_mu']), row(P['rw_w0']), P['rw_w2'].astype(BF16), row(P['rw_a0']),
      P['rw_a2'].astype(BF16), P['rw_g2'].astype(BF16), row(P['rw_kk']), row(P['rw_ka']), ones_bd)


def _rw_scaled(lw, kap, bet, tri):
    cs = _dot_exact_lhs(tri, lw)
    return cs, kap * jnp.exp(cs - lw), bet * jnp.exp(-cs)


def _tri_masks(c):
    row = lax.broadcasted_iota(jnp.int32, (c, c), 0)
    col = lax.broadcasted_iota(jnp.int32, (c, c), 1)
    return row, col


def _chunk_tri(rows, c):
    row, col = _tri_masks(rows)
    return jnp.where((col <= row) & (row // c == col // c), 1.0, 0.0).astype(BF16)


def _rw_ldiag_body(lw_ref, kap_ref, bet_ref, o_ref, *, c):
    rows = lw_ref.shape[0]
    _, kk_all, bt_all = _rw_scaled(lw_ref[...], kap_ref[...], bet_ref[...], _chunk_tri(rows, c))
    srow, scol = _tri_masks(RW_SUB)
    units = [(cc, h) for cc in range(rows // c) for h in range(RW_HEADS)]
    Ls = [_dot_lo(kk_all[cc * c:(cc + 1) * c, h * RW_HD:(h + 1) * RW_HD],
                  bt_all[cc * c:(cc + 1) * c, h * RW_HD:(h + 1) * RW_HD], _NT) for cc, h in units]
    for (cc, h), L in zip(units, Ls):
        for b in range(c // RW_SUB):
            rs = slice(b * RW_SUB, (b + 1) * RW_SUB)
            o_ref[cc * c + b * RW_SUB:cc * c + (b + 1) * RW_SUB, h * RW_SUB:(h + 1) * RW_SUB] = (
                jnp.where(scol < srow, L[rs, rs], 0.0))


def _rw_inv_body(l_ref, t_ref, a_ref, b_ref):
    n = RW_SUB
    nblk = l_ref.shape[0] // n
    for t in range(n):
        a_ref[t] = l_ref[pl.ds(t, nblk, stride=n), :].T
    entry = lambda ref, t, s: ref.at[t, pl.ds(s, RW_HEADS, stride=n), :]
    one = jnp.ones((RW_HEADS, nblk), F32)
    zero = jnp.zeros((RW_HEADS, nblk), F32)
    for t in range(n):
        for s in range(n):
            if s > t:
                entry(b_ref, t, s)[...] = zero
            elif s == t:
                entry(b_ref, t, s)[...] = one
            else:
                acc = entry(a_ref, t, s)[...]
                for j in range(s + 1, t):
                    acc = acc + entry(a_ref, t, j)[...] * entry(b_ref, j, s)[...]
                entry(b_ref, t, s)[...] = -acc
    for t in range(n):
        t_ref[pl.ds(t, nblk, stride=n), :] = b_ref[t].T


def _rw_main_body(r_ref, lw_ref, k_ref, v_ref, kap_ref, bet_ref, g_ref, td_ref, h0_ref, rk_ref, lng_ref, lnb_ref,
                  o_ref, hout_ref, h_ref, *, c):
    ci = pl.program_id(1)
    rows = r_ref.shape[0]
    nb = c // RW_SUB

    @pl.when(ci == 0)
    def _():
        h_ref[...] = h0_ref[0]

    crow, ccol = _tri_masks(c)
    strict = ccol < crow
    incl = ccol <= crow
    lw = lw_ref[...]
    cs, kk_all, bt_all = _rw_scaled(lw, kap_ref[...], bet_ref[...], _chunk_tri(rows, c))
    gi = jnp.exp(-cs)
    gg = jnp.exp(cs)
    kt_all = k_ref[...] * gi
    rt_all = r_ref[...] * gg
    bonus_all = r_ref[...] * k_ref[...] * rk_ref[...]
    hrow = lax.broadcasted_iota(jnp.int32, (RW_HD, RW_HD), 0)
    hcol = lax.broadcasted_iota(jnp.int32, (RW_HD, RW_HD), 1)
    HS = range(RW_HEADS)
    units = [(cc, h) for cc in range(rows // c) for h in HS]
    US = range(len(units))
    rsl = [slice(cc * c, (cc + 1) * c) for cc, _ in units]
    lsl = [slice(h * RW_HD, (h + 1) * RW_HD) for _, h in units]
    Kk = [kk_all[rsl[u], lsl[u]] for u in US]
    Bt = [bt_all[rsl[u], lsl[u]] for u in US]
    Kt = [kt_all[rsl[u], lsl[u]] for u in US]
    Rt = [rt_all[rsl[u], lsl[u]] for u in US]
    vv = [v_ref[rsl[u], lsl[u]] for u in US]
    Lm = [jnp.where(strict, _dot_lo(Kk[u], Bt[u], _NT), 0.0) for u in US]
    A1 = [jnp.where(strict, _dot_lo(Kk[u], Kt[u], _NT), 0.0) for u in US]
    A4 = [jnp.where(incl, _dot_lo(Rt[u], Bt[u], _NT), 0.0) for u in US]
    A3 = [jnp.where(incl, _dot_lo(Rt[u], Kt[u], _NT), 0.0) for u in US]
    X = [jnp.concatenate([Kk[u], _dot_lo(A1[u], vv[u])], axis=1) for u in US]
    zs = [[] for _ in US]
    for b in range(nb):
        rs = slice(b * RW_SUB, (b + 1) * RW_SUB)
        rhs = [X[u][rs] for u in US]
        if b:
            rhs = [rhs[u] - _dot_lo(Lm[u][rs, 0:b * RW_SUB], jnp.concatenate(zs[u], axis=0)) for u in US]
        for u, (cc, h) in enumerate(units):
            tbb = td_ref[cc * c + b * RW_SUB:cc * c + (b + 1) * RW_SUB, h * RW_SUB:(h + 1) * RW_SUB]
            zs[u].append(_dot_lo(tbb, rhs[u]))
    Z = [jnp.concatenate(zs[u], axis=0) if nb > 1 else zs[u][0] for u in US]
    A4Z = [_dot_lo(A4[u], Z[u]) for u in US]
    Rhat = [Rt[u] - A4Z[u][:, :RW_HD] for u in US]
    Yhat = [_dot_lo(A3[u], vv[u]) - A4Z[u][:, RW_HD:] for u in US]
    gC = [gg[(cc + 1) * c - 1:(cc + 1) * c, lsl[u]] for u, (cc, _) in enumerate(units)]
    MN = [_dot_lo(Bt[u] * gC[u], Z[u], _TN) for u in US]
    Mp = [jnp.where(hrow == hcol, gC[u], 0.0) - MN[u][:, :RW_HD] for u in US]
    Np = [_dot_lo(Kt[u] * gC[u], vv[u], _TN) - MN[u][:, RW_HD:] for u in US]
    H = [h_ref[h] for h in HS]
    ys = [None for _ in US]
    for cc in range(rows // c):
        for h in HS:
            u = cc * RW_HEADS + h
            ys[u] = _dot_lo(Rhat[u], H[h]) + Yhat[u]
        H = [_dot_hi(Mp[cc * RW_HEADS + h], H[h]) + Np[cc * RW_HEADS + h] for h in HS]
    for h in HS:
        h_ref[h] = H[h]
    for u in US:
        y = ys[u]
        mu = jnp.mean(y, axis=-1, keepdims=True)
        var = jnp.mean(jnp.square(y - mu), axis=-1, keepdims=True)
        yn = (y - mu) * lax.rsqrt(var + RW_LN_EPS) * lng_ref[:, lsl[u]] + lnb_ref[:, lsl[u]]
        yn = yn + jnp.sum(bonus_all[rsl[u], lsl[u]], axis=-1, keepdims=True) * vv[u]
        o_ref[rsl[u], lsl[u]] = yn * g_ref[rsl[u], lsl[u]]

    @pl.when(ci == pl.num_programs(1) - 1)
    def _():
        hout_ref[0] = h_ref[...]


def _rwkv7(zr, shift, S0, B, T, P):
    G = GROUP_W
    n = B * T
    r, lw, k, v, kap, bet, g = _rw_prep(zr, shift, B, T, P)
    c = min(RW_CHUNK, T)
    rows_l = _row_tile(T, c * RW_LDIAG_CHUNKS)
    rows_m = _row_tile(T, c * RW_MAIN_CHUNKS)
    nl, nc = T // rows_l, T // rows_m
    tile_l = pl.BlockSpec((rows_l, G), lambda b, i: (b * nl + i, 0))
    tile = pl.BlockSpec((rows_m, G), lambda b, i: (b * nc + i, 0))
    ld = pl.pallas_call(
        functools.partial(_rw_ldiag_body, c=c),
        grid=(B, nl),
        in_specs=[tile_l] * 3,
        out_specs=pl.BlockSpec((rows_l, RW_HEADS * RW_SUB), lambda b, i: (b * nl + i, 0)),
        out_shape=jax.ShapeDtypeStruct((n, RW_HEADS * RW_SUB), F32),
        compiler_params=_params(("parallel", "parallel"), 32),
        name="rwkv_ldiag",
    )(lw, kap, bet)
    rows_i = V7X_LANES * RW_SUB
    npad = -(-n // rows_i) * rows_i
    inv_spec = pl.BlockSpec((rows_i, RW_HEADS * RW_SUB), lambda i: (i, 0))
    inv_scratch = pltpu.VMEM((RW_SUB, RW_HEADS * RW_SUB, V7X_LANES), F32)
    td = pl.pallas_call(
        _rw_inv_body,
        grid=(npad // rows_i,),
        in_specs=[inv_spec],
        out_specs=inv_spec,
        out_shape=jax.ShapeDtypeStruct((npad, RW_HEADS * RW_SUB), F32),
        scratch_shapes=[inv_scratch, inv_scratch],
        compiler_params=_params(("parallel",), 32),
        name="rwkv_inv",
    )(jnp.pad(ld, ((0, npad - n), (0, 0))))[:n]
    h0 = jnp.swapaxes(S0, -1, -2)
    prow = lambda x: pl.BlockSpec((1, G), lambda b, i: (0, 0))
    st_spec = pl.BlockSpec((1, RW_HEADS, RW_HD, RW_HD), lambda b, i: (b, 0, 0, 0))
    out, hl = pl.pallas_call(
        functools.partial(_rw_main_body, c=c),
        grid=(B, nc),
        in_specs=[tile] * 7 + [pl.BlockSpec((rows_m, RW_HEADS * RW_SUB), lambda b, i: (b * nc + i, 0)), st_spec,
                               prow(0), prow(0), prow(0)],
        out_specs=[tile, st_spec],
        out_shape=[jax.ShapeDtypeStruct((n, G), F32), jax.ShapeDtypeStruct(S0.shape, F32)],
        scratch_shapes=[pltpu.VMEM((RW_HEADS, RW_HD, RW_HD), F32)],
        compiler_params=_params(("parallel", "arbitrary"), 32),
        name="rwkv_main",
    )(r, lw, k, v, kap, bet, g, td, h0, P['rw_rk'].reshape(1, G), P['rw_ln_g'].reshape(1, G),
      P['rw_ln_b'].reshape(1, G))
    return out, zr.reshape(B, T, -1)[:, -1], jnp.swapaxes(hl, -1, -2)


def _even_mixer(x2, B, T, g, st, P):
    conv_buf, lru_h, k_past, v_past, lf_past = st
    G = GROUP_W
    z_rg, z_qkv, z_og, z_fl = _norm_matmul(x2, g, P['e_w_in'], (2 * G, 3 * G, G, V7X_LANES))
    rnn_out, conv_new, h_last = _lru(z_rg, conv_buf, lru_h, B, T, P)
    qb, kn, kb, v, vb, lf = _fox_prep(z_qkv, z_fl, B, T, P)
    past = k_past.shape[1]
    lf_all = lf.reshape(B, T, V7X_LANES)
    kb_all, vb_all = kb.reshape(B, T, G), vb.reshape(B, T, G)
    if past:
        lf_all = jnp.concatenate([jnp.pad(lf_past, ((0, 0), (0, 0), (0, V7X_LANES - FOX_HEADS))), lf_all], axis=1)
        kb_all = jnp.concatenate([k_past.reshape(B, past, G).astype(BF16), kb_all], axis=1)
        vb_all = jnp.concatenate([v_past.reshape(B, past, G).astype(BF16), vb_all], axis=1)
    tail = ((0, 0), (0, -(past + T) % FOX_K_ROWS), (0, 0))
    ka, vt = _fox_keys(jnp.pad(lf_all, tail), jnp.pad(kb_all, tail), jnp.pad(vb_all, tail))
    fox_out = _fox_attention(qb, ka, vt, z_og, B, T, past)
    x2 = _out_proj(x2, rnn_out, fox_out, P['e_w_out'])
    heads = lambda t: t.reshape(B, T, FOX_HEADS, FOX_HD)
    return x2, (conv_new, h_last, heads(kn), heads(v), lf.reshape(B, T, V7X_LANES)[..., :FOX_HEADS])


def _odd_mixer(x2, B, T, g, st, lb, P):
    S_hg, shift, S_rw = st
    G = GROUP_W
    z_hg, z_rw = _norm_matmul(x2, g, P['o_w_in'], (4 * G, P['o_w_in'].shape[1] - 4 * G))
    hg_out, S_hg_new = _hgrn2(z_hg, lb, S_hg, B, T, P)
    rw_out, shift_new, S_rw_new = _rwkv7(z_rw, shift, S_rw, B, T, P)
    x2 = _out_proj(x2, hg_out, rw_out, P['o_w_out'])
    return x2, (S_hg_new, shift_new, S_rw_new)


def _trunk(x, states, W):
    lru_conv, lru_h, fox_k, fox_v, fox_lf, hg_S, rw_shift, rw_S = states
    B, T, D = x.shape
    depth = W['norm_g'].shape[0]
    sm = jax.nn.softmax(W['hg_lb_logits'], axis=0)
    lower_bounds = jnp.cumsum(sm, axis=0) - sm[0]
    x2 = x.reshape(B * T, D)
    even_new, odd_new = [], []
    for layer in range(depth):
        g = W['norm_g'][layer]
        x2 = _ffn(x2, g[0], W['ffn_w_in'][layer][0], W['ffn_w_out'][layer][0])
        if layer % 2 == 0:
            e = layer // 2
            P = {n: W[n][e] for n in ('e_w_in', 'e_w_out', 'lru_conv_w', 'lru_conv_b', 'lru_wa', 'lru_ba', 'lru_wx',
                                      'lru_bx', 'lru_lambda', 'fox_q_gain', 'fox_k_gain', 'fox_f_bias')}
            x2, new = _even_mixer(x2, B, T, g[1], (lru_conv[e], lru_h[e], fox_k[e], fox_v[e], fox_lf[e]), P)
            even_new.append(new)
        else:
            o = layer // 2
            P = {n: W[n][o] for n in ('o_w_in', 'o_w_out', 'hg_norm_g', 'rw_mu', 'rw_w0', 'rw_w2', 'rw_a0', 'rw_a2',
                                      'rw_g2', 'rw_kk', 'rw_ka', 'rw_rk', 'rw_ln_g', 'rw_ln_b')}
            x2, new = _odd_mixer(x2, B, T, g[1], (hg_S[o], rw_shift[o], rw_S[o]), lower_bounds[layer], P)
            odd_new.append(new)
        x2 = _ffn(x2, g[2], W['ffn_w_in'][layer][1], W['ffn_w_out'][layer][1])
    ev = [jnp.stack([n[j] for n in even_new]) for j in range(5)]
    od = [jnp.stack([n[j] for n in odd_new]) for j in range(3)]
    return x2.reshape(B, T, D), (ev[0], ev[1], ev[2], ev[3], ev[4], od[0], od[1], od[2])


def kernel(x_prompt, x_sample, state_lru_conv, state_lru_h, cache_fox_k, cache_fox_v, cache_fox_logf,
           state_hgrn_S, state_rwkv_shift, state_rwkv_S, norm_g, ffn_w_in, ffn_w_out, e_w_in, e_w_out,
           lru_conv_w, lru_conv_b, lru_wa, lru_ba, lru_wx, lru_bx, lru_lambda, fox_q_gain, fox_k_gain,
           fox_f_bias, o_w_in, o_w_out, hg_lb_logits, hg_norm_g, rw_mu, rw_w0, rw_w2, rw_a0, rw_a2, rw_g2,
           rw_kk, rw_ka, rw_rk, rw_ln_g, rw_ln_b):
    n_even, n_odd = e_w_in.shape[0], o_w_in.shape[0]
    W = dict(norm_g=norm_g, ffn_w_in=_ffn_w_in_tiles(ffn_w_in), ffn_w_out=ffn_w_out.astype(BF16),
             e_w_in=_pad_cols(e_w_in.astype(BF16)), e_w_out=e_w_out.astype(BF16),
             lru_conv_w=lru_conv_w, lru_conv_b=lru_conv_b, lru_wa=lru_wa, lru_ba=lru_ba, lru_wx=lru_wx,
             lru_bx=lru_bx, lru_lambda=lru_lambda, fox_q_gain=fox_q_gain, fox_k_gain=fox_k_gain,
             fox_f_bias=fox_f_bias, o_w_in=o_w_in.astype(BF16), o_w_out=o_w_out.astype(BF16),
             hg_lb_logits=hg_lb_logits, hg_norm_g=hg_norm_g, rw_mu=rw_mu, rw_w0=rw_w0, rw_w2=rw_w2, rw_a0=rw_a0,
             rw_a2=rw_a2, rw_g2=rw_g2, rw_kk=rw_kk, rw_ka=rw_ka, rw_rk=rw_rk, rw_ln_g=rw_ln_g, rw_ln_b=rw_ln_b)
    nb = x_prompt.shape[0]
    dt = x_prompt.dtype
    prompt_states = (jnp.zeros((n_even, nb, CONV_W - 1, GROUP_W), dt),
                     jnp.zeros((n_even, nb, GROUP_W), dt),
                     jnp.zeros((n_even, nb, 0, FOX_HEADS, FOX_HD), dt),
                     jnp.zeros((n_even, nb, 0, FOX_HEADS, FOX_HD), dt),
                     jnp.zeros((n_even, nb, 0, FOX_HEADS), dt),
                     jnp.zeros((n_odd, nb, HG_HEADS, GROUP_W // HG_HEADS, GROUP_W // HG_HEADS), dt),
                     jnp.zeros((n_odd, nb, rw_mu.shape[1]), dt),
                     jnp.zeros((n_odd, nb, RW_HEADS, RW_HD, RW_HD), dt))
    sample_states = (state_lru_conv, state_lru_h, cache_fox_k, cache_fox_v, cache_fox_logf,
                     state_hgrn_S, state_rwkv_shift, state_rwkv_S)
    y_prompt, p_new = _trunk(x_prompt, prompt_states, W)
    y_sample, s_new = _trunk(x_sample, sample_states, W)
    lru_conv_p, lru_h_p, fox_k_p, fox_v_p, fox_logf_p, hgrn_S_p, rwkv_shift_p, rwkv_S_p = p_new
    lru_conv_s, lru_h_s, fox_k_s, fox_v_s, fox_logf_s, hgrn_S_s, rwkv_shift_s, rwkv_S_s = s_new
    return (y_prompt, y_sample, lru_conv_p, lru_conv_s, lru_h_p, lru_h_s, fox_k_p, fox_k_s, fox_v_p, fox_v_s,
            fox_logf_p, fox_logf_s, hgrn_S_p, hgrn_S_s, rwkv_shift_p, rwkv_shift_s, rwkv_S_p, rwkv_S_s)
```

```python
import functools

import jax
import jax.numpy as jnp
from jax import lax
from jax.experimental import pallas as pl
from jax.experimental.pallas import tpu as pltpu

F32 = jnp.float32
BF16 = jnp.bfloat16

NORM_EPS = 1e-6
GROUP_W = 512
CONV_W = 4
LRU_C = 8.0
FOX_HEADS = 8
FOX_HD = 64
HG_HEADS = 4
RW_HEADS = 8
RW_HD = 64
RW_DECAY_LORA = 64
RW_A_LORA = 64
RW_GATE_LORA = 128
RW_LN_EPS = 64e-5

V7X_LANES = 128
FFN_COL_TILE = 1408
PREP_ROWS = 512
FFN_ROW_TILE = 1024
FFN_VMEM_MIB = 60


def _row_tile(n, want):
    t = min(n, want)
    while n % t:
        t //= 2
    return t


def _params(sem, vmem_mib):
    return pltpu.CompilerParams(dimension_semantics=sem, vmem_limit_bytes=vmem_mib << 20)


def _pad_cols(w):
    pad = -w.shape[-1] % V7X_LANES
    return jnp.pad(w, [(0, 0)] * (w.ndim - 1) + [(0, pad)])


def _rms(x, g):
    return x * lax.rsqrt(jnp.mean(x * x, axis=-1, keepdims=True) + NORM_EPS) * g


def _ffn_body(*refs, mixed):
    if mixed:
        x_ref, a_ref, b_ref, wa_ref, wb_ref, g_ref, wi_ref, wo_ref, o_ref, h_ref, acc_ref = refs
    else:
        x_ref, g_ref, wi_ref, wo_ref, o_ref, h_ref, acc_ref = refs
    j = pl.program_id(1)

    @pl.when(j == 0)
    def _():
        x = x_ref[...]
        if mixed:
            x = x + jnp.dot(a_ref[...], wa_ref[...], preferred_element_type=F32)
            x = x + jnp.dot(b_ref[...], wb_ref[...], preferred_element_type=F32)
        h_ref[...] = _rms(x, g_ref[...]).astype(BF16)
        acc_ref[...] = 2.0 * x

    tf = wo_ref.shape[0]
    gu = jnp.dot(h_ref[...], wi_ref[...], preferred_element_type=F32)
    gate, up = gu[:, :tf], gu[:, tf:]
    act = (gate * jax.nn.sigmoid(gate) * up).astype(BF16)
    acc_ref[...] += jnp.dot(act, wo_ref[...], preferred_element_type=F32)

    @pl.when(j == pl.num_programs(1) - 1)
    def _():
        o_ref[...] = 0.5 * acc_ref[...]


def _cast_body(x_ref, o_ref):
    o_ref[...] = x_ref[...].astype(o_ref.dtype)


def _ffn_w_in_tiles(w_in):
    *lead, d, f2 = w_in.shape
    tf = FFN_COL_TILE
    nf = f2 // 2 // tf
    w = w_in.reshape(-1, d, f2)
    out = pl.pallas_call(
        _cast_body,
        grid=(w.shape[0], nf, 2),
        in_specs=[pl.BlockSpec((1, d, tf), lambda i, j, gu: (i, 0, gu * nf + j))],
        out_specs=pl.BlockSpec((1, d, tf), lambda i, j, gu: (i, 0, 2 * j + gu)),
        out_shape=jax.ShapeDtypeStruct(w.shape, BF16),
        compiler_params=_params(("parallel", "parallel", "parallel"), 32),
        name="ffn_weight_tiles",
    )(w)
    return out.reshape(*lead, d, f2)


def _ffn(x, g, w_in, w_out, mix=None):
    n, d = x.shape
    f = w_out.shape[0]
    tm = _row_tile(n, FFN_ROW_TILE)
    tf = FFN_COL_TILE
    nf = f // tf
    row_spec = lambda w: pl.BlockSpec((tm, w), lambda i, j: (i, 0))
    args, specs = [x], [row_spec(d)]
    if mix is not None:
        a, b, w = mix
        ga, gb = a.shape[1], b.shape[1]
        args += [a, b, w[:ga], w[ga:]]
        specs += [row_spec(ga), row_spec(gb), pl.BlockSpec((ga, d), lambda i, j: (0, 0)),
                  pl.BlockSpec((gb, d), lambda i, j: (0, 0))]
    args += [g.reshape(1, d), w_in, w_out]
    specs += [pl.BlockSpec((1, d), lambda i, j: (0, 0)), pl.BlockSpec((d, 2 * tf), lambda i, j: (0, j)),
              pl.BlockSpec((tf, d), lambda i, j: (j, 0))]
    return pl.pallas_call(
        functools.partial(_ffn_body, mixed=mix is not None),
        grid=(n // tm, nf),
        in_specs=specs,
        out_specs=row_spec(d),
        out_shape=jax.ShapeDtypeStruct((n, d), F32),
        scratch_shapes=[pltpu.VMEM((tm, d), BF16), pltpu.VMEM((tm, d), F32)],
        compiler_params=_params(("parallel", "arbitrary"), FFN_VMEM_MIB),
        name="ffn",
    )(*args)


def _norm_matmul_body(x_ref, g_ref, w_ref, *o_refs):
    h = _rms(x_ref[...], g_ref[...]).astype(BF16)
    z = jnp.dot(h, w_ref[...], preferred_element_type=F32)
    start = 0
    for o_ref in o_refs:
        width = o_ref.shape[1]
        o_ref[...] = z[:, start:start + width]
        start += width


def _norm_matmul(x, g, w, widths):
    n, d = x.shape
    c = w.shape[1]
    assert sum(widths) == c and all(wd % V7X_LANES == 0 for wd in widths)
    tm = _row_tile(n, 512)
    return pl.pallas_call(
        _norm_matmul_body,
        grid=(n // tm,),
        in_specs=[
            pl.BlockSpec((tm, d), lambda i: (i, 0)),
            pl.BlockSpec((1, d), lambda i: (0, 0)),
            pl.BlockSpec((d, c), lambda i: (0, 0)),
        ],
        out_specs=[pl.BlockSpec((tm, wd), lambda i: (i, 0)) for wd in widths],
        out_shape=[jax.ShapeDtypeStruct((n, wd), F32) for wd in widths],
        compiler_params=_params(("parallel",), 48),
        name="norm_matmul",
    )(x, g.reshape(1, d), w)


LRU_ROWS = 256
CONV_PAD = 8


def _expm1(x):
    series = x * (1.0 + x * (1 / 2 + x * (1 / 6 + x * (1 / 24 + x * (1 / 120 + x * (1 / 720 + x * (1 / 5040 + x * (1 / 40320))))))))
    return jnp.where(jnp.abs(x) < 0.25, series, jnp.exp(x) - 1.0)


def _shift_rows(x, s, fill):
    row = lax.broadcasted_iota(jnp.int32, x.shape, 0)
    return jnp.where(row >= s, pltpu.roll(x, s, axis=0), fill)


def _lru_body(z_ref, buf_ref, h0_ref, cw_ref, cb_ref, wa_ref, ba_ref, wx_ref, bx_ref, lam_ref,
              o_ref, bufo_ref, ho_ref, x_ref, hc_ref):
    G = GROUP_W
    tt = z_ref.shape[0]

    @pl.when(pl.program_id(1) == 0)
    def _():
        x_ref[0:CONV_PAD, :] = buf_ref[0]
        hc_ref[...] = jnp.broadcast_to(h0_ref[0], hc_ref.shape)

    x_ref[CONV_PAD:CONV_PAD + tt, :] = z_ref[:, 0:G]
    xc = cb_ref[...]
    for j in range(CONV_W):
        lo = CONV_PAD - (CONV_W - 1) + j
        xc = xc + x_ref[lo:lo + tt, :] * cw_ref[j:j + 1, :]
    hist = x_ref[tt:tt + CONV_PAD, :]
    x_ref[0:CONV_PAD, :] = hist
    bufo_ref[0] = hist

    xb = xc.astype(BF16)
    r = jax.nn.sigmoid(jnp.dot(xb, wa_ref[...], preferred_element_type=F32) + ba_ref[...])
    ig = jax.nn.sigmoid(jnp.dot(xb, wx_ref[...], preferred_element_type=F32) + bx_ref[...])
    log_a = (-LRU_C * _softplus(-lam_ref[...])) * r
    a = jnp.exp(log_a)
    b = jnp.sqrt(-_expm1(2.0 * log_a)) * (ig * xc)
    s = 1
    while s < tt:
        if s % 8:
            b = a * _shift_rows(b, s, 0.0) + b
            a = a * _shift_rows(a, s, 1.0)
        else:
            b = jnp.concatenate([b[:s], a[s:] * b[:tt - s] + b[s:]], axis=0)
            a = jnp.concatenate([a[:s], a[s:] * a[:tt - s]], axis=0)
        s *= 2
    h = a * hc_ref[0:1, :] + b
    hc_ref[...] = jnp.broadcast_to(h[tt - 1:tt, :], hc_ref.shape)
    ho_ref[0] = h[tt - 1:tt, :]
    o_ref[...] = (jax.nn.gelu(z_ref[:, G:2 * G]) * h).astype(o_ref.dtype)


def _block_diag_dense(w):
    nb, bs, _ = w.shape
    eye = jnp.eye(nb, dtype=w.dtype)
    return (eye[:, None, :, None] * w[:, :, None, :]).reshape(nb * bs, nb * bs)


def _lru(z_rg, conv_buf, h0, B, T, P):
    G = GROUP_W
    n = B * T
    tt = _row_tile(T, LRU_ROWS)
    nt = T // tt
    buf = jnp.pad(conv_buf, ((0, 0), (CONV_PAD - (CONV_W - 1), 0), (0, 0)))
    cw = jnp.pad(P['lru_conv_w'], ((0, CONV_PAD - CONV_W), (0, 0)))
    row = lambda x: x.reshape(1, G)
    full = lambda shape: pl.BlockSpec(shape, lambda b, i: (0,) * len(shape))
    out, bufo, ho = pl.pallas_call(
        _lru_body,
        grid=(B, nt),
        in_specs=[
            pl.BlockSpec((tt, 2 * G), lambda b, i: (b * nt + i, 0)),
            pl.BlockSpec((1, CONV_PAD, G), lambda b, i: (b, 0, 0)),
            pl.BlockSpec((1, 1, G), lambda b, i: (b, 0, 0)),
            full((CONV_PAD, G)), full((1, G)), full((G, G)), full((1, G)), full((G, G)), full((1, G)), full((1, G)),
        ],
        out_specs=[
            pl.BlockSpec((tt, G), lambda b, i: (b * nt + i, 0)),
            pl.BlockSpec((1, CONV_PAD, G), lambda b, i: (b, 0, 0)),
            pl.BlockSpec((1, 1, G), lambda b, i: (b, 0, 0)),
        ],
        out_shape=[jax.ShapeDtypeStruct((n, G), BF16), jax.ShapeDtypeStruct((B, CONV_PAD, G), F32),
                   jax.ShapeDtypeStruct((B, 1, G), F32)],
        scratch_shapes=[pltpu.VMEM((tt + CONV_PAD, G), F32), pltpu.VMEM((8, G), F32)],
        compiler_params=_params(("parallel", "arbitrary"), 32),
        name="lru",
    )(z_rg, buf, h0.reshape(B, 1, G), cw, row(P['lru_conv_b']), _block_diag_dense(P['lru_wa']).astype(BF16),
      row(P['lru_ba']), _block_diag_dense(P['lru_wx']).astype(BF16), row(P['lru_bx']), row(P['lru_lambda']))
    return out, bufo[:, CONV_PAD - (CONV_W - 1):], ho.reshape(B, G)


FOX_Q_COLS = 2048
FOX_K_ROWS = 512
FOX_F_SPLIT = 3
FOX_NEG = -1e30
LOG2E = 1.4426950408889634
HEAD_PAIRS = FOX_HEADS // 2
PAIR_W = 2 * FOX_HD
FOX_KEY_TILE = 512


def _fox_prep_body(z_ref, fl_ref, qg_ref, kg_ref, fb_ref, ones_ref, q_ref, k_ref, kb_ref, v_ref, vb_ref, lf_ref,
                   *, q_transposed):
    G = GROUP_W
    q, k, v = z_ref[:, 0:G], z_ref[:, G:2 * G], z_ref[:, 2 * G:3 * G]
    inv = 1.0 / FOX_HD
    qn = q * lax.rsqrt(_dot_exact_rhs(q * q, ones_ref[...]) * inv + NORM_EPS) * qg_ref[...]
    kn = k * lax.rsqrt(_dot_exact_rhs(k * k, ones_ref[...]) * inv + NORM_EPS) * kg_ref[...]
    qs = qn * (LOG2E * FOX_HD ** -0.5)
    if q_transposed:
        for p in range(HEAD_PAIRS):
            q_ref[0, p] = qs[:, p * PAIR_W:(p + 1) * PAIR_W].T.astype(BF16)
    else:
        q_ref[...] = qs.astype(BF16)
    k_ref[...] = kn
    kb_ref[...] = kn.astype(BF16)
    v_ref[...] = v
    vb_ref[...] = v.astype(BF16)
    x = fl_ref[...] + fb_ref[...]
    lf_ref[...] = -_softplus(-x)


def _fox_prep(z_qkv, z_fl, B, T, P):
    n = z_qkv.shape[0]
    G = GROUP_W
    tt = _row_tile(T, PREP_ROWS)
    nt = T // tt
    q_transposed = tt % V7X_LANES == 0
    ones_bd = jnp.kron(jnp.eye(FOX_HEADS, dtype=F32), jnp.ones((FOX_HD, FOX_HD), F32)).astype(BF16)
    fb = jnp.pad(P['fox_f_bias'], (0, V7X_LANES - FOX_HEADS)).reshape(1, V7X_LANES)
    tile = lambda w: pl.BlockSpec((tt, w), lambda b, i: (b * nt + i, 0))
    full = lambda shape: pl.BlockSpec(shape, lambda b, i: (0,) * len(shape))
    if q_transposed:
        q_spec = pl.BlockSpec((1, HEAD_PAIRS, PAIR_W, tt), lambda b, i: (b, 0, 0, i))
        q_shape = jax.ShapeDtypeStruct((B, HEAD_PAIRS, PAIR_W, T), BF16)
    else:
        q_spec, q_shape = tile(G), jax.ShapeDtypeStruct((n, G), BF16)
    return pl.pallas_call(
        functools.partial(_fox_prep_body, q_transposed=q_transposed),
        grid=(B, nt),
        in_specs=[tile(3 * G), tile(V7X_LANES), full((1, G)), full((1, G)), full((1, V7X_LANES)), full((G, G))],
        out_specs=[q_spec] + [tile(G)] * 4 + [tile(V7X_LANES)],
        out_shape=[q_shape, jax.ShapeDtypeStruct((n, G), F32),
                   jax.ShapeDtypeStruct((n, G), BF16), jax.ShapeDtypeStruct((n, G), F32),
                   jax.ShapeDtypeStruct((n, G), BF16), jax.ShapeDtypeStruct((n, V7X_LANES), F32)],
        compiler_params=_params(("parallel", "parallel"), 32),
        name="fox_prep",
    )(z_qkv, z_fl, jnp.tile(P['fox_q_gain'], FOX_HEADS).reshape(1, G),
      jnp.tile(P['fox_k_gain'], FOX_HEADS).reshape(1, G), fb, ones_bd)


def _fox_keys_body(lf_ref, kb_ref, vb_ref, ka_ref, vt_ref, c_ref):
    tt = lf_ref.shape[1]

    @pl.when(pl.program_id(1) == 0)
    def _():
        c_ref[...] = jnp.zeros_like(c_ref)

    row, col = _tri_masks(tt)
    tri = jnp.where(col <= row, 1.0, 0.0).astype(BF16)
    f = _dot_exact_lhs(tri, lf_ref[0]) + c_ref[0:1, :]
    c_ref[...] = jnp.broadcast_to(f[tt - 1:tt, :], c_ref.shape)
    parts = _split3(f * LOG2E)
    srow = lax.broadcasted_iota(jnp.int32, (V7X_LANES, FOX_HD), 0)
    scol = lax.broadcasted_iota(jnp.int32, (V7X_LANES, FOX_HD), 1)
    for h in range(FOX_HEADS):
        aug = jnp.zeros((tt, FOX_HD), F32)
        for t, part in enumerate(parts):
            sel = jnp.where((srow == h) & (scol == t), 1.0, 0.0).astype(BF16)
            aug = aug + jnp.dot(part, sel, preferred_element_type=F32)
        ka_ref[0, h] = jnp.concatenate([kb_ref[0, :, h * FOX_HD:(h + 1) * FOX_HD], aug.astype(BF16)], axis=1)
    for p in range(HEAD_PAIRS):
        vt_ref[0, p] = vb_ref[0, :, p * PAIR_W:(p + 1) * PAIR_W].astype(F32).T.astype(BF16)


def _fox_keys(lf_all, kb_all, vb_all):
    B, tk_all, L = lf_all.shape
    G = GROUP_W
    tt = FOX_KEY_TILE
    return pl.pallas_call(
        _fox_keys_body,
        grid=(B, tk_all // tt),
        in_specs=[pl.BlockSpec((1, tt, L), lambda b, i: (b, i, 0)),
                  pl.BlockSpec((1, tt, G), lambda b, i: (b, i, 0)),
                  pl.BlockSpec((1, tt, G), lambda b, i: (b, i, 0))],
        out_specs=[pl.BlockSpec((1, FOX_HEADS, tt, 2 * FOX_HD), lambda b, i: (b, 0, i, 0)),
                   pl.BlockSpec((1, HEAD_PAIRS, PAIR_W, tt), lambda b, i: (b, 0, 0, i))],
        out_shape=[jax.ShapeDtypeStruct((B, FOX_HEADS, tk_all, 2 * FOX_HD), BF16),
                   jax.ShapeDtypeStruct((B, HEAD_PAIRS, PAIR_W, tk_all), BF16)],
        scratch_shapes=[pltpu.VMEM((8, L), F32)],
        compiler_params=_params(("parallel", "arbitrary"), 32),
        name="fox_keys",
    )(lf_all, kb_all, vb_all)


def _fox_attn_body(qt_ref, ka_ref, vt_ref, og_ref, o_ref, acc_ref, m_ref, l_ref, *, past, tk, t_real):
    qi = pl.program_id(2)
    tq = qt_ref.shape[3]
    t_out = o_ref.shape[0]
    first_q = past + qi * tq
    last_q = past + jnp.minimum(qi * tq + tq, t_real) - 1
    n_full = (first_q + 1) // tk
    n_all = last_q // tk + 1
    drow = lax.broadcasted_iota(jnp.int32, (FOX_HD, tq), 0)
    minus = jnp.where(drow < FOX_F_SPLIT, -1.0, 0.0).astype(BF16)
    rhs = [jnp.concatenate([qt_ref[0, 0, h * FOX_HD:(h + 1) * FOX_HD, :], minus], axis=0) for h in range(2)]
    acc_ref[...] = jnp.zeros_like(acc_ref)

    def update(ki, m_prev, l_prev, masked, q0=0):
        ks = pl.multiple_of(ki * tk, tk)
        s = [jnp.dot(ka_ref[0, h, pl.ds(ks, tk), :], rhs[h][:, q0:], preferred_element_type=F32) for h in range(2)]
        if masked:
            krow = lax.broadcasted_iota(jnp.int32, (tk, tq - q0), 0)
            qcol = lax.broadcasted_iota(jnp.int32, (tk, tq - q0), 1)
            vis = ks + krow <= first_q + q0 + qcol
            s = [jnp.where(vis, s[h], FOX_NEG) for h in range(2)]
        m_new = [jnp.maximum(m_prev[h], jnp.max(s[h], axis=0, keepdims=True)) for h in range(2)]
        alpha = [jnp.exp2(m_prev[h] - m_new[h]) for h in range(2)]
        p = [jnp.exp2(s[h] - m_new[h]) for h in range(2)]
        l_new = [alpha[h] * l_prev[h] + jnp.sum(p[h], axis=0, keepdims=True) for h in range(2)]
        vt = [vt_ref[0, 0, h * FOX_HD:(h + 1) * FOX_HD, pl.ds(ks, tk)] for h in range(2)]
        pv = [jnp.dot(vt[h], p[h].astype(BF16), preferred_element_type=F32) for h in range(2)]
        for h in range(2):
            acc_ref[h, :, q0:] = alpha[h] * acc_ref[h, :, q0:] + pv[h]
        return m_new, l_new

    def carried(ki, c, masked):
        return update(ki, c[0], c[1], masked)

    c = ([jnp.full((1, tq), FOX_NEG, F32)] * 2, [jnp.zeros((1, tq), F32)] * 2)
    c = lax.fori_loop(0, n_full, lambda ki, c: carried(ki, c, False), c)
    if tq > tk and tq % tk == 0 and past % tq == 0 and t_real % tq == 0:
        m, l = carried(n_full, c, True)
        for h in range(2):
            m_ref[h], l_ref[h] = m[h], l[h]
        for j in range(1, tq // tk):
            q0 = j * tk
            m, l = update(n_full + j, [m_ref[h, :, q0:] for h in range(2)], [l_ref[h, :, q0:] for h in range(2)],
                          True, q0=q0)
            for h in range(2):
                m_ref[h, :, q0:], l_ref[h, :, q0:] = m[h], l[h]
    else:
        _, l = lax.fori_loop(n_full, n_all, lambda ki, c: carried(ki, c, True), c)
        for h in range(2):
            l_ref[h] = l[h]
    o_t = jnp.concatenate([acc_ref[h] / l_ref[h] for h in range(2)], axis=0)
    o_ref[...] = (o_t.T[:t_out] * jax.nn.sigmoid(og_ref[...])).astype(o_ref.dtype)


def _fox_attention(q, ka, vt, z_og, B, T, past):
    G = GROUP_W
    pw = PAIR_W
    tq = max(_row_tile(T, FOX_Q_COLS), V7X_LANES)
    tqp = -(-T // tq) * tq
    nq = tqp // tq
    t_out = min(tq, T)
    tk = FOX_K_ROWS
    tkp = ka.shape[2]
    if q.ndim == 2:
        q = q.reshape(B, T, HEAD_PAIRS, pw).transpose(0, 2, 3, 1)
    qt = jnp.pad(q, ((0, 0), (0, 0), (0, 0), (0, tqp - T)))
    return pl.pallas_call(
        functools.partial(_fox_attn_body, past=past, tk=tk, t_real=T),
        grid=(B, HEAD_PAIRS, nq),
        in_specs=[
            pl.BlockSpec((1, 1, pw, tq), lambda b, p, i: (b, p, 0, i)),
            pl.BlockSpec((1, 2, tkp, 2 * FOX_HD), lambda b, p, i: (b, p, 0, 0)),
            pl.BlockSpec((1, 1, pw, tkp), lambda b, p, i: (b, p, 0, 0)),
            pl.BlockSpec((t_out, pw), lambda b, p, i: (b * nq + i, p)),
        ],
        out_specs=pl.BlockSpec((t_out, pw), lambda b, p, i: (b * nq + i, p)),
        out_shape=jax.ShapeDtypeStruct((B * T, G), BF16),
        scratch_shapes=[pltpu.VMEM((2, FOX_HD, tq), F32), pltpu.VMEM((2, 1, tq), F32), pltpu.VMEM((2, 1, tq), F32)],
        compiler_params=_params(("parallel", "parallel", "arbitrary"), 40),
        name="fox_attn",
    )(qt, ka, vt, z_og)


HG_CHUNK = 64
HG_STEP_CHUNKS = 8


def _hgrn_body(z_ref, lb_ref, s0_ref, ng_ref, o_ref, so_ref, st_ref, *, c):
    G = GROUP_W
    rows = z_ref.shape[0]
    nch = rows // c
    dk = G // HG_HEADS

    @pl.when(pl.program_id(1) == 0)
    def _():
        st_ref[...] = s0_ref[0]

    lb = lb_ref[...]
    f = lb + (1.0 - lb) * jax.nn.sigmoid(z_ref[:, G:2 * G])
    kx = 1.0 - f
    crow, ccol = _tri_masks(c)
    incl = ccol <= crow
    gs = _dot_exact_lhs(_chunk_tri(rows, c), jnp.log(f))
    qg_all = z_ref[:, 0:G] * jnp.exp(gs)
    kg_all = kx * jnp.exp(-gs)
    HS = range(HG_HEADS)
    units = [(cc, h) for cc in range(nch) for h in HS]
    US = range(len(units))
    rsl = [slice(cc * c, (cc + 1) * c) for cc, _ in units]
    lsl = [slice(h * dk, (h + 1) * dk) for _, h in units]
    g_last = [gs[(cc + 1) * c - 1:(cc + 1) * c, lsl[u]] for u, (cc, _) in enumerate(units)]
    vv = [z_ref[rsl[u], 2 * G + h * dk:2 * G + (h + 1) * dk] for u, (_, h) in enumerate(units)]
    A = [jnp.where(incl, _dot_lo(qg_all[rsl[u], lsl[u]], kg_all[rsl[u], lsl[u]], _NT), 0.0) for u in US]
    av = [_dot_lo(A[u], vv[u]) for u in US]
    kd = [kx[rsl[u], lsl[u]] * jnp.exp(g_last[u] - gs[rsl[u], lsl[u]]) for u in US]
    upd = [_dot_lo(vv[u], kd[u], _TN) for u in US]
    st = [st_ref[h] for h in HS]
    o = [None for _ in US]
    for cc in range(nch):
        for h in HS:
            u = cc * HG_HEADS + h
            o[u] = _dot_lo(qg_all[rsl[u], lsl[u]], st[h], _NT) + av[u]
        st = [st[h] * jnp.exp(g_last[cc * HG_HEADS + h]) + upd[cc * HG_HEADS + h] for h in HS]
    for h in HS:
        st_ref[h] = st[h]
    for u, (_, h) in enumerate(units):
        hg = z_ref[rsl[u], 3 * G + h * dk:3 * G + (h + 1) * dk]
        o_ref[rsl[u], lsl[u]] = (_rms(o[u], ng_ref[:, lsl[u]]) * (hg * jax.nn.sigmoid(hg))).astype(o_ref.dtype)

    @pl.when(pl.program_id(1) == pl.num_programs(1) - 1)
    def _():
        so_ref[0] = st_ref[...]


def _hgrn2(z_hg, lb, S0, B, T, P):
    G = GROUP_W
    c = min(HG_CHUNK, T)
    rows = _row_tile(T, c * HG_STEP_CHUNKS)
    nc = T // rows
    dk = G // HG_HEADS
    st_spec = pl.BlockSpec((1, HG_HEADS, dk, dk), lambda b, i: (b, 0, 0, 0))
    out, so = pl.pallas_call(
        functools.partial(_hgrn_body, c=c),
        grid=(B, nc),
        in_specs=[pl.BlockSpec((rows, 4 * G), lambda b, i: (b * nc + i, 0)),
                  pl.BlockSpec((1, G), lambda b, i: (0, 0)), st_spec, pl.BlockSpec((1, G), lambda b, i: (0, 0))],
        out_specs=[pl.BlockSpec((rows, G), lambda b, i: (b * nc + i, 0)), st_spec],
        out_shape=[jax.ShapeDtypeStruct((B * T, G), BF16), jax.ShapeDtypeStruct(S0.shape, F32)],
        scratch_shapes=[pltpu.VMEM((HG_HEADS, dk, dk), F32)],
        compiler_params=_params(("parallel", "arbitrary"), 32),
        name="hgrn2",
    )(z_hg, lb.reshape(1, G), jnp.swapaxes(S0, -1, -2), P['hg_norm_g'].reshape(1, G))
    return out, jnp.swapaxes(so, -1, -2)


RW_CHUNK = 64
RW_SUB = 16
RW_LDIAG_CHUNKS = 4
RW_MAIN_CHUNKS = 2

_NT = (((1,), (1,)), ((), ()))
_TN = (((0,), (0,)), ((), ()))
_NN = (((1,), (0,)), ((), ()))


def _split3(x):
    h1 = x.astype(BF16)
    r1 = x - h1.astype(F32)
    h2 = r1.astype(BF16)
    h3 = (r1 - h2.astype(F32)).astype(BF16)
    return h1, h2, h3


def _dot_lo(a, b, dims=_NN):
    return lax.dot_general(a.astype(BF16), b.astype(BF16), dims, preferred_element_type=F32)


def _dot_hi(a, b, dims=_NN):
    ah = a.astype(BF16)
    al = (a - ah.astype(F32)).astype(BF16)
    bh = b.astype(BF16)
    bl = (b - bh.astype(F32)).astype(BF16)
    d = functools.partial(lax.dot_general, dimension_numbers=dims, preferred_element_type=F32)
    return d(ah, bh) + (d(al, bh) + d(ah, bl))


def _dot_exact_rhs(a, b):
    h1, h2, h3 = _split3(a)
    d = functools.partial(jnp.dot, preferred_element_type=F32)
    return d(h1, b) + (d(h2, b) + d(h3, b))


def _dot_exact_lhs(a, b):
    h1, h2, h3 = _split3(b)
    d = functools.partial(jnp.dot, preferred_element_type=F32)
    return d(a, h1) + (d(a, h2) + d(a, h3))


def _softplus(x):
    return jnp.maximum(x, 0.0) + jnp.log1p(jnp.exp(-jnp.abs(x)))


def _rw_prep_body(z_ref, shift_ref, mu_ref, w0_ref, w2_ref, a0_ref, a2_ref, g2_ref, kk_ref, ka_ref, ones_ref,
                  r_ref, lw_ref, k_ref, v_ref, kap_ref, bet_ref, g_ref, prev_ref):
    G = GROUP_W

    @pl.when(pl.program_id(1) == 0)
    def _():
        prev_ref[0:1, :] = shift_ref[0]

    z = z_ref[...]
    tt = z.shape[0]
    row = lax.broadcasted_iota(jnp.int32, z.shape, 0)
    shifted = jnp.where(row == 0, prev_ref[0:1, :], pltpu.roll(z, 1, axis=0))
    prev_ref[0:1, :] = z[tt - 1:tt, :]
    zm = z + (shifted - z) * mu_ref[...]
    r, k, v = zm[:, 0:G], zm[:, G:2 * G], zm[:, 2 * G:3 * G]
    o = 3 * G
    wd = zm[:, o:o + RW_DECAY_LORA]
    ad = zm[:, o + RW_DECAY_LORA:o + RW_DECAY_LORA + RW_A_LORA]
    gd = zm[:, o + RW_DECAY_LORA + RW_A_LORA:]
    w = -_softplus(-(w0_ref[...] + _dot_lo(jnp.tanh(wd), w2_ref[...]))) - 0.5
    a = jax.nn.sigmoid(a0_ref[...] + _dot_lo(ad, a2_ref[...]))
    kk = k * kk_ref[...]
    ss = _dot_exact_rhs(kk * kk, ones_ref[...])
    kap = kk / jnp.maximum(jnp.sqrt(ss), 1e-12)
    r_ref[...] = r
    lw_ref[...] = -jnp.exp(w)
    k_ref[...] = k * (1.0 + (a - 1.0) * ka_ref[...])
    v_ref[...] = v
    kap_ref[...] = kap
    bet_ref[...] = kap * a
    g_ref[...] = _dot_lo(jax.nn.sigmoid(gd), g2_ref[...])


def _rw_prep(zr, shift, B, T, P):
    n, cols = zr.shape
    G = GROUP_W
    tt = _row_tile(T, PREP_ROWS)
    nt = T // tt
    ones_bd = jnp.kron(jnp.eye(RW_HEADS, dtype=F32), jnp.ones((RW_HD, RW_HD), F32)).astype(BF16)
    row = lambda x: x.reshape(1, -1)
    full = lambda shape: pl.BlockSpec(shape, lambda b, i: (0,) * len(shape))
    tile = pl.BlockSpec((tt, G), lambda b, i: (b * nt + i, 0))
    return pl.pallas_call(
        _rw_prep_body,
        grid=(B, nt),
        in_specs=[
            pl.BlockSpec((tt, cols), lambda b, i: (b * nt + i, 0)),
            pl.BlockSpec((1, 1, cols), lambda b, i: (b, 0, 0)),
            full((1, cols)), full((1, G)), full((RW_DECAY_LORA, G)), full((1, G)), full((RW_A_LORA, G)),
            full((RW_GATE_LORA, G)), full((1, G)), full((1, G)), full((G, G)),
        ],
        out_specs=[tile] * 7,
        out_shape=[jax.ShapeDtypeStruct((n, G), F32)] * 7,
        scratch_shapes=[pltpu.VMEM((8, cols), F32)],
        compiler_params=_params(("parallel", "arbitrary"), 40),
        name="rwkv_prep",
    )(zr, shift.reshape(B, 1, cols), row(P['rw_mu']), row(P['rw_w0']), P['rw_w2'].astype(BF16), row(P['rw_a0']),
      P['rw_a2'].astype(BF16), P['rw_g2'].astype(BF16), row(P['rw_kk']), row(P['rw_ka']), ones_bd)


def _rw_scaled(lw, kap, bet, tri):
    cs = _dot_exact_lhs(tri, lw)
    return cs, kap * jnp.exp(cs - lw), bet * jnp.exp(-cs)


def _tri_masks(c):
    row = lax.broadcasted_iota(jnp.int32, (c, c), 0)
    col = lax.broadcasted_iota(jnp.int32, (c, c), 1)
    return row, col


def _chunk_tri(rows, c):
    row, col = _tri_masks(rows)
    return jnp.where((col <= row) & (row // c == col // c), 1.0, 0.0).astype(BF16)


def _rw_ldiag_body(lw_ref, kap_ref, bet_ref, o_ref, *, c):
    rows = lw_ref.shape[0]
    _, kk_all, bt_all = _rw_scaled(lw_ref[...], kap_ref[...], bet_ref[...], _chunk_tri(rows, c))
    srow, scol = _tri_masks(RW_SUB)
    units = [(cc, h) for cc in range(rows // c) for h in range(RW_HEADS)]
    Ls = [_dot_lo(kk_all[cc * c:(cc + 1) * c, h * RW_HD:(h + 1) * RW_HD],
                  bt_all[cc * c:(cc + 1) * c, h * RW_HD:(h + 1) * RW_HD], _NT) for cc, h in units]
    for (cc, h), L in zip(units, Ls):
        for b in range(c // RW_SUB):
            rs = slice(b * RW_SUB, (b + 1) * RW_SUB)
            o_ref[cc * c + b * RW_SUB:cc * c + (b + 1) * RW_SUB, h * RW_SUB:(h + 1) * RW_SUB] = (
                jnp.where(scol < srow, L[rs, rs], 0.0))


def _rw_inv_body(l_ref, t_ref, a_ref, b_ref):
    n = RW_SUB
    nblk = l_ref.shape[0] // n
    for t in range(n):
        a_ref[t] = l_ref[pl.ds(t, nblk, stride=n), :].T
    entry = lambda ref, t, s: ref.at[t, pl.ds(s, RW_HEADS, stride=n), :]
    one = jnp.ones((RW_HEADS, nblk), F32)
    zero = jnp.zeros((RW_HEADS, nblk), F32)
    for t in range(n):
        for s in range(n):
            if s > t:
                entry(b_ref, t, s)[...] = zero
            elif s == t:
                entry(b_ref, t, s)[...] = one
            else:
                acc = entry(a_ref, t, s)[...]
                for j in range(s + 1, t):
                    acc = acc + entry(a_ref, t, j)[...] * entry(b_ref, j, s)[...]
                entry(b_ref, t, s)[...] = -acc
    for t in range(n):
        t_ref[pl.ds(t, nblk, stride=n), :] = b_ref[t].T


def _rw_main_body(r_ref, lw_ref, k_ref, v_ref, kap_ref, bet_ref, g_ref, td_ref, h0_ref, rk_ref, lng_ref, lnb_ref,
                  o_ref, hout_ref, h_ref, *, c):
    ci = pl.program_id(1)
    rows = r_ref.shape[0]
    nb = c // RW_SUB

    @pl.when(ci == 0)
    def _():
        h_ref[...] = h0_ref[0]

    crow, ccol = _tri_masks(c)
    strict = ccol < crow
    incl = ccol <= crow
    lw = lw_ref[...]
    cs, kk_all, bt_all = _rw_scaled(lw, kap_ref[...], bet_ref[...], _chunk_tri(rows, c))
    gi = jnp.exp(-cs)
    gg = jnp.exp(cs)
    kt_all = k_ref[...] * gi
    rt_all = r_ref[...] * gg
    bonus_all = r_ref[...] * k_ref[...] * rk_ref[...]
    hrow = lax.broadcasted_iota(jnp.int32, (RW_HD, RW_HD), 0)
    hcol = lax.broadcasted_iota(jnp.int32, (RW_HD, RW_HD), 1)
    HS = range(RW_HEADS)
    units = [(cc, h) for cc in range(rows // c) for h in HS]
    US = range(len(units))
    rsl = [slice(cc * c, (cc + 1) * c) for cc, _ in units]
    lsl = [slice(h * RW_HD, (h + 1) * RW_HD) for _, h in units]
    Kk = [kk_all[rsl[u], lsl[u]] for u in US]
    Bt = [bt_all[rsl[u], lsl[u]] for u in US]
    Kt = [kt_all[rsl[u], lsl[u]] for u in US]
    Rt = [rt_all[rsl[u], lsl[u]] for u in US]
    vv = [v_ref[rsl[u], lsl[u]] for u in US]
    Lm = [jnp.where(strict, _dot_lo(Kk[u], Bt[u], _NT), 0.0) for u in US]
    A1 = [jnp.where(strict, _dot_lo(Kk[u], Kt[u], _NT), 0.0) for u in US]
    A4 = [jnp.where(incl, _dot_lo(Rt[u], Bt[u], _NT), 0.0) for u in US]
    A3 = [jnp.where(incl, _dot_lo(Rt[u], Kt[u], _NT), 0.0) for u in US]
    X = [jnp.concatenate([Kk[u], _dot_lo(A1[u], vv[u])], axis=1) for u in US]
    zs = [[] for _ in US]
    for b in range(nb):
        rs = slice(b * RW_SUB, (b + 1) * RW_SUB)
        rhs = [X[u][rs] for u in US]
        if b:
            rhs = [rhs[u] - _dot_lo(Lm[u][rs, 0:b * RW_SUB], jnp.concatenate(zs[u], axis=0)) for u in US]
        for u, (cc, h) in enumerate(units):
            tbb = td_ref[cc * c + b * RW_SUB:cc * c + (b + 1) * RW_SUB, h * RW_SUB:(h + 1) * RW_SUB]
            zs[u].append(_dot_lo(tbb, rhs[u]))
    Z = [jnp.concatenate(zs[u], axis=0) if nb > 1 else zs[u][0] for u in US]
    A4Z = [_dot_lo(A4[u], Z[u]) for u in US]
    Rhat = [Rt[u] - A4Z[u][:, :RW_HD] for u in US]
    Yhat = [_dot_lo(A3[u], vv[u]) - A4Z[u][:, RW_HD:] for u in US]
    gC = [gg[(cc + 1) * c - 1:(cc + 1) * c, lsl[u]] for u, (cc, _) in enumerate(units)]
    MN = [_dot_lo(Bt[u] * gC[u], Z[u], _TN) for u in US]
    Mp = [jnp.where(hrow == hcol, gC[u], 0.0) - MN[u][:, :RW_HD] for u in US]
    Np = [_dot_lo(Kt[u] * gC[u], vv[u], _TN) - MN[u][:, RW_HD:] for u in US]
    H = [h_ref[h] for h in HS]
    ys = [None for _ in US]
    for cc in range(rows // c):
        for h in HS:
            u = cc * RW_HEADS + h
            ys[u] = _dot_lo(Rhat[u], H[h]) + Yhat[u]
        H = [_dot_hi(Mp[cc * RW_HEADS + h], H[h]) + Np[cc * RW_HEADS + h] for h in HS]
    for h in HS:
        h_ref[h] = H[h]
    for u in US:
        y = ys[u]
        mu = jnp.mean(y, axis=-1, keepdims=True)
        var = jnp.mean(jnp.square(y - mu), axis=-1, keepdims=True)
        yn = (y - mu) * lax.rsqrt(var + RW_LN_EPS) * lng_ref[:, lsl[u]] + lnb_ref[:, lsl[u]]
        yn = yn + jnp.sum(bonus_all[rsl[u], lsl[u]], axis=-1, keepdims=True) * vv[u]
        o_ref[rsl[u], lsl[u]] = (yn * g_ref[rsl[u], lsl[u]]).astype(o_ref.dtype)

    @pl.when(ci == pl.num_programs(1) - 1)
    def _():
        hout_ref[0] = h_ref[...]


def _rwkv7(zr, shift, S0, B, T, P):
    G = GROUP_W
    n = B * T
    r, lw, k, v, kap, bet, g = _rw_prep(zr, shift, B, T, P)
    c = min(RW_CHUNK, T)
    rows_l = _row_tile(T, c * RW_LDIAG_CHUNKS)
    rows_m = _row_tile(T, c * RW_MAIN_CHUNKS)
    nl, nc = T // rows_l, T // rows_m
    tile_l = pl.BlockSpec((rows_l, G), lambda b, i: (b * nl + i, 0))
    tile = pl.BlockSpec((rows_m, G), lambda b, i: (b * nc + i, 0))
    ld = pl.pallas_call(
        functools.partial(_rw_ldiag_body, c=c),
        grid=(B, nl),
        in_specs=[tile_l] * 3,
        out_specs=pl.BlockSpec((rows_l, RW_HEADS * RW_SUB), lambda b, i: (b * nl + i, 0)),
        out_shape=jax.ShapeDtypeStruct((n, RW_HEADS * RW_SUB), F32),
        compiler_params=_params(("parallel", "parallel"), 32),
        name="rwkv_ldiag",
    )(lw, kap, bet)
    rows_i = V7X_LANES * RW_SUB
    npad = -(-n // rows_i) * rows_i
    inv_spec = pl.BlockSpec((rows_i, RW_HEADS * RW_SUB), lambda i: (i, 0))
    inv_scratch = pltpu.VMEM((RW_SUB, RW_HEADS * RW_SUB, V7X_LANES), F32)
    td = pl.pallas_call(
        _rw_inv_body,
        grid=(npad // rows_i,),
        in_specs=[inv_spec],
        out_specs=inv_spec,
        out_shape=jax.ShapeDtypeStruct((npad, RW_HEADS * RW_SUB), F32),
        scratch_shapes=[inv_scratch, inv_scratch],
        compiler_params=_params(("parallel",), 32),
        name="rwkv_inv",
    )(jnp.pad(ld, ((0, npad - n), (0, 0))))[:n]
    h0 = jnp.swapaxes(S0, -1, -2)
    prow = lambda x: pl.BlockSpec((1, G), lambda b, i: (0, 0))
    st_spec = pl.BlockSpec((1, RW_HEADS, RW_HD, RW_HD), lambda b, i: (b, 0, 0, 0))
    out, hl = pl.pallas_call(
        functools.partial(_rw_main_body, c=c),
        grid=(B, nc),
        in_specs=[tile] * 7 + [pl.BlockSpec((rows_m, RW_HEADS * RW_SUB), lambda b, i: (b * nc + i, 0)), st_spec,
                               prow(0), prow(0), prow(0)],
        out_specs=[tile, st_spec],
        out_shape=[jax.ShapeDtypeStruct((n, G), BF16), jax.ShapeDtypeStruct(S0.shape, F32)],
        scratch_shapes=[pltpu.VMEM((RW_HEADS, RW_HD, RW_HD), F32)],
        compiler_params=_params(("parallel", "arbitrary"), 32),
        name="rwkv_main",
    )(r, lw, k, v, kap, bet, g, td, h0, P['rw_rk'].reshape(1, G), P['rw_ln_g'].reshape(1, G),
      P['rw_ln_b'].reshape(1, G))
    return out, zr.reshape(B, T, -1)[:, -1], jnp.swapaxes(hl, -1, -2)


def _even_mixer(x2, B, T, g, st, P):
    conv_buf, lru_h, k_past, v_past, lf_past = st
    G = GROUP_W
    z_rg, z_qkv, z_og, z_fl = _norm_matmul(x2, g, P['e_w_in'], (2 * G, 3 * G, G, V7X_LANES))
    rnn_out, conv_new, h_last = _lru(z_rg, conv_buf, lru_h, B, T, P)
    qb, kn, kb, v, vb, lf = _fox_prep(z_qkv, z_fl, B, T, P)
    past = k_past.shape[1]
    lf_all = lf.reshape(B, T, V7X_LANES)
    kb_all, vb_all = kb.reshape(B, T, G), vb.reshape(B, T, G)
    if past:
        lf_all = jnp.concatenate([jnp.pad(lf_past, ((0, 0), (0, 0), (0, V7X_LANES - FOX_HEADS))), lf_all], axis=1)
        kb_all = jnp.concatenate([k_past.reshape(B, past, G).astype(BF16), kb_all], axis=1)
        vb_all = jnp.concatenate([v_past.reshape(B, past, G).astype(BF16), vb_all], axis=1)
    tail = ((0, 0), (0, -(past + T) % FOX_K_ROWS), (0, 0))
    ka, vt = _fox_keys(jnp.pad(lf_all, tail), jnp.pad(kb_all, tail), jnp.pad(vb_all, tail))
    fox_out = _fox_attention(qb, ka, vt, z_og, B, T, past)
    heads = lambda t: t.reshape(B, T, FOX_HEADS, FOX_HD)
    return (rnn_out, fox_out, P['e_w_out']), (conv_new, h_last, heads(kn), heads(v), lf.reshape(B, T, V7X_LANES)[..., :FOX_HEADS])


def _odd_mixer(x2, B, T, g, st, lb, P):
    S_hg, shift, S_rw = st
    G = GROUP_W
    z_hg, z_rw = _norm_matmul(x2, g, P['o_w_in'], (4 * G, P['o_w_in'].shape[1] - 4 * G))
    hg_out, S_hg_new = _hgrn2(z_hg, lb, S_hg, B, T, P)
    rw_out, shift_new, S_rw_new = _rwkv7(z_rw, shift, S_rw, B, T, P)
    return (hg_out, rw_out, P['o_w_out']), (S_hg_new, shift_new, S_rw_new)


def _trunk(x, states, W):
    lru_conv, lru_h, fox_k, fox_v, fox_lf, hg_S, rw_shift, rw_S = states
    B, T, D = x.shape
    depth = W['norm_g'].shape[0]
    sm = jax.nn.softmax(W['hg_lb_logits'], axis=0)
    lower_bounds = jnp.cumsum(sm, axis=0) - sm[0]
    x2 = x.reshape(B * T, D)
    even_new, odd_new = [], []
    for layer in range(depth):
        g = W['norm_g'][layer]
        x2 = _ffn(x2, g[0], W['ffn_w_in'][layer][0], W['ffn_w_out'][layer][0])
        if layer % 2 == 0:
            e = layer // 2
            P = {n: W[n][e] for n in ('e_w_in', 'e_w_out', 'lru_conv_w', 'lru_conv_b', 'lru_wa', 'lru_ba', 'lru_wx',
                                      'lru_bx', 'lru_lambda', 'fox_q_gain', 'fox_k_gain', 'fox_f_bias')}
            mix, new = _even_mixer(x2, B, T, g[1], (lru_conv[e], lru_h[e], fox_k[e], fox_v[e], fox_lf[e]), P)
            even_new.append(new)
        else:
            o = layer // 2
            P = {n: W[n][o] for n in ('o_w_in', 'o_w_out', 'hg_norm_g', 'rw_mu', 'rw_w0', 'rw_w2', 'rw_a0', 'rw_a2',
                                      'rw_g2', 'rw_kk', 'rw_ka', 'rw_rk', 'rw_ln_g', 'rw_ln_b')}
            mix, new = _odd_mixer(x2, B, T, g[1], (hg_S[o], rw_shift[o], rw_S[o]), lower_bounds[layer], P)
            odd_new.append(new)
        x2 = _ffn(x2, g[2], W['ffn_w_in'][layer][1], W['ffn_w_out'][layer][1], mix)
    ev = [jnp.stack([n[j] for n in even_new]) for j in range(5)]
    od = [jnp.stack([n[j] for n in odd_new]) for j in range(3)]
    return x2.reshape(B, T, D), (ev[0], ev[1], ev[2], ev[3], ev[4], od[0], od[1], od[2])


def kernel(x_prompt, x_sample, state_lru_conv, state_lru_h, cache_fox_k, cache_fox_v, cache_fox_logf,
           state_hgrn_S, state_rwkv_shift, state_rwkv_S, norm_g, ffn_w_in, ffn_w_out, e_w_in, e_w_out,
           lru_conv_w, lru_conv_b, lru_wa, lru_ba, lru_wx, lru_bx, lru_lambda, fox_q_gain, fox_k_gain,
           fox_f_bias, o_w_in, o_w_out, hg_lb_logits, hg_norm_g, rw_mu, rw_w0, rw_w2, rw_a0, rw_a2, rw_g2,
           rw_kk, rw_ka, rw_rk, rw_ln_g, rw_ln_b):
    n_even, n_odd = e_w_in.shape[0], o_w_in.shape[0]
    W = dict(norm_g=norm_g, ffn_w_in=_ffn_w_in_tiles(ffn_w_in), ffn_w_out=ffn_w_out.astype(BF16),
             e_w_in=_pad_cols(e_w_in.astype(BF16)), e_w_out=e_w_out.astype(BF16),
             lru_conv_w=lru_conv_w, lru_conv_b=lru_conv_b, lru_wa=lru_wa, lru_ba=lru_ba, lru_wx=lru_wx,
             lru_bx=lru_bx, lru_lambda=lru_lambda, fox_q_gain=fox_q_gain, fox_k_gain=fox_k_gain,
             fox_f_bias=fox_f_bias, o_w_in=o_w_in.astype(BF16), o_w_out=o_w_out.astype(BF16),
             hg_lb_logits=hg_lb_logits, hg_norm_g=hg_norm_g, rw_mu=rw_mu, rw_w0=rw_w0, rw_w2=rw_w2, rw_a0=rw_a0,
             rw_a2=rw_a2, rw_g2=rw_g2, rw_kk=rw_kk, rw_ka=rw_ka, rw_rk=rw_rk, rw_ln_g=rw_ln_g, rw_ln_b=rw_ln_b)
    nb = x_prompt.shape[0]
    dt = x_prompt.dtype
    prompt_states = (jnp.zeros((n_even, nb, CONV_W - 1, GROUP_W), dt),
                     jnp.zeros((n_even, nb, GROUP_W), dt),
                     jnp.zeros((n_even, nb, 0, FOX_HEADS, FOX_HD), dt),
                     jnp.zeros((n_even, nb, 0, FOX_HEADS, FOX_HD), dt),
                     jnp.zeros((n_even, nb, 0, FOX_HEADS), dt),
                     jnp.zeros((n_odd, nb, HG_HEADS, GROUP_W // HG_HEADS, GROUP_W // HG_HEADS), dt),
                     jnp.zeros((n_odd, nb, rw_mu.shape[1]), dt),
                     jnp.zeros((n_odd, nb, RW_HEADS, RW_HD, RW_HD), dt))
    sample_states = (state_lru_conv, state_lru_h, cache_fox_k, cache_fox_v, cache_fox_logf,
                     state_hgrn_S, state_rwkv_shift, state_rwkv_S)
    y_prompt, p_new = _trunk(x_prompt, prompt_states, W)
    y_sample, s_new = _trunk(x_sample, sample_states, W)
    lru_conv_p, lru_h_p, fox_k_p, fox_v_p, fox_logf_p, hgrn_S_p, rwkv_shift_p, rwkv_S_p = p_new
    lru_conv_s, lru_h_s, fox_k_s, fox_v_s, fox_logf_s, hgrn_S_s, rwkv_shift_s, rwkv_S_s = s_new
    return (y_prompt, y_sample, lru_conv_p, lru_conv_s, lru_h_p, lru_h_s, fox_k_p, fox_k_s, fox_v_p, fox_v_s,
            fox_logf_p, fox_logf_s, hgrn_S_p, hgrn_S_s, rwkv_shift_p, rwkv_shift_s, rwkv_S_p, rwkv_S_s)
```

```python
import functools

import jax
import jax.numpy as jnp
from jax import lax
from jax.experimental import pallas as pl
from jax.experimental.pallas import tpu as pltpu

F32 = jnp.float32
BF16 = jnp.bfloat16

NORM_EPS = 1e-6
GROUP_W = 512
CONV_W = 4
LRU_C = 8.0
FOX_HEADS = 8
FOX_HD = 64
HG_HEADS = 4
RW_HEADS = 8
RW_HD = 64
RW_DECAY_LORA = 64
RW_A_LORA = 64
RW_GATE_LORA = 128
RW_LN_EPS = 64e-5

V7X_LANES = 128
FFN_COL_TILE = 1408
PREP_ROWS = 512
FFN_ROW_TILE = 1024
FFN_VMEM_MIB = 60


def _row_tile(n, want):
    t = min(n, want)
    while n % t:
        t //= 2
    return t


def _params(sem, vmem_mib):
    return pltpu.CompilerParams(dimension_semantics=sem, vmem_limit_bytes=vmem_mib << 20)


def _pad_cols(w):
    pad = -w.shape[-1] % V7X_LANES
    return jnp.pad(w, [(0, 0)] * (w.ndim - 1) + [(0, pad)])


def _rms(x, g):
    return x * lax.rsqrt(jnp.mean(x * x, axis=-1, keepdims=True) + NORM_EPS) * g


def _ffn_body(*refs, mixed):
    if mixed:
        x_ref, a_ref, b_ref, wa_ref, wb_ref, g_ref, wi_ref, wo_ref, o_ref, h_ref, acc_ref = refs
    else:
        x_ref, g_ref, wi_ref, wo_ref, o_ref, h_ref, acc_ref = refs
    j = pl.program_id(1)

    @pl.when(j == 0)
    def _():
        x = x_ref[...]
        if mixed:
            x = x + jnp.dot(a_ref[...], wa_ref[...], preferred_element_type=F32)
            x = x + jnp.dot(b_ref[...], wb_ref[...], preferred_element_type=F32)
        h_ref[...] = _rms(x, g_ref[...]).astype(BF16)
        acc_ref[...] = 2.0 * x

    tf = wo_ref.shape[0]
    gu = jnp.dot(h_ref[...], wi_ref[...], preferred_element_type=F32)
    gate, up = gu[:, :tf], gu[:, tf:]
    act = (gate * jax.nn.sigmoid(gate) * up).astype(BF16)
    acc_ref[...] += jnp.dot(act, wo_ref[...], preferred_element_type=F32)

    @pl.when(j == pl.num_programs(1) - 1)
    def _():
        o_ref[...] = 0.5 * acc_ref[...]


def _cast_body(x_ref, o_ref):
    o_ref[...] = x_ref[...].astype(o_ref.dtype)


def _ffn_w_in_tiles(w_in):
    *lead, d, f2 = w_in.shape
    tf = FFN_COL_TILE
    nf = f2 // 2 // tf
    w = w_in.reshape(-1, d, f2)
    out = pl.pallas_call(
        _cast_body,
        grid=(w.shape[0], nf, 2),
        in_specs=[pl.BlockSpec((1, d, tf), lambda i, j, gu: (i, 0, gu * nf + j))],
        out_specs=pl.BlockSpec((1, d, tf), lambda i, j, gu: (i, 0, 2 * j + gu)),
        out_shape=jax.ShapeDtypeStruct(w.shape, BF16),
        compiler_params=_params(("parallel", "parallel", "parallel"), 32),
        name="ffn_weight_tiles",
    )(w)
    return out.reshape(*lead, d, f2)


def _ffn(x, g, w_in, w_out, mix=None):
    n, d = x.shape
    f = w_out.shape[0]
    tm = _row_tile(n, FFN_ROW_TILE)
    tf = FFN_COL_TILE
    nf = f // tf
    row_spec = lambda w: pl.BlockSpec((tm, w), lambda i, j: (i, 0))
    args, specs = [x], [row_spec(d)]
    if mix is not None:
        a, b, w = mix
        ga, gb = a.shape[1], b.shape[1]
        args += [a, b, w[:ga], w[ga:]]
        specs += [row_spec(ga), row_spec(gb), pl.BlockSpec((ga, d), lambda i, j: (0, 0)),
                  pl.BlockSpec((gb, d), lambda i, j: (0, 0))]
    args += [g.reshape(1, d), w_in, w_out]
    specs += [pl.BlockSpec((1, d), lambda i, j: (0, 0)), pl.BlockSpec((d, 2 * tf), lambda i, j: (0, j)),
              pl.BlockSpec((tf, d), lambda i, j: (j, 0))]
    return pl.pallas_call(
        functools.partial(_ffn_body, mixed=mix is not None),
        grid=(n // tm, nf),
        in_specs=specs,
        out_specs=row_spec(d),
        out_shape=jax.ShapeDtypeStruct((n, d), F32),
        scratch_shapes=[pltpu.VMEM((tm, d), BF16), pltpu.VMEM((tm, d), F32)],
        compiler_params=_params(("parallel", "arbitrary"), FFN_VMEM_MIB),
        name="ffn",
    )(*args)


def _norm_matmul_body(x_ref, g_ref, w_ref, *o_refs):
    h = _rms(x_ref[...], g_ref[...]).astype(BF16)
    z = jnp.dot(h, w_ref[...], preferred_element_type=F32)
    start = 0
    for o_ref in o_refs:
        width = o_ref.shape[1]
        o_ref[...] = z[:, start:start + width]
        start += width


def _norm_matmul(x, g, w, widths):
    n, d = x.shape
    c = w.shape[1]
    assert sum(widths) == c and all(wd % V7X_LANES == 0 for wd in widths)
    tm = _row_tile(n, 512)
    return pl.pallas_call(
        _norm_matmul_body,
        grid=(n // tm,),
        in_specs=[
            pl.BlockSpec((tm, d), lambda i: (i, 0)),
            pl.BlockSpec((1, d), lambda i: (0, 0)),
            pl.BlockSpec((d, c), lambda i: (0, 0)),
        ],
        out_specs=[pl.BlockSpec((tm, wd), lambda i: (i, 0)) for wd in widths],
        out_shape=[jax.ShapeDtypeStruct((n, wd), F32) for wd in widths],
        compiler_params=_params(("parallel",), 48),
        name="norm_matmul",
    )(x, g.reshape(1, d), w)


LRU_ROWS = 256
CONV_PAD = 8


def _expm1(x):
    series = x * (1.0 + x * (1 / 2 + x * (1 / 6 + x * (1 / 24 + x * (1 / 120 + x * (1 / 720 + x * (1 / 5040 + x * (1 / 40320))))))))
    return jnp.where(jnp.abs(x) < 0.25, series, jnp.exp(x) - 1.0)


def _shift_rows(x, s, fill):
    row = lax.broadcasted_iota(jnp.int32, x.shape, 0)
    return jnp.where(row >= s, pltpu.roll(x, s, axis=0), fill)


def _lru_body(z_ref, buf_ref, h0_ref, cw_ref, cb_ref, wa_ref, ba_ref, wx_ref, bx_ref, lam_ref,
              o_ref, bufo_ref, ho_ref, x_ref, hc_ref):
    G = GROUP_W
    tt = z_ref.shape[0]

    @pl.when(pl.program_id(1) == 0)
    def _():
        x_ref[0:CONV_PAD, :] = buf_ref[0]
        hc_ref[...] = jnp.broadcast_to(h0_ref[0], hc_ref.shape)

    x_ref[CONV_PAD:CONV_PAD + tt, :] = z_ref[:, 0:G]
    xc = cb_ref[...]
    for j in range(CONV_W):
        lo = CONV_PAD - (CONV_W - 1) + j
        xc = xc + x_ref[lo:lo + tt, :] * cw_ref[j:j + 1, :]
    hist = x_ref[tt:tt + CONV_PAD, :]
    x_ref[0:CONV_PAD, :] = hist
    bufo_ref[0] = hist

    xb = xc.astype(BF16)
    r = jax.nn.sigmoid(jnp.dot(xb, wa_ref[...], preferred_element_type=F32) + ba_ref[...])
    ig = jax.nn.sigmoid(jnp.dot(xb, wx_ref[...], preferred_element_type=F32) + bx_ref[...])
    log_a = (-LRU_C * _softplus(-lam_ref[...])) * r
    a = jnp.exp(log_a)
    b = jnp.sqrt(-_expm1(2.0 * log_a)) * (ig * xc)
    s = 1
    while s < tt:
        if s % 8:
            b = a * _shift_rows(b, s, 0.0) + b
            a = a * _shift_rows(a, s, 1.0)
        else:
            b = jnp.concatenate([b[:s], a[s:] * b[:tt - s] + b[s:]], axis=0)
            a = jnp.concatenate([a[:s], a[s:] * a[:tt - s]], axis=0)
        s *= 2
    h = a * hc_ref[0:1, :] + b
    hc_ref[...] = jnp.broadcast_to(h[tt - 1:tt, :], hc_ref.shape)
    ho_ref[0] = h[tt - 1:tt, :]
    o_ref[...] = (jax.nn.gelu(z_ref[:, G:2 * G]) * h).astype(o_ref.dtype)


def _block_diag_dense(w):
    nb, bs, _ = w.shape
    eye = jnp.eye(nb, dtype=w.dtype)
    return (eye[:, None, :, None] * w[:, :, None, :]).reshape(nb * bs, nb * bs)


def _lru(z_rg, conv_buf, h0, B, T, P):
    G = GROUP_W
    n = B * T
    tt = _row_tile(T, LRU_ROWS)
    nt = T // tt
    buf = jnp.pad(conv_buf, ((0, 0), (CONV_PAD - (CONV_W - 1), 0), (0, 0)))
    cw = jnp.pad(P['lru_conv_w'], ((0, CONV_PAD - CONV_W), (0, 0)))
    row = lambda x: x.reshape(1, G)
    full = lambda shape: pl.BlockSpec(shape, lambda b, i: (0,) * len(shape))
    out, bufo, ho = pl.pallas_call(
        _lru_body,
        grid=(B, nt),
        in_specs=[
            pl.BlockSpec((tt, 2 * G), lambda b, i: (b * nt + i, 0)),
            pl.BlockSpec((1, CONV_PAD, G), lambda b, i: (b, 0, 0)),
            pl.BlockSpec((1, 1, G), lambda b, i: (b, 0, 0)),
            full((CONV_PAD, G)), full((1, G)), full((G, G)), full((1, G)), full((G, G)), full((1, G)), full((1, G)),
        ],
        out_specs=[
            pl.BlockSpec((tt, G), lambda b, i: (b * nt + i, 0)),
            pl.BlockSpec((1, CONV_PAD, G), lambda b, i: (b, 0, 0)),
            pl.BlockSpec((1, 1, G), lambda b, i: (b, 0, 0)),
        ],
        out_shape=[jax.ShapeDtypeStruct((n, G), BF16), jax.ShapeDtypeStruct((B, CONV_PAD, G), F32),
                   jax.ShapeDtypeStruct((B, 1, G), F32)],
        scratch_shapes=[pltpu.VMEM((tt + CONV_PAD, G), F32), pltpu.VMEM((8, G), F32)],
        compiler_params=_params(("parallel", "arbitrary"), 32),
        name="lru",
    )(z_rg, buf, h0.reshape(B, 1, G), cw, row(P['lru_conv_b']), _block_diag_dense(P['lru_wa']).astype(BF16),
      row(P['lru_ba']), _block_diag_dense(P['lru_wx']).astype(BF16), row(P['lru_bx']), row(P['lru_lambda']))
    return out, bufo[:, CONV_PAD - (CONV_W - 1):], ho.reshape(B, G)


FOX_Q_COLS = 2048
FOX_K_ROWS = 512
FOX_F_SPLIT = 3
FOX_NEG = -1e30
LOG2E = 1.4426950408889634
HEAD_PAIRS = FOX_HEADS // 2
PAIR_W = 2 * FOX_HD
FOX_KEY_TILE = 512


def _fox_prep_body(z_ref, fl_ref, qg_ref, kg_ref, fb_ref, ones_ref, q_ref, k_ref, kb_ref, v_ref, vb_ref, lf_ref,
                   *, q_transposed):
    G = GROUP_W
    q, k, v = z_ref[:, 0:G], z_ref[:, G:2 * G], z_ref[:, 2 * G:3 * G]
    inv = 1.0 / FOX_HD
    qn = q * lax.rsqrt(_dot_exact_rhs(q * q, ones_ref[...]) * inv + NORM_EPS) * qg_ref[...]
    kn = k * lax.rsqrt(_dot_exact_rhs(k * k, ones_ref[...]) * inv + NORM_EPS) * kg_ref[...]
    qs = qn * (LOG2E * FOX_HD ** -0.5)
    if q_transposed:
        for p in range(HEAD_PAIRS):
            q_ref[0, p] = qs[:, p * PAIR_W:(p + 1) * PAIR_W].T.astype(BF16)
    else:
        q_ref[...] = qs.astype(BF16)
    tt = z_ref.shape[0]
    for h in range(FOX_HEADS):
        k_ref[pl.ds(h, tt, stride=FOX_HEADS), :] = kn[:, h * FOX_HD:(h + 1) * FOX_HD]
        v_ref[pl.ds(h, tt, stride=FOX_HEADS), :] = v[:, h * FOX_HD:(h + 1) * FOX_HD]
    kb_ref[...] = kn.astype(BF16)
    vb_ref[...] = v.astype(BF16)
    x = fl_ref[...] + fb_ref[...]
    lf_ref[...] = -_softplus(-x)


def _fox_prep(z_qkv, z_fl, B, T, P):
    n = z_qkv.shape[0]
    G = GROUP_W
    tt = _row_tile(T, PREP_ROWS)
    nt = T // tt
    q_transposed = tt % V7X_LANES == 0
    ones_bd = jnp.kron(jnp.eye(FOX_HEADS, dtype=F32), jnp.ones((FOX_HD, FOX_HD), F32)).astype(BF16)
    fb = jnp.pad(P['fox_f_bias'], (0, V7X_LANES - FOX_HEADS)).reshape(1, V7X_LANES)
    tile = lambda w: pl.BlockSpec((tt, w), lambda b, i: (b * nt + i, 0))
    full = lambda shape: pl.BlockSpec(shape, lambda b, i: (0,) * len(shape))
    if q_transposed:
        q_spec = pl.BlockSpec((1, HEAD_PAIRS, PAIR_W, tt), lambda b, i: (b, 0, 0, i))
        q_shape = jax.ShapeDtypeStruct((B, HEAD_PAIRS, PAIR_W, T), BF16)
    else:
        q_spec, q_shape = tile(G), jax.ShapeDtypeStruct((n, G), BF16)
    state_spec = pl.BlockSpec((tt * FOX_HEADS, FOX_HD), lambda b, i: (b * nt + i, 0))
    state_shape = jax.ShapeDtypeStruct((n * FOX_HEADS, FOX_HD), F32)
    return pl.pallas_call(
        functools.partial(_fox_prep_body, q_transposed=q_transposed),
        grid=(B, nt),
        in_specs=[tile(3 * G), tile(V7X_LANES), full((1, G)), full((1, G)), full((1, V7X_LANES)), full((G, G))],
        out_specs=[q_spec, state_spec, tile(G), state_spec, tile(G), tile(V7X_LANES)],
        out_shape=[q_shape, state_shape, jax.ShapeDtypeStruct((n, G), BF16), state_shape,
                   jax.ShapeDtypeStruct((n, G), BF16), jax.ShapeDtypeStruct((n, V7X_LANES), F32)],
        compiler_params=_params(("parallel", "parallel"), 32),
        name="fox_prep",
    )(z_qkv, z_fl, jnp.tile(P['fox_q_gain'], FOX_HEADS).reshape(1, G),
      jnp.tile(P['fox_k_gain'], FOX_HEADS).reshape(1, G), fb, ones_bd)


def _fox_keys_body(lf_ref, kb_ref, vb_ref, ka_ref, vt_ref, c_ref):
    tt = lf_ref.shape[1]

    @pl.when(pl.program_id(1) == 0)
    def _():
        c_ref[...] = jnp.zeros_like(c_ref)

    row, col = _tri_masks(tt)
    tri = jnp.where(col <= row, 1.0, 0.0).astype(BF16)
    f = _dot_exact_lhs(tri, lf_ref[0]) + c_ref[0:1, :]
    c_ref[...] = jnp.broadcast_to(f[tt - 1:tt, :], c_ref.shape)
    parts = _split3(f * LOG2E)
    srow = lax.broadcasted_iota(jnp.int32, (V7X_LANES, FOX_HD), 0)
    scol = lax.broadcasted_iota(jnp.int32, (V7X_LANES, FOX_HD), 1)
    for h in range(FOX_HEADS):
        aug = jnp.zeros((tt, FOX_HD), F32)
        for t, part in enumerate(parts):
            sel = jnp.where((srow == h) & (scol == t), 1.0, 0.0).astype(BF16)
            aug = aug + jnp.dot(part, sel, preferred_element_type=F32)
        ka_ref[0, h] = jnp.concatenate([kb_ref[0, :, h * FOX_HD:(h + 1) * FOX_HD], aug.astype(BF16)], axis=1)
    for p in range(HEAD_PAIRS):
        vt_ref[0, p] = vb_ref[0, :, p * PAIR_W:(p + 1) * PAIR_W].astype(F32).T.astype(BF16)


def _fox_keys(lf_all, kb_all, vb_all):
    B, tk_all, L = lf_all.shape
    G = GROUP_W
    tt = FOX_KEY_TILE
    return pl.pallas_call(
        _fox_keys_body,
        grid=(B, tk_all // tt),
        in_specs=[pl.BlockSpec((1, tt, L), lambda b, i: (b, i, 0)),
                  pl.BlockSpec((1, tt, G), lambda b, i: (b, i, 0)),
                  pl.BlockSpec((1, tt, G), lambda b, i: (b, i, 0))],
        out_specs=[pl.BlockSpec((1, FOX_HEADS, tt, 2 * FOX_HD), lambda b, i: (b, 0, i, 0)),
                   pl.BlockSpec((1, HEAD_PAIRS, PAIR_W, tt), lambda b, i: (b, 0, 0, i))],
        out_shape=[jax.ShapeDtypeStruct((B, FOX_HEADS, tk_all, 2 * FOX_HD), BF16),
                   jax.ShapeDtypeStruct((B, HEAD_PAIRS, PAIR_W, tk_all), BF16)],
        scratch_shapes=[pltpu.VMEM((8, L), F32)],
        compiler_params=_params(("parallel", "arbitrary"), 32),
        name="fox_keys",
    )(lf_all, kb_all, vb_all)


def _fox_attn_body(qt_ref, ka_ref, vt_ref, og_ref, o_ref, acc_ref, m_ref, l_ref, *, past, tk, t_real):
    qi = pl.program_id(2)
    tq = qt_ref.shape[3]
    t_out = o_ref.shape[0]
    first_q = past + qi * tq
    last_q = past + jnp.minimum(qi * tq + tq, t_real) - 1
    n_full = (first_q + 1) // tk
    n_all = last_q // tk + 1
    drow = lax.broadcasted_iota(jnp.int32, (FOX_HD, tq), 0)
    minus = jnp.where(drow < FOX_F_SPLIT, -1.0, 0.0).astype(BF16)
    rhs = [jnp.concatenate([qt_ref[0, 0, h * FOX_HD:(h + 1) * FOX_HD, :], minus], axis=0) for h in range(2)]
    acc_ref[...] = jnp.zeros_like(acc_ref)

    def update(ki, m_prev, l_prev, masked, q0=0):
        ks = pl.multiple_of(ki * tk, tk)
        s = [jnp.dot(ka_ref[0, h, pl.ds(ks, tk), :], rhs[h][:, q0:], preferred_element_type=F32) for h in range(2)]
        if masked:
            krow = lax.broadcasted_iota(jnp.int32, (tk, tq - q0), 0)
            qcol = lax.broadcasted_iota(jnp.int32, (tk, tq - q0), 1)
            vis = ks + krow <= first_q + q0 + qcol
            s = [jnp.where(vis, s[h], FOX_NEG) for h in range(2)]
        m_new = [jnp.maximum(m_prev[h], jnp.max(s[h], axis=0, keepdims=True)) for h in range(2)]
        alpha = [jnp.exp2(m_prev[h] - m_new[h]) for h in range(2)]
        p = [jnp.exp2(s[h] - m_new[h]) for h in range(2)]
        l_new = [alpha[h] * l_prev[h] + jnp.sum(p[h], axis=0, keepdims=True) for h in range(2)]
        vt = [vt_ref[0, 0, h * FOX_HD:(h + 1) * FOX_HD, pl.ds(ks, tk)] for h in range(2)]
        pv = [jnp.dot(vt[h], p[h].astype(BF16), preferred_element_type=F32) for h in range(2)]
        for h in range(2):
            acc_ref[h, :, q0:] = alpha[h] * acc_ref[h, :, q0:] + pv[h]
        return m_new, l_new

    def carried(ki, c, masked):
        return update(ki, c[0], c[1], masked)

    c = ([jnp.full((1, tq), FOX_NEG, F32)] * 2, [jnp.zeros((1, tq), F32)] * 2)
    c = lax.fori_loop(0, n_full, lambda ki, c: carried(ki, c, False), c)
    if tq > tk and tq % tk == 0 and past % tq == 0 and t_real % tq == 0:
        m, l = carried(n_full, c, True)
        for h in range(2):
            m_ref[h], l_ref[h] = m[h], l[h]
        for j in range(1, tq // tk):
            q0 = j * tk
            m, l = update(n_full + j, [m_ref[h, :, q0:] for h in range(2)], [l_ref[h, :, q0:] for h in range(2)],
                          True, q0=q0)
            for h in range(2):
                m_ref[h, :, q0:], l_ref[h, :, q0:] = m[h], l[h]
    else:
        _, l = lax.fori_loop(n_full, n_all, lambda ki, c: carried(ki, c, True), c)
        for h in range(2):
            l_ref[h] = l[h]
    o_t = jnp.concatenate([acc_ref[h] / l_ref[h] for h in range(2)], axis=0)
    o_ref[...] = (o_t.T[:t_out] * jax.nn.sigmoid(og_ref[...])).astype(o_ref.dtype)


def _fox_attention(q, ka, vt, z_og, B, T, past):
    G = GROUP_W
    pw = PAIR_W
    tq = max(_row_tile(T, FOX_Q_COLS), V7X_LANES)
    tqp = -(-T // tq) * tq
    nq = tqp // tq
    t_out = min(tq, T)
    tk = FOX_K_ROWS
    tkp = ka.shape[2]
    if q.ndim == 2:
        q = q.reshape(B, T, HEAD_PAIRS, pw).transpose(0, 2, 3, 1)
    qt = jnp.pad(q, ((0, 0), (0, 0), (0, 0), (0, tqp - T)))
    return pl.pallas_call(
        functools.partial(_fox_attn_body, past=past, tk=tk, t_real=T),
        grid=(B, HEAD_PAIRS, nq),
        in_specs=[
            pl.BlockSpec((1, 1, pw, tq), lambda b, p, i: (b, p, 0, i)),
            pl.BlockSpec((1, 2, tkp, 2 * FOX_HD), lambda b, p, i: (b, p, 0, 0)),
            pl.BlockSpec((1, 1, pw, tkp), lambda b, p, i: (b, p, 0, 0)),
            pl.BlockSpec((t_out, pw), lambda b, p, i: (b * nq + i, p)),
        ],
        out_specs=pl.BlockSpec((t_out, pw), lambda b, p, i: (b * nq + i, p)),
        out_shape=jax.ShapeDtypeStruct((B * T, G), BF16),
        scratch_shapes=[pltpu.VMEM((2, FOX_HD, tq), F32), pltpu.VMEM((2, 1, tq), F32), pltpu.VMEM((2, 1, tq), F32)],
        compiler_params=_params(("parallel", "parallel", "arbitrary"), 40),
        name="fox_attn",
    )(qt, ka, vt, z_og)


HG_CHUNK = 64
HG_STEP_CHUNKS = 8


def _hgrn_body(z_ref, lb_ref, s0_ref, ng_ref, o_ref, so_ref, st_ref, *, c):
    G = GROUP_W
    rows = z_ref.shape[0]
    nch = rows // c
    dk = G // HG_HEADS

    @pl.when(pl.program_id(1) == 0)
    def _():
        st_ref[...] = s0_ref[0]

    lb = lb_ref[...]
    f = lb + (1.0 - lb) * jax.nn.sigmoid(z_ref[:, G:2 * G])
    kx = 1.0 - f
    crow, ccol = _tri_masks(c)
    incl = ccol <= crow
    gs = _dot_exact_lhs(_chunk_tri(rows, c), jnp.log(f))
    qg_all = z_ref[:, 0:G] * jnp.exp(gs)
    kg_all = kx * jnp.exp(-gs)
    HS = range(HG_HEADS)
    units = [(cc, h) for cc in range(nch) for h in HS]
    US = range(len(units))
    rsl = [slice(cc * c, (cc + 1) * c) for cc, _ in units]
    lsl = [slice(h * dk, (h + 1) * dk) for _, h in units]
    g_last = [gs[(cc + 1) * c - 1:(cc + 1) * c, lsl[u]] for u, (cc, _) in enumerate(units)]
    vv = [z_ref[rsl[u], 2 * G + h * dk:2 * G + (h + 1) * dk] for u, (_, h) in enumerate(units)]
    A = [jnp.where(incl, _dot_lo(qg_all[rsl[u], lsl[u]], kg_all[rsl[u], lsl[u]], _NT), 0.0) for u in US]
    av = [_dot_lo(A[u], vv[u]) for u in US]
    kd = [kx[rsl[u], lsl[u]] * jnp.exp(g_last[u] - gs[rsl[u], lsl[u]]) for u in US]
    upd = [_dot_lo(vv[u], kd[u], _TN) for u in US]
    st = [st_ref[h] for h in HS]
    o = [None for _ in US]
    for cc in range(nch):
        for h in HS:
            u = cc * HG_HEADS + h
            o[u] = _dot_lo(qg_all[rsl[u], lsl[u]], st[h], _NT) + av[u]
        st = [st[h] * jnp.exp(g_last[cc * HG_HEADS + h]) + upd[cc * HG_HEADS + h] for h in HS]
    for h in HS:
        st_ref[h] = st[h]
    for u, (_, h) in enumerate(units):
        hg = z_ref[rsl[u], 3 * G + h * dk:3 * G + (h + 1) * dk]
        o_ref[rsl[u], lsl[u]] = (_rms(o[u], ng_ref[:, lsl[u]]) * (hg * jax.nn.sigmoid(hg))).astype(o_ref.dtype)

    @pl.when(pl.program_id(1) == pl.num_programs(1) - 1)
    def _():
        so_ref[0] = st_ref[...]


def _hgrn2(z_hg, lb, S0, B, T, P):
    G = GROUP_W
    c = min(HG_CHUNK, T)
    rows = _row_tile(T, c * HG_STEP_CHUNKS)
    nc = T // rows
    dk = G // HG_HEADS
    st_spec = pl.BlockSpec((1, HG_HEADS, dk, dk), lambda b, i: (b, 0, 0, 0))
    out, so = pl.pallas_call(
        functools.partial(_hgrn_body, c=c),
        grid=(B, nc),
        in_specs=[pl.BlockSpec((rows, 4 * G), lambda b, i: (b * nc + i, 0)),
                  pl.BlockSpec((1, G), lambda b, i: (0, 0)), st_spec, pl.BlockSpec((1, G), lambda b, i: (0, 0))],
        out_specs=[pl.BlockSpec((rows, G), lambda b, i: (b * nc + i, 0)), st_spec],
        out_shape=[jax.ShapeDtypeStruct((B * T, G), BF16), jax.ShapeDtypeStruct(S0.shape, F32)],
        scratch_shapes=[pltpu.VMEM((HG_HEADS, dk, dk), F32)],
        compiler_params=_params(("parallel", "arbitrary"), 32),
        name="hgrn2",
    )(z_hg, lb.reshape(1, G), jnp.swapaxes(S0, -1, -2), P['hg_norm_g'].reshape(1, G))
    return out, jnp.swapaxes(so, -1, -2)


RW_CHUNK = 64
RW_SUB = 16
RW_LDIAG_CHUNKS = 4
RW_MAIN_CHUNKS = 2

_NT = (((1,), (1,)), ((), ()))
_TN = (((0,), (0,)), ((), ()))
_NN = (((1,), (0,)), ((), ()))


def _split3(x):
    h1 = x.astype(BF16)
    r1 = x - h1.astype(F32)
    h2 = r1.astype(BF16)
    h3 = (r1 - h2.astype(F32)).astype(BF16)
    return h1, h2, h3


def _dot_lo(a, b, dims=_NN):
    return lax.dot_general(a.astype(BF16), b.astype(BF16), dims, preferred_element_type=F32)


def _dot_hi(a, b, dims=_NN):
    ah = a.astype(BF16)
    al = (a - ah.astype(F32)).astype(BF16)
    bh = b.astype(BF16)
    bl = (b - bh.astype(F32)).astype(BF16)
    d = functools.partial(lax.dot_general, dimension_numbers=dims, preferred_element_type=F32)
    return d(ah, bh) + (d(al, bh) + d(ah, bl))


def _dot_exact_rhs(a, b):
    h1, h2, h3 = _split3(a)
    d = functools.partial(jnp.dot, preferred_element_type=F32)
    return d(h1, b) + (d(h2, b) + d(h3, b))


def _dot_exact_lhs(a, b):
    h1, h2, h3 = _split3(b)
    d = functools.partial(jnp.dot, preferred_element_type=F32)
    return d(a, h1) + (d(a, h2) + d(a, h3))


def _softplus(x):
    return jnp.maximum(x, 0.0) + jnp.log1p(jnp.exp(-jnp.abs(x)))


def _rw_prep_body(z_ref, shift_ref, mu_ref, w0_ref, w2_ref, a0_ref, a2_ref, g2_ref, kk_ref, ka_ref, ones_ref,
                  r_ref, lw_ref, k_ref, v_ref, kap_ref, bet_ref, g_ref, prev_ref):
    G = GROUP_W

    @pl.when(pl.program_id(1) == 0)
    def _():
        prev_ref[0:1, :] = shift_ref[0]

    z = z_ref[...]
    tt = z.shape[0]
    row = lax.broadcasted_iota(jnp.int32, z.shape, 0)
    shifted = jnp.where(row == 0, prev_ref[0:1, :], pltpu.roll(z, 1, axis=0))
    prev_ref[0:1, :] = z[tt - 1:tt, :]
    zm = z + (shifted - z) * mu_ref[...]
    r, k, v = zm[:, 0:G], zm[:, G:2 * G], zm[:, 2 * G:3 * G]
    o = 3 * G
    wd = zm[:, o:o + RW_DECAY_LORA]
    ad = zm[:, o + RW_DECAY_LORA:o + RW_DECAY_LORA + RW_A_LORA]
    gd = zm[:, o + RW_DECAY_LORA + RW_A_LORA:]
    w = -_softplus(-(w0_ref[...] + _dot_lo(jnp.tanh(wd), w2_ref[...]))) - 0.5
    a = jax.nn.sigmoid(a0_ref[...] + _dot_lo(ad, a2_ref[...]))
    kk = k * kk_ref[...]
    ss = _dot_exact_rhs(kk * kk, ones_ref[...])
    kap = kk / jnp.maximum(jnp.sqrt(ss), 1e-12)
    r_ref[...] = r
    lw_ref[...] = -jnp.exp(w)
    k_ref[...] = k * (1.0 + (a - 1.0) * ka_ref[...])
    v_ref[...] = v
    kap_ref[...] = kap
    bet_ref[...] = kap * a
    g_ref[...] = _dot_lo(jax.nn.sigmoid(gd), g2_ref[...])


def _rw_prep(zr, shift, B, T, P):
    n, cols = zr.shape
    G = GROUP_W
    tt = _row_tile(T, PREP_ROWS)
    nt = T // tt
    ones_bd = jnp.kron(jnp.eye(RW_HEADS, dtype=F32), jnp.ones((RW_HD, RW_HD), F32)).astype(BF16)
    row = lambda x: x.reshape(1, -1)
    full = lambda shape: pl.BlockSpec(shape, lambda b, i: (0,) * len(shape))
    tile = pl.BlockSpec((tt, G), lambda b, i: (b * nt + i, 0))
    return pl.pallas_call(
        _rw_prep_body,
        grid=(B, nt),
        in_specs=[
            pl.BlockSpec((tt, cols), lambda b, i: (b * nt + i, 0)),
            pl.BlockSpec((1, 1, cols), lambda b, i: (b, 0, 0)),
            full((1, cols)), full((1, G)), full((RW_DECAY_LORA, G)), full((1, G)), full((RW_A_LORA, G)),
            full((RW_GATE_LORA, G)), full((1, G)), full((1, G)), full((G, G)),
        ],
        out_specs=[tile] * 7,
        out_shape=[jax.ShapeDtypeStruct((n, G), F32)] * 7,
        scratch_shapes=[pltpu.VMEM((8, cols), F32)],
        compiler_params=_params(("parallel", "arbitrary"), 40),
        name="rwkv_prep",
    )(zr, shift.reshape(B, 1, cols), row(P['rw_mu']), row(P['rw_w0']), P['rw_w2'].astype(BF16), row(P['rw_a0']),
      P['rw_a2'].astype(BF16), P['rw_g2'].astype(BF16), row(P['rw_kk']), row(P['rw_ka']), ones_bd)


def _rw_scaled(lw, kap, bet, tri):
    cs = _dot_exact_lhs(tri, lw)
    return cs, kap * jnp.exp(cs - lw), bet * jnp.exp(-cs)


def _tri_masks(c):
    row = lax.broadcasted_iota(jnp.int32, (c, c), 0)
    col = lax.broadcasted_iota(jnp.int32, (c, c), 1)
    return row, col


def _chunk_tri(rows, c):
    row, col = _tri_masks(rows)
    return jnp.where((col <= row) & (row // c == col // c), 1.0, 0.0).astype(BF16)


def _rw_ldiag_body(lw_ref, kap_ref, bet_ref, o_ref, *, c):
    rows = lw_ref.shape[0]
    _, kk_all, bt_all = _rw_scaled(lw_ref[...], kap_ref[...], bet_ref[...], _chunk_tri(rows, c))
    srow, scol = _tri_masks(RW_SUB)
    units = [(cc, h) for cc in range(rows // c) for h in range(RW_HEADS)]
    Ls = [_dot_lo(kk_all[cc * c:(cc + 1) * c, h * RW_HD:(h + 1) * RW_HD],
                  bt_all[cc * c:(cc + 1) * c, h * RW_HD:(h + 1) * RW_HD], _NT) for cc, h in units]
    for (cc, h), L in zip(units, Ls):
        for b in range(c // RW_SUB):
            rs = slice(b * RW_SUB, (b + 1) * RW_SUB)
            o_ref[cc * c + b * RW_SUB:cc * c + (b + 1) * RW_SUB, h * RW_SUB:(h + 1) * RW_SUB] = (
                jnp.where(scol < srow, L[rs, rs], 0.0))


def _rw_inv_body(l_ref, t_ref, a_ref, b_ref):
    n = RW_SUB
    nblk = l_ref.shape[0] // n
    for t in range(n):
        a_ref[t] = l_ref[pl.ds(t, nblk, stride=n), :].T
    entry = lambda ref, t, s: ref.at[t, pl.ds(s, RW_HEADS, stride=n), :]
    one = jnp.ones((RW_HEADS, nblk), F32)
    zero = jnp.zeros((RW_HEADS, nblk), F32)
    for t in range(n):
        for s in range(n):
            if s > t:
                entry(b_ref, t, s)[...] = zero
            elif s == t:
                entry(b_ref, t, s)[...] = one
            else:
                acc = entry(a_ref, t, s)[...]
                for j in range(s + 1, t):
                    acc = acc + entry(a_ref, t, j)[...] * entry(b_ref, j, s)[...]
                entry(b_ref, t, s)[...] = -acc
    for t in range(n):
        t_ref[pl.ds(t, nblk, stride=n), :] = b_ref[t].T


def _rw_main_body(r_ref, lw_ref, k_ref, v_ref, kap_ref, bet_ref, g_ref, td_ref, h0_ref, rk_ref, lng_ref, lnb_ref,
                  o_ref, hout_ref, h_ref, *, c):
    ci = pl.program_id(1)
    rows = r_ref.shape[0]
    nb = c // RW_SUB

    @pl.when(ci == 0)
    def _():
        h_ref[...] = h0_ref[0]

    crow, ccol = _tri_masks(c)
    strict = ccol < crow
    incl = ccol <= crow
    lw = lw_ref[...]
    cs, kk_all, bt_all = _rw_scaled(lw, kap_ref[...], bet_ref[...], _chunk_tri(rows, c))
    gi = jnp.exp(-cs)
    gg = jnp.exp(cs)
    kt_all = k_ref[...] * gi
    rt_all = r_ref[...] * gg
    bonus_all = r_ref[...] * k_ref[...] * rk_ref[...]
    hrow = lax.broadcasted_iota(jnp.int32, (RW_HD, RW_HD), 0)
    hcol = lax.broadcasted_iota(jnp.int32, (RW_HD, RW_HD), 1)
    HS = range(RW_HEADS)
    units = [(cc, h) for cc in range(rows // c) for h in HS]
    US = range(len(units))
    rsl = [slice(cc * c, (cc + 1) * c) for cc, _ in units]
    lsl = [slice(h * RW_HD, (h + 1) * RW_HD) for _, h in units]
    Kk = [kk_all[rsl[u], lsl[u]] for u in US]
    Bt = [bt_all[rsl[u], lsl[u]] for u in US]
    Kt = [kt_all[rsl[u], lsl[u]] for u in US]
    Rt = [rt_all[rsl[u], lsl[u]] for u in US]
    vv = [v_ref[rsl[u], lsl[u]] for u in US]
    Lm = [jnp.where(strict, _dot_lo(Kk[u], Bt[u], _NT), 0.0) for u in US]
    A1 = [jnp.where(strict, _dot_lo(Kk[u], Kt[u], _NT), 0.0) for u in US]
    A4 = [jnp.where(incl, _dot_lo(Rt[u], Bt[u], _NT), 0.0) for u in US]
    A3 = [jnp.where(incl, _dot_lo(Rt[u], Kt[u], _NT), 0.0) for u in US]
    X = [jnp.concatenate([Kk[u], _dot_lo(A1[u], vv[u])], axis=1) for u in US]
    zs = [[] for _ in US]
    for b in range(nb):
        rs = slice(b * RW_SUB, (b + 1) * RW_SUB)
        rhs = [X[u][rs] for u in US]
        if b:
            rhs = [rhs[u] - _dot_lo(Lm[u][rs, 0:b * RW_SUB], jnp.concatenate(zs[u], axis=0)) for u in US]
        for u, (cc, h) in enumerate(units):
            tbb = td_ref[cc * c + b * RW_SUB:cc * c + (b + 1) * RW_SUB, h * RW_SUB:(h + 1) * RW_SUB]
            zs[u].append(_dot_lo(tbb, rhs[u]))
    Z = [jnp.concatenate(zs[u], axis=0) if nb > 1 else zs[u][0] for u in US]
    A4Z = [_dot_lo(A4[u], Z[u]) for u in US]
    Rhat = [Rt[u] - A4Z[u][:, :RW_HD] for u in US]
    Yhat = [_dot_lo(A3[u], vv[u]) - A4Z[u][:, RW_HD:] for u in US]
    gC = [gg[(cc + 1) * c - 1:(cc + 1) * c, lsl[u]] for u, (cc, _) in enumerate(units)]
    MN = [_dot_lo(Bt[u] * gC[u], Z[u], _TN) for u in US]
    Mp = [jnp.where(hrow == hcol, gC[u], 0.0) - MN[u][:, :RW_HD] for u in US]
    Np = [_dot_lo(Kt[u] * gC[u], vv[u], _TN) - MN[u][:, RW_HD:] for u in US]
    H = [h_ref[h] for h in HS]
    ys = [None for _ in US]
    for cc in range(rows // c):
        for h in HS:
            u = cc * RW_HEADS + h
            ys[u] = _dot_lo(Rhat[u], H[h]) + Yhat[u]
        H = [_dot_hi(Mp[cc * RW_HEADS + h], H[h]) + Np[cc * RW_HEADS + h] for h in HS]
    for h in HS:
        h_ref[h] = H[h]
    for u in US:
        y = ys[u]
        mu = jnp.mean(y, axis=-1, keepdims=True)
        var = jnp.mean(jnp.square(y - mu), axis=-1, keepdims=True)
        yn = (y - mu) * lax.rsqrt(var + RW_LN_EPS) * lng_ref[:, lsl[u]] + lnb_ref[:, lsl[u]]
        yn = yn + jnp.sum(bonus_all[rsl[u], lsl[u]], axis=-1, keepdims=True) * vv[u]
        o_ref[rsl[u], lsl[u]] = (yn * g_ref[rsl[u], lsl[u]]).astype(o_ref.dtype)

    @pl.when(ci == pl.num_programs(1) - 1)
    def _():
        hout_ref[0] = h_ref[...]


def _rwkv7(zr, shift, S0, B, T, P):
    G = GROUP_W
    n = B * T
    r, lw, k, v, kap, bet, g = _rw_prep(zr, shift, B, T, P)
    c = min(RW_CHUNK, T)
    rows_l = _row_tile(T, c * RW_LDIAG_CHUNKS)
    rows_m = _row_tile(T, c * RW_MAIN_CHUNKS)
    nl, nc = T // rows_l, T // rows_m
    tile_l = pl.BlockSpec((rows_l, G), lambda b, i: (b * nl + i, 0))
    tile = pl.BlockSpec((rows_m, G), lambda b, i: (b * nc + i, 0))
    ld = pl.pallas_call(
        functools.partial(_rw_ldiag_body, c=c),
        grid=(B, nl),
        in_specs=[tile_l] * 3,
        out_specs=pl.BlockSpec((rows_l, RW_HEADS * RW_SUB), lambda b, i: (b * nl + i, 0)),
        out_shape=jax.ShapeDtypeStruct((n, RW_HEADS * RW_SUB), F32),
        compiler_params=_params(("parallel", "parallel"), 32),
        name="rwkv_ldiag",
    )(lw, kap, bet)
    rows_i = V7X_LANES * RW_SUB
    npad = -(-n // rows_i) * rows_i
    inv_spec = pl.BlockSpec((rows_i, RW_HEADS * RW_SUB), lambda i: (i, 0))
    inv_scratch = pltpu.VMEM((RW_SUB, RW_HEADS * RW_SUB, V7X_LANES), F32)
    td = pl.pallas_call(
        _rw_inv_body,
        grid=(npad // rows_i,),
        in_specs=[inv_spec],
        out_specs=inv_spec,
        out_shape=jax.ShapeDtypeStruct((npad, RW_HEADS * RW_SUB), F32),
        scratch_shapes=[inv_scratch, inv_scratch],
        compiler_params=_params(("parallel",), 32),
        name="rwkv_inv",
    )(jnp.pad(ld, ((0, npad - n), (0, 0))))[:n]
    h0 = jnp.swapaxes(S0, -1, -2)
    prow = lambda x: pl.BlockSpec((1, G), lambda b, i: (0, 0))
    st_spec = pl.BlockSpec((1, RW_HEADS, RW_HD, RW_HD), lambda b, i: (b, 0, 0, 0))
    out, hl = pl.pallas_call(
        functools.partial(_rw_main_body, c=c),
        grid=(B, nc),
        in_specs=[tile] * 7 + [pl.BlockSpec((rows_m, RW_HEADS * RW_SUB), lambda b, i: (b * nc + i, 0)), st_spec,
                               prow(0), prow(0), prow(0)],
        out_specs=[tile, st_spec],
        out_shape=[jax.ShapeDtypeStruct((n, G), BF16), jax.ShapeDtypeStruct(S0.shape, F32)],
        scratch_shapes=[pltpu.VMEM((RW_HEADS, RW_HD, RW_HD), F32)],
        compiler_params=_params(("parallel", "arbitrary"), 32),
        name="rwkv_main",
    )(r, lw, k, v, kap, bet, g, td, h0, P['rw_rk'].reshape(1, G), P['rw_ln_g'].reshape(1, G),
      P['rw_ln_b'].reshape(1, G))
    return out, zr.reshape(B, T, -1)[:, -1], jnp.swapaxes(hl, -1, -2)


def _even_mixer(x2, B, T, g, st, P):
    conv_buf, lru_h, k_past, v_past, lf_past = st
    G = GROUP_W
    z_rg, z_qkv, z_og, z_fl = _norm_matmul(x2, g, P['e_w_in'], (2 * G, 3 * G, G, V7X_LANES))
    rnn_out, conv_new, h_last = _lru(z_rg, conv_buf, lru_h, B, T, P)
    qb, kn, kb, v, vb, lf = _fox_prep(z_qkv, z_fl, B, T, P)
    past = k_past.shape[1]
    lf_all = lf.reshape(B, T, V7X_LANES)
    kb_all, vb_all = kb.reshape(B, T, G), vb.reshape(B, T, G)
    if past:
        lf_all = jnp.concatenate([jnp.pad(lf_past, ((0, 0), (0, 0), (0, V7X_LANES - FOX_HEADS))), lf_all], axis=1)
        kb_all = jnp.concatenate([k_past.reshape(B, past, G).astype(BF16), kb_all], axis=1)
        vb_all = jnp.concatenate([v_past.reshape(B, past, G).astype(BF16), vb_all], axis=1)
    tail = ((0, 0), (0, -(past + T) % FOX_K_ROWS), (0, 0))
    ka, vt = _fox_keys(jnp.pad(lf_all, tail), jnp.pad(kb_all, tail), jnp.pad(vb_all, tail))
    fox_out = _fox_attention(qb, ka, vt, z_og, B, T, past)
    heads = lambda t: t.reshape(B, T, FOX_HEADS, FOX_HD)
    return (rnn_out, fox_out, P['e_w_out']), (conv_new, h_last, heads(kn), heads(v), lf.reshape(B, T, V7X_LANES)[..., :FOX_HEADS])


def _odd_mixer(x2, B, T, g, st, lb, P):
    S_hg, shift, S_rw = st
    G = GROUP_W
    z_hg, z_rw = _norm_matmul(x2, g, P['o_w_in'], (4 * G, P['o_w_in'].shape[1] - 4 * G))
    hg_out, S_hg_new = _hgrn2(z_hg, lb, S_hg, B, T, P)
    rw_out, shift_new, S_rw_new = _rwkv7(z_rw, shift, S_rw, B, T, P)
    return (hg_out, rw_out, P['o_w_out']), (S_hg_new, shift_new, S_rw_new)


def _trunk(x, states, W):
    lru_conv, lru_h, fox_k, fox_v, fox_lf, hg_S, rw_shift, rw_S = states
    B, T, D = x.shape
    depth = W['norm_g'].shape[0]
    sm = jax.nn.softmax(W['hg_lb_logits'], axis=0)
    lower_bounds = jnp.cumsum(sm, axis=0) - sm[0]
    x2 = x.reshape(B * T, D)
    even_new, odd_new = [], []
    for layer in range(depth):
        g = W['norm_g'][layer]
        x2 = _ffn(x2, g[0], W['ffn_w_in'][layer][0], W['ffn_w_out'][layer][0])
        if layer % 2 == 0:
            e = layer // 2
            P = {n: W[n][e] for n in ('e_w_in', 'e_w_out', 'lru_conv_w', 'lru_conv_b', 'lru_wa', 'lru_ba', 'lru_wx',
                                      'lru_bx', 'lru_lambda', 'fox_q_gain', 'fox_k_gain', 'fox_f_bias')}
            mix, new = _even_mixer(x2, B, T, g[1], (lru_conv[e], lru_h[e], fox_k[e], fox_v[e], fox_lf[e]), P)
            even_new.append(new)
        else:
            o = layer // 2
            P = {n: W[n][o] for n in ('o_w_in', 'o_w_out', 'hg_norm_g', 'rw_mu', 'rw_w0', 'rw_w2', 'rw_a0', 'rw_a2',
                                      'rw_g2', 'rw_kk', 'rw_ka', 'rw_rk', 'rw_ln_g', 'rw_ln_b')}
            mix, new = _odd_mixer(x2, B, T, g[1], (hg_S[o], rw_shift[o], rw_S[o]), lower_bounds[layer], P)
            odd_new.append(new)
        x2 = _ffn(x2, g[2], W['ffn_w_in'][layer][1], W['ffn_w_out'][layer][1], mix)
    ev = [jnp.stack([n[j] for n in even_new]) for j in range(5)]
    od = [jnp.stack([n[j] for n in odd_new]) for j in range(3)]
    return x2.reshape(B, T, D), (ev[0], ev[1], ev[2], ev[3], ev[4], od[0], od[1], od[2])


def kernel(x_prompt, x_sample, state_lru_conv, state_lru_h, cache_fox_k, cache_fox_v, cache_fox_logf,
           state_hgrn_S, state_rwkv_shift, state_rwkv_S, norm_g, ffn_w_in, ffn_w_out, e_w_in, e_w_out,
           lru_conv_w, lru_conv_b, lru_wa, lru_ba, lru_wx, lru_bx, lru_lambda, fox_q_gain, fox_k_gain,
           fox_f_bias, o_w_in, o_w_out, hg_lb_logits, hg_norm_g, rw_mu, rw_w0, rw_w2, rw_a0, rw_a2, rw_g2,
           rw_kk, rw_ka, rw_rk, rw_ln_g, rw_ln_b):
    n_even, n_odd = e_w_in.shape[0], o_w_in.shape[0]
    W = dict(norm_g=norm_g, ffn_w_in=_ffn_w_in_tiles(ffn_w_in), ffn_w_out=ffn_w_out.astype(BF16),
             e_w_in=_pad_cols(e_w_in.astype(BF16)), e_w_out=e_w_out.astype(BF16),
             lru_conv_w=lru_conv_w, lru_conv_b=lru_conv_b, lru_wa=lru_wa, lru_ba=lru_ba, lru_wx=lru_wx,
             lru_bx=lru_bx, lru_lambda=lru_lambda, fox_q_gain=fox_q_gain, fox_k_gain=fox_k_gain,
             fox_f_bias=fox_f_bias, o_w_in=o_w_in.astype(BF16), o_w_out=o_w_out.astype(BF16),
             hg_lb_logits=hg_lb_logits, hg_norm_g=hg_norm_g, rw_mu=rw_mu, rw_w0=rw_w0, rw_w2=rw_w2, rw_a0=rw_a0,
             rw_a2=rw_a2, rw_g2=rw_g2, rw_kk=rw_kk, rw_ka=rw_ka, rw_rk=rw_rk, rw_ln_g=rw_ln_g, rw_ln_b=rw_ln_b)
    nb = x_prompt.shape[0]
    dt = x_prompt.dtype
    prompt_states = (jnp.zeros((n_even, nb, CONV_W - 1, GROUP_W), dt),
                     jnp.zeros((n_even, nb, GROUP_W), dt),
                     jnp.zeros((n_even, nb, 0, FOX_HEADS, FOX_HD), dt),
                     jnp.zeros((n_even, nb, 0, FOX_HEADS, FOX_HD), dt),
                     jnp.zeros((n_even, nb, 0, FOX_HEADS), dt),
                     jnp.zeros((n_odd, nb, HG_HEADS, GROUP_W // HG_HEADS, GROUP_W // HG_HEADS), dt),
                     jnp.zeros((n_odd, nb, rw_mu.shape[1]), dt),
                     jnp.zeros((n_odd, nb, RW_HEADS, RW_HD, RW_HD), dt))
    sample_states = (state_lru_conv, state_lru_h, cache_fox_k, cache_fox_v, cache_fox_logf,
                     state_hgrn_S, state_rwkv_shift, state_rwkv_S)
    y_prompt, p_new = _trunk(x_prompt, prompt_states, W)
    y_sample, s_new = _trunk(x_sample, sample_states, W)
    lru_conv_p, lru_h_p, fox_k_p, fox_v_p, fox_logf_p, hgrn_S_p, rwkv_shift_p, rwkv_S_p = p_new
    lru_conv_s, lru_h_s, fox_k_s, fox_v_s, fox_logf_s, hgrn_S_s, rwkv_shift_s, rwkv_S_s = s_new
    return (y_prompt, y_sample, lru_conv_p, lru_conv_s, lru_h_p, lru_h_s, fox_k_p, fox_k_s, fox_v_p, fox_v_s,
            fox_logf_p, fox_logf_s, hgrn_S_p, hgrn_S_s, rwkv_shift_p, rwkv_shift_s, rwkv_S_p, rwkv_S_s)
```

```python
import functools

import jax
import jax.numpy as jnp
from jax import lax
from jax.experimental import pallas as pl
from jax.experimental.pallas import tpu as pltpu

F32 = jnp.float32
BF16 = jnp.bfloat16

NORM_EPS = 1e-6
GROUP_W = 512
CONV_W = 4
LRU_C = 8.0
FOX_HEADS = 8
FOX_HD = 64
HG_HEADS = 4
RW_HEADS = 8
RW_HD = 64
RW_DECAY_LORA = 64
RW_A_LORA = 64
RW_GATE_LORA = 128
RW_LN_EPS = 64e-5

V7X_LANES = 128
FFN_COL_TILE = 1408
PREP_ROWS = 512
FFN_ROW_TILE = 1024
FFN_VMEM_MIB = 60


def _row_tile(n, want):
    t = min(n, want)
    while n % t:
        t //= 2
    return t


def _params(sem, vmem_mib):
    return pltpu.CompilerParams(dimension_semantics=sem, vmem_limit_bytes=vmem_mib << 20)


def _pad_cols(w):
    pad = -w.shape[-1] % V7X_LANES
    return jnp.pad(w, [(0, 0)] * (w.ndim - 1) + [(0, pad)])


def _rms(x, g):
    return x * lax.rsqrt(jnp.mean(x * x, axis=-1, keepdims=True) + NORM_EPS) * g


def _ffn_body(*refs, mixed):
    if mixed:
        x_ref, a_ref, b_ref, wa_ref, wb_ref, g_ref, wi_ref, wo_ref, o_ref, h_ref, acc_ref = refs
    else:
        x_ref, g_ref, wi_ref, wo_ref, o_ref, h_ref, acc_ref = refs
    j = pl.program_id(1)

    @pl.when(j == 0)
    def _():
        x = x_ref[...]
        if mixed:
            x = x + jnp.dot(a_ref[...], wa_ref[...], preferred_element_type=F32)
            x = x + jnp.dot(b_ref[...], wb_ref[...], preferred_element_type=F32)
        h_ref[...] = _rms(x, g_ref[...]).astype(BF16)
        acc_ref[...] = 2.0 * x

    tf = wo_ref.shape[0]
    gu = jnp.dot(h_ref[...], wi_ref[...], preferred_element_type=F32)
    gate, up = gu[:, :tf], gu[:, tf:]
    act = (gate * jax.nn.sigmoid(gate) * up).astype(BF16)
    acc_ref[...] += jnp.dot(act, wo_ref[...], preferred_element_type=F32)

    @pl.when(j == pl.num_programs(1) - 1)
    def _():
        o_ref[...] = 0.5 * acc_ref[...]


def _cast_body(x_ref, o_ref):
    o_ref[...] = x_ref[...].astype(o_ref.dtype)


def _ffn_w_in_tiles(w_in):
    *lead, d, f2 = w_in.shape
    tf = FFN_COL_TILE
    nf = f2 // 2 // tf
    w = w_in.reshape(-1, d, f2)
    out = pl.pallas_call(
        _cast_body,
        grid=(w.shape[0], nf, 2),
        in_specs=[pl.BlockSpec((1, d, tf), lambda i, j, gu: (i, 0, gu * nf + j))],
        out_specs=pl.BlockSpec((1, d, tf), lambda i, j, gu: (i, 0, 2 * j + gu)),
        out_shape=jax.ShapeDtypeStruct(w.shape, BF16),
        compiler_params=_params(("parallel", "parallel", "parallel"), 32),
        name="ffn_weight_tiles",
    )(w)
    return out.reshape(*lead, d, f2)


def _ffn(x, g, w_in, w_out, mix=None):
    n, d = x.shape
    f = w_out.shape[0]
    tm = _row_tile(n, FFN_ROW_TILE)
    tf = FFN_COL_TILE
    nf = f // tf
    row_spec = lambda w: pl.BlockSpec((tm, w), lambda i, j: (i, 0))
    args, specs = [x], [row_spec(d)]
    if mix is not None:
        a, b, w = mix
        ga, gb = a.shape[1], b.shape[1]
        args += [a, b, w[:ga], w[ga:]]
        specs += [row_spec(ga), row_spec(gb), pl.BlockSpec((ga, d), lambda i, j: (0, 0)),
                  pl.BlockSpec((gb, d), lambda i, j: (0, 0))]
    args += [g.reshape(1, d), w_in, w_out]
    specs += [pl.BlockSpec((1, d), lambda i, j: (0, 0)), pl.BlockSpec((d, 2 * tf), lambda i, j: (0, j)),
              pl.BlockSpec((tf, d), lambda i, j: (j, 0))]
    return pl.pallas_call(
        functools.partial(_ffn_body, mixed=mix is not None),
        grid=(n // tm, nf),
        in_specs=specs,
        out_specs=row_spec(d),
        out_shape=jax.ShapeDtypeStruct((n, d), F32),
        scratch_shapes=[pltpu.VMEM((tm, d), BF16), pltpu.VMEM((tm, d), F32)],
        compiler_params=_params(("parallel", "arbitrary"), FFN_VMEM_MIB),
        name="ffn",
    )(*args)


def _norm_matmul_body(x_ref, g_ref, w_ref, *o_refs):
    h = _rms(x_ref[...], g_ref[...]).astype(BF16)
    z = jnp.dot(h, w_ref[...], preferred_element_type=F32)
    start = 0
    for o_ref in o_refs:
        width = o_ref.shape[1]
        o_ref[...] = z[:, start:start + width]
        start += width


def _norm_matmul(x, g, w, widths):
    n, d = x.shape
    c = w.shape[1]
    assert sum(widths) == c and all(wd % V7X_LANES == 0 for wd in widths)
    tm = _row_tile(n, 512)
    return pl.pallas_call(
        _norm_matmul_body,
        grid=(n // tm,),
        in_specs=[
            pl.BlockSpec((tm, d), lambda i: (i, 0)),
            pl.BlockSpec((1, d), lambda i: (0, 0)),
            pl.BlockSpec((d, c), lambda i: (0, 0)),
        ],
        out_specs=[pl.BlockSpec((tm, wd), lambda i: (i, 0)) for wd in widths],
        out_shape=[jax.ShapeDtypeStruct((n, wd), F32) for wd in widths],
        compiler_params=_params(("parallel",), 48),
        name="norm_matmul",
    )(x, g.reshape(1, d), w)


LRU_ROWS = 256
CONV_PAD = 8


def _expm1(x):
    series = x * (1.0 + x * (1 / 2 + x * (1 / 6 + x * (1 / 24 + x * (1 / 120 + x * (1 / 720 + x * (1 / 5040 + x * (1 / 40320))))))))
    return jnp.where(jnp.abs(x) < 0.25, series, jnp.exp(x) - 1.0)


def _shift_rows(x, s, fill):
    row = lax.broadcasted_iota(jnp.int32, x.shape, 0)
    return jnp.where(row >= s, pltpu.roll(x, s, axis=0), fill)


def _lru_body(z_ref, buf_ref, h0_ref, cw_ref, cb_ref, wa_ref, ba_ref, wx_ref, bx_ref, lam_ref,
              o_ref, bufo_ref, ho_ref, x_ref, hc_ref):
    G = GROUP_W
    tt = z_ref.shape[0]

    @pl.when(pl.program_id(1) == 0)
    def _():
        x_ref[0:CONV_PAD, :] = buf_ref[0]
        hc_ref[...] = jnp.broadcast_to(h0_ref[0], hc_ref.shape)

    x_ref[CONV_PAD:CONV_PAD + tt, :] = z_ref[:, 0:G]
    xc = cb_ref[...]
    for j in range(CONV_W):
        lo = CONV_PAD - (CONV_W - 1) + j
        xc = xc + x_ref[lo:lo + tt, :] * cw_ref[j:j + 1, :]
    hist = x_ref[tt:tt + CONV_PAD, :]
    x_ref[0:CONV_PAD, :] = hist
    bufo_ref[0] = hist

    xb = xc.astype(BF16)
    r = jax.nn.sigmoid(jnp.dot(xb, wa_ref[...], preferred_element_type=F32) + ba_ref[...])
    ig = jax.nn.sigmoid(jnp.dot(xb, wx_ref[...], preferred_element_type=F32) + bx_ref[...])
    log_a = (-LRU_C * _softplus(-lam_ref[...])) * r
    a = jnp.exp(log_a)
    b = jnp.sqrt(-_expm1(2.0 * log_a)) * (ig * xc)
    s = 1
    while s < tt:
        if s % 8:
            b = a * _shift_rows(b, s, 0.0) + b
            a = a * _shift_rows(a, s, 1.0)
        else:
            b = jnp.concatenate([b[:s], a[s:] * b[:tt - s] + b[s:]], axis=0)
            a = jnp.concatenate([a[:s], a[s:] * a[:tt - s]], axis=0)
        s *= 2
    h = a * hc_ref[0:1, :] + b
    hc_ref[...] = jnp.broadcast_to(h[tt - 1:tt, :], hc_ref.shape)
    ho_ref[0] = h[tt - 1:tt, :]
    o_ref[...] = (jax.nn.gelu(z_ref[:, G:2 * G]) * h).astype(o_ref.dtype)


def _block_diag_dense(w):
    nb, bs, _ = w.shape
    eye = jnp.eye(nb, dtype=w.dtype)
    return (eye[:, None, :, None] * w[:, :, None, :]).reshape(nb * bs, nb * bs)


def _lru(z_rg, conv_buf, h0, B, T, P):
    G = GROUP_W
    n = B * T
    tt = _row_tile(T, LRU_ROWS)
    nt = T // tt
    buf = jnp.pad(conv_buf, ((0, 0), (CONV_PAD - (CONV_W - 1), 0), (0, 0)))
    cw = jnp.pad(P['lru_conv_w'], ((0, CONV_PAD - CONV_W), (0, 0)))
    row = lambda x: x.reshape(1, G)
    full = lambda shape: pl.BlockSpec(shape, lambda b, i: (0,) * len(shape))
    out, bufo, ho = pl.pallas_call(
        _lru_body,
        grid=(B, nt),
        in_specs=[
            pl.BlockSpec((tt, 2 * G), lambda b, i: (b * nt + i, 0)),
            pl.BlockSpec((1, CONV_PAD, G), lambda b, i: (b, 0, 0)),
            pl.BlockSpec((1, 1, G), lambda b, i: (b, 0, 0)),
            full((CONV_PAD, G)), full((1, G)), full((G, G)), full((1, G)), full((G, G)), full((1, G)), full((1, G)),
        ],
        out_specs=[
            pl.BlockSpec((tt, G), lambda b, i: (b * nt + i, 0)),
            pl.BlockSpec((1, CONV_PAD, G), lambda b, i: (b, 0, 0)),
            pl.BlockSpec((1, 1, G), lambda b, i: (b, 0, 0)),
        ],
        out_shape=[jax.ShapeDtypeStruct((n, G), BF16), jax.ShapeDtypeStruct((B, CONV_PAD, G), F32),
                   jax.ShapeDtypeStruct((B, 1, G), F32)],
        scratch_shapes=[pltpu.VMEM((tt + CONV_PAD, G), F32), pltpu.VMEM((8, G), F32)],
        compiler_params=_params(("parallel", "arbitrary"), 32),
        name="lru",
    )(z_rg, buf, h0.reshape(B, 1, G), cw, row(P['lru_conv_b']), _block_diag_dense(P['lru_wa']).astype(BF16),
      row(P['lru_ba']), _block_diag_dense(P['lru_wx']).astype(BF16), row(P['lru_bx']), row(P['lru_lambda']))
    return out, bufo[:, CONV_PAD - (CONV_W - 1):], ho.reshape(B, G)


FOX_Q_COLS = 2048
FOX_K_ROWS = 512
FOX_F_SPLIT = 3
FOX_NEG = -1e30
LOG2E = 1.4426950408889634
HEAD_PAIRS = FOX_HEADS // 2
PAIR_W = 2 * FOX_HD
FOX_CUMSUM_ROWS = 256
FOX_KEY_TILE = 512


def _fox_prep_body(z_ref, fl_ref, qg_ref, kg_ref, fb_ref, ones_ref, q_ref, k_ref, kb_ref, v_ref, vb_ref, lf_ref,
                   *, q_transposed):
    G = GROUP_W
    q, k, v = z_ref[:, 0:G], z_ref[:, G:2 * G], z_ref[:, 2 * G:3 * G]
    inv = 1.0 / FOX_HD
    qn = q * lax.rsqrt(_dot_exact_rhs(q * q, ones_ref[...]) * inv + NORM_EPS) * qg_ref[...]
    kn = k * lax.rsqrt(_dot_exact_rhs(k * k, ones_ref[...]) * inv + NORM_EPS) * kg_ref[...]
    qs = qn * (LOG2E * FOX_HD ** -0.5)
    if q_transposed:
        for p in range(HEAD_PAIRS):
            q_ref[0, p] = qs[:, p * PAIR_W:(p + 1) * PAIR_W].T.astype(BF16)
    else:
        q_ref[...] = qs.astype(BF16)
    tt = z_ref.shape[0]
    for h in range(FOX_HEADS):
        k_ref[pl.ds(h, tt, stride=FOX_HEADS), :] = kn[:, h * FOX_HD:(h + 1) * FOX_HD]
        v_ref[pl.ds(h, tt, stride=FOX_HEADS), :] = v[:, h * FOX_HD:(h + 1) * FOX_HD]
    kb_ref[...] = kn.astype(BF16)
    vb_ref[...] = v.astype(BF16)
    x = fl_ref[...] + fb_ref[...]
    lf_ref[...] = -_softplus(-x)


def _fox_prep(z_qkv, z_fl, B, T, P):
    n = z_qkv.shape[0]
    G = GROUP_W
    tt = _row_tile(T, PREP_ROWS)
    nt = T // tt
    q_transposed = tt % V7X_LANES == 0
    ones_bd = jnp.kron(jnp.eye(FOX_HEADS, dtype=F32), jnp.ones((FOX_HD, FOX_HD), F32)).astype(BF16)
    fb = jnp.pad(P['fox_f_bias'], (0, V7X_LANES - FOX_HEADS)).reshape(1, V7X_LANES)
    tile = lambda w: pl.BlockSpec((tt, w), lambda b, i: (b * nt + i, 0))
    full = lambda shape: pl.BlockSpec(shape, lambda b, i: (0,) * len(shape))
    if q_transposed:
        q_spec = pl.BlockSpec((1, HEAD_PAIRS, PAIR_W, tt), lambda b, i: (b, 0, 0, i))
        q_shape = jax.ShapeDtypeStruct((B, HEAD_PAIRS, PAIR_W, T), BF16)
    else:
        q_spec, q_shape = tile(G), jax.ShapeDtypeStruct((n, G), BF16)
    state_spec = pl.BlockSpec((tt * FOX_HEADS, FOX_HD), lambda b, i: (b * nt + i, 0))
    state_shape = jax.ShapeDtypeStruct((n * FOX_HEADS, FOX_HD), F32)
    return pl.pallas_call(
        functools.partial(_fox_prep_body, q_transposed=q_transposed),
        grid=(B, nt),
        in_specs=[tile(3 * G), tile(V7X_LANES), full((1, G)), full((1, G)), full((1, V7X_LANES)), full((G, G))],
        out_specs=[q_spec, state_spec, tile(G), state_spec, tile(G), tile(V7X_LANES)],
        out_shape=[q_shape, state_shape, jax.ShapeDtypeStruct((n, G), BF16), state_shape,
                   jax.ShapeDtypeStruct((n, G), BF16), jax.ShapeDtypeStruct((n, V7X_LANES), F32)],
        compiler_params=_params(("parallel", "parallel"), 32),
        name="fox_prep",
    )(z_qkv, z_fl, jnp.tile(P['fox_q_gain'], FOX_HEADS).reshape(1, G),
      jnp.tile(P['fox_k_gain'], FOX_HEADS).reshape(1, G), fb, ones_bd)


def _fox_keys_body(lf_ref, kb_ref, vb_ref, ka_ref, vt_ref, c_ref):
    tt = lf_ref.shape[1]

    @pl.when(pl.program_id(1) == 0)
    def _():
        c_ref[...] = jnp.zeros_like(c_ref)

    grp = min(tt, FOX_CUMSUM_ROWS)
    carry, fs = c_ref[0:1, :], []
    for i in range(0, tt, grp):
        fs.append(_chunk_cumsum(lf_ref[0, i:i + grp, :], grp) + carry)
        carry = fs[-1][grp - 1:grp, :]
    f = jnp.concatenate(fs, axis=0) if len(fs) > 1 else fs[0]
    c_ref[...] = jnp.broadcast_to(carry, c_ref.shape)
    parts = _split3(f * LOG2E)
    srow = lax.broadcasted_iota(jnp.int32, (V7X_LANES, FOX_HD), 0)
    scol = lax.broadcasted_iota(jnp.int32, (V7X_LANES, FOX_HD), 1)
    for h in range(FOX_HEADS):
        aug = jnp.zeros((tt, FOX_HD), F32)
        for t, part in enumerate(parts):
            sel = jnp.where((srow == h) & (scol == t), 1.0, 0.0).astype(BF16)
            aug = aug + jnp.dot(part, sel, preferred_element_type=F32)
        ka_ref[0, h] = jnp.concatenate([kb_ref[0, :, h * FOX_HD:(h + 1) * FOX_HD], aug.astype(BF16)], axis=1)
    for p in range(HEAD_PAIRS):
        vt_ref[0, p] = vb_ref[0, :, p * PAIR_W:(p + 1) * PAIR_W].astype(F32).T.astype(BF16)


def _fox_keys(lf_all, kb_all, vb_all):
    B, tk_all, L = lf_all.shape
    G = GROUP_W
    tt = FOX_KEY_TILE
    return pl.pallas_call(
        _fox_keys_body,
        grid=(B, tk_all // tt),
        in_specs=[pl.BlockSpec((1, tt, L), lambda b, i: (b, i, 0)),
                  pl.BlockSpec((1, tt, G), lambda b, i: (b, i, 0)),
                  pl.BlockSpec((1, tt, G), lambda b, i: (b, i, 0))],
        out_specs=[pl.BlockSpec((1, FOX_HEADS, tt, 2 * FOX_HD), lambda b, i: (b, 0, i, 0)),
                   pl.BlockSpec((1, HEAD_PAIRS, PAIR_W, tt), lambda b, i: (b, 0, 0, i))],
        out_shape=[jax.ShapeDtypeStruct((B, FOX_HEADS, tk_all, 2 * FOX_HD), BF16),
                   jax.ShapeDtypeStruct((B, HEAD_PAIRS, PAIR_W, tk_all), BF16)],
        scratch_shapes=[pltpu.VMEM((8, L), F32)],
        compiler_params=_params(("parallel", "arbitrary"), 32),
        name="fox_keys",
    )(lf_all, kb_all, vb_all)


def _fox_attn_body(qt_ref, ka_ref, vt_ref, og_ref, o_ref, acc_ref, m_ref, l_ref, *, past, tk, t_real):
    qi = pl.program_id(2)
    tq = qt_ref.shape[3]
    t_out = o_ref.shape[0]
    first_q = past + qi * tq
    last_q = past + jnp.minimum(qi * tq + tq, t_real) - 1
    n_full = (first_q + 1) // tk
    n_all = last_q // tk + 1
    drow = lax.broadcasted_iota(jnp.int32, (FOX_HD, tq), 0)
    minus = jnp.where(drow < FOX_F_SPLIT, -1.0, 0.0).astype(BF16)
    rhs = [jnp.concatenate([qt_ref[0, 0, h * FOX_HD:(h + 1) * FOX_HD, :], minus], axis=0) for h in range(2)]
    acc_ref[...] = jnp.zeros_like(acc_ref)

    def update(ki, m_prev, l_prev, masked, q0=0):
        ks = pl.multiple_of(ki * tk, tk)
        s = [jnp.dot(ka_ref[0, h, pl.ds(ks, tk), :], rhs[h][:, q0:], preferred_element_type=F32) for h in range(2)]
        if masked:
            krow = lax.broadcasted_iota(jnp.int32, (tk, tq - q0), 0)
            qcol = lax.broadcasted_iota(jnp.int32, (tk, tq - q0), 1)
            vis = ks + krow <= first_q + q0 + qcol
            s = [jnp.where(vis, s[h], FOX_NEG) for h in range(2)]
        m_new = [jnp.maximum(m_prev[h], jnp.max(s[h], axis=0, keepdims=True)) for h in range(2)]
        alpha = [jnp.exp2(m_prev[h] - m_new[h]) for h in range(2)]
        p = [jnp.exp2(s[h] - m_new[h]) for h in range(2)]
        l_new = [alpha[h] * l_prev[h] + jnp.sum(p[h], axis=0, keepdims=True) for h in range(2)]
        vt = [vt_ref[0, 0, h * FOX_HD:(h + 1) * FOX_HD, pl.ds(ks, tk)] for h in range(2)]
        pv = [jnp.dot(vt[h], p[h].astype(BF16), preferred_element_type=F32) for h in range(2)]
        for h in range(2):
            acc_ref[h, :, q0:] = alpha[h] * acc_ref[h, :, q0:] + pv[h]
        return m_new, l_new

    def carried(ki, c, masked):
        return update(ki, c[0], c[1], masked)

    c = ([jnp.full((1, tq), FOX_NEG, F32)] * 2, [jnp.zeros((1, tq), F32)] * 2)
    c = lax.fori_loop(0, n_full, lambda ki, c: carried(ki, c, False), c)
    if tq > tk and tq % tk == 0 and past % tq == 0 and t_real % tq == 0:
        m, l = carried(n_full, c, True)
        for h in range(2):
            m_ref[h], l_ref[h] = m[h], l[h]
        for j in range(1, tq // tk):
            q0 = j * tk
            m, l = update(n_full + j, [m_ref[h, :, q0:] for h in range(2)], [l_ref[h, :, q0:] for h in range(2)],
                          True, q0=q0)
            for h in range(2):
                m_ref[h, :, q0:], l_ref[h, :, q0:] = m[h], l[h]
    else:
        _, l = lax.fori_loop(n_full, n_all, lambda ki, c: carried(ki, c, True), c)
        for h in range(2):
            l_ref[h] = l[h]
    o_t = jnp.concatenate([acc_ref[h] / l_ref[h] for h in range(2)], axis=0)
    o_ref[...] = (o_t.T[:t_out] * jax.nn.sigmoid(og_ref[...])).astype(o_ref.dtype)


def _fox_attention(q, ka, vt, z_og, B, T, past):
    G = GROUP_W
    pw = PAIR_W
    tq = max(_row_tile(T, FOX_Q_COLS), V7X_LANES)
    tqp = -(-T // tq) * tq
    nq = tqp // tq
    t_out = min(tq, T)
    tk = FOX_K_ROWS
    tkp = ka.shape[2]
    if q.ndim == 2:
        q = q.reshape(B, T, HEAD_PAIRS, pw).transpose(0, 2, 3, 1)
    qt = jnp.pad(q, ((0, 0), (0, 0), (0, 0), (0, tqp - T)))
    return pl.pallas_call(
        functools.partial(_fox_attn_body, past=past, tk=tk, t_real=T),
        grid=(B, HEAD_PAIRS, nq),
        in_specs=[
            pl.BlockSpec((1, 1, pw, tq), lambda b, p, i: (b, p, 0, i)),
            pl.BlockSpec((1, 2, tkp, 2 * FOX_HD), lambda b, p, i: (b, p, 0, 0)),
            pl.BlockSpec((1, 1, pw, tkp), lambda b, p, i: (b, p, 0, 0)),
            pl.BlockSpec((t_out, pw), lambda b, p, i: (b * nq + i, p)),
        ],
        out_specs=pl.BlockSpec((t_out, pw), lambda b, p, i: (b * nq + i, p)),
        out_shape=jax.ShapeDtypeStruct((B * T, G), BF16),
        scratch_shapes=[pltpu.VMEM((2, FOX_HD, tq), F32), pltpu.VMEM((2, 1, tq), F32), pltpu.VMEM((2, 1, tq), F32)],
        compiler_params=_params(("parallel", "parallel", "arbitrary"), 40),
        name="fox_attn",
    )(qt, ka, vt, z_og)


HG_CHUNK = 64
HG_STEP_CHUNKS = 8


def _hgrn_body(z_ref, lb_ref, s0_ref, ng_ref, o_ref, so_ref, st_ref, *, c):
    G = GROUP_W
    rows = z_ref.shape[0]
    nch = rows // c
    dk = G // HG_HEADS

    @pl.when(pl.program_id(1) == 0)
    def _():
        st_ref[...] = s0_ref[0]

    lb = lb_ref[...]
    f = lb + (1.0 - lb) * jax.nn.sigmoid(z_ref[:, G:2 * G])
    kx = 1.0 - f
    crow, ccol = _tri_masks(c)
    incl = ccol <= crow
    gs = _chunk_cumsum(jnp.log(f), c)
    qg_all = z_ref[:, 0:G] * jnp.exp(gs)
    kg_all = kx * jnp.exp(-gs)
    HS = range(HG_HEADS)
    units = [(cc, h) for cc in range(nch) for h in HS]
    US = range(len(units))
    rsl = [slice(cc * c, (cc + 1) * c) for cc, _ in units]
    lsl = [slice(h * dk, (h + 1) * dk) for _, h in units]
    g_last = [gs[(cc + 1) * c - 1:(cc + 1) * c, lsl[u]] for u, (cc, _) in enumerate(units)]
    vv = [z_ref[rsl[u], 2 * G + h * dk:2 * G + (h + 1) * dk] for u, (_, h) in enumerate(units)]
    A = [jnp.where(incl, _dot_lo(qg_all[rsl[u], lsl[u]], kg_all[rsl[u], lsl[u]], _NT), 0.0) for u in US]
    av = [_dot_lo(A[u], vv[u]) for u in US]
    kd = [kx[rsl[u], lsl[u]] * jnp.exp(g_last[u] - gs[rsl[u], lsl[u]]) for u in US]
    upd = [_dot_lo(vv[u], kd[u], _TN) for u in US]
    st = [st_ref[h] for h in HS]
    o = [None for _ in US]
    for cc in range(nch):
        for h in HS:
            u = cc * HG_HEADS + h
            o[u] = _dot_lo(qg_all[rsl[u], lsl[u]], st[h], _NT) + av[u]
        st = [st[h] * jnp.exp(g_last[cc * HG_HEADS + h]) + upd[cc * HG_HEADS + h] for h in HS]
    for h in HS:
        st_ref[h] = st[h]
    for u, (_, h) in enumerate(units):
        hg = z_ref[rsl[u], 3 * G + h * dk:3 * G + (h + 1) * dk]
        o_ref[rsl[u], lsl[u]] = (_rms(o[u], ng_ref[:, lsl[u]]) * (hg * jax.nn.sigmoid(hg))).astype(o_ref.dtype)

    @pl.when(pl.program_id(1) == pl.num_programs(1) - 1)
    def _():
        so_ref[0] = st_ref[...]


def _hgrn2(z_hg, lb, S0, B, T, P):
    G = GROUP_W
    c = min(HG_CHUNK, T)
    rows = _row_tile(T, c * HG_STEP_CHUNKS)
    nc = T // rows
    dk = G // HG_HEADS
    st_spec = pl.BlockSpec((1, HG_HEADS, dk, dk), lambda b, i: (b, 0, 0, 0))
    out, so = pl.pallas_call(
        functools.partial(_hgrn_body, c=c),
        grid=(B, nc),
        in_specs=[pl.BlockSpec((rows, 4 * G), lambda b, i: (b * nc + i, 0)),
                  pl.BlockSpec((1, G), lambda b, i: (0, 0)), st_spec, pl.BlockSpec((1, G), lambda b, i: (0, 0))],
        out_specs=[pl.BlockSpec((rows, G), lambda b, i: (b * nc + i, 0)), st_spec],
        out_shape=[jax.ShapeDtypeStruct((B * T, G), BF16), jax.ShapeDtypeStruct(S0.shape, F32)],
        scratch_shapes=[pltpu.VMEM((HG_HEADS, dk, dk), F32)],
        compiler_params=_params(("parallel", "arbitrary"), 32),
        name="hgrn2",
    )(z_hg, lb.reshape(1, G), jnp.swapaxes(S0, -1, -2), P['hg_norm_g'].reshape(1, G))
    return out, jnp.swapaxes(so, -1, -2)


RW_CHUNK = 64
RW_SUB = 16
RW_LDIAG_CHUNKS = 4
RW_MAIN_CHUNKS = 2

_NT = (((1,), (1,)), ((), ()))
_TN = (((0,), (0,)), ((), ()))
_NN = (((1,), (0,)), ((), ()))


def _split3(x):
    h1 = x.astype(BF16)
    r1 = x - h1.astype(F32)
    h2 = r1.astype(BF16)
    h3 = (r1 - h2.astype(F32)).astype(BF16)
    return h1, h2, h3


def _dot_lo(a, b, dims=_NN):
    return lax.dot_general(a.astype(BF16), b.astype(BF16), dims, preferred_element_type=F32)


def _dot_hi(a, b, dims=_NN):
    ah = a.astype(BF16)
    al = (a - ah.astype(F32)).astype(BF16)
    bh = b.astype(BF16)
    bl = (b - bh.astype(F32)).astype(BF16)
    d = functools.partial(lax.dot_general, dimension_numbers=dims, preferred_element_type=F32)
    return d(ah, bh) + (d(al, bh) + d(ah, bl))


def _dot_exact_rhs(a, b):
    h1, h2, h3 = _split3(a)
    d = functools.partial(jnp.dot, preferred_element_type=F32)
    return d(h1, b) + (d(h2, b) + d(h3, b))


def _dot_exact_lhs(a, b):
    h1, h2, h3 = _split3(b)
    d = functools.partial(jnp.dot, preferred_element_type=F32)
    return d(a, h1) + (d(a, h2) + d(a, h3))


def _softplus(x):
    return jnp.maximum(x, 0.0) + jnp.log1p(jnp.exp(-jnp.abs(x)))


def _rw_prep_body(z_ref, shift_ref, mu_ref, w0_ref, w2_ref, a0_ref, a2_ref, g2_ref, kk_ref, ka_ref, ones_ref,
                  r_ref, lw_ref, k_ref, v_ref, kap_ref, bet_ref, g_ref, prev_ref):
    G = GROUP_W

    @pl.when(pl.program_id(1) == 0)
    def _():
        prev_ref[0:1, :] = shift_ref[0]

    z = z_ref[...]
    tt = z.shape[0]
    row = lax.broadcasted_iota(jnp.int32, z.shape, 0)
    shifted = jnp.where(row == 0, prev_ref[0:1, :], pltpu.roll(z, 1, axis=0))
    prev_ref[0:1, :] = z[tt - 1:tt, :]
    zm = z + (shifted - z) * mu_ref[...]
    r, k, v = zm[:, 0:G], zm[:, G:2 * G], zm[:, 2 * G:3 * G]
    o = 3 * G
    wd = zm[:, o:o + RW_DECAY_LORA]
    ad = zm[:, o + RW_DECAY_LORA:o + RW_DECAY_LORA + RW_A_LORA]
    gd = zm[:, o + RW_DECAY_LORA + RW_A_LORA:]
    w = -_softplus(-(w0_ref[...] + _dot_lo(jnp.tanh(wd), w2_ref[...]))) - 0.5
    a = jax.nn.sigmoid(a0_ref[...] + _dot_lo(ad, a2_ref[...]))
    kk = k * kk_ref[...]
    ss = _dot_exact_rhs(kk * kk, ones_ref[...])
    kap = kk / jnp.maximum(jnp.sqrt(ss), 1e-12)
    r_ref[...] = r
    lw_ref[...] = -jnp.exp(w)
    k_ref[...] = k * (1.0 + (a - 1.0) * ka_ref[...])
    v_ref[...] = v
    kap_ref[...] = kap
    bet_ref[...] = kap * a
    g_ref[...] = _dot_lo(jax.nn.sigmoid(gd), g2_ref[...])


def _rw_prep(zr, shift, B, T, P):
    n, cols = zr.shape
    G = GROUP_W
    tt = _row_tile(T, PREP_ROWS)
    nt = T // tt
    ones_bd = jnp.kron(jnp.eye(RW_HEADS, dtype=F32), jnp.ones((RW_HD, RW_HD), F32)).astype(BF16)
    row = lambda x: x.reshape(1, -1)
    full = lambda shape: pl.BlockSpec(shape, lambda b, i: (0,) * len(shape))
    tile = pl.BlockSpec((tt, G), lambda b, i: (b * nt + i, 0))
    return pl.pallas_call(
        _rw_prep_body,
        grid=(B, nt),
        in_specs=[
            pl.BlockSpec((tt, cols), lambda b, i: (b * nt + i, 0)),
            pl.BlockSpec((1, 1, cols), lambda b, i: (b, 0, 0)),
            full((1, cols)), full((1, G)), full((RW_DECAY_LORA, G)), full((1, G)), full((RW_A_LORA, G)),
            full((RW_GATE_LORA, G)), full((1, G)), full((1, G)), full((G, G)),
        ],
        out_specs=[tile] * 7,
        out_shape=[jax.ShapeDtypeStruct((n, G), F32)] * 7,
        scratch_shapes=[pltpu.VMEM((8, cols), F32)],
        compiler_params=_params(("parallel", "arbitrary"), 40),
        name="rwkv_prep",
    )(zr, shift.reshape(B, 1, cols), row(P['rw_mu']), row(P['rw_w0']), P['rw_w2'].astype(BF16), row(P['rw_a0']),
      P['rw_a2'].astype(BF16), P['rw_g2'].astype(BF16), row(P['rw_kk']), row(P['rw_ka']), ones_bd)


def _rw_scaled(lw, kap, bet, c):
    cs = _chunk_cumsum(lw, c)
    return cs, kap * jnp.exp(cs - lw), bet * jnp.exp(-cs)


def _tri_masks(c):
    row = lax.broadcasted_iota(jnp.int32, (c, c), 0)
    col = lax.broadcasted_iota(jnp.int32, (c, c), 1)
    return row, col


def _chunk_cumsum(x, c):
    row, col = _tri_masks(c)
    tri = jnp.where(col <= row, 1.0, 0.0).astype(BF16)
    parts = [_dot_exact_lhs(tri, x[i:i + c]) for i in range(0, x.shape[0], c)]
    return parts[0] if len(parts) == 1 else jnp.concatenate(parts, axis=0)


def _rw_ldiag_body(lw_ref, kap_ref, bet_ref, o_ref, *, c):
    rows = lw_ref.shape[0]
    _, kk_all, bt_all = _rw_scaled(lw_ref[...], kap_ref[...], bet_ref[...], c)
    srow, scol = _tri_masks(RW_SUB)
    units = [(cc, h) for cc in range(rows // c) for h in range(RW_HEADS)]
    Ls = [_dot_lo(kk_all[cc * c:(cc + 1) * c, h * RW_HD:(h + 1) * RW_HD],
                  bt_all[cc * c:(cc + 1) * c, h * RW_HD:(h + 1) * RW_HD], _NT) for cc, h in units]
    for (cc, h), L in zip(units, Ls):
        for b in range(c // RW_SUB):
            rs = slice(b * RW_SUB, (b + 1) * RW_SUB)
            o_ref[cc * c + b * RW_SUB:cc * c + (b + 1) * RW_SUB, h * RW_SUB:(h + 1) * RW_SUB] = (
                jnp.where(scol < srow, L[rs, rs], 0.0))


def _rw_inv_body(l_ref, t_ref, a_ref, b_ref):
    n = RW_SUB
    nblk = l_ref.shape[0] // n
    for t in range(n):
        a_ref[t] = l_ref[pl.ds(t, nblk, stride=n), :].T
    entry = lambda ref, t, s: ref.at[t, pl.ds(s, RW_HEADS, stride=n), :]
    one = jnp.ones((RW_HEADS, nblk), F32)
    zero = jnp.zeros((RW_HEADS, nblk), F32)
    for t in range(n):
        for s in range(n):
            if s > t:
                entry(b_ref, t, s)[...] = zero
            elif s == t:
                entry(b_ref, t, s)[...] = one
            else:
                acc = entry(a_ref, t, s)[...]
                for j in range(s + 1, t):
                    acc = acc + entry(a_ref, t, j)[...] * entry(b_ref, j, s)[...]
                entry(b_ref, t, s)[...] = -acc
    for t in range(n):
        t_ref[pl.ds(t, nblk, stride=n), :] = b_ref[t].T


def _rw_main_body(r_ref, lw_ref, k_ref, v_ref, kap_ref, bet_ref, g_ref, td_ref, h0_ref, rk_ref, lng_ref, lnb_ref,
                  o_ref, hout_ref, h_ref, *, c):
    ci = pl.program_id(1)
    rows = r_ref.shape[0]
    nb = c // RW_SUB

    @pl.when(ci == 0)
    def _():
        h_ref[...] = h0_ref[0]

    crow, ccol = _tri_masks(c)
    strict = ccol < crow
    incl = ccol <= crow
    lw = lw_ref[...]
    cs, kk_all, bt_all = _rw_scaled(lw, kap_ref[...], bet_ref[...], c)
    gi = jnp.exp(-cs)
    gg = jnp.exp(cs)
    kt_all = k_ref[...] * gi
    rt_all = r_ref[...] * gg
    bonus_all = r_ref[...] * k_ref[...] * rk_ref[...]
    hrow = lax.broadcasted_iota(jnp.int32, (RW_HD, RW_HD), 0)
    hcol = lax.broadcasted_iota(jnp.int32, (RW_HD, RW_HD), 1)
    HS = range(RW_HEADS)
    units = [(cc, h) for cc in range(rows // c) for h in HS]
    US = range(len(units))
    rsl = [slice(cc * c, (cc + 1) * c) for cc, _ in units]
    lsl = [slice(h * RW_HD, (h + 1) * RW_HD) for _, h in units]
    Kk = [kk_all[rsl[u], lsl[u]] for u in US]
    Bt = [bt_all[rsl[u], lsl[u]] for u in US]
    Kt = [kt_all[rsl[u], lsl[u]] for u in US]
    Rt = [rt_all[rsl[u], lsl[u]] for u in US]
    vv = [v_ref[rsl[u], lsl[u]] for u in US]
    Lm = [jnp.where(strict, _dot_lo(Kk[u], Bt[u], _NT), 0.0) for u in US]
    A1 = [jnp.where(strict, _dot_lo(Kk[u], Kt[u], _NT), 0.0) for u in US]
    A4 = [jnp.where(incl, _dot_lo(Rt[u], Bt[u], _NT), 0.0) for u in US]
    A3 = [jnp.where(incl, _dot_lo(Rt[u], Kt[u], _NT), 0.0) for u in US]
    X = [jnp.concatenate([Kk[u], _dot_lo(A1[u], vv[u])], axis=1) for u in US]
    zs = [[] for _ in US]
    for b in range(nb):
        rs = slice(b * RW_SUB, (b + 1) * RW_SUB)
        rhs = [X[u][rs] for u in US]
        if b:
            rhs = [rhs[u] - _dot_lo(Lm[u][rs, 0:b * RW_SUB], jnp.concatenate(zs[u], axis=0)) for u in US]
        for u, (cc, h) in enumerate(units):
            tbb = td_ref[cc * c + b * RW_SUB:cc * c + (b + 1) * RW_SUB, h * RW_SUB:(h + 1) * RW_SUB]
            zs[u].append(_dot_lo(tbb, rhs[u]))
    Z = [jnp.concatenate(zs[u], axis=0) if nb > 1 else zs[u][0] for u in US]
    A4Z = [_dot_lo(A4[u], Z[u]) for u in US]
    Rhat = [Rt[u] - A4Z[u][:, :RW_HD] for u in US]
    Yhat = [_dot_lo(A3[u], vv[u]) - A4Z[u][:, RW_HD:] for u in US]
    gC = [gg[(cc + 1) * c - 1:(cc + 1) * c, lsl[u]] for u, (cc, _) in enumerate(units)]
    MN = [_dot_lo(Bt[u] * gC[u], Z[u], _TN) for u in US]
    Mp = [jnp.where(hrow == hcol, gC[u], 0.0) - MN[u][:, :RW_HD] for u in US]
    Np = [_dot_lo(Kt[u] * gC[u], vv[u], _TN) - MN[u][:, RW_HD:] for u in US]
    H = [h_ref[h] for h in HS]
    ys = [None for _ in US]
    for cc in range(rows // c):
        for h in HS:
            u = cc * RW_HEADS + h
            ys[u] = _dot_lo(Rhat[u], H[h]) + Yhat[u]
        H = [_dot_hi(Mp[cc * RW_HEADS + h], H[h]) + Np[cc * RW_HEADS + h] for h in HS]
    for h in HS:
        h_ref[h] = H[h]
    for u in US:
        y = ys[u]
        mu = jnp.mean(y, axis=-1, keepdims=True)
        var = jnp.mean(jnp.square(y - mu), axis=-1, keepdims=True)
        yn = (y - mu) * lax.rsqrt(var + RW_LN_EPS) * lng_ref[:, lsl[u]] + lnb_ref[:, lsl[u]]
        yn = yn + jnp.sum(bonus_all[rsl[u], lsl[u]], axis=-1, keepdims=True) * vv[u]
        o_ref[rsl[u], lsl[u]] = (yn * g_ref[rsl[u], lsl[u]]).astype(o_ref.dtype)

    @pl.when(ci == pl.num_programs(1) - 1)
    def _():
        hout_ref[0] = h_ref[...]


def _rwkv7(zr, shift, S0, B, T, P):
    G = GROUP_W
    n = B * T
    r, lw, k, v, kap, bet, g = _rw_prep(zr, shift, B, T, P)
    c = min(RW_CHUNK, T)
    rows_l = _row_tile(T, c * RW_LDIAG_CHUNKS)
    rows_m = _row_tile(T, c * RW_MAIN_CHUNKS)
    nl, nc = T // rows_l, T // rows_m
    tile_l = pl.BlockSpec((rows_l, G), lambda b, i: (b * nl + i, 0))
    tile = pl.BlockSpec((rows_m, G), lambda b, i: (b * nc + i, 0))
    ld = pl.pallas_call(
        functools.partial(_rw_ldiag_body, c=c),
        grid=(B, nl),
        in_specs=[tile_l] * 3,
        out_specs=pl.BlockSpec((rows_l, RW_HEADS * RW_SUB), lambda b, i: (b * nl + i, 0)),
        out_shape=jax.ShapeDtypeStruct((n, RW_HEADS * RW_SUB), F32),
        compiler_params=_params(("parallel", "parallel"), 32),
        name="rwkv_ldiag",
    )(lw, kap, bet)
    rows_i = V7X_LANES * RW_SUB
    npad = -(-n // rows_i) * rows_i
    inv_spec = pl.BlockSpec((rows_i, RW_HEADS * RW_SUB), lambda i: (i, 0))
    inv_scratch = pltpu.VMEM((RW_SUB, RW_HEADS * RW_SUB, V7X_LANES), F32)
    td = pl.pallas_call(
        _rw_inv_body,
        grid=(npad // rows_i,),
        in_specs=[inv_spec],
        out_specs=inv_spec,
        out_shape=jax.ShapeDtypeStruct((npad, RW_HEADS * RW_SUB), F32),
        scratch_shapes=[inv_scratch, inv_scratch],
        compiler_params=_params(("parallel",), 32),
        name="rwkv_inv",
    )(jnp.pad(ld, ((0, npad - n), (0, 0))))[:n]
    h0 = jnp.swapaxes(S0, -1, -2)
    prow = lambda x: pl.BlockSpec((1, G), lambda b, i: (0, 0))
    st_spec = pl.BlockSpec((1, RW_HEADS, RW_HD, RW_HD), lambda b, i: (b, 0, 0, 0))
    out, hl = pl.pallas_call(
        functools.partial(_rw_main_body, c=c),
        grid=(B, nc),
        in_specs=[tile] * 7 + [pl.BlockSpec((rows_m, RW_HEADS * RW_SUB), lambda b, i: (b * nc + i, 0)), st_spec,
                               prow(0), prow(0), prow(0)],
        out_specs=[tile, st_spec],
        out_shape=[jax.ShapeDtypeStruct((n, G), BF16), jax.ShapeDtypeStruct(S0.shape, F32)],
        scratch_shapes=[pltpu.VMEM((RW_HEADS, RW_HD, RW_HD), F32)],
        compiler_params=_params(("parallel", "arbitrary"), 32),
        name="rwkv_main",
    )(r, lw, k, v, kap, bet, g, td, h0, P['rw_rk'].reshape(1, G), P['rw_ln_g'].reshape(1, G),
      P['rw_ln_b'].reshape(1, G))
    return out, zr.reshape(B, T, -1)[:, -1], jnp.swapaxes(hl, -1, -2)


def _even_mixer(x2, B, T, g, st, P):
    conv_buf, lru_h, k_past, v_past, lf_past = st
    G = GROUP_W
    z_rg, z_qkv, z_og, z_fl = _norm_matmul(x2, g, P['e_w_in'], (2 * G, 3 * G, G, V7X_LANES))
    rnn_out, conv_new, h_last = _lru(z_rg, conv_buf, lru_h, B, T, P)
    qb, kn, kb, v, vb, lf = _fox_prep(z_qkv, z_fl, B, T, P)
    past = k_past.shape[1]
    lf_all = lf.reshape(B, T, V7X_LANES)
    kb_all, vb_all = kb.reshape(B, T, G), vb.reshape(B, T, G)
    if past:
        lf_all = jnp.concatenate([jnp.pad(lf_past, ((0, 0), (0, 0), (0, V7X_LANES - FOX_HEADS))), lf_all], axis=1)
        kb_all = jnp.concatenate([k_past.reshape(B, past, G).astype(BF16), kb_all], axis=1)
        vb_all = jnp.concatenate([v_past.reshape(B, past, G).astype(BF16), vb_all], axis=1)
    tail = ((0, 0), (0, -(past + T) % FOX_K_ROWS), (0, 0))
    ka, vt = _fox_keys(jnp.pad(lf_all, tail), jnp.pad(kb_all, tail), jnp.pad(vb_all, tail))
    fox_out = _fox_attention(qb, ka, vt, z_og, B, T, past)
    heads = lambda t: t.reshape(B, T, FOX_HEADS, FOX_HD)
    return (rnn_out, fox_out, P['e_w_out']), (conv_new, h_last, heads(kn), heads(v), lf.reshape(B, T, V7X_LANES)[..., :FOX_HEADS])


def _odd_mixer(x2, B, T, g, st, lb, P):
    S_hg, shift, S_rw = st
    G = GROUP_W
    z_hg, z_rw = _norm_matmul(x2, g, P['o_w_in'], (4 * G, P['o_w_in'].shape[1] - 4 * G))
    hg_out, S_hg_new = _hgrn2(z_hg, lb, S_hg, B, T, P)
    rw_out, shift_new, S_rw_new = _rwkv7(z_rw, shift, S_rw, B, T, P)
    return (hg_out, rw_out, P['o_w_out']), (S_hg_new, shift_new, S_rw_new)


def _trunk(x, states, W):
    lru_conv, lru_h, fox_k, fox_v, fox_lf, hg_S, rw_shift, rw_S = states
    B, T, D = x.shape
    depth = W['norm_g'].shape[0]
    sm = jax.nn.softmax(W['hg_lb_logits'], axis=0)
    lower_bounds = jnp.cumsum(sm, axis=0) - sm[0]
    x2 = x.reshape(B * T, D)
    even_new, odd_new = [], []
    for layer in range(depth):
        g = W['norm_g'][layer]
        x2 = _ffn(x2, g[0], W['ffn_w_in'][layer][0], W['ffn_w_out'][layer][0])
        if layer % 2 == 0:
            e = layer // 2
            P = {n: W[n][e] for n in ('e_w_in', 'e_w_out', 'lru_conv_w', 'lru_conv_b', 'lru_wa', 'lru_ba', 'lru_wx',
                                      'lru_bx', 'lru_lambda', 'fox_q_gain', 'fox_k_gain', 'fox_f_bias')}
            mix, new = _even_mixer(x2, B, T, g[1], (lru_conv[e], lru_h[e], fox_k[e], fox_v[e], fox_lf[e]), P)
            even_new.append(new)
        else:
            o = layer // 2
            P = {n: W[n][o] for n in ('o_w_in', 'o_w_out', 'hg_norm_g', 'rw_mu', 'rw_w0', 'rw_w2', 'rw_a0', 'rw_a2',
                                      'rw_g2', 'rw_kk', 'rw_ka', 'rw_rk', 'rw_ln_g', 'rw_ln_b')}
            mix, new = _odd_mixer(x2, B, T, g[1], (hg_S[o], rw_shift[o], rw_S[o]), lower_bounds[layer], P)
            odd_new.append(new)
        x2 = _ffn(x2, g[2], W['ffn_w_in'][layer][1], W['ffn_w_out'][layer][1], mix)
    ev = [jnp.stack([n[j] for n in even_new]) for j in range(5)]
    od = [jnp.stack([n[j] for n in odd_new]) for j in range(3)]
    return x2.reshape(B, T, D), (ev[0], ev[1], ev[2], ev[3], ev[4], od[0], od[1], od[2])


def kernel(x_prompt, x_sample, state_lru_conv, state_lru_h, cache_fox_k, cache_fox_v, cache_fox_logf,
           state_hgrn_S, state_rwkv_shift, state_rwkv_S, norm_g, ffn_w_in, ffn_w_out, e_w_in, e_w_out,
           lru_conv_w, lru_conv_b, lru_wa, lru_ba, lru_wx, lru_bx, lru_lambda, fox_q_gain, fox_k_gain,
           fox_f_bias, o_w_in, o_w_out, hg_lb_logits, hg_norm_g, rw_mu, rw_w0, rw_w2, rw_a0, rw_a2, rw_g2,
           rw_kk, rw_ka, rw_rk, rw_ln_g, rw_ln_b):
    n_even, n_odd = e_w_in.shape[0], o_w_in.shape[0]
    W = dict(norm_g=norm_g, ffn_w_in=_ffn_w_in_tiles(ffn_w_in), ffn_w_out=ffn_w_out.astype(BF16),
             e_w_in=_pad_cols(e_w_in.astype(BF16)), e_w_out=e_w_out.astype(BF16),
             lru_conv_w=lru_conv_w, lru_conv_b=lru_conv_b, lru_wa=lru_wa, lru_ba=lru_ba, lru_wx=lru_wx,
             lru_bx=lru_bx, lru_lambda=lru_lambda, fox_q_gain=fox_q_gain, fox_k_gain=fox_k_gain,
             fox_f_bias=fox_f_bias, o_w_in=o_w_in.astype(BF16), o_w_out=o_w_out.astype(BF16),
             hg_lb_logits=hg_lb_logits, hg_norm_g=hg_norm_g, rw_mu=rw_mu, rw_w0=rw_w0, rw_w2=rw_w2, rw_a0=rw_a0,
             rw_a2=rw_a2, rw_g2=rw_g2, rw_kk=rw_kk, rw_ka=rw_ka, rw_rk=rw_rk, rw_ln_g=rw_ln_g, rw_ln_b=rw_ln_b)
    nb = x_prompt.shape[0]
    dt = x_prompt.dtype
    prompt_states = (jnp.zeros((n_even, nb, CONV_W - 1, GROUP_W), dt),
                     jnp.zeros((n_even, nb, GROUP_W), dt),
                     jnp.zeros((n_even, nb, 0, FOX_HEADS, FOX_HD), dt),
                     jnp.zeros((n_even, nb, 0, FOX_HEADS, FOX_HD), dt),
                     jnp.zeros((n_even, nb, 0, FOX_HEADS), dt),
                     jnp.zeros((n_odd, nb, HG_HEADS, GROUP_W // HG_HEADS, GROUP_W // HG_HEADS), dt),
                     jnp.zeros((n_odd, nb, rw_mu.shape[1]), dt),
                     jnp.zeros((n_odd, nb, RW_HEADS, RW_HD, RW_HD), dt))
    sample_states = (state_lru_conv, state_lru_h, cache_fox_k, cache_fox_v, cache_fox_logf,
                     state_hgrn_S, state_rwkv_shift, state_rwkv_S)
    y_prompt, p_new = _trunk(x_prompt, prompt_states, W)
    y_sample, s_new = _trunk(x_sample, sample_states, W)
    lru_conv_p, lru_h_p, fox_k_p, fox_v_p, fox_logf_p, hgrn_S_p, rwkv_shift_p, rwkv_S_p = p_new
    lru_conv_s, lru_h_s, fox_k_s, fox_v_s, fox_logf_s, hgrn_S_s, rwkv_shift_s, rwkv_S_s = s_new
    return (y_prompt, y_sample, lru_conv_p, lru_conv_s, lru_h_p, lru_h_s, fox_k_p, fox_k_s, fox_v_p, fox_v_s,
            fox_logf_p, fox_logf_s, hgrn_S_p, hgrn_S_s, rwkv_shift_p, rwkv_shift_s, rwkv_S_p, rwkv_S_s)
```

```python
import functools

import jax
import jax.numpy as jnp
from jax import lax
from jax.experimental import pallas as pl
from jax.experimental.pallas import tpu as pltpu

F32 = jnp.float32
BF16 = jnp.bfloat16

NORM_EPS = 1e-6
GROUP_W = 512
CONV_W = 4
LRU_C = 8.0
FOX_HEADS = 8
FOX_HD = 64
HG_HEADS = 4
RW_HEADS = 8
RW_HD = 64
RW_DECAY_LORA = 64
RW_A_LORA = 64
RW_GATE_LORA = 128
RW_LN_EPS = 64e-5

V7X_LANES = 128
FFN_COL_TILE = 1408
PREP_ROWS = 512
FFN_ROW_TILE = 1024
FFN_VMEM_MIB = 60


def _row_tile(n, want):
    t = min(n, want)
    while n % t:
        t //= 2
    return t


def _params(sem, vmem_mib):
    return pltpu.CompilerParams(dimension_semantics=sem, vmem_limit_bytes=vmem_mib << 20)


def _pad_cols(w):
    pad = -w.shape[-1] % V7X_LANES
    return jnp.pad(w, [(0, 0)] * (w.ndim - 1) + [(0, pad)])


def _rms(x, g):
    return x * lax.rsqrt(jnp.mean(x * x, axis=-1, keepdims=True) + NORM_EPS) * g


def _ffn_body(*refs, mixed):
    if mixed:
        x_ref, a_ref, b_ref, wa_ref, wb_ref, g_ref, wi_ref, wo_ref, o_ref, h_ref, acc_ref = refs
    else:
        x_ref, g_ref, wi_ref, wo_ref, o_ref, h_ref, acc_ref = refs
    j = pl.program_id(1)

    @pl.when(j == 0)
    def _():
        x = x_ref[...]
        if mixed:
            x = x + jnp.dot(a_ref[...], wa_ref[...], preferred_element_type=F32)
            x = x + jnp.dot(b_ref[...], wb_ref[...], preferred_element_type=F32)
        h_ref[...] = _rms(x, g_ref[...]).astype(BF16)
        acc_ref[...] = 2.0 * x

    tf = wo_ref.shape[0]
    gu = jnp.dot(h_ref[...], wi_ref[...], preferred_element_type=F32)
    gate, up = gu[:, :tf], gu[:, tf:]
    act = (gate * jax.nn.sigmoid(gate) * up).astype(BF16)
    acc_ref[...] += jnp.dot(act, wo_ref[...], preferred_element_type=F32)

    @pl.when(j == pl.num_programs(1) - 1)
    def _():
        o_ref[...] = 0.5 * acc_ref[...]


def _cast_body(x_ref, o_ref):
    o_ref[...] = x_ref[...].astype(o_ref.dtype)


def _ffn_w_in_tiles(w_in):
    *lead, d, f2 = w_in.shape
    tf = FFN_COL_TILE
    nf = f2 // 2 // tf
    w = w_in.reshape(-1, d, f2)
    out = pl.pallas_call(
        _cast_body,
        grid=(w.shape[0], nf, 2),
        in_specs=[pl.BlockSpec((1, d, tf), lambda i, j, gu: (i, 0, gu * nf + j))],
        out_specs=pl.BlockSpec((1, d, tf), lambda i, j, gu: (i, 0, 2 * j + gu)),
        out_shape=jax.ShapeDtypeStruct(w.shape, BF16),
        compiler_params=_params(("parallel", "parallel", "parallel"), 32),
        name="ffn_weight_tiles",
    )(w)
    return out.reshape(*lead, d, f2)


def _ffn(x, g, w_in, w_out, mix=None):
    n, d = x.shape
    f = w_out.shape[0]
    tm = _row_tile(n, FFN_ROW_TILE)
    tf = FFN_COL_TILE
    nf = f // tf
    row_spec = lambda w: pl.BlockSpec((tm, w), lambda i, j: (i, 0))
    args, specs = [x], [row_spec(d)]
    if mix is not None:
        a, b, w = mix
        ga, gb = a.shape[1], b.shape[1]
        args += [a, b, w[:ga], w[ga:]]
        specs += [row_spec(ga), row_spec(gb), pl.BlockSpec((ga, d), lambda i, j: (0, 0)),
                  pl.BlockSpec((gb, d), lambda i, j: (0, 0))]
    args += [g.reshape(1, d), w_in, w_out]
    specs += [pl.BlockSpec((1, d), lambda i, j: (0, 0)), pl.BlockSpec((d, 2 * tf), lambda i, j: (0, j)),
              pl.BlockSpec((tf, d), lambda i, j: (j, 0))]
    return pl.pallas_call(
        functools.partial(_ffn_body, mixed=mix is not None),
        grid=(n // tm, nf),
        in_specs=specs,
        out_specs=row_spec(d),
        out_shape=jax.ShapeDtypeStruct((n, d), F32),
        scratch_shapes=[pltpu.VMEM((tm, d), BF16), pltpu.VMEM((tm, d), F32)],
        compiler_params=_params(("parallel", "arbitrary"), FFN_VMEM_MIB),
        name="ffn",
    )(*args)


def _norm_matmul_body(x_ref, g_ref, w_ref, *o_refs):
    h = _rms(x_ref[...], g_ref[...]).astype(BF16)
    z = jnp.dot(h, w_ref[...], preferred_element_type=F32)
    start = 0
    for o_ref in o_refs:
        width = o_ref.shape[1]
        o_ref[...] = z[:, start:start + width]
        start += width


def _norm_matmul(x, g, w, widths):
    n, d = x.shape
    c = w.shape[1]
    assert sum(widths) == c and all(wd % V7X_LANES == 0 for wd in widths)
    tm = _row_tile(n, 512)
    return pl.pallas_call(
        _norm_matmul_body,
        grid=(n // tm,),
        in_specs=[
            pl.BlockSpec((tm, d), lambda i: (i, 0)),
            pl.BlockSpec((1, d), lambda i: (0, 0)),
            pl.BlockSpec((d, c), lambda i: (0, 0)),
        ],
        out_specs=[pl.BlockSpec((tm, wd), lambda i: (i, 0)) for wd in widths],
        out_shape=[jax.ShapeDtypeStruct((n, wd), F32) for wd in widths],
        compiler_params=_params(("parallel",), 48),
        name="norm_matmul",
    )(x, g.reshape(1, d), w)


LRU_ROWS = 256
CONV_PAD = 8


def _expm1(x):
    series = x * (1.0 + x * (1 / 2 + x * (1 / 6 + x * (1 / 24 + x * (1 / 120 + x * (1 / 720 + x * (1 / 5040 + x * (1 / 40320))))))))
    return jnp.where(jnp.abs(x) < 0.25, series, jnp.exp(x) - 1.0)


def _shift_rows(x, s, fill):
    row = lax.broadcasted_iota(jnp.int32, x.shape, 0)
    return jnp.where(row >= s, pltpu.roll(x, s, axis=0), fill)


def _lru_body(z_ref, buf_ref, h0_ref, cw_ref, cb_ref, wa_ref, ba_ref, wx_ref, bx_ref, lam_ref,
              o_ref, bufo_ref, ho_ref, x_ref, hc_ref):
    G = GROUP_W
    tt = z_ref.shape[0]

    @pl.when(pl.program_id(1) == 0)
    def _():
        x_ref[0:CONV_PAD, :] = buf_ref[0]
        hc_ref[...] = jnp.broadcast_to(h0_ref[0], hc_ref.shape)

    x_ref[CONV_PAD:CONV_PAD + tt, :] = z_ref[:, 0:G]
    xc = cb_ref[...]
    for j in range(CONV_W):
        lo = CONV_PAD - (CONV_W - 1) + j
        xc = xc + x_ref[lo:lo + tt, :] * cw_ref[j:j + 1, :]
    hist = x_ref[tt:tt + CONV_PAD, :]
    x_ref[0:CONV_PAD, :] = hist
    bufo_ref[0] = hist

    xb = xc.astype(BF16)
    r = jax.nn.sigmoid(jnp.dot(xb, wa_ref[...], preferred_element_type=F32) + ba_ref[...])
    ig = jax.nn.sigmoid(jnp.dot(xb, wx_ref[...], preferred_element_type=F32) + bx_ref[...])
    log_a = (-LRU_C * _softplus(-lam_ref[...])) * r
    a = jnp.exp(log_a)
    b = jnp.sqrt(-_expm1(2.0 * log_a)) * (ig * xc)
    s = 1
    while s < tt:
        if s % 8:
            b = a * _shift_rows(b, s, 0.0) + b
            a = a * _shift_rows(a, s, 1.0)
        else:
            b = jnp.concatenate([b[:s], a[s:] * b[:tt - s] + b[s:]], axis=0)
            a = jnp.concatenate([a[:s], a[s:] * a[:tt - s]], axis=0)
        s *= 2
    h = a * hc_ref[0:1, :] + b
    hc_ref[...] = jnp.broadcast_to(h[tt - 1:tt, :], hc_ref.shape)
    ho_ref[0] = h[tt - 1:tt, :]
    o_ref[...] = (jax.nn.gelu(z_ref[:, G:2 * G]) * h).astype(o_ref.dtype)


def _block_diag_dense(w):
    nb, bs, _ = w.shape
    eye = jnp.eye(nb, dtype=w.dtype)
    return (eye[:, None, :, None] * w[:, :, None, :]).reshape(nb * bs, nb * bs)


def _lru(z_rg, conv_buf, h0, B, T, P):
    G = GROUP_W
    n = B * T
    tt = _row_tile(T, LRU_ROWS)
    nt = T // tt
    buf = jnp.pad(conv_buf, ((0, 0), (CONV_PAD - (CONV_W - 1), 0), (0, 0)))
    cw = jnp.pad(P['lru_conv_w'], ((0, CONV_PAD - CONV_W), (0, 0)))
    row = lambda x: x.reshape(1, G)
    full = lambda shape: pl.BlockSpec(shape, lambda b, i: (0,) * len(shape))
    out, bufo, ho = pl.pallas_call(
        _lru_body,
        grid=(B, nt),
        in_specs=[
            pl.BlockSpec((tt, 2 * G), lambda b, i: (b * nt + i, 0)),
            pl.BlockSpec((1, CONV_PAD, G), lambda b, i: (b, 0, 0)),
            pl.BlockSpec((1, 1, G), lambda b, i: (b, 0, 0)),
            full((CONV_PAD, G)), full((1, G)), full((G, G)), full((1, G)), full((G, G)), full((1, G)), full((1, G)),
        ],
        out_specs=[
            pl.BlockSpec((tt, G), lambda b, i: (b * nt + i, 0)),
            pl.BlockSpec((1, CONV_PAD, G), lambda b, i: (b, 0, 0)),
            pl.BlockSpec((1, 1, G), lambda b, i: (b, 0, 0)),
        ],
        out_shape=[jax.ShapeDtypeStruct((n, G), BF16), jax.ShapeDtypeStruct((B, CONV_PAD, G), F32),
                   jax.ShapeDtypeStruct((B, 1, G), F32)],
        scratch_shapes=[pltpu.VMEM((tt + CONV_PAD, G), F32), pltpu.VMEM((8, G), F32)],
        compiler_params=_params(("parallel", "arbitrary"), 32),
        name="lru",
    )(z_rg, buf, h0.reshape(B, 1, G), cw, row(P['lru_conv_b']), _block_diag_dense(P['lru_wa']).astype(BF16),
      row(P['lru_ba']), _block_diag_dense(P['lru_wx']).astype(BF16), row(P['lru_bx']), row(P['lru_lambda']))
    return out, bufo[:, CONV_PAD - (CONV_W - 1):], ho.reshape(B, G)


FOX_Q_COLS = 2048
FOX_K_ROWS = 512
FOX_F_SPLIT = 3
FOX_NEG = -1e30
LOG2E = 1.4426950408889634
HEAD_PAIRS = FOX_HEADS // 2
PAIR_W = 2 * FOX_HD
FOX_CUMSUM_ROWS = 256
FOX_KEY_TILE = 512


def _fox_prep_body(z_ref, fl_ref, qg_ref, kg_ref, fb_ref, ones_ref, q_ref, k_ref, kb_ref, v_ref, vb_ref, lf_ref,
                   *, q_transposed):
    G = GROUP_W
    q, k, v = z_ref[:, 0:G], z_ref[:, G:2 * G], z_ref[:, 2 * G:3 * G]
    inv = 1.0 / FOX_HD
    qn = q * lax.rsqrt(_dot_exact_rhs(q * q, ones_ref[...]) * inv + NORM_EPS) * qg_ref[...]
    kn = k * lax.rsqrt(_dot_exact_rhs(k * k, ones_ref[...]) * inv + NORM_EPS) * kg_ref[...]
    qs = qn * (LOG2E * FOX_HD ** -0.5)
    if q_transposed:
        for p in range(HEAD_PAIRS):
            q_ref[0, p] = qs[:, p * PAIR_W:(p + 1) * PAIR_W].T.astype(BF16)
    else:
        q_ref[...] = qs.astype(BF16)
    tt = z_ref.shape[0]
    for h in range(FOX_HEADS):
        k_ref[pl.ds(h, tt, stride=FOX_HEADS), :] = kn[:, h * FOX_HD:(h + 1) * FOX_HD]
        v_ref[pl.ds(h, tt, stride=FOX_HEADS), :] = v[:, h * FOX_HD:(h + 1) * FOX_HD]
    kb_ref[...] = kn.astype(BF16)
    vb_ref[...] = v.astype(BF16)
    x = fl_ref[...] + fb_ref[...]
    lf_ref[...] = -_softplus(-x)


def _fox_prep(z_qkv, z_fl, B, T, P):
    n = z_qkv.shape[0]
    G = GROUP_W
    tt = _row_tile(T, PREP_ROWS)
    nt = T // tt
    q_transposed = tt % V7X_LANES == 0
    ones_bd = jnp.kron(jnp.eye(FOX_HEADS, dtype=F32), jnp.ones((FOX_HD, FOX_HD), F32)).astype(BF16)
    fb = jnp.pad(P['fox_f_bias'], (0, V7X_LANES - FOX_HEADS)).reshape(1, V7X_LANES)
    tile = lambda w: pl.BlockSpec((tt, w), lambda b, i: (b * nt + i, 0))
    full = lambda shape: pl.BlockSpec(shape, lambda b, i: (0,) * len(shape))
    if q_transposed:
        q_spec = pl.BlockSpec((1, HEAD_PAIRS, PAIR_W, tt), lambda b, i: (b, 0, 0, i))
        q_shape = jax.ShapeDtypeStruct((B, HEAD_PAIRS, PAIR_W, T), BF16)
    else:
        q_spec, q_shape = tile(G), jax.ShapeDtypeStruct((n, G), BF16)
    state_spec = pl.BlockSpec((tt * FOX_HEADS, FOX_HD), lambda b, i: (b * nt + i, 0))
    state_shape = jax.ShapeDtypeStruct((n * FOX_HEADS, FOX_HD), F32)
    return pl.pallas_call(
        functools.partial(_fox_prep_body, q_transposed=q_transposed),
        grid=(B, nt),
        in_specs=[tile(3 * G), tile(V7X_LANES), full((1, G)), full((1, G)), full((1, V7X_LANES)), full((G, G))],
        out_specs=[q_spec, state_spec, tile(G), state_spec, tile(G), tile(V7X_LANES)],
        out_shape=[q_shape, state_shape, jax.ShapeDtypeStruct((n, G), BF16), state_shape,
                   jax.ShapeDtypeStruct((n, G), BF16), jax.ShapeDtypeStruct((n, V7X_LANES), F32)],
        compiler_params=_params(("parallel", "parallel"), 32),
        name="fox_prep",
    )(z_qkv, z_fl, jnp.tile(P['fox_q_gain'], FOX_HEADS).reshape(1, G),
      jnp.tile(P['fox_k_gain'], FOX_HEADS).reshape(1, G), fb, ones_bd)


def _fox_keys_body(lf_ref, kb_ref, vb_ref, ka_ref, vt_ref, c_ref):
    tt = lf_ref.shape[1]

    @pl.when(pl.program_id(1) == 0)
    def _():
        c_ref[...] = jnp.zeros_like(c_ref)

    grp = min(tt, FOX_CUMSUM_ROWS)
    carry, fs = c_ref[0:1, :], []
    for i in range(0, tt, grp):
        fs.append(_chunk_cumsum(lf_ref[0, i:i + grp, :], grp) + carry)
        carry = fs[-1][grp - 1:grp, :]
    f = jnp.concatenate(fs, axis=0) if len(fs) > 1 else fs[0]
    c_ref[...] = jnp.broadcast_to(carry, c_ref.shape)
    parts = _split3(f * LOG2E)
    srow = lax.broadcasted_iota(jnp.int32, (V7X_LANES, GROUP_W), 0)
    scol = lax.broadcasted_iota(jnp.int32, (V7X_LANES, GROUP_W), 1)
    aug = jnp.zeros((tt, GROUP_W), F32)
    for t, part in enumerate(parts):
        sel = jnp.where((scol // FOX_HD == srow) & (scol % FOX_HD == t), 1.0, 0.0).astype(BF16)
        aug = aug + jnp.dot(part, sel, preferred_element_type=F32)
    aug = aug.astype(BF16)
    for h in range(FOX_HEADS):
        hs = slice(h * FOX_HD, (h + 1) * FOX_HD)
        ka_ref[0, h] = jnp.concatenate([kb_ref[0, :, hs], aug[:, hs]], axis=1)
    for p in range(HEAD_PAIRS):
        vt_ref[0, p] = vb_ref[0, :, p * PAIR_W:(p + 1) * PAIR_W].astype(F32).T.astype(BF16)


def _fox_keys(lf_all, kb_all, vb_all):
    B, tk_all, L = lf_all.shape
    G = GROUP_W
    tt = FOX_KEY_TILE
    return pl.pallas_call(
        _fox_keys_body,
        grid=(B, tk_all // tt),
        in_specs=[pl.BlockSpec((1, tt, L), lambda b, i: (b, i, 0)),
                  pl.BlockSpec((1, tt, G), lambda b, i: (b, i, 0)),
                  pl.BlockSpec((1, tt, G), lambda b, i: (b, i, 0))],
        out_specs=[pl.BlockSpec((1, FOX_HEADS, tt, 2 * FOX_HD), lambda b, i: (b, 0, i, 0)),
                   pl.BlockSpec((1, HEAD_PAIRS, PAIR_W, tt), lambda b, i: (b, 0, 0, i))],
        out_shape=[jax.ShapeDtypeStruct((B, FOX_HEADS, tk_all, 2 * FOX_HD), BF16),
                   jax.ShapeDtypeStruct((B, HEAD_PAIRS, PAIR_W, tk_all), BF16)],
        scratch_shapes=[pltpu.VMEM((8, L), F32)],
        compiler_params=_params(("parallel", "arbitrary"), 32),
        name="fox_keys",
    )(lf_all, kb_all, vb_all)


def _fox_attn_body(qt_ref, ka_ref, vt_ref, og_ref, o_ref, acc_ref, m_ref, l_ref, *, past, tk, t_real):
    qi = pl.program_id(2)
    tq = qt_ref.shape[3]
    t_out = o_ref.shape[0]
    first_q = past + qi * tq
    last_q = past + jnp.minimum(qi * tq + tq, t_real) - 1
    n_full = (first_q + 1) // tk
    n_all = last_q // tk + 1
    drow = lax.broadcasted_iota(jnp.int32, (FOX_HD, tq), 0)
    minus = jnp.where(drow < FOX_F_SPLIT, -1.0, 0.0).astype(BF16)
    rhs = [jnp.concatenate([qt_ref[0, 0, h * FOX_HD:(h + 1) * FOX_HD, :], minus], axis=0) for h in range(2)]
    acc_ref[...] = jnp.zeros_like(acc_ref)

    def update(ki, m_prev, l_prev, masked, q0=0):
        ks = pl.multiple_of(ki * tk, tk)
        s = [jnp.dot(ka_ref[0, h, pl.ds(ks, tk), :], rhs[h][:, q0:], preferred_element_type=F32) for h in range(2)]
        if masked:
            krow = lax.broadcasted_iota(jnp.int32, (tk, tq - q0), 0)
            qcol = lax.broadcasted_iota(jnp.int32, (tk, tq - q0), 1)
            vis = ks + krow <= first_q + q0 + qcol
            s = [jnp.where(vis, s[h], FOX_NEG) for h in range(2)]
        m_new = [jnp.maximum(m_prev[h], jnp.max(s[h], axis=0, keepdims=True)) for h in range(2)]
        alpha = [jnp.exp2(m_prev[h] - m_new[h]) for h in range(2)]
        p = [jnp.exp2(s[h] - m_new[h]) for h in range(2)]
        l_new = [alpha[h] * l_prev[h] + jnp.sum(p[h], axis=0, keepdims=True) for h in range(2)]
        vt = [vt_ref[0, 0, h * FOX_HD:(h + 1) * FOX_HD, pl.ds(ks, tk)] for h in range(2)]
        pv = [jnp.dot(vt[h], p[h].astype(BF16), preferred_element_type=F32) for h in range(2)]
        for h in range(2):
            acc_ref[h, :, q0:] = alpha[h] * acc_ref[h, :, q0:] + pv[h]
        return m_new, l_new

    def carried(ki, c, masked):
        return update(ki, c[0], c[1], masked)

    c = ([jnp.full((1, tq), FOX_NEG, F32)] * 2, [jnp.zeros((1, tq), F32)] * 2)
    c = lax.fori_loop(0, n_full, lambda ki, c: carried(ki, c, False), c)
    if tq > tk and tq % tk == 0 and past % tq == 0 and t_real % tq == 0:
        m, l = carried(n_full, c, True)
        for h in range(2):
            m_ref[h], l_ref[h] = m[h], l[h]
        for j in range(1, tq // tk):
            q0 = j * tk
            m, l = update(n_full + j, [m_ref[h, :, q0:] for h in range(2)], [l_ref[h, :, q0:] for h in range(2)],
                          True, q0=q0)
            for h in range(2):
                m_ref[h, :, q0:], l_ref[h, :, q0:] = m[h], l[h]
    else:
        _, l = lax.fori_loop(n_full, n_all, lambda ki, c: carried(ki, c, True), c)
        for h in range(2):
            l_ref[h] = l[h]
    o_t = jnp.concatenate([acc_ref[h] / l_ref[h] for h in range(2)], axis=0)
    o_ref[...] = (o_t.T[:t_out] * jax.nn.sigmoid(og_ref[...])).astype(o_ref.dtype)


def _fox_attention(q, ka, vt, z_og, B, T, past):
    G = GROUP_W
    pw = PAIR_W
    tq = max(_row_tile(T, FOX_Q_COLS), V7X_LANES)
    tqp = -(-T // tq) * tq
    nq = tqp // tq
    t_out = min(tq, T)
    tk = FOX_K_ROWS
    tkp = ka.shape[2]
    if q.ndim == 2:
        q = q.reshape(B, T, HEAD_PAIRS, pw).transpose(0, 2, 3, 1)
    qt = jnp.pad(q, ((0, 0), (0, 0), (0, 0), (0, tqp - T)))
    return pl.pallas_call(
        functools.partial(_fox_attn_body, past=past, tk=tk, t_real=T),
        grid=(B, HEAD_PAIRS, nq),
        in_specs=[
            pl.BlockSpec((1, 1, pw, tq), lambda b, p, i: (b, p, 0, i)),
            pl.BlockSpec((1, 2, tkp, 2 * FOX_HD), lambda b, p, i: (b, p, 0, 0)),
            pl.BlockSpec((1, 1, pw, tkp), lambda b, p, i: (b, p, 0, 0)),
            pl.BlockSpec((t_out, pw), lambda b, p, i: (b * nq + i, p)),
        ],
        out_specs=pl.BlockSpec((t_out, pw), lambda b, p, i: (b * nq + i, p)),
        out_shape=jax.ShapeDtypeStruct((B * T, G), BF16),
        scratch_shapes=[pltpu.VMEM((2, FOX_HD, tq), F32), pltpu.VMEM((2, 1, tq), F32), pltpu.VMEM((2, 1, tq), F32)],
        compiler_params=_params(("parallel", "parallel", "arbitrary"), 40),
        name="fox_attn",
    )(qt, ka, vt, z_og)


HG_CHUNK = 64
HG_STEP_CHUNKS = 16


def _hgrn_body(z_ref, lb_ref, s0_ref, ng_ref, o_ref, so_ref, st_ref, *, c):
    G = GROUP_W
    rows = z_ref.shape[0]
    nch = rows // c
    dk = G // HG_HEADS

    @pl.when(pl.program_id(1) == 0)
    def _():
        st_ref[...] = s0_ref[0]

    lb = lb_ref[...]
    f = lb + (1.0 - lb) * jax.nn.sigmoid(z_ref[:, G:2 * G])
    kx = 1.0 - f
    crow, ccol = _tri_masks(c)
    incl = ccol <= crow
    gs = _chunk_cumsum(jnp.log(f), c)
    qg_all = z_ref[:, 0:G] * jnp.exp(gs)
    kg_all = kx * jnp.exp(-gs)
    HS = range(HG_HEADS)
    units = [(cc, h) for cc in range(nch) for h in HS]
    US = range(len(units))
    rsl = [slice(cc * c, (cc + 1) * c) for cc, _ in units]
    lsl = [slice(h * dk, (h + 1) * dk) for _, h in units]
    g_last = [gs[(cc + 1) * c - 1:(cc + 1) * c, lsl[u]] for u, (cc, _) in enumerate(units)]
    vv = [z_ref[rsl[u], 2 * G + h * dk:2 * G + (h + 1) * dk] for u, (_, h) in enumerate(units)]
    A = [jnp.where(incl, _dot_lo(qg_all[rsl[u], lsl[u]], kg_all[rsl[u], lsl[u]], _NT), 0.0) for u in US]
    av = [_dot_lo(A[u], vv[u]) for u in US]
    kd = [kx[rsl[u], lsl[u]] * jnp.exp(g_last[u] - gs[rsl[u], lsl[u]]) for u in US]
    upd = [_dot_lo(vv[u], kd[u], _TN) for u in US]
    st = [st_ref[h] for h in HS]
    o = [None for _ in US]
    for cc in range(nch):
        for h in HS:
            u = cc * HG_HEADS + h
            o[u] = _dot_lo(qg_all[rsl[u], lsl[u]], st[h], _NT) + av[u]
        st = [st[h] * jnp.exp(g_last[cc * HG_HEADS + h]) + upd[cc * HG_HEADS + h] for h in HS]
    for h in HS:
        st_ref[h] = st[h]
    for u, (_, h) in enumerate(units):
        hg = z_ref[rsl[u], 3 * G + h * dk:3 * G + (h + 1) * dk]
        o_ref[rsl[u], lsl[u]] = (_rms(o[u], ng_ref[:, lsl[u]]) * (hg * jax.nn.sigmoid(hg))).astype(o_ref.dtype)

    @pl.when(pl.program_id(1) == pl.num_programs(1) - 1)
    def _():
        so_ref[0] = st_ref[...]


def _hgrn2(z_hg, lb, S0, B, T, P):
    G = GROUP_W
    c = min(HG_CHUNK, T)
    rows = _row_tile(T, c * HG_STEP_CHUNKS)
    nc = T // rows
    dk = G // HG_HEADS
    st_spec = pl.BlockSpec((1, HG_HEADS, dk, dk), lambda b, i: (b, 0, 0, 0))
    out, so = pl.pallas_call(
        functools.partial(_hgrn_body, c=c),
        grid=(B, nc),
        in_specs=[pl.BlockSpec((rows, 4 * G), lambda b, i: (b * nc + i, 0)),
                  pl.BlockSpec((1, G), lambda b, i: (0, 0)), st_spec, pl.BlockSpec((1, G), lambda b, i: (0, 0))],
        out_specs=[pl.BlockSpec((rows, G), lambda b, i: (b * nc + i, 0)), st_spec],
        out_shape=[jax.ShapeDtypeStruct((B * T, G), BF16), jax.ShapeDtypeStruct(S0.shape, F32)],
        scratch_shapes=[pltpu.VMEM((HG_HEADS, dk, dk), F32)],
        compiler_params=_params(("parallel", "arbitrary"), 32),
        name="hgrn2",
    )(z_hg, lb.reshape(1, G), jnp.swapaxes(S0, -1, -2), P['hg_norm_g'].reshape(1, G))
    return out, jnp.swapaxes(so, -1, -2)


RW_CHUNK = 64
RW_SUB = 16
RW_LDIAG_CHUNKS = 8
RW_MAIN_CHUNKS = 2

_NT = (((1,), (1,)), ((), ()))
_TN = (((0,), (0,)), ((), ()))
_NN = (((1,), (0,)), ((), ()))


def _split3(x):
    h1 = x.astype(BF16)
    r1 = x - h1.astype(F32)
    h2 = r1.astype(BF16)
    h3 = (r1 - h2.astype(F32)).astype(BF16)
    return h1, h2, h3


def _dot_lo(a, b, dims=_NN):
    return lax.dot_general(a.astype(BF16), b.astype(BF16), dims, preferred_element_type=F32)


def _dot_hi(a, b, dims=_NN):
    ah = a.astype(BF16)
    al = (a - ah.astype(F32)).astype(BF16)
    bh = b.astype(BF16)
    bl = (b - bh.astype(F32)).astype(BF16)
    d = functools.partial(lax.dot_general, dimension_numbers=dims, preferred_element_type=F32)
    return d(ah, bh) + (d(al, bh) + d(ah, bl))


def _dot_exact_rhs(a, b):
    h1, h2, h3 = _split3(a)
    d = functools.partial(jnp.dot, preferred_element_type=F32)
    return d(h1, b) + (d(h2, b) + d(h3, b))


def _dot_exact_lhs(a, b):
    h1, h2, h3 = _split3(b)
    d = functools.partial(jnp.dot, preferred_element_type=F32)
    return d(a, h1) + (d(a, h2) + d(a, h3))


def _softplus(x):
    return jnp.maximum(x, 0.0) + jnp.log1p(jnp.exp(-jnp.abs(x)))


def _rw_prep_body(z_ref, shift_ref, mu_ref, w0_ref, w2_ref, a0_ref, a2_ref, g2_ref, kk_ref, ka_ref, ones_ref,
                  r_ref, lw_ref, k_ref, v_ref, kap_ref, bet_ref, g_ref, prev_ref):
    G = GROUP_W

    @pl.when(pl.program_id(1) == 0)
    def _():
        prev_ref[0:1, :] = shift_ref[0]

    z = z_ref[...]
    tt = z.shape[0]
    row = lax.broadcasted_iota(jnp.int32, z.shape, 0)
    shifted = jnp.where(row == 0, prev_ref[0:1, :], pltpu.roll(z, 1, axis=0))
    prev_ref[0:1, :] = z[tt - 1:tt, :]
    zm = z + (shifted - z) * mu_ref[...]
    r, k, v = zm[:, 0:G], zm[:, G:2 * G], zm[:, 2 * G:3 * G]
    o = 3 * G
    wd = zm[:, o:o + RW_DECAY_LORA]
    ad = zm[:, o + RW_DECAY_LORA:o + RW_DECAY_LORA + RW_A_LORA]
    gd = zm[:, o + RW_DECAY_LORA + RW_A_LORA:]
    w = -_softplus(-(w0_ref[...] + _dot_lo(jnp.tanh(wd), w2_ref[...]))) - 0.5
    a = jax.nn.sigmoid(a0_ref[...] + _dot_lo(ad, a2_ref[...]))
    kk = k * kk_ref[...]
    ss = _dot_exact_rhs(kk * kk, ones_ref[...])
    kap = kk / jnp.maximum(jnp.sqrt(ss), 1e-12)
    r_ref[...] = r
    lw_ref[...] = -jnp.exp(w)
    k_ref[...] = k * (1.0 + (a - 1.0) * ka_ref[...])
    v_ref[...] = v
    kap_ref[...] = kap
    bet_ref[...] = kap * a
    g_ref[...] = _dot_lo(jax.nn.sigmoid(gd), g2_ref[...])


def _rw_prep(zr, shift, B, T, P):
    n, cols = zr.shape
    G = GROUP_W
    tt = _row_tile(T, PREP_ROWS)
    nt = T // tt
    ones_bd = jnp.kron(jnp.eye(RW_HEADS, dtype=F32), jnp.ones((RW_HD, RW_HD), F32)).astype(BF16)
    row = lambda x: x.reshape(1, -1)
    full = lambda shape: pl.BlockSpec(shape, lambda b, i: (0,) * len(shape))
    tile = pl.BlockSpec((tt, G), lambda b, i: (b * nt + i, 0))
    return pl.pallas_call(
        _rw_prep_body,
        grid=(B, nt),
        in_specs=[
            pl.BlockSpec((tt, cols), lambda b, i: (b * nt + i, 0)),
            pl.BlockSpec((1, 1, cols), lambda b, i: (b, 0, 0)),
            full((1, cols)), full((1, G)), full((RW_DECAY_LORA, G)), full((1, G)), full((RW_A_LORA, G)),
            full((RW_GATE_LORA, G)), full((1, G)), full((1, G)), full((G, G)),
        ],
        out_specs=[tile] * 7,
        out_shape=[jax.ShapeDtypeStruct((n, G), F32)] * 7,
        scratch_shapes=[pltpu.VMEM((8, cols), F32)],
        compiler_params=_params(("parallel", "arbitrary"), 40),
        name="rwkv_prep",
    )(zr, shift.reshape(B, 1, cols), row(P['rw_mu']), row(P['rw_w0']), P['rw_w2'].astype(BF16), row(P['rw_a0']),
      P['rw_a2'].astype(BF16), P['rw_g2'].astype(BF16), row(P['rw_kk']), row(P['rw_ka']), ones_bd)


def _rw_scaled(lw, kap, bet, c):
    cs = _chunk_cumsum(lw, c)
    return cs, kap * jnp.exp(cs - lw), bet * jnp.exp(-cs)


def _tri_masks(c):
    row = lax.broadcasted_iota(jnp.int32, (c, c), 0)
    col = lax.broadcasted_iota(jnp.int32, (c, c), 1)
    return row, col


def _chunk_cumsum(x, c):
    row, col = _tri_masks(c)
    tri = jnp.where(col <= row, 1.0, 0.0).astype(BF16)
    parts = [_dot_exact_lhs(tri, x[i:i + c]) for i in range(0, x.shape[0], c)]
    return parts[0] if len(parts) == 1 else jnp.concatenate(parts, axis=0)


def _rw_ldiag_body(lw_ref, kap_ref, bet_ref, o_ref, *, c):
    rows = lw_ref.shape[0]
    _, kk_all, bt_all = _rw_scaled(lw_ref[...], kap_ref[...], bet_ref[...], c)
    srow, scol = _tri_masks(RW_SUB)
    units = [(cc, h) for cc in range(rows // c) for h in range(RW_HEADS)]
    Ls = [_dot_lo(kk_all[cc * c:(cc + 1) * c, h * RW_HD:(h + 1) * RW_HD],
                  bt_all[cc * c:(cc + 1) * c, h * RW_HD:(h + 1) * RW_HD], _NT) for cc, h in units]
    for (cc, h), L in zip(units, Ls):
        for b in range(c // RW_SUB):
            rs = slice(b * RW_SUB, (b + 1) * RW_SUB)
            o_ref[cc * c + b * RW_SUB:cc * c + (b + 1) * RW_SUB, h * RW_SUB:(h + 1) * RW_SUB] = (
                jnp.where(scol < srow, L[rs, rs], 0.0))


def _rw_inv_body(l_ref, t_ref, a_ref, b_ref):
    n = RW_SUB
    nblk = l_ref.shape[0] // n
    for t in range(n):
        a_ref[t] = l_ref[pl.ds(t, nblk, stride=n), :].T
    entry = lambda ref, t, s: ref.at[t, pl.ds(s, RW_HEADS, stride=n), :]
    one = jnp.ones((RW_HEADS, nblk), F32)
    zero = jnp.zeros((RW_HEADS, nblk), F32)
    for t in range(n):
        for s in range(n):
            if s > t:
                entry(b_ref, t, s)[...] = zero
            elif s == t:
                entry(b_ref, t, s)[...] = one
            else:
                acc = entry(a_ref, t, s)[...]
                for j in range(s + 1, t):
                    acc = acc + entry(a_ref, t, j)[...] * entry(b_ref, j, s)[...]
                entry(b_ref, t, s)[...] = -acc
    for t in range(n):
        t_ref[pl.ds(t, nblk, stride=n), :] = b_ref[t].T


def _rw_main_body(r_ref, lw_ref, k_ref, v_ref, kap_ref, bet_ref, g_ref, td_ref, h0_ref, rk_ref, lng_ref, lnb_ref,
                  o_ref, hout_ref, h_ref, *, c):
    ci = pl.program_id(1)
    rows = r_ref.shape[0]
    nb = c // RW_SUB

    @pl.when(ci == 0)
    def _():
        h_ref[...] = h0_ref[0]

    crow, ccol = _tri_masks(c)
    strict = ccol < crow
    incl = ccol <= crow
    lw = lw_ref[...]
    cs, kk_all, bt_all = _rw_scaled(lw, kap_ref[...], bet_ref[...], c)
    gi = jnp.exp(-cs)
    gg = jnp.exp(cs)
    kt_all = k_ref[...] * gi
    rt_all = r_ref[...] * gg
    bonus_all = r_ref[...] * k_ref[...] * rk_ref[...]
    hrow = lax.broadcasted_iota(jnp.int32, (RW_HD, RW_HD), 0)
    hcol = lax.broadcasted_iota(jnp.int32, (RW_HD, RW_HD), 1)
    HS = range(RW_HEADS)
    units = [(cc, h) for cc in range(rows // c) for h in HS]
    US = range(len(units))
    rsl = [slice(cc * c, (cc + 1) * c) for cc, _ in units]
    lsl = [slice(h * RW_HD, (h + 1) * RW_HD) for _, h in units]
    Kk = [kk_all[rsl[u], lsl[u]] for u in US]
    Bt = [bt_all[rsl[u], lsl[u]] for u in US]
    Kt = [kt_all[rsl[u], lsl[u]] for u in US]
    Rt = [rt_all[rsl[u], lsl[u]] for u in US]
    vv = [v_ref[rsl[u], lsl[u]] for u in US]
    Lm = [jnp.where(strict, _dot_lo(Kk[u], Bt[u], _NT), 0.0) for u in US]
    A1 = [jnp.where(strict, _dot_lo(Kk[u], Kt[u], _NT), 0.0) for u in US]
    A4 = [jnp.where(incl, _dot_lo(Rt[u], Bt[u], _NT), 0.0) for u in US]
    A3 = [jnp.where(incl, _dot_lo(Rt[u], Kt[u], _NT), 0.0) for u in US]
    X = [jnp.concatenate([Kk[u], _dot_lo(A1[u], vv[u])], axis=1) for u in US]
    zs = [[] for _ in US]
    for b in range(nb):
        rs = slice(b * RW_SUB, (b + 1) * RW_SUB)
        rhs = [X[u][rs] for u in US]
        if b:
            rhs = [rhs[u] - _dot_lo(Lm[u][rs, 0:b * RW_SUB], jnp.concatenate(zs[u], axis=0)) for u in US]
        for u, (cc, h) in enumerate(units):
            tbb = td_ref[cc * c + b * RW_SUB:cc * c + (b + 1) * RW_SUB, h * RW_SUB:(h + 1) * RW_SUB]
            zs[u].append(_dot_lo(tbb, rhs[u]))
    Z = [jnp.concatenate(zs[u], axis=0) if nb > 1 else zs[u][0] for u in US]
    A4Z = [_dot_lo(A4[u], Z[u]) for u in US]
    Rhat = [Rt[u] - A4Z[u][:, :RW_HD] for u in US]
    Yhat = [_dot_lo(A3[u], vv[u]) - A4Z[u][:, RW_HD:] for u in US]
    gC = [gg[(cc + 1) * c - 1:(cc + 1) * c, lsl[u]] for u, (cc, _) in enumerate(units)]
    MN = [_dot_lo(Bt[u] * gC[u], Z[u], _TN) for u in US]
    Mp = [jnp.where(hrow == hcol, gC[u], 0.0) - MN[u][:, :RW_HD] for u in US]
    Np = [_dot_lo(Kt[u] * gC[u], vv[u], _TN) - MN[u][:, RW_HD:] for u in US]
    H = [h_ref[h] for h in HS]
    ys = [None for _ in US]
    for cc in range(rows // c):
        for h in HS:
            u = cc * RW_HEADS + h
            ys[u] = _dot_lo(Rhat[u], H[h]) + Yhat[u]
        H = [_dot_hi(Mp[cc * RW_HEADS + h], H[h]) + Np[cc * RW_HEADS + h] for h in HS]
    for h in HS:
        h_ref[h] = H[h]
    for u in US:
        y = ys[u]
        mu = jnp.mean(y, axis=-1, keepdims=True)
        var = jnp.mean(jnp.square(y - mu), axis=-1, keepdims=True)
        yn = (y - mu) * lax.rsqrt(var + RW_LN_EPS) * lng_ref[:, lsl[u]] + lnb_ref[:, lsl[u]]
        yn = yn + jnp.sum(bonus_all[rsl[u], lsl[u]], axis=-1, keepdims=True) * vv[u]
        o_ref[rsl[u], lsl[u]] = (yn * g_ref[rsl[u], lsl[u]]).astype(o_ref.dtype)

    @pl.when(ci == pl.num_programs(1) - 1)
    def _():
        hout_ref[0] = h_ref[...]


def _rwkv7(zr, shift, S0, B, T, P):
    G = GROUP_W
    n = B * T
    r, lw, k, v, kap, bet, g = _rw_prep(zr, shift, B, T, P)
    c = min(RW_CHUNK, T)
    rows_l = _row_tile(T, c * RW_LDIAG_CHUNKS)
    rows_m = _row_tile(T, c * RW_MAIN_CHUNKS)
    nl, nc = T // rows_l, T // rows_m
    tile_l = pl.BlockSpec((rows_l, G), lambda b, i: (b * nl + i, 0))
    tile = pl.BlockSpec((rows_m, G), lambda b, i: (b * nc + i, 0))
    ld = pl.pallas_call(
        functools.partial(_rw_ldiag_body, c=c),
        grid=(B, nl),
        in_specs=[tile_l] * 3,
        out_specs=pl.BlockSpec((rows_l, RW_HEADS * RW_SUB), lambda b, i: (b * nl + i, 0)),
        out_shape=jax.ShapeDtypeStruct((n, RW_HEADS * RW_SUB), F32),
        compiler_params=_params(("parallel", "parallel"), 32),
        name="rwkv_ldiag",
    )(lw, kap, bet)
    rows_i = V7X_LANES * RW_SUB
    npad = -(-n // rows_i) * rows_i
    inv_spec = pl.BlockSpec((rows_i, RW_HEADS * RW_SUB), lambda i: (i, 0))
    inv_scratch = pltpu.VMEM((RW_SUB, RW_HEADS * RW_SUB, V7X_LANES), F32)
    td = pl.pallas_call(
        _rw_inv_body,
        grid=(npad // rows_i,),
        in_specs=[inv_spec],
        out_specs=inv_spec,
        out_shape=jax.ShapeDtypeStruct((npad, RW_HEADS * RW_SUB), F32),
        scratch_shapes=[inv_scratch, inv_scratch],
        compiler_params=_params(("parallel",), 32),
        name="rwkv_inv",
    )(jnp.pad(ld, ((0, npad - n), (0, 0))))[:n]
    h0 = jnp.swapaxes(S0, -1, -2)
    prow = lambda x: pl.BlockSpec((1, G), lambda b, i: (0, 0))
    st_spec = pl.BlockSpec((1, RW_HEADS, RW_HD, RW_HD), lambda b, i: (b, 0, 0, 0))
    out, hl = pl.pallas_call(
        functools.partial(_rw_main_body, c=c),
        grid=(B, nc),
        in_specs=[tile] * 7 + [pl.BlockSpec((rows_m, RW_HEADS * RW_SUB), lambda b, i: (b * nc + i, 0)), st_spec,
                               prow(0), prow(0), prow(0)],
        out_specs=[tile, st_spec],
        out_shape=[jax.ShapeDtypeStruct((n, G), BF16), jax.ShapeDtypeStruct(S0.shape, F32)],
        scratch_shapes=[pltpu.VMEM((RW_HEADS, RW_HD, RW_HD), F32)],
        compiler_params=_params(("parallel", "arbitrary"), 32),
        name="rwkv_main",
    )(r, lw, k, v, kap, bet, g, td, h0, P['rw_rk'].reshape(1, G), P['rw_ln_g'].reshape(1, G),
      P['rw_ln_b'].reshape(1, G))
    return out, zr.reshape(B, T, -1)[:, -1], jnp.swapaxes(hl, -1, -2)


def _even_mixer(x2, B, T, g, st, P):
    conv_buf, lru_h, k_past, v_past, lf_past = st
    G = GROUP_W
    z_rg, z_qkv, z_og, z_fl = _norm_matmul(x2, g, P['e_w_in'], (2 * G, 3 * G, G, V7X_LANES))
    rnn_out, conv_new, h_last = _lru(z_rg, conv_buf, lru_h, B, T, P)
    qb, kn, kb, v, vb, lf = _fox_prep(z_qkv, z_fl, B, T, P)
    past = k_past.shape[1]
    lf_all = lf.reshape(B, T, V7X_LANES)
    kb_all, vb_all = kb.reshape(B, T, G), vb.reshape(B, T, G)
    if past:
        lf_all = jnp.concatenate([jnp.pad(lf_past, ((0, 0), (0, 0), (0, V7X_LANES - FOX_HEADS))), lf_all], axis=1)
        kb_all = jnp.concatenate([k_past.reshape(B, past, G).astype(BF16), kb_all], axis=1)
        vb_all = jnp.concatenate([v_past.reshape(B, past, G).astype(BF16), vb_all], axis=1)
    tail = ((0, 0), (0, -(past + T) % FOX_K_ROWS), (0, 0))
    ka, vt = _fox_keys(jnp.pad(lf_all, tail), jnp.pad(kb_all, tail), jnp.pad(vb_all, tail))
    fox_out = _fox_attention(qb, ka, vt, z_og, B, T, past)
    heads = lambda t: t.reshape(B, T, FOX_HEADS, FOX_HD)
    return (rnn_out, fox_out, P['e_w_out']), (conv_new, h_last, heads(kn), heads(v), lf.reshape(B, T, V7X_LANES)[..., :FOX_HEADS])


def _odd_mixer(x2, B, T, g, st, lb, P):
    S_hg, shift, S_rw = st
    G = GROUP_W
    z_hg, z_rw = _norm_matmul(x2, g, P['o_w_in'], (4 * G, P['o_w_in'].shape[1] - 4 * G))
    hg_out, S_hg_new = _hgrn2(z_hg, lb, S_hg, B, T, P)
    rw_out, shift_new, S_rw_new = _rwkv7(z_rw, shift, S_rw, B, T, P)
    return (hg_out, rw_out, P['o_w_out']), (S_hg_new, shift_new, S_rw_new)


def _trunk(x, states, W):
    lru_conv, lru_h, fox_k, fox_v, fox_lf, hg_S, rw_shift, rw_S = states
    B, T, D = x.shape
    depth = W['norm_g'].shape[0]
    sm = jax.nn.softmax(W['hg_lb_logits'], axis=0)
    lower_bounds = jnp.cumsum(sm, axis=0) - sm[0]
    x2 = x.reshape(B * T, D)
    even_new, odd_new = [], []
    for layer in range(depth):
        g = W['norm_g'][layer]
        x2 = _ffn(x2, g[0], W['ffn_w_in'][layer][0], W['ffn_w_out'][layer][0])
        if layer % 2 == 0:
            e = layer // 2
            P = {n: W[n][e] for n in ('e_w_in', 'e_w_out', 'lru_conv_w', 'lru_conv_b', 'lru_wa', 'lru_ba', 'lru_wx',
                                      'lru_bx', 'lru_lambda', 'fox_q_gain', 'fox_k_gain', 'fox_f_bias')}
            mix, new = _even_mixer(x2, B, T, g[1], (lru_conv[e], lru_h[e], fox_k[e], fox_v[e], fox_lf[e]), P)
            even_new.append(new)
        else:
            o = layer // 2
            P = {n: W[n][o] for n in ('o_w_in', 'o_w_out', 'hg_norm_g', 'rw_mu', 'rw_w0', 'rw_w2', 'rw_a0', 'rw_a2',
                                      'rw_g2', 'rw_kk', 'rw_ka', 'rw_rk', 'rw_ln_g', 'rw_ln_b')}
            mix, new = _odd_mixer(x2, B, T, g[1], (hg_S[o], rw_shift[o], rw_S[o]), lower_bounds[layer], P)
            odd_new.append(new)
        x2 = _ffn(x2, g[2], W['ffn_w_in'][layer][1], W['ffn_w_out'][layer][1], mix)
    ev = [jnp.stack([n[j] for n in even_new]) for j in range(5)]
    od = [jnp.stack([n[j] for n in odd_new]) for j in range(3)]
    return x2.reshape(B, T, D), (ev[0], ev[1], ev[2], ev[3], ev[4], od[0], od[1], od[2])


def kernel(x_prompt, x_sample, state_lru_conv, state_lru_h, cache_fox_k, cache_fox_v, cache_fox_logf,
           state_hgrn_S, state_rwkv_shift, state_rwkv_S, norm_g, ffn_w_in, ffn_w_out, e_w_in, e_w_out,
           lru_conv_w, lru_conv_b, lru_wa, lru_ba, lru_wx, lru_bx, lru_lambda, fox_q_gain, fox_k_gain,
           fox_f_bias, o_w_in, o_w_out, hg_lb_logits, hg_norm_g, rw_mu, rw_w0, rw_w2, rw_a0, rw_a2, rw_g2,
           rw_kk, rw_ka, rw_rk, rw_ln_g, rw_ln_b):
    n_even, n_odd = e_w_in.shape[0], o_w_in.shape[0]
    W = dict(norm_g=norm_g, ffn_w_in=_ffn_w_in_tiles(ffn_w_in), ffn_w_out=ffn_w_out.astype(BF16),
             e_w_in=_pad_cols(e_w_in.astype(BF16)), e_w_out=e_w_out.astype(BF16),
             lru_conv_w=lru_conv_w, lru_conv_b=lru_conv_b, lru_wa=lru_wa, lru_ba=lru_ba, lru_wx=lru_wx,
             lru_bx=lru_bx, lru_lambda=lru_lambda, fox_q_gain=fox_q_gain, fox_k_gain=fox_k_gain,
             fox_f_bias=fox_f_bias, o_w_in=o_w_in.astype(BF16), o_w_out=o_w_out.astype(BF16),
             hg_lb_logits=hg_lb_logits, hg_norm_g=hg_norm_g, rw_mu=rw_mu, rw_w0=rw_w0, rw_w2=rw_w2, rw_a0=rw_a0,
             rw_a2=rw_a2, rw_g2=rw_g2, rw_kk=rw_kk, rw_ka=rw_ka, rw_rk=rw_rk, rw_ln_g=rw_ln_g, rw_ln_b=rw_ln_b)
    nb = x_prompt.shape[0]
    dt = x_prompt.dtype
    prompt_states = (jnp.zeros((n_even, nb, CONV_W - 1, GROUP_W), dt),
                     jnp.zeros((n_even, nb, GROUP_W), dt),
                     jnp.zeros((n_even, nb, 0, FOX_HEADS, FOX_HD), dt),
                     jnp.zeros((n_even, nb, 0, FOX_HEADS, FOX_HD), dt),
                     jnp.zeros((n_even, nb, 0, FOX_HEADS), dt),
                     jnp.zeros((n_odd, nb, HG_HEADS, GROUP_W // HG_HEADS, GROUP_W // HG_HEADS), dt),
                     jnp.zeros((n_odd, nb, rw_mu.shape[1]), dt),
                     jnp.zeros((n_odd, nb, RW_HEADS, RW_HD, RW_HD), dt))
    sample_states = (state_lru_conv, state_lru_h, cache_fox_k, cache_fox_v, cache_fox_logf,
                     state_hgrn_S, state_rwkv_shift, state_rwkv_S)
    y_prompt, p_new = _trunk(x_prompt, prompt_states, W)
    y_sample, s_new = _trunk(x_sample, sample_states, W)
    lru_conv_p, lru_h_p, fox_k_p, fox_v_p, fox_logf_p, hgrn_S_p, rwkv_shift_p, rwkv_S_p = p_new
    lru_conv_s, lru_h_s, fox_k_s, fox_v_s, fox_logf_s, hgrn_S_s, rwkv_shift_s, rwkv_S_s = s_new
    return (y_prompt, y_sample, lru_conv_p, lru_conv_s, lru_h_p, lru_h_s, fox_k_p, fox_k_s, fox_v_p, fox_v_s,
            fox_logf_p, fox_logf_s, hgrn_S_p, hgrn_S_s, rwkv_shift_p, rwkv_shift_s, rwkv_S_p, rwkv_S_s)
```

```python
import functools

import jax
import jax.numpy as jnp
from jax import lax
from jax.experimental import pallas as pl
from jax.experimental.pallas import tpu as pltpu

F32 = jnp.float32
BF16 = jnp.bfloat16

NORM_EPS = 1e-6
GROUP_W = 512
CONV_W = 4
LRU_C = 8.0
FOX_HEADS = 8
FOX_HD = 64
HG_HEADS = 4
RW_HEADS = 8
RW_HD = 64
RW_DECAY_LORA = 64
RW_A_LORA = 64
RW_GATE_LORA = 128
RW_LN_EPS = 64e-5

V7X_LANES = 128
V7X_MXU = 256
FFN_COL_TILE = 1408
PREP_ROWS = 512
FFN_ROW_TILE = 1024
FFN_VMEM_MIB = 60


def _row_tile(n, want):
    t = min(n, want)
    while n % t:
        t //= 2
    return t


def _params(sem, vmem_mib):
    return pltpu.CompilerParams(dimension_semantics=sem, vmem_limit_bytes=vmem_mib << 20)


def _pad_cols(w):
    pad = -w.shape[-1] % V7X_LANES
    return jnp.pad(w, [(0, 0)] * (w.ndim - 1) + [(0, pad)])


def _head_ones(hd):
    return jnp.kron(jnp.eye(V7X_MXU // hd, dtype=F32), jnp.ones((hd, hd), F32)).astype(BF16)


def _rms(x, g):
    return x * lax.rsqrt(jnp.mean(x * x, axis=-1, keepdims=True) + NORM_EPS) * g


def _ffn_body(*refs, mixed):
    if mixed:
        x_ref, a_ref, b_ref, wa_ref, wb_ref, g_ref, wi_ref, wo_ref, o_ref, h_ref, acc_ref = refs
    else:
        x_ref, g_ref, wi_ref, wo_ref, o_ref, h_ref, acc_ref = refs
    j = pl.program_id(1)

    @pl.when(j == 0)
    def _():
        x = x_ref[...]
        if mixed:
            x = x + jnp.dot(a_ref[...], wa_ref[...], preferred_element_type=F32)
            x = x + jnp.dot(b_ref[...], wb_ref[...], preferred_element_type=F32)
        h_ref[...] = _rms(x, g_ref[...]).astype(BF16)
        acc_ref[...] = 2.0 * x

    tf = wo_ref.shape[0]
    gu = jnp.dot(h_ref[...], wi_ref[...], preferred_element_type=F32)
    gate, up = gu[:, :tf], gu[:, tf:]
    act = (gate * jax.nn.sigmoid(gate) * up).astype(BF16)
    acc_ref[...] += jnp.dot(act, wo_ref[...], preferred_element_type=F32)

    @pl.when(j == pl.num_programs(1) - 1)
    def _():
        o_ref[...] = 0.5 * acc_ref[...]


def _cast_body(x_ref, o_ref):
    o_ref[...] = x_ref[...].astype(o_ref.dtype)


def _ffn_w_in_tiles(w_in):
    *lead, d, f2 = w_in.shape
    tf = FFN_COL_TILE
    nf = f2 // 2 // tf
    w = w_in.reshape(-1, d, f2)
    out = pl.pallas_call(
        _cast_body,
        grid=(w.shape[0], nf, 2),
        in_specs=[pl.BlockSpec((1, d, tf), lambda i, j, gu: (i, 0, gu * nf + j))],
        out_specs=pl.BlockSpec((1, d, tf), lambda i, j, gu: (i, 0, 2 * j + gu)),
        out_shape=jax.ShapeDtypeStruct(w.shape, BF16),
        compiler_params=_params(("parallel", "parallel", "parallel"), 32),
        name="ffn_weight_tiles",
    )(w)
    return out.reshape(*lead, d, f2)


def _ffn(x, g, w_in, w_out, mix=None):
    n, d = x.shape
    f = w_out.shape[0]
    tm = _row_tile(n, FFN_ROW_TILE)
    tf = FFN_COL_TILE
    nf = f // tf
    row_spec = lambda w: pl.BlockSpec((tm, w), lambda i, j: (i, 0))
    args, specs = [x], [row_spec(d)]
    if mix is not None:
        a, b, w = mix
        ga, gb = a.shape[1], b.shape[1]
        args += [a, b, w[:ga], w[ga:]]
        specs += [row_spec(ga), row_spec(gb), pl.BlockSpec((ga, d), lambda i, j: (0, 0)),
                  pl.BlockSpec((gb, d), lambda i, j: (0, 0))]
    args += [g.reshape(1, d), w_in, w_out]
    specs += [pl.BlockSpec((1, d), lambda i, j: (0, 0)), pl.BlockSpec((d, 2 * tf), lambda i, j: (0, j)),
              pl.BlockSpec((tf, d), lambda i, j: (j, 0))]
    return pl.pallas_call(
        functools.partial(_ffn_body, mixed=mix is not None),
        grid=(n // tm, nf),
        in_specs=specs,
        out_specs=row_spec(d),
        out_shape=jax.ShapeDtypeStruct((n, d), F32),
        scratch_shapes=[pltpu.VMEM((tm, d), BF16), pltpu.VMEM((tm, d), F32)],
        compiler_params=_params(("parallel", "arbitrary"), FFN_VMEM_MIB),
        name="ffn",
    )(*args)


def _norm_matmul_body(x_ref, g_ref, w_ref, *o_refs):
    h = _rms(x_ref[...], g_ref[...]).astype(BF16)
    z = jnp.dot(h, w_ref[...], preferred_element_type=F32)
    start = 0
    for o_ref in o_refs:
        width = o_ref.shape[1]
        o_ref[...] = z[:, start:start + width]
        start += width


def _norm_matmul(x, g, w, widths):
    n, d = x.shape
    c = w.shape[1]
    assert sum(widths) == c and all(wd % V7X_LANES == 0 for wd in widths)
    tm = _row_tile(n, 512)
    return pl.pallas_call(
        _norm_matmul_body,
        grid=(n // tm,),
        in_specs=[
            pl.BlockSpec((tm, d), lambda i: (i, 0)),
            pl.BlockSpec((1, d), lambda i: (0, 0)),
            pl.BlockSpec((d, c), lambda i: (0, 0)),
        ],
        out_specs=[pl.BlockSpec((tm, wd), lambda i: (i, 0)) for wd in widths],
        out_shape=[jax.ShapeDtypeStruct((n, wd), F32) for wd in widths],
        compiler_params=_params(("parallel",), 48),
        name="norm_matmul",
    )(x, g.reshape(1, d), w)


LRU_ROWS = 256
CONV_PAD = 8


def _expm1(x):
    series = x * (1.0 + x * (1 / 2 + x * (1 / 6 + x * (1 / 24 + x * (1 / 120 + x * (1 / 720 + x * (1 / 5040 + x * (1 / 40320))))))))
    return jnp.where(jnp.abs(x) < 0.25, series, jnp.exp(x) - 1.0)


def _shift_rows(x, s, fill):
    row = lax.broadcasted_iota(jnp.int32, x.shape, 0)
    return jnp.where(row >= s, pltpu.roll(x, s, axis=0), fill)


def _lru_body(z_ref, buf_ref, h0_ref, cw_ref, cb_ref, wa_ref, ba_ref, wx_ref, bx_ref, lam_ref,
              o_ref, bufo_ref, ho_ref, x_ref, hc_ref):
    G = GROUP_W
    tt = z_ref.shape[0]

    @pl.when(pl.program_id(1) == 0)
    def _():
        x_ref[0:CONV_PAD, :] = buf_ref[0]
        hc_ref[...] = jnp.broadcast_to(h0_ref[0], hc_ref.shape)

    x_ref[CONV_PAD:CONV_PAD + tt, :] = z_ref[:, 0:G]
    xc = cb_ref[...]
    for j in range(CONV_W):
        lo = CONV_PAD - (CONV_W - 1) + j
        xc = xc + x_ref[lo:lo + tt, :] * cw_ref[j:j + 1, :]
    hist = x_ref[tt:tt + CONV_PAD, :]
    x_ref[0:CONV_PAD, :] = hist
    bufo_ref[0] = hist

    xb = xc.astype(BF16)
    r = jax.nn.sigmoid(jnp.dot(xb, wa_ref[...], preferred_element_type=F32) + ba_ref[...])
    ig = jax.nn.sigmoid(jnp.dot(xb, wx_ref[...], preferred_element_type=F32) + bx_ref[...])
    log_a = (-LRU_C * _softplus(-lam_ref[...])) * r
    a = jnp.exp(log_a)
    b = jnp.sqrt(-_expm1(2.0 * log_a)) * (ig * xc)
    s = 1
    while s < tt:
        if s % 8:
            b = a * _shift_rows(b, s, 0.0) + b
            a = a * _shift_rows(a, s, 1.0)
        else:
            b = jnp.concatenate([b[:s], a[s:] * b[:tt - s] + b[s:]], axis=0)
            a = jnp.concatenate([a[:s], a[s:] * a[:tt - s]], axis=0)
        s *= 2
    h = a * hc_ref[0:1, :] + b
    hc_ref[...] = jnp.broadcast_to(h[tt - 1:tt, :], hc_ref.shape)
    ho_ref[0] = h[tt - 1:tt, :]
    o_ref[...] = (jax.nn.gelu(z_ref[:, G:2 * G]) * h).astype(o_ref.dtype)


def _block_diag_dense(w):
    nb, bs, _ = w.shape
    eye = jnp.eye(nb, dtype=w.dtype)
    return (eye[:, None, :, None] * w[:, :, None, :]).reshape(nb * bs, nb * bs)


def _lru(z_rg, conv_buf, h0, B, T, P):
    G = GROUP_W
    n = B * T
    tt = _row_tile(T, LRU_ROWS)
    nt = T // tt
    buf = jnp.pad(conv_buf, ((0, 0), (CONV_PAD - (CONV_W - 1), 0), (0, 0)))
    cw = jnp.pad(P['lru_conv_w'], ((0, CONV_PAD - CONV_W), (0, 0)))
    row = lambda x: x.reshape(1, G)
    full = lambda shape: pl.BlockSpec(shape, lambda b, i: (0,) * len(shape))
    out, bufo, ho = pl.pallas_call(
        _lru_body,
        grid=(B, nt),
        in_specs=[
            pl.BlockSpec((tt, 2 * G), lambda b, i: (b * nt + i, 0)),
            pl.BlockSpec((1, CONV_PAD, G), lambda b, i: (b, 0, 0)),
            pl.BlockSpec((1, 1, G), lambda b, i: (b, 0, 0)),
            full((CONV_PAD, G)), full((1, G)), full((G, G)), full((1, G)), full((G, G)), full((1, G)), full((1, G)),
        ],
        out_specs=[
            pl.BlockSpec((tt, G), lambda b, i: (b * nt + i, 0)),
            pl.BlockSpec((1, CONV_PAD, G), lambda b, i: (b, 0, 0)),
            pl.BlockSpec((1, 1, G), lambda b, i: (b, 0, 0)),
        ],
        out_shape=[jax.ShapeDtypeStruct((n, G), BF16), jax.ShapeDtypeStruct((B, CONV_PAD, G), F32),
                   jax.ShapeDtypeStruct((B, 1, G), F32)],
        scratch_shapes=[pltpu.VMEM((tt + CONV_PAD, G), F32), pltpu.VMEM((8, G), F32)],
        compiler_params=_params(("parallel", "arbitrary"), 32),
        name="lru",
    )(z_rg, buf, h0.reshape(B, 1, G), cw, row(P['lru_conv_b']), _block_diag_dense(P['lru_wa']).astype(BF16),
      row(P['lru_ba']), _block_diag_dense(P['lru_wx']).astype(BF16), row(P['lru_bx']), row(P['lru_lambda']))
    return out, bufo[:, CONV_PAD - (CONV_W - 1):], ho.reshape(B, G)


FOX_Q_COLS = 2048
FOX_K_ROWS = 512
FOX_F_SPLIT = 3
FOX_NEG = -1e30
LOG2E = 1.4426950408889634
HEAD_PAIRS = FOX_HEADS // 2
PAIR_W = 2 * FOX_HD
FOX_CUMSUM_ROWS = 256
FOX_KEY_TILE = 512


def _fox_prep_body(z_ref, fl_ref, qg_ref, kg_ref, fb_ref, ones_ref, q_ref, k_ref, kb_ref, v_ref, vb_ref, lf_ref,
                   *, q_transposed):
    G = GROUP_W
    q, k, v = z_ref[:, 0:G], z_ref[:, G:2 * G], z_ref[:, 2 * G:3 * G]
    inv = 1.0 / FOX_HD
    qn = q * lax.rsqrt(_head_sums(q * q, ones_ref[...]) * inv + NORM_EPS) * qg_ref[...]
    kn = k * lax.rsqrt(_head_sums(k * k, ones_ref[...]) * inv + NORM_EPS) * kg_ref[...]
    qs = qn * (LOG2E * FOX_HD ** -0.5)
    if q_transposed:
        for p in range(HEAD_PAIRS):
            q_ref[0, p] = qs[:, p * PAIR_W:(p + 1) * PAIR_W].T.astype(BF16)
    else:
        q_ref[...] = qs.astype(BF16)
    tt = z_ref.shape[0]
    for h in range(FOX_HEADS):
        k_ref[pl.ds(h, tt, stride=FOX_HEADS), :] = kn[:, h * FOX_HD:(h + 1) * FOX_HD]
        v_ref[pl.ds(h, tt, stride=FOX_HEADS), :] = v[:, h * FOX_HD:(h + 1) * FOX_HD]
    kb_ref[...] = kn.astype(BF16)
    vb_ref[...] = v.astype(BF16)
    x = fl_ref[...] + fb_ref[...]
    lf_ref[...] = -_softplus(-x)


def _fox_prep(z_qkv, z_fl, B, T, P):
    n = z_qkv.shape[0]
    G = GROUP_W
    tt = _row_tile(T, PREP_ROWS)
    nt = T // tt
    q_transposed = tt % V7X_LANES == 0
    ones_bd = _head_ones(FOX_HD)
    fb = jnp.pad(P['fox_f_bias'], (0, V7X_LANES - FOX_HEADS)).reshape(1, V7X_LANES)
    tile = lambda w: pl.BlockSpec((tt, w), lambda b, i: (b * nt + i, 0))
    full = lambda shape: pl.BlockSpec(shape, lambda b, i: (0,) * len(shape))
    if q_transposed:
        q_spec = pl.BlockSpec((1, HEAD_PAIRS, PAIR_W, tt), lambda b, i: (b, 0, 0, i))
        q_shape = jax.ShapeDtypeStruct((B, HEAD_PAIRS, PAIR_W, T), BF16)
    else:
        q_spec, q_shape = tile(G), jax.ShapeDtypeStruct((n, G), BF16)
    state_spec = pl.BlockSpec((tt * FOX_HEADS, FOX_HD), lambda b, i: (b * nt + i, 0))
    state_shape = jax.ShapeDtypeStruct((n * FOX_HEADS, FOX_HD), F32)
    return pl.pallas_call(
        functools.partial(_fox_prep_body, q_transposed=q_transposed),
        grid=(B, nt),
        in_specs=[tile(3 * G), tile(V7X_LANES), full((1, G)), full((1, G)), full((1, V7X_LANES)),
                  full((V7X_MXU, V7X_MXU))],
        out_specs=[q_spec, state_spec, tile(G), state_spec, tile(G), tile(V7X_LANES)],
        out_shape=[q_shape, state_shape, jax.ShapeDtypeStruct((n, G), BF16), state_shape,
                   jax.ShapeDtypeStruct((n, G), BF16), jax.ShapeDtypeStruct((n, V7X_LANES), F32)],
        compiler_params=_params(("parallel", "parallel"), 32),
        name="fox_prep",
    )(z_qkv, z_fl, jnp.tile(P['fox_q_gain'], FOX_HEADS).reshape(1, G),
      jnp.tile(P['fox_k_gain'], FOX_HEADS).reshape(1, G), fb, ones_bd)


def _fox_keys_body(lf_ref, kb_ref, vb_ref, ka_ref, vt_ref, c_ref):
    tt = lf_ref.shape[1]

    @pl.when(pl.program_id(1) == 0)
    def _():
        c_ref[...] = jnp.zeros_like(c_ref)

    grp = min(tt, FOX_CUMSUM_ROWS)
    carry, fs = c_ref[0:1, :], []
    for i in range(0, tt, grp):
        fs.append(_chunk_cumsum(lf_ref[0, i:i + grp, :], grp) + carry)
        carry = fs[-1][grp - 1:grp, :]
    f = jnp.concatenate(fs, axis=0) if len(fs) > 1 else fs[0]
    c_ref[...] = jnp.broadcast_to(carry, c_ref.shape)
    parts = _split3(f * LOG2E)
    srow = lax.broadcasted_iota(jnp.int32, (V7X_LANES, GROUP_W), 0)
    scol = lax.broadcasted_iota(jnp.int32, (V7X_LANES, GROUP_W), 1)
    aug = jnp.zeros((tt, GROUP_W), F32)
    for t, part in enumerate(parts):
        sel = jnp.where((scol // FOX_HD == srow) & (scol % FOX_HD == t), 1.0, 0.0).astype(BF16)
        aug = aug + jnp.dot(part, sel, preferred_element_type=F32)
    aug = aug.astype(BF16)
    for h in range(FOX_HEADS):
        hs = slice(h * FOX_HD, (h + 1) * FOX_HD)
        ka_ref[0, h] = jnp.concatenate([kb_ref[0, :, hs], aug[:, hs]], axis=1)
    for p in range(HEAD_PAIRS):
        vt_ref[0, p] = vb_ref[0, :, p * PAIR_W:(p + 1) * PAIR_W].astype(F32).T.astype(BF16)


def _fox_keys(lf_all, kb_all, vb_all):
    B, tk_all, L = lf_all.shape
    G = GROUP_W
    tt = FOX_KEY_TILE
    return pl.pallas_call(
        _fox_keys_body,
        grid=(B, tk_all // tt),
        in_specs=[pl.BlockSpec((1, tt, L), lambda b, i: (b, i, 0)),
                  pl.BlockSpec((1, tt, G), lambda b, i: (b, i, 0)),
                  pl.BlockSpec((1, tt, G), lambda b, i: (b, i, 0))],
        out_specs=[pl.BlockSpec((1, FOX_HEADS, tt, 2 * FOX_HD), lambda b, i: (b, 0, i, 0)),
                   pl.BlockSpec((1, HEAD_PAIRS, PAIR_W, tt), lambda b, i: (b, 0, 0, i))],
        out_shape=[jax.ShapeDtypeStruct((B, FOX_HEADS, tk_all, 2 * FOX_HD), BF16),
                   jax.ShapeDtypeStruct((B, HEAD_PAIRS, PAIR_W, tk_all), BF16)],
        scratch_shapes=[pltpu.VMEM((8, L), F32)],
        compiler_params=_params(("parallel", "arbitrary"), 32),
        name="fox_keys",
    )(lf_all, kb_all, vb_all)


def _fox_attn_body(qt_ref, ka_ref, vt_ref, og_ref, o_ref, acc_ref, m_ref, l_ref, *, past, tk, t_real):
    qi = pl.program_id(2)
    tq = qt_ref.shape[3]
    t_out = o_ref.shape[0]
    first_q = past + qi * tq
    last_q = past + jnp.minimum(qi * tq + tq, t_real) - 1
    n_full = (first_q + 1) // tk
    n_all = last_q // tk + 1
    drow = lax.broadcasted_iota(jnp.int32, (FOX_HD, tq), 0)
    minus = jnp.where(drow < FOX_F_SPLIT, -1.0, 0.0).astype(BF16)
    rhs = [jnp.concatenate([qt_ref[0, 0, h * FOX_HD:(h + 1) * FOX_HD, :], minus], axis=0) for h in range(2)]
    acc_ref[...] = jnp.zeros_like(acc_ref)

    def update(ki, m_prev, l_prev, masked, q0=0):
        ks = pl.multiple_of(ki * tk, tk)
        s = [jnp.dot(ka_ref[0, h, pl.ds(ks, tk), :], rhs[h][:, q0:], preferred_element_type=F32) for h in range(2)]
        if masked:
            krow = lax.broadcasted_iota(jnp.int32, (tk, tq - q0), 0)
            qcol = lax.broadcasted_iota(jnp.int32, (tk, tq - q0), 1)
            vis = ks + krow <= first_q + q0 + qcol
            s = [jnp.where(vis, s[h], FOX_NEG) for h in range(2)]
        m_new = [jnp.maximum(m_prev[h], jnp.max(s[h], axis=0, keepdims=True)) for h in range(2)]
        alpha = [jnp.exp2(m_prev[h] - m_new[h]) for h in range(2)]
        p = [jnp.exp2(s[h] - m_new[h]) for h in range(2)]
        l_new = [alpha[h] * l_prev[h] + jnp.sum(p[h], axis=0, keepdims=True) for h in range(2)]
        vt = [vt_ref[0, 0, h * FOX_HD:(h + 1) * FOX_HD, pl.ds(ks, tk)] for h in range(2)]
        pv = [jnp.dot(vt[h], p[h].astype(BF16), preferred_element_type=F32) for h in range(2)]
        for h in range(2):
            acc_ref[h, :, q0:] = alpha[h] * acc_ref[h, :, q0:] + pv[h]
        return m_new, l_new

    def carried(ki, c, masked):
        return update(ki, c[0], c[1], masked)

    c = ([jnp.full((1, tq), FOX_NEG, F32)] * 2, [jnp.zeros((1, tq), F32)] * 2)
    c = lax.fori_loop(0, n_full, lambda ki, c: carried(ki, c, False), c)
    if tq > tk and tq % tk == 0 and past % tq == 0 and t_real % tq == 0:
        m, l = carried(n_full, c, True)
        for h in range(2):
            m_ref[h], l_ref[h] = m[h], l[h]
        for j in range(1, tq // tk):
            q0 = j * tk
            m, l = update(n_full + j, [m_ref[h, :, q0:] for h in range(2)], [l_ref[h, :, q0:] for h in range(2)],
                          True, q0=q0)
            for h in range(2):
                m_ref[h, :, q0:], l_ref[h, :, q0:] = m[h], l[h]
    else:
        _, l = lax.fori_loop(n_full, n_all, lambda ki, c: carried(ki, c, True), c)
        for h in range(2):
            l_ref[h] = l[h]
    o_t = jnp.concatenate([acc_ref[h] / l_ref[h] for h in range(2)], axis=0)
    o_ref[...] = (o_t.T[:t_out] * jax.nn.sigmoid(og_ref[...])).astype(o_ref.dtype)


def _fox_attention(q, ka, vt, z_og, B, T, past):
    G = GROUP_W
    pw = PAIR_W
    tq = max(_row_tile(T, FOX_Q_COLS), V7X_LANES)
    tqp = -(-T // tq) * tq
    nq = tqp // tq
    t_out = min(tq, T)
    tk = FOX_K_ROWS
    tkp = ka.shape[2]
    if q.ndim == 2:
        q = q.reshape(B, T, HEAD_PAIRS, pw).transpose(0, 2, 3, 1)
    qt = jnp.pad(q, ((0, 0), (0, 0), (0, 0), (0, tqp - T)))
    return pl.pallas_call(
        functools.partial(_fox_attn_body, past=past, tk=tk, t_real=T),
        grid=(B, HEAD_PAIRS, nq),
        in_specs=[
            pl.BlockSpec((1, 1, pw, tq), lambda b, p, i: (b, p, 0, i)),
            pl.BlockSpec((1, 2, tkp, 2 * FOX_HD), lambda b, p, i: (b, p, 0, 0)),
            pl.BlockSpec((1, 1, pw, tkp), lambda b, p, i: (b, p, 0, 0)),
            pl.BlockSpec((t_out, pw), lambda b, p, i: (b * nq + i, p)),
        ],
        out_specs=pl.BlockSpec((t_out, pw), lambda b, p, i: (b * nq + i, p)),
        out_shape=jax.ShapeDtypeStruct((B * T, G), BF16),
        scratch_shapes=[pltpu.VMEM((2, FOX_HD, tq), F32), pltpu.VMEM((2, 1, tq), F32), pltpu.VMEM((2, 1, tq), F32)],
        compiler_params=_params(("parallel", "parallel", "arbitrary"), 40),
        name="fox_attn",
    )(qt, ka, vt, z_og)


HG_CHUNK = 64
HG_STEP_CHUNKS = 16


def _hgrn_body(z_ref, lb_ref, s0_ref, ng_ref, o_ref, so_ref, st_ref, *, c):
    G = GROUP_W
    rows = z_ref.shape[0]
    nch = rows // c
    dk = G // HG_HEADS

    @pl.when(pl.program_id(1) == 0)
    def _():
        st_ref[...] = s0_ref[0]

    lb = lb_ref[...]
    f = lb + (1.0 - lb) * jax.nn.sigmoid(z_ref[:, G:2 * G])
    kx = 1.0 - f
    crow, ccol = _tri_masks(c)
    incl = ccol <= crow
    gs = _chunk_cumsum(jnp.log(f), c)
    qg_all = z_ref[:, 0:G] * jnp.exp(gs)
    kg_all = kx * jnp.exp(-gs)
    HS = range(HG_HEADS)
    units = [(cc, h) for cc in range(nch) for h in HS]
    US = range(len(units))
    rsl = [slice(cc * c, (cc + 1) * c) for cc, _ in units]
    lsl = [slice(h * dk, (h + 1) * dk) for _, h in units]
    g_last = [gs[(cc + 1) * c - 1:(cc + 1) * c, lsl[u]] for u, (cc, _) in enumerate(units)]
    vv = [z_ref[rsl[u], 2 * G + h * dk:2 * G + (h + 1) * dk] for u, (_, h) in enumerate(units)]
    A = [jnp.where(incl, _dot_lo(qg_all[rsl[u], lsl[u]], kg_all[rsl[u], lsl[u]], _NT), 0.0) for u in US]
    av = [_dot_lo(A[u], vv[u]) for u in US]
    kd = [kx[rsl[u], lsl[u]] * jnp.exp(g_last[u] - gs[rsl[u], lsl[u]]) for u in US]
    upd = [_dot_lo(vv[u], kd[u], _TN) for u in US]
    st = [st_ref[h] for h in HS]
    o = [None for _ in US]
    for cc in range(nch):
        for h in HS:
            u = cc * HG_HEADS + h
            o[u] = _dot_lo(qg_all[rsl[u], lsl[u]], st[h], _NT) + av[u]
        st = [st[h] * jnp.exp(g_last[cc * HG_HEADS + h]) + upd[cc * HG_HEADS + h] for h in HS]
    for h in HS:
        st_ref[h] = st[h]
    for u, (_, h) in enumerate(units):
        hg = z_ref[rsl[u], 3 * G + h * dk:3 * G + (h + 1) * dk]
        o_ref[rsl[u], lsl[u]] = (_rms(o[u], ng_ref[:, lsl[u]]) * (hg * jax.nn.sigmoid(hg))).astype(o_ref.dtype)

    @pl.when(pl.program_id(1) == pl.num_programs(1) - 1)
    def _():
        so_ref[0] = st_ref[...]


def _hgrn2(z_hg, lb, S0, B, T, P):
    G = GROUP_W
    c = min(HG_CHUNK, T)
    rows = _row_tile(T, c * HG_STEP_CHUNKS)
    nc = T // rows
    dk = G // HG_HEADS
    st_spec = pl.BlockSpec((1, HG_HEADS, dk, dk), lambda b, i: (b, 0, 0, 0))
    out, so = pl.pallas_call(
        functools.partial(_hgrn_body, c=c),
        grid=(B, nc),
        in_specs=[pl.BlockSpec((rows, 4 * G), lambda b, i: (b * nc + i, 0)),
                  pl.BlockSpec((1, G), lambda b, i: (0, 0)), st_spec, pl.BlockSpec((1, G), lambda b, i: (0, 0))],
        out_specs=[pl.BlockSpec((rows, G), lambda b, i: (b * nc + i, 0)), st_spec],
        out_shape=[jax.ShapeDtypeStruct((B * T, G), BF16), jax.ShapeDtypeStruct(S0.shape, F32)],
        scratch_shapes=[pltpu.VMEM((HG_HEADS, dk, dk), F32)],
        compiler_params=_params(("parallel", "arbitrary"), 32),
        name="hgrn2",
    )(z_hg, lb.reshape(1, G), jnp.swapaxes(S0, -1, -2), P['hg_norm_g'].reshape(1, G))
    return out, jnp.swapaxes(so, -1, -2)


RW_CHUNK = 64
RW_SUB = 16
RW_LDIAG_CHUNKS = 8
RW_MAIN_CHUNKS = 2

_NT = (((1,), (1,)), ((), ()))
_TN = (((0,), (0,)), ((), ()))
_NN = (((1,), (0,)), ((), ()))


def _split3(x):
    h1 = x.astype(BF16)
    r1 = x - h1.astype(F32)
    h2 = r1.astype(BF16)
    h3 = (r1 - h2.astype(F32)).astype(BF16)
    return h1, h2, h3


def _dot_lo(a, b, dims=_NN):
    return lax.dot_general(a.astype(BF16), b.astype(BF16), dims, preferred_element_type=F32)


def _dot_hi(a, b, dims=_NN):
    ah = a.astype(BF16)
    al = (a - ah.astype(F32)).astype(BF16)
    bh = b.astype(BF16)
    bl = (b - bh.astype(F32)).astype(BF16)
    d = functools.partial(lax.dot_general, dimension_numbers=dims, preferred_element_type=F32)
    return d(ah, bh) + (d(al, bh) + d(ah, bl))


def _dot_exact_rhs(a, b):
    h1, h2, h3 = _split3(a)
    d = functools.partial(jnp.dot, preferred_element_type=F32)
    return d(h1, b) + (d(h2, b) + d(h3, b))


def _head_sums(x, ones):
    w = ones.shape[0]
    return jnp.concatenate([_dot_exact_rhs(x[:, i:i + w], ones) for i in range(0, x.shape[1], w)], axis=1)


def _dot_exact_lhs(a, b):
    h1, h2, h3 = _split3(b)
    d = functools.partial(jnp.dot, preferred_element_type=F32)
    return d(a, h1) + (d(a, h2) + d(a, h3))


def _softplus(x):
    return jnp.maximum(x, 0.0) + jnp.log1p(jnp.exp(-jnp.abs(x)))


def _rw_prep_body(z_ref, shift_ref, mu_ref, w0_ref, w2_ref, a0_ref, a2_ref, g2_ref, kk_ref, ka_ref, ones_ref,
                  r_ref, lw_ref, k_ref, v_ref, kap_ref, bet_ref, g_ref, prev_ref):
    G = GROUP_W

    @pl.when(pl.program_id(1) == 0)
    def _():
        prev_ref[0:1, :] = shift_ref[0]

    z = z_ref[...]
    tt = z.shape[0]
    row = lax.broadcasted_iota(jnp.int32, z.shape, 0)
    shifted = jnp.where(row == 0, prev_ref[0:1, :], pltpu.roll(z, 1, axis=0))
    prev_ref[0:1, :] = z[tt - 1:tt, :]
    zm = z + (shifted - z) * mu_ref[...]
    r, k, v = zm[:, 0:G], zm[:, G:2 * G], zm[:, 2 * G:3 * G]
    o = 3 * G
    wd = zm[:, o:o + RW_DECAY_LORA]
    ad = zm[:, o + RW_DECAY_LORA:o + RW_DECAY_LORA + RW_A_LORA]
    gd = zm[:, o + RW_DECAY_LORA + RW_A_LORA:]
    w = -_softplus(-(w0_ref[...] + _dot_lo(jnp.tanh(wd), w2_ref[...]))) - 0.5
    a = jax.nn.sigmoid(a0_ref[...] + _dot_lo(ad, a2_ref[...]))
    kk = k * kk_ref[...]
    ss = _head_sums(kk * kk, ones_ref[...])
    kap = kk / jnp.maximum(jnp.sqrt(ss), 1e-12)
    r_ref[...] = r
    lw_ref[...] = -jnp.exp(w)
    k_ref[...] = k * (1.0 + (a - 1.0) * ka_ref[...])
    v_ref[...] = v
    kap_ref[...] = kap
    bet_ref[...] = kap * a
    g_ref[...] = _dot_lo(jax.nn.sigmoid(gd), g2_ref[...])


def _rw_prep(zr, shift, B, T, P):
    n, cols = zr.shape
    G = GROUP_W
    tt = _row_tile(T, PREP_ROWS)
    nt = T // tt
    ones_bd = _head_ones(RW_HD)
    row = lambda x: x.reshape(1, -1)
    full = lambda shape: pl.BlockSpec(shape, lambda b, i: (0,) * len(shape))
    tile = pl.BlockSpec((tt, G), lambda b, i: (b * nt + i, 0))
    return pl.pallas_call(
        _rw_prep_body,
        grid=(B, nt),
        in_specs=[
            pl.BlockSpec((tt, cols), lambda b, i: (b * nt + i, 0)),
            pl.BlockSpec((1, 1, cols), lambda b, i: (b, 0, 0)),
            full((1, cols)), full((1, G)), full((RW_DECAY_LORA, G)), full((1, G)), full((RW_A_LORA, G)),
            full((RW_GATE_LORA, G)), full((1, G)), full((1, G)), full((V7X_MXU, V7X_MXU)),
        ],
        out_specs=[tile] * 7,
        out_shape=[jax.ShapeDtypeStruct((n, G), F32)] * 7,
        scratch_shapes=[pltpu.VMEM((8, cols), F32)],
        compiler_params=_params(("parallel", "arbitrary"), 40),
        name="rwkv_prep",
    )(zr, shift.reshape(B, 1, cols), row(P['rw_mu']), row(P['rw_w0']), P['rw_w2'].astype(BF16), row(P['rw_a0']),
      P['rw_a2'].astype(BF16), P['rw_g2'].astype(BF16), row(P['rw_kk']), row(P['rw_ka']), ones_bd)


def _rw_scaled(lw, kap, bet, c):
    cs = _chunk_cumsum(lw, c)
    return cs, kap * jnp.exp(cs - lw), bet * jnp.exp(-cs)


def _tri_masks(c):
    row = lax.broadcasted_iota(jnp.int32, (c, c), 0)
    col = lax.broadcasted_iota(jnp.int32, (c, c), 1)
    return row, col


def _chunk_cumsum(x, c):
    row, col = _tri_masks(c)
    tri = jnp.where(col <= row, 1.0, 0.0).astype(BF16)
    parts = [_dot_exact_lhs(tri, x[i:i + c]) for i in range(0, x.shape[0], c)]
    return parts[0] if len(parts) == 1 else jnp.concatenate(parts, axis=0)


def _rw_ldiag_body(lw_ref, kap_ref, bet_ref, o_ref, *, c):
    rows = lw_ref.shape[0]
    _, kk_all, bt_all = _rw_scaled(lw_ref[...], kap_ref[...], bet_ref[...], c)
    srow, scol = _tri_masks(RW_SUB)
    units = [(cc, h) for cc in range(rows // c) for h in range(RW_HEADS)]
    Ls = [_dot_lo(kk_all[cc * c:(cc + 1) * c, h * RW_HD:(h + 1) * RW_HD],
                  bt_all[cc * c:(cc + 1) * c, h * RW_HD:(h + 1) * RW_HD], _NT) for cc, h in units]
    for (cc, h), L in zip(units, Ls):
        for b in range(c // RW_SUB):
            rs = slice(b * RW_SUB, (b + 1) * RW_SUB)
            o_ref[cc * c + b * RW_SUB:cc * c + (b + 1) * RW_SUB, h * RW_SUB:(h + 1) * RW_SUB] = (
                jnp.where(scol < srow, L[rs, rs], 0.0))


def _rw_inv_body(l_ref, t_ref, a_ref, b_ref):
    n = RW_SUB
    nblk = l_ref.shape[0] // n
    for t in range(n):
        a_ref[t] = l_ref[pl.ds(t, nblk, stride=n), :].T
    entry = lambda ref, t, s: ref.at[t, pl.ds(s, RW_HEADS, stride=n), :]
    one = jnp.ones((RW_HEADS, nblk), F32)
    zero = jnp.zeros((RW_HEADS, nblk), F32)
    for t in range(n):
        for s in range(n):
            if s > t:
                entry(b_ref, t, s)[...] = zero
            elif s == t:
                entry(b_ref, t, s)[...] = one
            else:
                acc = entry(a_ref, t, s)[...]
                for j in range(s + 1, t):
                    acc = acc + entry(a_ref, t, j)[...] * entry(b_ref, j, s)[...]
                entry(b_ref, t, s)[...] = -acc
    for t in range(n):
        t_ref[pl.ds(t, nblk, stride=n), :] = b_ref[t].T


def _rw_main_body(r_ref, lw_ref, k_ref, v_ref, kap_ref, bet_ref, g_ref, td_ref, h0_ref, rk_ref, lng_ref, lnb_ref,
                  o_ref, hout_ref, h_ref, *, c):
    ci = pl.program_id(1)
    rows = r_ref.shape[0]
    nb = c // RW_SUB

    @pl.when(ci == 0)
    def _():
        h_ref[...] = h0_ref[0]

    crow, ccol = _tri_masks(c)
    strict = ccol < crow
    incl = ccol <= crow
    lw = lw_ref[...]
    cs, kk_all, bt_all = _rw_scaled(lw, kap_ref[...], bet_ref[...], c)
    gi = jnp.exp(-cs)
    gg = jnp.exp(cs)
    kt_all = k_ref[...] * gi
    rt_all = r_ref[...] * gg
    bonus_all = r_ref[...] * k_ref[...] * rk_ref[...]
    hrow = lax.broadcasted_iota(jnp.int32, (RW_HD, RW_HD), 0)
    hcol = lax.broadcasted_iota(jnp.int32, (RW_HD, RW_HD), 1)
    HS = range(RW_HEADS)
    units = [(cc, h) for cc in range(rows // c) for h in HS]
    US = range(len(units))
    rsl = [slice(cc * c, (cc + 1) * c) for cc, _ in units]
    lsl = [slice(h * RW_HD, (h + 1) * RW_HD) for _, h in units]
    Kk = [kk_all[rsl[u], lsl[u]] for u in US]
    Bt = [bt_all[rsl[u], lsl[u]] for u in US]
    Kt = [kt_all[rsl[u], lsl[u]] for u in US]
    Rt = [rt_all[rsl[u], lsl[u]] for u in US]
    vv = [v_ref[rsl[u], lsl[u]] for u in US]
    Lm = [jnp.where(strict, _dot_lo(Kk[u], Bt[u], _NT), 0.0) for u in US]
    A1 = [jnp.where(strict, _dot_lo(Kk[u], Kt[u], _NT), 0.0) for u in US]
    A4 = [jnp.where(incl, _dot_lo(Rt[u], Bt[u], _NT), 0.0) for u in US]
    A3 = [jnp.where(incl, _dot_lo(Rt[u], Kt[u], _NT), 0.0) for u in US]
    X = [jnp.concatenate([Kk[u], _dot_lo(A1[u], vv[u])], axis=1) for u in US]
    zs = [[] for _ in US]
    for b in range(nb):
        rs = slice(b * RW_SUB, (b + 1) * RW_SUB)
        rhs = [X[u][rs] for u in US]
        if b:
            rhs = [rhs[u] - _dot_lo(Lm[u][rs, 0:b * RW_SUB], jnp.concatenate(zs[u], axis=0)) for u in US]
        for u, (cc, h) in enumerate(units):
            tbb = td_ref[cc * c + b * RW_SUB:cc * c + (b + 1) * RW_SUB, h * RW_SUB:(h + 1) * RW_SUB]
            zs[u].append(_dot_lo(tbb, rhs[u]))
    Z = [jnp.concatenate(zs[u], axis=0) if nb > 1 else zs[u][0] for u in US]
    A4Z = [_dot_lo(A4[u], Z[u]) for u in US]
    Rhat = [Rt[u] - A4Z[u][:, :RW_HD] for u in US]
    Yhat = [_dot_lo(A3[u], vv[u]) - A4Z[u][:, RW_HD:] for u in US]
    gC = [gg[(cc + 1) * c - 1:(cc + 1) * c, lsl[u]] for u, (cc, _) in enumerate(units)]
    MN = [_dot_lo(Bt[u] * gC[u], Z[u], _TN) for u in US]
    Mp = [jnp.where(hrow == hcol, gC[u], 0.0) - MN[u][:, :RW_HD] for u in US]
    Np = [_dot_lo(Kt[u] * gC[u], vv[u], _TN) - MN[u][:, RW_HD:] for u in US]
    H = [h_ref[h] for h in HS]
    ys = [None for _ in US]
    for cc in range(rows // c):
        for h in HS:
            u = cc * RW_HEADS + h
            ys[u] = _dot_lo(Rhat[u], H[h]) + Yhat[u]
        H = [_dot_hi(Mp[cc * RW_HEADS + h], H[h]) + Np[cc * RW_HEADS + h] for h in HS]
    for h in HS:
        h_ref[h] = H[h]
    for u in US:
        y = ys[u]
        mu = jnp.mean(y, axis=-1, keepdims=True)
        var = jnp.mean(jnp.square(y - mu), axis=-1, keepdims=True)
        yn = (y - mu) * lax.rsqrt(var + RW_LN_EPS) * lng_ref[:, lsl[u]] + lnb_ref[:, lsl[u]]
        yn = yn + jnp.sum(bonus_all[rsl[u], lsl[u]], axis=-1, keepdims=True) * vv[u]
        o_ref[rsl[u], lsl[u]] = (yn * g_ref[rsl[u], lsl[u]]).astype(o_ref.dtype)

    @pl.when(ci == pl.num_programs(1) - 1)
    def _():
        hout_ref[0] = h_ref[...]


def _rwkv7(zr, shift, S0, B, T, P):
    G = GROUP_W
    n = B * T
    r, lw, k, v, kap, bet, g = _rw_prep(zr, shift, B, T, P)
    c = min(RW_CHUNK, T)
    rows_l = _row_tile(T, c * RW_LDIAG_CHUNKS)
    rows_m = _row_tile(T, c * RW_MAIN_CHUNKS)
    nl, nc = T // rows_l, T // rows_m
    tile_l = pl.BlockSpec((rows_l, G), lambda b, i: (b * nl + i, 0))
    tile = pl.BlockSpec((rows_m, G), lambda b, i: (b * nc + i, 0))
    ld = pl.pallas_call(
        functools.partial(_rw_ldiag_body, c=c),
        grid=(B, nl),
        in_specs=[tile_l] * 3,
        out_specs=pl.BlockSpec((rows_l, RW_HEADS * RW_SUB), lambda b, i: (b * nl + i, 0)),
        out_shape=jax.ShapeDtypeStruct((n, RW_HEADS * RW_SUB), F32),
        compiler_params=_params(("parallel", "parallel"), 32),
        name="rwkv_ldiag",
    )(lw, kap, bet)
    rows_i = V7X_LANES * RW_SUB
    npad = -(-n // rows_i) * rows_i
    inv_spec = pl.BlockSpec((rows_i, RW_HEADS * RW_SUB), lambda i: (i, 0))
    inv_scratch = pltpu.VMEM((RW_SUB, RW_HEADS * RW_SUB, V7X_LANES), F32)
    td = pl.pallas_call(
        _rw_inv_body,
        grid=(npad // rows_i,),
        in_specs=[inv_spec],
        out_specs=inv_spec,
        out_shape=jax.ShapeDtypeStruct((npad, RW_HEADS * RW_SUB), F32),
        scratch_shapes=[inv_scratch, inv_scratch],
        compiler_params=_params(("parallel",), 32),
        name="rwkv_inv",
    )(jnp.pad(ld, ((0, npad - n), (0, 0))))[:n]
    h0 = jnp.swapaxes(S0, -1, -2)
    prow = lambda x: pl.BlockSpec((1, G), lambda b, i: (0, 0))
    st_spec = pl.BlockSpec((1, RW_HEADS, RW_HD, RW_HD), lambda b, i: (b, 0, 0, 0))
    out, hl = pl.pallas_call(
        functools.partial(_rw_main_body, c=c),
        grid=(B, nc),
        in_specs=[tile] * 7 + [pl.BlockSpec((rows_m, RW_HEADS * RW_SUB), lambda b, i: (b * nc + i, 0)), st_spec,
                               prow(0), prow(0), prow(0)],
        out_specs=[tile, st_spec],
        out_shape=[jax.ShapeDtypeStruct((n, G), BF16), jax.ShapeDtypeStruct(S0.shape, F32)],
        scratch_shapes=[pltpu.VMEM((RW_HEADS, RW_HD, RW_HD), F32)],
        compiler_params=_params(("parallel", "arbitrary"), 32),
        name="rwkv_main",
    )(r, lw, k, v, kap, bet, g, td, h0, P['rw_rk'].reshape(1, G), P['rw_ln_g'].reshape(1, G),
      P['rw_ln_b'].reshape(1, G))
    return out, zr.reshape(B, T, -1)[:, -1], jnp.swapaxes(hl, -1, -2)


def _even_mixer(x2, B, T, g, st, P):
    conv_buf, lru_h, k_past, v_past, lf_past = st
    G = GROUP_W
    z_rg, z_qkv, z_og, z_fl = _norm_matmul(x2, g, P['e_w_in'], (2 * G, 3 * G, G, V7X_LANES))
    rnn_out, conv_new, h_last = _lru(z_rg, conv_buf, lru_h, B, T, P)
    qb, kn, kb, v, vb, lf = _fox_prep(z_qkv, z_fl, B, T, P)
    past = k_past.shape[1]
    lf_all = lf.reshape(B, T, V7X_LANES)
    kb_all, vb_all = kb.reshape(B, T, G), vb.reshape(B, T, G)
    if past:
        lf_all = jnp.concatenate([jnp.pad(lf_past, ((0, 0), (0, 0), (0, V7X_LANES - FOX_HEADS))), lf_all], axis=1)
        kb_all = jnp.concatenate([k_past.reshape(B, past, G).astype(BF16), kb_all], axis=1)
        vb_all = jnp.concatenate([v_past.reshape(B, past, G).astype(BF16), vb_all], axis=1)
    tail = ((0, 0), (0, -(past + T) % FOX_K_ROWS), (0, 0))
    ka, vt = _fox_keys(jnp.pad(lf_all, tail), jnp.pad(kb_all, tail), jnp.pad(vb_all, tail))
    fox_out = _fox_attention(qb, ka, vt, z_og, B, T, past)
    heads = lambda t: t.reshape(B, T, FOX_HEADS, FOX_HD)
    return (rnn_out, fox_out, P['e_w_out']), (conv_new, h_last, heads(kn), heads(v), lf.reshape(B, T, V7X_LANES)[..., :FOX_HEADS])


def _odd_mixer(x2, B, T, g, st, lb, P):
    S_hg, shift, S_rw = st
    G = GROUP_W
    z_hg, z_rw = _norm_matmul(x2, g, P['o_w_in'], (4 * G, P['o_w_in'].shape[1] - 4 * G))
    hg_out, S_hg_new = _hgrn2(z_hg, lb, S_hg, B, T, P)
    rw_out, shift_new, S_rw_new = _rwkv7(z_rw, shift, S_rw, B, T, P)
    return (hg_out, rw_out, P['o_w_out']), (S_hg_new, shift_new, S_rw_new)


def _trunk(x, states, W):
    lru_conv, lru_h, fox_k, fox_v, fox_lf, hg_S, rw_shift, rw_S = states
    B, T, D = x.shape
    depth = W['norm_g'].shape[0]
    sm = jax.nn.softmax(W['hg_lb_logits'], axis=0)
    lower_bounds = jnp.cumsum(sm, axis=0) - sm[0]
    x2 = x.reshape(B * T, D)
    even_new, odd_new = [], []
    for layer in range(depth):
        g = W['norm_g'][layer]
        x2 = _ffn(x2, g[0], W['ffn_w_in'][layer][0], W['ffn_w_out'][layer][0])
        if layer % 2 == 0:
            e = layer // 2
            P = {n: W[n][e] for n in ('e_w_in', 'e_w_out', 'lru_conv_w', 'lru_conv_b', 'lru_wa', 'lru_ba', 'lru_wx',
                                      'lru_bx', 'lru_lambda', 'fox_q_gain', 'fox_k_gain', 'fox_f_bias')}
            mix, new = _even_mixer(x2, B, T, g[1], (lru_conv[e], lru_h[e], fox_k[e], fox_v[e], fox_lf[e]), P)
            even_new.append(new)
        else:
            o = layer // 2
            P = {n: W[n][o] for n in ('o_w_in', 'o_w_out', 'hg_norm_g', 'rw_mu', 'rw_w0', 'rw_w2', 'rw_a0', 'rw_a2',
                                      'rw_g2', 'rw_kk', 'rw_ka', 'rw_rk', 'rw_ln_g', 'rw_ln_b')}
            mix, new = _odd_mixer(x2, B, T, g[1], (hg_S[o], rw_shift[o], rw_S[o]), lower_bounds[layer], P)
            odd_new.append(new)
        x2 = _ffn(x2, g[2], W['ffn_w_in'][layer][1], W['ffn_w_out'][layer][1], mix)
    ev = [jnp.stack([n[j] for n in even_new]) for j in range(5)]
    od = [jnp.stack([n[j] for n in odd_new]) for j in range(3)]
    return x2.reshape(B, T, D), (ev[0], ev[1], ev[2], ev[3], ev[4], od[0], od[1], od[2])


def kernel(x_prompt, x_sample, state_lru_conv, state_lru_h, cache_fox_k, cache_fox_v, cache_fox_logf,
           state_hgrn_S, state_rwkv_shift, state_rwkv_S, norm_g, ffn_w_in, ffn_w_out, e_w_in, e_w_out,
           lru_conv_w, lru_conv_b, lru_wa, lru_ba, lru_wx, lru_bx, lru_lambda, fox_q_gain, fox_k_gain,
           fox_f_bias, o_w_in, o_w_out, hg_lb_logits, hg_norm_g, rw_mu, rw_w0, rw_w2, rw_a0, rw_a2, rw_g2,
           rw_kk, rw_ka, rw_rk, rw_ln_g, rw_ln_b):
    n_even, n_odd = e_w_in.shape[0], o_w_in.shape[0]
    W = dict(norm_g=norm_g, ffn_w_in=_ffn_w_in_tiles(ffn_w_in), ffn_w_out=ffn_w_out.astype(BF16),
             e_w_in=_pad_cols(e_w_in.astype(BF16)), e_w_out=e_w_out.astype(BF16),
             lru_conv_w=lru_conv_w, lru_conv_b=lru_conv_b, lru_wa=lru_wa, lru_ba=lru_ba, lru_wx=lru_wx,
             lru_bx=lru_bx, lru_lambda=lru_lambda, fox_q_gain=fox_q_gain, fox_k_gain=fox_k_gain,
             fox_f_bias=fox_f_bias, o_w_in=o_w_in.astype(BF16), o_w_out=o_w_out.astype(BF16),
             hg_lb_logits=hg_lb_logits, hg_norm_g=hg_norm_g, rw_mu=rw_mu, rw_w0=rw_w0, rw_w2=rw_w2, rw_a0=rw_a0,
             rw_a2=rw_a2, rw_g2=rw_g2, rw_kk=rw_kk, rw_ka=rw_ka, rw_rk=rw_rk, rw_ln_g=rw_ln_g, rw_ln_b=rw_ln_b)
    nb = x_prompt.shape[0]
    dt = x_prompt.dtype
    prompt_states = (jnp.zeros((n_even, nb, CONV_W - 1, GROUP_W), dt),
                     jnp.zeros((n_even, nb, GROUP_W), dt),
                     jnp.zeros((n_even, nb, 0, FOX_HEADS, FOX_HD), dt),
                     jnp.zeros((n_even, nb, 0, FOX_HEADS, FOX_HD), dt),
                     jnp.zeros((n_even, nb, 0, FOX_HEADS), dt),
                     jnp.zeros((n_odd, nb, HG_HEADS, GROUP_W // HG_HEADS, GROUP_W // HG_HEADS), dt),
                     jnp.zeros((n_odd, nb, rw_mu.shape[1]), dt),
                     jnp.zeros((n_odd, nb, RW_HEADS, RW_HD, RW_HD), dt))
    sample_states = (state_lru_conv, state_lru_h, cache_fox_k, cache_fox_v, cache_fox_logf,
                     state_hgrn_S, state_rwkv_shift, state_rwkv_S)
    y_prompt, p_new = _trunk(x_prompt, prompt_states, W)
    y_sample, s_new = _trunk(x_sample, sample_states, W)
    lru_conv_p, lru_h_p, fox_k_p, fox_v_p, fox_logf_p, hgrn_S_p, rwkv_shift_p, rwkv_S_p = p_new
    lru_conv_s, lru_h_s, fox_k_s, fox_v_s, fox_logf_s, hgrn_S_s, rwkv_shift_s, rwkv_S_s = s_new
    return (y_prompt, y_sample, lru_conv_p, lru_conv_s, lru_h_p, lru_h_s, fox_k_p, fox_k_s, fox_v_p, fox_v_s,
            fox_logf_p, fox_logf_s, hgrn_S_p, hgrn_S_s, rwkv_shift_p, rwkv_shift_s, rwkv_S_p, rwkv_S_s)
```

```python
import functools

import jax
import jax.numpy as jnp
from jax import lax
from jax.experimental import pallas as pl
from jax.experimental.pallas import tpu as pltpu

F32 = jnp.float32
BF16 = jnp.bfloat16

NORM_EPS = 1e-6
GROUP_W = 512
CONV_W = 4
LRU_C = 8.0
FOX_HEADS = 8
FOX_HD = 64
HG_HEADS = 4
RW_HEADS = 8
RW_HD = 64
RW_DECAY_LORA = 64
RW_A_LORA = 64
RW_GATE_LORA = 128
RW_LN_EPS = 64e-5

V7X_LANES = 128
V7X_MXU = 256
FFN_COL_TILE = 1408
PREP_ROWS = 512
FFN_ROW_TILE = 1024
FFN_VMEM_MIB = 60


def _row_tile(n, want):
    t = min(n, want)
    while n % t:
        t //= 2
    return t


def _params(sem, vmem_mib):
    return pltpu.CompilerParams(dimension_semantics=sem, vmem_limit_bytes=vmem_mib << 20)


def _pad_cols(w):
    pad = -w.shape[-1] % V7X_LANES
    return jnp.pad(w, [(0, 0)] * (w.ndim - 1) + [(0, pad)])


def _head_ones(hd):
    return jnp.kron(jnp.eye(V7X_MXU // hd, dtype=F32), jnp.ones((hd, hd), F32)).astype(BF16)


def _rms(x, g):
    return x * lax.rsqrt(jnp.mean(x * x, axis=-1, keepdims=True) + NORM_EPS) * g


def _ffn_body(*refs, mixed):
    if mixed:
        x_ref, a_ref, b_ref, wa_ref, wb_ref, g_ref, wi_ref, wo_ref, o_ref, h_ref, acc_ref = refs
    else:
        x_ref, g_ref, wi_ref, wo_ref, o_ref, h_ref, acc_ref = refs
    j = pl.program_id(1)

    @pl.when(j == 0)
    def _():
        x = x_ref[...]
        if mixed:
            x = x + jnp.dot(a_ref[...], wa_ref[...], preferred_element_type=F32)
            x = x + jnp.dot(b_ref[...], wb_ref[...], preferred_element_type=F32)
        h_ref[...] = _rms(x, g_ref[...]).astype(BF16)
        acc_ref[...] = 2.0 * x

    tf = wo_ref.shape[0]
    gu = jnp.dot(h_ref[...], wi_ref[...], preferred_element_type=F32)
    gate, up = gu[:, :tf], gu[:, tf:]
    act = (gate * jax.nn.sigmoid(gate) * up).astype(BF16)
    acc_ref[...] += jnp.dot(act, wo_ref[...], preferred_element_type=F32)

    @pl.when(j == pl.num_programs(1) - 1)
    def _():
        o_ref[...] = 0.5 * acc_ref[...]


def _cast_body(x_ref, o_ref):
    o_ref[...] = x_ref[...].astype(o_ref.dtype)


def _ffn_w_in_tiles(w_in):
    *lead, d, f2 = w_in.shape
    tf = FFN_COL_TILE
    nf = f2 // 2 // tf
    w = w_in.reshape(-1, d, f2)
    out = pl.pallas_call(
        _cast_body,
        grid=(w.shape[0], nf, 2),
        in_specs=[pl.BlockSpec((1, d, tf), lambda i, j, gu: (i, 0, gu * nf + j))],
        out_specs=pl.BlockSpec((1, d, tf), lambda i, j, gu: (i, 0, 2 * j + gu)),
        out_shape=jax.ShapeDtypeStruct(w.shape, BF16),
        compiler_params=_params(("parallel", "parallel", "parallel"), 32),
        name="ffn_weight_tiles",
    )(w)
    return out.reshape(*lead, d, f2)


def _ffn(x, g, w_in, w_out, mix=None):
    n, d = x.shape
    f = w_out.shape[0]
    tm = _row_tile(n, FFN_ROW_TILE)
    tf = FFN_COL_TILE
    nf = f // tf
    row_spec = lambda w: pl.BlockSpec((tm, w), lambda i, j: (i, 0))
    args, specs = [x], [row_spec(d)]
    if mix is not None:
        a, b, w = mix
        ga, gb = a.shape[1], b.shape[1]
        args += [a, b, w[:ga], w[ga:]]
        specs += [row_spec(ga), row_spec(gb), pl.BlockSpec((ga, d), lambda i, j: (0, 0)),
                  pl.BlockSpec((gb, d), lambda i, j: (0, 0))]
    args += [g.reshape(1, d), w_in, w_out]
    specs += [pl.BlockSpec((1, d), lambda i, j: (0, 0)), pl.BlockSpec((d, 2 * tf), lambda i, j: (0, j)),
              pl.BlockSpec((tf, d), lambda i, j: (j, 0))]
    return pl.pallas_call(
        functools.partial(_ffn_body, mixed=mix is not None),
        grid=(n // tm, nf),
        in_specs=specs,
        out_specs=row_spec(d),
        out_shape=jax.ShapeDtypeStruct((n, d), F32),
        scratch_shapes=[pltpu.VMEM((tm, d), BF16), pltpu.VMEM((tm, d), F32)],
        compiler_params=_params(("parallel", "arbitrary"), FFN_VMEM_MIB),
        name="ffn",
    )(*args)


def _norm_matmul_body(x_ref, g_ref, w_ref, *o_refs):
    h = _rms(x_ref[...], g_ref[...]).astype(BF16)
    z = jnp.dot(h, w_ref[...], preferred_element_type=F32)
    start = 0
    for o_ref in o_refs:
        width = o_ref.shape[1]
        o_ref[...] = z[:, start:start + width]
        start += width


def _norm_matmul(x, g, w, widths):
    n, d = x.shape
    c = w.shape[1]
    assert sum(widths) == c and all(wd % V7X_LANES == 0 for wd in widths)
    tm = _row_tile(n, 512)
    return pl.pallas_call(
        _norm_matmul_body,
        grid=(n // tm,),
        in_specs=[
            pl.BlockSpec((tm, d), lambda i: (i, 0)),
            pl.BlockSpec((1, d), lambda i: (0, 0)),
            pl.BlockSpec((d, c), lambda i: (0, 0)),
        ],
        out_specs=[pl.BlockSpec((tm, wd), lambda i: (i, 0)) for wd in widths],
        out_shape=[jax.ShapeDtypeStruct((n, wd), F32) for wd in widths],
        compiler_params=_params(("parallel",), 48),
        name="norm_matmul",
    )(x, g.reshape(1, d), w)


LRU_ROWS = 256
CONV_PAD = 8


def _expm1(x):
    series = x * (1.0 + x * (1 / 2 + x * (1 / 6 + x * (1 / 24 + x * (1 / 120 + x * (1 / 720 + x * (1 / 5040 + x * (1 / 40320))))))))
    return jnp.where(jnp.abs(x) < 0.25, series, jnp.exp(x) - 1.0)


def _shift_rows(x, s, fill):
    row = lax.broadcasted_iota(jnp.int32, x.shape, 0)
    return jnp.where(row >= s, pltpu.roll(x, s, axis=0), fill)


def _lru_body(z_ref, buf_ref, h0_ref, cw_ref, cb_ref, wa_ref, ba_ref, wx_ref, bx_ref, lam_ref,
              o_ref, bufo_ref, ho_ref, x_ref, hc_ref):
    G = GROUP_W
    tt = z_ref.shape[0]

    @pl.when(pl.program_id(1) == 0)
    def _():
        x_ref[0:CONV_PAD, :] = buf_ref[0]
        hc_ref[...] = jnp.broadcast_to(h0_ref[0], hc_ref.shape)

    x_ref[CONV_PAD:CONV_PAD + tt, :] = z_ref[:, 0:G]
    xc = cb_ref[...]
    for j in range(CONV_W):
        lo = CONV_PAD - (CONV_W - 1) + j
        xc = xc + x_ref[lo:lo + tt, :] * cw_ref[j:j + 1, :]
    hist = x_ref[tt:tt + CONV_PAD, :]
    x_ref[0:CONV_PAD, :] = hist
    bufo_ref[0] = hist

    xb = xc.astype(BF16)
    r = jax.nn.sigmoid(jnp.dot(xb, wa_ref[...], preferred_element_type=F32) + ba_ref[...])
    ig = jax.nn.sigmoid(jnp.dot(xb, wx_ref[...], preferred_element_type=F32) + bx_ref[...])
    log_a = (-LRU_C * _softplus(-lam_ref[...])) * r
    a = jnp.exp(log_a)
    b = jnp.sqrt(-_expm1(2.0 * log_a)) * (ig * xc)
    s = 1
    while s < tt:
        if s % 8:
            b = a * _shift_rows(b, s, 0.0) + b
            a = a * _shift_rows(a, s, 1.0)
        else:
            b = jnp.concatenate([b[:s], a[s:] * b[:tt - s] + b[s:]], axis=0)
            a = jnp.concatenate([a[:s], a[s:] * a[:tt - s]], axis=0)
        s *= 2
    h = a * hc_ref[0:1, :] + b
    hc_ref[...] = jnp.broadcast_to(h[tt - 1:tt, :], hc_ref.shape)
    ho_ref[0] = h[tt - 1:tt, :]
    o_ref[...] = (jax.nn.gelu(z_ref[:, G:2 * G]) * h).astype(o_ref.dtype)


def _block_diag_dense(w):
    nb, bs, _ = w.shape
    eye = jnp.eye(nb, dtype=w.dtype)
    return (eye[:, None, :, None] * w[:, :, None, :]).reshape(nb * bs, nb * bs)


def _lru(z_rg, conv_buf, h0, B, T, P):
    G = GROUP_W
    n = B * T
    tt = _row_tile(T, LRU_ROWS)
    nt = T // tt
    buf = jnp.pad(conv_buf, ((0, 0), (CONV_PAD - (CONV_W - 1), 0), (0, 0)))
    cw = jnp.pad(P['lru_conv_w'], ((0, CONV_PAD - CONV_W), (0, 0)))
    row = lambda x: x.reshape(1, G)
    full = lambda shape: pl.BlockSpec(shape, lambda b, i: (0,) * len(shape))
    out, bufo, ho = pl.pallas_call(
        _lru_body,
        grid=(B, nt),
        in_specs=[
            pl.BlockSpec((tt, 2 * G), lambda b, i: (b * nt + i, 0)),
            pl.BlockSpec((1, CONV_PAD, G), lambda b, i: (b, 0, 0)),
            pl.BlockSpec((1, 1, G), lambda b, i: (b, 0, 0)),
            full((CONV_PAD, G)), full((1, G)), full((G, G)), full((1, G)), full((G, G)), full((1, G)), full((1, G)),
        ],
        out_specs=[
            pl.BlockSpec((tt, G), lambda b, i: (b * nt + i, 0)),
            pl.BlockSpec((1, CONV_PAD, G), lambda b, i: (b, 0, 0)),
            pl.BlockSpec((1, 1, G), lambda b, i: (b, 0, 0)),
        ],
        out_shape=[jax.ShapeDtypeStruct((n, G), BF16), jax.ShapeDtypeStruct((B, CONV_PAD, G), F32),
                   jax.ShapeDtypeStruct((B, 1, G), F32)],
        scratch_shapes=[pltpu.VMEM((tt + CONV_PAD, G), F32), pltpu.VMEM((8, G), F32)],
        compiler_params=_params(("parallel", "arbitrary"), 32),
        name="lru",
    )(z_rg, buf, h0.reshape(B, 1, G), cw, row(P['lru_conv_b']), _block_diag_dense(P['lru_wa']).astype(BF16),
      row(P['lru_ba']), _block_diag_dense(P['lru_wx']).astype(BF16), row(P['lru_bx']), row(P['lru_lambda']))
    return out, bufo[:, CONV_PAD - (CONV_W - 1):], ho.reshape(B, G)


FOX_Q_COLS = 2048
FOX_K_ROWS = 512
FOX_F_SPLIT = 3
FOX_NEG = -1e30
LOG2E = 1.4426950408889634
HEAD_PAIRS = FOX_HEADS // 2
PAIR_W = 2 * FOX_HD
FOX_CUMSUM_ROWS = 256
FOX_KEY_TILE = 512


def _fox_prep_body(z_ref, fl_ref, qg_ref, kg_ref, fb_ref, ones_ref, q_ref, k_ref, kb_ref, v_ref, vb_ref, lf_ref,
                   *, q_transposed):
    G = GROUP_W
    q, k, v = z_ref[:, 0:G], z_ref[:, G:2 * G], z_ref[:, 2 * G:3 * G]
    inv = 1.0 / FOX_HD
    qn = q * lax.rsqrt(_head_sums(q * q, ones_ref[...]) * inv + NORM_EPS) * qg_ref[...]
    kn = k * lax.rsqrt(_head_sums(k * k, ones_ref[...]) * inv + NORM_EPS) * kg_ref[...]
    qs = qn * (LOG2E * FOX_HD ** -0.5)
    if q_transposed:
        for p in range(HEAD_PAIRS):
            q_ref[0, p] = qs[:, p * PAIR_W:(p + 1) * PAIR_W].T.astype(BF16)
    else:
        q_ref[...] = qs.astype(BF16)
    tt = z_ref.shape[0]
    for h in range(FOX_HEADS):
        k_ref[pl.ds(h, tt, stride=FOX_HEADS), :] = kn[:, h * FOX_HD:(h + 1) * FOX_HD]
        v_ref[pl.ds(h, tt, stride=FOX_HEADS), :] = v[:, h * FOX_HD:(h + 1) * FOX_HD]
    kb_ref[...] = kn.astype(BF16)
    vb_ref[...] = v.astype(BF16)
    x = fl_ref[...] + fb_ref[...]
    lf_ref[...] = -_softplus(-x)


def _fox_prep(z_qkv, z_fl, B, T, P):
    n = z_qkv.shape[0]
    G = GROUP_W
    tt = _row_tile(T, PREP_ROWS)
    nt = T // tt
    q_transposed = tt % V7X_LANES == 0
    ones_bd = _head_ones(FOX_HD)
    fb = jnp.pad(P['fox_f_bias'], (0, V7X_LANES - FOX_HEADS)).reshape(1, V7X_LANES)
    tile = lambda w: pl.BlockSpec((tt, w), lambda b, i: (b * nt + i, 0))
    full = lambda shape: pl.BlockSpec(shape, lambda b, i: (0,) * len(shape))
    if q_transposed:
        q_spec = pl.BlockSpec((1, HEAD_PAIRS, PAIR_W, tt), lambda b, i: (b, 0, 0, i))
        q_shape = jax.ShapeDtypeStruct((B, HEAD_PAIRS, PAIR_W, T), BF16)
    else:
        q_spec, q_shape = tile(G), jax.ShapeDtypeStruct((n, G), BF16)
    state_spec = pl.BlockSpec((tt * FOX_HEADS, FOX_HD), lambda b, i: (b * nt + i, 0))
    state_shape = jax.ShapeDtypeStruct((n * FOX_HEADS, FOX_HD), F32)
    return pl.pallas_call(
        functools.partial(_fox_prep_body, q_transposed=q_transposed),
        grid=(B, nt),
        in_specs=[tile(3 * G), tile(V7X_LANES), full((1, G)), full((1, G)), full((1, V7X_LANES)),
                  full((V7X_MXU, V7X_MXU))],
        out_specs=[q_spec, state_spec, tile(G), state_spec, tile(G), tile(V7X_LANES)],
        out_shape=[q_shape, state_shape, jax.ShapeDtypeStruct((n, G), BF16), state_shape,
                   jax.ShapeDtypeStruct((n, G), BF16), jax.ShapeDtypeStruct((n, V7X_LANES), F32)],
        compiler_params=_params(("parallel", "parallel"), 32),
        name="fox_prep",
    )(z_qkv, z_fl, jnp.tile(P['fox_q_gain'], FOX_HEADS).reshape(1, G),
      jnp.tile(P['fox_k_gain'], FOX_HEADS).reshape(1, G), fb, ones_bd)


def _fox_keys_body(lf_ref, kb_ref, vb_ref, ka_ref, vt_ref, c_ref):
    tt = lf_ref.shape[1]

    @pl.when(pl.program_id(1) == 0)
    def _():
        c_ref[...] = jnp.zeros_like(c_ref)

    grp = min(tt, FOX_CUMSUM_ROWS)
    carry, fs = c_ref[0:1, :], []
    for i in range(0, tt, grp):
        fs.append(_chunk_cumsum(lf_ref[0, i:i + grp, :], grp) + carry)
        carry = fs[-1][grp - 1:grp, :]
    f = jnp.concatenate(fs, axis=0) if len(fs) > 1 else fs[0]
    c_ref[...] = jnp.broadcast_to(carry, c_ref.shape)
    parts = _split3(f * LOG2E)
    srow = lax.broadcasted_iota(jnp.int32, (V7X_LANES, GROUP_W), 0)
    scol = lax.broadcasted_iota(jnp.int32, (V7X_LANES, GROUP_W), 1)
    aug = jnp.zeros((tt, GROUP_W), F32)
    for t, part in enumerate(parts):
        sel = jnp.where((scol // FOX_HD == srow) & (scol % FOX_HD == t), 1.0, 0.0).astype(BF16)
        aug = aug + jnp.dot(part, sel, preferred_element_type=F32)
    aug = aug.astype(BF16)
    for h in range(FOX_HEADS):
        hs = slice(h * FOX_HD, (h + 1) * FOX_HD)
        ka_ref[0, h] = jnp.concatenate([kb_ref[0, :, hs], aug[:, hs]], axis=1)
    for p in range(HEAD_PAIRS):
        vt_ref[0, p] = vb_ref[0, :, p * PAIR_W:(p + 1) * PAIR_W].astype(F32).T.astype(BF16)


def _fox_keys(lf_all, kb_all, vb_all):
    B, tk_all, L = lf_all.shape
    G = GROUP_W
    tt = FOX_KEY_TILE
    return pl.pallas_call(
        _fox_keys_body,
        grid=(B, tk_all // tt),
        in_specs=[pl.BlockSpec((1, tt, L), lambda b, i: (b, i, 0)),
                  pl.BlockSpec((1, tt, G), lambda b, i: (b, i, 0)),
                  pl.BlockSpec((1, tt, G), lambda b, i: (b, i, 0))],
        out_specs=[pl.BlockSpec((1, FOX_HEADS, tt, 2 * FOX_HD), lambda b, i: (b, 0, i, 0)),
                   pl.BlockSpec((1, HEAD_PAIRS, PAIR_W, tt), lambda b, i: (b, 0, 0, i))],
        out_shape=[jax.ShapeDtypeStruct((B, FOX_HEADS, tk_all, 2 * FOX_HD), BF16),
                   jax.ShapeDtypeStruct((B, HEAD_PAIRS, PAIR_W, tk_all), BF16)],
        scratch_shapes=[pltpu.VMEM((8, L), F32)],
        compiler_params=_params(("parallel", "arbitrary"), 32),
        name="fox_keys",
    )(lf_all, kb_all, vb_all)


def _fox_attn_body(qt_ref, ka_ref, vt_ref, og_ref, o_ref, acc_ref, m_ref, l_ref, *, past, tk, t_real):
    qi = pl.program_id(2)
    tq = qt_ref.shape[3]
    t_out = o_ref.shape[0]
    first_q = past + qi * tq
    last_q = past + jnp.minimum(qi * tq + tq, t_real) - 1
    n_full = (first_q + 1) // tk
    n_all = last_q // tk + 1
    drow = lax.broadcasted_iota(jnp.int32, (FOX_HD, tq), 0)
    minus = jnp.where(drow < FOX_F_SPLIT, -1.0, 0.0).astype(BF16)
    rhs = [jnp.concatenate([qt_ref[0, 0, h * FOX_HD:(h + 1) * FOX_HD, :], minus], axis=0) for h in range(2)]
    acc_ref[...] = jnp.zeros_like(acc_ref)

    def update(ki, m_prev, l_prev, masked, q0=0):
        ks = pl.multiple_of(ki * tk, tk)
        s = [jnp.dot(ka_ref[0, h, pl.ds(ks, tk), :], rhs[h][:, q0:], preferred_element_type=F32) for h in range(2)]
        if masked:
            krow = lax.broadcasted_iota(jnp.int32, (tk, tq - q0), 0)
            qcol = lax.broadcasted_iota(jnp.int32, (tk, tq - q0), 1)
            vis = ks + krow <= first_q + q0 + qcol
            s = [jnp.where(vis, s[h], FOX_NEG) for h in range(2)]
        m_new = [jnp.maximum(m_prev[h], jnp.max(s[h], axis=0, keepdims=True)) for h in range(2)]
        alpha = [jnp.exp2(m_prev[h] - m_new[h]) for h in range(2)]
        p = [jnp.exp2(s[h] - m_new[h]) for h in range(2)]
        l_new = [alpha[h] * l_prev[h] + jnp.sum(p[h], axis=0, keepdims=True) for h in range(2)]
        vt = [vt_ref[0, 0, h * FOX_HD:(h + 1) * FOX_HD, pl.ds(ks, tk)] for h in range(2)]
        pv = [jnp.dot(vt[h], p[h].astype(BF16), preferred_element_type=F32) for h in range(2)]
        for h in range(2):
            acc_ref[h, :, q0:] = alpha[h] * acc_ref[h, :, q0:] + pv[h]
        return m_new, l_new

    def carried(ki, c, masked):
        return update(ki, c[0], c[1], masked)

    c = ([jnp.full((1, tq), FOX_NEG, F32)] * 2, [jnp.zeros((1, tq), F32)] * 2)
    c = lax.fori_loop(0, n_full, lambda ki, c: carried(ki, c, False), c)
    if tq > tk and tq % tk == 0 and past % tq == 0 and t_real % tq == 0:
        m, l = carried(n_full, c, True)
        for h in range(2):
            m_ref[h], l_ref[h] = m[h], l[h]
        for j in range(1, tq // tk):
            q0 = j * tk
            m, l = update(n_full + j, [m_ref[h, :, q0:] for h in range(2)], [l_ref[h, :, q0:] for h in range(2)],
                          True, q0=q0)
            for h in range(2):
                m_ref[h, :, q0:], l_ref[h, :, q0:] = m[h], l[h]
    else:
        _, l = lax.fori_loop(n_full, n_all, lambda ki, c: carried(ki, c, True), c)
        for h in range(2):
            l_ref[h] = l[h]
    o_t = jnp.concatenate([acc_ref[h] / l_ref[h] for h in range(2)], axis=0)
    o_ref[...] = (o_t.T[:t_out] * jax.nn.sigmoid(og_ref[...])).astype(o_ref.dtype)


def _fox_attention(q, ka, vt, z_og, B, T, past):
    G = GROUP_W
    pw = PAIR_W
    tq = max(_row_tile(T, FOX_Q_COLS), V7X_LANES)
    tqp = -(-T // tq) * tq
    nq = tqp // tq
    t_out = min(tq, T)
    tk = FOX_K_ROWS
    tkp = ka.shape[2]
    if q.ndim == 2:
        q = q.reshape(B, T, HEAD_PAIRS, pw).transpose(0, 2, 3, 1)
    qt = jnp.pad(q, ((0, 0), (0, 0), (0, 0), (0, tqp - T)))
    return pl.pallas_call(
        functools.partial(_fox_attn_body, past=past, tk=tk, t_real=T),
        grid=(B, HEAD_PAIRS, nq),
        in_specs=[
            pl.BlockSpec((1, 1, pw, tq), lambda b, p, i: (b, p, 0, i)),
            pl.BlockSpec((1, 2, tkp, 2 * FOX_HD), lambda b, p, i: (b, p, 0, 0)),
            pl.BlockSpec((1, 1, pw, tkp), lambda b, p, i: (b, p, 0, 0)),
            pl.BlockSpec((t_out, pw), lambda b, p, i: (b * nq + i, p)),
        ],
        out_specs=pl.BlockSpec((t_out, pw), lambda b, p, i: (b * nq + i, p)),
        out_shape=jax.ShapeDtypeStruct((B * T, G), BF16),
        scratch_shapes=[pltpu.VMEM((2, FOX_HD, tq), F32), pltpu.VMEM((2, 1, tq), F32), pltpu.VMEM((2, 1, tq), F32)],
        compiler_params=_params(("parallel", "parallel", "arbitrary"), 40),
        name="fox_attn",
    )(qt, ka, vt, z_og)


HG_CHUNK = 64
HG_STEP_CHUNKS = 16


def _hgrn_body(z_ref, lb_ref, s0_ref, ng_ref, o_ref, so_ref, st_ref, *, c):
    G = GROUP_W
    rows = z_ref.shape[0]
    nch = rows // c
    dk = G // HG_HEADS

    @pl.when(pl.program_id(1) == 0)
    def _():
        st_ref[...] = s0_ref[0]

    lb = lb_ref[...]
    f = lb + (1.0 - lb) * jax.nn.sigmoid(z_ref[:, G:2 * G])
    kx = 1.0 - f
    crow, ccol = _tri_masks(c)
    incl = ccol <= crow
    gs = _chunk_cumsum(jnp.log(f), c)
    qg_all = z_ref[:, 0:G] * jnp.exp(gs)
    kg_all = kx * jnp.exp(-gs)
    HS = range(HG_HEADS)
    units = [(cc, h) for cc in range(nch) for h in HS]
    US = range(len(units))
    rsl = [slice(cc * c, (cc + 1) * c) for cc, _ in units]
    lsl = [slice(h * dk, (h + 1) * dk) for _, h in units]
    g_last = [gs[(cc + 1) * c - 1:(cc + 1) * c, lsl[u]] for u, (cc, _) in enumerate(units)]
    vv = [z_ref[rsl[u], 2 * G + h * dk:2 * G + (h + 1) * dk] for u, (_, h) in enumerate(units)]
    A = [jnp.where(incl, _dot_lo(qg_all[rsl[u], lsl[u]], kg_all[rsl[u], lsl[u]], _NT), 0.0) for u in US]
    av = [_dot_lo(A[u], vv[u]) for u in US]
    kd = [kx[rsl[u], lsl[u]] * jnp.exp(g_last[u] - gs[rsl[u], lsl[u]]) for u in US]
    upd = [_dot_lo(vv[u], kd[u], _TN) for u in US]
    st = [st_ref[h] for h in HS]
    o = [None for _ in US]
    for cc in range(nch):
        for h in HS:
            u = cc * HG_HEADS + h
            o[u] = _dot_lo(qg_all[rsl[u], lsl[u]], st[h], _NT) + av[u]
        st = [st[h] * jnp.exp(g_last[cc * HG_HEADS + h]) + upd[cc * HG_HEADS + h] for h in HS]
    for h in HS:
        st_ref[h] = st[h]
    for u, (_, h) in enumerate(units):
        hg = z_ref[rsl[u], 3 * G + h * dk:3 * G + (h + 1) * dk]
        o_ref[rsl[u], lsl[u]] = (_rms(o[u], ng_ref[:, lsl[u]]) * (hg * jax.nn.sigmoid(hg))).astype(o_ref.dtype)

    @pl.when(pl.program_id(1) == pl.num_programs(1) - 1)
    def _():
        so_ref[0] = st_ref[...]


def _hgrn2(z_hg, lb, S0, B, T, P):
    G = GROUP_W
    c = min(HG_CHUNK, T)
    rows = _row_tile(T, c * HG_STEP_CHUNKS)
    nc = T // rows
    dk = G // HG_HEADS
    st_spec = pl.BlockSpec((1, HG_HEADS, dk, dk), lambda b, i: (b, 0, 0, 0))
    out, so = pl.pallas_call(
        functools.partial(_hgrn_body, c=c),
        grid=(B, nc),
        in_specs=[pl.BlockSpec((rows, 4 * G), lambda b, i: (b * nc + i, 0)),
                  pl.BlockSpec((1, G), lambda b, i: (0, 0)), st_spec, pl.BlockSpec((1, G), lambda b, i: (0, 0))],
        out_specs=[pl.BlockSpec((rows, G), lambda b, i: (b * nc + i, 0)), st_spec],
        out_shape=[jax.ShapeDtypeStruct((B * T, G), BF16), jax.ShapeDtypeStruct(S0.shape, F32)],
        scratch_shapes=[pltpu.VMEM((HG_HEADS, dk, dk), F32)],
        compiler_params=_params(("parallel", "arbitrary"), 32),
        name="hgrn2",
    )(z_hg, lb.reshape(1, G), jnp.swapaxes(S0, -1, -2), P['hg_norm_g'].reshape(1, G))
    return out, jnp.swapaxes(so, -1, -2)


RW_CHUNK = 64
RW_SUB = 16
RW_MAIN_CHUNKS = 2

_NT = (((1,), (1,)), ((), ()))
_TN = (((0,), (0,)), ((), ()))
_NN = (((1,), (0,)), ((), ()))


def _split3(x):
    h1 = x.astype(BF16)
    r1 = x - h1.astype(F32)
    h2 = r1.astype(BF16)
    h3 = (r1 - h2.astype(F32)).astype(BF16)
    return h1, h2, h3


def _dot_lo(a, b, dims=_NN):
    return lax.dot_general(a.astype(BF16), b.astype(BF16), dims, preferred_element_type=F32)


def _dot_hi(a, b, dims=_NN):
    ah = a.astype(BF16)
    al = (a - ah.astype(F32)).astype(BF16)
    bh = b.astype(BF16)
    bl = (b - bh.astype(F32)).astype(BF16)
    d = functools.partial(lax.dot_general, dimension_numbers=dims, preferred_element_type=F32)
    return d(ah, bh) + (d(al, bh) + d(ah, bl))


def _dot_exact_rhs(a, b):
    h1, h2, h3 = _split3(a)
    d = functools.partial(jnp.dot, preferred_element_type=F32)
    return d(h1, b) + (d(h2, b) + d(h3, b))


def _head_sums(x, ones):
    w = ones.shape[0]
    return jnp.concatenate([_dot_exact_rhs(x[:, i:i + w], ones) for i in range(0, x.shape[1], w)], axis=1)


def _dot_exact_lhs(a, b):
    h1, h2, h3 = _split3(b)
    d = functools.partial(jnp.dot, preferred_element_type=F32)
    return d(a, h1) + (d(a, h2) + d(a, h3))


def _softplus(x):
    return jnp.maximum(x, 0.0) + jnp.log1p(jnp.exp(-jnp.abs(x)))


def _rw_prep_body(z_ref, shift_ref, mu_ref, w0_ref, w2_ref, a0_ref, a2_ref, g2_ref, kk_ref, ka_ref, ones_ref,
                  r_ref, lw_ref, k_ref, v_ref, kap_ref, bet_ref, g_ref, ld_ref, prev_ref, *, c):
    G = GROUP_W

    @pl.when(pl.program_id(1) == 0)
    def _():
        prev_ref[0:1, :] = shift_ref[0]

    z = z_ref[...]
    tt = z.shape[0]
    row = lax.broadcasted_iota(jnp.int32, z.shape, 0)
    shifted = jnp.where(row == 0, prev_ref[0:1, :], pltpu.roll(z, 1, axis=0))
    prev_ref[0:1, :] = z[tt - 1:tt, :]
    zm = z + (shifted - z) * mu_ref[...]
    r, k, v = zm[:, 0:G], zm[:, G:2 * G], zm[:, 2 * G:3 * G]
    o = 3 * G
    wd = zm[:, o:o + RW_DECAY_LORA]
    ad = zm[:, o + RW_DECAY_LORA:o + RW_DECAY_LORA + RW_A_LORA]
    gd = zm[:, o + RW_DECAY_LORA + RW_A_LORA:]
    w = -_softplus(-(w0_ref[...] + _dot_lo(jnp.tanh(wd), w2_ref[...]))) - 0.5
    a = jax.nn.sigmoid(a0_ref[...] + _dot_lo(ad, a2_ref[...]))
    kk = k * kk_ref[...]
    ss = _head_sums(kk * kk, ones_ref[...])
    kap = kk / jnp.maximum(jnp.sqrt(ss), 1e-12)
    lw = -jnp.exp(w)
    bet = kap * a
    r_ref[...] = r
    lw_ref[...] = lw
    k_ref[...] = k * (1.0 + (a - 1.0) * ka_ref[...])
    v_ref[...] = v
    kap_ref[...] = kap
    bet_ref[...] = bet
    g_ref[...] = _dot_lo(jax.nn.sigmoid(gd), g2_ref[...])
    _rw_ldiag_tile(lw, kap, bet, ld_ref, c)


def _rw_prep(zr, shift, B, T, P):
    n, cols = zr.shape
    G = GROUP_W
    tt = _row_tile(T, PREP_ROWS)
    nt = T // tt
    ones_bd = _head_ones(RW_HD)
    row = lambda x: x.reshape(1, -1)
    full = lambda shape: pl.BlockSpec(shape, lambda b, i: (0,) * len(shape))
    tile = pl.BlockSpec((tt, G), lambda b, i: (b * nt + i, 0))
    ld_tile = pl.BlockSpec((tt, RW_HEADS * RW_SUB), lambda b, i: (b * nt + i, 0))
    return pl.pallas_call(
        functools.partial(_rw_prep_body, c=min(RW_CHUNK, T)),
        grid=(B, nt),
        in_specs=[
            pl.BlockSpec((tt, cols), lambda b, i: (b * nt + i, 0)),
            pl.BlockSpec((1, 1, cols), lambda b, i: (b, 0, 0)),
            full((1, cols)), full((1, G)), full((RW_DECAY_LORA, G)), full((1, G)), full((RW_A_LORA, G)),
            full((RW_GATE_LORA, G)), full((1, G)), full((1, G)), full((V7X_MXU, V7X_MXU)),
        ],
        out_specs=[tile] * 7 + [ld_tile],
        out_shape=[jax.ShapeDtypeStruct((n, G), F32)] * 7 + [jax.ShapeDtypeStruct((n, RW_HEADS * RW_SUB), F32)],
        scratch_shapes=[pltpu.VMEM((8, cols), F32)],
        compiler_params=_params(("parallel", "arbitrary"), 40),
        name="rwkv_prep",
    )(zr, shift.reshape(B, 1, cols), row(P['rw_mu']), row(P['rw_w0']), P['rw_w2'].astype(BF16), row(P['rw_a0']),
      P['rw_a2'].astype(BF16), P['rw_g2'].astype(BF16), row(P['rw_kk']), row(P['rw_ka']), ones_bd)


def _rw_scaled(lw, kap, bet, c):
    cs = _chunk_cumsum(lw, c)
    return cs, kap * jnp.exp(cs - lw), bet * jnp.exp(-cs)


def _tri_masks(c):
    row = lax.broadcasted_iota(jnp.int32, (c, c), 0)
    col = lax.broadcasted_iota(jnp.int32, (c, c), 1)
    return row, col


def _chunk_cumsum(x, c):
    row, col = _tri_masks(c)
    tri = jnp.where(col <= row, 1.0, 0.0).astype(BF16)
    parts = [_dot_exact_lhs(tri, x[i:i + c]) for i in range(0, x.shape[0], c)]
    return parts[0] if len(parts) == 1 else jnp.concatenate(parts, axis=0)


def _rw_ldiag_tile(lw, kap, bet, o_ref, c):
    rows = lw.shape[0]
    _, kk_all, bt_all = _rw_scaled(lw, kap, bet, c)
    srow, scol = _tri_masks(RW_SUB)
    units = [(cc, h) for cc in range(rows // c) for h in range(RW_HEADS)]
    Ls = [_dot_lo(kk_all[cc * c:(cc + 1) * c, h * RW_HD:(h + 1) * RW_HD],
                  bt_all[cc * c:(cc + 1) * c, h * RW_HD:(h + 1) * RW_HD], _NT) for cc, h in units]
    for (cc, h), L in zip(units, Ls):
        for b in range(c // RW_SUB):
            rs = slice(b * RW_SUB, (b + 1) * RW_SUB)
            o_ref[cc * c + b * RW_SUB:cc * c + (b + 1) * RW_SUB, h * RW_SUB:(h + 1) * RW_SUB] = (
                jnp.where(scol < srow, L[rs, rs], 0.0))


def _rw_inv_body(l_ref, t_ref, a_ref, b_ref):
    n = RW_SUB
    nblk = l_ref.shape[0] // n
    for t in range(n):
        a_ref[t] = l_ref[pl.ds(t, nblk, stride=n), :].T
    entry = lambda ref, t, s: ref.at[t, pl.ds(s, RW_HEADS, stride=n), :]
    one = jnp.ones((RW_HEADS, nblk), F32)
    zero = jnp.zeros((RW_HEADS, nblk), F32)
    for t in range(n):
        for s in range(n):
            if s > t:
                entry(b_ref, t, s)[...] = zero
            elif s == t:
                entry(b_ref, t, s)[...] = one
            else:
                acc = entry(a_ref, t, s)[...]
                for j in range(s + 1, t):
                    acc = acc + entry(a_ref, t, j)[...] * entry(b_ref, j, s)[...]
                entry(b_ref, t, s)[...] = -acc
    for t in range(n):
        t_ref[pl.ds(t, nblk, stride=n), :] = b_ref[t].T


def _rw_main_body(r_ref, lw_ref, k_ref, v_ref, kap_ref, bet_ref, g_ref, td_ref, h0_ref, rk_ref, lng_ref, lnb_ref,
                  o_ref, hout_ref, h_ref, *, c):
    ci = pl.program_id(1)
    rows = r_ref.shape[0]
    nb = c // RW_SUB

    @pl.when(ci == 0)
    def _():
        h_ref[...] = h0_ref[0]

    crow, ccol = _tri_masks(c)
    strict = ccol < crow
    incl = ccol <= crow
    lw = lw_ref[...]
    cs, kk_all, bt_all = _rw_scaled(lw, kap_ref[...], bet_ref[...], c)
    gi = jnp.exp(-cs)
    gg = jnp.exp(cs)
    kt_all = k_ref[...] * gi
    rt_all = r_ref[...] * gg
    bonus_all = r_ref[...] * k_ref[...] * rk_ref[...]
    hrow = lax.broadcasted_iota(jnp.int32, (RW_HD, RW_HD), 0)
    hcol = lax.broadcasted_iota(jnp.int32, (RW_HD, RW_HD), 1)
    HS = range(RW_HEADS)
    units = [(cc, h) for cc in range(rows // c) for h in HS]
    US = range(len(units))
    rsl = [slice(cc * c, (cc + 1) * c) for cc, _ in units]
    lsl = [slice(h * RW_HD, (h + 1) * RW_HD) for _, h in units]
    Kk = [kk_all[rsl[u], lsl[u]] for u in US]
    Bt = [bt_all[rsl[u], lsl[u]] for u in US]
    Kt = [kt_all[rsl[u], lsl[u]] for u in US]
    Rt = [rt_all[rsl[u], lsl[u]] for u in US]
    vv = [v_ref[rsl[u], lsl[u]] for u in US]
    Lm = [jnp.where(strict, _dot_lo(Kk[u], Bt[u], _NT), 0.0) for u in US]
    A1 = [jnp.where(strict, _dot_lo(Kk[u], Kt[u], _NT), 0.0) for u in US]
    A4 = [jnp.where(incl, _dot_lo(Rt[u], Bt[u], _NT), 0.0) for u in US]
    A3 = [jnp.where(incl, _dot_lo(Rt[u], Kt[u], _NT), 0.0) for u in US]
    X = [jnp.concatenate([Kk[u], _dot_lo(A1[u], vv[u])], axis=1) for u in US]
    zs = [[] for _ in US]
    for b in range(nb):
        rs = slice(b * RW_SUB, (b + 1) * RW_SUB)
        rhs = [X[u][rs] for u in US]
        if b:
            rhs = [rhs[u] - _dot_lo(Lm[u][rs, 0:b * RW_SUB], jnp.concatenate(zs[u], axis=0)) for u in US]
        for u, (cc, h) in enumerate(units):
            tbb = td_ref[cc * c + b * RW_SUB:cc * c + (b + 1) * RW_SUB, h * RW_SUB:(h + 1) * RW_SUB]
            zs[u].append(_dot_lo(tbb, rhs[u]))
    Z = [jnp.concatenate(zs[u], axis=0) if nb > 1 else zs[u][0] for u in US]
    A4Z = [_dot_lo(A4[u], Z[u]) for u in US]
    Rhat = [Rt[u] - A4Z[u][:, :RW_HD] for u in US]
    Yhat = [_dot_lo(A3[u], vv[u]) - A4Z[u][:, RW_HD:] for u in US]
    gC = [gg[(cc + 1) * c - 1:(cc + 1) * c, lsl[u]] for u, (cc, _) in enumerate(units)]
    MN = [_dot_lo(Bt[u] * gC[u], Z[u], _TN) for u in US]
    Mp = [jnp.where(hrow == hcol, gC[u], 0.0) - MN[u][:, :RW_HD] for u in US]
    Np = [_dot_lo(Kt[u] * gC[u], vv[u], _TN) - MN[u][:, RW_HD:] for u in US]
    H = [h_ref[h] for h in HS]
    ys = [None for _ in US]
    for cc in range(rows // c):
        for h in HS:
            u = cc * RW_HEADS + h
            ys[u] = _dot_lo(Rhat[u], H[h]) + Yhat[u]
        H = [_dot_hi(Mp[cc * RW_HEADS + h], H[h]) + Np[cc * RW_HEADS + h] for h in HS]
    for h in HS:
        h_ref[h] = H[h]
    for u in US:
        y = ys[u]
        mu = jnp.mean(y, axis=-1, keepdims=True)
        var = jnp.mean(jnp.square(y - mu), axis=-1, keepdims=True)
        yn = (y - mu) * lax.rsqrt(var + RW_LN_EPS) * lng_ref[:, lsl[u]] + lnb_ref[:, lsl[u]]
        yn = yn + jnp.sum(bonus_all[rsl[u], lsl[u]], axis=-1, keepdims=True) * vv[u]
        o_ref[rsl[u], lsl[u]] = (yn * g_ref[rsl[u], lsl[u]]).astype(o_ref.dtype)

    @pl.when(ci == pl.num_programs(1) - 1)
    def _():
        hout_ref[0] = h_ref[...]


def _rwkv7(zr, shift, S0, B, T, P):
    G = GROUP_W
    n = B * T
    r, lw, k, v, kap, bet, g, ld = _rw_prep(zr, shift, B, T, P)
    c = min(RW_CHUNK, T)
    rows_m = _row_tile(T, c * RW_MAIN_CHUNKS)
    nc = T // rows_m
    tile = pl.BlockSpec((rows_m, G), lambda b, i: (b * nc + i, 0))
    rows_i = V7X_LANES * RW_SUB
    npad = -(-n // rows_i) * rows_i
    inv_spec = pl.BlockSpec((rows_i, RW_HEADS * RW_SUB), lambda i: (i, 0))
    inv_scratch = pltpu.VMEM((RW_SUB, RW_HEADS * RW_SUB, V7X_LANES), F32)
    td = pl.pallas_call(
        _rw_inv_body,
        grid=(npad // rows_i,),
        in_specs=[inv_spec],
        out_specs=inv_spec,
        out_shape=jax.ShapeDtypeStruct((npad, RW_HEADS * RW_SUB), F32),
        scratch_shapes=[inv_scratch, inv_scratch],
        compiler_params=_params(("parallel",), 32),
        name="rwkv_inv",
    )(jnp.pad(ld, ((0, npad - n), (0, 0))))[:n]
    h0 = jnp.swapaxes(S0, -1, -2)
    prow = lambda x: pl.BlockSpec((1, G), lambda b, i: (0, 0))
    st_spec = pl.BlockSpec((1, RW_HEADS, RW_HD, RW_HD), lambda b, i: (b, 0, 0, 0))
    out, hl = pl.pallas_call(
        functools.partial(_rw_main_body, c=c),
        grid=(B, nc),
        in_specs=[tile] * 7 + [pl.BlockSpec((rows_m, RW_HEADS * RW_SUB), lambda b, i: (b * nc + i, 0)), st_spec,
                               prow(0), prow(0), prow(0)],
        out_specs=[tile, st_spec],
        out_shape=[jax.ShapeDtypeStruct((n, G), BF16), jax.ShapeDtypeStruct(S0.shape, F32)],
        scratch_shapes=[pltpu.VMEM((RW_HEADS, RW_HD, RW_HD), F32)],
        compiler_params=_params(("parallel", "arbitrary"), 32),
        name="rwkv_main",
    )(r, lw, k, v, kap, bet, g, td, h0, P['rw_rk'].reshape(1, G), P['rw_ln_g'].reshape(1, G),
      P['rw_ln_b'].reshape(1, G))
    return out, zr.reshape(B, T, -1)[:, -1], jnp.swapaxes(hl, -1, -2)


def _even_mixer(x2, B, T, g, st, P):
    conv_buf, lru_h, k_past, v_past, lf_past = st
    G = GROUP_W
    z_rg, z_qkv, z_og, z_fl = _norm_matmul(x2, g, P['e_w_in'], (2 * G, 3 * G, G, V7X_LANES))
    rnn_out, conv_new, h_last = _lru(z_rg, conv_buf, lru_h, B, T, P)
    qb, kn, kb, v, vb, lf = _fox_prep(z_qkv, z_fl, B, T, P)
    past = k_past.shape[1]
    lf_all = lf.reshape(B, T, V7X_LANES)
    kb_all, vb_all = kb.reshape(B, T, G), vb.reshape(B, T, G)
    if past:
        lf_all = jnp.concatenate([jnp.pad(lf_past, ((0, 0), (0, 0), (0, V7X_LANES - FOX_HEADS))), lf_all], axis=1)
        kb_all = jnp.concatenate([k_past.reshape(B, past, G).astype(BF16), kb_all], axis=1)
        vb_all = jnp.concatenate([v_past.reshape(B, past, G).astype(BF16), vb_all], axis=1)
    tail = ((0, 0), (0, -(past + T) % FOX_K_ROWS), (0, 0))
    ka, vt = _fox_keys(jnp.pad(lf_all, tail), jnp.pad(kb_all, tail), jnp.pad(vb_all, tail))
    fox_out = _fox_attention(qb, ka, vt, z_og, B, T, past)
    heads = lambda t: t.reshape(B, T, FOX_HEADS, FOX_HD)
    return (rnn_out, fox_out, P['e_w_out']), (conv_new, h_last, heads(kn), heads(v), lf.reshape(B, T, V7X_LANES)[..., :FOX_HEADS])


def _odd_mixer(x2, B, T, g, st, lb, P):
    S_hg, shift, S_rw = st
    G = GROUP_W
    z_hg, z_rw = _norm_matmul(x2, g, P['o_w_in'], (4 * G, P['o_w_in'].shape[1] - 4 * G))
    hg_out, S_hg_new = _hgrn2(z_hg, lb, S_hg, B, T, P)
    rw_out, shift_new, S_rw_new = _rwkv7(z_rw, shift, S_rw, B, T, P)
    return (hg_out, rw_out, P['o_w_out']), (S_hg_new, shift_new, S_rw_new)


def _trunk(x, states, W):
    lru_conv, lru_h, fox_k, fox_v, fox_lf, hg_S, rw_shift, rw_S = states
    B, T, D = x.shape
    depth = W['norm_g'].shape[0]
    sm = jax.nn.softmax(W['hg_lb_logits'], axis=0)
    lower_bounds = jnp.cumsum(sm, axis=0) - sm[0]
    x2 = x.reshape(B * T, D)
    even_new, odd_new = [], []
    for layer in range(depth):
        g = W['norm_g'][layer]
        x2 = _ffn(x2, g[0], W['ffn_w_in'][layer][0], W['ffn_w_out'][layer][0])
        if layer % 2 == 0:
            e = layer // 2
            P = {n: W[n][e] for n in ('e_w_in', 'e_w_out', 'lru_conv_w', 'lru_conv_b', 'lru_wa', 'lru_ba', 'lru_wx',
                                      'lru_bx', 'lru_lambda', 'fox_q_gain', 'fox_k_gain', 'fox_f_bias')}
            mix, new = _even_mixer(x2, B, T, g[1], (lru_conv[e], lru_h[e], fox_k[e], fox_v[e], fox_lf[e]), P)
            even_new.append(new)
        else:
            o = layer // 2
            P = {n: W[n][o] for n in ('o_w_in', 'o_w_out', 'hg_norm_g', 'rw_mu', 'rw_w0', 'rw_w2', 'rw_a0', 'rw_a2',
                                      'rw_g2', 'rw_kk', 'rw_ka', 'rw_rk', 'rw_ln_g', 'rw_ln_b')}
            mix, new = _odd_mixer(x2, B, T, g[1], (hg_S[o], rw_shift[o], rw_S[o]), lower_bounds[layer], P)
            odd_new.append(new)
        x2 = _ffn(x2, g[2], W['ffn_w_in'][layer][1], W['ffn_w_out'][layer][1], mix)
    ev = [jnp.stack([n[j] for n in even_new]) for j in range(5)]
    od = [jnp.stack([n[j] for n in odd_new]) for j in range(3)]
    return x2.reshape(B, T, D), (ev[0], ev[1], ev[2], ev[3], ev[4], od[0], od[1], od[2])


def kernel(x_prompt, x_sample, state_lru_conv, state_lru_h, cache_fox_k, cache_fox_v, cache_fox_logf,
           state_hgrn_S, state_rwkv_shift, state_rwkv_S, norm_g, ffn_w_in, ffn_w_out, e_w_in, e_w_out,
           lru_conv_w, lru_conv_b, lru_wa, lru_ba, lru_wx, lru_bx, lru_lambda, fox_q_gain, fox_k_gain,
           fox_f_bias, o_w_in, o_w_out, hg_lb_logits, hg_norm_g, rw_mu, rw_w0, rw_w2, rw_a0, rw_a2, rw_g2,
           rw_kk, rw_ka, rw_rk, rw_ln_g, rw_ln_b):
    n_even, n_odd = e_w_in.shape[0], o_w_in.shape[0]
    W = dict(norm_g=norm_g, ffn_w_in=_ffn_w_in_tiles(ffn_w_in), ffn_w_out=ffn_w_out.astype(BF16),
             e_w_in=_pad_cols(e_w_in.astype(BF16)), e_w_out=e_w_out.astype(BF16),
             lru_conv_w=lru_conv_w, lru_conv_b=lru_conv_b, lru_wa=lru_wa, lru_ba=lru_ba, lru_wx=lru_wx,
             lru_bx=lru_bx, lru_lambda=lru_lambda, fox_q_gain=fox_q_gain, fox_k_gain=fox_k_gain,
             fox_f_bias=fox_f_bias, o_w_in=o_w_in.astype(BF16), o_w_out=o_w_out.astype(BF16),
             hg_lb_logits=hg_lb_logits, hg_norm_g=hg_norm_g, rw_mu=rw_mu, rw_w0=rw_w0, rw_w2=rw_w2, rw_a0=rw_a0,
             rw_a2=rw_a2, rw_g2=rw_g2, rw_kk=rw_kk, rw_ka=rw_ka, rw_rk=rw_rk, rw_ln_g=rw_ln_g, rw_ln_b=rw_ln_b)
    nb = x_prompt.shape[0]
    dt = x_prompt.dtype
    prompt_states = (jnp.zeros((n_even, nb, CONV_W - 1, GROUP_W), dt),
                     jnp.zeros((n_even, nb, GROUP_W), dt),
                     jnp.zeros((n_even, nb, 0, FOX_HEADS, FOX_HD), dt),
                     jnp.zeros((n_even, nb, 0, FOX_HEADS, FOX_HD), dt),
                     jnp.zeros((n_even, nb, 0, FOX_HEADS), dt),
                     jnp.zeros((n_odd, nb, HG_HEADS, GROUP_W // HG_HEADS, GROUP_W // HG_HEADS), dt),
                     jnp.zeros((n_odd, nb, rw_mu.shape[1]), dt),
                     jnp.zeros((n_odd, nb, RW_HEADS, RW_HD, RW_HD), dt))
    sample_states = (state_lru_conv, state_lru_h, cache_fox_k, cache_fox_v, cache_fox_logf,
                     state_hgrn_S, state_rwkv_shift, state_rwkv_S)
    y_prompt, p_new = _trunk(x_prompt, prompt_states, W)
    y_sample, s_new = _trunk(x_sample, sample_states, W)
    lru_conv_p, lru_h_p, fox_k_p, fox_v_p, fox_logf_p, hgrn_S_p, rwkv_shift_p, rwkv_S_p = p_new
    lru_conv_s, lru_h_s, fox_k_s, fox_v_s, fox_logf_s, hgrn_S_s, rwkv_shift_s, rwkv_S_s = s_new
    return (y_prompt, y_sample, lru_conv_p, lru_conv_s, lru_h_p, lru_h_s, fox_k_p, fox_k_s, fox_v_p, fox_v_s,
            fox_logf_p, fox_logf_s, hgrn_S_p, hgrn_S_s, rwkv_shift_p, rwkv_shift_s, rwkv_S_p, rwkv_S_s)
```

```python
import functools

import jax
import jax.numpy as jnp
from jax import lax
from jax.experimental import pallas as pl
from jax.experimental.pallas import tpu as pltpu

F32 = jnp.float32
BF16 = jnp.bfloat16

NORM_EPS = 1e-6
GROUP_W = 512
CONV_W = 4
LRU_C = 8.0
FOX_HEADS = 8
FOX_HD = 64
HG_HEADS = 4
RW_HEADS = 8
RW_HD = 64
RW_DECAY_LORA = 64
RW_A_LORA = 64
RW_GATE_LORA = 128
RW_LN_EPS = 64e-5

V7X_LANES = 128
V7X_MXU = 256
FFN_COL_TILE = 1408
PREP_ROWS = 512
FFN_ROW_TILE = 1024
FFN_VMEM_MIB = 60


def _row_tile(n, want):
    t = min(n, want)
    while n % t:
        t //= 2
    return t


def _params(sem, vmem_mib):
    return pltpu.CompilerParams(dimension_semantics=sem, vmem_limit_bytes=vmem_mib << 20)


def _pad_cols(w):
    pad = -w.shape[-1] % V7X_LANES
    return jnp.pad(w, [(0, 0)] * (w.ndim - 1) + [(0, pad)])


def _head_ones(hd):
    return jnp.kron(jnp.eye(V7X_MXU // hd, dtype=F32), jnp.ones((hd, hd), F32)).astype(BF16)


def _rms(x, g):
    return x * lax.rsqrt(jnp.mean(x * x, axis=-1, keepdims=True) + NORM_EPS) * g


def _ffn_body(*refs, mixed):
    if mixed:
        x_ref, a_ref, b_ref, wa_ref, wb_ref, g_ref, wi_ref, wo_ref, o_ref, h_ref, acc_ref = refs
    else:
        x_ref, g_ref, wi_ref, wo_ref, o_ref, h_ref, acc_ref = refs
    j = pl.program_id(1)

    @pl.when(j == 0)
    def _():
        x = x_ref[...]
        if mixed:
            x = x + jnp.dot(a_ref[...], wa_ref[...], preferred_element_type=F32)
            x = x + jnp.dot(b_ref[...], wb_ref[...], preferred_element_type=F32)
        h_ref[...] = _rms(x, g_ref[...]).astype(BF16)
        acc_ref[...] = 2.0 * x

    tf = wo_ref.shape[0]
    gu = jnp.dot(h_ref[...], wi_ref[...], preferred_element_type=F32)
    gate, up = gu[:, :tf], gu[:, tf:]
    act = (gate * jax.nn.sigmoid(gate) * up).astype(BF16)
    acc_ref[...] += jnp.dot(act, wo_ref[...], preferred_element_type=F32)

    @pl.when(j == pl.num_programs(1) - 1)
    def _():
        o_ref[...] = 0.5 * acc_ref[...]


def _cast_body(x_ref, o_ref):
    o_ref[...] = x_ref[...].astype(o_ref.dtype)


def _ffn_w_in_tiles(w_in):
    *lead, d, f2 = w_in.shape
    tf = FFN_COL_TILE
    nf = f2 // 2 // tf
    w = w_in.reshape(-1, d, f2)
    out = pl.pallas_call(
        _cast_body,
        grid=(w.shape[0], nf, 2),
        in_specs=[pl.BlockSpec((1, d, tf), lambda i, j, gu: (i, 0, gu * nf + j))],
        out_specs=pl.BlockSpec((1, d, tf), lambda i, j, gu: (i, 0, 2 * j + gu)),
        out_shape=jax.ShapeDtypeStruct(w.shape, BF16),
        compiler_params=_params(("parallel", "parallel", "parallel"), 32),
        name="ffn_weight_tiles",
    )(w)
    return out.reshape(*lead, d, f2)


def _ffn(x, g, w_in, w_out, mix=None):
    n, d = x.shape
    f = w_out.shape[0]
    tm = _row_tile(n, FFN_ROW_TILE)
    tf = FFN_COL_TILE
    nf = f // tf
    row_spec = lambda w: pl.BlockSpec((tm, w), lambda i, j: (i, 0))
    args, specs = [x], [row_spec(d)]
    if mix is not None:
        a, b, w = mix
        ga, gb = a.shape[1], b.shape[1]
        args += [a, b, w[:ga], w[ga:]]
        specs += [row_spec(ga), row_spec(gb), pl.BlockSpec((ga, d), lambda i, j: (0, 0)),
                  pl.BlockSpec((gb, d), lambda i, j: (0, 0))]
    args += [g.reshape(1, d), w_in, w_out]
    specs += [pl.BlockSpec((1, d), lambda i, j: (0, 0)), pl.BlockSpec((d, 2 * tf), lambda i, j: (0, j)),
              pl.BlockSpec((tf, d), lambda i, j: (j, 0))]
    return pl.pallas_call(
        functools.partial(_ffn_body, mixed=mix is not None),
        grid=(n // tm, nf),
        in_specs=specs,
        out_specs=row_spec(d),
        out_shape=jax.ShapeDtypeStruct((n, d), F32),
        scratch_shapes=[pltpu.VMEM((tm, d), BF16), pltpu.VMEM((tm, d), F32)],
        compiler_params=_params(("parallel", "arbitrary"), FFN_VMEM_MIB),
        name="ffn",
    )(*args)


def _norm_matmul_body(x_ref, g_ref, w_ref, *o_refs):
    h = _rms(x_ref[...], g_ref[...]).astype(BF16)
    z = jnp.dot(h, w_ref[...], preferred_element_type=F32)
    start = 0
    for o_ref in o_refs:
        width = o_ref.shape[1]
        o_ref[...] = z[:, start:start + width]
        start += width


def _norm_matmul(x, g, w, widths):
    n, d = x.shape
    c = w.shape[1]
    assert sum(widths) == c and all(wd % V7X_LANES == 0 for wd in widths)
    tm = _row_tile(n, 512)
    return pl.pallas_call(
        _norm_matmul_body,
        grid=(n // tm,),
        in_specs=[
            pl.BlockSpec((tm, d), lambda i: (i, 0)),
            pl.BlockSpec((1, d), lambda i: (0, 0)),
            pl.BlockSpec((d, c), lambda i: (0, 0)),
        ],
        out_specs=[pl.BlockSpec((tm, wd), lambda i: (i, 0)) for wd in widths],
        out_shape=[jax.ShapeDtypeStruct((n, wd), F32) for wd in widths],
        compiler_params=_params(("parallel",), 48),
        name="norm_matmul",
    )(x, g.reshape(1, d), w)


LRU_ROWS = 256
CONV_PAD = 8


def _expm1(x):
    series = x * (1.0 + x * (1 / 2 + x * (1 / 6 + x * (1 / 24 + x * (1 / 120 + x * (1 / 720 + x * (1 / 5040 + x * (1 / 40320))))))))
    return jnp.where(jnp.abs(x) < 0.25, series, jnp.exp(x) - 1.0)


def _shift_rows(x, s, fill):
    row = lax.broadcasted_iota(jnp.int32, x.shape, 0)
    return jnp.where(row >= s, pltpu.roll(x, s, axis=0), fill)


def _lru_body(z_ref, buf_ref, h0_ref, cw_ref, cb_ref, wa_ref, ba_ref, wx_ref, bx_ref, lam_ref,
              o_ref, bufo_ref, ho_ref, x_ref, hc_ref):
    G = GROUP_W
    tt = z_ref.shape[0]

    @pl.when(pl.program_id(1) == 0)
    def _():
        x_ref[0:CONV_PAD, :] = buf_ref[0]
        hc_ref[...] = jnp.broadcast_to(h0_ref[0], hc_ref.shape)

    x_ref[CONV_PAD:CONV_PAD + tt, :] = z_ref[:, 0:G]
    xc = cb_ref[...]
    for j in range(CONV_W):
        lo = CONV_PAD - (CONV_W - 1) + j
        xc = xc + x_ref[lo:lo + tt, :] * cw_ref[j:j + 1, :]
    hist = x_ref[tt:tt + CONV_PAD, :]
    x_ref[0:CONV_PAD, :] = hist
    bufo_ref[0] = hist

    xb = xc.astype(BF16)
    r = jax.nn.sigmoid(jnp.dot(xb, wa_ref[...], preferred_element_type=F32) + ba_ref[...])
    ig = jax.nn.sigmoid(jnp.dot(xb, wx_ref[...], preferred_element_type=F32) + bx_ref[...])
    log_a = (-LRU_C * _softplus(-lam_ref[...])) * r
    a = jnp.exp(log_a)
    b = jnp.sqrt(-_expm1(2.0 * log_a)) * (ig * xc)
    s = 1
    while s < tt:
        if s % 8:
            b = a * _shift_rows(b, s, 0.0) + b
            a = a * _shift_rows(a, s, 1.0)
        else:
            b = jnp.concatenate([b[:s], a[s:] * b[:tt - s] + b[s:]], axis=0)
            a = jnp.concatenate([a[:s], a[s:] * a[:tt - s]], axis=0)
        s *= 2
    h = a * hc_ref[0:1, :] + b
    hc_ref[...] = jnp.broadcast_to(h[tt - 1:tt, :], hc_ref.shape)
    ho_ref[0] = h[tt - 1:tt, :]
    o_ref[...] = (jax.nn.gelu(z_ref[:, G:2 * G]) * h).astype(o_ref.dtype)


def _block_diag_dense(w):
    nb, bs, _ = w.shape
    eye = jnp.eye(nb, dtype=w.dtype)
    return (eye[:, None, :, None] * w[:, :, None, :]).reshape(nb * bs, nb * bs)


def _lru(z_rg, conv_buf, h0, B, T, P):
    G = GROUP_W
    n = B * T
    tt = _row_tile(T, LRU_ROWS)
    nt = T // tt
    buf = jnp.pad(conv_buf, ((0, 0), (CONV_PAD - (CONV_W - 1), 0), (0, 0)))
    cw = jnp.pad(P['lru_conv_w'], ((0, CONV_PAD - CONV_W), (0, 0)))
    row = lambda x: x.reshape(1, G)
    full = lambda shape: pl.BlockSpec(shape, lambda b, i: (0,) * len(shape))
    out, bufo, ho = pl.pallas_call(
        _lru_body,
        grid=(B, nt),
        in_specs=[
            pl.BlockSpec((tt, 2 * G), lambda b, i: (b * nt + i, 0)),
            pl.BlockSpec((1, CONV_PAD, G), lambda b, i: (b, 0, 0)),
            pl.BlockSpec((1, 1, G), lambda b, i: (b, 0, 0)),
            full((CONV_PAD, G)), full((1, G)), full((G, G)), full((1, G)), full((G, G)), full((1, G)), full((1, G)),
        ],
        out_specs=[
            pl.BlockSpec((tt, G), lambda b, i: (b * nt + i, 0)),
            pl.BlockSpec((1, CONV_PAD, G), lambda b, i: (b, 0, 0)),
            pl.BlockSpec((1, 1, G), lambda b, i: (b, 0, 0)),
        ],
        out_shape=[jax.ShapeDtypeStruct((n, G), BF16), jax.ShapeDtypeStruct((B, CONV_PAD, G), F32),
                   jax.ShapeDtypeStruct((B, 1, G), F32)],
        scratch_shapes=[pltpu.VMEM((tt + CONV_PAD, G), F32), pltpu.VMEM((8, G), F32)],
        compiler_params=_params(("parallel", "arbitrary"), 32),
        name="lru",
    )(z_rg, buf, h0.reshape(B, 1, G), cw, row(P['lru_conv_b']), _block_diag_dense(P['lru_wa']).astype(BF16),
      row(P['lru_ba']), _block_diag_dense(P['lru_wx']).astype(BF16), row(P['lru_bx']), row(P['lru_lambda']))
    return out, bufo[:, CONV_PAD - (CONV_W - 1):], ho.reshape(B, G)


FOX_Q_COLS = 2048
FOX_K_ROWS = 512
FOX_F_SPLIT = 3
FOX_NEG = -1e30
LOG2E = 1.4426950408889634
HEAD_PAIRS = FOX_HEADS // 2
PAIR_W = 2 * FOX_HD
FOX_CUMSUM_ROWS = 256
FOX_KEY_TILE = 512


def _fox_prep_body(z_ref, fl_ref, qg_ref, kg_ref, fb_ref, ones_ref, q_ref, k_ref, kb_ref, v_ref, vb_ref, lf_ref,
                   *, q_transposed):
    G = GROUP_W
    q, k, v = z_ref[:, 0:G], z_ref[:, G:2 * G], z_ref[:, 2 * G:3 * G]
    inv = 1.0 / FOX_HD
    qn = q * lax.rsqrt(_head_sums(q * q, ones_ref[...]) * inv + NORM_EPS) * qg_ref[...]
    kn = k * lax.rsqrt(_head_sums(k * k, ones_ref[...]) * inv + NORM_EPS) * kg_ref[...]
    qs = qn * (LOG2E * FOX_HD ** -0.5)
    if q_transposed:
        for p in range(HEAD_PAIRS):
            q_ref[0, p] = qs[:, p * PAIR_W:(p + 1) * PAIR_W].T.astype(BF16)
    else:
        q_ref[...] = qs.astype(BF16)
    tt = z_ref.shape[0]
    for h in range(FOX_HEADS):
        k_ref[pl.ds(h, tt, stride=FOX_HEADS), :] = kn[:, h * FOX_HD:(h + 1) * FOX_HD]
        v_ref[pl.ds(h, tt, stride=FOX_HEADS), :] = v[:, h * FOX_HD:(h + 1) * FOX_HD]
    kb_ref[...] = kn.astype(BF16)
    vb_ref[...] = v.astype(BF16)
    x = fl_ref[...] + fb_ref[...]
    lf_ref[...] = -_softplus(-x)


def _fox_prep(z_qkv, z_fl, B, T, P):
    n = z_qkv.shape[0]
    G = GROUP_W
    tt = _row_tile(T, PREP_ROWS)
    nt = T // tt
    q_transposed = tt % V7X_LANES == 0
    ones_bd = _head_ones(FOX_HD)
    fb = jnp.pad(P['fox_f_bias'], (0, V7X_LANES - FOX_HEADS)).reshape(1, V7X_LANES)
    tile = lambda w: pl.BlockSpec((tt, w), lambda b, i: (b * nt + i, 0))
    full = lambda shape: pl.BlockSpec(shape, lambda b, i: (0,) * len(shape))
    if q_transposed:
        q_spec = pl.BlockSpec((1, HEAD_PAIRS, PAIR_W, tt), lambda b, i: (b, 0, 0, i))
        q_shape = jax.ShapeDtypeStruct((B, HEAD_PAIRS, PAIR_W, T), BF16)
    else:
        q_spec, q_shape = tile(G), jax.ShapeDtypeStruct((n, G), BF16)
    state_spec = pl.BlockSpec((tt * FOX_HEADS, FOX_HD), lambda b, i: (b * nt + i, 0))
    state_shape = jax.ShapeDtypeStruct((n * FOX_HEADS, FOX_HD), F32)
    return pl.pallas_call(
        functools.partial(_fox_prep_body, q_transposed=q_transposed),
        grid=(B, nt),
        in_specs=[tile(3 * G), tile(V7X_LANES), full((1, G)), full((1, G)), full((1, V7X_LANES)),
                  full((V7X_MXU, V7X_MXU))],
        out_specs=[q_spec, state_spec, tile(G), state_spec, tile(G), tile(V7X_LANES)],
        out_shape=[q_shape, state_shape, jax.ShapeDtypeStruct((n, G), BF16), state_shape,
                   jax.ShapeDtypeStruct((n, G), BF16), jax.ShapeDtypeStruct((n, V7X_LANES), F32)],
        compiler_params=_params(("parallel", "parallel"), 32),
        name="fox_prep",
    )(z_qkv, z_fl, jnp.tile(P['fox_q_gain'], FOX_HEADS).reshape(1, G),
      jnp.tile(P['fox_k_gain'], FOX_HEADS).reshape(1, G), fb, ones_bd)


def _fox_keys_body(lf_ref, kb_ref, vb_ref, ka_ref, vt_ref, c_ref):
    tt = lf_ref.shape[1]

    @pl.when(pl.program_id(1) == 0)
    def _():
        c_ref[...] = jnp.zeros_like(c_ref)

    grp = min(tt, FOX_CUMSUM_ROWS)
    carry, fs = c_ref[0:1, :], []
    for i in range(0, tt, grp):
        fs.append(_chunk_cumsum(lf_ref[0, i:i + grp, :], grp) + carry)
        carry = fs[-1][grp - 1:grp, :]
    f = jnp.concatenate(fs, axis=0) if len(fs) > 1 else fs[0]
    c_ref[...] = jnp.broadcast_to(carry, c_ref.shape)
    parts = _split3(f * LOG2E)
    srow = lax.broadcasted_iota(jnp.int32, (V7X_LANES, GROUP_W), 0)
    scol = lax.broadcasted_iota(jnp.int32, (V7X_LANES, GROUP_W), 1)
    aug = jnp.zeros((tt, GROUP_W), F32)
    for t, part in enumerate(parts):
        sel = jnp.where((scol // FOX_HD == srow) & (scol % FOX_HD == t), 1.0, 0.0).astype(BF16)
        aug = aug + jnp.dot(part, sel, preferred_element_type=F32)
    aug = aug.astype(BF16)
    for h in range(FOX_HEADS):
        hs = slice(h * FOX_HD, (h + 1) * FOX_HD)
        ka_ref[0, h] = jnp.concatenate([kb_ref[0, :, hs], aug[:, hs]], axis=1)
    for p in range(HEAD_PAIRS):
        vt_ref[0, p] = vb_ref[0, :, p * PAIR_W:(p + 1) * PAIR_W].astype(F32).T.astype(BF16)


def _fox_keys(lf_all, kb_all, vb_all):
    B, tk_all, L = lf_all.shape
    G = GROUP_W
    tt = FOX_KEY_TILE
    return pl.pallas_call(
        _fox_keys_body,
        grid=(B, tk_all // tt),
        in_specs=[pl.BlockSpec((1, tt, L), lambda b, i: (b, i, 0)),
                  pl.BlockSpec((1, tt, G), lambda b, i: (b, i, 0)),
                  pl.BlockSpec((1, tt, G), lambda b, i: (b, i, 0))],
        out_specs=[pl.BlockSpec((1, FOX_HEADS, tt, 2 * FOX_HD), lambda b, i: (b, 0, i, 0)),
                   pl.BlockSpec((1, HEAD_PAIRS, PAIR_W, tt), lambda b, i: (b, 0, 0, i))],
        out_shape=[jax.ShapeDtypeStruct((B, FOX_HEADS, tk_all, 2 * FOX_HD), BF16),
                   jax.ShapeDtypeStruct((B, HEAD_PAIRS, PAIR_W, tk_all), BF16)],
        scratch_shapes=[pltpu.VMEM((8, L), F32)],
        compiler_params=_params(("parallel", "arbitrary"), 32),
        name="fox_keys",
    )(lf_all, kb_all, vb_all)


def _fox_attn_body(qt_ref, ka_ref, vt_ref, og_ref, o_ref, acc_ref, m_ref, l_ref, *, past, tk, t_real):
    qi = pl.program_id(2)
    tq = qt_ref.shape[3]
    t_out = o_ref.shape[0]
    first_q = past + qi * tq
    last_q = past + jnp.minimum(qi * tq + tq, t_real) - 1
    n_full = (first_q + 1) // tk
    n_all = last_q // tk + 1
    drow = lax.broadcasted_iota(jnp.int32, (FOX_HD, tq), 0)
    minus = jnp.where(drow < FOX_F_SPLIT, -1.0, 0.0).astype(BF16)
    rhs = [jnp.concatenate([qt_ref[0, 0, h * FOX_HD:(h + 1) * FOX_HD, :], minus], axis=0) for h in range(2)]
    acc_ref[...] = jnp.zeros_like(acc_ref)

    def update(ki, m_prev, l_prev, masked, q0=0):
        ks = pl.multiple_of(ki * tk, tk)
        s = [jnp.dot(ka_ref[0, h, pl.ds(ks, tk), :], rhs[h][:, q0:], preferred_element_type=F32) for h in range(2)]
        if masked:
            krow = lax.broadcasted_iota(jnp.int32, (tk, tq - q0), 0)
            qcol = lax.broadcasted_iota(jnp.int32, (tk, tq - q0), 1)
            vis = ks + krow <= first_q + q0 + qcol
            s = [jnp.where(vis, s[h], FOX_NEG) for h in range(2)]
        m_new = [jnp.maximum(m_prev[h], jnp.max(s[h], axis=0, keepdims=True)) for h in range(2)]
        alpha = [jnp.exp2(m_prev[h] - m_new[h]) for h in range(2)]
        p = [jnp.exp2(s[h] - m_new[h]) for h in range(2)]
        l_new = [alpha[h] * l_prev[h] + jnp.sum(p[h], axis=0, keepdims=True) for h in range(2)]
        vt = [vt_ref[0, 0, h * FOX_HD:(h + 1) * FOX_HD, pl.ds(ks, tk)] for h in range(2)]
        pv = [jnp.dot(vt[h], p[h].astype(BF16), preferred_element_type=F32) for h in range(2)]
        for h in range(2):
            acc_ref[h, :, q0:] = alpha[h] * acc_ref[h, :, q0:] + pv[h]
        return m_new, l_new

    def carried(ki, c, masked):
        return update(ki, c[0], c[1], masked)

    c = ([jnp.full((1, tq), FOX_NEG, F32)] * 2, [jnp.zeros((1, tq), F32)] * 2)
    c = lax.fori_loop(0, n_full, lambda ki, c: carried(ki, c, False), c)
    if tq > tk and tq % tk == 0 and past % tq == 0 and t_real % tq == 0:
        m, l = carried(n_full, c, True)
        for h in range(2):
            m_ref[h], l_ref[h] = m[h], l[h]
        for j in range(1, tq // tk):
            q0 = j * tk
            m, l = update(n_full + j, [m_ref[h, :, q0:] for h in range(2)], [l_ref[h, :, q0:] for h in range(2)],
                          True, q0=q0)
            for h in range(2):
                m_ref[h, :, q0:], l_ref[h, :, q0:] = m[h], l[h]
    else:
        _, l = lax.fori_loop(n_full, n_all, lambda ki, c: carried(ki, c, True), c)
        for h in range(2):
            l_ref[h] = l[h]
    o_t = jnp.concatenate([acc_ref[h] / l_ref[h] for h in range(2)], axis=0)
    o_ref[...] = (o_t.T[:t_out] * jax.nn.sigmoid(og_ref[...])).astype(o_ref.dtype)


def _fox_attention(q, ka, vt, z_og, B, T, past):
    G = GROUP_W
    pw = PAIR_W
    tq = max(_row_tile(T, FOX_Q_COLS), V7X_LANES)
    tqp = -(-T // tq) * tq
    nq = tqp // tq
    t_out = min(tq, T)
    tk = FOX_K_ROWS
    tkp = ka.shape[2]
    if q.ndim == 2:
        q = q.reshape(B, T, HEAD_PAIRS, pw).transpose(0, 2, 3, 1)
    qt = jnp.pad(q, ((0, 0), (0, 0), (0, 0), (0, tqp - T)))
    return pl.pallas_call(
        functools.partial(_fox_attn_body, past=past, tk=tk, t_real=T),
        grid=(B, HEAD_PAIRS, nq),
        in_specs=[
            pl.BlockSpec((1, 1, pw, tq), lambda b, p, i: (b, p, 0, i)),
            pl.BlockSpec((1, 2, tkp, 2 * FOX_HD), lambda b, p, i: (b, p, 0, 0)),
            pl.BlockSpec((1, 1, pw, tkp), lambda b, p, i: (b, p, 0, 0)),
            pl.BlockSpec((t_out, pw), lambda b, p, i: (b * nq + i, p)),
        ],
        out_specs=pl.BlockSpec((t_out, pw), lambda b, p, i: (b * nq + i, p)),
        out_shape=jax.ShapeDtypeStruct((B * T, G), BF16),
        scratch_shapes=[pltpu.VMEM((2, FOX_HD, tq), F32), pltpu.VMEM((2, 1, tq), F32), pltpu.VMEM((2, 1, tq), F32)],
        compiler_params=_params(("parallel", "parallel", "arbitrary"), 40),
        name="fox_attn",
    )(qt, ka, vt, z_og)


HG_CHUNK = 64
HG_STEP_CHUNKS = 16


def _hgrn_body(z_ref, lb_ref, s0_ref, ng_ref, o_ref, so_ref, st_ref, *, c):
    G = GROUP_W
    rows = z_ref.shape[0]
    nch = rows // c
    dk = G // HG_HEADS

    @pl.when(pl.program_id(1) == 0)
    def _():
        st_ref[...] = s0_ref[0]

    lb = lb_ref[...]
    f = lb + (1.0 - lb) * jax.nn.sigmoid(z_ref[:, G:2 * G])
    kx = 1.0 - f
    crow, ccol = _tri_masks(c)
    incl = ccol <= crow
    gs = _chunk_cumsum(jnp.log(f), c)
    qg_all = z_ref[:, 0:G] * jnp.exp(gs)
    kg_all = kx * jnp.exp(-gs)
    HS = range(HG_HEADS)
    units = [(cc, h) for cc in range(nch) for h in HS]
    US = range(len(units))
    rsl = [slice(cc * c, (cc + 1) * c) for cc, _ in units]
    lsl = [slice(h * dk, (h + 1) * dk) for _, h in units]
    g_last = [gs[(cc + 1) * c - 1:(cc + 1) * c, lsl[u]] for u, (cc, _) in enumerate(units)]
    vv = [z_ref[rsl[u], 2 * G + h * dk:2 * G + (h + 1) * dk] for u, (_, h) in enumerate(units)]
    A = [jnp.where(incl, _dot_lo(qg_all[rsl[u], lsl[u]], kg_all[rsl[u], lsl[u]], _NT), 0.0) for u in US]
    av = [_dot_lo(A[u], vv[u]) for u in US]
    kd = [kx[rsl[u], lsl[u]] * jnp.exp(g_last[u] - gs[rsl[u], lsl[u]]) for u in US]
    upd = [_dot_lo(vv[u], kd[u], _TN) for u in US]
    st = [st_ref[h] for h in HS]
    o = [None for _ in US]
    for cc in range(nch):
        for h in HS:
            u = cc * HG_HEADS + h
            o[u] = _dot_lo(qg_all[rsl[u], lsl[u]], st[h], _NT) + av[u]
        st = [st[h] * jnp.exp(g_last[cc * HG_HEADS + h]) + upd[cc * HG_HEADS + h] for h in HS]
    for h in HS:
        st_ref[h] = st[h]
    for u, (_, h) in enumerate(units):
        hg = z_ref[rsl[u], 3 * G + h * dk:3 * G + (h + 1) * dk]
        o_ref[rsl[u], lsl[u]] = (_rms(o[u], ng_ref[:, lsl[u]]) * (hg * jax.nn.sigmoid(hg))).astype(o_ref.dtype)

    @pl.when(pl.program_id(1) == pl.num_programs(1) - 1)
    def _():
        so_ref[0] = st_ref[...]


def _hgrn2(z_hg, lb, S0, B, T, P):
    G = GROUP_W
    c = min(HG_CHUNK, T)
    rows = _row_tile(T, c * HG_STEP_CHUNKS)
    nc = T // rows
    dk = G // HG_HEADS
    st_spec = pl.BlockSpec((1, HG_HEADS, dk, dk), lambda b, i: (b, 0, 0, 0))
    out, so = pl.pallas_call(
        functools.partial(_hgrn_body, c=c),
        grid=(B, nc),
        in_specs=[pl.BlockSpec((rows, 4 * G), lambda b, i: (b * nc + i, 0)),
                  pl.BlockSpec((1, G), lambda b, i: (0, 0)), st_spec, pl.BlockSpec((1, G), lambda b, i: (0, 0))],
        out_specs=[pl.BlockSpec((rows, G), lambda b, i: (b * nc + i, 0)), st_spec],
        out_shape=[jax.ShapeDtypeStruct((B * T, G), BF16), jax.ShapeDtypeStruct(S0.shape, F32)],
        scratch_shapes=[pltpu.VMEM((HG_HEADS, dk, dk), F32)],
        compiler_params=_params(("parallel", "arbitrary"), 32),
        name="hgrn2",
    )(z_hg, lb.reshape(1, G), jnp.swapaxes(S0, -1, -2), P['hg_norm_g'].reshape(1, G))
    return out, jnp.swapaxes(so, -1, -2)


RW_CHUNK = 64
RW_SUB = 16
RW_MAIN_CHUNKS = 2

_NT = (((1,), (1,)), ((), ()))
_TN = (((0,), (0,)), ((), ()))
_NN = (((1,), (0,)), ((), ()))


def _split3(x):
    h1 = x.astype(BF16)
    r1 = x - h1.astype(F32)
    h2 = r1.astype(BF16)
    h3 = (r1 - h2.astype(F32)).astype(BF16)
    return h1, h2, h3


def _dot_lo(a, b, dims=_NN):
    return lax.dot_general(a.astype(BF16), b.astype(BF16), dims, preferred_element_type=F32)


def _dot_hi(a, b, dims=_NN):
    ah = a.astype(BF16)
    al = (a - ah.astype(F32)).astype(BF16)
    bh = b.astype(BF16)
    bl = (b - bh.astype(F32)).astype(BF16)
    d = functools.partial(lax.dot_general, dimension_numbers=dims, preferred_element_type=F32)
    return d(ah, bh) + (d(al, bh) + d(ah, bl))


def _dot_exact_rhs(a, b):
    h1, h2, h3 = _split3(a)
    d = functools.partial(jnp.dot, preferred_element_type=F32)
    return d(h1, b) + (d(h2, b) + d(h3, b))


def _head_sums(x, ones):
    w = ones.shape[0]
    return jnp.concatenate([_dot_exact_rhs(x[:, i:i + w], ones) for i in range(0, x.shape[1], w)], axis=1)


def _dot_exact_lhs(a, b):
    h1, h2, h3 = _split3(b)
    d = functools.partial(jnp.dot, preferred_element_type=F32)
    return d(a, h1) + (d(a, h2) + d(a, h3))


def _softplus(x):
    return jnp.maximum(x, 0.0) + jnp.log1p(jnp.exp(-jnp.abs(x)))


def _rw_prep_body(x_ref, ng_ref, wz_ref, shift_ref, mu_ref, w0_ref, w2_ref, a0_ref, a2_ref, g2_ref, kk_ref, ka_ref,
                  ones_ref, r_ref, lw_ref, k_ref, v_ref, kap_ref, bet_ref, g_ref, ld_ref, sh_ref, prev_ref, *, c):
    G = GROUP_W

    @pl.when(pl.program_id(1) == 0)
    def _():
        prev_ref[0:1, :] = shift_ref[0]

    z = jnp.dot(_rms(x_ref[...], ng_ref[...]).astype(BF16), wz_ref[...], preferred_element_type=F32)
    tt = z.shape[0]
    sh_ref[0] = z[tt - 1:tt, :]
    row = lax.broadcasted_iota(jnp.int32, z.shape, 0)
    shifted = jnp.where(row == 0, prev_ref[0:1, :], pltpu.roll(z, 1, axis=0))
    prev_ref[0:1, :] = z[tt - 1:tt, :]
    zm = z + (shifted - z) * mu_ref[...]
    r, k, v = zm[:, 0:G], zm[:, G:2 * G], zm[:, 2 * G:3 * G]
    o = 3 * G
    wd = zm[:, o:o + RW_DECAY_LORA]
    ad = zm[:, o + RW_DECAY_LORA:o + RW_DECAY_LORA + RW_A_LORA]
    gd = zm[:, o + RW_DECAY_LORA + RW_A_LORA:]
    w = -_softplus(-(w0_ref[...] + _dot_lo(jnp.tanh(wd), w2_ref[...]))) - 0.5
    a = jax.nn.sigmoid(a0_ref[...] + _dot_lo(ad, a2_ref[...]))
    kk = k * kk_ref[...]
    ss = _head_sums(kk * kk, ones_ref[...])
    kap = kk / jnp.maximum(jnp.sqrt(ss), 1e-12)
    lw = -jnp.exp(w)
    bet = kap * a
    r_ref[...] = r
    lw_ref[...] = lw
    k_ref[...] = k * (1.0 + (a - 1.0) * ka_ref[...])
    v_ref[...] = v
    kap_ref[...] = kap
    bet_ref[...] = bet
    g_ref[...] = _dot_lo(jax.nn.sigmoid(gd), g2_ref[...])
    _rw_ldiag_tile(lw, kap, bet, ld_ref, c)


def _rw_prep(x2, norm_g, w_rw, shift, B, T, P):
    n, d = x2.shape
    cols = w_rw.shape[1]
    G = GROUP_W
    tt = _row_tile(T, PREP_ROWS)
    nt = T // tt
    ones_bd = _head_ones(RW_HD)
    row = lambda x: x.reshape(1, -1)
    full = lambda shape: pl.BlockSpec(shape, lambda b, i: (0,) * len(shape))
    tile = pl.BlockSpec((tt, G), lambda b, i: (b * nt + i, 0))
    ld_tile = pl.BlockSpec((tt, RW_HEADS * RW_SUB), lambda b, i: (b * nt + i, 0))
    return pl.pallas_call(
        functools.partial(_rw_prep_body, c=min(RW_CHUNK, T)),
        grid=(B, nt),
        in_specs=[
            pl.BlockSpec((tt, d), lambda b, i: (b * nt + i, 0)), full((1, d)), full((d, cols)),
            pl.BlockSpec((1, 1, cols), lambda b, i: (b, 0, 0)),
            full((1, cols)), full((1, G)), full((RW_DECAY_LORA, G)), full((1, G)), full((RW_A_LORA, G)),
            full((RW_GATE_LORA, G)), full((1, G)), full((1, G)), full((V7X_MXU, V7X_MXU)),
        ],
        out_specs=[tile] * 7 + [ld_tile, pl.BlockSpec((1, 1, cols), lambda b, i: (b, 0, 0))],
        out_shape=[jax.ShapeDtypeStruct((n, G), F32)] * 7 + [jax.ShapeDtypeStruct((n, RW_HEADS * RW_SUB), F32),
                                                             jax.ShapeDtypeStruct((B, 1, cols), F32)],
        scratch_shapes=[pltpu.VMEM((8, cols), F32)],
        compiler_params=_params(("parallel", "arbitrary"), 48),
        name="rwkv_prep",
    )(x2, norm_g.reshape(1, d), w_rw, shift.reshape(B, 1, cols), row(P['rw_mu']), row(P['rw_w0']),
      P['rw_w2'].astype(BF16), row(P['rw_a0']), P['rw_a2'].astype(BF16), P['rw_g2'].astype(BF16), row(P['rw_kk']),
      row(P['rw_ka']), ones_bd)


def _rw_scaled(lw, kap, bet, c):
    cs = _chunk_cumsum(lw, c)
    return cs, kap * jnp.exp(cs - lw), bet * jnp.exp(-cs)


def _tri_masks(c):
    row = lax.broadcasted_iota(jnp.int32, (c, c), 0)
    col = lax.broadcasted_iota(jnp.int32, (c, c), 1)
    return row, col


def _chunk_cumsum(x, c):
    row, col = _tri_masks(c)
    tri = jnp.where(col <= row, 1.0, 0.0).astype(BF16)
    parts = [_dot_exact_lhs(tri, x[i:i + c]) for i in range(0, x.shape[0], c)]
    return parts[0] if len(parts) == 1 else jnp.concatenate(parts, axis=0)


def _rw_ldiag_tile(lw, kap, bet, o_ref, c):
    rows = lw.shape[0]
    _, kk_all, bt_all = _rw_scaled(lw, kap, bet, c)
    srow, scol = _tri_masks(RW_SUB)
    units = [(cc, h) for cc in range(rows // c) for h in range(RW_HEADS)]
    Ls = [_dot_lo(kk_all[cc * c:(cc + 1) * c, h * RW_HD:(h + 1) * RW_HD],
                  bt_all[cc * c:(cc + 1) * c, h * RW_HD:(h + 1) * RW_HD], _NT) for cc, h in units]
    for (cc, h), L in zip(units, Ls):
        for b in range(c // RW_SUB):
            rs = slice(b * RW_SUB, (b + 1) * RW_SUB)
            o_ref[cc * c + b * RW_SUB:cc * c + (b + 1) * RW_SUB, h * RW_SUB:(h + 1) * RW_SUB] = (
                jnp.where(scol < srow, L[rs, rs], 0.0))


def _rw_inv_body(l_ref, t_ref, a_ref, b_ref):
    n = RW_SUB
    nblk = l_ref.shape[0] // n
    for t in range(n):
        a_ref[t] = l_ref[pl.ds(t, nblk, stride=n), :].T
    entry = lambda ref, t, s: ref.at[t, pl.ds(s, RW_HEADS, stride=n), :]
    one = jnp.ones((RW_HEADS, nblk), F32)
    zero = jnp.zeros((RW_HEADS, nblk), F32)
    for t in range(n):
        for s in range(n):
            if s > t:
                entry(b_ref, t, s)[...] = zero
            elif s == t:
                entry(b_ref, t, s)[...] = one
            else:
                acc = entry(a_ref, t, s)[...]
                for j in range(s + 1, t):
                    acc = acc + entry(a_ref, t, j)[...] * entry(b_ref, j, s)[...]
                entry(b_ref, t, s)[...] = -acc
    for t in range(n):
        t_ref[pl.ds(t, nblk, stride=n), :] = b_ref[t].T


def _rw_main_body(r_ref, lw_ref, k_ref, v_ref, kap_ref, bet_ref, g_ref, td_ref, h0_ref, rk_ref, lng_ref, lnb_ref,
                  o_ref, hout_ref, h_ref, *, c):
    ci = pl.program_id(1)
    rows = r_ref.shape[0]
    nb = c // RW_SUB

    @pl.when(ci == 0)
    def _():
        h_ref[...] = h0_ref[0]

    crow, ccol = _tri_masks(c)
    strict = ccol < crow
    incl = ccol <= crow
    lw = lw_ref[...]
    cs, kk_all, bt_all = _rw_scaled(lw, kap_ref[...], bet_ref[...], c)
    gi = jnp.exp(-cs)
    gg = jnp.exp(cs)
    kt_all = k_ref[...] * gi
    rt_all = r_ref[...] * gg
    bonus_all = r_ref[...] * k_ref[...] * rk_ref[...]
    hrow = lax.broadcasted_iota(jnp.int32, (RW_HD, RW_HD), 0)
    hcol = lax.broadcasted_iota(jnp.int32, (RW_HD, RW_HD), 1)
    HS = range(RW_HEADS)
    units = [(cc, h) for cc in range(rows // c) for h in HS]
    US = range(len(units))
    rsl = [slice(cc * c, (cc + 1) * c) for cc, _ in units]
    lsl = [slice(h * RW_HD, (h + 1) * RW_HD) for _, h in units]
    Kk = [kk_all[rsl[u], lsl[u]] for u in US]
    Bt = [bt_all[rsl[u], lsl[u]] for u in US]
    Kt = [kt_all[rsl[u], lsl[u]] for u in US]
    Rt = [rt_all[rsl[u], lsl[u]] for u in US]
    vv = [v_ref[rsl[u], lsl[u]] for u in US]
    Lm = [jnp.where(strict, _dot_lo(Kk[u], Bt[u], _NT), 0.0) for u in US]
    A1 = [jnp.where(strict, _dot_lo(Kk[u], Kt[u], _NT), 0.0) for u in US]
    A4 = [jnp.where(incl, _dot_lo(Rt[u], Bt[u], _NT), 0.0) for u in US]
    A3 = [jnp.where(incl, _dot_lo(Rt[u], Kt[u], _NT), 0.0) for u in US]
    X = [jnp.concatenate([Kk[u], _dot_lo(A1[u], vv[u])], axis=1) for u in US]
    zs = [[] for _ in US]
    for b in range(nb):
        rs = slice(b * RW_SUB, (b + 1) * RW_SUB)
        rhs = [X[u][rs] for u in US]
        if b:
            rhs = [rhs[u] - _dot_lo(Lm[u][rs, 0:b * RW_SUB], jnp.concatenate(zs[u], axis=0)) for u in US]
        for u, (cc, h) in enumerate(units):
            tbb = td_ref[cc * c + b * RW_SUB:cc * c + (b + 1) * RW_SUB, h * RW_SUB:(h + 1) * RW_SUB]
            zs[u].append(_dot_lo(tbb, rhs[u]))
    Z = [jnp.concatenate(zs[u], axis=0) if nb > 1 else zs[u][0] for u in US]
    A4Z = [_dot_lo(A4[u], Z[u]) for u in US]
    Rhat = [Rt[u] - A4Z[u][:, :RW_HD] for u in US]
    Yhat = [_dot_lo(A3[u], vv[u]) - A4Z[u][:, RW_HD:] for u in US]
    gC = [gg[(cc + 1) * c - 1:(cc + 1) * c, lsl[u]] for u, (cc, _) in enumerate(units)]
    MN = [_dot_lo(Bt[u] * gC[u], Z[u], _TN) for u in US]
    Mp = [jnp.where(hrow == hcol, gC[u], 0.0) - MN[u][:, :RW_HD] for u in US]
    Np = [_dot_lo(Kt[u] * gC[u], vv[u], _TN) - MN[u][:, RW_HD:] for u in US]
    H = [h_ref[h] for h in HS]
    ys = [None for _ in US]
    for cc in range(rows // c):
        for h in HS:
            u = cc * RW_HEADS + h
            ys[u] = _dot_lo(Rhat[u], H[h]) + Yhat[u]
        H = [_dot_hi(Mp[cc * RW_HEADS + h], H[h]) + Np[cc * RW_HEADS + h] for h in HS]
    for h in HS:
        h_ref[h] = H[h]
    for u in US:
        y = ys[u]
        mu = jnp.mean(y, axis=-1, keepdims=True)
        var = jnp.mean(jnp.square(y - mu), axis=-1, keepdims=True)
        yn = (y - mu) * lax.rsqrt(var + RW_LN_EPS) * lng_ref[:, lsl[u]] + lnb_ref[:, lsl[u]]
        yn = yn + jnp.sum(bonus_all[rsl[u], lsl[u]], axis=-1, keepdims=True) * vv[u]
        o_ref[rsl[u], lsl[u]] = (yn * g_ref[rsl[u], lsl[u]]).astype(o_ref.dtype)

    @pl.when(ci == pl.num_programs(1) - 1)
    def _():
        hout_ref[0] = h_ref[...]


def _rwkv7(x2, norm_g, w_rw, shift, S0, B, T, P):
    G = GROUP_W
    n = B * T
    r, lw, k, v, kap, bet, g, ld, shift_new = _rw_prep(x2, norm_g, w_rw, shift, B, T, P)
    c = min(RW_CHUNK, T)
    rows_m = _row_tile(T, c * RW_MAIN_CHUNKS)
    nc = T // rows_m
    tile = pl.BlockSpec((rows_m, G), lambda b, i: (b * nc + i, 0))
    rows_i = V7X_LANES * RW_SUB
    npad = -(-n // rows_i) * rows_i
    inv_spec = pl.BlockSpec((rows_i, RW_HEADS * RW_SUB), lambda i: (i, 0))
    inv_scratch = pltpu.VMEM((RW_SUB, RW_HEADS * RW_SUB, V7X_LANES), F32)
    td = pl.pallas_call(
        _rw_inv_body,
        grid=(npad // rows_i,),
        in_specs=[inv_spec],
        out_specs=inv_spec,
        out_shape=jax.ShapeDtypeStruct((npad, RW_HEADS * RW_SUB), F32),
        scratch_shapes=[inv_scratch, inv_scratch],
        compiler_params=_params(("parallel",), 32),
        name="rwkv_inv",
    )(jnp.pad(ld, ((0, npad - n), (0, 0))))[:n]
    h0 = jnp.swapaxes(S0, -1, -2)
    prow = lambda x: pl.BlockSpec((1, G), lambda b, i: (0, 0))
    st_spec = pl.BlockSpec((1, RW_HEADS, RW_HD, RW_HD), lambda b, i: (b, 0, 0, 0))
    out, hl = pl.pallas_call(
        functools.partial(_rw_main_body, c=c),
        grid=(B, nc),
        in_specs=[tile] * 7 + [pl.BlockSpec((rows_m, RW_HEADS * RW_SUB), lambda b, i: (b * nc + i, 0)), st_spec,
                               prow(0), prow(0), prow(0)],
        out_specs=[tile, st_spec],
        out_shape=[jax.ShapeDtypeStruct((n, G), BF16), jax.ShapeDtypeStruct(S0.shape, F32)],
        scratch_shapes=[pltpu.VMEM((RW_HEADS, RW_HD, RW_HD), F32)],
        compiler_params=_params(("parallel", "arbitrary"), 32),
        name="rwkv_main",
    )(r, lw, k, v, kap, bet, g, td, h0, P['rw_rk'].reshape(1, G), P['rw_ln_g'].reshape(1, G),
      P['rw_ln_b'].reshape(1, G))
    return out, shift_new.reshape(B, -1), jnp.swapaxes(hl, -1, -2)


def _even_mixer(x2, B, T, g, st, P):
    conv_buf, lru_h, k_past, v_past, lf_past = st
    G = GROUP_W
    z_rg, z_qkv, z_og, z_fl = _norm_matmul(x2, g, P['e_w_in'], (2 * G, 3 * G, G, V7X_LANES))
    rnn_out, conv_new, h_last = _lru(z_rg, conv_buf, lru_h, B, T, P)
    qb, kn, kb, v, vb, lf = _fox_prep(z_qkv, z_fl, B, T, P)
    past = k_past.shape[1]
    lf_all = lf.reshape(B, T, V7X_LANES)
    kb_all, vb_all = kb.reshape(B, T, G), vb.reshape(B, T, G)
    if past:
        lf_all = jnp.concatenate([jnp.pad(lf_past, ((0, 0), (0, 0), (0, V7X_LANES - FOX_HEADS))), lf_all], axis=1)
        kb_all = jnp.concatenate([k_past.reshape(B, past, G).astype(BF16), kb_all], axis=1)
        vb_all = jnp.concatenate([v_past.reshape(B, past, G).astype(BF16), vb_all], axis=1)
    tail = ((0, 0), (0, -(past + T) % FOX_K_ROWS), (0, 0))
    ka, vt = _fox_keys(jnp.pad(lf_all, tail), jnp.pad(kb_all, tail), jnp.pad(vb_all, tail))
    fox_out = _fox_attention(qb, ka, vt, z_og, B, T, past)
    heads = lambda t: t.reshape(B, T, FOX_HEADS, FOX_HD)
    return (rnn_out, fox_out, P['e_w_out']), (conv_new, h_last, heads(kn), heads(v), lf.reshape(B, T, V7X_LANES)[..., :FOX_HEADS])


def _odd_mixer(x2, B, T, g, st, lb, P):
    S_hg, shift, S_rw = st
    G = GROUP_W
    (z_hg,) = _norm_matmul(x2, g, P['o_w_in'][:, :4 * G], (4 * G,))
    hg_out, S_hg_new = _hgrn2(z_hg, lb, S_hg, B, T, P)
    rw_out, shift_new, S_rw_new = _rwkv7(x2, g, P['o_w_in'][:, 4 * G:], shift, S_rw, B, T, P)
    return (hg_out, rw_out, P['o_w_out']), (S_hg_new, shift_new, S_rw_new)


def _trunk(x, states, W):
    lru_conv, lru_h, fox_k, fox_v, fox_lf, hg_S, rw_shift, rw_S = states
    B, T, D = x.shape
    depth = W['norm_g'].shape[0]
    sm = jax.nn.softmax(W['hg_lb_logits'], axis=0)
    lower_bounds = jnp.cumsum(sm, axis=0) - sm[0]
    x2 = x.reshape(B * T, D)
    even_new, odd_new = [], []
    for layer in range(depth):
        g = W['norm_g'][layer]
        x2 = _ffn(x2, g[0], W['ffn_w_in'][layer][0], W['ffn_w_out'][layer][0])
        if layer % 2 == 0:
            e = layer // 2
            P = {n: W[n][e] for n in ('e_w_in', 'e_w_out', 'lru_conv_w', 'lru_conv_b', 'lru_wa', 'lru_ba', 'lru_wx',
                                      'lru_bx', 'lru_lambda', 'fox_q_gain', 'fox_k_gain', 'fox_f_bias')}
            mix, new = _even_mixer(x2, B, T, g[1], (lru_conv[e], lru_h[e], fox_k[e], fox_v[e], fox_lf[e]), P)
            even_new.append(new)
        else:
            o = layer // 2
            P = {n: W[n][o] for n in ('o_w_in', 'o_w_out', 'hg_norm_g', 'rw_mu', 'rw_w0', 'rw_w2', 'rw_a0', 'rw_a2',
                                      'rw_g2', 'rw_kk', 'rw_ka', 'rw_rk', 'rw_ln_g', 'rw_ln_b')}
            mix, new = _odd_mixer(x2, B, T, g[1], (hg_S[o], rw_shift[o], rw_S[o]), lower_bounds[layer], P)
            odd_new.append(new)
        x2 = _ffn(x2, g[2], W['ffn_w_in'][layer][1], W['ffn_w_out'][layer][1], mix)
    ev = [jnp.stack([n[j] for n in even_new]) for j in range(5)]
    od = [jnp.stack([n[j] for n in odd_new]) for j in range(3)]
    return x2.reshape(B, T, D), (ev[0], ev[1], ev[2], ev[3], ev[4], od[0], od[1], od[2])


def kernel(x_prompt, x_sample, state_lru_conv, state_lru_h, cache_fox_k, cache_fox_v, cache_fox_logf,
           state_hgrn_S, state_rwkv_shift, state_rwkv_S, norm_g, ffn_w_in, ffn_w_out, e_w_in, e_w_out,
           lru_conv_w, lru_conv_b, lru_wa, lru_ba, lru_wx, lru_bx, lru_lambda, fox_q_gain, fox_k_gain,
           fox_f_bias, o_w_in, o_w_out, hg_lb_logits, hg_norm_g, rw_mu, rw_w0, rw_w2, rw_a0, rw_a2, rw_g2,
           rw_kk, rw_ka, rw_rk, rw_ln_g, rw_ln_b):
    n_even, n_odd = e_w_in.shape[0], o_w_in.shape[0]
    W = dict(norm_g=norm_g, ffn_w_in=_ffn_w_in_tiles(ffn_w_in), ffn_w_out=ffn_w_out.astype(BF16),
             e_w_in=_pad_cols(e_w_in.astype(BF16)), e_w_out=e_w_out.astype(BF16),
             lru_conv_w=lru_conv_w, lru_conv_b=lru_conv_b, lru_wa=lru_wa, lru_ba=lru_ba, lru_wx=lru_wx,
             lru_bx=lru_bx, lru_lambda=lru_lambda, fox_q_gain=fox_q_gain, fox_k_gain=fox_k_gain,
             fox_f_bias=fox_f_bias, o_w_in=o_w_in.astype(BF16), o_w_out=o_w_out.astype(BF16),
             hg_lb_logits=hg_lb_logits, hg_norm_g=hg_norm_g, rw_mu=rw_mu, rw_w0=rw_w0, rw_w2=rw_w2, rw_a0=rw_a0,
             rw_a2=rw_a2, rw_g2=rw_g2, rw_kk=rw_kk, rw_ka=rw_ka, rw_rk=rw_rk, rw_ln_g=rw_ln_g, rw_ln_b=rw_ln_b)
    nb = x_prompt.shape[0]
    dt = x_prompt.dtype
    prompt_states = (jnp.zeros((n_even, nb, CONV_W - 1, GROUP_W), dt),
                     jnp.zeros((n_even, nb, GROUP_W), dt),
                     jnp.zeros((n_even, nb, 0, FOX_HEADS, FOX_HD), dt),
                     jnp.zeros((n_even, nb, 0, FOX_HEADS, FOX_HD), dt),
                     jnp.zeros((n_even, nb, 0, FOX_HEADS), dt),
                     jnp.zeros((n_odd, nb, HG_HEADS, GROUP_W // HG_HEADS, GROUP_W // HG_HEADS), dt),
                     jnp.zeros((n_odd, nb, rw_mu.shape[1]), dt),
                     jnp.zeros((n_odd, nb, RW_HEADS, RW_HD, RW_HD), dt))
    sample_states = (state_lru_conv, state_lru_h, cache_fox_k, cache_fox_v, cache_fox_logf,
                     state_hgrn_S, state_rwkv_shift, state_rwkv_S)
    y_prompt, p_new = _trunk(x_prompt, prompt_states, W)
    y_sample, s_new = _trunk(x_sample, sample_states, W)
    lru_conv_p, lru_h_p, fox_k_p, fox_v_p, fox_logf_p, hgrn_S_p, rwkv_shift_p, rwkv_S_p = p_new
    lru_conv_s, lru_h_s, fox_k_s, fox_v_s, fox_logf_s, hgrn_S_s, rwkv_shift_s, rwkv_S_s = s_new
    return (y_prompt, y_sample, lru_conv_p, lru_conv_s, lru_h_p, lru_h_s, fox_k_p, fox_k_s, fox_v_p, fox_v_s,
            fox_logf_p, fox_logf_s, hgrn_S_p, hgrn_S_s, rwkv_shift_p, rwkv_shift_s, rwkv_S_p, rwkv_S_s)
```
